```python
import math
import jax, jax.numpy as jnp
from jax import lax
import numpy as np

D_MODEL = 1024
BATCH = 8
SEQ = 8192
DEPTH = 4

N_EVEN = (DEPTH + 1) // 2
N_ODD = DEPTH // 2

MEM_LEN = 256
XA_HEADS = 4
XA_HEAD_DIM = D_MODEL // XA_HEADS

POOL_W = D_MODEL // 2
POOL_WINDOWS = (2, 4, 8, 16)
N_POOL_GROUPS = len(POOL_WINDOWS)
POOL_GROUP = POOL_W // N_POOL_GROUPS
CONV_W = D_MODEL // 2
CONV_K = 31

MLA_HEADS = 16
QK_NOPE = 64
QK_ROPE = 32
V_HEAD = 64
Q_LORA = 384
KV_LORA = 256
ROPE_THETA = 10000.0
Q_BLOCK = 128
MLA_SCALE = 1.0 / math.sqrt(QK_NOPE + QK_ROPE)

D_FF = 2816
FFN_CONV_K = 3

EPS = 1e-6
NEG = -1e30

kernel_name = "hybrid_pool_conv_mla_memxattn_convffn"


def rmsnorm(x, g):
    xf = x.astype(jnp.float32)
    y = xf * lax.rsqrt(jnp.mean(xf * xf, axis=-1, keepdims=True) + EPS)
    return (y * g.astype(jnp.float32)).astype(x.dtype)


def layernorm(x, g, b):
    xf = x.astype(jnp.float32)
    mu = jnp.mean(xf, axis=-1, keepdims=True)
    xc = xf - mu
    y = xc * lax.rsqrt(jnp.mean(xc * xc, axis=-1, keepdims=True) + EPS)
    return (y * g.astype(jnp.float32) + b.astype(jnp.float32)).astype(x.dtype)


def causal_dwconv(u, w):
    k = w.shape[0]
    return lax.conv_general_dilated(
        u, w[:, None, :], window_strides=(1,), padding=[(k - 1, 0)],
        dimension_numbers=("NWC", "WIO", "NWC"), feature_group_count=u.shape[-1])


def window_mean_minus_self(u, w):
    t = u.shape[1]
    uf = u.astype(jnp.float32)
    cs = jnp.cumsum(uf, axis=1)
    cs_lag = jnp.pad(cs, ((0, 0), (w, 0), (0, 0)))[:, :t]
    cnt = jnp.minimum(jnp.arange(t) + 1, w).astype(jnp.float32)
    return ((cs - cs_lag) / cnt[None, :, None] - uf).astype(u.dtype)


def rope_tables(positions):
    inv = 1.0 / (ROPE_THETA ** (jnp.arange(0, QK_ROPE, 2, dtype=jnp.float32) / QK_ROPE))
    ang = positions.astype(jnp.float32)[..., None] * inv
    return jnp.cos(ang), jnp.sin(ang)


def apply_rope(x, cos, sin):
    half = x.shape[-1] // 2
    c = cos.astype(x.dtype)
    s = sin.astype(x.dtype)
    x1, x2 = x[..., :half], x[..., half:]
    return jnp.concatenate([x1 * c - x2 * s, x1 * s + x2 * c], axis=-1)


def pool_conv_mixer(h, w_in, pool_w, pool_scale, dw_w, dw_b, ln_g, ln_b, w_out):
    b, t, _ = h.shape
    z = h @ w_in
    u, glu_a, glu_b = jnp.split(z, [POOL_W, POOL_W + CONV_W], axis=-1)
    ug = u.reshape(b, t, N_POOL_GROUPS, POOL_GROUP)
    pooled = jnp.stack([window_mean_minus_self(ug[:, :, i], w)
                        for i, w in enumerate(POOL_WINDOWS)], axis=2)
    ya = jnp.einsum("btgc,gcd->btgd", pooled, pool_w).reshape(b, t, POOL_W) * pool_scale
    gl = glu_a * jax.nn.sigmoid(glu_b)
    cv = causal_dwconv(gl, dw_w) + dw_b
    yb = jax.nn.silu(layernorm(cv, ln_g, ln_b))
    return jnp.concatenate([ya, yb], axis=-1) @ w_out


def mla_attention(h, cos, sin, w_dq_dkv, q_norm_g, w_uq, kv_norm_g, w_ukv, w_o):
    b, t, _ = h.shape
    c = h @ w_dq_dkv
    cq, ckv, k_pe = jnp.split(c, [Q_LORA, Q_LORA + KV_LORA], axis=-1)
    q = (rmsnorm(cq, q_norm_g) @ w_uq).reshape(b, t, MLA_HEADS, QK_NOPE + QK_ROPE)
    q_nope = q[..., :QK_NOPE]
    q_pe = apply_rope(q[..., QK_NOPE:], cos[:, :, None, :], sin[:, :, None, :])
    kv = (rmsnorm(ckv, kv_norm_g) @ w_ukv).reshape(b, t, MLA_HEADS, QK_NOPE + V_HEAD)
    k_nope, v = kv[..., :QK_NOPE], kv[..., QK_NOPE:]
    k_pe = apply_rope(k_pe, cos, sin)
    nb = t // Q_BLOCK
    kpos = jnp.arange(t)

    def block(args):
        qn, qp, i = args
        s = (jnp.einsum("bqhd,bkhd->bhqk", qn, k_nope)
             + jnp.einsum("bqhr,bkr->bhqk", qp, k_pe)).astype(jnp.float32) * MLA_SCALE
        qpos = i * Q_BLOCK + jnp.arange(Q_BLOCK)
        s = jnp.where(kpos[None, :] <= qpos[:, None], s, NEG)
        p = jax.nn.softmax(s, axis=-1).astype(v.dtype)
        return jnp.einsum("bhqk,bkhd->bqhd", p, v)

    qn_b = q_nope.reshape(b, nb, Q_BLOCK, MLA_HEADS, QK_NOPE).transpose(1, 0, 2, 3, 4)
    qp_b = q_pe.reshape(b, nb, Q_BLOCK, MLA_HEADS, QK_ROPE).transpose(1, 0, 2, 3, 4)
    o = lax.map(block, (qn_b, qp_b, jnp.arange(nb)))
    o = o.transpose(1, 0, 2, 3, 4).reshape(b, t, MLA_HEADS * V_HEAD)
    return o @ w_o


def memory_cross_attention(h, m, wq, wkv, wo):
    b, t, _ = h.shape
    q = (h @ wq).reshape(b, t, XA_HEADS, XA_HEAD_DIM)
    k, v = jnp.split(m @ wkv, 2, axis=-1)
    k = k.reshape(b, MEM_LEN, XA_HEADS, XA_HEAD_DIM)
    v = v.reshape(b, MEM_LEN, XA_HEADS, XA_HEAD_DIM)
    s = jnp.einsum("bthd,bmhd->bhtm", q, k).astype(jnp.float32) * (XA_HEAD_DIM ** -0.5)
    p = jax.nn.softmax(s, axis=-1).astype(v.dtype)
    o = jnp.einsum("bhtm,bmhd->bthd", p, v).reshape(b, t, D_MODEL)
    return o @ wo


def conv_ffn(h, w_up, conv_w, conv_b, w_down):
    a, g = jnp.split(h @ w_up, 2, axis=-1)
    g = causal_dwconv(g, conv_w) + conv_b
    return (jax.nn.silu(g) * a) @ w_down


def _fwd_setup_inputs(seed: int = 0) -> dict:
    key = jax.random.key(seed)
    ks = iter(jax.random.split(key, 40))
    f32 = jnp.float32

    def dense(shape, fan_in, scale=1.0):
        return jax.random.normal(next(ks), shape, f32) * (scale * fan_in ** -0.5)

    def gain(shape):
        return 1.0 + 0.02 * jax.random.normal(next(ks), shape, f32)

    def bias(shape):
        return 0.01 * jax.random.normal(next(ks), shape, f32)

    out_scale = 0.5
    x = jax.random.normal(next(ks), (BATCH, SEQ, D_MODEL), f32)
    mem = jax.random.normal(next(ks), (BATCH, MEM_LEN, D_MODEL), f32)
    offsets = jax.random.randint(next(ks), (BATCH, 1), 0, 4096, dtype=jnp.int32)
    positions = offsets + jnp.arange(SEQ, dtype=jnp.int32)[None, :]
    return {
        "x": x,
        "mem": mem,
        "positions": positions,
        "norm_mix_g": gain((DEPTH, D_MODEL)),
        "norm_xa_g": gain((DEPTH, D_MODEL)),
        "norm_mem_g": gain((DEPTH, D_MODEL)),
        "xa_wq": dense((DEPTH, D_MODEL, D_MODEL), D_MODEL),
        "xa_wkv": dense((DEPTH, D_MODEL, 2 * D_MODEL), D_MODEL),
        "xa_wo": dense((DEPTH, D_MODEL, D_MODEL), D_MODEL, out_scale),
        "norm_ffn_g": gain((DEPTH, D_MODEL)),
        "ffn_w_up": dense((DEPTH, D_MODEL, 2 * D_FF), D_MODEL),
        "ffn_conv_w": dense((DEPTH, FFN_CONV_K, D_FF), FFN_CONV_K),
        "ffn_conv_b": bias((DEPTH, D_FF)),
        "ffn_w_down": dense((DEPTH, D_FF, D_MODEL), D_FF, out_scale),
        "pc_w_in": dense((N_EVEN, D_MODEL, POOL_W + 2 * CONV_W), D_MODEL),
        "pool_w": dense((N_EVEN, N_POOL_GROUPS, POOL_GROUP, POOL_GROUP), POOL_GROUP),
        "pool_scale": gain((N_EVEN, POOL_W)),
        "conv_dw_w": dense((N_EVEN, CONV_K, CONV_W), CONV_K),
        "conv_dw_b": bias((N_EVEN, CONV_W)),
        "conv_ln_g": gain((N_EVEN, CONV_W)),
        "conv_ln_b": bias((N_EVEN, CONV_W)),
        "pc_w_out": dense((N_EVEN, POOL_W + CONV_W, D_MODEL), POOL_W + CONV_W, out_scale),
        "mla_w_dq_dkv": dense((N_ODD, D_MODEL, Q_LORA + KV_LORA + QK_ROPE), D_MODEL),
        "mla_q_norm_g": gain((N_ODD, Q_LORA)),
        "mla_w_uq": dense((N_ODD, Q_LORA, MLA_HEADS * (QK_NOPE + QK_ROPE)), Q_LORA),
        "mla_kv_norm_g": gain((N_ODD, KV_LORA)),
        "mla_w_ukv": dense((N_ODD, KV_LORA, MLA_HEADS * (QK_NOPE + V_HEAD)), KV_LORA),
        "mla_w_o": dense((N_ODD, MLA_HEADS * V_HEAD, D_MODEL), MLA_HEADS * V_HEAD, out_scale),
        "final_norm_g": gain((D_MODEL,)),
    }


def _fwd_reference(x, mem, positions, norm_mix_g, norm_xa_g, norm_mem_g, xa_wq, xa_wkv, xa_wo,
              norm_ffn_g, ffn_w_up, ffn_conv_w, ffn_conv_b, ffn_w_down,
              pc_w_in, pool_w, pool_scale, conv_dw_w, conv_dw_b, conv_ln_g, conv_ln_b, pc_w_out,
              mla_w_dq_dkv, mla_q_norm_g, mla_w_uq, mla_kv_norm_g, mla_w_ukv, mla_w_o,
              final_norm_g):
    cos, sin = rope_tables(positions)
    for l in range(DEPTH):
        h = rmsnorm(x, norm_mix_g[l])
        if l % 2 == 0:
            e = l // 2
            x = x + pool_conv_mixer(h, pc_w_in[e], pool_w[e], pool_scale[e], conv_dw_w[e],
                                    conv_dw_b[e], conv_ln_g[e], conv_ln_b[e], pc_w_out[e])
        else:
            o = l // 2
            x = x + mla_attention(h, cos, sin, mla_w_dq_dkv[o], mla_q_norm_g[o], mla_w_uq[o],
                                  mla_kv_norm_g[o], mla_w_ukv[o], mla_w_o[o])
        x = x + memory_cross_attention(rmsnorm(x, norm_xa_g[l]), rmsnorm(mem, norm_mem_g[l]),
                                       xa_wq[l], xa_wkv[l], xa_wo[l])
        x = x + conv_ffn(rmsnorm(x, norm_ffn_g[l]), ffn_w_up[l], ffn_conv_w[l], ffn_conv_b[l],
                         ffn_w_down[l])
    return rmsnorm(x, final_norm_g)


import jax as _jax
import jax.numpy as _jnp

TWIN_FORMAT = 'train_step'
FWD_PARAMS = ['x', 'mem', 'positions', 'norm_mix_g', 'norm_xa_g', 'norm_mem_g', 'xa_wq', 'xa_wkv', 'xa_wo', 'norm_ffn_g', 'ffn_w_up', 'ffn_conv_w', 'ffn_conv_b', 'ffn_w_down', 'pc_w_in', 'pool_w', 'pool_scale', 'conv_dw_w', 'conv_dw_b', 'conv_ln_g', 'conv_ln_b', 'pc_w_out', 'mla_w_dq_dkv', 'mla_q_norm_g', 'mla_w_uq', 'mla_kv_norm_g', 'mla_w_ukv', 'mla_w_o', 'final_norm_g']
TWIN_WEIGHTS = ['norm_mix_g', 'norm_xa_g', 'norm_mem_g', 'xa_wq', 'xa_wkv', 'xa_wo', 'norm_ffn_g', 'ffn_w_up', 'ffn_conv_w', 'ffn_conv_b', 'ffn_w_down', 'pc_w_in', 'pool_w', 'pool_scale', 'conv_dw_w', 'conv_dw_b', 'conv_ln_g', 'conv_ln_b', 'pc_w_out', 'mla_w_dq_dkv', 'mla_q_norm_g', 'mla_w_uq', 'mla_kv_norm_g', 'mla_w_ukv', 'mla_w_o', 'final_norm_g']
TWIN_DIFF_INPUT = 'x'
TWIN_INPUTS = ['x', 'mem', 'positions', 'norm_mix_g', 'norm_xa_g', 'norm_mem_g', 'xa_wq', 'xa_wkv', 'xa_wo', 'norm_ffn_g', 'ffn_w_up', 'ffn_conv_w', 'ffn_conv_b', 'ffn_w_down', 'pc_w_in', 'pool_w', 'pool_scale', 'conv_dw_w', 'conv_dw_b', 'conv_ln_g', 'conv_ln_b', 'pc_w_out', 'mla_w_dq_dkv', 'mla_q_norm_g', 'mla_w_uq', 'mla_kv_norm_g', 'mla_w_ukv', 'mla_w_o', 'final_norm_g', 'loss_target', 'm_norm_mix_g', 'm_norm_xa_g', 'm_norm_mem_g', 'm_xa_wq', 'm_xa_wkv', 'm_xa_wo', 'm_norm_ffn_g', 'm_ffn_w_up', 'm_ffn_conv_w', 'm_ffn_conv_b', 'm_ffn_w_down', 'm_pc_w_in', 'm_pool_w', 'm_pool_scale', 'm_conv_dw_w', 'm_conv_dw_b', 'm_conv_ln_g', 'm_conv_ln_b', 'm_pc_w_out', 'm_mla_w_dq_dkv', 'm_mla_q_norm_g', 'm_mla_w_uq', 'm_mla_kv_norm_g', 'm_mla_w_ukv', 'm_mla_w_o', 'm_final_norm_g', 'v_norm_mix_g', 'v_norm_xa_g', 'v_norm_mem_g', 'v_xa_wq', 'v_xa_wkv', 'v_xa_wo', 'v_norm_ffn_g', 'v_ffn_w_up', 'v_ffn_conv_w', 'v_ffn_conv_b', 'v_ffn_w_down', 'v_pc_w_in', 'v_pool_w', 'v_pool_scale', 'v_conv_dw_w', 'v_conv_dw_b', 'v_conv_ln_g', 'v_conv_ln_b', 'v_pc_w_out', 'v_mla_w_dq_dkv', 'v_mla_q_norm_g', 'v_mla_w_uq', 'v_mla_kv_norm_g', 'v_mla_w_ukv', 'v_mla_w_o', 'v_final_norm_g']
TWIN_OUTPUTS = ['loss', 'grad_x', 'grad_norm_mix_g', 'grad_norm_xa_g', 'grad_norm_mem_g', 'grad_xa_wq', 'grad_xa_wkv', 'grad_xa_wo', 'grad_norm_ffn_g', 'grad_ffn_w_up', 'grad_ffn_conv_w', 'grad_ffn_conv_b', 'grad_ffn_w_down', 'grad_pc_w_in', 'grad_pool_w', 'grad_pool_scale', 'grad_conv_dw_w', 'grad_conv_dw_b', 'grad_conv_ln_g', 'grad_conv_ln_b', 'grad_pc_w_out', 'grad_mla_w_dq_dkv', 'grad_mla_q_norm_g', 'grad_mla_w_uq', 'grad_mla_kv_norm_g', 'grad_mla_w_ukv', 'grad_mla_w_o', 'grad_final_norm_g', 'delta_norm_mix_g', 'delta_norm_xa_g', 'delta_norm_mem_g', 'delta_xa_wq', 'delta_xa_wkv', 'delta_xa_wo', 'delta_norm_ffn_g', 'delta_ffn_w_up', 'delta_ffn_conv_w', 'delta_ffn_conv_b', 'delta_ffn_w_down', 'delta_pc_w_in', 'delta_pool_w', 'delta_pool_scale', 'delta_conv_dw_w', 'delta_conv_dw_b', 'delta_conv_ln_g', 'delta_conv_ln_b', 'delta_pc_w_out', 'delta_mla_w_dq_dkv', 'delta_mla_q_norm_g', 'delta_mla_w_uq', 'delta_mla_kv_norm_g', 'delta_mla_w_ukv', 'delta_mla_w_o', 'delta_final_norm_g', 'new_m_norm_mix_g', 'new_m_norm_xa_g', 'new_m_norm_mem_g', 'new_m_xa_wq', 'new_m_xa_wkv', 'new_m_xa_wo', 'new_m_norm_ffn_g', 'new_m_ffn_w_up', 'new_m_ffn_conv_w', 'new_m_ffn_conv_b', 'new_m_ffn_w_down', 'new_m_pc_w_in', 'new_m_pool_w', 'new_m_pool_scale', 'new_m_conv_dw_w', 'new_m_conv_dw_b', 'new_m_conv_ln_g', 'new_m_conv_ln_b', 'new_m_pc_w_out', 'new_m_mla_w_dq_dkv', 'new_m_mla_q_norm_g', 'new_m_mla_w_uq', 'new_m_mla_kv_norm_g', 'new_m_mla_w_ukv', 'new_m_mla_w_o', 'new_m_final_norm_g', 'new_v_norm_mix_g', 'new_v_norm_xa_g', 'new_v_norm_mem_g', 'new_v_xa_wq', 'new_v_xa_wkv', 'new_v_xa_wo', 'new_v_norm_ffn_g', 'new_v_ffn_w_up', 'new_v_ffn_conv_w', 'new_v_ffn_conv_b', 'new_v_ffn_w_down', 'new_v_pc_w_in', 'new_v_pool_w', 'new_v_pool_scale', 'new_v_conv_dw_w', 'new_v_conv_dw_b', 'new_v_conv_ln_g', 'new_v_conv_ln_b', 'new_v_pc_w_out', 'new_v_mla_w_dq_dkv', 'new_v_mla_q_norm_g', 'new_v_mla_w_uq', 'new_v_mla_kv_norm_g', 'new_v_mla_w_ukv', 'new_v_mla_w_o', 'new_v_final_norm_g']
TWIN_LEAF_KINDS = {'loss': 'loss', 'grad_x': 'grad_x', 'grad_norm_mix_g': 'grad_w', 'grad_norm_xa_g': 'grad_w', 'grad_norm_mem_g': 'grad_w', 'grad_xa_wq': 'grad_w', 'grad_xa_wkv': 'grad_w', 'grad_xa_wo': 'grad_w', 'grad_norm_ffn_g': 'grad_w', 'grad_ffn_w_up': 'grad_w', 'grad_ffn_conv_w': 'grad_w', 'grad_ffn_conv_b': 'grad_w', 'grad_ffn_w_down': 'grad_w', 'grad_pc_w_in': 'grad_w', 'grad_pool_w': 'grad_w', 'grad_pool_scale': 'grad_w', 'grad_conv_dw_w': 'grad_w', 'grad_conv_dw_b': 'grad_w', 'grad_conv_ln_g': 'grad_w', 'grad_conv_ln_b': 'grad_w', 'grad_pc_w_out': 'grad_w', 'grad_mla_w_dq_dkv': 'grad_w', 'grad_mla_q_norm_g': 'grad_w', 'grad_mla_w_uq': 'grad_w', 'grad_mla_kv_norm_g': 'grad_w', 'grad_mla_w_ukv': 'grad_w', 'grad_mla_w_o': 'grad_w', 'grad_final_norm_g': 'grad_w', 'delta_norm_mix_g': 'delta_w', 'delta_norm_xa_g': 'delta_w', 'delta_norm_mem_g': 'delta_w', 'delta_xa_wq': 'delta_w', 'delta_xa_wkv': 'delta_w', 'delta_xa_wo': 'delta_w', 'delta_norm_ffn_g': 'delta_w', 'delta_ffn_w_up': 'delta_w', 'delta_ffn_conv_w': 'delta_w', 'delta_ffn_conv_b': 'delta_w', 'delta_ffn_w_down': 'delta_w', 'delta_pc_w_in': 'delta_w', 'delta_pool_w': 'delta_w', 'delta_pool_scale': 'delta_w', 'delta_conv_dw_w': 'delta_w', 'delta_conv_dw_b': 'delta_w', 'delta_conv_ln_g': 'delta_w', 'delta_conv_ln_b': 'delta_w', 'delta_pc_w_out': 'delta_w', 'delta_mla_w_dq_dkv': 'delta_w', 'delta_mla_q_norm_g': 'delta_w', 'delta_mla_w_uq': 'delta_w', 'delta_mla_kv_norm_g': 'delta_w', 'delta_mla_w_ukv': 'delta_w', 'delta_mla_w_o': 'delta_w', 'delta_final_norm_g': 'delta_w', 'new_m_norm_mix_g': 'new_m', 'new_m_norm_xa_g': 'new_m', 'new_m_norm_mem_g': 'new_m', 'new_m_xa_wq': 'new_m', 'new_m_xa_wkv': 'new_m', 'new_m_xa_wo': 'new_m', 'new_m_norm_ffn_g': 'new_m', 'new_m_ffn_w_up': 'new_m', 'new_m_ffn_conv_w': 'new_m', 'new_m_ffn_conv_b': 'new_m', 'new_m_ffn_w_down': 'new_m', 'new_m_pc_w_in': 'new_m', 'new_m_pool_w': 'new_m', 'new_m_pool_scale': 'new_m', 'new_m_conv_dw_w': 'new_m', 'new_m_conv_dw_b': 'new_m', 'new_m_conv_ln_g': 'new_m', 'new_m_conv_ln_b': 'new_m', 'new_m_pc_w_out': 'new_m', 'new_m_mla_w_dq_dkv': 'new_m', 'new_m_mla_q_norm_g': 'new_m', 'new_m_mla_w_uq': 'new_m', 'new_m_mla_kv_norm_g': 'new_m', 'new_m_mla_w_ukv': 'new_m', 'new_m_mla_w_o': 'new_m', 'new_m_final_norm_g': 'new_m', 'new_v_norm_mix_g': 'new_v', 'new_v_norm_xa_g': 'new_v', 'new_v_norm_mem_g': 'new_v', 'new_v_xa_wq': 'new_v', 'new_v_xa_wkv': 'new_v', 'new_v_xa_wo': 'new_v', 'new_v_norm_ffn_g': 'new_v', 'new_v_ffn_w_up': 'new_v', 'new_v_ffn_conv_w': 'new_v', 'new_v_ffn_conv_b': 'new_v', 'new_v_ffn_w_down': 'new_v', 'new_v_pc_w_in': 'new_v', 'new_v_pool_w': 'new_v', 'new_v_pool_scale': 'new_v', 'new_v_conv_dw_w': 'new_v', 'new_v_conv_dw_b': 'new_v', 'new_v_conv_ln_g': 'new_v', 'new_v_conv_ln_b': 'new_v', 'new_v_pc_w_out': 'new_v', 'new_v_mla_w_dq_dkv': 'new_v', 'new_v_mla_q_norm_g': 'new_v', 'new_v_mla_w_uq': 'new_v', 'new_v_mla_kv_norm_g': 'new_v', 'new_v_mla_w_ukv': 'new_v', 'new_v_mla_w_o': 'new_v', 'new_v_final_norm_g': 'new_v'}


def _forward(args):
    return _fwd_reference(*[args[k] for k in FWD_PARAMS])


def _output_shape():
    out = _jax.eval_shape(lambda: _forward(_fwd_setup_inputs(0)))
    return out.shape, out.dtype

N_MICROBATCH = 1
ADAM_LR = 0.001
ADAM_B1 = 0.9
ADAM_B2 = 0.999
ADAM_EPS = 1e-08
ADAM_WD = 0.01
ADAM_STEP = 10
PER_EXAMPLE_BATCH_AXIS = {'x': 0, 'mem': 0, 'positions': 0, 'loss_target': 0}
SHARED_INPUTS = []
_WEIGHT_DTYPES = {'norm_mix_g': _jnp.float32, 'norm_xa_g': _jnp.float32, 'norm_mem_g': _jnp.float32, 'xa_wq': _jnp.float32, 'xa_wkv': _jnp.float32, 'xa_wo': _jnp.float32, 'norm_ffn_g': _jnp.float32, 'ffn_w_up': _jnp.float32, 'ffn_conv_w': _jnp.float32, 'ffn_conv_b': _jnp.float32, 'ffn_w_down': _jnp.float32, 'pc_w_in': _jnp.float32, 'pool_w': _jnp.float32, 'pool_scale': _jnp.float32, 'conv_dw_w': _jnp.float32, 'conv_dw_b': _jnp.float32, 'conv_ln_g': _jnp.float32, 'conv_ln_b': _jnp.float32, 'pc_w_out': _jnp.float32, 'mla_w_dq_dkv': _jnp.float32, 'mla_q_norm_g': _jnp.float32, 'mla_w_uq': _jnp.float32, 'mla_kv_norm_g': _jnp.float32, 'mla_w_ukv': _jnp.float32, 'mla_w_o': _jnp.float32, 'final_norm_g': _jnp.float32}
MOMENT_SCALE = {'norm_mix_g': 7.199030e-02, 'norm_xa_g': 1.301351e-02, 'norm_mem_g': 2.003485e-02, 'xa_wq': 1.243562e-02, 'xa_wkv': 1.272807e-02, 'xa_wo': 2.575790e-02, 'norm_ffn_g': 9.932029e-02, 'ffn_w_up': 4.178580e-02, 'ffn_conv_w': 4.195424e-02, 'ffn_conv_b': 4.139530e-02, 'ffn_w_down': 1.361983e-01, 'pc_w_in': 7.797703e-02, 'pool_w': 1.077345e-01, 'pool_scale': 1.153455e-01, 'conv_dw_w': 7.720885e-02, 'conv_dw_b': 1.925799e-01, 'conv_ln_g': 9.764741e-02, 'conv_ln_b': 8.389693e-02, 'pc_w_out': 1.848595e-01, 'mla_w_dq_dkv': 4.829470e-02, 'mla_q_norm_g': 3.284038e-02, 'mla_w_uq': 1.661019e-02, 'mla_kv_norm_g': 6.599293e-02, 'mla_w_ukv': 2.273173e-02, 'mla_w_o': 5.493446e-02, 'final_norm_g': 6.392790e+01}


def _to_microbatches(a, axis):
    t = _jnp.moveaxis(a, axis, 0)
    t = t.reshape((N_MICROBATCH, t.shape[0] // N_MICROBATCH) + t.shape[1:])
    return _jnp.moveaxis(t, 1, axis + 1)


def setup_inputs(seed: int = 0) -> dict:
    inp = _fwd_setup_inputs(seed)
    key = _jax.random.fold_in(_jax.random.key(seed), 7919)
    shape, _ = _output_shape()
    out = dict(inp)
    out["loss_target"] = _jax.random.normal(_jax.random.fold_in(key, 0), shape, _jnp.float32)
    for i, name in enumerate(TWIN_WEIGHTS):
        w = inp[name].astype(_jnp.float32)
        if MOMENT_SCALE is None:
            s = _jnp.sqrt(_jnp.mean(_jnp.square(w)) + 1e-30)
        else:
            s = MOMENT_SCALE[name]
        km, kv = _jax.random.split(_jax.random.fold_in(key, i + 1))
        out[name] = w
        out["m_" + name] = s * _jax.random.normal(km, w.shape, _jnp.float32)
        out["v_" + name] = (s * s) * _jax.random.uniform(kv, w.shape, _jnp.float32, 0.5, 1.5)
    if N_MICROBATCH > 1:
        for name, axis in PER_EXAMPLE_BATCH_AXIS.items():
            out[name] = _to_microbatches(out[name], axis)
    return {'x': out['x'], 'mem': out['mem'], 'positions': out['positions'], 'norm_mix_g': out['norm_mix_g'], 'norm_xa_g': out['norm_xa_g'], 'norm_mem_g': out['norm_mem_g'], 'xa_wq': out['xa_wq'], 'xa_wkv': out['xa_wkv'], 'xa_wo': out['xa_wo'], 'norm_ffn_g': out['norm_ffn_g'], 'ffn_w_up': out['ffn_w_up'], 'ffn_conv_w': out['ffn_conv_w'], 'ffn_conv_b': out['ffn_conv_b'], 'ffn_w_down': out['ffn_w_down'], 'pc_w_in': out['pc_w_in'], 'pool_w': out['pool_w'], 'pool_scale': out['pool_scale'], 'conv_dw_w': out['conv_dw_w'], 'conv_dw_b': out['conv_dw_b'], 'conv_ln_g': out['conv_ln_g'], 'conv_ln_b': out['conv_ln_b'], 'pc_w_out': out['pc_w_out'], 'mla_w_dq_dkv': out['mla_w_dq_dkv'], 'mla_q_norm_g': out['mla_q_norm_g'], 'mla_w_uq': out['mla_w_uq'], 'mla_kv_norm_g': out['mla_kv_norm_g'], 'mla_w_ukv': out['mla_w_ukv'], 'mla_w_o': out['mla_w_o'], 'final_norm_g': out['final_norm_g'], 'loss_target': out['loss_target'], 'm_norm_mix_g': out['m_norm_mix_g'], 'm_norm_xa_g': out['m_norm_xa_g'], 'm_norm_mem_g': out['m_norm_mem_g'], 'm_xa_wq': out['m_xa_wq'], 'm_xa_wkv': out['m_xa_wkv'], 'm_xa_wo': out['m_xa_wo'], 'm_norm_ffn_g': out['m_norm_ffn_g'], 'm_ffn_w_up': out['m_ffn_w_up'], 'm_ffn_conv_w': out['m_ffn_conv_w'], 'm_ffn_conv_b': out['m_ffn_conv_b'], 'm_ffn_w_down': out['m_ffn_w_down'], 'm_pc_w_in': out['m_pc_w_in'], 'm_pool_w': out['m_pool_w'], 'm_pool_scale': out['m_pool_scale'], 'm_conv_dw_w': out['m_conv_dw_w'], 'm_conv_dw_b': out['m_conv_dw_b'], 'm_conv_ln_g': out['m_conv_ln_g'], 'm_conv_ln_b': out['m_conv_ln_b'], 'm_pc_w_out': out['m_pc_w_out'], 'm_mla_w_dq_dkv': out['m_mla_w_dq_dkv'], 'm_mla_q_norm_g': out['m_mla_q_norm_g'], 'm_mla_w_uq': out['m_mla_w_uq'], 'm_mla_kv_norm_g': out['m_mla_kv_norm_g'], 'm_mla_w_ukv': out['m_mla_w_ukv'], 'm_mla_w_o': out['m_mla_w_o'], 'm_final_norm_g': out['m_final_norm_g'], 'v_norm_mix_g': out['v_norm_mix_g'], 'v_norm_xa_g': out['v_norm_xa_g'], 'v_norm_mem_g': out['v_norm_mem_g'], 'v_xa_wq': out['v_xa_wq'], 'v_xa_wkv': out['v_xa_wkv'], 'v_xa_wo': out['v_xa_wo'], 'v_norm_ffn_g': out['v_norm_ffn_g'], 'v_ffn_w_up': out['v_ffn_w_up'], 'v_ffn_conv_w': out['v_ffn_conv_w'], 'v_ffn_conv_b': out['v_ffn_conv_b'], 'v_ffn_w_down': out['v_ffn_w_down'], 'v_pc_w_in': out['v_pc_w_in'], 'v_pool_w': out['v_pool_w'], 'v_pool_scale': out['v_pool_scale'], 'v_conv_dw_w': out['v_conv_dw_w'], 'v_conv_dw_b': out['v_conv_dw_b'], 'v_conv_ln_g': out['v_conv_ln_g'], 'v_conv_ln_b': out['v_conv_ln_b'], 'v_pc_w_out': out['v_pc_w_out'], 'v_mla_w_dq_dkv': out['v_mla_w_dq_dkv'], 'v_mla_q_norm_g': out['v_mla_q_norm_g'], 'v_mla_w_uq': out['v_mla_w_uq'], 'v_mla_kv_norm_g': out['v_mla_kv_norm_g'], 'v_mla_w_ukv': out['v_mla_w_ukv'], 'v_mla_w_o': out['v_mla_w_o'], 'v_final_norm_g': out['v_final_norm_g']}


def _loss(weights, diff, rest, loss_target):
    with _jax.named_scope("forward"):
        args = {**rest, TWIN_DIFF_INPUT: diff, **{k: w.astype(_WEIGHT_DTYPES[k]) for k, w in weights.items()}}
        y = _forward(args)
    with _jax.named_scope("loss_head"):
        err = _jnp.square(y.astype(_jnp.float32) - loss_target)
        return 0.5 * _jnp.sum(_jnp.mean(err, axis=-1)) if err.ndim else 0.5 * err


def _adamw(w, g, m, v):
    m = ADAM_B1 * m + (1.0 - ADAM_B1) * g
    v = ADAM_B2 * v + (1.0 - ADAM_B2) * _jnp.square(g)
    m_hat = m / (1.0 - ADAM_B1 ** ADAM_STEP)
    v_hat = v / (1.0 - ADAM_B2 ** ADAM_STEP)
    delta = -ADAM_LR * (m_hat / (_jnp.sqrt(v_hat) + ADAM_EPS) + ADAM_WD * w)
    return delta, m, v


def reference(x, mem, positions, norm_mix_g, norm_xa_g, norm_mem_g, xa_wq, xa_wkv, xa_wo, norm_ffn_g, ffn_w_up, ffn_conv_w, ffn_conv_b, ffn_w_down, pc_w_in, pool_w, pool_scale, conv_dw_w, conv_dw_b, conv_ln_g, conv_ln_b, pc_w_out, mla_w_dq_dkv, mla_q_norm_g, mla_w_uq, mla_kv_norm_g, mla_w_ukv, mla_w_o, final_norm_g, loss_target, m_norm_mix_g, m_norm_xa_g, m_norm_mem_g, m_xa_wq, m_xa_wkv, m_xa_wo, m_norm_ffn_g, m_ffn_w_up, m_ffn_conv_w, m_ffn_conv_b, m_ffn_w_down, m_pc_w_in, m_pool_w, m_pool_scale, m_conv_dw_w, m_conv_dw_b, m_conv_ln_g, m_conv_ln_b, m_pc_w_out, m_mla_w_dq_dkv, m_mla_q_norm_g, m_mla_w_uq, m_mla_kv_norm_g, m_mla_w_ukv, m_mla_w_o, m_final_norm_g, v_norm_mix_g, v_norm_xa_g, v_norm_mem_g, v_xa_wq, v_xa_wkv, v_xa_wo, v_norm_ffn_g, v_ffn_w_up, v_ffn_conv_w, v_ffn_conv_b, v_ffn_w_down, v_pc_w_in, v_pool_w, v_pool_scale, v_conv_dw_w, v_conv_dw_b, v_conv_ln_g, v_conv_ln_b, v_pc_w_out, v_mla_w_dq_dkv, v_mla_q_norm_g, v_mla_w_uq, v_mla_kv_norm_g, v_mla_w_ukv, v_mla_w_o, v_final_norm_g):
    given = dict(x=x, mem=mem, positions=positions, norm_mix_g=norm_mix_g, norm_xa_g=norm_xa_g, norm_mem_g=norm_mem_g, xa_wq=xa_wq, xa_wkv=xa_wkv, xa_wo=xa_wo, norm_ffn_g=norm_ffn_g, ffn_w_up=ffn_w_up, ffn_conv_w=ffn_conv_w, ffn_conv_b=ffn_conv_b, ffn_w_down=ffn_w_down, pc_w_in=pc_w_in, pool_w=pool_w, pool_scale=pool_scale, conv_dw_w=conv_dw_w, conv_dw_b=conv_dw_b, conv_ln_g=conv_ln_g, conv_ln_b=conv_ln_b, pc_w_out=pc_w_out, mla_w_dq_dkv=mla_w_dq_dkv, mla_q_norm_g=mla_q_norm_g, mla_w_uq=mla_w_uq, mla_kv_norm_g=mla_kv_norm_g, mla_w_ukv=mla_w_ukv, mla_w_o=mla_w_o, final_norm_g=final_norm_g, loss_target=loss_target, m_norm_mix_g=m_norm_mix_g, m_norm_xa_g=m_norm_xa_g, m_norm_mem_g=m_norm_mem_g, m_xa_wq=m_xa_wq, m_xa_wkv=m_xa_wkv, m_xa_wo=m_xa_wo, m_norm_ffn_g=m_norm_ffn_g, m_ffn_w_up=m_ffn_w_up, m_ffn_conv_w=m_ffn_conv_w, m_ffn_conv_b=m_ffn_conv_b, m_ffn_w_down=m_ffn_w_down, m_pc_w_in=m_pc_w_in, m_pool_w=m_pool_w, m_pool_scale=m_pool_scale, m_conv_dw_w=m_conv_dw_w, m_conv_dw_b=m_conv_dw_b, m_conv_ln_g=m_conv_ln_g, m_conv_ln_b=m_conv_ln_b, m_pc_w_out=m_pc_w_out, m_mla_w_dq_dkv=m_mla_w_dq_dkv, m_mla_q_norm_g=m_mla_q_norm_g, m_mla_w_uq=m_mla_w_uq, m_mla_kv_norm_g=m_mla_kv_norm_g, m_mla_w_ukv=m_mla_w_ukv, m_mla_w_o=m_mla_w_o, m_final_norm_g=m_final_norm_g, v_norm_mix_g=v_norm_mix_g, v_norm_xa_g=v_norm_xa_g, v_norm_mem_g=v_norm_mem_g, v_xa_wq=v_xa_wq, v_xa_wkv=v_xa_wkv, v_xa_wo=v_xa_wo, v_norm_ffn_g=v_norm_ffn_g, v_ffn_w_up=v_ffn_w_up, v_ffn_conv_w=v_ffn_conv_w, v_ffn_conv_b=v_ffn_conv_b, v_ffn_w_down=v_ffn_w_down, v_pc_w_in=v_pc_w_in, v_pool_w=v_pool_w, v_pool_scale=v_pool_scale, v_conv_dw_w=v_conv_dw_w, v_conv_dw_b=v_conv_dw_b, v_conv_ln_g=v_conv_ln_g, v_conv_ln_b=v_conv_ln_b, v_pc_w_out=v_pc_w_out, v_mla_w_dq_dkv=v_mla_w_dq_dkv, v_mla_q_norm_g=v_mla_q_norm_g, v_mla_w_uq=v_mla_w_uq, v_mla_kv_norm_g=v_mla_kv_norm_g, v_mla_w_ukv=v_mla_w_ukv, v_mla_w_o=v_mla_w_o, v_final_norm_g=v_final_norm_g)
    weights = {n: given[n] for n in TWIN_WEIGHTS}
    shared = {n: given[n] for n in SHARED_INPUTS}
    per_example = {n: given[n] for n in ['x', 'mem', 'positions']}
    grad_fn = _jax.value_and_grad(_loss, argnums=(0, 1))

    def one_microbatch(ex, loss_target):
        ex = dict(ex)
        diff = ex.pop(TWIN_DIFF_INPUT)
        return grad_fn(weights, diff, {**shared, **ex}, loss_target)

    if N_MICROBATCH == 1:
        loss, (grad_w, grad_x) = one_microbatch(per_example, given["loss_target"])
    else:
        def body(carry, xs):
            loss_sum, grad_sum = carry
            l_k, (gw_k, gx_k) = one_microbatch(xs[0], xs[1])
            with _jax.named_scope("update"):
                return (loss_sum + l_k, _jax.tree.map(_jnp.add, grad_sum, gw_k)), gx_k

        init = (_jnp.zeros((), _jnp.float32), _jax.tree.map(_jnp.zeros_like, weights))
        (loss, grad_w), grad_x = _jax.lax.scan(body, init, (per_example, given["loss_target"]))
    with _jax.named_scope("update"):
        delta_w, new_m, new_v = {}, {}, {}
        for n in TWIN_WEIGHTS:
            delta_w[n], new_m[n], new_v[n] = _adamw(weights[n], grad_w[n], given["m_" + n], given["v_" + n])
    return (loss, grad_x, *[grad_w[n] for n in TWIN_WEIGHTS], *[delta_w[n] for n in TWIN_WEIGHTS],
            *[new_m[n] for n in TWIN_WEIGHTS], *[new_v[n] for n in TWIN_WEIGHTS])
```

```python
import functools
import math

import numpy as np
import jax
import jax.numpy as jnp
from jax import lax
from jax.experimental import pallas as pl
from jax.experimental.pallas import tpu as pltpu

F32 = jnp.float32
MXU_DT = jnp.bfloat16

D_MODEL = 1024
DEPTH = 4
MEM_LEN = 256
XA_HEADS = 4
XA_HEAD_DIM = 256
POOL_W = 512
POOL_WINDOWS = (2, 4, 8, 16)
POOL_GROUP = 128
CONV_W = 512
CONV_K = 31
MLA_HEADS = 16
QK_NOPE = 64
QK_ROPE = 32
V_HEAD = 64
Q_LORA = 384
KV_LORA = 256
ROPE_THETA = 10000.0
MLA_SCALE = 1.0 / math.sqrt(QK_NOPE + QK_ROPE)
D_FF = 2816
FFN_CONV_K = 3
EPS = 1e-6
NEG = -1e30
ADAM_LR = 0.001
ADAM_B1 = 0.9
ADAM_B2 = 0.999
ADAM_EPS = 1e-08
ADAM_WD = 0.01
ADAM_STEP = 10

HEAD_PAD = 128
C_PAD = 768
KPE_LANE = 64

VMEM_LIMIT = 52 * 1024 * 1024
BLOCK_BYTES = 6 * 1024 * 1024
LANE = 128
SUBLANE = 8

TM = 1024
TN = 1408
TK = 1024
TT = 512
TW = 256
TA = 512
MIX_HALO = 32
FFN_HALO = 8

NN = (((1,), (0,)), ((), ()))
NT = (((1,), (1,)), ((), ()))
TN_DIMS = (((0,), (0,)), ((), ()))
MESH_ID = pl.DeviceIdType.MESH


def _cparams(*sem):
    return pltpu.CompilerParams(dimension_semantics=sem, vmem_limit_bytes=VMEM_LIMIT)


def _tile(n, pref, limit=None):
    cap = pref if limit is None else min(pref, limit)
    if n <= cap:
        return n
    t = (cap // LANE) * LANE
    while t >= LANE:
        if n % t == 0:
            return t
        t -= LANE
    return n


def _rows(t, pref):
    return t if t <= pref else pref


def _sigmoid(x):
    return 1.0 / (1.0 + jnp.exp(-x))


def _matmul(a, b, mode, out_dtype, name, layer=None, res=None):
    if layer is None:
        b2 = b.shape
    else:
        b2 = b.shape[1:]
    if mode == "tn":
        k, m = a.shape
        k2, n = b2
    elif mode == "nn":
        m, k = a.shape
        k2, n = b2
    else:
        m, k = a.shape
        n, k2 = b2
    assert k == k2, (a.shape, b.shape, mode)
    isz_a = jnp.dtype(a.dtype).itemsize
    isz_b = jnp.dtype(b.dtype).itemsize
    if mode == "tn":
        tk = _tile(k, TK)
        tm = _tile(m, TN, BLOCK_BYTES // (tk * isz_a))
        tn = _tile(n, TN, BLOCK_BYTES // (tk * isz_b))
    else:
        tk = k
        tn = _tile(n, TN, BLOCK_BYTES // (tk * isz_b))
        tm = _tile(m, TM, min(BLOCK_BYTES // (tk * isz_a), BLOCK_BYTES // (tn * 4)))
    nk = k // tk
    grid = (m // tm, n // tn, nk)
    if mode == "nn":
        a_spec = pl.BlockSpec((tm, tk), lambda i, j, kk: (i, kk))
        b_blk, b_map, dn = (tk, tn), (lambda i, j, kk: (kk, j)), NN
    elif mode == "nt":
        a_spec = pl.BlockSpec((tm, tk), lambda i, j, kk: (i, kk))
        b_blk, b_map, dn = (tn, tk), (lambda i, j, kk: (j, kk)), NT
    else:
        a_spec = pl.BlockSpec((tk, tm), lambda i, j, kk: (kk, i))
        b_blk, b_map, dn = (tk, tn), (lambda i, j, kk: (kk, j)), TN_DIMS
    if layer is None:
        b_spec = pl.BlockSpec(b_blk, b_map)
    else:
        b_spec = pl.BlockSpec((None,) + b_blk, lambda i, j, kk: (layer,) + b_map(i, j, kk))
    o_spec = pl.BlockSpec((tm, tn), lambda i, j, kk: (i, j))
    in_specs = [a_spec, b_spec]
    args = [a, b]
    if res is not None:
        in_specs.append(pl.BlockSpec((tm, tn), lambda i, j, kk: (i, j)))
        args.append(res)
    has_res = res is not None

    def body(*refs):
        a_ref, b_ref = refs[0], refs[1]
        r_ref = refs[2] if has_res else None
        o_ref = refs[3] if has_res else refs[2]
        p = lax.dot_general(a_ref[...].astype(MXU_DT), b_ref[...].astype(MXU_DT), dn, preferred_element_type=F32)
        if nk == 1:
            if has_res:
                p = r_ref[...] + p
            o_ref[...] = p.astype(o_ref.dtype)
        else:
            acc_ref = refs[-1]
            kk = pl.program_id(2)

            @pl.when(kk == 0)
            def _():
                acc_ref[...] = jnp.zeros_like(acc_ref)

            acc_ref[...] += p

            @pl.when(kk == nk - 1)
            def _():
                r = acc_ref[...]
                if has_res:
                    r = r_ref[...] + r
                o_ref[...] = r.astype(o_ref.dtype)

    scratch = [pltpu.VMEM((tm, tn), F32)] if nk > 1 else []
    return pl.pallas_call(
        body,
        grid=grid,
        in_specs=in_specs,
        out_specs=o_spec,
        out_shape=jax.ShapeDtypeStruct((m, n), out_dtype),
        scratch_shapes=scratch,
        name=name,
        compiler_params=_cparams("parallel", "parallel", "arbitrary"),
    )(*args)


def _rms_fwd(x, g, name):
    t, d = x.shape
    tt = _rows(t, TT)

    def body(x_ref, g_ref, o_ref):
        xf = x_ref[...]
        r = lax.rsqrt(jnp.mean(xf * xf, axis=-1, keepdims=True) + EPS)
        o_ref[...] = ((xf * r) * g_ref[...]).astype(o_ref.dtype)

    return pl.pallas_call(
        body,
        grid=(t // tt,),
        in_specs=[pl.BlockSpec((tt, d), lambda i: (i, 0)), pl.BlockSpec((1, d), lambda i: (0, 0))],
        out_specs=pl.BlockSpec((tt, d), lambda i: (i, 0)),
        out_shape=jax.ShapeDtypeStruct((t, d), MXU_DT),
        name=name,
        compiler_params=_cparams("parallel"),
    )(x, g.reshape(1, d))


def _rms_bwd(dh, x, g, dx_in, name):
    t, d = x.shape
    tt = _rows(t, TT)

    def body(dh_ref, x_ref, g_ref, dxi_ref, dx_ref, dg_ref):
        @pl.when(pl.program_id(0) == 0)
        def _():
            dg_ref[...] = jnp.zeros_like(dg_ref)

        xf = x_ref[...]
        dh_v = dh_ref[...]
        r = lax.rsqrt(jnp.mean(xf * xf, axis=-1, keepdims=True) + EPS)
        xh = xf * r
        gy = dh_v * g_ref[...]
        dx = r * (gy - xh * jnp.mean(gy * xh, axis=-1, keepdims=True))
        dx_ref[...] = dxi_ref[...] + dx
        dg_ref[...] += jnp.sum(dh_v * xh, axis=0, keepdims=True)

    row = pl.BlockSpec((tt, d), lambda i: (i, 0))
    vec = pl.BlockSpec((1, d), lambda i: (0, 0))
    return pl.pallas_call(
        body,
        grid=(t // tt,),
        in_specs=[row, row, vec, row],
        out_specs=[row, vec],
        out_shape=[jax.ShapeDtypeStruct((t, d), F32), jax.ShapeDtypeStruct((1, d), F32)],
        name=name,
        compiler_params=_cparams("arbitrary"),
    )(dh, x, g.reshape(1, d), dx_in)


def _rms_bwd_gain(dh, x, g, name):
    t, d = x.shape
    tt = _rows(t, TT)

    def body(dh_ref, x_ref, dg_ref):
        @pl.when(pl.program_id(0) == 0)
        def _():
            dg_ref[...] = jnp.zeros_like(dg_ref)

        xf = x_ref[...]
        r = lax.rsqrt(jnp.mean(xf * xf, axis=-1, keepdims=True) + EPS)
        dg_ref[...] += jnp.sum(dh_ref[...] * (xf * r), axis=0, keepdims=True)

    row = pl.BlockSpec((tt, d), lambda i: (i, 0))
    vec = pl.BlockSpec((1, d), lambda i: (0, 0))
    return pl.pallas_call(
        body,
        grid=(t // tt,),
        in_specs=[row, row],
        out_specs=vec,
        out_shape=jax.ShapeDtypeStruct((1, d), F32),
        name=name,
        compiler_params=_cparams("arbitrary"),
    )(dh, x)


def _loss_head(x, target, g, name):
    t, d = x.shape
    tt = _rows(t, TT)

    def body(x_ref, t_ref, g_ref, dx_ref, dg_ref, loss_ref):
        @pl.when(pl.program_id(0) == 0)
        def _():
            dg_ref[...] = jnp.zeros_like(dg_ref)
            loss_ref[...] = jnp.zeros_like(loss_ref)

        xf = x_ref[...]
        gv = g_ref[...]
        r = lax.rsqrt(jnp.mean(xf * xf, axis=-1, keepdims=True) + EPS)
        xh = xf * r
        err = xh * gv - t_ref[...]
        e2 = jnp.sum(err * err, axis=-1, keepdims=True)
        loss_ref[...] += (0.5 / d) * jnp.sum(e2, axis=0, keepdims=True)
        dy = err * (1.0 / d)
        gy = dy * gv
        dx_ref[...] = r * (gy - xh * jnp.mean(gy * xh, axis=-1, keepdims=True))
        dg_ref[...] += jnp.sum(dy * xh, axis=0, keepdims=True)

    row = pl.BlockSpec((tt, d), lambda i: (i, 0))
    vec = pl.BlockSpec((1, d), lambda i: (0, 0))
    return pl.pallas_call(
        body,
        grid=(t // tt,),
        in_specs=[row, row, vec],
        out_specs=[row, vec, pl.BlockSpec((1, 1), lambda i: (0, 0))],
        out_shape=[
            jax.ShapeDtypeStruct((t, d), F32),
            jax.ShapeDtypeStruct((1, d), F32),
            jax.ShapeDtypeStruct((1, 1), F32),
        ],
        name=name,
        compiler_params=_cparams("arbitrary"),
    )(x, target, g.reshape(1, d))


def _prev_halo(tt, hp, width):
    return pl.BlockSpec((hp, width), lambda i: (jnp.maximum(i * (tt // hp) - 1, 0), 0))


def _next_halo(tt, hp, width, t):
    return pl.BlockSpec((hp, width), lambda i: (jnp.minimum((i + 1) * (tt // hp), t // hp - 1), 0))


def _ffn_chunks():
    half = D_FF // 2
    return [(c * half, half) for c in range(2)]


def _ffn_fwd(up, conv_w, conv_b, name):
    t = up.shape[0]
    tt = _rows(t, TW)
    hp = FFN_HALO

    def body(up_ref, gp_ref, w_ref, b_ref, act_ref, ext_ref):
        first = pl.program_id(0) == 0
        for c0, cw in _ffn_chunks():
            ga = pl.ds(D_FF + c0, cw)
            ext_ref[0:hp, :] = jnp.where(first, 0.0, gp_ref[:, ga])
            ext_ref[hp : hp + tt, :] = up_ref[:, ga]
            gc = b_ref[:, pl.ds(c0, cw)]
            for j in range(FFN_CONV_K):
                off = hp - (FFN_CONV_K - 1) + j
                gc = gc + w_ref[j : j + 1, pl.ds(c0, cw)] * ext_ref[off : off + tt, :]
            a = up_ref[:, pl.ds(c0, cw)]
            act_ref[:, pl.ds(c0, cw)] = (gc * _sigmoid(gc) * a).astype(act_ref.dtype)

    return pl.pallas_call(
        body,
        grid=(t // tt,),
        in_specs=[
            pl.BlockSpec((tt, 2 * D_FF), lambda i: (i, 0)),
            _prev_halo(tt, hp, 2 * D_FF),
            pl.BlockSpec((FFN_CONV_K, D_FF), lambda i: (0, 0)),
            pl.BlockSpec((1, D_FF), lambda i: (0, 0)),
        ],
        out_specs=pl.BlockSpec((tt, D_FF), lambda i: (i, 0)),
        out_shape=jax.ShapeDtypeStruct((t, D_FF), MXU_DT),
        scratch_shapes=[pltpu.VMEM((tt + hp, D_FF // 2), F32)],
        name=name,
        compiler_params=_cparams("parallel"),
    )(up, up, conv_w, conv_b.reshape(1, D_FF))


def _ffn_bwd(up, dact, conv_w, conv_b, name):
    t = up.shape[0]
    tt = _rows(t, TW)
    hp = FFN_HALO
    nt = t // tt
    kk = FFN_CONV_K

    def body(up_ref, upp_ref, upn_ref, da_ref, dan_ref, w_ref, b_ref, dup_ref, dw_ref, db_ref, ext_ref, dgc_ref):
        i = pl.program_id(0)
        first = i == 0
        last = i == nt - 1

        @pl.when(first)
        def _():
            dw_ref[...] = jnp.zeros_like(dw_ref)
            db_ref[...] = jnp.zeros_like(db_ref)

        for c0, cw in _ffn_chunks():
            ca = pl.ds(c0, cw)
            ga = pl.ds(D_FF + c0, cw)
            ext_ref[0:hp, :] = jnp.where(first, 0.0, upp_ref[:, ga])
            ext_ref[hp : hp + tt, :] = up_ref[:, ga]
            ext_ref[hp + tt : hp + tt + hp, :] = upn_ref[:, ga]
            gc = b_ref[:, ca]
            for j in range(kk):
                off = hp - (kk - 1) + j
                gc = gc + w_ref[j : j + 1, ca] * ext_ref[off : off + tt + hp, :]
            sg = _sigmoid(gc)
            silu = gc * sg
            dsilu = sg * (1.0 + gc * (1.0 - sg))
            a_all = jnp.concatenate([up_ref[:, ca], upn_ref[:, ca]], axis=0)
            dact_all = jnp.concatenate([da_ref[:, ca], jnp.where(last, 0.0, dan_ref[:, ca])], axis=0)
            dgc = dact_all * a_all * dsilu
            dgc_ref[...] = dgc
            dup_ref[:, ca] = (dact_all[0:tt] * silu[0:tt]).astype(dup_ref.dtype)
            dg = jnp.zeros((tt, cw), F32)
            for j in range(kk):
                dg = dg + w_ref[j : j + 1, ca] * dgc_ref[kk - 1 - j : kk - 1 - j + tt, :]
            dup_ref[:, ga] = dg.astype(dup_ref.dtype)
            dgc_t = dgc[0:tt]
            db_ref[:, ca] += jnp.sum(dgc_t, axis=0, keepdims=True)
            for j in range(kk):
                off = hp - (kk - 1) + j
                dw_ref[j : j + 1, ca] += jnp.sum(dgc_t * ext_ref[off : off + tt, :], axis=0, keepdims=True)

    return pl.pallas_call(
        body,
        grid=(nt,),
        in_specs=[
            pl.BlockSpec((tt, 2 * D_FF), lambda i: (i, 0)),
            _prev_halo(tt, hp, 2 * D_FF),
            _next_halo(tt, hp, 2 * D_FF, t),
            pl.BlockSpec((tt, D_FF), lambda i: (i, 0)),
            _next_halo(tt, hp, D_FF, t),
            pl.BlockSpec((kk, D_FF), lambda i: (0, 0)),
            pl.BlockSpec((1, D_FF), lambda i: (0, 0)),
        ],
        out_specs=[
            pl.BlockSpec((tt, 2 * D_FF), lambda i: (i, 0)),
            pl.BlockSpec((kk, D_FF), lambda i: (0, 0)),
            pl.BlockSpec((1, D_FF), lambda i: (0, 0)),
        ],
        out_shape=[
            jax.ShapeDtypeStruct((t, 2 * D_FF), MXU_DT),
            jax.ShapeDtypeStruct((kk, D_FF), F32),
            jax.ShapeDtypeStruct((1, D_FF), F32),
        ],
        scratch_shapes=[pltpu.VMEM((tt + 2 * hp, D_FF // 2), F32), pltpu.VMEM((tt + hp, D_FF // 2), F32)],
        name=name,
        compiler_params=_cparams("arbitrary"),
    )(up, up, up, dact, dact, conv_w, conv_b.reshape(1, D_FF))


def _layernorm_silu(cv, ln_g, ln_b):
    mu = jnp.mean(cv, axis=-1, keepdims=True)
    xc = cv - mu
    rstd = lax.rsqrt(jnp.mean(xc * xc, axis=-1, keepdims=True) + EPS)
    xh = xc * rstd
    a = xh * ln_g + ln_b
    return xh, rstd, a


def _mix_fwd(z, pool_w, pool_scale, dw_w, dw_b, ln_g, ln_b, name):
    t = z.shape[0]
    tt = _rows(t, TW)
    hp = MIX_HALO
    zw = POOL_W + 2 * CONV_W

    def body(z_ref, zp_ref, pw_ref, ps_ref, w_ref, b_ref, lg_ref, lb_ref, cat_ref, eu_ref, egl_ref):
        i = pl.program_id(0)
        first = i == 0
        eu_ref[0:hp, :] = jnp.where(first, 0.0, zp_ref[:, 0:POOL_W])
        eu_ref[hp : hp + tt, :] = z_ref[:, 0:POOL_W]
        glp = zp_ref[:, POOL_W : POOL_W + CONV_W] * _sigmoid(zp_ref[:, POOL_W + CONV_W : zw])
        egl_ref[0:hp, :] = jnp.where(first, 0.0, glp)
        egl_ref[hp : hp + tt, :] = z_ref[:, POOL_W : POOL_W + CONV_W] * _sigmoid(z_ref[:, POOL_W + CONV_W : zw])
        row = i * tt + lax.broadcasted_iota(jnp.int32, (tt, 1), 0)
        for gi, w in enumerate(POOL_WINDOWS):
            cols = pl.ds(gi * POOL_GROUP, POOL_GROUP)
            u = eu_ref[hp : hp + tt, cols]
            acc = u
            for k in range(1, w):
                acc = acc + eu_ref[hp - k : hp - k + tt, cols]
            cnt = jnp.minimum(row + 1, w).astype(F32)
            pooled = acc / cnt - u
            y = jnp.dot(pooled.astype(MXU_DT), pw_ref[gi].astype(MXU_DT), preferred_element_type=F32)
            cat_ref[:, cols] = (y * ps_ref[:, cols]).astype(cat_ref.dtype)
        cv = b_ref[...]
        for j in range(CONV_K):
            off = hp - (CONV_K - 1) + j
            cv = cv + w_ref[j : j + 1, :] * egl_ref[off : off + tt, :]
        _, _, a = _layernorm_silu(cv, lg_ref[...], lb_ref[...])
        cat_ref[:, POOL_W : POOL_W + CONV_W] = (a * _sigmoid(a)).astype(cat_ref.dtype)

    vec = pl.BlockSpec((1, CONV_W), lambda i: (0, 0))
    return pl.pallas_call(
        body,
        grid=(t // tt,),
        in_specs=[
            pl.BlockSpec((tt, zw), lambda i: (i, 0)),
            _prev_halo(tt, hp, zw),
            pl.BlockSpec((len(POOL_WINDOWS), POOL_GROUP, POOL_GROUP), lambda i: (0, 0, 0)),
            vec,
            pl.BlockSpec((CONV_K, CONV_W), lambda i: (0, 0)),
            vec,
            vec,
            vec,
        ],
        out_specs=pl.BlockSpec((tt, POOL_W + CONV_W), lambda i: (i, 0)),
        out_shape=jax.ShapeDtypeStruct((t, POOL_W + CONV_W), MXU_DT),
        scratch_shapes=[pltpu.VMEM((tt + hp, POOL_W), F32), pltpu.VMEM((tt + hp, CONV_W), F32)],
        name=name,
        compiler_params=_cparams("parallel"),
    )(z, z, pool_w, pool_scale.reshape(1, POOL_W), dw_w, dw_b.reshape(1, CONV_W), ln_g.reshape(1, CONV_W), ln_b.reshape(1, CONV_W))


def _mix_bwd(z, dcat, pool_w, pool_scale, dw_w, dw_b, ln_g, ln_b, name):
    t = z.shape[0]
    tt = _rows(t, TW)
    hp = MIX_HALO
    nt = t // tt
    zw = POOL_W + 2 * CONV_W
    ng = len(POOL_WINDOWS)

    def body(z_ref, zp_ref, zn_ref, dc_ref, dcn_ref, pw_ref, ps_ref, w_ref, b_ref, lg_ref, lb_ref,
             dz_ref, dpw_ref, dps_ref, dww_ref, dwb_ref, dlg_ref, dlb_ref, eu_ref, ee_ref, egl_ref, edcv_ref):
        i = pl.program_id(0)
        first = i == 0
        last = i == nt - 1

        @pl.when(first)
        def _():
            for r in (dpw_ref, dps_ref, dww_ref, dwb_ref, dlg_ref, dlb_ref):
                r[...] = jnp.zeros_like(r)

        eu_ref[0:hp, :] = jnp.where(first, 0.0, zp_ref[:, 0:POOL_W])
        eu_ref[hp : hp + tt, :] = z_ref[:, 0:POOL_W]
        row = i * tt + lax.broadcasted_iota(jnp.int32, (tt, 1), 0)
        row_ext = i * tt + lax.broadcasted_iota(jnp.int32, (tt + hp, 1), 0)
        for gi, w in enumerate(POOL_WINDOWS):
            cols = pl.ds(gi * POOL_GROUP, POOL_GROUP)
            u = eu_ref[hp : hp + tt, cols]
            acc = u
            for k in range(1, w):
                acc = acc + eu_ref[hp - k : hp - k + tt, cols]
            pooled = (acc / jnp.minimum(row + 1, w).astype(F32) - u).astype(MXU_DT)
            pw = pw_ref[gi].astype(MXU_DT)
            dya = dc_ref[:, cols]
            y = jnp.dot(pooled, pw, preferred_element_type=F32)
            dps_ref[:, cols] += jnp.sum(dya * y, axis=0, keepdims=True)
            scale = ps_ref[:, cols]
            dy_all = jnp.concatenate([dya, jnp.where(last, 0.0, dcn_ref[:, cols])], axis=0) * scale
            dy_all = dy_all.astype(MXU_DT)
            dpw_ref[gi] += lax.dot_general(pooled, dy_all[0:tt], TN_DIMS, preferred_element_type=F32)
            dpooled = lax.dot_general(dy_all, pw, NT, preferred_element_type=F32)
            ee_ref[:, cols] = dpooled / jnp.minimum(row_ext + 1, w).astype(F32)
            du = -dpooled[0:tt]
            for k in range(w):
                du = du + ee_ref[k : k + tt, cols]
            dz_ref[:, cols] = du.astype(dz_ref.dtype)

        ca = slice(POOL_W, POOL_W + CONV_W)
        cb = slice(POOL_W + CONV_W, zw)
        egl_ref[0:hp, :] = jnp.where(first, 0.0, zp_ref[:, ca] * _sigmoid(zp_ref[:, cb]))
        ga = z_ref[:, ca]
        sgb = _sigmoid(z_ref[:, cb])
        egl_ref[hp : hp + tt, :] = ga * sgb
        egl_ref[hp + tt : hp + tt + hp, :] = zn_ref[:, ca] * _sigmoid(zn_ref[:, cb])
        cv = b_ref[...]
        for j in range(CONV_K):
            off = hp - (CONV_K - 1) + j
            cv = cv + w_ref[j : j + 1, :] * egl_ref[off : off + tt + hp, :]
        lg = lg_ref[...]
        xh, rstd, a = _layernorm_silu(cv, lg, lb_ref[...])
        sa = _sigmoid(a)
        dyb = jnp.concatenate([dc_ref[:, ca], jnp.where(last, 0.0, dcn_ref[:, ca])], axis=0)
        da = dyb * (sa * (1.0 + a * (1.0 - sa)))
        dlg_ref[...] += jnp.sum(da[0:tt] * xh[0:tt], axis=0, keepdims=True)
        dlb_ref[...] += jnp.sum(da[0:tt], axis=0, keepdims=True)
        dxh = da * lg
        dcv = rstd * (dxh - jnp.mean(dxh, axis=-1, keepdims=True) - xh * jnp.mean(dxh * xh, axis=-1, keepdims=True))
        edcv_ref[...] = dcv
        dcv_t = dcv[0:tt]
        dwb_ref[...] += jnp.sum(dcv_t, axis=0, keepdims=True)
        dgl = jnp.zeros((tt, CONV_W), F32)
        for j in range(CONV_K):
            off = hp - (CONV_K - 1) + j
            dww_ref[j : j + 1, :] += jnp.sum(dcv_t * egl_ref[off : off + tt, :], axis=0, keepdims=True)
            dgl = dgl + w_ref[j : j + 1, :] * edcv_ref[CONV_K - 1 - j : CONV_K - 1 - j + tt, :]
        dz_ref[:, ca] = (dgl * sgb).astype(dz_ref.dtype)
        dz_ref[:, cb] = (dgl * ga * sgb * (1.0 - sgb)).astype(dz_ref.dtype)

    vec = pl.BlockSpec((1, CONV_W), lambda i: (0, 0))
    pw_spec = pl.BlockSpec((ng, POOL_GROUP, POOL_GROUP), lambda i: (0, 0, 0))
    w_spec = pl.BlockSpec((CONV_K, CONV_W), lambda i: (0, 0))
    return pl.pallas_call(
        body,
        grid=(nt,),
        in_specs=[
            pl.BlockSpec((tt, zw), lambda i: (i, 0)),
            _prev_halo(tt, hp, zw),
            _next_halo(tt, hp, zw, t),
            pl.BlockSpec((tt, POOL_W + CONV_W), lambda i: (i, 0)),
            _next_halo(tt, hp, POOL_W + CONV_W, t),
            pw_spec, vec, w_spec, vec, vec, vec,
        ],
        out_specs=[pl.BlockSpec((tt, zw), lambda i: (i, 0)), pw_spec, vec, w_spec, vec, vec, vec],
        out_shape=[
            jax.ShapeDtypeStruct((t, zw), MXU_DT),
            jax.ShapeDtypeStruct((ng, POOL_GROUP, POOL_GROUP), F32),
            jax.ShapeDtypeStruct((1, POOL_W), F32),
            jax.ShapeDtypeStruct((CONV_K, CONV_W), F32),
            jax.ShapeDtypeStruct((1, CONV_W), F32),
            jax.ShapeDtypeStruct((1, CONV_W), F32),
            jax.ShapeDtypeStruct((1, CONV_W), F32),
        ],
        scratch_shapes=[
            pltpu.VMEM((tt + hp, POOL_W), F32),
            pltpu.VMEM((tt + hp, POOL_W), F32),
            pltpu.VMEM((tt + 2 * hp, CONV_W), F32),
            pltpu.VMEM((tt + hp, CONV_W), F32),
        ],
        name=name,
        compiler_params=_cparams("arbitrary"),
    )(z, z, z, dcat, dcat, pool_w, pool_scale.reshape(1, POOL_W), dw_w, dw_b.reshape(1, CONV_W),
      ln_g.reshape(1, CONV_W), ln_b.reshape(1, CONV_W))


def _xa_fwd(q, kvm, name):
    t = q.shape[0]
    tt = _rows(t, TT)
    scale = XA_HEAD_DIM ** -0.5

    def body(q_ref, kv_ref, o_ref):
        for h in range(XA_HEADS):
            cs = pl.ds(h * XA_HEAD_DIM, XA_HEAD_DIM)
            vs = pl.ds(D_MODEL + h * XA_HEAD_DIM, XA_HEAD_DIM)
            s = lax.dot_general(q_ref[:, cs], kv_ref[:, cs], NT, preferred_element_type=F32) * scale
            p = jnp.exp(s - jnp.max(s, axis=-1, keepdims=True))
            p = p / jnp.sum(p, axis=-1, keepdims=True)
            o_ref[:, cs] = jnp.dot(p.astype(MXU_DT), kv_ref[:, vs], preferred_element_type=F32).astype(o_ref.dtype)

    return pl.pallas_call(
        body,
        grid=(t // tt,),
        in_specs=[pl.BlockSpec((tt, D_MODEL), lambda i: (i, 0)), pl.BlockSpec((MEM_LEN, 2 * D_MODEL), lambda i: (0, 0))],
        out_specs=pl.BlockSpec((tt, D_MODEL), lambda i: (i, 0)),
        out_shape=jax.ShapeDtypeStruct((t, D_MODEL), MXU_DT),
        name=name,
        compiler_params=_cparams("parallel"),
    )(q, kvm)


def _xa_bwd(q, kvm, do, name):
    t = q.shape[0]
    tt = _rows(t, TT)
    scale = XA_HEAD_DIM ** -0.5

    def body(q_ref, kv_ref, do_ref, dq_ref, dkv_ref):
        @pl.when(pl.program_id(0) == 0)
        def _():
            dkv_ref[...] = jnp.zeros_like(dkv_ref)

        for h in range(XA_HEADS):
            cs = pl.ds(h * XA_HEAD_DIM, XA_HEAD_DIM)
            vs = pl.ds(D_MODEL + h * XA_HEAD_DIM, XA_HEAD_DIM)
            qh = q_ref[:, cs]
            kh = kv_ref[:, cs]
            doh = do_ref[:, cs]
            s = lax.dot_general(qh, kh, NT, preferred_element_type=F32) * scale
            p = jnp.exp(s - jnp.max(s, axis=-1, keepdims=True))
            p = p / jnp.sum(p, axis=-1, keepdims=True)
            dp = lax.dot_general(doh, kv_ref[:, vs], NT, preferred_element_type=F32)
            ds = (p * (dp - jnp.sum(p * dp, axis=-1, keepdims=True)) * scale).astype(MXU_DT)
            dq_ref[:, cs] = jnp.dot(ds, kh, preferred_element_type=F32).astype(dq_ref.dtype)
            dkv_ref[:, cs] += lax.dot_general(ds, qh, TN_DIMS, preferred_element_type=F32)
            dkv_ref[:, vs] += lax.dot_general(p.astype(MXU_DT), doh, TN_DIMS, preferred_element_type=F32)

    row = pl.BlockSpec((tt, D_MODEL), lambda i: (i, 0))
    kvs = pl.BlockSpec((MEM_LEN, 2 * D_MODEL), lambda i: (0, 0))
    return pl.pallas_call(
        body,
        grid=(t // tt,),
        in_specs=[row, kvs, row],
        out_specs=[row, kvs],
        out_shape=[jax.ShapeDtypeStruct((t, D_MODEL), MXU_DT), jax.ShapeDtypeStruct((MEM_LEN, 2 * D_MODEL), F32)],
        name=name,
        compiler_params=_cparams("arbitrary"),
    )(q, kvm, do)


def _rope_tables(positions, name):
    t = positions.shape[0]
    tt = _rows(t, TT)
    inv = 1.0 / (ROPE_THETA ** (np.arange(0, QK_ROPE, 2, dtype=np.float32) / QK_ROPE))
    lanes = np.zeros((1, HEAD_PAD), np.float32)
    half = QK_ROPE // 2
    lanes[0, KPE_LANE : KPE_LANE + half] = inv
    lanes[0, KPE_LANE + half : KPE_LANE + QK_ROPE] = inv

    def body(pos_ref, inv_ref, cos_ref, sa_ref, sb_ref):
        ang = pos_ref[...].astype(F32) * inv_ref[...]
        lane = lax.broadcasted_iota(jnp.int32, (tt, HEAD_PAD), 1)
        c = jnp.cos(ang)
        s = jnp.sin(ang)
        lo = (lane >= KPE_LANE) & (lane < KPE_LANE + half)
        hi = (lane >= KPE_LANE + half) & (lane < KPE_LANE + QK_ROPE)
        cos_ref[...] = jnp.where(lo | hi, c, 1.0)
        sa_ref[...] = jnp.where(hi, s, 0.0)
        sb_ref[...] = jnp.where(lo, -s, 0.0)

    tab = pl.BlockSpec((tt, HEAD_PAD), lambda i: (i, 0))
    return pl.pallas_call(
        body,
        grid=(t // tt,),
        in_specs=[pl.BlockSpec((tt, 1), lambda i: (i, 0)), pl.BlockSpec((1, HEAD_PAD), lambda i: (0, 0))],
        out_specs=[tab, tab, tab],
        out_shape=[jax.ShapeDtypeStruct((t, HEAD_PAD), F32)] * 3,
        name=name,
        compiler_params=_cparams("parallel"),
    )(positions, jnp.asarray(lanes))


def _rotate(x, cos, sa, sb, sign):
    half = QK_ROPE // 2
    return x * cos + sign * (pltpu.roll(x, half, 1) * sa + pltpu.roll(x, HEAD_PAD - half, 1) * sb)


def _rope_heads(x, tables, sign, name):
    t, w = x.shape
    tt = _rows(t, TT)
    nh = w // HEAD_PAD

    def body(x_ref, c_ref, sa_ref, sb_ref, o_ref):
        cos, sa, sb = c_ref[...], sa_ref[...], sb_ref[...]
        for h in range(nh):
            cs = pl.ds(h * HEAD_PAD, HEAD_PAD)
            o_ref[:, cs] = _rotate(x_ref[:, cs], cos, sa, sb, sign).astype(o_ref.dtype)

    tab = pl.BlockSpec((tt, HEAD_PAD), lambda i: (i, 0))
    row = pl.BlockSpec((tt, w), lambda i: (i, 0))
    return pl.pallas_call(
        body,
        grid=(t // tt,),
        in_specs=[row, tab, tab, tab],
        out_specs=row,
        out_shape=jax.ShapeDtypeStruct((t, w), MXU_DT),
        name=name,
        compiler_params=_cparams("parallel"),
    )(x, *tables)


def _mla_prep(cp, qg, kvg, tables, name):
    t = cp.shape[0]
    tt = _rows(t, TT)

    def body(cp_ref, qg_ref, kvg_ref, c_ref, sa_ref, sb_ref, qn_ref, kvn_ref, kpe_ref):
        cq = cp_ref[:, 0:Q_LORA]
        r = lax.rsqrt(jnp.mean(cq * cq, axis=-1, keepdims=True) + EPS)
        qn_ref[...] = ((cq * r) * qg_ref[...]).astype(qn_ref.dtype)
        ckv = cp_ref[:, Q_LORA : Q_LORA + KV_LORA]
        r = lax.rsqrt(jnp.mean(ckv * ckv, axis=-1, keepdims=True) + EPS)
        kvn_ref[...] = ((ckv * r) * kvg_ref[...]).astype(kvn_ref.dtype)
        kpe = cp_ref[:, Q_LORA + KV_LORA : C_PAD]
        kpe_ref[...] = _rotate(kpe, c_ref[...], sa_ref[...], sb_ref[...], 1.0).astype(kpe_ref.dtype)

    tab = pl.BlockSpec((tt, HEAD_PAD), lambda i: (i, 0))
    return pl.pallas_call(
        body,
        grid=(t // tt,),
        in_specs=[
            pl.BlockSpec((tt, C_PAD), lambda i: (i, 0)),
            pl.BlockSpec((1, Q_LORA), lambda i: (0, 0)),
            pl.BlockSpec((1, KV_LORA), lambda i: (0, 0)),
            tab, tab, tab,
        ],
        out_specs=[
            pl.BlockSpec((tt, Q_LORA), lambda i: (i, 0)),
            pl.BlockSpec((tt, KV_LORA), lambda i: (i, 0)),
            tab,
        ],
        out_shape=[
            jax.ShapeDtypeStruct((t, Q_LORA), MXU_DT),
            jax.ShapeDtypeStruct((t, KV_LORA), MXU_DT),
            jax.ShapeDtypeStruct((t, HEAD_PAD), MXU_DT),
        ],
        name=name,
        compiler_params=_cparams("parallel"),
    )(cp, qg.reshape(1, Q_LORA), kvg.reshape(1, KV_LORA), *tables)


def _mla_prep_bwd(cp, dqn, dkvn, dkpe_heads, qg, kvg, tables, name):
    t = cp.shape[0]
    tt = _rows(t, TT)

    def norm_bwd(x, dy, g):
        r = lax.rsqrt(jnp.mean(x * x, axis=-1, keepdims=True) + EPS)
        xh = x * r
        gy = dy * g
        return r * (gy - xh * jnp.mean(gy * xh, axis=-1, keepdims=True)), jnp.sum(dy * xh, axis=0, keepdims=True)

    def body(cp_ref, dqn_ref, dkvn_ref, dkpe_ref, qg_ref, kvg_ref, c_ref, sa_ref, sb_ref, dcp_ref, dqg_ref, dkvg_ref):
        @pl.when(pl.program_id(0) == 0)
        def _():
            dqg_ref[...] = jnp.zeros_like(dqg_ref)
            dkvg_ref[...] = jnp.zeros_like(dkvg_ref)

        dcq, dg = norm_bwd(cp_ref[:, 0:Q_LORA], dqn_ref[...], qg_ref[...])
        dcp_ref[:, 0:Q_LORA] = dcq.astype(dcp_ref.dtype)
        dqg_ref[...] += dg
        dckv, dg = norm_bwd(cp_ref[:, Q_LORA : Q_LORA + KV_LORA], dkvn_ref[...], kvg_ref[...])
        dcp_ref[:, Q_LORA : Q_LORA + KV_LORA] = dckv.astype(dcp_ref.dtype)
        dkvg_ref[...] += dg
        dk = dkpe_ref[0]
        for h in range(1, MLA_HEADS):
            dk = dk + dkpe_ref[h]
        dcp_ref[:, Q_LORA + KV_LORA : C_PAD] = _rotate(dk, c_ref[...], sa_ref[...], sb_ref[...], -1.0).astype(dcp_ref.dtype)

    tab = pl.BlockSpec((tt, HEAD_PAD), lambda i: (i, 0))
    return pl.pallas_call(
        body,
        grid=(t // tt,),
        in_specs=[
            pl.BlockSpec((tt, C_PAD), lambda i: (i, 0)),
            pl.BlockSpec((tt, Q_LORA), lambda i: (i, 0)),
            pl.BlockSpec((tt, KV_LORA), lambda i: (i, 0)),
            pl.BlockSpec((MLA_HEADS, tt, HEAD_PAD), lambda i: (0, i, 0)),
            pl.BlockSpec((1, Q_LORA), lambda i: (0, 0)),
            pl.BlockSpec((1, KV_LORA), lambda i: (0, 0)),
            tab, tab, tab,
        ],
        out_specs=[
            pl.BlockSpec((tt, C_PAD), lambda i: (i, 0)),
            pl.BlockSpec((1, Q_LORA), lambda i: (0, 0)),
            pl.BlockSpec((1, KV_LORA), lambda i: (0, 0)),
        ],
        out_shape=[
            jax.ShapeDtypeStruct((t, C_PAD), MXU_DT),
            jax.ShapeDtypeStruct((1, Q_LORA), F32),
            jax.ShapeDtypeStruct((1, KV_LORA), F32),
        ],
        name=name,
        compiler_params=_cparams("arbitrary"),
    )(cp, dqn, dkvn, dkpe_heads, qg.reshape(1, Q_LORA), kvg.reshape(1, KV_LORA), *tables)


def _flash_fwd(qr, kv, kpe, name):
    t = qr.shape[0]
    ta = _rows(t, TA)
    nq = t // ta

    def body(q_ref, kv_ref, kpe_ref, o_ref, lse_ref):
        qi = pl.program_id(1)
        q = q_ref[...]
        lane = lax.broadcasted_iota(jnp.int32, (ta, HEAD_PAD), 1)

        def kblock(j):
            rows = pl.ds(pl.multiple_of(j * ta, ta), ta)
            kvb = kv_ref[rows, :]
            return kvb, jnp.where(lane < QK_NOPE, kvb, kpe_ref[rows, :])

        def update(carry, s, kvb):
            m, l, acc = carry
            m_new = jnp.maximum(m, jnp.max(s, axis=-1, keepdims=True))
            alpha = jnp.exp(m - m_new)
            p = jnp.exp(s - m_new)
            l = alpha * l + jnp.sum(p, axis=-1, keepdims=True)
            acc = alpha * acc + jnp.dot(p.astype(MXU_DT), kvb, preferred_element_type=F32)
            return m_new, l, acc

        def step(j, carry):
            kvb, k = kblock(j)
            s = lax.dot_general(q, k, NT, preferred_element_type=F32) * MLA_SCALE
            return update(carry, s, kvb)

        init = (jnp.full((ta, 1), -jnp.inf, F32), jnp.zeros((ta, 1), F32), jnp.zeros((ta, HEAD_PAD), F32))
        carry = lax.fori_loop(0, qi, step, init)
        kvb, k = kblock(qi)
        s = lax.dot_general(q, k, NT, preferred_element_type=F32) * MLA_SCALE
        r = lax.broadcasted_iota(jnp.int32, (ta, ta), 0)
        c = lax.broadcasted_iota(jnp.int32, (ta, ta), 1)
        m, l, acc = update(carry, jnp.where(c <= r, s, NEG), kvb)
        o_ref[...] = jnp.where(lane >= QK_NOPE, acc / l, 0.0).astype(o_ref.dtype)
        lse_ref[...] = m + jnp.log(l)

    return pl.pallas_call(
        body,
        grid=(MLA_HEADS, nq),
        in_specs=[
            pl.BlockSpec((ta, HEAD_PAD), lambda h, i: (i, h)),
            pl.BlockSpec((t, HEAD_PAD), lambda h, i: (0, h)),
            pl.BlockSpec((t, HEAD_PAD), lambda h, i: (0, 0)),
        ],
        out_specs=[
            pl.BlockSpec((ta, HEAD_PAD), lambda h, i: (i, h)),
            pl.BlockSpec((None, ta, 1), lambda h, i: (h, i, 0)),
        ],
        out_shape=[
            jax.ShapeDtypeStruct((t, MLA_HEADS * HEAD_PAD), MXU_DT),
            jax.ShapeDtypeStruct((MLA_HEADS, t, 1), F32),
        ],
        name=name,
        compiler_params=_cparams("parallel", "parallel"),
    )(qr, kv, kpe)


def _flash_bwd(qr, kv, kpe, o, do, lse, name):
    t = qr.shape[0]
    ta = _rows(t, TA)
    nq = t // ta

    def body(q_ref, o_ref, do_ref, lse_ref, kv_ref, kpe_ref, dq_ref, dkv_ref, dkpe_ref, dk_acc, dv_acc):
        kj = pl.program_id(1)

        @pl.when(kj == 0)
        def _():
            dq_ref[...] = jnp.zeros_like(dq_ref)

        lane = lax.broadcasted_iota(jnp.int32, (ta, HEAD_PAD), 1)
        kvb = kv_ref[...]
        k = jnp.where(lane < QK_NOPE, kvb, kpe_ref[...])
        dk_acc[...] = jnp.zeros_like(dk_acc)
        dv_acc[...] = jnp.zeros_like(dv_acc)

        def tile(qi, diagonal):
            rows = pl.ds(pl.multiple_of(qi * ta, ta), ta)
            q = q_ref[rows, :]
            dob = do_ref[rows, :]
            delta = jnp.sum(dob.astype(F32) * o_ref[rows, :].astype(F32), axis=-1, keepdims=True)
            s = lax.dot_general(q, k, NT, preferred_element_type=F32) * MLA_SCALE
            if diagonal:
                r = lax.broadcasted_iota(jnp.int32, (ta, ta), 0)
                c = lax.broadcasted_iota(jnp.int32, (ta, ta), 1)
                s = jnp.where(c <= r, s, NEG)
            p = jnp.exp(s - lse_ref[rows, :])
            dp = lax.dot_general(dob, kvb, NT, preferred_element_type=F32)
            ds = (p * (dp - delta) * MLA_SCALE).astype(MXU_DT)
            dq_ref[rows, :] += jnp.dot(ds, k, preferred_element_type=F32)
            dk_acc[...] += lax.dot_general(ds, q, TN_DIMS, preferred_element_type=F32)
            dv_acc[...] += lax.dot_general(p.astype(MXU_DT), dob, TN_DIMS, preferred_element_type=F32)

        tile(kj, True)

        def step(qi, carry):
            tile(qi, False)
            return carry

        lax.fori_loop(kj + 1, nq, step, 0)
        dk = dk_acc[...]
        dkv_ref[...] = jnp.where(lane < QK_NOPE, dk, dv_acc[...]).astype(dkv_ref.dtype)
        dkpe_ref[...] = jnp.where((lane >= KPE_LANE) & (lane < KPE_LANE + QK_ROPE), dk, 0.0)

    head_rows = pl.BlockSpec((t, HEAD_PAD), lambda h, j: (0, h))
    return pl.pallas_call(
        body,
        grid=(MLA_HEADS, nq),
        in_specs=[
            head_rows,
            head_rows,
            head_rows,
            pl.BlockSpec((None, t, 1), lambda h, j: (h, 0, 0)),
            pl.BlockSpec((ta, HEAD_PAD), lambda h, j: (j, h)),
            pl.BlockSpec((ta, HEAD_PAD), lambda h, j: (j, 0)),
        ],
        out_specs=[
            head_rows,
            pl.BlockSpec((ta, HEAD_PAD), lambda h, j: (j, h)),
            pl.BlockSpec((None, ta, HEAD_PAD), lambda h, j: (h, j, 0)),
        ],
        out_shape=[
            jax.ShapeDtypeStruct((t, MLA_HEADS * HEAD_PAD), F32),
            jax.ShapeDtypeStruct((t, MLA_HEADS * HEAD_PAD), MXU_DT),
            jax.ShapeDtypeStruct((MLA_HEADS, t, HEAD_PAD), F32),
        ],
        scratch_shapes=[pltpu.VMEM((ta, HEAD_PAD), F32), pltpu.VMEM((ta, HEAD_PAD), F32)],
        name=name,
        compiler_params=_cparams("parallel", "arbitrary"),
    )(qr, o, do, lse, kv, kpe)


def _as2d(a):
    if a.ndim == 1:
        return a.reshape(1, a.shape[0])
    return a.reshape(-1, a.shape[-1])


def _adamw(w, g, m, v, name):
    shape = w.shape
    w2, g2, m2, v2 = (_as2d(a) for a in (w, g, m, v))
    r, c = w2.shape
    tr = _tile_rows(r, c)
    c1 = 1.0 - ADAM_B1 ** ADAM_STEP
    c2 = 1.0 - ADAM_B2 ** ADAM_STEP

    def body(w_ref, g_ref, m_ref, v_ref, d_ref, nm_ref, nv_ref):
        gv = g_ref[...]
        nm = ADAM_B1 * m_ref[...] + (1.0 - ADAM_B1) * gv
        nv = ADAM_B2 * v_ref[...] + (1.0 - ADAM_B2) * (gv * gv)
        d_ref[...] = -ADAM_LR * ((nm / c1) / (jnp.sqrt(nv / c2) + ADAM_EPS) + ADAM_WD * w_ref[...])
        nm_ref[...] = nm
        nv_ref[...] = nv

    blk = pl.BlockSpec((tr, c), lambda i: (i, 0))
    outs = pl.pallas_call(
        body,
        grid=(r // tr,),
        in_specs=[blk] * 4,
        out_specs=[blk] * 3,
        out_shape=[jax.ShapeDtypeStruct((r, c), F32)] * 3,
        name=name,
        compiler_params=_cparams("parallel"),
    )(w2, g2, m2, v2)
    return tuple(o.reshape(shape) for o in outs)


def _tile_rows(r, c):
    limit = max(SUBLANE, (BLOCK_BYTES // 4) // (4 * c))
    if r <= limit:
        return r
    t = (limit // SUBLANE) * SUBLANE
    while t >= SUBLANE:
        if r % t == 0:
            return t
        t -= SUBLANE
    return r


def _sum_leading(a, name):
    n, r, c = a.shape
    tr = _tile_rows(r, c * n)

    def body(a_ref, o_ref):
        s = a_ref[0]
        for k in range(1, n):
            s = s + a_ref[k]
        o_ref[...] = s

    return pl.pallas_call(
        body,
        grid=(r // tr,),
        in_specs=[pl.BlockSpec((n, tr, c), lambda i: (0, i, 0))],
        out_specs=pl.BlockSpec((tr, c), lambda i: (i, 0)),
        out_shape=jax.ShapeDtypeStruct((r, c), F32),
        name=name,
        compiler_params=_cparams("parallel"),
    )(a)


def _add_half(g, s, c_idx, name):
    nl, r, c = g.shape
    h = nl // 2
    tr = _tile_rows(r, 2 * c)

    def body(c_ref, g_ref, s_ref, o_ref):
        o_ref[...] = g_ref[...] + s_ref[...]

    grid_spec = pltpu.PrefetchScalarGridSpec(
        num_scalar_prefetch=1,
        grid=(h, r // tr),
        in_specs=[
            pl.BlockSpec((None, tr, c), lambda l, i, cr: (cr[0] * h + l, i, 0)),
            pl.BlockSpec((None, tr, c), lambda l, i, cr: (l, i, 0)),
        ],
        out_specs=pl.BlockSpec((None, tr, c), lambda l, i, cr: (l, i, 0)),
    )
    return pl.pallas_call(
        body,
        grid_spec=grid_spec,
        out_shape=jax.ShapeDtypeStruct((h, r, c), F32),
        name=name,
        compiler_params=_cparams("parallel", "parallel"),
    )(c_idx, g, s)


def _mesh_pos():
    return lax.axis_index("x"), lax.axis_index("y"), lax.axis_index("c")


def _other_chips(x, y):
    return [(1 - x, y), (x, 1 - y), (1 - x, 1 - y)]


def _all_gather_rows(block, name):
    m_per, n = block.shape

    def body(x_ref, out_ref, send_sems, recv_sems, local_sem):
        x, y, c = _mesh_pos()
        me, sibling = (x, y, c), (x, y, 1 - c)
        chips = _other_chips(x, y)

        def rows(px, py, pc):
            return out_ref.at[pl.ds((4 * px + 2 * py + pc) * m_per, m_per), :]

        def copy(k, blk, to, src=None):
            return pltpu.make_async_remote_copy(
                src_ref=rows(*blk) if src is None else src,
                dst_ref=rows(*blk),
                send_sem=send_sems.at[k],
                recv_sem=recv_sems.at[k],
                device_id=to,
                device_id_type=MESH_ID,
            )

        mine = pltpu.make_async_copy(x_ref, rows(*me), local_sem)
        mine.start()
        first = [copy(0, me, sibling, src=x_ref)]
        first += [copy(1 + j, me, (*chip, c), src=x_ref) for j, chip in enumerate(chips)]
        for cp in first:
            cp.start()
        passed = [copy(4 + j, (*chip, c), sibling) for j, chip in enumerate(chips)]
        for j, chip in enumerate(chips):
            copy(1 + j, (*chip, c), me).wait_recv()
            passed[j].start()
        copy(0, sibling, me).wait_recv()
        for j, chip in enumerate(chips):
            copy(4 + j, (*chip, 1 - c), me).wait_recv()
        for cp in first + passed:
            cp.wait_send()
        mine.wait()

    return pl.pallas_call(
        body,
        out_shape=jax.ShapeDtypeStruct((8 * m_per, n), block.dtype),
        in_specs=[pl.BlockSpec(memory_space=pltpu.VMEM)],
        out_specs=pl.BlockSpec(memory_space=pltpu.VMEM),
        scratch_shapes=[pltpu.SemaphoreType.DMA((7,)), pltpu.SemaphoreType.DMA((7,)), pltpu.SemaphoreType.DMA],
        name=name,
        compiler_params=pltpu.CompilerParams(vmem_limit_bytes=VMEM_LIMIT),
    )(block)


def _shard_window(ref, layers, chip, rows, cols):
    if rows is not None:
        return ref.at[layers, pl.ds(pl.multiple_of(chip * rows, rows), rows), :]
    return ref.at[layers, :, pl.ds(pl.multiple_of(chip * cols, cols), cols)]


def _all_gather_weights(shards, kinds, name):
    nw = len(shards)
    out_shapes = []
    for s, kind in zip(shards, kinds):
        nl, r, c = s.shape
        full = (nl, 4 * r, c) if kind == "row" else (nl, r, 4 * c)
        out_shapes.append(jax.ShapeDtypeStruct(full, s.dtype))

    def body(*refs):
        ins, outs = refs[:nw], refs[nw : 2 * nw]
        send_sems, recv_sems, local_sems = refs[2 * nw :]
        x, y, c = _mesh_pos()
        sibling = (x, y, 1 - c)
        chips = _other_chips(x, y)
        my_chip = 2 * x + y

        def window(w, chip, half_idx):
            nl, r, cc = shards[w].shape
            h = nl // 2
            layers = pl.ds(half_idx * h, h)
            if kinds[w] == "row":
                return _shard_window(outs[w], layers, chip, r, None)
            return _shard_window(outs[w], layers, chip, None, cc)

        def whole(w, chip):
            nl, r, cc = shards[w].shape
            if kinds[w] == "row":
                return _shard_window(outs[w], pl.ds(0, nl), chip, r, None)
            return _shard_window(outs[w], pl.ds(0, nl), chip, None, cc)

        def copy(w, k, src, dst, to):
            return pltpu.make_async_remote_copy(
                src_ref=src, dst_ref=dst, send_sem=send_sems.at[w, k], recv_sem=recv_sems.at[w, k],
                device_id=to, device_id_type=MESH_ID)

        local, sent = [], []
        for w in range(nw):
            h = shards[w].shape[0] // 2
            cp = pltpu.make_async_copy(ins[w], whole(w, my_chip), local_sems.at[w])
            cp.start()
            local.append(cp)
            mine = ins[w].at[pl.ds(c * h, h)]
            for j, chip in enumerate(chips):
                cp = copy(w, j, mine, window(w, my_chip, c), (*chip, c))
                cp.start()
                sent.append(cp)
        for w in range(nw):
            for j, (cx, cy) in enumerate(chips):
                got = window(w, 2 * cx + cy, c)
                copy(w, j, got, got, (cx, cy, c)).wait_recv()
                cp = copy(w, 3 + j, got, got, sibling)
                cp.start()
                sent.append(cp)
        for w in range(nw):
            for j, (cx, cy) in enumerate(chips):
                got = window(w, 2 * cx + cy, 1 - c)
                copy(w, 3 + j, got, got, sibling).wait_recv()
        for cp in sent:
            cp.wait_send()
        for cp in local:
            cp.wait()

    anyspec = pl.BlockSpec(memory_space=pl.ANY)
    return pl.pallas_call(
        body,
        out_shape=out_shapes,
        in_specs=[anyspec] * nw,
        out_specs=[anyspec] * nw,
        scratch_shapes=[pltpu.SemaphoreType.DMA((nw, 6)), pltpu.SemaphoreType.DMA((nw, 6)), pltpu.SemaphoreType.DMA((nw,))],
        name=name,
    )(*shards)


def _exchange_halves(grads, name):
    nw = len(grads)
    out_shapes = [jax.ShapeDtypeStruct((g.shape[0] // 2,) + g.shape[1:], g.dtype) for g in grads]

    def body(*refs):
        ins, outs = refs[:nw], refs[nw : 2 * nw]
        send_sems, recv_sems = refs[2 * nw :]
        x, y, c = _mesh_pos()
        cps = []
        for w in range(nw):
            h = grads[w].shape[0] // 2
            cp = pltpu.make_async_remote_copy(
                src_ref=ins[w].at[pl.ds((1 - c) * h, h)], dst_ref=outs[w], send_sem=send_sems.at[w],
                recv_sem=recv_sems.at[w], device_id=(x, y, 1 - c), device_id_type=MESH_ID)
            cp.start()
            cps.append(cp)
        for cp in cps:
            cp.wait()

    anyspec = pl.BlockSpec(memory_space=pl.ANY)
    return pl.pallas_call(
        body,
        out_shape=out_shapes,
        in_specs=[anyspec] * nw,
        out_specs=[anyspec] * nw,
        scratch_shapes=[pltpu.SemaphoreType.DMA((nw,)), pltpu.SemaphoreType.DMA((nw,))],
        name=name,
    )(*grads)


def _scatter_to_chips(parts, kinds, name):
    nw = len(parts)
    shard_shapes = []
    for p, kind in zip(parts, kinds):
        h, r, c = p.shape
        shard_shapes.append((h, r // 4, c) if kind == "row" else (h, r, c // 4))
    out_shapes = [jax.ShapeDtypeStruct((4,) + s, F32) for s in shard_shapes]

    def body(*refs):
        ins, outs = refs[:nw], refs[nw : 2 * nw]
        send_sems, recv_sems, local_sems = refs[2 * nw :]
        x, y, c = _mesh_pos()
        chips = _other_chips(x, y)
        my_chip = 2 * x + y

        def piece(w, chip):
            h, r, cc = shard_shapes[w]
            if kinds[w] == "row":
                return _shard_window(ins[w], pl.ds(0, h), chip, r, None)
            return _shard_window(ins[w], pl.ds(0, h), chip, None, cc)

        local, sent = [], []
        for w in range(nw):
            cp = pltpu.make_async_copy(piece(w, my_chip), outs[w].at[my_chip], local_sems.at[w])
            cp.start()
            local.append(cp)
            for j, (cx, cy) in enumerate(chips):
                cp = pltpu.make_async_remote_copy(
                    src_ref=piece(w, 2 * cx + cy), dst_ref=outs[w].at[my_chip], send_sem=send_sems.at[w, j],
                    recv_sem=recv_sems.at[w, j], device_id=(cx, cy, c), device_id_type=MESH_ID)
                cp.start()
                sent.append(cp)
        for w in range(nw):
            for j, (cx, cy) in enumerate(chips):
                slot = outs[w].at[2 * cx + cy]
                pltpu.make_async_remote_copy(
                    src_ref=slot, dst_ref=slot, send_sem=send_sems.at[w, j], recv_sem=recv_sems.at[w, j],
                    device_id=(cx, cy, c), device_id_type=MESH_ID).wait_recv()
        for cp in sent:
            cp.wait_send()
        for cp in local:
            cp.wait()

    anyspec = pl.BlockSpec(memory_space=pl.ANY)
    return pl.pallas_call(
        body,
        out_shape=out_shapes,
        in_specs=[anyspec] * nw,
        out_specs=[anyspec] * nw,
        scratch_shapes=[pltpu.SemaphoreType.DMA((nw, 3)), pltpu.SemaphoreType.DMA((nw, 3)), pltpu.SemaphoreType.DMA((nw,))],
        name=name,
    )(*parts)


def _join_halves(halves, name):
    nw = len(halves)
    out_shapes = [jax.ShapeDtypeStruct((2 * p.shape[0],) + p.shape[1:], p.dtype) for p in halves]

    def body(*refs):
        ins, outs = refs[:nw], refs[nw : 2 * nw]
        send_sems, recv_sems, local_sems = refs[2 * nw :]
        x, y, c = _mesh_pos()
        local, sent = [], []
        for w in range(nw):
            h = halves[w].shape[0]
            mine = outs[w].at[pl.ds(c * h, h)]
            cp = pltpu.make_async_copy(ins[w], mine, local_sems.at[w])
            cp.start()
            local.append(cp)
            cp = pltpu.make_async_remote_copy(
                src_ref=ins[w], dst_ref=mine, send_sem=send_sems.at[w], recv_sem=recv_sems.at[w],
                device_id=(x, y, 1 - c), device_id_type=MESH_ID)
            cp.start()
            sent.append(cp)
        for w in range(nw):
            h = halves[w].shape[0]
            theirs = outs[w].at[pl.ds((1 - c) * h, h)]
            pltpu.make_async_remote_copy(
                src_ref=theirs, dst_ref=theirs, send_sem=send_sems.at[w], recv_sem=recv_sems.at[w],
                device_id=(x, y, 1 - c), device_id_type=MESH_ID).wait_recv()
        for cp in sent:
            cp.wait_send()
        for cp in local:
            cp.wait()

    anyspec = pl.BlockSpec(memory_space=pl.ANY)
    return pl.pallas_call(
        body,
        out_shape=out_shapes,
        in_specs=[anyspec] * nw,
        out_specs=[anyspec] * nw,
        scratch_shapes=[pltpu.SemaphoreType.DMA((nw,)), pltpu.SemaphoreType.DMA((nw,)), pltpu.SemaphoreType.DMA((nw,))],
        name=name,
    )(*halves)


def _pad_wdq(w):
    z = lambda n: jnp.zeros((w.shape[0], n), w.dtype)
    base = Q_LORA + KV_LORA
    return jnp.concatenate([w[:, :base], z(KPE_LANE), w[:, base:], z(HEAD_PAD - KPE_LANE - QK_ROPE)], axis=1)


def _unpad_wdq(g):
    base = Q_LORA + KV_LORA
    return jnp.concatenate([g[:, :base], g[:, base + KPE_LANE : base + KPE_LANE + QK_ROPE]], axis=1)


def _pad_wuq(w):
    w3 = w.reshape(Q_LORA, MLA_HEADS, QK_NOPE + QK_ROPE)
    w3 = jnp.pad(w3, ((0, 0), (0, 0), (0, HEAD_PAD - QK_NOPE - QK_ROPE)))
    return w3.reshape(Q_LORA, MLA_HEADS * HEAD_PAD)


def _unpad_wuq(g):
    g3 = g.reshape(Q_LORA, MLA_HEADS, HEAD_PAD)[:, :, : QK_NOPE + QK_ROPE]
    return g3.reshape(Q_LORA, MLA_HEADS * (QK_NOPE + QK_ROPE))


def _pad_wo(w):
    w3 = w.reshape(MLA_HEADS, V_HEAD, D_MODEL)
    w3 = jnp.pad(w3, ((0, 0), (HEAD_PAD - V_HEAD, 0), (0, 0)))
    return w3.reshape(MLA_HEADS * HEAD_PAD, D_MODEL)


def _unpad_wo(g):
    g3 = g.reshape(MLA_HEADS, HEAD_PAD, D_MODEL)[:, HEAD_PAD - V_HEAD :, :]
    return g3.reshape(MLA_HEADS * V_HEAD, D_MODEL)


def _local_step(x, mem, positions, target, wb, ws):
    t = x.shape[0]
    tables = _rope_tables(positions.reshape(t, 1), "rope_tables")
    saved = []
    for l in range(DEPTH):
        s = {"x0": x}
        h1 = _rms_fwd(x, ws["norm_mix_g"][l], f"l{l}_norm_mix")
        s["h1"] = h1
        if l % 2 == 0:
            e = l // 2
            z = _matmul(h1, wb["pc_w_in"], "nn", F32, f"l{l}_pc_in", layer=e)
            cat = _mix_fwd(z, ws["pool_w"][e], ws["pool_scale"][e], ws["conv_dw_w"][e], ws["conv_dw_b"][e],
                           ws["conv_ln_g"][e], ws["conv_ln_b"][e], f"l{l}_mix")
            x = _matmul(cat, wb["pc_w_out"], "nn", F32, f"l{l}_pc_out", layer=e, res=x)
            s.update(z=z, cat=cat)
        else:
            o = l // 2
            cp = _matmul(h1, wb["mla_wdq"], "nn", F32, f"l{l}_mla_dq", layer=o)
            qn, kvn, kpe = _mla_prep(cp, ws["mla_q_norm_g"][o], ws["mla_kv_norm_g"][o], tables, f"l{l}_mla_prep")
            q = _matmul(qn, wb["mla_wuq"], "nn", F32, f"l{l}_mla_uq", layer=o)
            qr = _rope_heads(q, tables, 1.0, f"l{l}_mla_rope")
            kv = _matmul(kvn, wb["mla_w_ukv"], "nn", MXU_DT, f"l{l}_mla_ukv", layer=o)
            att, lse = _flash_fwd(qr, kv, kpe, f"l{l}_mla_attn")
            x = _matmul(att, wb["mla_wo"], "nn", F32, f"l{l}_mla_o", layer=o, res=x)
            s.update(cp=cp, qn=qn, kvn=kvn, kpe=kpe, qr=qr, kv=kv, att=att, lse=lse)
        s["x1"] = x
        h2 = _rms_fwd(x, ws["norm_xa_g"][l], f"l{l}_norm_xa")
        hm = _rms_fwd(mem, ws["norm_mem_g"][l], f"l{l}_norm_mem")
        q2 = _matmul(h2, wb["xa_wq"], "nn", MXU_DT, f"l{l}_xa_q", layer=l)
        kvm = _matmul(hm, wb["xa_wkv"], "nn", MXU_DT, f"l{l}_xa_kv", layer=l)
        o2 = _xa_fwd(q2, kvm, f"l{l}_xa_attn")
        x = _matmul(o2, wb["xa_wo"], "nn", F32, f"l{l}_xa_o", layer=l, res=x)
        s.update(h2=h2, hm=hm, q2=q2, kvm=kvm, o2=o2, x2=x)
        h3 = _rms_fwd(x, ws["norm_ffn_g"][l], f"l{l}_norm_ffn")
        up = _matmul(h3, wb["ffn_w_up"], "nn", F32, f"l{l}_ffn_up", layer=l)
        act = _ffn_fwd(up, ws["ffn_conv_w"][l], ws["ffn_conv_b"][l], f"l{l}_ffn_mid")
        x = _matmul(act, wb["ffn_w_down"], "nn", F32, f"l{l}_ffn_down", layer=l, res=x)
        s.update(h3=h3, up=up, act=act)
        saved.append(s)

    dx, dg_final, loss = _loss_head(x, target, ws["final_norm_g"], "loss_head")
    g = {k: [None] * DEPTH for k in ("norm_mix_g", "norm_xa_g", "norm_mem_g", "xa_wq", "xa_wkv", "xa_wo", "norm_ffn_g",
                                      "ffn_w_up", "ffn_conv_w", "ffn_conv_b", "ffn_w_down")}
    g.update({k: [None] * (DEPTH // 2) for k in ("pc_w_in", "pool_w", "pool_scale", "conv_dw_w", "conv_dw_b", "conv_ln_g",
                                                 "conv_ln_b", "pc_w_out", "mla_w_dq_dkv", "mla_q_norm_g", "mla_w_uq",
                                                 "mla_kv_norm_g", "mla_w_ukv", "mla_w_o")})
    for l in reversed(range(DEPTH)):
        s = saved[l]
        dact = _matmul(dx, wb["ffn_w_down"], "nt", F32, f"l{l}_b_ffn_dact", layer=l)
        g["ffn_w_down"][l] = _matmul(s["act"], dx, "tn", F32, f"l{l}_b_ffn_dwdown")
        dup, dcw, dcb = _ffn_bwd(s["up"], dact, ws["ffn_conv_w"][l], ws["ffn_conv_b"][l], f"l{l}_b_ffn_mid")
        g["ffn_conv_w"][l], g["ffn_conv_b"][l] = dcw, dcb[0]
        g["ffn_w_up"][l] = _matmul(s["h3"], dup, "tn", F32, f"l{l}_b_ffn_dwup")
        dh = _matmul(dup, wb["ffn_w_up"], "nt", F32, f"l{l}_b_ffn_dh", layer=l)
        dx, dg = _rms_bwd(dh, s["x2"], ws["norm_ffn_g"][l], dx, f"l{l}_b_norm_ffn")
        g["norm_ffn_g"][l] = dg[0]
        do2 = _matmul(dx, wb["xa_wo"], "nt", MXU_DT, f"l{l}_b_xa_do", layer=l)
        g["xa_wo"][l] = _matmul(s["o2"], dx, "tn", F32, f"l{l}_b_xa_dwo")
        dq2, dkvm = _xa_bwd(s["q2"], s["kvm"], do2, f"l{l}_b_xa_attn")
        g["xa_wq"][l] = _matmul(s["h2"], dq2, "tn", F32, f"l{l}_b_xa_dwq")
        dh = _matmul(dq2, wb["xa_wq"], "nt", F32, f"l{l}_b_xa_dh", layer=l)
        g["xa_wkv"][l] = _matmul(s["hm"], dkvm, "tn", F32, f"l{l}_b_xa_dwkv")
        dhm = _matmul(dkvm, wb["xa_wkv"], "nt", F32, f"l{l}_b_xa_dhm", layer=l)
        g["norm_mem_g"][l] = _rms_bwd_gain(dhm, mem, ws["norm_mem_g"][l], f"l{l}_b_norm_mem")[0]
        dx, dg = _rms_bwd(dh, s["x1"], ws["norm_xa_g"][l], dx, f"l{l}_b_norm_xa")
        g["norm_xa_g"][l] = dg[0]
        if l % 2 == 0:
            e = l // 2
            dcat = _matmul(dx, wb["pc_w_out"], "nt", F32, f"l{l}_b_pc_dcat", layer=e)
            g["pc_w_out"][e] = _matmul(s["cat"], dx, "tn", F32, f"l{l}_b_pc_dwout")
            dz, dpw, dps, dww, dwb, dlg, dlb = _mix_bwd(
                s["z"], dcat, ws["pool_w"][e], ws["pool_scale"][e], ws["conv_dw_w"][e], ws["conv_dw_b"][e],
                ws["conv_ln_g"][e], ws["conv_ln_b"][e], f"l{l}_b_mix")
            g["pool_w"][e], g["pool_scale"][e], g["conv_dw_w"][e] = dpw, dps[0], dww
            g["conv_dw_b"][e], g["conv_ln_g"][e], g["conv_ln_b"][e] = dwb[0], dlg[0], dlb[0]
            g["pc_w_in"][e] = _matmul(s["h1"], dz, "tn", F32, f"l{l}_b_pc_dwin")
            dh = _matmul(dz, wb["pc_w_in"], "nt", F32, f"l{l}_b_pc_dh", layer=e)
        else:
            o = l // 2
            do = _matmul(dx, wb["mla_wo"], "nt", MXU_DT, f"l{l}_b_mla_do", layer=o)
            g["mla_w_o"][o] = _unpad_wo(_matmul(s["att"], dx, "tn", F32, f"l{l}_b_mla_dwo"))
            dqr, dkv, dkpe = _flash_bwd(s["qr"], s["kv"], s["kpe"], s["att"], do, s["lse"], f"l{l}_b_mla_attn")
            dq = _rope_heads(dqr, tables, -1.0, f"l{l}_b_mla_rope")
            g["mla_w_uq"][o] = _unpad_wuq(_matmul(s["qn"], dq, "tn", F32, f"l{l}_b_mla_dwuq"))
            dqn = _matmul(dq, wb["mla_wuq"], "nt", F32, f"l{l}_b_mla_dqn", layer=o)
            g["mla_w_ukv"][o] = _matmul(s["kvn"], dkv, "tn", F32, f"l{l}_b_mla_dwukv")
            dkvn = _matmul(dkv, wb["mla_w_ukv"], "nt", F32, f"l{l}_b_mla_dkvn", layer=o)
            dcp, dqg, dkvg = _mla_prep_bwd(s["cp"], dqn, dkvn, dkpe, ws["mla_q_norm_g"][o], ws["mla_kv_norm_g"][o],
                                           tables, f"l{l}_b_mla_prep")
            g["mla_q_norm_g"][o], g["mla_kv_norm_g"][o] = dqg[0], dkvg[0]
            g["mla_w_dq_dkv"][o] = _unpad_wdq(_matmul(s["h1"], dcp, "tn", F32, f"l{l}_b_mla_dwdq"))
            dh = _matmul(dcp, wb["mla_wdq"], "nt", F32, f"l{l}_b_mla_dh", layer=o)
        dx, dg = _rms_bwd(dh, s["x0"], ws["norm_mix_g"][l], dx, f"l{l}_b_norm_mix")
        g["norm_mix_g"][l] = dg[0]
    grads = {k: jnp.stack(v) for k, v in g.items()}
    grads["final_norm_g"] = dg_final[0]
    return loss, dx, grads


BIG = (
    ("xa_wq", "row"), ("xa_wkv", "col"), ("xa_wo", "row"), ("ffn_w_up", "col"), ("ffn_w_down", "row"),
    ("pc_w_in", "col"), ("pc_w_out", "row"), ("mla_w_dq_dkv", "row"), ("mla_w_uq", "col"), ("mla_w_ukv", "col"),
    ("mla_w_o", "row"),
)
SMALL_SHARDED = ("ffn_conv_w", "conv_dw_w", "mla_q_norm_g", "mla_kv_norm_g")
SMALL_REPLICATED = ("norm_mix_g", "norm_xa_g", "norm_mem_g", "norm_ffn_g", "ffn_conv_b", "pool_w", "pool_scale",
                    "conv_dw_b", "conv_ln_g", "conv_ln_b", "final_norm_g")
WEIGHTS = ("norm_mix_g", "norm_xa_g", "norm_mem_g", "xa_wq", "xa_wkv", "xa_wo", "norm_ffn_g", "ffn_w_up", "ffn_conv_w",
           "ffn_conv_b", "ffn_w_down", "pc_w_in", "pool_w", "pool_scale", "conv_dw_w", "conv_dw_b", "conv_ln_g",
           "conv_ln_b", "pc_w_out", "mla_w_dq_dkv", "mla_q_norm_g", "mla_w_uq", "mla_kv_norm_g", "mla_w_ukv", "mla_w_o",
           "final_norm_g")
PACK_ROW = SUBLANE * LANE


def _pack(arrays):
    flat = jnp.concatenate([a.reshape(-1) for a in arrays])
    n = flat.shape[0]
    pad = (-n) % PACK_ROW
    return jnp.pad(flat, (0, pad)).reshape(-1, LANE)


def _unpack(flat, shapes):
    out, off = [], 0
    for s in shapes:
        n = int(np.prod(s))
        out.append(flat[off : off + n].reshape(s))
        off += n
    return out


def kernel(x, mem, positions, norm_mix_g, norm_xa_g, norm_mem_g, xa_wq, xa_wkv, xa_wo, norm_ffn_g, ffn_w_up, ffn_conv_w, ffn_conv_b, ffn_w_down, pc_w_in, pool_w, pool_scale, conv_dw_w, conv_dw_b, conv_ln_g, conv_ln_b, pc_w_out, mla_w_dq_dkv, mla_q_norm_g, mla_w_uq, mla_kv_norm_g, mla_w_ukv, mla_w_o, final_norm_g, loss_target, m_norm_mix_g, m_norm_xa_g, m_norm_mem_g, m_xa_wq, m_xa_wkv, m_xa_wo, m_norm_ffn_g, m_ffn_w_up, m_ffn_conv_w, m_ffn_conv_b, m_ffn_w_down, m_pc_w_in, m_pool_w, m_pool_scale, m_conv_dw_w, m_conv_dw_b, m_conv_ln_g, m_conv_ln_b, m_pc_w_out, m_mla_w_dq_dkv, m_mla_q_norm_g, m_mla_w_uq, m_mla_kv_norm_g, m_mla_w_ukv, m_mla_w_o, m_final_norm_g, v_norm_mix_g, v_norm_xa_g, v_norm_mem_g, v_xa_wq, v_xa_wkv, v_xa_wo, v_norm_ffn_g, v_ffn_w_up, v_ffn_conv_w, v_ffn_conv_b, v_ffn_w_down, v_pc_w_in, v_pool_w, v_pool_scale, v_conv_dw_w, v_conv_dw_b, v_conv_ln_g, v_conv_ln_b, v_pc_w_out, v_mla_w_dq_dkv, v_mla_q_norm_g, v_mla_w_uq, v_mla_kv_norm_g, v_mla_w_ukv, v_mla_w_o, v_final_norm_g):
    args = dict(locals())
    w = {n: args[n] for n in WEIGHTS}
    m = {n: args["m_" + n] for n in WEIGHTS}
    v = {n: args["v_" + n] for n in WEIGHTS}
    cx, cy, cc = lax.axis_index("x"), lax.axis_index("y"), lax.axis_index("c")
    chip = 2 * cx + cy

    full = _all_gather_weights([w[n].astype(MXU_DT) for n, _ in BIG], [k for _, k in BIG], "gather_weights")
    full = dict(zip([n for n, _ in BIG], full))
    small_shapes = [w[n].shape for n in SMALL_SHARDED]
    gathered = _all_gather_rows(_pack([w[n] for n in SMALL_SHARDED]), "gather_small")
    gathered = gathered.reshape(8, -1)
    ws = {n: w[n] for n in SMALL_REPLICATED}
    pieces = [_unpack(gathered[2 * k], small_shapes) for k in range(4)]
    for i, n in enumerate(SMALL_SHARDED):
        ws[n] = jnp.concatenate([pieces[k][i] for k in range(4)], axis=-1)
    wb = {n: full[n] for n in ("xa_wq", "xa_wkv", "xa_wo", "ffn_w_up", "ffn_w_down", "pc_w_in", "pc_w_out", "mla_w_ukv")}
    wb["mla_wdq"] = jnp.stack([_pad_wdq(full["mla_w_dq_dkv"][o]) for o in range(DEPTH // 2)])
    wb["mla_wuq"] = jnp.stack([_pad_wuq(full["mla_w_uq"][o]) for o in range(DEPTH // 2)])
    wb["mla_wo"] = jnp.stack([_pad_wo(full["mla_w_o"][o]) for o in range(DEPTH // 2)])

    loss, grad_x, grads = _local_step(x[0], mem[0], positions[0], loss_target[0], wb, ws)
    loss = lax.psum(loss[0, 0], ("x", "y", "c"))

    kinds = [k for _, k in BIG]
    big = [grads[n] for n, _ in BIG]
    c_idx = cc.reshape(1).astype(jnp.int32)
    theirs = _exchange_halves(big, "reduce_pair")
    pair = [_add_half(gr, th, c_idx, f"reduce_pair_add_{n}") for gr, th, (n, _) in zip(big, theirs, BIG)]
    slots = _scatter_to_chips(pair, kinds, "reduce_chips")
    halves = []
    for sl, (n, _) in zip(slots, BIG):
        k4, h, r, c = sl.shape
        halves.append(_sum_leading(sl.reshape(k4, h * r, c), f"reduce_chips_add_{n}").reshape(h, r, c))
    gsum = dict(zip([n for n, _ in BIG], _join_halves(halves, "reduce_join")))

    small_names = SMALL_REPLICATED + SMALL_SHARDED
    small_grad_shapes = [grads[n].shape for n in small_names]
    packed = _pack([grads[n] for n in small_names])
    rows = packed.shape[0]
    allparts = _all_gather_rows(packed, "gather_small_grads").reshape(8, rows, LANE)
    total = _sum_leading(allparts, "sum_small_grads").reshape(-1)
    for n, gfull in zip(small_names, _unpack(total, small_grad_shapes)):
        if n in SMALL_SHARDED:
            width = w[n].shape[-1]
            gfull = lax.dynamic_slice_in_dim(gfull, chip * width, width, axis=gfull.ndim - 1)
        gsum[n] = gfull

    delta, new_m, new_v = {}, {}, {}
    for n in WEIGHTS:
        delta[n], new_m[n], new_v[n] = _adamw(w[n], gsum[n], m[n], v[n], f"adamw_{n}")
    return (loss, grad_x[None], *[gsum[n] for n in WEIGHTS], *[delta[n] for n in WEIGHTS],
            *[new_m[n] for n in WEIGHTS], *[new_v[n] for n in WEIGHTS])
```

```python
import functools
import math

import numpy as np
import jax
import jax.numpy as jnp
from jax import lax
from jax.experimental import pallas as pl
from jax.experimental.pallas import tpu as pltpu

F32 = jnp.float32
MXU_DT = jnp.bfloat16
XFER_DT = jnp.bfloat16

D_MODEL = 1024
DEPTH = 4
MEM_LEN = 256
XA_HEADS = 4
XA_HEAD_DIM = 256
POOL_W = 512
POOL_WINDOWS = (2, 4, 8, 16)
POOL_GROUP = 128
CONV_W = 512
CONV_K = 31
MLA_HEADS = 16
QK_NOPE = 64
QK_ROPE = 32
V_HEAD = 64
Q_LORA = 384
KV_LORA = 256
ROPE_THETA = 10000.0
MLA_SCALE = 1.0 / math.sqrt(QK_NOPE + QK_ROPE)
LOG2E = math.log2(math.e)
D_FF = 2816
FFN_CONV_K = 3
EPS = 1e-6
NEG = -1e30
ADAM_LR = 0.001
ADAM_B1 = 0.9
ADAM_B2 = 0.999
ADAM_EPS = 1e-08
ADAM_WD = 0.01
ADAM_STEP = 10

HEAD_PAD = 128
C_PAD = 768
KPE_LANE = 64

VMEM_LIMIT = 52 * 1024 * 1024
BLOCK_BYTES = 6 * 1024 * 1024
LANE = 128
SUBLANE = 8

TM = 1024
TN = 1408
TK = 1024
TT = 512
TW = 256
TA = 512
MIX_HALO = 32
FFN_HALO = 8

NN = (((1,), (0,)), ((), ()))
NT = (((1,), (1,)), ((), ()))
TN_DIMS = (((0,), (0,)), ((), ()))
MESH_ID = pl.DeviceIdType.MESH


def _cparams(*sem):
    return pltpu.CompilerParams(dimension_semantics=sem, vmem_limit_bytes=VMEM_LIMIT)


def _tile(n, pref, limit=None):
    cap = pref if limit is None else min(pref, limit)
    if n <= cap:
        return n
    t = (cap // LANE) * LANE
    while t >= LANE:
        if n % t == 0:
            return t
        t -= LANE
    return n


def _rows(t, pref):
    return t if t <= pref else pref


def _sigmoid(x):
    return 1.0 / (1.0 + jnp.exp(-x))


def _matmul(a, b, mode, out_dtype, name, layer=None, res=None):
    if layer is None:
        b2 = b.shape
    else:
        b2 = b.shape[1:]
    if mode == "tn":
        k, m = a.shape
        k2, n = b2
    elif mode == "nn":
        m, k = a.shape
        k2, n = b2
    else:
        m, k = a.shape
        n, k2 = b2
    assert k == k2, (a.shape, b.shape, mode)
    isz_a = jnp.dtype(a.dtype).itemsize
    isz_b = jnp.dtype(b.dtype).itemsize
    if mode == "tn":
        tk = _tile(k, TK)
        tm = _tile(m, TN, BLOCK_BYTES // (tk * isz_a))
        tn = _tile(n, TN, BLOCK_BYTES // (tk * isz_b))
    else:
        tk = k
        tn = _tile(n, TN, BLOCK_BYTES // (tk * isz_b))
        tm = _tile(m, TM, min(BLOCK_BYTES // (tk * isz_a), BLOCK_BYTES // (tn * 4)))
    nk = k // tk
    grid = (m // tm, n // tn, nk)
    if mode == "nn":
        a_spec = pl.BlockSpec((tm, tk), lambda i, j, kk: (i, kk))
        b_blk, b_map, dn = (tk, tn), (lambda i, j, kk: (kk, j)), NN
    elif mode == "nt":
        a_spec = pl.BlockSpec((tm, tk), lambda i, j, kk: (i, kk))
        b_blk, b_map, dn = (tn, tk), (lambda i, j, kk: (j, kk)), NT
    else:
        a_spec = pl.BlockSpec((tk, tm), lambda i, j, kk: (kk, i))
        b_blk, b_map, dn = (tk, tn), (lambda i, j, kk: (kk, j)), TN_DIMS
    if layer is None:
        b_spec = pl.BlockSpec(b_blk, b_map)
    else:
        b_spec = pl.BlockSpec((None,) + b_blk, lambda i, j, kk: (layer,) + b_map(i, j, kk))
    o_spec = pl.BlockSpec((tm, tn), lambda i, j, kk: (i, j))
    in_specs = [a_spec, b_spec]
    args = [a, b]
    if res is not None:
        in_specs.append(pl.BlockSpec((tm, tn), lambda i, j, kk: (i, j)))
        args.append(res)
    has_res = res is not None

    def body(*refs):
        a_ref, b_ref = refs[0], refs[1]
        r_ref = refs[2] if has_res else None
        o_ref = refs[3] if has_res else refs[2]
        p = lax.dot_general(a_ref[...].astype(MXU_DT), b_ref[...].astype(MXU_DT), dn, preferred_element_type=F32)
        if nk == 1:
            if has_res:
                p = r_ref[...] + p
            o_ref[...] = p.astype(o_ref.dtype)
        else:
            acc_ref = refs[-1]
            kk = pl.program_id(2)

            @pl.when(kk == 0)
            def _():
                acc_ref[...] = jnp.zeros_like(acc_ref)

            acc_ref[...] += p

            @pl.when(kk == nk - 1)
            def _():
                r = acc_ref[...]
                if has_res:
                    r = r_ref[...] + r
                o_ref[...] = r.astype(o_ref.dtype)

    scratch = [pltpu.VMEM((tm, tn), F32)] if nk > 1 else []
    return pl.pallas_call(
        body,
        grid=grid,
        in_specs=in_specs,
        out_specs=o_spec,
        out_shape=jax.ShapeDtypeStruct((m, n), out_dtype),
        scratch_shapes=scratch,
        name=name,
        compiler_params=_cparams("parallel", "parallel", "arbitrary"),
    )(*args)


def _rms_fwd(x, g, name):
    t, d = x.shape
    tt = _rows(t, TT)

    def body(x_ref, g_ref, o_ref):
        xf = x_ref[...]
        r = lax.rsqrt(jnp.mean(xf * xf, axis=-1, keepdims=True) + EPS)
        o_ref[...] = ((xf * r) * g_ref[...]).astype(o_ref.dtype)

    return pl.pallas_call(
        body,
        grid=(t // tt,),
        in_specs=[pl.BlockSpec((tt, d), lambda i: (i, 0)), pl.BlockSpec((1, d), lambda i: (0, 0))],
        out_specs=pl.BlockSpec((tt, d), lambda i: (i, 0)),
        out_shape=jax.ShapeDtypeStruct((t, d), MXU_DT),
        name=name,
        compiler_params=_cparams("parallel"),
    )(x, g.reshape(1, d))


def _rms_bwd(dh, x, g, dx_in, name):
    t, d = x.shape
    tt = _rows(t, TT)

    def body(dh_ref, x_ref, g_ref, dxi_ref, dx_ref, dg_ref):
        @pl.when(pl.program_id(0) == 0)
        def _():
            dg_ref[...] = jnp.zeros_like(dg_ref)

        xf = x_ref[...]
        dh_v = dh_ref[...]
        r = lax.rsqrt(jnp.mean(xf * xf, axis=-1, keepdims=True) + EPS)
        xh = xf * r
        gy = dh_v * g_ref[...]
        dx = r * (gy - xh * jnp.mean(gy * xh, axis=-1, keepdims=True))
        dx_ref[...] = dxi_ref[...] + dx
        dg_ref[...] += jnp.sum(dh_v * xh, axis=0, keepdims=True)

    row = pl.BlockSpec((tt, d), lambda i: (i, 0))
    vec = pl.BlockSpec((1, d), lambda i: (0, 0))
    return pl.pallas_call(
        body,
        grid=(t // tt,),
        in_specs=[row, row, vec, row],
        out_specs=[row, vec],
        out_shape=[jax.ShapeDtypeStruct((t, d), F32), jax.ShapeDtypeStruct((1, d), F32)],
        name=name,
        compiler_params=_cparams("arbitrary"),
    )(dh, x, g.reshape(1, d), dx_in)


def _rms_bwd_gain(dh, x, g, name):
    t, d = x.shape
    tt = _rows(t, TT)

    def body(dh_ref, x_ref, dg_ref):
        @pl.when(pl.program_id(0) == 0)
        def _():
            dg_ref[...] = jnp.zeros_like(dg_ref)

        xf = x_ref[...]
        r = lax.rsqrt(jnp.mean(xf * xf, axis=-1, keepdims=True) + EPS)
        dg_ref[...] += jnp.sum(dh_ref[...] * (xf * r), axis=0, keepdims=True)

    row = pl.BlockSpec((tt, d), lambda i: (i, 0))
    vec = pl.BlockSpec((1, d), lambda i: (0, 0))
    return pl.pallas_call(
        body,
        grid=(t // tt,),
        in_specs=[row, row],
        out_specs=vec,
        out_shape=jax.ShapeDtypeStruct((1, d), F32),
        name=name,
        compiler_params=_cparams("arbitrary"),
    )(dh, x)


def _loss_head(x, target, g, name):
    t, d = x.shape
    tt = _rows(t, TT)

    def body(x_ref, t_ref, g_ref, dx_ref, dg_ref, loss_ref):
        @pl.when(pl.program_id(0) == 0)
        def _():
            dg_ref[...] = jnp.zeros_like(dg_ref)
            loss_ref[...] = jnp.zeros_like(loss_ref)

        xf = x_ref[...]
        gv = g_ref[...]
        r = lax.rsqrt(jnp.mean(xf * xf, axis=-1, keepdims=True) + EPS)
        xh = xf * r
        err = xh * gv - t_ref[...]
        e2 = jnp.sum(err * err, axis=-1, keepdims=True)
        loss_ref[...] += (0.5 / d) * jnp.sum(e2, axis=0, keepdims=True)
        dy = err * (1.0 / d)
        gy = dy * gv
        dx_ref[...] = r * (gy - xh * jnp.mean(gy * xh, axis=-1, keepdims=True))
        dg_ref[...] += jnp.sum(dy * xh, axis=0, keepdims=True)

    row = pl.BlockSpec((tt, d), lambda i: (i, 0))
    vec = pl.BlockSpec((1, d), lambda i: (0, 0))
    return pl.pallas_call(
        body,
        grid=(t // tt,),
        in_specs=[row, row, vec],
        out_specs=[row, vec, pl.BlockSpec((1, 1), lambda i: (0, 0))],
        out_shape=[
            jax.ShapeDtypeStruct((t, d), F32),
            jax.ShapeDtypeStruct((1, d), F32),
            jax.ShapeDtypeStruct((1, 1), F32),
        ],
        name=name,
        compiler_params=_cparams("arbitrary"),
    )(x, target, g.reshape(1, d))


def _prev_halo(tt, hp, width):
    return pl.BlockSpec((hp, width), lambda i: (jnp.maximum(i * (tt // hp) - 1, 0), 0))


def _next_halo(tt, hp, width, t):
    return pl.BlockSpec((hp, width), lambda i: (jnp.minimum((i + 1) * (tt // hp), t // hp - 1), 0))


def _ffn_chunks():
    half = D_FF // 2
    return [(c * half, half) for c in range(2)]


def _ffn_fwd(up, conv_w, conv_b, name):
    t = up.shape[0]
    tt = _rows(t, TW)
    hp = FFN_HALO

    def body(up_ref, gp_ref, w_ref, b_ref, act_ref, ext_ref):
        first = pl.program_id(0) == 0
        for c0, cw in _ffn_chunks():
            ga = pl.ds(D_FF + c0, cw)
            ext_ref[0:hp, :] = jnp.where(first, 0.0, gp_ref[:, ga])
            ext_ref[hp : hp + tt, :] = up_ref[:, ga]
            gc = b_ref[:, pl.ds(c0, cw)]
            for j in range(FFN_CONV_K):
                off = hp - (FFN_CONV_K - 1) + j
                gc = gc + w_ref[j : j + 1, pl.ds(c0, cw)] * ext_ref[off : off + tt, :]
            a = up_ref[:, pl.ds(c0, cw)]
            act_ref[:, pl.ds(c0, cw)] = (gc * _sigmoid(gc) * a).astype(act_ref.dtype)

    return pl.pallas_call(
        body,
        grid=(t // tt,),
        in_specs=[
            pl.BlockSpec((tt, 2 * D_FF), lambda i: (i, 0)),
            _prev_halo(tt, hp, 2 * D_FF),
            pl.BlockSpec((FFN_CONV_K, D_FF), lambda i: (0, 0)),
            pl.BlockSpec((1, D_FF), lambda i: (0, 0)),
        ],
        out_specs=pl.BlockSpec((tt, D_FF), lambda i: (i, 0)),
        out_shape=jax.ShapeDtypeStruct((t, D_FF), MXU_DT),
        scratch_shapes=[pltpu.VMEM((tt + hp, D_FF // 2), F32)],
        name=name,
        compiler_params=_cparams("parallel"),
    )(up, up, conv_w, conv_b.reshape(1, D_FF))


def _ffn_bwd(up, dact, conv_w, conv_b, name):
    t = up.shape[0]
    tt = _rows(t, TW)
    hp = FFN_HALO
    nt = t // tt
    kk = FFN_CONV_K

    def body(up_ref, upp_ref, upn_ref, da_ref, dan_ref, w_ref, b_ref, dup_ref, dw_ref, db_ref, ext_ref, dgc_ref):
        i = pl.program_id(0)
        first = i == 0
        last = i == nt - 1

        @pl.when(first)
        def _():
            dw_ref[...] = jnp.zeros_like(dw_ref)
            db_ref[...] = jnp.zeros_like(db_ref)

        for c0, cw in _ffn_chunks():
            ca = pl.ds(c0, cw)
            ga = pl.ds(D_FF + c0, cw)
            ext_ref[0:hp, :] = jnp.where(first, 0.0, upp_ref[:, ga])
            ext_ref[hp : hp + tt, :] = up_ref[:, ga]
            ext_ref[hp + tt : hp + tt + hp, :] = upn_ref[:, ga]
            gc = b_ref[:, ca]
            for j in range(kk):
                off = hp - (kk - 1) + j
                gc = gc + w_ref[j : j + 1, ca] * ext_ref[off : off + tt + hp, :]
            sg = _sigmoid(gc)
            silu = gc * sg
            dsilu = sg * (1.0 + gc * (1.0 - sg))
            a_all = jnp.concatenate([up_ref[:, ca], upn_ref[:, ca]], axis=0)
            dact_all = jnp.concatenate([da_ref[:, ca], jnp.where(last, 0.0, dan_ref[:, ca])], axis=0)
            dgc = dact_all * a_all * dsilu
            dgc_ref[...] = dgc
            dup_ref[:, ca] = (dact_all[0:tt] * silu[0:tt]).astype(dup_ref.dtype)
            dg = jnp.zeros((tt, cw), F32)
            for j in range(kk):
                dg = dg + w_ref[j : j + 1, ca] * dgc_ref[kk - 1 - j : kk - 1 - j + tt, :]
            dup_ref[:, ga] = dg.astype(dup_ref.dtype)
            dgc_t = dgc[0:tt]
            db_ref[:, ca] += jnp.sum(dgc_t, axis=0, keepdims=True)
            for j in range(kk):
                off = hp - (kk - 1) + j
                dw_ref[j : j + 1, ca] += jnp.sum(dgc_t * ext_ref[off : off + tt, :], axis=0, keepdims=True)

    return pl.pallas_call(
        body,
        grid=(nt,),
        in_specs=[
            pl.BlockSpec((tt, 2 * D_FF), lambda i: (i, 0)),
            _prev_halo(tt, hp, 2 * D_FF),
            _next_halo(tt, hp, 2 * D_FF, t),
            pl.BlockSpec((tt, D_FF), lambda i: (i, 0)),
            _next_halo(tt, hp, D_FF, t),
            pl.BlockSpec((kk, D_FF), lambda i: (0, 0)),
            pl.BlockSpec((1, D_FF), lambda i: (0, 0)),
        ],
        out_specs=[
            pl.BlockSpec((tt, 2 * D_FF), lambda i: (i, 0)),
            pl.BlockSpec((kk, D_FF), lambda i: (0, 0)),
            pl.BlockSpec((1, D_FF), lambda i: (0, 0)),
        ],
        out_shape=[
            jax.ShapeDtypeStruct((t, 2 * D_FF), MXU_DT),
            jax.ShapeDtypeStruct((kk, D_FF), F32),
            jax.ShapeDtypeStruct((1, D_FF), F32),
        ],
        scratch_shapes=[pltpu.VMEM((tt + 2 * hp, D_FF // 2), F32), pltpu.VMEM((tt + hp, D_FF // 2), F32)],
        name=name,
        compiler_params=_cparams("arbitrary"),
    )(up, up, up, dact, dact, conv_w, conv_b.reshape(1, D_FF))


def _layernorm_silu(cv, ln_g, ln_b):
    mu = jnp.mean(cv, axis=-1, keepdims=True)
    xc = cv - mu
    rstd = lax.rsqrt(jnp.mean(xc * xc, axis=-1, keepdims=True) + EPS)
    xh = xc * rstd
    a = xh * ln_g + ln_b
    return xh, rstd, a


def _mix_fwd(z, pool_w, pool_scale, dw_w, dw_b, ln_g, ln_b, name):
    t = z.shape[0]
    tt = _rows(t, TW)
    hp = MIX_HALO
    zw = POOL_W + 2 * CONV_W

    def body(z_ref, zp_ref, pw_ref, ps_ref, w_ref, b_ref, lg_ref, lb_ref, cat_ref, eu_ref, egl_ref):
        i = pl.program_id(0)
        first = i == 0
        eu_ref[0:hp, :] = jnp.where(first, 0.0, zp_ref[:, 0:POOL_W])
        eu_ref[hp : hp + tt, :] = z_ref[:, 0:POOL_W]
        glp = zp_ref[:, POOL_W : POOL_W + CONV_W] * _sigmoid(zp_ref[:, POOL_W + CONV_W : zw])
        egl_ref[0:hp, :] = jnp.where(first, 0.0, glp)
        egl_ref[hp : hp + tt, :] = z_ref[:, POOL_W : POOL_W + CONV_W] * _sigmoid(z_ref[:, POOL_W + CONV_W : zw])
        row = i * tt + lax.broadcasted_iota(jnp.int32, (tt, 1), 0)
        for gi, w in enumerate(POOL_WINDOWS):
            cols = pl.ds(gi * POOL_GROUP, POOL_GROUP)
            u = eu_ref[hp : hp + tt, cols]
            acc = u
            for k in range(1, w):
                acc = acc + eu_ref[hp - k : hp - k + tt, cols]
            cnt = jnp.minimum(row + 1, w).astype(F32)
            pooled = acc / cnt - u
            y = jnp.dot(pooled.astype(MXU_DT), pw_ref[gi].astype(MXU_DT), preferred_element_type=F32)
            cat_ref[:, cols] = (y * ps_ref[:, cols]).astype(cat_ref.dtype)
        cv = b_ref[...]
        for j in range(CONV_K):
            off = hp - (CONV_K - 1) + j
            cv = cv + w_ref[j : j + 1, :] * egl_ref[off : off + tt, :]
        _, _, a = _layernorm_silu(cv, lg_ref[...], lb_ref[...])
        cat_ref[:, POOL_W : POOL_W + CONV_W] = (a * _sigmoid(a)).astype(cat_ref.dtype)

    vec = pl.BlockSpec((1, CONV_W), lambda i: (0, 0))
    return pl.pallas_call(
        body,
        grid=(t // tt,),
        in_specs=[
            pl.BlockSpec((tt, zw), lambda i: (i, 0)),
            _prev_halo(tt, hp, zw),
            pl.BlockSpec((len(POOL_WINDOWS), POOL_GROUP, POOL_GROUP), lambda i: (0, 0, 0)),
            vec,
            pl.BlockSpec((CONV_K, CONV_W), lambda i: (0, 0)),
            vec,
            vec,
            vec,
        ],
        out_specs=pl.BlockSpec((tt, POOL_W + CONV_W), lambda i: (i, 0)),
        out_shape=jax.ShapeDtypeStruct((t, POOL_W + CONV_W), MXU_DT),
        scratch_shapes=[pltpu.VMEM((tt + hp, POOL_W), F32), pltpu.VMEM((tt + hp, CONV_W), F32)],
        name=name,
        compiler_params=_cparams("parallel"),
    )(z, z, pool_w, pool_scale.reshape(1, POOL_W), dw_w, dw_b.reshape(1, CONV_W), ln_g.reshape(1, CONV_W), ln_b.reshape(1, CONV_W))


def _mix_bwd(z, dcat, pool_w, pool_scale, dw_w, dw_b, ln_g, ln_b, name):
    t = z.shape[0]
    tt = _rows(t, TW)
    hp = MIX_HALO
    nt = t // tt
    zw = POOL_W + 2 * CONV_W
    ng = len(POOL_WINDOWS)

    def body(z_ref, zp_ref, zn_ref, dc_ref, dcn_ref, pw_ref, ps_ref, w_ref, b_ref, lg_ref, lb_ref,
             dz_ref, dpw_ref, dps_ref, dww_ref, dwb_ref, dlg_ref, dlb_ref, eu_ref, ee_ref, egl_ref, edcv_ref):
        i = pl.program_id(0)
        first = i == 0
        last = i == nt - 1

        @pl.when(first)
        def _():
            for r in (dpw_ref, dps_ref, dww_ref, dwb_ref, dlg_ref, dlb_ref):
                r[...] = jnp.zeros_like(r)

        eu_ref[0:hp, :] = jnp.where(first, 0.0, zp_ref[:, 0:POOL_W])
        eu_ref[hp : hp + tt, :] = z_ref[:, 0:POOL_W]
        row = i * tt + lax.broadcasted_iota(jnp.int32, (tt, 1), 0)
        row_ext = i * tt + lax.broadcasted_iota(jnp.int32, (tt + hp, 1), 0)
        for gi, w in enumerate(POOL_WINDOWS):
            cols = pl.ds(gi * POOL_GROUP, POOL_GROUP)
            u = eu_ref[hp : hp + tt, cols]
            acc = u
            for k in range(1, w):
                acc = acc + eu_ref[hp - k : hp - k + tt, cols]
            pooled = (acc / jnp.minimum(row + 1, w).astype(F32) - u).astype(MXU_DT)
            pw = pw_ref[gi].astype(MXU_DT)
            dya = dc_ref[:, cols]
            y = jnp.dot(pooled, pw, preferred_element_type=F32)
            dps_ref[:, cols] += jnp.sum(dya * y, axis=0, keepdims=True)
            scale = ps_ref[:, cols]
            dy_all = jnp.concatenate([dya, jnp.where(last, 0.0, dcn_ref[:, cols])], axis=0) * scale
            dy_all = dy_all.astype(MXU_DT)
            dpw_ref[gi] += lax.dot_general(pooled, dy_all[0:tt], TN_DIMS, preferred_element_type=F32)
            dpooled = lax.dot_general(dy_all, pw, NT, preferred_element_type=F32)
            ee_ref[:, cols] = dpooled / jnp.minimum(row_ext + 1, w).astype(F32)
            du = -dpooled[0:tt]
            for k in range(w):
                du = du + ee_ref[k : k + tt, cols]
            dz_ref[:, cols] = du.astype(dz_ref.dtype)

        ca = slice(POOL_W, POOL_W + CONV_W)
        cb = slice(POOL_W + CONV_W, zw)
        egl_ref[0:hp, :] = jnp.where(first, 0.0, zp_ref[:, ca] * _sigmoid(zp_ref[:, cb]))
        ga = z_ref[:, ca]
        sgb = _sigmoid(z_ref[:, cb])
        egl_ref[hp : hp + tt, :] = ga * sgb
        egl_ref[hp + tt : hp + tt + hp, :] = zn_ref[:, ca] * _sigmoid(zn_ref[:, cb])
        cv = b_ref[...]
        for j in range(CONV_K):
            off = hp - (CONV_K - 1) + j
            cv = cv + w_ref[j : j + 1, :] * egl_ref[off : off + tt + hp, :]
        lg = lg_ref[...]
        xh, rstd, a = _layernorm_silu(cv, lg, lb_ref[...])
        sa = _sigmoid(a)
        dyb = jnp.concatenate([dc_ref[:, ca], jnp.where(last, 0.0, dcn_ref[:, ca])], axis=0)
        da = dyb * (sa * (1.0 + a * (1.0 - sa)))
        dlg_ref[...] += jnp.sum(da[0:tt] * xh[0:tt], axis=0, keepdims=True)
        dlb_ref[...] += jnp.sum(da[0:tt], axis=0, keepdims=True)
        dxh = da * lg
        dcv = rstd * (dxh - jnp.mean(dxh, axis=-1, keepdims=True) - xh * jnp.mean(dxh * xh, axis=-1, keepdims=True))
        edcv_ref[...] = dcv
        dcv_t = dcv[0:tt]
        dwb_ref[...] += jnp.sum(dcv_t, axis=0, keepdims=True)
        dgl = jnp.zeros((tt, CONV_W), F32)
        for j in range(CONV_K):
            off = hp - (CONV_K - 1) + j
            dww_ref[j : j + 1, :] += jnp.sum(dcv_t * egl_ref[off : off + tt, :], axis=0, keepdims=True)
            dgl = dgl + w_ref[j : j + 1, :] * edcv_ref[CONV_K - 1 - j : CONV_K - 1 - j + tt, :]
        dz_ref[:, ca] = (dgl * sgb).astype(dz_ref.dtype)
        dz_ref[:, cb] = (dgl * ga * sgb * (1.0 - sgb)).astype(dz_ref.dtype)

    vec = pl.BlockSpec((1, CONV_W), lambda i: (0, 0))
    pw_spec = pl.BlockSpec((ng, POOL_GROUP, POOL_GROUP), lambda i: (0, 0, 0))
    w_spec = pl.BlockSpec((CONV_K, CONV_W), lambda i: (0, 0))
    return pl.pallas_call(
        body,
        grid=(nt,),
        in_specs=[
            pl.BlockSpec((tt, zw), lambda i: (i, 0)),
            _prev_halo(tt, hp, zw),
            _next_halo(tt, hp, zw, t),
            pl.BlockSpec((tt, POOL_W + CONV_W), lambda i: (i, 0)),
            _next_halo(tt, hp, POOL_W + CONV_W, t),
            pw_spec, vec, w_spec, vec, vec, vec,
        ],
        out_specs=[pl.BlockSpec((tt, zw), lambda i: (i, 0)), pw_spec, vec, w_spec, vec, vec, vec],
        out_shape=[
            jax.ShapeDtypeStruct((t, zw), MXU_DT),
            jax.ShapeDtypeStruct((ng, POOL_GROUP, POOL_GROUP), F32),
            jax.ShapeDtypeStruct((1, POOL_W), F32),
            jax.ShapeDtypeStruct((CONV_K, CONV_W), F32),
            jax.ShapeDtypeStruct((1, CONV_W), F32),
            jax.ShapeDtypeStruct((1, CONV_W), F32),
            jax.ShapeDtypeStruct((1, CONV_W), F32),
        ],
        scratch_shapes=[
            pltpu.VMEM((tt + hp, POOL_W), F32),
            pltpu.VMEM((tt + hp, POOL_W), F32),
            pltpu.VMEM((tt + 2 * hp, CONV_W), F32),
            pltpu.VMEM((tt + hp, CONV_W), F32),
        ],
        name=name,
        compiler_params=_cparams("arbitrary"),
    )(z, z, z, dcat, dcat, pool_w, pool_scale.reshape(1, POOL_W), dw_w, dw_b.reshape(1, CONV_W),
      ln_g.reshape(1, CONV_W), ln_b.reshape(1, CONV_W))


def _xa_fwd(q, kvm, name):
    t = q.shape[0]
    tt = _rows(t, TT)
    scale = XA_HEAD_DIM ** -0.5

    def body(q_ref, kv_ref, o_ref):
        for h in range(XA_HEADS):
            cs = pl.ds(h * XA_HEAD_DIM, XA_HEAD_DIM)
            vs = pl.ds(D_MODEL + h * XA_HEAD_DIM, XA_HEAD_DIM)
            s = lax.dot_general(q_ref[:, cs], kv_ref[:, cs], NT, preferred_element_type=F32) * scale
            p = jnp.exp(s - jnp.max(s, axis=-1, keepdims=True))
            p = p / jnp.sum(p, axis=-1, keepdims=True)
            o_ref[:, cs] = jnp.dot(p.astype(MXU_DT), kv_ref[:, vs], preferred_element_type=F32).astype(o_ref.dtype)

    return pl.pallas_call(
        body,
        grid=(t // tt,),
        in_specs=[pl.BlockSpec((tt, D_MODEL), lambda i: (i, 0)), pl.BlockSpec((MEM_LEN, 2 * D_MODEL), lambda i: (0, 0))],
        out_specs=pl.BlockSpec((tt, D_MODEL), lambda i: (i, 0)),
        out_shape=jax.ShapeDtypeStruct((t, D_MODEL), MXU_DT),
        name=name,
        compiler_params=_cparams("parallel"),
    )(q, kvm)


def _xa_bwd(q, kvm, do, name):
    t = q.shape[0]
    tt = _rows(t, TT)
    scale = XA_HEAD_DIM ** -0.5

    def body(q_ref, kv_ref, do_ref, dq_ref, dkv_ref):
        @pl.when(pl.program_id(0) == 0)
        def _():
            dkv_ref[...] = jnp.zeros_like(dkv_ref)

        for h in range(XA_HEADS):
            cs = pl.ds(h * XA_HEAD_DIM, XA_HEAD_DIM)
            vs = pl.ds(D_MODEL + h * XA_HEAD_DIM, XA_HEAD_DIM)
            qh = q_ref[:, cs]
            kh = kv_ref[:, cs]
            doh = do_ref[:, cs]
            s = lax.dot_general(qh, kh, NT, preferred_element_type=F32) * scale
            p = jnp.exp(s - jnp.max(s, axis=-1, keepdims=True))
            p = p / jnp.sum(p, axis=-1, keepdims=True)
            dp = lax.dot_general(doh, kv_ref[:, vs], NT, preferred_element_type=F32)
            ds = (p * (dp - jnp.sum(p * dp, axis=-1, keepdims=True)) * scale).astype(MXU_DT)
            dq_ref[:, cs] = jnp.dot(ds, kh, preferred_element_type=F32).astype(dq_ref.dtype)
            dkv_ref[:, cs] += lax.dot_general(ds, qh, TN_DIMS, preferred_element_type=F32)
            dkv_ref[:, vs] += lax.dot_general(p.astype(MXU_DT), doh, TN_DIMS, preferred_element_type=F32)

    row = pl.BlockSpec((tt, D_MODEL), lambda i: (i, 0))
    kvs = pl.BlockSpec((MEM_LEN, 2 * D_MODEL), lambda i: (0, 0))
    return pl.pallas_call(
        body,
        grid=(t // tt,),
        in_specs=[row, kvs, row],
        out_specs=[row, kvs],
        out_shape=[jax.ShapeDtypeStruct((t, D_MODEL), MXU_DT), jax.ShapeDtypeStruct((MEM_LEN, 2 * D_MODEL), F32)],
        name=name,
        compiler_params=_cparams("arbitrary"),
    )(q, kvm, do)


def _rope_tables(positions, name):
    t = positions.shape[0]
    tt = _rows(t, TT)
    inv = 1.0 / (ROPE_THETA ** (np.arange(0, QK_ROPE, 2, dtype=np.float32) / QK_ROPE))
    lanes = np.zeros((1, HEAD_PAD), np.float32)
    half = QK_ROPE // 2
    lanes[0, KPE_LANE : KPE_LANE + half] = inv
    lanes[0, KPE_LANE + half : KPE_LANE + QK_ROPE] = inv

    def body(pos_ref, inv_ref, cos_ref, sa_ref, sb_ref):
        ang = pos_ref[...].astype(F32) * inv_ref[...]
        lane = lax.broadcasted_iota(jnp.int32, (tt, HEAD_PAD), 1)
        c = jnp.cos(ang)
        s = jnp.sin(ang)
        lo = (lane >= KPE_LANE) & (lane < KPE_LANE + half)
        hi = (lane >= KPE_LANE + half) & (lane < KPE_LANE + QK_ROPE)
        cos_ref[...] = jnp.where(lo | hi, c, 1.0)
        sa_ref[...] = jnp.where(hi, s, 0.0)
        sb_ref[...] = jnp.where(lo, -s, 0.0)

    tab = pl.BlockSpec((tt, HEAD_PAD), lambda i: (i, 0))
    return pl.pallas_call(
        body,
        grid=(t // tt,),
        in_specs=[pl.BlockSpec((tt, 1), lambda i: (i, 0)), pl.BlockSpec((1, HEAD_PAD), lambda i: (0, 0))],
        out_specs=[tab, tab, tab],
        out_shape=[jax.ShapeDtypeStruct((t, HEAD_PAD), F32)] * 3,
        name=name,
        compiler_params=_cparams("parallel"),
    )(positions, jnp.asarray(lanes))


def _rotate(x, cos, sa, sb, sign):
    half = QK_ROPE // 2
    return x * cos + sign * (pltpu.roll(x, half, 1) * sa + pltpu.roll(x, HEAD_PAD - half, 1) * sb)


def _rope_heads(x, tables, sign, scale, name):
    t, w = x.shape
    tt = _rows(t, TT)
    nh = w // HEAD_PAD

    def body(x_ref, c_ref, sa_ref, sb_ref, o_ref):
        cos, sa, sb = c_ref[...] * scale, sa_ref[...] * scale, sb_ref[...] * scale
        for h in range(nh):
            cs = pl.ds(h * HEAD_PAD, HEAD_PAD)
            o_ref[:, cs] = _rotate(x_ref[:, cs], cos, sa, sb, sign).astype(o_ref.dtype)

    tab = pl.BlockSpec((tt, HEAD_PAD), lambda i: (i, 0))
    row = pl.BlockSpec((tt, w), lambda i: (i, 0))
    return pl.pallas_call(
        body,
        grid=(t // tt,),
        in_specs=[row, tab, tab, tab],
        out_specs=row,
        out_shape=jax.ShapeDtypeStruct((t, w), MXU_DT),
        name=name,
        compiler_params=_cparams("parallel"),
    )(x, *tables)


def _mla_prep(cp, qg, kvg, tables, name):
    t = cp.shape[0]
    tt = _rows(t, TT)

    def body(cp_ref, qg_ref, kvg_ref, c_ref, sa_ref, sb_ref, qn_ref, kvn_ref, kpe_ref):
        cq = cp_ref[:, 0:Q_LORA]
        r = lax.rsqrt(jnp.mean(cq * cq, axis=-1, keepdims=True) + EPS)
        qn_ref[...] = ((cq * r) * qg_ref[...]).astype(qn_ref.dtype)
        ckv = cp_ref[:, Q_LORA : Q_LORA + KV_LORA]
        r = lax.rsqrt(jnp.mean(ckv * ckv, axis=-1, keepdims=True) + EPS)
        kvn_ref[...] = ((ckv * r) * kvg_ref[...]).astype(kvn_ref.dtype)
        kpe = cp_ref[:, Q_LORA + KV_LORA : C_PAD]
        kpe_ref[...] = _rotate(kpe, c_ref[...], sa_ref[...], sb_ref[...], 1.0).astype(kpe_ref.dtype)

    tab = pl.BlockSpec((tt, HEAD_PAD), lambda i: (i, 0))
    return pl.pallas_call(
        body,
        grid=(t // tt,),
        in_specs=[
            pl.BlockSpec((tt, C_PAD), lambda i: (i, 0)),
            pl.BlockSpec((1, Q_LORA), lambda i: (0, 0)),
            pl.BlockSpec((1, KV_LORA), lambda i: (0, 0)),
            tab, tab, tab,
        ],
        out_specs=[
            pl.BlockSpec((tt, Q_LORA), lambda i: (i, 0)),
            pl.BlockSpec((tt, KV_LORA), lambda i: (i, 0)),
            tab,
        ],
        out_shape=[
            jax.ShapeDtypeStruct((t, Q_LORA), MXU_DT),
            jax.ShapeDtypeStruct((t, KV_LORA), MXU_DT),
            jax.ShapeDtypeStruct((t, HEAD_PAD), MXU_DT),
        ],
        name=name,
        compiler_params=_cparams("parallel"),
    )(cp, qg.reshape(1, Q_LORA), kvg.reshape(1, KV_LORA), *tables)


def _mla_prep_bwd(cp, dqn, dkvn, dkpe_heads, qg, kvg, tables, name):
    t = cp.shape[0]
    tt = _rows(t, TT)

    def norm_bwd(x, dy, g):
        r = lax.rsqrt(jnp.mean(x * x, axis=-1, keepdims=True) + EPS)
        xh = x * r
        gy = dy * g
        return r * (gy - xh * jnp.mean(gy * xh, axis=-1, keepdims=True)), jnp.sum(dy * xh, axis=0, keepdims=True)

    def body(cp_ref, dqn_ref, dkvn_ref, dkpe_ref, qg_ref, kvg_ref, c_ref, sa_ref, sb_ref, dcp_ref, dqg_ref, dkvg_ref):
        @pl.when(pl.program_id(0) == 0)
        def _():
            dqg_ref[...] = jnp.zeros_like(dqg_ref)
            dkvg_ref[...] = jnp.zeros_like(dkvg_ref)

        dcq, dg = norm_bwd(cp_ref[:, 0:Q_LORA], dqn_ref[...], qg_ref[...])
        dcp_ref[:, 0:Q_LORA] = dcq.astype(dcp_ref.dtype)
        dqg_ref[...] += dg
        dckv, dg = norm_bwd(cp_ref[:, Q_LORA : Q_LORA + KV_LORA], dkvn_ref[...], kvg_ref[...])
        dcp_ref[:, Q_LORA : Q_LORA + KV_LORA] = dckv.astype(dcp_ref.dtype)
        dkvg_ref[...] += dg
        dk = dkpe_ref[0]
        for h in range(1, MLA_HEADS):
            dk = dk + dkpe_ref[h]
        dcp_ref[:, Q_LORA + KV_LORA : C_PAD] = _rotate(dk, c_ref[...], sa_ref[...], sb_ref[...], -1.0).astype(dcp_ref.dtype)

    tab = pl.BlockSpec((tt, HEAD_PAD), lambda i: (i, 0))
    return pl.pallas_call(
        body,
        grid=(t // tt,),
        in_specs=[
            pl.BlockSpec((tt, C_PAD), lambda i: (i, 0)),
            pl.BlockSpec((tt, Q_LORA), lambda i: (i, 0)),
            pl.BlockSpec((tt, KV_LORA), lambda i: (i, 0)),
            pl.BlockSpec((MLA_HEADS, tt, HEAD_PAD), lambda i: (0, i, 0)),
            pl.BlockSpec((1, Q_LORA), lambda i: (0, 0)),
            pl.BlockSpec((1, KV_LORA), lambda i: (0, 0)),
            tab, tab, tab,
        ],
        out_specs=[
            pl.BlockSpec((tt, C_PAD), lambda i: (i, 0)),
            pl.BlockSpec((1, Q_LORA), lambda i: (0, 0)),
            pl.BlockSpec((1, KV_LORA), lambda i: (0, 0)),
        ],
        out_shape=[
            jax.ShapeDtypeStruct((t, C_PAD), MXU_DT),
            jax.ShapeDtypeStruct((1, Q_LORA), F32),
            jax.ShapeDtypeStruct((1, KV_LORA), F32),
        ],
        name=name,
        compiler_params=_cparams("arbitrary"),
    )(cp, dqn, dkvn, dkpe_heads, qg.reshape(1, Q_LORA), kvg.reshape(1, KV_LORA), *tables)


def _flash_fwd(qs, kv, kpe, name):
    t = qs.shape[0]
    ta = _rows(t, TA)
    nq = t // ta
    hq = ta // 2

    def body(q_ref, kv_ref, kpe_ref, o_ref, lse_ref):
        qi = pl.program_id(1)
        halves = [pl.ds(0, hq), pl.ds(hq, hq)]
        lane = lax.broadcasted_iota(jnp.int32, (ta, HEAD_PAD), 1)

        def kblock(j):
            rows = pl.ds(pl.multiple_of(j * ta, ta), ta)
            kvb = kv_ref[rows, :]
            return kvb, jnp.where(lane < QK_NOPE, kvb, kpe_ref[rows, :])

        def update(carry, s, kvb):
            m, l, acc = carry
            m_new = jnp.maximum(m, jnp.max(s, axis=-1, keepdims=True))
            alpha = jnp.exp2(m - m_new)
            p = jnp.exp2(s - m_new)
            l = alpha * l + jnp.sum(p, axis=-1, keepdims=True)
            acc = alpha * acc + jnp.dot(p.astype(MXU_DT), kvb, preferred_element_type=F32)
            return m_new, l, acc

        def step(j, carry):
            kvb, k = kblock(j)
            return tuple(
                update(carry[h], lax.dot_general(q_ref[halves[h], :], k, NT, preferred_element_type=F32), kvb)
                for h in range(2))

        one = (jnp.full((hq, 1), -jnp.inf, F32), jnp.zeros((hq, 1), F32), jnp.zeros((hq, HEAD_PAD), F32))
        carry = lax.fori_loop(0, qi, step, (one, one))
        kvb, k = kblock(qi)
        lane_h = lax.broadcasted_iota(jnp.int32, (hq, HEAD_PAD), 1)
        for h in range(2):
            nk = (h + 1) * hq
            s = lax.dot_general(q_ref[halves[h], :], k[0:nk], NT, preferred_element_type=F32)
            r = lax.broadcasted_iota(jnp.int32, (hq, nk), 0) + h * hq
            c = lax.broadcasted_iota(jnp.int32, (hq, nk), 1)
            m, l, acc = update(carry[h], jnp.where(c <= r, s, NEG), kvb[0:nk])
            o_ref[halves[h], :] = jnp.where(lane_h >= QK_NOPE, acc / l, 0.0).astype(o_ref.dtype)
            lse_ref[halves[h], :] = m + jnp.log2(l)

    return pl.pallas_call(
        body,
        grid=(MLA_HEADS, nq),
        in_specs=[
            pl.BlockSpec((ta, HEAD_PAD), lambda h, i: (i, h)),
            pl.BlockSpec((t, HEAD_PAD), lambda h, i: (0, h)),
            pl.BlockSpec((t, HEAD_PAD), lambda h, i: (0, 0)),
        ],
        out_specs=[
            pl.BlockSpec((ta, HEAD_PAD), lambda h, i: (i, h)),
            pl.BlockSpec((None, ta, 1), lambda h, i: (h, i, 0)),
        ],
        out_shape=[
            jax.ShapeDtypeStruct((t, MLA_HEADS * HEAD_PAD), MXU_DT),
            jax.ShapeDtypeStruct((MLA_HEADS, t, 1), F32),
        ],
        name=name,
        compiler_params=_cparams("parallel", "parallel"),
    )(qs, kv, kpe)


def _flash_bwd(qs, kv, kpe, o, do, lse, name):
    t = qs.shape[0]
    ta = _rows(t, TA)
    nq = t // ta

    def body(q_ref, o_ref, do_ref, lse_ref, kv_ref, kpe_ref, dq_ref, dkv_ref, dkpe_ref, dk_acc, dv_acc):
        kj = pl.program_id(1)

        @pl.when(kj == 0)
        def _():
            dq_ref[...] = jnp.zeros_like(dq_ref)

        lane = lax.broadcasted_iota(jnp.int32, (ta, HEAD_PAD), 1)
        kvb = kv_ref[...]
        k = jnp.where(lane < QK_NOPE, kvb, kpe_ref[...])
        dk_acc[...] = jnp.zeros_like(dk_acc)
        dv_acc[...] = jnp.zeros_like(dv_acc)

        def tile(qi, diagonal):
            rows = pl.ds(pl.multiple_of(qi * ta, ta), ta)
            q = q_ref[rows, :]
            dob = do_ref[rows, :]
            delta = jnp.sum(dob.astype(F32) * o_ref[rows, :].astype(F32), axis=-1, keepdims=True)
            s = lax.dot_general(q, k, NT, preferred_element_type=F32)
            if diagonal:
                r = lax.broadcasted_iota(jnp.int32, (ta, ta), 0)
                c = lax.broadcasted_iota(jnp.int32, (ta, ta), 1)
                s = jnp.where(c <= r, s, NEG)
            p = jnp.exp2(s - lse_ref[rows, :])
            dp = lax.dot_general(dob, kvb, NT, preferred_element_type=F32)
            ds = (p * (dp - delta)).astype(MXU_DT)
            dq_ref[rows, :] += jnp.dot(ds, k, preferred_element_type=F32)
            dk_acc[...] += lax.dot_general(ds, q, TN_DIMS, preferred_element_type=F32)
            dv_acc[...] += lax.dot_general(p.astype(MXU_DT), dob, TN_DIMS, preferred_element_type=F32)

        tile(kj, True)

        def step(qi, carry):
            tile(qi, False)
            return carry

        lax.fori_loop(kj + 1, nq, step, 0)
        dk = dk_acc[...] * (1.0 / LOG2E)
        dkv_ref[...] = jnp.where(lane < QK_NOPE, dk, dv_acc[...]).astype(dkv_ref.dtype)
        dkpe_ref[...] = jnp.where((lane >= KPE_LANE) & (lane < KPE_LANE + QK_ROPE), dk, 0.0)

    head_rows = pl.BlockSpec((t, HEAD_PAD), lambda h, j: (0, h))
    return pl.pallas_call(
        body,
        grid=(MLA_HEADS, nq),
        in_specs=[
            head_rows,
            head_rows,
            head_rows,
            pl.BlockSpec((None, t, 1), lambda h, j: (h, 0, 0)),
            pl.BlockSpec((ta, HEAD_PAD), lambda h, j: (j, h)),
            pl.BlockSpec((ta, HEAD_PAD), lambda h, j: (j, 0)),
        ],
        out_specs=[
            head_rows,
            pl.BlockSpec((ta, HEAD_PAD), lambda h, j: (j, h)),
            pl.BlockSpec((None, ta, HEAD_PAD), lambda h, j: (h, j, 0)),
        ],
        out_shape=[
            jax.ShapeDtypeStruct((t, MLA_HEADS * HEAD_PAD), F32),
            jax.ShapeDtypeStruct((t, MLA_HEADS * HEAD_PAD), MXU_DT),
            jax.ShapeDtypeStruct((MLA_HEADS, t, HEAD_PAD), F32),
        ],
        scratch_shapes=[pltpu.VMEM((ta, HEAD_PAD), F32), pltpu.VMEM((ta, HEAD_PAD), F32)],
        name=name,
        compiler_params=_cparams("parallel", "arbitrary"),
    )(qs, o, do, lse, kv, kpe)


def _as2d(a):
    if a.ndim == 1:
        return a.reshape(1, a.shape[0])
    return a.reshape(-1, a.shape[-1])


def _adamw(w, g, m, v, name):
    shape = w.shape
    w2, g2, m2, v2 = (_as2d(a) for a in (w, g, m, v))
    r, c = w2.shape
    tr = _tile_rows(r, c)
    c1 = 1.0 - ADAM_B1 ** ADAM_STEP
    c2 = 1.0 - ADAM_B2 ** ADAM_STEP

    def body(w_ref, g_ref, m_ref, v_ref, d_ref, nm_ref, nv_ref):
        gv = g_ref[...]
        nm = ADAM_B1 * m_ref[...] + (1.0 - ADAM_B1) * gv
        nv = ADAM_B2 * v_ref[...] + (1.0 - ADAM_B2) * (gv * gv)
        d_ref[...] = -ADAM_LR * ((nm / c1) / (jnp.sqrt(nv / c2) + ADAM_EPS) + ADAM_WD * w_ref[...])
        nm_ref[...] = nm
        nv_ref[...] = nv

    blk = pl.BlockSpec((tr, c), lambda i: (i, 0))
    outs = pl.pallas_call(
        body,
        grid=(r // tr,),
        in_specs=[blk] * 4,
        out_specs=[blk] * 3,
        out_shape=[jax.ShapeDtypeStruct((r, c), F32)] * 3,
        name=name,
        compiler_params=_cparams("parallel"),
    )(w2, g2, m2, v2)
    return tuple(o.reshape(shape) for o in outs)


def _adamw_halves(w, mine, other, m, v, c_idx, name):
    nl, r, c = w.shape
    h = nl // 2
    tr = _tile_rows(r, 2 * c)
    c1 = 1.0 - ADAM_B1 ** ADAM_STEP
    c2 = 1.0 - ADAM_B2 ** ADAM_STEP

    def body(c_ref, w_ref, a_ref, b_ref, m_ref, v_ref, g_ref, d_ref, nm_ref, nv_ref):
        l = pl.program_id(0)
        gv = jnp.where(l // h == c_ref[0], a_ref[...], b_ref[...])
        nm = ADAM_B1 * m_ref[...] + (1.0 - ADAM_B1) * gv
        nv = ADAM_B2 * v_ref[...] + (1.0 - ADAM_B2) * (gv * gv)
        g_ref[...] = gv
        d_ref[...] = -ADAM_LR * ((nm / c1) / (jnp.sqrt(nv / c2) + ADAM_EPS) + ADAM_WD * w_ref[...])
        nm_ref[...] = nm
        nv_ref[...] = nv

    def half_map(mine_side):
        def index(l, i, cr):
            first = cr[0] * h if mine_side else (1 - cr[0]) * h
            return (jnp.clip(l - first, 0, h - 1), i, 0)
        return index

    full = pl.BlockSpec((None, tr, c), lambda l, i, cr: (l, i, 0))
    grid_spec = pltpu.PrefetchScalarGridSpec(
        num_scalar_prefetch=1,
        grid=(nl, r // tr),
        in_specs=[full, pl.BlockSpec((None, tr, c), half_map(True)), pl.BlockSpec((None, tr, c), half_map(False)), full, full],
        out_specs=[full] * 4,
    )
    return pl.pallas_call(
        body,
        grid_spec=grid_spec,
        out_shape=[jax.ShapeDtypeStruct((nl, r, c), F32)] * 4,
        name=name,
        compiler_params=_cparams("parallel", "parallel"),
    )(c_idx, w, mine, other, m, v)


def _tile_rows(r, c, mult=SUBLANE):
    limit = max(mult, (BLOCK_BYTES // 4) // (4 * c))
    if r <= limit:
        return r
    t = (limit // mult) * mult
    while t >= mult:
        if r % t == 0:
            return t
        t -= mult
    return r


def _sum_leading(a, name):
    n, r, c = a.shape
    tr = _tile_rows(r, c * n)

    def body(a_ref, o_ref):
        s = a_ref[0]
        for k in range(1, n):
            s = s + a_ref[k]
        o_ref[...] = s

    return pl.pallas_call(
        body,
        grid=(r // tr,),
        in_specs=[pl.BlockSpec((n, tr, c), lambda i: (0, i, 0))],
        out_specs=pl.BlockSpec((tr, c), lambda i: (i, 0)),
        out_shape=jax.ShapeDtypeStruct((r, c), F32),
        name=name,
        compiler_params=_cparams("parallel"),
    )(a)


def _add_half(g, s, c_idx, name):
    nl, r, c = g.shape
    h = nl // 2
    tr = _tile_rows(r, 2 * c, 2 * SUBLANE)

    def body(c_ref, g_ref, s_ref, o_ref):
        o_ref[...] = (g_ref[...] + s_ref[...]).astype(o_ref.dtype)

    grid_spec = pltpu.PrefetchScalarGridSpec(
        num_scalar_prefetch=1,
        grid=(h, r // tr),
        in_specs=[
            pl.BlockSpec((None, tr, c), lambda l, i, cr: (cr[0] * h + l, i, 0)),
            pl.BlockSpec((None, tr, c), lambda l, i, cr: (l, i, 0)),
        ],
        out_specs=pl.BlockSpec((None, tr, c), lambda l, i, cr: (l, i, 0)),
    )
    return pl.pallas_call(
        body,
        grid_spec=grid_spec,
        out_shape=jax.ShapeDtypeStruct((h, r, c), XFER_DT),
        name=name,
        compiler_params=_cparams("parallel", "parallel"),
    )(c_idx, g, s)


def _sum_chips(slots, pair, chip_idx, kind, name):
    _, h, r, c = slots.shape
    tr = _tile_rows(r, 5 * c, 2 * SUBLANE)
    nr = r // tr

    def body(chip_ref, s_ref, own_ref, o_ref):
        chip = chip_ref[0]
        own = own_ref[...].astype(F32)
        parts = [s_ref[j].astype(F32) for j in range(3)]
        total = None
        for k in range(4):
            d = jnp.bitwise_xor(chip, k)
            v = jnp.where(d == 0, own, jnp.where(d == 2, parts[0], jnp.where(d == 1, parts[1], parts[2])))
            total = v if total is None else total + v
        o_ref[...] = total

    if kind == "row":
        own_spec = pl.BlockSpec((None, tr, c), lambda l, i, cr: (l, cr[0] * nr + i, 0))
    else:
        own_spec = pl.BlockSpec((None, tr, c), lambda l, i, cr: (l, i, cr[0]))
    grid_spec = pltpu.PrefetchScalarGridSpec(
        num_scalar_prefetch=1,
        grid=(h, nr),
        in_specs=[pl.BlockSpec((3, None, tr, c), lambda l, i, cr: (0, l, i, 0)), own_spec],
        out_specs=pl.BlockSpec((None, tr, c), lambda l, i, cr: (l, i, 0)),
    )
    return pl.pallas_call(
        body,
        grid_spec=grid_spec,
        out_shape=jax.ShapeDtypeStruct((h, r, c), F32),
        name=name,
        compiler_params=_cparams("parallel", "parallel"),
    )(chip_idx, slots, pair)


def _mesh_pos():
    return lax.axis_index("x"), lax.axis_index("y"), lax.axis_index("c")


def _other_chips(x, y):
    return [(1 - x, y), (x, 1 - y), (1 - x, 1 - y)]


def _all_gather_rows(block, name):
    m_per, n = block.shape

    def body(x_ref, out_ref, send_sems, recv_sems, local_sem):
        x, y, c = _mesh_pos()
        me, sibling = (x, y, c), (x, y, 1 - c)
        chips = _other_chips(x, y)

        def rows(px, py, pc):
            return out_ref.at[pl.ds((4 * px + 2 * py + pc) * m_per, m_per), :]

        def copy(k, blk, to, src=None):
            return pltpu.make_async_remote_copy(
                src_ref=rows(*blk) if src is None else src,
                dst_ref=rows(*blk),
                send_sem=send_sems.at[k],
                recv_sem=recv_sems.at[k],
                device_id=to,
                device_id_type=MESH_ID,
            )

        mine = pltpu.make_async_copy(x_ref, rows(*me), local_sem)
        mine.start()
        first = [copy(0, me, sibling, src=x_ref)]
        first += [copy(1 + j, me, (*chip, c), src=x_ref) for j, chip in enumerate(chips)]
        for cp in first:
            cp.start()
        passed = [copy(4 + j, (*chip, c), sibling) for j, chip in enumerate(chips)]
        for j, chip in enumerate(chips):
            copy(1 + j, (*chip, c), me).wait_recv()
            passed[j].start()
        copy(0, sibling, me).wait_recv()
        for j, chip in enumerate(chips):
            copy(4 + j, (*chip, 1 - c), me).wait_recv()
        for cp in first + passed:
            cp.wait_send()
        mine.wait()

    return pl.pallas_call(
        body,
        out_shape=jax.ShapeDtypeStruct((8 * m_per, n), block.dtype),
        in_specs=[pl.BlockSpec(memory_space=pltpu.VMEM)],
        out_specs=pl.BlockSpec(memory_space=pltpu.VMEM),
        scratch_shapes=[pltpu.SemaphoreType.DMA((7,)), pltpu.SemaphoreType.DMA((7,)), pltpu.SemaphoreType.DMA],
        name=name,
        compiler_params=pltpu.CompilerParams(vmem_limit_bytes=VMEM_LIMIT),
    )(block)


def _shard_window(ref, layers, chip, rows, cols):
    if rows is not None:
        return ref.at[layers, pl.ds(pl.multiple_of(chip * rows, rows), rows), :]
    return ref.at[layers, :, pl.ds(pl.multiple_of(chip * cols, cols), cols)]


def _all_gather_weights(shards, kinds, name):
    nw = len(shards)
    out_shapes = []
    for s, kind in zip(shards, kinds):
        nl, r, c = s.shape
        full = (nl, 4 * r, c) if kind == "row" else (nl, r, 4 * c)
        out_shapes.append(jax.ShapeDtypeStruct(full, s.dtype))

    def body(*refs):
        ins, outs = refs[:nw], refs[nw : 2 * nw]
        send_sems, recv_sems, in_sems, out_sems = refs[2 * nw : 2 * nw + 4]
        bufs = refs[2 * nw + 4 :]
        x, y, c = _mesh_pos()
        sibling = (x, y, 1 - c)
        chips = _other_chips(x, y)
        my_chip = 2 * x + y

        def window(w, chip, layers):
            _, r, cc = shards[w].shape
            if kinds[w] == "row":
                return _shard_window(outs[w], layers, chip, r, None)
            return _shard_window(outs[w], layers, chip, None, cc)

        def half(w, half_idx):
            h = shards[w].shape[0] // 2
            return pl.ds(half_idx * h, h)

        def copy(w, k, src, dst, to):
            return pltpu.make_async_remote_copy(
                src_ref=src, dst_ref=dst, send_sem=send_sems.at[w, k], recv_sem=recv_sems.at[w, k],
                device_id=to, device_id_type=MESH_ID)

        sent = []
        for w in range(nw):
            mine = ins[w].at[half(w, c)]
            for j, chip in enumerate(chips):
                cp = copy(w, j, mine, window(w, my_chip, half(w, c)), (*chip, c))
                cp.start()
                sent.append(cp)
        for w in range(nw):
            nl = shards[w].shape[0]

            def load(l, w=w):
                return pltpu.make_async_copy(ins[w].at[l], bufs[w].at[l % 2], in_sems.at[w, l % 2])

            def store(l, w=w):
                return pltpu.make_async_copy(bufs[w].at[l % 2], window(w, my_chip, l), out_sems.at[w, l % 2])

            load(0).start()
            for l in range(nl):
                load(l).wait()
                store(l).start()
                if l + 1 < nl:
                    if l >= 1:
                        store(l - 1).wait()
                    load(l + 1).start()
            for l in range(max(nl - 2, 0), nl):
                store(l).wait()
        for w in range(nw):
            for j, (cx, cy) in enumerate(chips):
                got = window(w, 2 * cx + cy, half(w, c))
                copy(w, j, got, got, (cx, cy, c)).wait_recv()
                cp = copy(w, 3 + j, got, got, sibling)
                cp.start()
                sent.append(cp)
        for w in range(nw):
            for j, (cx, cy) in enumerate(chips):
                got = window(w, 2 * cx + cy, half(w, 1 - c))
                copy(w, 3 + j, got, got, sibling).wait_recv()
        for cp in sent:
            cp.wait_send()

    anyspec = pl.BlockSpec(memory_space=pl.ANY)
    return pl.pallas_call(
        body,
        out_shape=out_shapes,
        in_specs=[anyspec] * nw,
        out_specs=[anyspec] * nw,
        scratch_shapes=[pltpu.SemaphoreType.DMA((nw, 6)), pltpu.SemaphoreType.DMA((nw, 6)),
                        pltpu.SemaphoreType.DMA((nw, 2)), pltpu.SemaphoreType.DMA((nw, 2))]
        + [pltpu.VMEM((2,) + s.shape[1:], s.dtype) for s in shards],
        name=name,
        compiler_params=pltpu.CompilerParams(vmem_limit_bytes=VMEM_LIMIT),
    )(*shards)


def _exchange_halves(grads, name):
    nw = len(grads)
    out_shapes = [jax.ShapeDtypeStruct((g.shape[0] // 2,) + g.shape[1:], g.dtype) for g in grads]

    def body(*refs):
        ins, outs = refs[:nw], refs[nw : 2 * nw]
        send_sems, recv_sems = refs[2 * nw :]
        x, y, c = _mesh_pos()
        cps = []
        for w in range(nw):
            h = grads[w].shape[0] // 2
            cp = pltpu.make_async_remote_copy(
                src_ref=ins[w].at[pl.ds((1 - c) * h, h)], dst_ref=outs[w], send_sem=send_sems.at[w],
                recv_sem=recv_sems.at[w], device_id=(x, y, 1 - c), device_id_type=MESH_ID)
            cp.start()
            cps.append(cp)
        for cp in cps:
            cp.wait()

    anyspec = pl.BlockSpec(memory_space=pl.ANY)
    return pl.pallas_call(
        body,
        out_shape=out_shapes,
        in_specs=[anyspec] * nw,
        out_specs=[anyspec] * nw,
        scratch_shapes=[pltpu.SemaphoreType.DMA((nw,)), pltpu.SemaphoreType.DMA((nw,))],
        name=name,
    )(*grads)


def _scatter_to_chips(parts, kinds, name):
    nw = len(parts)
    shard_shapes = []
    for p, kind in zip(parts, kinds):
        h, r, c = p.shape
        shard_shapes.append((h, r // 4, c) if kind == "row" else (h, r, c // 4))
    out_shapes = [jax.ShapeDtypeStruct((3,) + s, p.dtype) for s, p in zip(shard_shapes, parts)]

    def body(*refs):
        ins, outs = refs[:nw], refs[nw : 2 * nw]
        send_sems, recv_sems = refs[2 * nw :]
        x, y, c = _mesh_pos()
        chips = _other_chips(x, y)

        def piece(w, chip):
            h, r, cc = shard_shapes[w]
            if kinds[w] == "row":
                return _shard_window(ins[w], pl.ds(0, h), chip, r, None)
            return _shard_window(ins[w], pl.ds(0, h), chip, None, cc)

        def copy(w, j, cx, cy):
            return pltpu.make_async_remote_copy(
                src_ref=piece(w, 2 * cx + cy), dst_ref=outs[w].at[j], send_sem=send_sems.at[w, j],
                recv_sem=recv_sems.at[w, j], device_id=(cx, cy, c), device_id_type=MESH_ID)

        cps = [copy(w, j, cx, cy) for w in range(nw) for j, (cx, cy) in enumerate(chips)]
        for cp in cps:
            cp.start()
        for cp in cps:
            cp.wait()

    anyspec = pl.BlockSpec(memory_space=pl.ANY)
    return pl.pallas_call(
        body,
        out_shape=out_shapes,
        in_specs=[anyspec] * nw,
        out_specs=[anyspec] * nw,
        scratch_shapes=[pltpu.SemaphoreType.DMA((nw, 3)), pltpu.SemaphoreType.DMA((nw, 3))],
        name=name,
    )(*parts)


def _swap_halves(halves, name):
    nw = len(halves)
    out_shapes = [jax.ShapeDtypeStruct(p.shape, p.dtype) for p in halves]

    def body(*refs):
        ins, outs = refs[:nw], refs[nw : 2 * nw]
        send_sems, recv_sems = refs[2 * nw :]
        x, y, c = _mesh_pos()
        cps = [pltpu.make_async_remote_copy(
            src_ref=ins[w], dst_ref=outs[w], send_sem=send_sems.at[w], recv_sem=recv_sems.at[w],
            device_id=(x, y, 1 - c), device_id_type=MESH_ID) for w in range(nw)]
        for cp in cps:
            cp.start()
        for cp in cps:
            cp.wait()

    anyspec = pl.BlockSpec(memory_space=pl.ANY)
    return pl.pallas_call(
        body,
        out_shape=out_shapes,
        in_specs=[anyspec] * nw,
        out_specs=[anyspec] * nw,
        scratch_shapes=[pltpu.SemaphoreType.DMA((nw,)), pltpu.SemaphoreType.DMA((nw,))],
        name=name,
    )(*halves)


def _pad_wdq(w):
    z = lambda n: jnp.zeros((w.shape[0], n), w.dtype)
    base = Q_LORA + KV_LORA
    return jnp.concatenate([w[:, :base], z(KPE_LANE), w[:, base:], z(HEAD_PAD - KPE_LANE - QK_ROPE)], axis=1)


def _unpad_wdq(g):
    base = Q_LORA + KV_LORA
    return jnp.concatenate([g[:, :base], g[:, base + KPE_LANE : base + KPE_LANE + QK_ROPE]], axis=1)


def _pad_wuq(w):
    w3 = w.reshape(Q_LORA, MLA_HEADS, QK_NOPE + QK_ROPE)
    w3 = jnp.pad(w3, ((0, 0), (0, 0), (0, HEAD_PAD - QK_NOPE - QK_ROPE)))
    return w3.reshape(Q_LORA, MLA_HEADS * HEAD_PAD)


def _unpad_wuq(g):
    g3 = g.reshape(Q_LORA, MLA_HEADS, HEAD_PAD)[:, :, : QK_NOPE + QK_ROPE]
    return g3.reshape(Q_LORA, MLA_HEADS * (QK_NOPE + QK_ROPE))


def _pad_wo(w):
    w3 = w.reshape(MLA_HEADS, V_HEAD, D_MODEL)
    w3 = jnp.pad(w3, ((0, 0), (HEAD_PAD - V_HEAD, 0), (0, 0)))
    return w3.reshape(MLA_HEADS * HEAD_PAD, D_MODEL)


def _unpad_wo(g):
    g3 = g.reshape(MLA_HEADS, HEAD_PAD, D_MODEL)[:, HEAD_PAD - V_HEAD :, :]
    return g3.reshape(MLA_HEADS * V_HEAD, D_MODEL)


def _local_step(x, mem, positions, target, wb, ws):
    t = x.shape[0]
    tables = _rope_tables(positions.reshape(t, 1), "rope_tables")
    saved = []
    for l in range(DEPTH):
        s = {"x0": x}
        h1 = _rms_fwd(x, ws["norm_mix_g"][l], f"l{l}_norm_mix")
        s["h1"] = h1
        if l % 2 == 0:
            e = l // 2
            z = _matmul(h1, wb["pc_w_in"], "nn", F32, f"l{l}_pc_in", layer=e)
            cat = _mix_fwd(z, ws["pool_w"][e], ws["pool_scale"][e], ws["conv_dw_w"][e], ws["conv_dw_b"][e],
                           ws["conv_ln_g"][e], ws["conv_ln_b"][e], f"l{l}_mix")
            x = _matmul(cat, wb["pc_w_out"], "nn", F32, f"l{l}_pc_out", layer=e, res=x)
            s.update(z=z, cat=cat)
        else:
            o = l // 2
            cp = _matmul(h1, wb["mla_wdq"], "nn", F32, f"l{l}_mla_dq", layer=o)
            qn, kvn, kpe = _mla_prep(cp, ws["mla_q_norm_g"][o], ws["mla_kv_norm_g"][o], tables, f"l{l}_mla_prep")
            q = _matmul(qn, wb["mla_wuq"], "nn", F32, f"l{l}_mla_uq", layer=o)
            qr = _rope_heads(q, tables, 1.0, MLA_SCALE * LOG2E, f"l{l}_mla_rope")
            kv = _matmul(kvn, wb["mla_w_ukv"], "nn", MXU_DT, f"l{l}_mla_ukv", layer=o)
            att, lse = _flash_fwd(qr, kv, kpe, f"l{l}_mla_attn")
            x = _matmul(att, wb["mla_wo"], "nn", F32, f"l{l}_mla_o", layer=o, res=x)
            s.update(cp=cp, qn=qn, kvn=kvn, kpe=kpe, qr=qr, kv=kv, att=att, lse=lse)
        s["x1"] = x
        h2 = _rms_fwd(x, ws["norm_xa_g"][l], f"l{l}_norm_xa")
        hm = _rms_fwd(mem, ws["norm_mem_g"][l], f"l{l}_norm_mem")
        q2 = _matmul(h2, wb["xa_wq"], "nn", MXU_DT, f"l{l}_xa_q", layer=l)
        kvm = _matmul(hm, wb["xa_wkv"], "nn", MXU_DT, f"l{l}_xa_kv", layer=l)
        o2 = _xa_fwd(q2, kvm, f"l{l}_xa_attn")
        x = _matmul(o2, wb["xa_wo"], "nn", F32, f"l{l}_xa_o", layer=l, res=x)
        s.update(h2=h2, hm=hm, q2=q2, kvm=kvm, o2=o2, x2=x)
        h3 = _rms_fwd(x, ws["norm_ffn_g"][l], f"l{l}_norm_ffn")
        up = _matmul(h3, wb["ffn_w_up"], "nn", F32, f"l{l}_ffn_up", layer=l)
        act = _ffn_fwd(up, ws["ffn_conv_w"][l], ws["ffn_conv_b"][l], f"l{l}_ffn_mid")
        x = _matmul(act, wb["ffn_w_down"], "nn", F32, f"l{l}_ffn_down", layer=l, res=x)
        s.update(h3=h3, up=up, act=act)
        saved.append(s)

    dx, dg_final, loss = _loss_head(x, target, ws["final_norm_g"], "loss_head")
    g = {k: [None] * DEPTH for k in ("norm_mix_g", "norm_xa_g", "norm_mem_g", "xa_wq", "xa_wkv", "xa_wo", "norm_ffn_g",
                                      "ffn_w_up", "ffn_conv_w", "ffn_conv_b", "ffn_w_down")}
    g.update({k: [None] * (DEPTH // 2) for k in ("pc_w_in", "pool_w", "pool_scale", "conv_dw_w", "conv_dw_b", "conv_ln_g",
                                                 "conv_ln_b", "pc_w_out", "mla_w_dq_dkv", "mla_q_norm_g", "mla_w_uq",
                                                 "mla_kv_norm_g", "mla_w_ukv", "mla_w_o")})
    for l in reversed(range(DEPTH)):
        s = saved[l]
        dact = _matmul(dx, wb["ffn_w_down"], "nt", F32, f"l{l}_b_ffn_dact", layer=l)
        g["ffn_w_down"][l] = _matmul(s["act"], dx, "tn", F32, f"l{l}_b_ffn_dwdown")
        dup, dcw, dcb = _ffn_bwd(s["up"], dact, ws["ffn_conv_w"][l], ws["ffn_conv_b"][l], f"l{l}_b_ffn_mid")
        g["ffn_conv_w"][l], g["ffn_conv_b"][l] = dcw, dcb[0]
        g["ffn_w_up"][l] = _matmul(s["h3"], dup, "tn", F32, f"l{l}_b_ffn_dwup")
        dh = _matmul(dup, wb["ffn_w_up"], "nt", F32, f"l{l}_b_ffn_dh", layer=l)
        dx, dg = _rms_bwd(dh, s["x2"], ws["norm_ffn_g"][l], dx, f"l{l}_b_norm_ffn")
        g["norm_ffn_g"][l] = dg[0]
        do2 = _matmul(dx, wb["xa_wo"], "nt", MXU_DT, f"l{l}_b_xa_do", layer=l)
        g["xa_wo"][l] = _matmul(s["o2"], dx, "tn", F32, f"l{l}_b_xa_dwo")
        dq2, dkvm = _xa_bwd(s["q2"], s["kvm"], do2, f"l{l}_b_xa_attn")
        g["xa_wq"][l] = _matmul(s["h2"], dq2, "tn", F32, f"l{l}_b_xa_dwq")
        dh = _matmul(dq2, wb["xa_wq"], "nt", F32, f"l{l}_b_xa_dh", layer=l)
        g["xa_wkv"][l] = _matmul(s["hm"], dkvm, "tn", F32, f"l{l}_b_xa_dwkv")
        dhm = _matmul(dkvm, wb["xa_wkv"], "nt", F32, f"l{l}_b_xa_dhm", layer=l)
        g["norm_mem_g"][l] = _rms_bwd_gain(dhm, mem, ws["norm_mem_g"][l], f"l{l}_b_norm_mem")[0]
        dx, dg = _rms_bwd(dh, s["x1"], ws["norm_xa_g"][l], dx, f"l{l}_b_norm_xa")
        g["norm_xa_g"][l] = dg[0]
        if l % 2 == 0:
            e = l // 2
            dcat = _matmul(dx, wb["pc_w_out"], "nt", F32, f"l{l}_b_pc_dcat", layer=e)
            g["pc_w_out"][e] = _matmul(s["cat"], dx, "tn", F32, f"l{l}_b_pc_dwout")
            dz, dpw, dps, dww, dwb, dlg, dlb = _mix_bwd(
                s["z"], dcat, ws["pool_w"][e], ws["pool_scale"][e], ws["conv_dw_w"][e], ws["conv_dw_b"][e],
                ws["conv_ln_g"][e], ws["conv_ln_b"][e], f"l{l}_b_mix")
            g["pool_w"][e], g["pool_scale"][e], g["conv_dw_w"][e] = dpw, dps[0], dww
            g["conv_dw_b"][e], g["conv_ln_g"][e], g["conv_ln_b"][e] = dwb[0], dlg[0], dlb[0]
            g["pc_w_in"][e] = _matmul(s["h1"], dz, "tn", F32, f"l{l}_b_pc_dwin")
            dh = _matmul(dz, wb["pc_w_in"], "nt", F32, f"l{l}_b_pc_dh", layer=e)
        else:
            o = l // 2
            do = _matmul(dx, wb["mla_wo"], "nt", MXU_DT, f"l{l}_b_mla_do", layer=o)
            g["mla_w_o"][o] = _unpad_wo(_matmul(s["att"], dx, "tn", F32, f"l{l}_b_mla_dwo"))
            dqr, dkv, dkpe = _flash_bwd(s["qr"], s["kv"], s["kpe"], s["att"], do, s["lse"], f"l{l}_b_mla_attn")
            dq = _rope_heads(dqr, tables, -1.0, MLA_SCALE, f"l{l}_b_mla_rope")
            g["mla_w_uq"][o] = _unpad_wuq(_matmul(s["qn"], dq, "tn", F32, f"l{l}_b_mla_dwuq"))
            dqn = _matmul(dq, wb["mla_wuq"], "nt", F32, f"l{l}_b_mla_dqn", layer=o)
            g["mla_w_ukv"][o] = _matmul(s["kvn"], dkv, "tn", F32, f"l{l}_b_mla_dwukv")
            dkvn = _matmul(dkv, wb["mla_w_ukv"], "nt", F32, f"l{l}_b_mla_dkvn", layer=o)
            dcp, dqg, dkvg = _mla_prep_bwd(s["cp"], dqn, dkvn, dkpe, ws["mla_q_norm_g"][o], ws["mla_kv_norm_g"][o],
                                           tables, f"l{l}_b_mla_prep")
            g["mla_q_norm_g"][o], g["mla_kv_norm_g"][o] = dqg[0], dkvg[0]
            g["mla_w_dq_dkv"][o] = _unpad_wdq(_matmul(s["h1"], dcp, "tn", F32, f"l{l}_b_mla_dwdq"))
            dh = _matmul(dcp, wb["mla_wdq"], "nt", F32, f"l{l}_b_mla_dh", layer=o)
        dx, dg = _rms_bwd(dh, s["x0"], ws["norm_mix_g"][l], dx, f"l{l}_b_norm_mix")
        g["norm_mix_g"][l] = dg[0]
    grads = {k: jnp.stack(v) for k, v in g.items()}
    grads["final_norm_g"] = dg_final[0]
    return loss, dx, grads


BIG = (
    ("xa_wq", "row"), ("xa_wkv", "col"), ("xa_wo", "row"), ("ffn_w_up", "col"), ("ffn_w_down", "row"),
    ("pc_w_in", "col"), ("pc_w_out", "row"), ("mla_w_dq_dkv", "row"), ("mla_w_uq", "col"), ("mla_w_ukv", "col"),
    ("mla_w_o", "row"),
)
SMALL_SHARDED = ("ffn_conv_w", "conv_dw_w", "mla_q_norm_g", "mla_kv_norm_g")
SMALL_REPLICATED = ("norm_mix_g", "norm_xa_g", "norm_mem_g", "norm_ffn_g", "ffn_conv_b", "pool_w", "pool_scale",
                    "conv_dw_b", "conv_ln_g", "conv_ln_b", "final_norm_g")
WEIGHTS = ("norm_mix_g", "norm_xa_g", "norm_mem_g", "xa_wq", "xa_wkv", "xa_wo", "norm_ffn_g", "ffn_w_up", "ffn_conv_w",
           "ffn_conv_b", "ffn_w_down", "pc_w_in", "pool_w", "pool_scale", "conv_dw_w", "conv_dw_b", "conv_ln_g",
           "conv_ln_b", "pc_w_out", "mla_w_dq_dkv", "mla_q_norm_g", "mla_w_uq", "mla_kv_norm_g", "mla_w_ukv", "mla_w_o",
           "final_norm_g")
PACK_ROW = SUBLANE * LANE


def _pack(arrays):
    flat = jnp.concatenate([a.reshape(-1) for a in arrays])
    n = flat.shape[0]
    pad = (-n) % PACK_ROW
    return jnp.pad(flat, (0, pad)).reshape(-1, LANE)


def _unpack(flat, shapes):
    out, off = [], 0
    for s in shapes:
        n = int(np.prod(s))
        out.append(flat[off : off + n].reshape(s))
        off += n
    return out


def kernel(x, mem, positions, norm_mix_g, norm_xa_g, norm_mem_g, xa_wq, xa_wkv, xa_wo, norm_ffn_g, ffn_w_up, ffn_conv_w, ffn_conv_b, ffn_w_down, pc_w_in, pool_w, pool_scale, conv_dw_w, conv_dw_b, conv_ln_g, conv_ln_b, pc_w_out, mla_w_dq_dkv, mla_q_norm_g, mla_w_uq, mla_kv_norm_g, mla_w_ukv, mla_w_o, final_norm_g, loss_target, m_norm_mix_g, m_norm_xa_g, m_norm_mem_g, m_xa_wq, m_xa_wkv, m_xa_wo, m_norm_ffn_g, m_ffn_w_up, m_ffn_conv_w, m_ffn_conv_b, m_ffn_w_down, m_pc_w_in, m_pool_w, m_pool_scale, m_conv_dw_w, m_conv_dw_b, m_conv_ln_g, m_conv_ln_b, m_pc_w_out, m_mla_w_dq_dkv, m_mla_q_norm_g, m_mla_w_uq, m_mla_kv_norm_g, m_mla_w_ukv, m_mla_w_o, m_final_norm_g, v_norm_mix_g, v_norm_xa_g, v_norm_mem_g, v_xa_wq, v_xa_wkv, v_xa_wo, v_norm_ffn_g, v_ffn_w_up, v_ffn_conv_w, v_ffn_conv_b, v_ffn_w_down, v_pc_w_in, v_pool_w, v_pool_scale, v_conv_dw_w, v_conv_dw_b, v_conv_ln_g, v_conv_ln_b, v_pc_w_out, v_mla_w_dq_dkv, v_mla_q_norm_g, v_mla_w_uq, v_mla_kv_norm_g, v_mla_w_ukv, v_mla_w_o, v_final_norm_g):
    args = dict(locals())
    w = {n: args[n] for n in WEIGHTS}
    m = {n: args["m_" + n] for n in WEIGHTS}
    v = {n: args["v_" + n] for n in WEIGHTS}
    cx, cy, cc = lax.axis_index("x"), lax.axis_index("y"), lax.axis_index("c")
    chip = 2 * cx + cy

    full = _all_gather_weights([w[n].astype(MXU_DT) for n, _ in BIG], [k for _, k in BIG], "gather_weights")
    full = dict(zip([n for n, _ in BIG], full))
    small_shapes = [w[n].shape for n in SMALL_SHARDED]
    gathered = _all_gather_rows(_pack([w[n] for n in SMALL_SHARDED]), "gather_small")
    gathered = gathered.reshape(8, -1)
    ws = {n: w[n] for n in SMALL_REPLICATED}
    pieces = [_unpack(gathered[2 * k], small_shapes) for k in range(4)]
    for i, n in enumerate(SMALL_SHARDED):
        ws[n] = jnp.concatenate([pieces[k][i] for k in range(4)], axis=-1)
    wb = {n: full[n] for n in ("xa_wq", "xa_wkv", "xa_wo", "ffn_w_up", "ffn_w_down", "pc_w_in", "pc_w_out", "mla_w_ukv")}
    wb["mla_wdq"] = jnp.stack([_pad_wdq(full["mla_w_dq_dkv"][o]) for o in range(DEPTH // 2)])
    wb["mla_wuq"] = jnp.stack([_pad_wuq(full["mla_w_uq"][o]) for o in range(DEPTH // 2)])
    wb["mla_wo"] = jnp.stack([_pad_wo(full["mla_w_o"][o]) for o in range(DEPTH // 2)])

    loss, grad_x, grads = _local_step(x[0], mem[0], positions[0], loss_target[0], wb, ws)
    loss = lax.psum(loss[0, 0], ("x", "y", "c"))

    kinds = [k for _, k in BIG]
    big = [grads[n] for n, _ in BIG]
    c_idx = cc.reshape(1).astype(jnp.int32)
    chip_idx = chip.reshape(1).astype(jnp.int32)
    theirs = _exchange_halves(big, "reduce_pair")
    pair = [_add_half(gr, th, c_idx, f"reduce_pair_add_{n}") for gr, th, (n, _) in zip(big, theirs, BIG)]
    slots = _scatter_to_chips(pair, kinds, "reduce_chips")
    halves = [_sum_chips(sl, pr, chip_idx, kind, f"reduce_chips_add_{n}")
              for sl, pr, (n, kind) in zip(slots, pair, BIG)]
    others = _swap_halves(halves, "reduce_join")
    gsum, delta, new_m, new_v = {}, {}, {}, {}
    for mine, other, (n, _) in zip(halves, others, BIG):
        gsum[n], delta[n], new_m[n], new_v[n] = _adamw_halves(w[n], mine, other, m[n], v[n], c_idx, f"adamw_{n}")

    small_names = SMALL_REPLICATED + SMALL_SHARDED
    small_grad_shapes = [grads[n].shape for n in small_names]
    packed = _pack([grads[n] for n in small_names])
    rows = packed.shape[0]
    allparts = _all_gather_rows(packed, "gather_small_grads").reshape(8, rows, LANE)
    total = _sum_leading(allparts, "sum_small_grads").reshape(-1)
    for n, gfull in zip(small_names, _unpack(total, small_grad_shapes)):
        if n in SMALL_SHARDED:
            width = w[n].shape[-1]
            gfull = lax.dynamic_slice_in_dim(gfull, chip * width, width, axis=gfull.ndim - 1)
        gsum[n] = gfull

    for n in SMALL_REPLICATED + SMALL_SHARDED:
        delta[n], new_m[n], new_v[n] = _adamw(w[n], gsum[n], m[n], v[n], f"adamw_{n}")
    return (loss, grad_x[None], *[gsum[n] for n in WEIGHTS], *[delta[n] for n in WEIGHTS],
            *[new_m[n] for n in WEIGHTS], *[new_v[n] for n in WEIGHTS])
```

```python
import functools
import math

import numpy as np
import jax
import jax.numpy as jnp
from jax import lax
from jax.experimental import pallas as pl
from jax.experimental.pallas import tpu as pltpu

F32 = jnp.float32
MXU_DT = jnp.bfloat16
XFER_DT = jnp.bfloat16

D_MODEL = 1024
DEPTH = 4
MEM_LEN = 256
XA_HEADS = 4
XA_HEAD_DIM = 256
POOL_W = 512
POOL_WINDOWS = (2, 4, 8, 16)
POOL_GROUP = 128
CONV_W = 512
CONV_K = 31
MLA_HEADS = 16
QK_NOPE = 64
QK_ROPE = 32
V_HEAD = 64
Q_LORA = 384
KV_LORA = 256
ROPE_THETA = 10000.0
MLA_SCALE = 1.0 / math.sqrt(QK_NOPE + QK_ROPE)
LOG2E = math.log2(math.e)
D_FF = 2816
FFN_CONV_K = 3
EPS = 1e-6
NEG = -1e30
ADAM_LR = 0.001
ADAM_B1 = 0.9
ADAM_B2 = 0.999
ADAM_EPS = 1e-08
ADAM_WD = 0.01
ADAM_STEP = 10

HEAD_PAD = 128
C_PAD = 768
KPE_LANE = 64

VMEM_LIMIT = 52 * 1024 * 1024
BLOCK_BYTES = 6 * 1024 * 1024
LANE = 128
SUBLANE = 8

TM = 1024
TN = 1408
TK = 1024
TT = 512
TW = 256
TWF = 64
FFN_CHUNK = 256
TA = 512
TAQ = 1024
MIX_HALO = 32
FFN_HALO = 8

NN = (((1,), (0,)), ((), ()))
NT = (((1,), (1,)), ((), ()))
TN_DIMS = (((0,), (0,)), ((), ()))
MESH_ID = pl.DeviceIdType.MESH


def _cparams(*sem):
    return pltpu.CompilerParams(dimension_semantics=sem, vmem_limit_bytes=VMEM_LIMIT)


def _tile(n, pref, limit=None):
    cap = pref if limit is None else min(pref, limit)
    if n <= cap:
        return n
    t = (cap // LANE) * LANE
    while t >= LANE:
        if n % t == 0:
            return t
        t -= LANE
    return n


def _rows(t, pref):
    return t if t <= pref else pref


def _sigmoid(x):
    return 1.0 / (1.0 + jnp.exp(-x))


def _matmul(a, b, mode, out_dtype, name, layer=None, res=None):
    if layer is None:
        b2 = b.shape
    else:
        b2 = b.shape[1:]
    if mode == "tn":
        k, m = a.shape
        k2, n = b2
    elif mode == "nn":
        m, k = a.shape
        k2, n = b2
    else:
        m, k = a.shape
        n, k2 = b2
    assert k == k2, (a.shape, b.shape, mode)
    isz_a = jnp.dtype(a.dtype).itemsize
    isz_b = jnp.dtype(b.dtype).itemsize
    if mode == "tn":
        tk = _tile(k, TK)
        tm = _tile(m, TN, BLOCK_BYTES // (tk * isz_a))
        tn = _tile(n, TN, BLOCK_BYTES // (tk * isz_b))
    else:
        tk = k
        tn = _tile(n, TN, BLOCK_BYTES // (tk * isz_b))
        tm = _tile(m, TM, min(BLOCK_BYTES // (tk * isz_a), BLOCK_BYTES // (tn * 4)))
    nk = k // tk
    grid = (m // tm, n // tn, nk)
    if mode == "nn":
        a_spec = pl.BlockSpec((tm, tk), lambda i, j, kk: (i, kk))
        b_blk, b_map, dn = (tk, tn), (lambda i, j, kk: (kk, j)), NN
    elif mode == "nt":
        a_spec = pl.BlockSpec((tm, tk), lambda i, j, kk: (i, kk))
        b_blk, b_map, dn = (tn, tk), (lambda i, j, kk: (j, kk)), NT
    else:
        a_spec = pl.BlockSpec((tk, tm), lambda i, j, kk: (kk, i))
        b_blk, b_map, dn = (tk, tn), (lambda i, j, kk: (kk, j)), TN_DIMS
    if layer is None:
        b_spec = pl.BlockSpec(b_blk, b_map)
    else:
        b_spec = pl.BlockSpec((None,) + b_blk, lambda i, j, kk: (layer,) + b_map(i, j, kk))
    o_spec = pl.BlockSpec((tm, tn), lambda i, j, kk: (i, j))
    in_specs = [a_spec, b_spec]
    args = [a, b]
    if res is not None:
        in_specs.append(pl.BlockSpec((tm, tn), lambda i, j, kk: (i, j)))
        args.append(res)
    has_res = res is not None

    def body(*refs):
        a_ref, b_ref = refs[0], refs[1]
        r_ref = refs[2] if has_res else None
        o_ref = refs[3] if has_res else refs[2]
        p = lax.dot_general(a_ref[...].astype(MXU_DT), b_ref[...].astype(MXU_DT), dn, preferred_element_type=F32)
        if nk == 1:
            if has_res:
                p = r_ref[...] + p
            o_ref[...] = p.astype(o_ref.dtype)
        else:
            acc_ref = refs[-1]
            kk = pl.program_id(2)

            @pl.when(kk == 0)
            def _():
                acc_ref[...] = jnp.zeros_like(acc_ref)

            acc_ref[...] += p

            @pl.when(kk == nk - 1)
            def _():
                r = acc_ref[...]
                if has_res:
                    r = r_ref[...] + r
                o_ref[...] = r.astype(o_ref.dtype)

    scratch = [pltpu.VMEM((tm, tn), F32)] if nk > 1 else []
    return pl.pallas_call(
        body,
        grid=grid,
        in_specs=in_specs,
        out_specs=o_spec,
        out_shape=jax.ShapeDtypeStruct((m, n), out_dtype),
        scratch_shapes=scratch,
        name=name,
        compiler_params=_cparams("parallel", "parallel", "arbitrary"),
    )(*args)


def _rms_fwd(x, g, name):
    t, d = x.shape
    tt = _rows(t, TT)

    def body(x_ref, g_ref, o_ref):
        xf = x_ref[...]
        r = lax.rsqrt(jnp.mean(xf * xf, axis=-1, keepdims=True) + EPS)
        o_ref[...] = ((xf * r) * g_ref[...]).astype(o_ref.dtype)

    return pl.pallas_call(
        body,
        grid=(t // tt,),
        in_specs=[pl.BlockSpec((tt, d), lambda i: (i, 0)), pl.BlockSpec((1, d), lambda i: (0, 0))],
        out_specs=pl.BlockSpec((tt, d), lambda i: (i, 0)),
        out_shape=jax.ShapeDtypeStruct((t, d), MXU_DT),
        name=name,
        compiler_params=_cparams("parallel"),
    )(x, g.reshape(1, d))


def _rms_bwd(dh, x, g, dx_in, name):
    t, d = x.shape
    tt = _rows(t, TT)

    def body(dh_ref, x_ref, g_ref, dxi_ref, dx_ref, dg_ref):
        @pl.when(pl.program_id(0) == 0)
        def _():
            dg_ref[...] = jnp.zeros_like(dg_ref)

        xf = x_ref[...]
        dh_v = dh_ref[...]
        r = lax.rsqrt(jnp.mean(xf * xf, axis=-1, keepdims=True) + EPS)
        xh = xf * r
        gy = dh_v * g_ref[...]
        dx = r * (gy - xh * jnp.mean(gy * xh, axis=-1, keepdims=True))
        dx_ref[...] = dxi_ref[...] + dx
        dg_ref[...] += jnp.sum(dh_v * xh, axis=0, keepdims=True)

    row = pl.BlockSpec((tt, d), lambda i: (i, 0))
    vec = pl.BlockSpec((1, d), lambda i: (0, 0))
    return pl.pallas_call(
        body,
        grid=(t // tt,),
        in_specs=[row, row, vec, row],
        out_specs=[row, vec],
        out_shape=[jax.ShapeDtypeStruct((t, d), F32), jax.ShapeDtypeStruct((1, d), F32)],
        name=name,
        compiler_params=_cparams("arbitrary"),
    )(dh, x, g.reshape(1, d), dx_in)


def _rms_bwd_gain(dh, x, g, name):
    t, d = x.shape
    tt = _rows(t, TT)

    def body(dh_ref, x_ref, dg_ref):
        @pl.when(pl.program_id(0) == 0)
        def _():
            dg_ref[...] = jnp.zeros_like(dg_ref)

        xf = x_ref[...]
        r = lax.rsqrt(jnp.mean(xf * xf, axis=-1, keepdims=True) + EPS)
        dg_ref[...] += jnp.sum(dh_ref[...] * (xf * r), axis=0, keepdims=True)

    row = pl.BlockSpec((tt, d), lambda i: (i, 0))
    vec = pl.BlockSpec((1, d), lambda i: (0, 0))
    return pl.pallas_call(
        body,
        grid=(t // tt,),
        in_specs=[row, row],
        out_specs=vec,
        out_shape=jax.ShapeDtypeStruct((1, d), F32),
        name=name,
        compiler_params=_cparams("arbitrary"),
    )(dh, x)


def _loss_head(x, target, g, name):
    t, d = x.shape
    tt = _rows(t, TT)

    def body(x_ref, t_ref, g_ref, dx_ref, dg_ref, loss_ref):
        @pl.when(pl.program_id(0) == 0)
        def _():
            dg_ref[...] = jnp.zeros_like(dg_ref)
            loss_ref[...] = jnp.zeros_like(loss_ref)

        xf = x_ref[...]
        gv = g_ref[...]
        r = lax.rsqrt(jnp.mean(xf * xf, axis=-1, keepdims=True) + EPS)
        xh = xf * r
        err = xh * gv - t_ref[...]
        e2 = jnp.sum(err * err, axis=-1, keepdims=True)
        loss_ref[...] += (0.5 / d) * jnp.sum(e2, axis=0, keepdims=True)
        dy = err * (1.0 / d)
        gy = dy * gv
        dx_ref[...] = r * (gy - xh * jnp.mean(gy * xh, axis=-1, keepdims=True))
        dg_ref[...] += jnp.sum(dy * xh, axis=0, keepdims=True)

    row = pl.BlockSpec((tt, d), lambda i: (i, 0))
    vec = pl.BlockSpec((1, d), lambda i: (0, 0))
    return pl.pallas_call(
        body,
        grid=(t // tt,),
        in_specs=[row, row, vec],
        out_specs=[row, vec, pl.BlockSpec((1, 1), lambda i: (0, 0))],
        out_shape=[
            jax.ShapeDtypeStruct((t, d), F32),
            jax.ShapeDtypeStruct((1, d), F32),
            jax.ShapeDtypeStruct((1, 1), F32),
        ],
        name=name,
        compiler_params=_cparams("arbitrary"),
    )(x, target, g.reshape(1, d))


def _prev_halo(tt, hp, width):
    return pl.BlockSpec((hp, width), lambda i: (jnp.maximum(i * (tt // hp) - 1, 0), 0))


def _next_halo(tt, hp, width, t):
    return pl.BlockSpec((hp, width), lambda i: (jnp.minimum((i + 1) * (tt // hp), t // hp - 1), 0))


def _ffn_chunks():
    return [(c0, FFN_CHUNK) for c0 in range(0, D_FF, FFN_CHUNK)]


def _ffn_fwd(up, conv_w, conv_b, name):
    t = up.shape[0]
    tt = _rows(t, TWF)
    hp = FFN_HALO

    def body(up_ref, gp_ref, w_ref, b_ref, act_ref, ext_ref):
        first = pl.program_id(0) == 0
        for c0, cw in _ffn_chunks():
            ga = pl.ds(D_FF + c0, cw)
            ext_ref[0:hp, :] = jnp.where(first, 0.0, gp_ref[:, ga])
            ext_ref[hp : hp + tt, :] = up_ref[:, ga]
            gc = b_ref[:, pl.ds(c0, cw)]
            for j in range(FFN_CONV_K):
                off = hp - (FFN_CONV_K - 1) + j
                gc = gc + w_ref[j : j + 1, pl.ds(c0, cw)] * ext_ref[off : off + tt, :]
            a = up_ref[:, pl.ds(c0, cw)]
            act_ref[:, pl.ds(c0, cw)] = (gc * _sigmoid(gc) * a).astype(act_ref.dtype)

    return pl.pallas_call(
        body,
        grid=(t // tt,),
        in_specs=[
            pl.BlockSpec((tt, 2 * D_FF), lambda i: (i, 0)),
            _prev_halo(tt, hp, 2 * D_FF),
            pl.BlockSpec((FFN_CONV_K, D_FF), lambda i: (0, 0)),
            pl.BlockSpec((1, D_FF), lambda i: (0, 0)),
        ],
        out_specs=pl.BlockSpec((tt, D_FF), lambda i: (i, 0)),
        out_shape=jax.ShapeDtypeStruct((t, D_FF), MXU_DT),
        scratch_shapes=[pltpu.VMEM((tt + hp, FFN_CHUNK), F32)],
        name=name,
        compiler_params=_cparams("parallel"),
    )(up, up, conv_w, conv_b.reshape(1, D_FF))


def _ffn_bwd(up, dact, conv_w, conv_b, name):
    t = up.shape[0]
    tt = _rows(t, TWF)
    hp = FFN_HALO
    nt = t // tt
    kk = FFN_CONV_K

    def body(up_ref, upp_ref, upn_ref, da_ref, dan_ref, w_ref, b_ref, dup_ref, dw_ref, db_ref, ext_ref, dgc_ref):
        i = pl.program_id(0)
        first = i == 0
        last = i == nt - 1

        @pl.when(first)
        def _():
            dw_ref[...] = jnp.zeros_like(dw_ref)
            db_ref[...] = jnp.zeros_like(db_ref)

        for c0, cw in _ffn_chunks():
            ca = pl.ds(c0, cw)
            ga = pl.ds(D_FF + c0, cw)
            ext_ref[0:hp, :] = jnp.where(first, 0.0, upp_ref[:, ga])
            ext_ref[hp : hp + tt, :] = up_ref[:, ga]
            ext_ref[hp + tt : hp + tt + hp, :] = upn_ref[:, ga]
            gc = b_ref[:, ca]
            for j in range(kk):
                off = hp - (kk - 1) + j
                gc = gc + w_ref[j : j + 1, ca] * ext_ref[off : off + tt + hp, :]
            sg = _sigmoid(gc)
            silu = gc * sg
            dsilu = sg * (1.0 + gc * (1.0 - sg))
            a_all = jnp.concatenate([up_ref[:, ca], upn_ref[:, ca]], axis=0)
            dact_all = jnp.concatenate([da_ref[:, ca], jnp.where(last, 0.0, dan_ref[:, ca])], axis=0)
            dgc = dact_all * a_all * dsilu
            dgc_ref[...] = dgc
            dup_ref[:, ca] = (dact_all[0:tt] * silu[0:tt]).astype(dup_ref.dtype)
            dg = jnp.zeros((tt, cw), F32)
            for j in range(kk):
                dg = dg + w_ref[j : j + 1, ca] * dgc_ref[kk - 1 - j : kk - 1 - j + tt, :]
            dup_ref[:, ga] = dg.astype(dup_ref.dtype)
            dgc_t = dgc[0:tt]
            db_ref[:, ca] += jnp.sum(dgc_t, axis=0, keepdims=True)
            for j in range(kk):
                off = hp - (kk - 1) + j
                dw_ref[j : j + 1, ca] += jnp.sum(dgc_t * ext_ref[off : off + tt, :], axis=0, keepdims=True)

    return pl.pallas_call(
        body,
        grid=(nt,),
        in_specs=[
            pl.BlockSpec((tt, 2 * D_FF), lambda i: (i, 0)),
            _prev_halo(tt, hp, 2 * D_FF),
            _next_halo(tt, hp, 2 * D_FF, t),
            pl.BlockSpec((tt, D_FF), lambda i: (i, 0)),
            _next_halo(tt, hp, D_FF, t),
            pl.BlockSpec((kk, D_FF), lambda i: (0, 0)),
            pl.BlockSpec((1, D_FF), lambda i: (0, 0)),
        ],
        out_specs=[
            pl.BlockSpec((tt, 2 * D_FF), lambda i: (i, 0)),
            pl.BlockSpec((kk, D_FF), lambda i: (0, 0)),
            pl.BlockSpec((1, D_FF), lambda i: (0, 0)),
        ],
        out_shape=[
            jax.ShapeDtypeStruct((t, 2 * D_FF), MXU_DT),
            jax.ShapeDtypeStruct((kk, D_FF), F32),
            jax.ShapeDtypeStruct((1, D_FF), F32),
        ],
        scratch_shapes=[pltpu.VMEM((tt + 2 * hp, FFN_CHUNK), F32), pltpu.VMEM((tt + hp, FFN_CHUNK), F32)],
        name=name,
        compiler_params=_cparams("arbitrary"),
    )(up, up, up, dact, dact, conv_w, conv_b.reshape(1, D_FF))


def _layernorm_silu(cv, ln_g, ln_b):
    mu = jnp.mean(cv, axis=-1, keepdims=True)
    xc = cv - mu
    rstd = lax.rsqrt(jnp.mean(xc * xc, axis=-1, keepdims=True) + EPS)
    xh = xc * rstd
    a = xh * ln_g + ln_b
    return xh, rstd, a


def _mix_fwd(z, pool_w, pool_scale, dw_w, dw_b, ln_g, ln_b, name):
    t = z.shape[0]
    tt = _rows(t, TW)
    hp = MIX_HALO
    zw = POOL_W + 2 * CONV_W

    def body(z_ref, zp_ref, pw_ref, ps_ref, w_ref, b_ref, lg_ref, lb_ref, cat_ref, eu_ref, egl_ref):
        i = pl.program_id(0)
        first = i == 0
        eu_ref[0:hp, :] = jnp.where(first, 0.0, zp_ref[:, 0:POOL_W])
        eu_ref[hp : hp + tt, :] = z_ref[:, 0:POOL_W]
        glp = zp_ref[:, POOL_W : POOL_W + CONV_W] * _sigmoid(zp_ref[:, POOL_W + CONV_W : zw])
        egl_ref[0:hp, :] = jnp.where(first, 0.0, glp)
        egl_ref[hp : hp + tt, :] = z_ref[:, POOL_W : POOL_W + CONV_W] * _sigmoid(z_ref[:, POOL_W + CONV_W : zw])
        row = i * tt + lax.broadcasted_iota(jnp.int32, (tt, 1), 0)
        for gi, w in enumerate(POOL_WINDOWS):
            cols = pl.ds(gi * POOL_GROUP, POOL_GROUP)
            u = eu_ref[hp : hp + tt, cols]
            acc = u
            for k in range(1, w):
                acc = acc + eu_ref[hp - k : hp - k + tt, cols]
            cnt = jnp.minimum(row + 1, w).astype(F32)
            pooled = acc / cnt - u
            y = jnp.dot(pooled.astype(MXU_DT), pw_ref[gi].astype(MXU_DT), preferred_element_type=F32)
            cat_ref[:, cols] = (y * ps_ref[:, cols]).astype(cat_ref.dtype)
        cv = b_ref[...]
        for j in range(CONV_K):
            off = hp - (CONV_K - 1) + j
            cv = cv + w_ref[j : j + 1, :] * egl_ref[off : off + tt, :]
        _, _, a = _layernorm_silu(cv, lg_ref[...], lb_ref[...])
        cat_ref[:, POOL_W : POOL_W + CONV_W] = (a * _sigmoid(a)).astype(cat_ref.dtype)

    vec = pl.BlockSpec((1, CONV_W), lambda i: (0, 0))
    return pl.pallas_call(
        body,
        grid=(t // tt,),
        in_specs=[
            pl.BlockSpec((tt, zw), lambda i: (i, 0)),
            _prev_halo(tt, hp, zw),
            pl.BlockSpec((len(POOL_WINDOWS), POOL_GROUP, POOL_GROUP), lambda i: (0, 0, 0)),
            vec,
            pl.BlockSpec((CONV_K, CONV_W), lambda i: (0, 0)),
            vec,
            vec,
            vec,
        ],
        out_specs=pl.BlockSpec((tt, POOL_W + CONV_W), lambda i: (i, 0)),
        out_shape=jax.ShapeDtypeStruct((t, POOL_W + CONV_W), MXU_DT),
        scratch_shapes=[pltpu.VMEM((tt + hp, POOL_W), F32), pltpu.VMEM((tt + hp, CONV_W), F32)],
        name=name,
        compiler_params=_cparams("parallel"),
    )(z, z, pool_w, pool_scale.reshape(1, POOL_W), dw_w, dw_b.reshape(1, CONV_W), ln_g.reshape(1, CONV_W), ln_b.reshape(1, CONV_W))


def _mix_bwd(z, dcat, pool_w, pool_scale, dw_w, dw_b, ln_g, ln_b, name):
    t = z.shape[0]
    tt = _rows(t, TW)
    hp = MIX_HALO
    nt = t // tt
    zw = POOL_W + 2 * CONV_W
    ng = len(POOL_WINDOWS)

    def body(z_ref, zp_ref, zn_ref, dc_ref, dcn_ref, pw_ref, ps_ref, w_ref, b_ref, lg_ref, lb_ref,
             dz_ref, dpw_ref, dps_ref, dww_ref, dwb_ref, dlg_ref, dlb_ref, eu_ref, ee_ref, egl_ref, edcv_ref):
        i = pl.program_id(0)
        first = i == 0
        last = i == nt - 1

        @pl.when(first)
        def _():
            for r in (dpw_ref, dps_ref, dww_ref, dwb_ref, dlg_ref, dlb_ref):
                r[...] = jnp.zeros_like(r)

        eu_ref[0:hp, :] = jnp.where(first, 0.0, zp_ref[:, 0:POOL_W])
        eu_ref[hp : hp + tt, :] = z_ref[:, 0:POOL_W]
        row = i * tt + lax.broadcasted_iota(jnp.int32, (tt, 1), 0)
        row_ext = i * tt + lax.broadcasted_iota(jnp.int32, (tt + hp, 1), 0)
        for gi, w in enumerate(POOL_WINDOWS):
            cols = pl.ds(gi * POOL_GROUP, POOL_GROUP)
            u = eu_ref[hp : hp + tt, cols]
            acc = u
            for k in range(1, w):
                acc = acc + eu_ref[hp - k : hp - k + tt, cols]
            pooled = (acc / jnp.minimum(row + 1, w).astype(F32) - u).astype(MXU_DT)
            pw = pw_ref[gi].astype(MXU_DT)
            dya = dc_ref[:, cols]
            y = jnp.dot(pooled, pw, preferred_element_type=F32)
            dps_ref[:, cols] += jnp.sum(dya * y, axis=0, keepdims=True)
            scale = ps_ref[:, cols]
            dy_all = jnp.concatenate([dya, jnp.where(last, 0.0, dcn_ref[:, cols])], axis=0) * scale
            dy_all = dy_all.astype(MXU_DT)
            dpw_ref[gi] += lax.dot_general(pooled, dy_all[0:tt], TN_DIMS, preferred_element_type=F32)
            dpooled = lax.dot_general(dy_all, pw, NT, preferred_element_type=F32)
            ee_ref[:, cols] = dpooled / jnp.minimum(row_ext + 1, w).astype(F32)
            du = -dpooled[0:tt]
            for k in range(w):
                du = du + ee_ref[k : k + tt, cols]
            dz_ref[:, cols] = du.astype(dz_ref.dtype)

        ca = slice(POOL_W, POOL_W + CONV_W)
        cb = slice(POOL_W + CONV_W, zw)
        egl_ref[0:hp, :] = jnp.where(first, 0.0, zp_ref[:, ca] * _sigmoid(zp_ref[:, cb]))
        ga = z_ref[:, ca]
        sgb = _sigmoid(z_ref[:, cb])
        egl_ref[hp : hp + tt, :] = ga * sgb
        egl_ref[hp + tt : hp + tt + hp, :] = zn_ref[:, ca] * _sigmoid(zn_ref[:, cb])
        cv = b_ref[...]
        for j in range(CONV_K):
            off = hp - (CONV_K - 1) + j
            cv = cv + w_ref[j : j + 1, :] * egl_ref[off : off + tt + hp, :]
        lg = lg_ref[...]
        xh, rstd, a = _layernorm_silu(cv, lg, lb_ref[...])
        sa = _sigmoid(a)
        dyb = jnp.concatenate([dc_ref[:, ca], jnp.where(last, 0.0, dcn_ref[:, ca])], axis=0)
        da = dyb * (sa * (1.0 + a * (1.0 - sa)))
        dlg_ref[...] += jnp.sum(da[0:tt] * xh[0:tt], axis=0, keepdims=True)
        dlb_ref[...] += jnp.sum(da[0:tt], axis=0, keepdims=True)
        dxh = da * lg
        dcv = rstd * (dxh - jnp.mean(dxh, axis=-1, keepdims=True) - xh * jnp.mean(dxh * xh, axis=-1, keepdims=True))
        edcv_ref[...] = dcv
        dcv_t = dcv[0:tt]
        dwb_ref[...] += jnp.sum(dcv_t, axis=0, keepdims=True)
        dgl = jnp.zeros((tt, CONV_W), F32)
        for j in range(CONV_K):
            off = hp - (CONV_K - 1) + j
            dww_ref[j : j + 1, :] += jnp.sum(dcv_t * egl_ref[off : off + tt, :], axis=0, keepdims=True)
            dgl = dgl + w_ref[j : j + 1, :] * edcv_ref[CONV_K - 1 - j : CONV_K - 1 - j + tt, :]
        dz_ref[:, ca] = (dgl * sgb).astype(dz_ref.dtype)
        dz_ref[:, cb] = (dgl * ga * sgb * (1.0 - sgb)).astype(dz_ref.dtype)

    vec = pl.BlockSpec((1, CONV_W), lambda i: (0, 0))
    pw_spec = pl.BlockSpec((ng, POOL_GROUP, POOL_GROUP), lambda i: (0, 0, 0))
    w_spec = pl.BlockSpec((CONV_K, CONV_W), lambda i: (0, 0))
    return pl.pallas_call(
        body,
        grid=(nt,),
        in_specs=[
            pl.BlockSpec((tt, zw), lambda i: (i, 0)),
            _prev_halo(tt, hp, zw),
            _next_halo(tt, hp, zw, t),
            pl.BlockSpec((tt, POOL_W + CONV_W), lambda i: (i, 0)),
            _next_halo(tt, hp, POOL_W + CONV_W, t),
            pw_spec, vec, w_spec, vec, vec, vec,
        ],
        out_specs=[pl.BlockSpec((tt, zw), lambda i: (i, 0)), pw_spec, vec, w_spec, vec, vec, vec],
        out_shape=[
            jax.ShapeDtypeStruct((t, zw), MXU_DT),
            jax.ShapeDtypeStruct((ng, POOL_GROUP, POOL_GROUP), F32),
            jax.ShapeDtypeStruct((1, POOL_W), F32),
            jax.ShapeDtypeStruct((CONV_K, CONV_W), F32),
            jax.ShapeDtypeStruct((1, CONV_W), F32),
            jax.ShapeDtypeStruct((1, CONV_W), F32),
            jax.ShapeDtypeStruct((1, CONV_W), F32),
        ],
        scratch_shapes=[
            pltpu.VMEM((tt + hp, POOL_W), F32),
            pltpu.VMEM((tt + hp, POOL_W), F32),
            pltpu.VMEM((tt + 2 * hp, CONV_W), F32),
            pltpu.VMEM((tt + hp, CONV_W), F32),
        ],
        name=name,
        compiler_params=_cparams("arbitrary"),
    )(z, z, z, dcat, dcat, pool_w, pool_scale.reshape(1, POOL_W), dw_w, dw_b.reshape(1, CONV_W),
      ln_g.reshape(1, CONV_W), ln_b.reshape(1, CONV_W))


def _xa_fwd(q, kvm, name):
    t = q.shape[0]
    tt = _rows(t, TT)
    scale = XA_HEAD_DIM ** -0.5

    def body(q_ref, kv_ref, o_ref):
        for h in range(XA_HEADS):
            cs = pl.ds(h * XA_HEAD_DIM, XA_HEAD_DIM)
            vs = pl.ds(D_MODEL + h * XA_HEAD_DIM, XA_HEAD_DIM)
            s = lax.dot_general(q_ref[:, cs], kv_ref[:, cs], NT, preferred_element_type=F32) * scale
            p = jnp.exp(s - jnp.max(s, axis=-1, keepdims=True))
            p = p / jnp.sum(p, axis=-1, keepdims=True)
            o_ref[:, cs] = jnp.dot(p.astype(MXU_DT), kv_ref[:, vs], preferred_element_type=F32).astype(o_ref.dtype)

    return pl.pallas_call(
        body,
        grid=(t // tt,),
        in_specs=[pl.BlockSpec((tt, D_MODEL), lambda i: (i, 0)), pl.BlockSpec((MEM_LEN, 2 * D_MODEL), lambda i: (0, 0))],
        out_specs=pl.BlockSpec((tt, D_MODEL), lambda i: (i, 0)),
        out_shape=jax.ShapeDtypeStruct((t, D_MODEL), MXU_DT),
        name=name,
        compiler_params=_cparams("parallel"),
    )(q, kvm)


def _xa_bwd(q, kvm, do, name):
    t = q.shape[0]
    tt = _rows(t, TT)
    scale = XA_HEAD_DIM ** -0.5

    def body(q_ref, kv_ref, do_ref, dq_ref, dkv_ref):
        @pl.when(pl.program_id(0) == 0)
        def _():
            dkv_ref[...] = jnp.zeros_like(dkv_ref)

        for h in range(XA_HEADS):
            cs = pl.ds(h * XA_HEAD_DIM, XA_HEAD_DIM)
            vs = pl.ds(D_MODEL + h * XA_HEAD_DIM, XA_HEAD_DIM)
            qh = q_ref[:, cs]
            kh = kv_ref[:, cs]
            doh = do_ref[:, cs]
            s = lax.dot_general(qh, kh, NT, preferred_element_type=F32) * scale
            p = jnp.exp(s - jnp.max(s, axis=-1, keepdims=True))
            p = p / jnp.sum(p, axis=-1, keepdims=True)
            dp = lax.dot_general(doh, kv_ref[:, vs], NT, preferred_element_type=F32)
            ds = (p * (dp - jnp.sum(p * dp, axis=-1, keepdims=True)) * scale).astype(MXU_DT)
            dq_ref[:, cs] = jnp.dot(ds, kh, preferred_element_type=F32).astype(dq_ref.dtype)
            dkv_ref[:, cs] += lax.dot_general(ds, qh, TN_DIMS, preferred_element_type=F32)
            dkv_ref[:, vs] += lax.dot_general(p.astype(MXU_DT), doh, TN_DIMS, preferred_element_type=F32)

    row = pl.BlockSpec((tt, D_MODEL), lambda i: (i, 0))
    kvs = pl.BlockSpec((MEM_LEN, 2 * D_MODEL), lambda i: (0, 0))
    return pl.pallas_call(
        body,
        grid=(t // tt,),
        in_specs=[row, kvs, row],
        out_specs=[row, kvs],
        out_shape=[jax.ShapeDtypeStruct((t, D_MODEL), MXU_DT), jax.ShapeDtypeStruct((MEM_LEN, 2 * D_MODEL), F32)],
        name=name,
        compiler_params=_cparams("arbitrary"),
    )(q, kvm, do)


def _rope_tables(positions, name):
    t = positions.shape[0]
    tt = _rows(t, TT)
    inv = 1.0 / (ROPE_THETA ** (np.arange(0, QK_ROPE, 2, dtype=np.float32) / QK_ROPE))
    lanes = np.zeros((1, HEAD_PAD), np.float32)
    half = QK_ROPE // 2
    lanes[0, KPE_LANE : KPE_LANE + half] = inv
    lanes[0, KPE_LANE + half : KPE_LANE + QK_ROPE] = inv

    def body(pos_ref, inv_ref, cos_ref, sa_ref, sb_ref):
        ang = pos_ref[...].astype(F32) * inv_ref[...]
        lane = lax.broadcasted_iota(jnp.int32, (tt, HEAD_PAD), 1)
        c = jnp.cos(ang)
        s = jnp.sin(ang)
        lo = (lane >= KPE_LANE) & (lane < KPE_LANE + half)
        hi = (lane >= KPE_LANE + half) & (lane < KPE_LANE + QK_ROPE)
        cos_ref[...] = jnp.where(lo | hi, c, 1.0)
        sa_ref[...] = jnp.where(hi, s, 0.0)
        sb_ref[...] = jnp.where(lo, -s, 0.0)

    tab = pl.BlockSpec((tt, HEAD_PAD), lambda i: (i, 0))
    return pl.pallas_call(
        body,
        grid=(t // tt,),
        in_specs=[pl.BlockSpec((tt, 1), lambda i: (i, 0)), pl.BlockSpec((1, HEAD_PAD), lambda i: (0, 0))],
        out_specs=[tab, tab, tab],
        out_shape=[jax.ShapeDtypeStruct((t, HEAD_PAD), F32)] * 3,
        name=name,
        compiler_params=_cparams("parallel"),
    )(positions, jnp.asarray(lanes))


def _rotate(x, cos, sa, sb, sign):
    half = QK_ROPE // 2
    return x * cos + sign * (pltpu.roll(x, half, 1) * sa + pltpu.roll(x, HEAD_PAD - half, 1) * sb)


def _rope_heads(x, tables, sign, scale, name):
    t, w = x.shape
    tt = _rows(t, TT)
    nh = w // HEAD_PAD

    def body(x_ref, c_ref, sa_ref, sb_ref, o_ref):
        cos, sa, sb = c_ref[...] * scale, sa_ref[...] * scale, sb_ref[...] * scale
        for h in range(nh):
            cs = pl.ds(h * HEAD_PAD, HEAD_PAD)
            o_ref[:, cs] = _rotate(x_ref[:, cs], cos, sa, sb, sign).astype(o_ref.dtype)

    tab = pl.BlockSpec((tt, HEAD_PAD), lambda i: (i, 0))
    row = pl.BlockSpec((tt, w), lambda i: (i, 0))
    return pl.pallas_call(
        body,
        grid=(t // tt,),
        in_specs=[row, tab, tab, tab],
        out_specs=row,
        out_shape=jax.ShapeDtypeStruct((t, w), MXU_DT),
        name=name,
        compiler_params=_cparams("parallel"),
    )(x, *tables)


def _mla_prep(cp, qg, kvg, tables, name):
    t = cp.shape[0]
    tt = _rows(t, TT)

    def body(cp_ref, qg_ref, kvg_ref, c_ref, sa_ref, sb_ref, qn_ref, kvn_ref, kpe_ref):
        cq = cp_ref[:, 0:Q_LORA]
        r = lax.rsqrt(jnp.mean(cq * cq, axis=-1, keepdims=True) + EPS)
        qn_ref[...] = ((cq * r) * qg_ref[...]).astype(qn_ref.dtype)
        ckv = cp_ref[:, Q_LORA : Q_LORA + KV_LORA]
        r = lax.rsqrt(jnp.mean(ckv * ckv, axis=-1, keepdims=True) + EPS)
        kvn_ref[...] = ((ckv * r) * kvg_ref[...]).astype(kvn_ref.dtype)
        kpe = cp_ref[:, Q_LORA + KV_LORA : C_PAD]
        kpe_ref[...] = _rotate(kpe, c_ref[...], sa_ref[...], sb_ref[...], 1.0).astype(kpe_ref.dtype)

    tab = pl.BlockSpec((tt, HEAD_PAD), lambda i: (i, 0))
    return pl.pallas_call(
        body,
        grid=(t // tt,),
        in_specs=[
            pl.BlockSpec((tt, C_PAD), lambda i: (i, 0)),
            pl.BlockSpec((1, Q_LORA), lambda i: (0, 0)),
            pl.BlockSpec((1, KV_LORA), lambda i: (0, 0)),
            tab, tab, tab,
        ],
        out_specs=[
            pl.BlockSpec((tt, Q_LORA), lambda i: (i, 0)),
            pl.BlockSpec((tt, KV_LORA), lambda i: (i, 0)),
            tab,
        ],
        out_shape=[
            jax.ShapeDtypeStruct((t, Q_LORA), MXU_DT),
            jax.ShapeDtypeStruct((t, KV_LORA), MXU_DT),
            jax.ShapeDtypeStruct((t, HEAD_PAD), MXU_DT),
        ],
        name=name,
        compiler_params=_cparams("parallel"),
    )(cp, qg.reshape(1, Q_LORA), kvg.reshape(1, KV_LORA), *tables)


def _mla_prep_bwd(cp, dqn, dkvn, dkpe_heads, qg, kvg, tables, name):
    t = cp.shape[0]
    tt = _rows(t, TT)

    def norm_bwd(x, dy, g):
        r = lax.rsqrt(jnp.mean(x * x, axis=-1, keepdims=True) + EPS)
        xh = x * r
        gy = dy * g
        return r * (gy - xh * jnp.mean(gy * xh, axis=-1, keepdims=True)), jnp.sum(dy * xh, axis=0, keepdims=True)

    def body(cp_ref, dqn_ref, dkvn_ref, dkpe_ref, qg_ref, kvg_ref, c_ref, sa_ref, sb_ref, dcp_ref, dqg_ref, dkvg_ref):
        @pl.when(pl.program_id(0) == 0)
        def _():
            dqg_ref[...] = jnp.zeros_like(dqg_ref)
            dkvg_ref[...] = jnp.zeros_like(dkvg_ref)

        dcq, dg = norm_bwd(cp_ref[:, 0:Q_LORA], dqn_ref[...], qg_ref[...])
        dcp_ref[:, 0:Q_LORA] = dcq.astype(dcp_ref.dtype)
        dqg_ref[...] += dg
        dckv, dg = norm_bwd(cp_ref[:, Q_LORA : Q_LORA + KV_LORA], dkvn_ref[...], kvg_ref[...])
        dcp_ref[:, Q_LORA : Q_LORA + KV_LORA] = dckv.astype(dcp_ref.dtype)
        dkvg_ref[...] += dg
        dk = dkpe_ref[0]
        for h in range(1, MLA_HEADS):
            dk = dk + dkpe_ref[h]
        dcp_ref[:, Q_LORA + KV_LORA : C_PAD] = _rotate(dk, c_ref[...], sa_ref[...], sb_ref[...], -1.0).astype(dcp_ref.dtype)

    tab = pl.BlockSpec((tt, HEAD_PAD), lambda i: (i, 0))
    return pl.pallas_call(
        body,
        grid=(t // tt,),
        in_specs=[
            pl.BlockSpec((tt, C_PAD), lambda i: (i, 0)),
            pl.BlockSpec((tt, Q_LORA), lambda i: (i, 0)),
            pl.BlockSpec((tt, KV_LORA), lambda i: (i, 0)),
            pl.BlockSpec((MLA_HEADS, tt, HEAD_PAD), lambda i: (0, i, 0)),
            pl.BlockSpec((1, Q_LORA), lambda i: (0, 0)),
            pl.BlockSpec((1, KV_LORA), lambda i: (0, 0)),
            tab, tab, tab,
        ],
        out_specs=[
            pl.BlockSpec((tt, C_PAD), lambda i: (i, 0)),
            pl.BlockSpec((1, Q_LORA), lambda i: (0, 0)),
            pl.BlockSpec((1, KV_LORA), lambda i: (0, 0)),
        ],
        out_shape=[
            jax.ShapeDtypeStruct((t, C_PAD), MXU_DT),
            jax.ShapeDtypeStruct((1, Q_LORA), F32),
            jax.ShapeDtypeStruct((1, KV_LORA), F32),
        ],
        name=name,
        compiler_params=_cparams("arbitrary"),
    )(cp, dqn, dkvn, dkpe_heads, qg.reshape(1, Q_LORA), kvg.reshape(1, KV_LORA), *tables)


def _flash_fwd(qs, kv, kpe, name):
    t = qs.shape[0]
    ta = _rows(t, TA)
    tq = _rows(t, TAQ)
    nq = t // tq
    per = tq // ta

    def body(q_ref, kv_ref, kpe_ref, o_ref, lse_ref):
        qi = pl.program_id(1)
        q = q_ref[...]
        lane = lax.broadcasted_iota(jnp.int32, (ta, HEAD_PAD), 1)

        def kblock(j):
            rows = pl.ds(pl.multiple_of(j * ta, ta), ta)
            kvb = kv_ref[rows, :]
            ones_v = jnp.where(lane < QK_NOPE, jnp.ones_like(kvb), kvb)
            return ones_v, jnp.where(lane < QK_NOPE, kvb, kpe_ref[rows, :])

        def update(carry, s, ones_v):
            m, acc = carry
            m_new = jnp.maximum(m, jnp.max(s, axis=-1, keepdims=True))
            p = jnp.exp2(s - m_new).astype(MXU_DT)
            acc = jnp.exp2(m - m_new) * acc + jnp.dot(p, ones_v, preferred_element_type=F32)
            return m_new, acc

        def step(j, carry):
            ones_v, k = kblock(j)
            return update(carry, lax.dot_general(q, k, NT, preferred_element_type=F32), ones_v)

        init = (jnp.full((tq, 1), -jnp.inf, F32), jnp.zeros((tq, HEAD_PAD), F32))
        carry = lax.fori_loop(0, qi * per, step, init)
        r = lax.broadcasted_iota(jnp.int32, (tq, ta), 0)
        c = lax.broadcasted_iota(jnp.int32, (tq, ta), 1)
        for d in range(per):
            ones_v, k = kblock(qi * per + d)
            s = lax.dot_general(q, k, NT, preferred_element_type=F32)
            carry = update(carry, jnp.where(c + d * ta <= r, s, NEG), ones_v)
        m, acc = carry
        l = acc[:, 0:1]
        lane_q = lax.broadcasted_iota(jnp.int32, (tq, HEAD_PAD), 1)
        o_ref[...] = jnp.where(lane_q >= QK_NOPE, acc / l, 0.0).astype(o_ref.dtype)
        lse_ref[...] = m + jnp.log2(l)

    return pl.pallas_call(
        body,
        grid=(MLA_HEADS, nq),
        in_specs=[
            pl.BlockSpec((tq, HEAD_PAD), lambda h, i: (i, h)),
            pl.BlockSpec((t, HEAD_PAD), lambda h, i: (0, h)),
            pl.BlockSpec((t, HEAD_PAD), lambda h, i: (0, 0)),
        ],
        out_specs=[
            pl.BlockSpec((tq, HEAD_PAD), lambda h, i: (i, h)),
            pl.BlockSpec((None, tq, 1), lambda h, i: (h, i, 0)),
        ],
        out_shape=[
            jax.ShapeDtypeStruct((t, MLA_HEADS * HEAD_PAD), MXU_DT),
            jax.ShapeDtypeStruct((MLA_HEADS, t, 1), F32),
        ],
        name=name,
        compiler_params=_cparams("parallel", "parallel"),
    )(qs, kv, kpe)


def _flash_bwd(qs, kv, kpe, o, do, lse, name):
    t = qs.shape[0]
    ta = _rows(t, TA)
    tq = _rows(t, TAQ)
    nq = t // ta
    per = tq // ta

    def body(q_ref, o_ref, do_ref, lse_ref, kv_ref, kpe_ref, dq_ref, dkv_ref, dkpe_ref, dk_acc, dv_acc):
        kj = pl.program_id(1)

        @pl.when(kj == 0)
        def _():
            dq_ref[...] = jnp.zeros_like(dq_ref)

        lane = lax.broadcasted_iota(jnp.int32, (ta, HEAD_PAD), 1)
        kvb = kv_ref[...]
        k = jnp.where(lane < QK_NOPE, kvb, kpe_ref[...])
        dk_acc[...] = jnp.zeros_like(dk_acc)
        dv_acc[...] = jnp.zeros_like(dv_acc)

        def tile(qq, first_key):
            rows = pl.ds(pl.multiple_of(qq * tq, tq), tq)
            q = q_ref[rows, :]
            dob = do_ref[rows, :]
            delta = jnp.sum(dob.astype(F32) * o_ref[rows, :].astype(F32), axis=-1, keepdims=True)
            s = lax.dot_general(q, k, NT, preferred_element_type=F32)
            if first_key is not None:
                r = lax.broadcasted_iota(jnp.int32, (tq, ta), 0)
                c = lax.broadcasted_iota(jnp.int32, (tq, ta), 1)
                s = jnp.where(c + first_key <= r, s, NEG)
            p = jnp.exp2(s - lse_ref[rows, :])
            dp = lax.dot_general(dob, kvb, NT, preferred_element_type=F32)
            ds = (p * (dp - delta)).astype(MXU_DT)
            dq_ref[rows, :] += jnp.dot(ds, k, preferred_element_type=F32)
            dk_acc[...] += lax.dot_general(ds, q, TN_DIMS, preferred_element_type=F32)
            dv_acc[...] += lax.dot_general(p.astype(MXU_DT), dob, TN_DIMS, preferred_element_type=F32)

        qq0 = kj // per
        tile(qq0, (kj - qq0 * per) * ta)

        def step(qq, carry):
            tile(qq, None)
            return carry

        lax.fori_loop(qq0 + 1, t // tq, step, 0)
        dk = dk_acc[...] * (1.0 / LOG2E)
        dkv_ref[...] = jnp.where(lane < QK_NOPE, dk, dv_acc[...]).astype(dkv_ref.dtype)
        dkpe_ref[...] = jnp.where((lane >= KPE_LANE) & (lane < KPE_LANE + QK_ROPE), dk, 0.0)

    head_rows = pl.BlockSpec((t, HEAD_PAD), lambda h, j: (0, h))
    return pl.pallas_call(
        body,
        grid=(MLA_HEADS, nq),
        in_specs=[
            head_rows,
            head_rows,
            head_rows,
            pl.BlockSpec((None, t, 1), lambda h, j: (h, 0, 0)),
            pl.BlockSpec((ta, HEAD_PAD), lambda h, j: (j, h)),
            pl.BlockSpec((ta, HEAD_PAD), lambda h, j: (j, 0)),
        ],
        out_specs=[
            head_rows,
            pl.BlockSpec((ta, HEAD_PAD), lambda h, j: (j, h)),
            pl.BlockSpec((None, ta, HEAD_PAD), lambda h, j: (h, j, 0)),
        ],
        out_shape=[
            jax.ShapeDtypeStruct((t, MLA_HEADS * HEAD_PAD), F32),
            jax.ShapeDtypeStruct((t, MLA_HEADS * HEAD_PAD), MXU_DT),
            jax.ShapeDtypeStruct((MLA_HEADS, t, HEAD_PAD), F32),
        ],
        scratch_shapes=[pltpu.VMEM((ta, HEAD_PAD), F32), pltpu.VMEM((ta, HEAD_PAD), F32)],
        name=name,
        compiler_params=_cparams("parallel", "arbitrary"),
    )(qs, o, do, lse, kv, kpe)


def _as2d(a):
    if a.ndim == 1:
        return a.reshape(1, a.shape[0])
    return a.reshape(-1, a.shape[-1])


def _adamw(w, g, m, v, name):
    shape = w.shape
    w2, g2, m2, v2 = (_as2d(a) for a in (w, g, m, v))
    r, c = w2.shape
    tr = _tile_rows(r, c)
    c1 = 1.0 - ADAM_B1 ** ADAM_STEP
    c2 = 1.0 - ADAM_B2 ** ADAM_STEP

    def body(w_ref, g_ref, m_ref, v_ref, d_ref, nm_ref, nv_ref):
        gv = g_ref[...]
        nm = ADAM_B1 * m_ref[...] + (1.0 - ADAM_B1) * gv
        nv = ADAM_B2 * v_ref[...] + (1.0 - ADAM_B2) * (gv * gv)
        d_ref[...] = -ADAM_LR * ((nm / c1) / (jnp.sqrt(nv / c2) + ADAM_EPS) + ADAM_WD * w_ref[...])
        nm_ref[...] = nm
        nv_ref[...] = nv

    blk = pl.BlockSpec((tr, c), lambda i: (i, 0))
    outs = pl.pallas_call(
        body,
        grid=(r // tr,),
        in_specs=[blk] * 4,
        out_specs=[blk] * 3,
        out_shape=[jax.ShapeDtypeStruct((r, c), F32)] * 3,
        name=name,
        compiler_params=_cparams("parallel"),
    )(w2, g2, m2, v2)
    return tuple(o.reshape(shape) for o in outs)


def _adamw_halves(w, mine, other, m, v, c_idx, name):
    nl, r, c = w.shape
    h = nl // 2
    tr = _tile_rows(r, 2 * c)
    c1 = 1.0 - ADAM_B1 ** ADAM_STEP
    c2 = 1.0 - ADAM_B2 ** ADAM_STEP

    def body(c_ref, w_ref, a_ref, b_ref, m_ref, v_ref, g_ref, d_ref, nm_ref, nv_ref):
        l = pl.program_id(0)
        gv = jnp.where(l // h == c_ref[0], a_ref[...], b_ref[...])
        nm = ADAM_B1 * m_ref[...] + (1.0 - ADAM_B1) * gv
        nv = ADAM_B2 * v_ref[...] + (1.0 - ADAM_B2) * (gv * gv)
        g_ref[...] = gv
        d_ref[...] = -ADAM_LR * ((nm / c1) / (jnp.sqrt(nv / c2) + ADAM_EPS) + ADAM_WD * w_ref[...])
        nm_ref[...] = nm
        nv_ref[...] = nv

    def half_map(mine_side):
        def index(l, i, cr):
            first = cr[0] * h if mine_side else (1 - cr[0]) * h
            return (jnp.clip(l - first, 0, h - 1), i, 0)
        return index

    full = pl.BlockSpec((None, tr, c), lambda l, i, cr: (l, i, 0))
    grid_spec = pltpu.PrefetchScalarGridSpec(
        num_scalar_prefetch=1,
        grid=(nl, r // tr),
        in_specs=[full, pl.BlockSpec((None, tr, c), half_map(True)), pl.BlockSpec((None, tr, c), half_map(False)), full, full],
        out_specs=[full] * 4,
    )
    return pl.pallas_call(
        body,
        grid_spec=grid_spec,
        out_shape=[jax.ShapeDtypeStruct((nl, r, c), F32)] * 4,
        name=name,
        compiler_params=_cparams("parallel", "parallel"),
    )(c_idx, w, mine, other, m, v)


def _tile_rows(r, c, mult=SUBLANE):
    limit = max(mult, (BLOCK_BYTES // 4) // (4 * c))
    if r <= limit:
        return r
    t = (limit // mult) * mult
    while t >= mult:
        if r % t == 0:
            return t
        t -= mult
    return r


def _sum_leading(a, name):
    n, r, c = a.shape
    tr = _tile_rows(r, c * n)

    def body(a_ref, o_ref):
        s = a_ref[0]
        for k in range(1, n):
            s = s + a_ref[k]
        o_ref[...] = s

    return pl.pallas_call(
        body,
        grid=(r // tr,),
        in_specs=[pl.BlockSpec((n, tr, c), lambda i: (0, i, 0))],
        out_specs=pl.BlockSpec((tr, c), lambda i: (i, 0)),
        out_shape=jax.ShapeDtypeStruct((r, c), F32),
        name=name,
        compiler_params=_cparams("parallel"),
    )(a)


def _add_half(g, s, c_idx, name):
    nl, r, c = g.shape
    h = nl // 2
    tr = _tile_rows(r, 2 * c, 2 * SUBLANE)

    def body(c_ref, g_ref, s_ref, o_ref):
        o_ref[...] = (g_ref[...] + s_ref[...]).astype(o_ref.dtype)

    grid_spec = pltpu.PrefetchScalarGridSpec(
        num_scalar_prefetch=1,
        grid=(h, r // tr),
        in_specs=[
            pl.BlockSpec((None, tr, c), lambda l, i, cr: (cr[0] * h + l, i, 0)),
            pl.BlockSpec((None, tr, c), lambda l, i, cr: (l, i, 0)),
        ],
        out_specs=pl.BlockSpec((None, tr, c), lambda l, i, cr: (l, i, 0)),
    )
    return pl.pallas_call(
        body,
        grid_spec=grid_spec,
        out_shape=jax.ShapeDtypeStruct((h, r, c), XFER_DT),
        name=name,
        compiler_params=_cparams("parallel", "parallel"),
    )(c_idx, g, s)


def _sum_chips(slots, pair, chip_idx, kind, name):
    _, h, r, c = slots.shape
    tr = _tile_rows(r, 5 * c, 2 * SUBLANE)
    nr = r // tr

    def body(chip_ref, s_ref, own_ref, o_ref):
        chip = chip_ref[0]
        own = own_ref[...].astype(F32)
        parts = [s_ref[j].astype(F32) for j in range(3)]
        total = None
        for k in range(4):
            d = jnp.bitwise_xor(chip, k)
            v = jnp.where(d == 0, own, jnp.where(d == 2, parts[0], jnp.where(d == 1, parts[1], parts[2])))
            total = v if total is None else total + v
        o_ref[...] = total

    if kind == "row":
        own_spec = pl.BlockSpec((None, tr, c), lambda l, i, cr: (l, cr[0] * nr + i, 0))
    else:
        own_spec = pl.BlockSpec((None, tr, c), lambda l, i, cr: (l, i, cr[0]))
    grid_spec = pltpu.PrefetchScalarGridSpec(
        num_scalar_prefetch=1,
        grid=(h, nr),
        in_specs=[pl.BlockSpec((3, None, tr, c), lambda l, i, cr: (0, l, i, 0)), own_spec],
        out_specs=pl.BlockSpec((None, tr, c), lambda l, i, cr: (l, i, 0)),
    )
    return pl.pallas_call(
        body,
        grid_spec=grid_spec,
        out_shape=jax.ShapeDtypeStruct((h, r, c), F32),
        name=name,
        compiler_params=_cparams("parallel", "parallel"),
    )(chip_idx, slots, pair)


def _mesh_pos():
    return lax.axis_index("x"), lax.axis_index("y"), lax.axis_index("c")


def _other_chips(x, y):
    return [(1 - x, y), (x, 1 - y), (1 - x, 1 - y)]


def _all_gather_rows(block, name):
    m_per, n = block.shape

    def body(x_ref, out_ref, send_sems, recv_sems, local_sem):
        x, y, c = _mesh_pos()
        me, sibling = (x, y, c), (x, y, 1 - c)
        chips = _other_chips(x, y)

        def rows(px, py, pc):
            return out_ref.at[pl.ds((4 * px + 2 * py + pc) * m_per, m_per), :]

        def copy(k, blk, to, src=None):
            return pltpu.make_async_remote_copy(
                src_ref=rows(*blk) if src is None else src,
                dst_ref=rows(*blk),
                send_sem=send_sems.at[k],
                recv_sem=recv_sems.at[k],
                device_id=to,
                device_id_type=MESH_ID,
            )

        mine = pltpu.make_async_copy(x_ref, rows(*me), local_sem)
        mine.start()
        first = [copy(0, me, sibling, src=x_ref)]
        first += [copy(1 + j, me, (*chip, c), src=x_ref) for j, chip in enumerate(chips)]
        for cp in first:
            cp.start()
        passed = [copy(4 + j, (*chip, c), sibling) for j, chip in enumerate(chips)]
        for j, chip in enumerate(chips):
            copy(1 + j, (*chip, c), me).wait_recv()
            passed[j].start()
        copy(0, sibling, me).wait_recv()
        for j, chip in enumerate(chips):
            copy(4 + j, (*chip, 1 - c), me).wait_recv()
        for cp in first + passed:
            cp.wait_send()
        mine.wait()

    return pl.pallas_call(
        body,
        out_shape=jax.ShapeDtypeStruct((8 * m_per, n), block.dtype),
        in_specs=[pl.BlockSpec(memory_space=pltpu.VMEM)],
        out_specs=pl.BlockSpec(memory_space=pltpu.VMEM),
        scratch_shapes=[pltpu.SemaphoreType.DMA((7,)), pltpu.SemaphoreType.DMA((7,)), pltpu.SemaphoreType.DMA],
        name=name,
        compiler_params=pltpu.CompilerParams(vmem_limit_bytes=VMEM_LIMIT),
    )(block)


def _shard_window(ref, layers, chip, rows, cols):
    if rows is not None:
        return ref.at[layers, pl.ds(pl.multiple_of(chip * rows, rows), rows), :]
    return ref.at[layers, :, pl.ds(pl.multiple_of(chip * cols, cols), cols)]


def _all_gather_weights(shards, kinds, name):
    nw = len(shards)
    out_shapes = []
    for s, kind in zip(shards, kinds):
        nl, r, c = s.shape
        full = (nl, 4 * r, c) if kind == "row" else (nl, r, 4 * c)
        out_shapes.append(jax.ShapeDtypeStruct(full, s.dtype))

    def body(*refs):
        ins, outs = refs[:nw], refs[nw : 2 * nw]
        send_sems, recv_sems, in_sems, out_sems = refs[2 * nw : 2 * nw + 4]
        bufs = refs[2 * nw + 4 :]
        x, y, c = _mesh_pos()
        sibling = (x, y, 1 - c)
        chips = _other_chips(x, y)
        my_chip = 2 * x + y

        def window(w, chip, layers):
            _, r, cc = shards[w].shape
            if kinds[w] == "row":
                return _shard_window(outs[w], layers, chip, r, None)
            return _shard_window(outs[w], layers, chip, None, cc)

        def half(w, half_idx):
            h = shards[w].shape[0] // 2
            return pl.ds(half_idx * h, h)

        def copy(w, k, src, dst, to):
            return pltpu.make_async_remote_copy(
                src_ref=src, dst_ref=dst, send_sem=send_sems.at[w, k], recv_sem=recv_sems.at[w, k],
                device_id=to, device_id_type=MESH_ID)

        sent = []
        for w in range(nw):
            mine = ins[w].at[half(w, c)]
            for j, chip in enumerate(chips):
                cp = copy(w, j, mine, window(w, my_chip, half(w, c)), (*chip, c))
                cp.start()
                sent.append(cp)
        for w in range(nw):
            nl = shards[w].shape[0]

            def load(l, w=w):
                return pltpu.make_async_copy(ins[w].at[l], bufs[w].at[l % 2], in_sems.at[w, l % 2])

            def store(l, w=w):
                return pltpu.make_async_copy(bufs[w].at[l % 2], window(w, my_chip, l), out_sems.at[w, l % 2])

            load(0).start()
            for l in range(nl):
                load(l).wait()
                store(l).start()
                if l + 1 < nl:
                    if l >= 1:
                        store(l - 1).wait()
                    load(l + 1).start()
            for l in range(max(nl - 2, 0), nl):
                store(l).wait()
        for w in range(nw):
            for j, (cx, cy) in enumerate(chips):
                got = window(w, 2 * cx + cy, half(w, c))
                copy(w, j, got, got, (cx, cy, c)).wait_recv()
                cp = copy(w, 3 + j, got, got, sibling)
                cp.start()
                sent.append(cp)
        for w in range(nw):
            for j, (cx, cy) in enumerate(chips):
                got = window(w, 2 * cx + cy, half(w, 1 - c))
                copy(w, 3 + j, got, got, sibling).wait_recv()
        for cp in sent:
            cp.wait_send()

    anyspec = pl.BlockSpec(memory_space=pl.ANY)
    return pl.pallas_call(
        body,
        out_shape=out_shapes,
        in_specs=[anyspec] * nw,
        out_specs=[anyspec] * nw,
        scratch_shapes=[pltpu.SemaphoreType.DMA((nw, 6)), pltpu.SemaphoreType.DMA((nw, 6)),
                        pltpu.SemaphoreType.DMA((nw, 2)), pltpu.SemaphoreType.DMA((nw, 2))]
        + [pltpu.VMEM((2,) + s.shape[1:], s.dtype) for s in shards],
        name=name,
        compiler_params=pltpu.CompilerParams(vmem_limit_bytes=VMEM_LIMIT),
    )(*shards)


def _exchange_halves(grads, name):
    nw = len(grads)
    out_shapes = [jax.ShapeDtypeStruct((g.shape[0] // 2,) + g.shape[1:], g.dtype) for g in grads]

    def body(*refs):
        ins, outs = refs[:nw], refs[nw : 2 * nw]
        send_sems, recv_sems = refs[2 * nw :]
        x, y, c = _mesh_pos()
        cps = []
        for w in range(nw):
            h = grads[w].shape[0] // 2
            cp = pltpu.make_async_remote_copy(
                src_ref=ins[w].at[pl.ds((1 - c) * h, h)], dst_ref=outs[w], send_sem=send_sems.at[w],
                recv_sem=recv_sems.at[w], device_id=(x, y, 1 - c), device_id_type=MESH_ID)
            cp.start()
            cps.append(cp)
        for cp in cps:
            cp.wait()

    anyspec = pl.BlockSpec(memory_space=pl.ANY)
    return pl.pallas_call(
        body,
        out_shape=out_shapes,
        in_specs=[anyspec] * nw,
        out_specs=[anyspec] * nw,
        scratch_shapes=[pltpu.SemaphoreType.DMA((nw,)), pltpu.SemaphoreType.DMA((nw,))],
        name=name,
    )(*grads)


def _scatter_to_chips(parts, kinds, name):
    nw = len(parts)
    shard_shapes = []
    for p, kind in zip(parts, kinds):
        h, r, c = p.shape
        shard_shapes.append((h, r // 4, c) if kind == "row" else (h, r, c // 4))
    out_shapes = [jax.ShapeDtypeStruct((3,) + s, p.dtype) for s, p in zip(shard_shapes, parts)]

    def body(*refs):
        ins, outs = refs[:nw], refs[nw : 2 * nw]
        send_sems, recv_sems = refs[2 * nw :]
        x, y, c = _mesh_pos()
        chips = _other_chips(x, y)

        def piece(w, chip):
            h, r, cc = shard_shapes[w]
            if kinds[w] == "row":
                return _shard_window(ins[w], pl.ds(0, h), chip, r, None)
            return _shard_window(ins[w], pl.ds(0, h), chip, None, cc)

        def copy(w, j, cx, cy):
            return pltpu.make_async_remote_copy(
                src_ref=piece(w, 2 * cx + cy), dst_ref=outs[w].at[j], send_sem=send_sems.at[w, j],
                recv_sem=recv_sems.at[w, j], device_id=(cx, cy, c), device_id_type=MESH_ID)

        cps = [copy(w, j, cx, cy) for w in range(nw) for j, (cx, cy) in enumerate(chips)]
        for cp in cps:
            cp.start()
        for cp in cps:
            cp.wait()

    anyspec = pl.BlockSpec(memory_space=pl.ANY)
    return pl.pallas_call(
        body,
        out_shape=out_shapes,
        in_specs=[anyspec] * nw,
        out_specs=[anyspec] * nw,
        scratch_shapes=[pltpu.SemaphoreType.DMA((nw, 3)), pltpu.SemaphoreType.DMA((nw, 3))],
        name=name,
    )(*parts)


def _swap_halves(halves, name):
    nw = len(halves)
    out_shapes = [jax.ShapeDtypeStruct(p.shape, p.dtype) for p in halves]

    def body(*refs):
        ins, outs = refs[:nw], refs[nw : 2 * nw]
        send_sems, recv_sems = refs[2 * nw :]
        x, y, c = _mesh_pos()
        cps = [pltpu.make_async_remote_copy(
            src_ref=ins[w], dst_ref=outs[w], send_sem=send_sems.at[w], recv_sem=recv_sems.at[w],
            device_id=(x, y, 1 - c), device_id_type=MESH_ID) for w in range(nw)]
        for cp in cps:
            cp.start()
        for cp in cps:
            cp.wait()

    anyspec = pl.BlockSpec(memory_space=pl.ANY)
    return pl.pallas_call(
        body,
        out_shape=out_shapes,
        in_specs=[anyspec] * nw,
        out_specs=[anyspec] * nw,
        scratch_shapes=[pltpu.SemaphoreType.DMA((nw,)), pltpu.SemaphoreType.DMA((nw,))],
        name=name,
    )(*halves)


def _pad_wdq(w):
    z = lambda n: jnp.zeros((w.shape[0], n), w.dtype)
    base = Q_LORA + KV_LORA
    return jnp.concatenate([w[:, :base], z(KPE_LANE), w[:, base:], z(HEAD_PAD - KPE_LANE - QK_ROPE)], axis=1)


def _unpad_wdq(g):
    base = Q_LORA + KV_LORA
    return jnp.concatenate([g[:, :base], g[:, base + KPE_LANE : base + KPE_LANE + QK_ROPE]], axis=1)


def _pad_wuq(w):
    w3 = w.reshape(Q_LORA, MLA_HEADS, QK_NOPE + QK_ROPE)
    w3 = jnp.pad(w3, ((0, 0), (0, 0), (0, HEAD_PAD - QK_NOPE - QK_ROPE)))
    return w3.reshape(Q_LORA, MLA_HEADS * HEAD_PAD)


def _unpad_wuq(g):
    g3 = g.reshape(Q_LORA, MLA_HEADS, HEAD_PAD)[:, :, : QK_NOPE + QK_ROPE]
    return g3.reshape(Q_LORA, MLA_HEADS * (QK_NOPE + QK_ROPE))


def _pad_wo(w):
    w3 = w.reshape(MLA_HEADS, V_HEAD, D_MODEL)
    w3 = jnp.pad(w3, ((0, 0), (HEAD_PAD - V_HEAD, 0), (0, 0)))
    return w3.reshape(MLA_HEADS * HEAD_PAD, D_MODEL)


def _unpad_wo(g):
    g3 = g.reshape(MLA_HEADS, HEAD_PAD, D_MODEL)[:, HEAD_PAD - V_HEAD :, :]
    return g3.reshape(MLA_HEADS * V_HEAD, D_MODEL)


def _local_step(x, mem, positions, target, wb, ws):
    t = x.shape[0]
    tables = _rope_tables(positions.reshape(t, 1), "rope_tables")
    saved = []
    for l in range(DEPTH):
        s = {"x0": x}
        h1 = _rms_fwd(x, ws["norm_mix_g"][l], f"l{l}_norm_mix")
        s["h1"] = h1
        if l % 2 == 0:
            e = l // 2
            z = _matmul(h1, wb["pc_w_in"], "nn", F32, f"l{l}_pc_in", layer=e)
            cat = _mix_fwd(z, ws["pool_w"][e], ws["pool_scale"][e], ws["conv_dw_w"][e], ws["conv_dw_b"][e],
                           ws["conv_ln_g"][e], ws["conv_ln_b"][e], f"l{l}_mix")
            x = _matmul(cat, wb["pc_w_out"], "nn", F32, f"l{l}_pc_out", layer=e, res=x)
            s.update(z=z, cat=cat)
        else:
            o = l // 2
            cp = _matmul(h1, wb["mla_wdq"], "nn", F32, f"l{l}_mla_dq", layer=o)
            qn, kvn, kpe = _mla_prep(cp, ws["mla_q_norm_g"][o], ws["mla_kv_norm_g"][o], tables, f"l{l}_mla_prep")
            q = _matmul(qn, wb["mla_wuq"], "nn", F32, f"l{l}_mla_uq", layer=o)
            qr = _rope_heads(q, tables, 1.0, MLA_SCALE * LOG2E, f"l{l}_mla_rope")
            kv = _matmul(kvn, wb["mla_w_ukv"], "nn", MXU_DT, f"l{l}_mla_ukv", layer=o)
            att, lse = _flash_fwd(qr, kv, kpe, f"l{l}_mla_attn")
            x = _matmul(att, wb["mla_wo"], "nn", F32, f"l{l}_mla_o", layer=o, res=x)
            s.update(cp=cp, qn=qn, kvn=kvn, kpe=kpe, qr=qr, kv=kv, att=att, lse=lse)
        s["x1"] = x
        h2 = _rms_fwd(x, ws["norm_xa_g"][l], f"l{l}_norm_xa")
        hm = _rms_fwd(mem, ws["norm_mem_g"][l], f"l{l}_norm_mem")
        q2 = _matmul(h2, wb["xa_wq"], "nn", MXU_DT, f"l{l}_xa_q", layer=l)
        kvm = _matmul(hm, wb["xa_wkv"], "nn", MXU_DT, f"l{l}_xa_kv", layer=l)
        o2 = _xa_fwd(q2, kvm, f"l{l}_xa_attn")
        x = _matmul(o2, wb["xa_wo"], "nn", F32, f"l{l}_xa_o", layer=l, res=x)
        s.update(h2=h2, hm=hm, q2=q2, kvm=kvm, o2=o2, x2=x)
        h3 = _rms_fwd(x, ws["norm_ffn_g"][l], f"l{l}_norm_ffn")
        up = _matmul(h3, wb["ffn_w_up"], "nn", F32, f"l{l}_ffn_up", layer=l)
        act = _ffn_fwd(up, ws["ffn_conv_w"][l], ws["ffn_conv_b"][l], f"l{l}_ffn_mid")
        x = _matmul(act, wb["ffn_w_down"], "nn", F32, f"l{l}_ffn_down", layer=l, res=x)
        s.update(h3=h3, up=up, act=act)
        saved.append(s)

    dx, dg_final, loss = _loss_head(x, target, ws["final_norm_g"], "loss_head")
    g = {k: [None] * DEPTH for k in ("norm_mix_g", "norm_xa_g", "norm_mem_g", "xa_wq", "xa_wkv", "xa_wo", "norm_ffn_g",
                                      "ffn_w_up", "ffn_conv_w", "ffn_conv_b", "ffn_w_down")}
    g.update({k: [None] * (DEPTH // 2) for k in ("pc_w_in", "pool_w", "pool_scale", "conv_dw_w", "conv_dw_b", "conv_ln_g",
                                                 "conv_ln_b", "pc_w_out", "mla_w_dq_dkv", "mla_q_norm_g", "mla_w_uq",
                                                 "mla_kv_norm_g", "mla_w_ukv", "mla_w_o")})
    for l in reversed(range(DEPTH)):
        s = saved[l]
        dact = _matmul(dx, wb["ffn_w_down"], "nt", F32, f"l{l}_b_ffn_dact", layer=l)
        g["ffn_w_down"][l] = _matmul(s["act"], dx, "tn", F32, f"l{l}_b_ffn_dwdown")
        dup, dcw, dcb = _ffn_bwd(s["up"], dact, ws["ffn_conv_w"][l], ws["ffn_conv_b"][l], f"l{l}_b_ffn_mid")
        g["ffn_conv_w"][l], g["ffn_conv_b"][l] = dcw, dcb[0]
        g["ffn_w_up"][l] = _matmul(s["h3"], dup, "tn", F32, f"l{l}_b_ffn_dwup")
        dh = _matmul(dup, wb["ffn_w_up"], "nt", F32, f"l{l}_b_ffn_dh", layer=l)
        dx, dg = _rms_bwd(dh, s["x2"], ws["norm_ffn_g"][l], dx, f"l{l}_b_norm_ffn")
        g["norm_ffn_g"][l] = dg[0]
        do2 = _matmul(dx, wb["xa_wo"], "nt", MXU_DT, f"l{l}_b_xa_do", layer=l)
        g["xa_wo"][l] = _matmul(s["o2"], dx, "tn", F32, f"l{l}_b_xa_dwo")
        dq2, dkvm = _xa_bwd(s["q2"], s["kvm"], do2, f"l{l}_b_xa_attn")
        g["xa_wq"][l] = _matmul(s["h2"], dq2, "tn", F32, f"l{l}_b_xa_dwq")
        dh = _matmul(dq2, wb["xa_wq"], "nt", F32, f"l{l}_b_xa_dh", layer=l)
        g["xa_wkv"][l] = _matmul(s["hm"], dkvm, "tn", F32, f"l{l}_b_xa_dwkv")
        dhm = _matmul(dkvm, wb["xa_wkv"], "nt", F32, f"l{l}_b_xa_dhm", layer=l)
        g["norm_mem_g"][l] = _rms_bwd_gain(dhm, mem, ws["norm_mem_g"][l], f"l{l}_b_norm_mem")[0]
        dx, dg = _rms_bwd(dh, s["x1"], ws["norm_xa_g"][l], dx, f"l{l}_b_norm_xa")
        g["norm_xa_g"][l] = dg[0]
        if l % 2 == 0:
            e = l // 2
            dcat = _matmul(dx, wb["pc_w_out"], "nt", F32, f"l{l}_b_pc_dcat", layer=e)
            g["pc_w_out"][e] = _matmul(s["cat"], dx, "tn", F32, f"l{l}_b_pc_dwout")
            dz, dpw, dps, dww, dwb, dlg, dlb = _mix_bwd(
                s["z"], dcat, ws["pool_w"][e], ws["pool_scale"][e], ws["conv_dw_w"][e], ws["conv_dw_b"][e],
                ws["conv_ln_g"][e], ws["conv_ln_b"][e], f"l{l}_b_mix")
            g["pool_w"][e], g["pool_scale"][e], g["conv_dw_w"][e] = dpw, dps[0], dww
            g["conv_dw_b"][e], g["conv_ln_g"][e], g["conv_ln_b"][e] = dwb[0], dlg[0], dlb[0]
            g["pc_w_in"][e] = _matmul(s["h1"], dz, "tn", F32, f"l{l}_b_pc_dwin")
            dh = _matmul(dz, wb["pc_w_in"], "nt", F32, f"l{l}_b_pc_dh", layer=e)
        else:
            o = l // 2
            do = _matmul(dx, wb["mla_wo"], "nt", MXU_DT, f"l{l}_b_mla_do", layer=o)
            g["mla_w_o"][o] = _unpad_wo(_matmul(s["att"], dx, "tn", F32, f"l{l}_b_mla_dwo"))
            dqr, dkv, dkpe = _flash_bwd(s["qr"], s["kv"], s["kpe"], s["att"], do, s["lse"], f"l{l}_b_mla_attn")
            dq = _rope_heads(dqr, tables, -1.0, MLA_SCALE, f"l{l}_b_mla_rope")
            g["mla_w_uq"][o] = _unpad_wuq(_matmul(s["qn"], dq, "tn", F32, f"l{l}_b_mla_dwuq"))
            dqn = _matmul(dq, wb["mla_wuq"], "nt", F32, f"l{l}_b_mla_dqn", layer=o)
            g["mla_w_ukv"][o] = _matmul(s["kvn"], dkv, "tn", F32, f"l{l}_b_mla_dwukv")
            dkvn = _matmul(dkv, wb["mla_w_ukv"], "nt", F32, f"l{l}_b_mla_dkvn", layer=o)
            dcp, dqg, dkvg = _mla_prep_bwd(s["cp"], dqn, dkvn, dkpe, ws["mla_q_norm_g"][o], ws["mla_kv_norm_g"][o],
                                           tables, f"l{l}_b_mla_prep")
            g["mla_q_norm_g"][o], g["mla_kv_norm_g"][o] = dqg[0], dkvg[0]
            g["mla_w_dq_dkv"][o] = _unpad_wdq(_matmul(s["h1"], dcp, "tn", F32, f"l{l}_b_mla_dwdq"))
            dh = _matmul(dcp, wb["mla_wdq"], "nt", F32, f"l{l}_b_mla_dh", layer=o)
        dx, dg = _rms_bwd(dh, s["x0"], ws["norm_mix_g"][l], dx, f"l{l}_b_norm_mix")
        g["norm_mix_g"][l] = dg[0]
    grads = {k: jnp.stack(v) for k, v in g.items()}
    grads["final_norm_g"] = dg_final[0]
    return loss, dx, grads


BIG = (
    ("xa_wq", "row"), ("xa_wkv", "col"), ("xa_wo", "row"), ("ffn_w_up", "col"), ("ffn_w_down", "row"),
    ("pc_w_in", "col"), ("pc_w_out", "row"), ("mla_w_dq_dkv", "row"), ("mla_w_uq", "col"), ("mla_w_ukv", "col"),
    ("mla_w_o", "row"),
)
SMALL_SHARDED = ("ffn_conv_w", "conv_dw_w", "mla_q_norm_g", "mla_kv_norm_g")
SMALL_REPLICATED = ("norm_mix_g", "norm_xa_g", "norm_mem_g", "norm_ffn_g", "ffn_conv_b", "pool_w", "pool_scale",
                    "conv_dw_b", "conv_ln_g", "conv_ln_b", "final_norm_g")
WEIGHTS = ("norm_mix_g", "norm_xa_g", "norm_mem_g", "xa_wq", "xa_wkv", "xa_wo", "norm_ffn_g", "ffn_w_up", "ffn_conv_w",
           "ffn_conv_b", "ffn_w_down", "pc_w_in", "pool_w", "pool_scale", "conv_dw_w", "conv_dw_b", "conv_ln_g",
           "conv_ln_b", "pc_w_out", "mla_w_dq_dkv", "mla_q_norm_g", "mla_w_uq", "mla_kv_norm_g", "mla_w_ukv", "mla_w_o",
           "final_norm_g")
PACK_ROW = SUBLANE * LANE


def _pack(arrays):
    flat = jnp.concatenate([a.reshape(-1) for a in arrays])
    n = flat.shape[0]
    pad = (-n) % PACK_ROW
    return jnp.pad(flat, (0, pad)).reshape(-1, LANE)


def _unpack(flat, shapes):
    out, off = [], 0
    for s in shapes:
        n = int(np.prod(s))
        out.append(flat[off : off + n].reshape(s))
        off += n
    return out


def kernel(x, mem, positions, norm_mix_g, norm_xa_g, norm_mem_g, xa_wq, xa_wkv, xa_wo, norm_ffn_g, ffn_w_up, ffn_conv_w, ffn_conv_b, ffn_w_down, pc_w_in, pool_w, pool_scale, conv_dw_w, conv_dw_b, conv_ln_g, conv_ln_b, pc_w_out, mla_w_dq_dkv, mla_q_norm_g, mla_w_uq, mla_kv_norm_g, mla_w_ukv, mla_w_o, final_norm_g, loss_target, m_norm_mix_g, m_norm_xa_g, m_norm_mem_g, m_xa_wq, m_xa_wkv, m_xa_wo, m_norm_ffn_g, m_ffn_w_up, m_ffn_conv_w, m_ffn_conv_b, m_ffn_w_down, m_pc_w_in, m_pool_w, m_pool_scale, m_conv_dw_w, m_conv_dw_b, m_conv_ln_g, m_conv_ln_b, m_pc_w_out, m_mla_w_dq_dkv, m_mla_q_norm_g, m_mla_w_uq, m_mla_kv_norm_g, m_mla_w_ukv, m_mla_w_o, m_final_norm_g, v_norm_mix_g, v_norm_xa_g, v_norm_mem_g, v_xa_wq, v_xa_wkv, v_xa_wo, v_norm_ffn_g, v_ffn_w_up, v_ffn_conv_w, v_ffn_conv_b, v_ffn_w_down, v_pc_w_in, v_pool_w, v_pool_scale, v_conv_dw_w, v_conv_dw_b, v_conv_ln_g, v_conv_ln_b, v_pc_w_out, v_mla_w_dq_dkv, v_mla_q_norm_g, v_mla_w_uq, v_mla_kv_norm_g, v_mla_w_ukv, v_mla_w_o, v_final_norm_g):
    args = dict(locals())
    w = {n: args[n] for n in WEIGHTS}
    m = {n: args["m_" + n] for n in WEIGHTS}
    v = {n: args["v_" + n] for n in WEIGHTS}
    cx, cy, cc = lax.axis_index("x"), lax.axis_index("y"), lax.axis_index("c")
    chip = 2 * cx + cy

    full = _all_gather_weights([w[n].astype(MXU_DT) for n, _ in BIG], [k for _, k in BIG], "gather_weights")
    full = dict(zip([n for n, _ in BIG], full))
    small_shapes = [w[n].shape for n in SMALL_SHARDED]
    gathered = _all_gather_rows(_pack([w[n] for n in SMALL_SHARDED]), "gather_small")
    gathered = gathered.reshape(8, -1)
    ws = {n: w[n] for n in SMALL_REPLICATED}
    pieces = [_unpack(gathered[2 * k], small_shapes) for k in range(4)]
    for i, n in enumerate(SMALL_SHARDED):
        ws[n] = jnp.concatenate([pieces[k][i] for k in range(4)], axis=-1)
    wb = {n: full[n] for n in ("xa_wq", "xa_wkv", "xa_wo", "ffn_w_up", "ffn_w_down", "pc_w_in", "pc_w_out", "mla_w_ukv")}
    wb["mla_wdq"] = jnp.stack([_pad_wdq(full["mla_w_dq_dkv"][o]) for o in range(DEPTH // 2)])
    wb["mla_wuq"] = jnp.stack([_pad_wuq(full["mla_w_uq"][o]) for o in range(DEPTH // 2)])
    wb["mla_wo"] = jnp.stack([_pad_wo(full["mla_w_o"][o]) for o in range(DEPTH // 2)])

    loss, grad_x, grads = _local_step(x[0], mem[0], positions[0], loss_target[0], wb, ws)
    loss = lax.psum(loss[0, 0], ("x", "y", "c"))

    kinds = [k for _, k in BIG]
    big = [grads[n] for n, _ in BIG]
    c_idx = cc.reshape(1).astype(jnp.int32)
    chip_idx = chip.reshape(1).astype(jnp.int32)
    theirs = _exchange_halves(big, "reduce_pair")
    pair = [_add_half(gr, th, c_idx, f"reduce_pair_add_{n}") for gr, th, (n, _) in zip(big, theirs, BIG)]
    slots = _scatter_to_chips(pair, kinds, "reduce_chips")
    halves = [_sum_chips(sl, pr, chip_idx, kind, f"reduce_chips_add_{n}")
              for sl, pr, (n, kind) in zip(slots, pair, BIG)]
    others = _swap_halves(halves, "reduce_join")
    gsum, delta, new_m, new_v = {}, {}, {}, {}
    for mine, other, (n, _) in zip(halves, others, BIG):
        gsum[n], delta[n], new_m[n], new_v[n] = _adamw_halves(w[n], mine, other, m[n], v[n], c_idx, f"adamw_{n}")

    small_names = SMALL_REPLICATED + SMALL_SHARDED
    small_grad_shapes = [grads[n].shape for n in small_names]
    packed = _pack([grads[n] for n in small_names])
    rows = packed.shape[0]
    allparts = _all_gather_rows(packed, "gather_small_grads").reshape(8, rows, LANE)
    total = _sum_leading(allparts, "sum_small_grads").reshape(-1)
    for n, gfull in zip(small_names, _unpack(total, small_grad_shapes)):
        if n in SMALL_SHARDED:
            width = w[n].shape[-1]
            gfull = lax.dynamic_slice_in_dim(gfull, chip * width, width, axis=gfull.ndim - 1)
        gsum[n] = gfull

    for n in SMALL_REPLICATED + SMALL_SHARDED:
        delta[n], new_m[n], new_v[n] = _adamw(w[n], gsum[n], m[n], v[n], f"adamw_{n}")
    return (loss, grad_x[None], *[gsum[n] for n in WEIGHTS], *[delta[n] for n in WEIGHTS],
            *[new_m[n] for n in WEIGHTS], *[new_v[n] for n in WEIGHTS])
```

```python
import functools
import math

import numpy as np
import jax
import jax.numpy as jnp
from jax import lax
from jax.experimental import pallas as pl
from jax.experimental.pallas import tpu as pltpu

F32 = jnp.float32
MXU_DT = jnp.bfloat16
XFER_DT = jnp.bfloat16

D_MODEL = 1024
DEPTH = 4
MEM_LEN = 256
XA_HEADS = 4
XA_HEAD_DIM = 256
POOL_W = 512
POOL_WINDOWS = (2, 4, 8, 16)
POOL_GROUP = 128
CONV_W = 512
CONV_K = 31
MLA_HEADS = 16
QK_NOPE = 64
QK_ROPE = 32
V_HEAD = 64
Q_LORA = 384
KV_LORA = 256
ROPE_THETA = 10000.0
MLA_SCALE = 1.0 / math.sqrt(QK_NOPE + QK_ROPE)
LOG2E = math.log2(math.e)
D_FF = 2816
FFN_CONV_K = 3
EPS = 1e-6
NEG = -1e30
ADAM_LR = 0.001
ADAM_B1 = 0.9
ADAM_B2 = 0.999
ADAM_EPS = 1e-08
ADAM_WD = 0.01
ADAM_STEP = 10

HEAD_PAD = 128
C_PAD = 768
KPE_LANE = 64

VMEM_LIMIT = 52 * 1024 * 1024
BLOCK_BYTES = 6 * 1024 * 1024
LANE = 128
SUBLANE = 8

TM = 1024
TN = 1408
TK = 2048
TT = 512
TW = 256
TWF = 128
FFN_CHUNK = 256
TA = 512
TAQ = 1024
MIX_HALO = 32
FFN_HALO = 8

NN = (((1,), (0,)), ((), ()))
NT = (((1,), (1,)), ((), ()))
TN_DIMS = (((0,), (0,)), ((), ()))
MESH_ID = pl.DeviceIdType.MESH


def _cparams(*sem):
    return pltpu.CompilerParams(dimension_semantics=sem, vmem_limit_bytes=VMEM_LIMIT)


def _tile(n, pref, limit=None):
    cap = pref if limit is None else min(pref, limit)
    if n <= cap:
        return n
    t = (cap // LANE) * LANE
    while t >= LANE:
        if n % t == 0:
            return t
        t -= LANE
    return n


def _rows(t, pref):
    return t if t <= pref else pref


def _sigmoid(x):
    return 1.0 / (1.0 + jnp.exp(-x))


def _matmul(a, b, mode, out_dtype, name, layer=None, res=None):
    if layer is None:
        b2 = b.shape
    else:
        b2 = b.shape[1:]
    if mode == "tn":
        k, m = a.shape
        k2, n = b2
    elif mode == "nn":
        m, k = a.shape
        k2, n = b2
    else:
        m, k = a.shape
        n, k2 = b2
    assert k == k2, (a.shape, b.shape, mode)
    isz_a = jnp.dtype(a.dtype).itemsize
    isz_b = jnp.dtype(b.dtype).itemsize
    if mode == "tn":
        tk = _tile(k, TK * 2 // max(isz_a, isz_b))
        tm = _tile(m, TN, BLOCK_BYTES // (tk * isz_a))
        tn = _tile(n, TN, BLOCK_BYTES // (tk * isz_b))
    else:
        tk = k
        tn = _tile(n, TN, BLOCK_BYTES // (tk * isz_b))
        tm = _tile(m, TM, min(BLOCK_BYTES // (tk * isz_a), BLOCK_BYTES // (tn * 4)))
    nk = k // tk
    grid = (m // tm, n // tn, nk)
    if mode == "nn":
        a_spec = pl.BlockSpec((tm, tk), lambda i, j, kk: (i, kk))
        b_blk, b_map, dn = (tk, tn), (lambda i, j, kk: (kk, j)), NN
    elif mode == "nt":
        a_spec = pl.BlockSpec((tm, tk), lambda i, j, kk: (i, kk))
        b_blk, b_map, dn = (tn, tk), (lambda i, j, kk: (j, kk)), NT
    else:
        a_spec = pl.BlockSpec((tk, tm), lambda i, j, kk: (kk, i))
        b_blk, b_map, dn = (tk, tn), (lambda i, j, kk: (kk, j)), TN_DIMS
    if layer is None:
        b_spec = pl.BlockSpec(b_blk, b_map)
    else:
        b_spec = pl.BlockSpec((None,) + b_blk, lambda i, j, kk: (layer,) + b_map(i, j, kk))
    o_spec = pl.BlockSpec((tm, tn), lambda i, j, kk: (i, j))
    in_specs = [a_spec, b_spec]
    args = [a, b]
    if res is not None:
        in_specs.append(pl.BlockSpec((tm, tn), lambda i, j, kk: (i, j)))
        args.append(res)
    has_res = res is not None

    def body(*refs):
        a_ref, b_ref = refs[0], refs[1]
        r_ref = refs[2] if has_res else None
        o_ref = refs[3] if has_res else refs[2]
        p = lax.dot_general(a_ref[...].astype(MXU_DT), b_ref[...].astype(MXU_DT), dn, preferred_element_type=F32)
        if nk == 1:
            if has_res:
                p = r_ref[...] + p
            o_ref[...] = p.astype(o_ref.dtype)
        else:
            acc_ref = refs[-1]
            kk = pl.program_id(2)

            @pl.when(kk == 0)
            def _():
                acc_ref[...] = jnp.zeros_like(acc_ref)

            acc_ref[...] += p

            @pl.when(kk == nk - 1)
            def _():
                r = acc_ref[...]
                if has_res:
                    r = r_ref[...] + r
                o_ref[...] = r.astype(o_ref.dtype)

    scratch = [pltpu.VMEM((tm, tn), F32)] if nk > 1 else []
    return pl.pallas_call(
        body,
        grid=grid,
        in_specs=in_specs,
        out_specs=o_spec,
        out_shape=jax.ShapeDtypeStruct((m, n), out_dtype),
        scratch_shapes=scratch,
        name=name,
        compiler_params=_cparams("parallel", "parallel", "arbitrary"),
    )(*args)


def _rms_fwd(x, g, name):
    t, d = x.shape
    tt = _rows(t, TT)

    def body(x_ref, g_ref, o_ref):
        xf = x_ref[...]
        r = lax.rsqrt(jnp.mean(xf * xf, axis=-1, keepdims=True) + EPS)
        o_ref[...] = ((xf * r) * g_ref[...]).astype(o_ref.dtype)

    return pl.pallas_call(
        body,
        grid=(t // tt,),
        in_specs=[pl.BlockSpec((tt, d), lambda i: (i, 0)), pl.BlockSpec((1, d), lambda i: (0, 0))],
        out_specs=pl.BlockSpec((tt, d), lambda i: (i, 0)),
        out_shape=jax.ShapeDtypeStruct((t, d), MXU_DT),
        name=name,
        compiler_params=_cparams("parallel"),
    )(x, g.reshape(1, d))


def _rms_bwd(dh, x, g, dx_in, name):
    t, d = x.shape
    tt = _rows(t, TT)

    def body(dh_ref, x_ref, g_ref, dxi_ref, dx_ref, dg_ref):
        @pl.when(pl.program_id(0) == 0)
        def _():
            dg_ref[...] = jnp.zeros_like(dg_ref)

        xf = x_ref[...]
        dh_v = dh_ref[...]
        r = lax.rsqrt(jnp.mean(xf * xf, axis=-1, keepdims=True) + EPS)
        xh = xf * r
        gy = dh_v * g_ref[...]
        dx = r * (gy - xh * jnp.mean(gy * xh, axis=-1, keepdims=True))
        dx_ref[...] = dxi_ref[...] + dx
        dg_ref[...] += jnp.sum(dh_v * xh, axis=0, keepdims=True)

    row = pl.BlockSpec((tt, d), lambda i: (i, 0))
    vec = pl.BlockSpec((1, d), lambda i: (0, 0))
    return pl.pallas_call(
        body,
        grid=(t // tt,),
        in_specs=[row, row, vec, row],
        out_specs=[row, vec],
        out_shape=[jax.ShapeDtypeStruct((t, d), F32), jax.ShapeDtypeStruct((1, d), F32)],
        name=name,
        compiler_params=_cparams("arbitrary"),
    )(dh, x, g.reshape(1, d), dx_in)


def _rms_bwd_gain(dh, x, g, name):
    t, d = x.shape
    tt = _rows(t, TT)

    def body(dh_ref, x_ref, dg_ref):
        @pl.when(pl.program_id(0) == 0)
        def _():
            dg_ref[...] = jnp.zeros_like(dg_ref)

        xf = x_ref[...]
        r = lax.rsqrt(jnp.mean(xf * xf, axis=-1, keepdims=True) + EPS)
        dg_ref[...] += jnp.sum(dh_ref[...] * (xf * r), axis=0, keepdims=True)

    row = pl.BlockSpec((tt, d), lambda i: (i, 0))
    vec = pl.BlockSpec((1, d), lambda i: (0, 0))
    return pl.pallas_call(
        body,
        grid=(t // tt,),
        in_specs=[row, row],
        out_specs=vec,
        out_shape=jax.ShapeDtypeStruct((1, d), F32),
        name=name,
        compiler_params=_cparams("arbitrary"),
    )(dh, x)


def _loss_head(x, target, g, name):
    t, d = x.shape
    tt = _rows(t, TT)

    def body(x_ref, t_ref, g_ref, dx_ref, dg_ref, loss_ref):
        @pl.when(pl.program_id(0) == 0)
        def _():
            dg_ref[...] = jnp.zeros_like(dg_ref)
            loss_ref[...] = jnp.zeros_like(loss_ref)

        xf = x_ref[...]
        gv = g_ref[...]
        r = lax.rsqrt(jnp.mean(xf * xf, axis=-1, keepdims=True) + EPS)
        xh = xf * r
        err = xh * gv - t_ref[...]
        e2 = jnp.sum(err * err, axis=-1, keepdims=True)
        loss_ref[...] += (0.5 / d) * jnp.sum(e2, axis=0, keepdims=True)
        dy = err * (1.0 / d)
        gy = dy * gv
        dx_ref[...] = r * (gy - xh * jnp.mean(gy * xh, axis=-1, keepdims=True))
        dg_ref[...] += jnp.sum(dy * xh, axis=0, keepdims=True)

    row = pl.BlockSpec((tt, d), lambda i: (i, 0))
    vec = pl.BlockSpec((1, d), lambda i: (0, 0))
    return pl.pallas_call(
        body,
        grid=(t // tt,),
        in_specs=[row, row, vec],
        out_specs=[row, vec, pl.BlockSpec((1, 1), lambda i: (0, 0))],
        out_shape=[
            jax.ShapeDtypeStruct((t, d), F32),
            jax.ShapeDtypeStruct((1, d), F32),
            jax.ShapeDtypeStruct((1, 1), F32),
        ],
        name=name,
        compiler_params=_cparams("arbitrary"),
    )(x, target, g.reshape(1, d))


def _prev_halo(tt, hp, width):
    return pl.BlockSpec((hp, width), lambda i: (jnp.maximum(i * (tt // hp) - 1, 0), 0))


def _next_halo(tt, hp, width, t):
    return pl.BlockSpec((hp, width), lambda i: (jnp.minimum((i + 1) * (tt // hp), t // hp - 1), 0))


def _ffn_chunks():
    return [(c0, FFN_CHUNK) for c0 in range(0, D_FF, FFN_CHUNK)]


def _ffn_fwd(up, conv_w, conv_b, name):
    t = up.shape[0]
    tt = _rows(t, TWF)
    hp = FFN_HALO

    def body(up_ref, gp_ref, w_ref, b_ref, act_ref, ext_ref):
        first = pl.program_id(0) == 0
        for c0, cw in _ffn_chunks():
            ga = pl.ds(D_FF + c0, cw)
            ext_ref[0:hp, :] = jnp.where(first, 0.0, gp_ref[:, ga])
            ext_ref[hp : hp + tt, :] = up_ref[:, ga]
            gc = b_ref[:, pl.ds(c0, cw)]
            for j in range(FFN_CONV_K):
                off = hp - (FFN_CONV_K - 1) + j
                gc = gc + w_ref[j : j + 1, pl.ds(c0, cw)] * ext_ref[off : off + tt, :]
            a = up_ref[:, pl.ds(c0, cw)]
            act_ref[:, pl.ds(c0, cw)] = (gc * _sigmoid(gc) * a).astype(act_ref.dtype)

    return pl.pallas_call(
        body,
        grid=(t // tt,),
        in_specs=[
            pl.BlockSpec((tt, 2 * D_FF), lambda i: (i, 0)),
            _prev_halo(tt, hp, 2 * D_FF),
            pl.BlockSpec((FFN_CONV_K, D_FF), lambda i: (0, 0)),
            pl.BlockSpec((1, D_FF), lambda i: (0, 0)),
        ],
        out_specs=pl.BlockSpec((tt, D_FF), lambda i: (i, 0)),
        out_shape=jax.ShapeDtypeStruct((t, D_FF), MXU_DT),
        scratch_shapes=[pltpu.VMEM((tt + hp, FFN_CHUNK), F32)],
        name=name,
        compiler_params=_cparams("parallel"),
    )(up, up, conv_w, conv_b.reshape(1, D_FF))


def _ffn_bwd(up, dact, conv_w, conv_b, name):
    t = up.shape[0]
    tt = _rows(t, TWF)
    hp = FFN_HALO
    nt = t // tt
    kk = FFN_CONV_K

    def body(up_ref, upp_ref, upn_ref, da_ref, dan_ref, w_ref, b_ref, dup_ref, dw_ref, db_ref, ext_ref, dgc_ref):
        i = pl.program_id(0)
        first = i == 0
        last = i == nt - 1

        @pl.when(first)
        def _():
            dw_ref[...] = jnp.zeros_like(dw_ref)
            db_ref[...] = jnp.zeros_like(db_ref)

        for c0, cw in _ffn_chunks():
            ca = pl.ds(c0, cw)
            ga = pl.ds(D_FF + c0, cw)
            ext_ref[0:hp, :] = jnp.where(first, 0.0, upp_ref[:, ga])
            ext_ref[hp : hp + tt, :] = up_ref[:, ga]
            ext_ref[hp + tt : hp + tt + hp, :] = upn_ref[:, ga]
            gc = b_ref[:, ca]
            for j in range(kk):
                off = hp - (kk - 1) + j
                gc = gc + w_ref[j : j + 1, ca] * ext_ref[off : off + tt + hp, :]
            sg = _sigmoid(gc)
            silu = gc * sg
            dsilu = sg * (1.0 + gc * (1.0 - sg))
            a_all = jnp.concatenate([up_ref[:, ca], upn_ref[:, ca]], axis=0)
            dact_all = jnp.concatenate([da_ref[:, ca], jnp.where(last, 0.0, dan_ref[:, ca])], axis=0)
            dgc = dact_all * a_all * dsilu
            dgc_ref[...] = dgc
            dup_ref[:, ca] = (dact_all[0:tt] * silu[0:tt]).astype(dup_ref.dtype)
            dg = jnp.zeros((tt, cw), F32)
            for j in range(kk):
                dg = dg + w_ref[j : j + 1, ca] * dgc_ref[kk - 1 - j : kk - 1 - j + tt, :]
            dup_ref[:, ga] = dg.astype(dup_ref.dtype)
            dgc_t = dgc[0:tt]
            db_ref[:, ca] += jnp.sum(dgc_t, axis=0, keepdims=True)
            for j in range(kk):
                off = hp - (kk - 1) + j
                dw_ref[j : j + 1, ca] += jnp.sum(dgc_t * ext_ref[off : off + tt, :], axis=0, keepdims=True)

    return pl.pallas_call(
        body,
        grid=(nt,),
        in_specs=[
            pl.BlockSpec((tt, 2 * D_FF), lambda i: (i, 0)),
            _prev_halo(tt, hp, 2 * D_FF),
            _next_halo(tt, hp, 2 * D_FF, t),
            pl.BlockSpec((tt, D_FF), lambda i: (i, 0)),
            _next_halo(tt, hp, D_FF, t),
            pl.BlockSpec((kk, D_FF), lambda i: (0, 0)),
            pl.BlockSpec((1, D_FF), lambda i: (0, 0)),
        ],
        out_specs=[
            pl.BlockSpec((tt, 2 * D_FF), lambda i: (i, 0)),
            pl.BlockSpec((kk, D_FF), lambda i: (0, 0)),
            pl.BlockSpec((1, D_FF), lambda i: (0, 0)),
        ],
        out_shape=[
            jax.ShapeDtypeStruct((t, 2 * D_FF), MXU_DT),
            jax.ShapeDtypeStruct((kk, D_FF), F32),
            jax.ShapeDtypeStruct((1, D_FF), F32),
        ],
        scratch_shapes=[pltpu.VMEM((tt + 2 * hp, FFN_CHUNK), F32), pltpu.VMEM((tt + hp, FFN_CHUNK), F32)],
        name=name,
        compiler_params=_cparams("arbitrary"),
    )(up, up, up, dact, dact, conv_w, conv_b.reshape(1, D_FF))


def _layernorm_silu(cv, ln_g, ln_b):
    mu = jnp.mean(cv, axis=-1, keepdims=True)
    xc = cv - mu
    rstd = lax.rsqrt(jnp.mean(xc * xc, axis=-1, keepdims=True) + EPS)
    xh = xc * rstd
    a = xh * ln_g + ln_b
    return xh, rstd, a


def _shifted_copies(ref, n):
    for b in range(1, SUBLANE):
        ref[b, 0 : n - SUBLANE, :] = ref[0, b : b + n - SUBLANE, :]


def _tap(ref, offset, rows, cols):
    b = offset % SUBLANE
    return ref[b, offset - b : offset - b + rows, cols]


def _mix_fwd(z, pool_w, pool_scale, dw_w, dw_b, ln_g, ln_b, name):
    t = z.shape[0]
    tt = _rows(t, TW)
    hp = MIX_HALO
    zw = POOL_W + 2 * CONV_W

    def body(z_ref, zp_ref, pw_ref, ps_ref, w_ref, b_ref, lg_ref, lb_ref, cat_ref, eu_ref, egl_ref, cv_ref):
        i = pl.program_id(0)
        first = i == 0
        eu_ref[0:hp, :] = jnp.where(first, 0.0, zp_ref[:, 0:POOL_W])
        eu_ref[hp : hp + tt, :] = z_ref[:, 0:POOL_W]
        glp = zp_ref[:, POOL_W : POOL_W + CONV_W] * _sigmoid(zp_ref[:, POOL_W + CONV_W : zw])
        egl_ref[0, 0:hp, :] = jnp.where(first, 0.0, glp)
        egl_ref[0, hp : hp + tt, :] = z_ref[:, POOL_W : POOL_W + CONV_W] * _sigmoid(z_ref[:, POOL_W + CONV_W : zw])
        _shifted_copies(egl_ref, tt + hp)
        row = i * tt + lax.broadcasted_iota(jnp.int32, (tt, 1), 0)
        for gi, w in enumerate(POOL_WINDOWS):
            cols = pl.ds(gi * POOL_GROUP, POOL_GROUP)
            u = eu_ref[hp : hp + tt, cols]
            acc = u
            for k in range(1, w):
                acc = acc + eu_ref[hp - k : hp - k + tt, cols]
            cnt = jnp.minimum(row + 1, w).astype(F32)
            pooled = acc / cnt - u
            y = jnp.dot(pooled.astype(MXU_DT), pw_ref[gi].astype(MXU_DT), preferred_element_type=F32)
            cat_ref[:, cols] = (y * ps_ref[:, cols]).astype(cat_ref.dtype)
        for c0 in range(0, CONV_W, LANE):
            cs = pl.ds(c0, LANE)
            acc = jnp.broadcast_to(b_ref[:, cs], (tt, LANE))
            for j in range(CONV_K):
                acc = acc + w_ref[j : j + 1, cs] * _tap(egl_ref, hp - (CONV_K - 1) + j, tt, cs)
            cv_ref[:, cs] = acc
        _, _, a = _layernorm_silu(cv_ref[...], lg_ref[...], lb_ref[...])
        cat_ref[:, POOL_W : POOL_W + CONV_W] = (a * _sigmoid(a)).astype(cat_ref.dtype)

    vec = pl.BlockSpec((1, CONV_W), lambda i: (0, 0))
    return pl.pallas_call(
        body,
        grid=(t // tt,),
        in_specs=[
            pl.BlockSpec((tt, zw), lambda i: (i, 0)),
            _prev_halo(tt, hp, zw),
            pl.BlockSpec((len(POOL_WINDOWS), POOL_GROUP, POOL_GROUP), lambda i: (0, 0, 0)),
            vec,
            pl.BlockSpec((CONV_K, CONV_W), lambda i: (0, 0)),
            vec,
            vec,
            vec,
        ],
        out_specs=pl.BlockSpec((tt, POOL_W + CONV_W), lambda i: (i, 0)),
        out_shape=jax.ShapeDtypeStruct((t, POOL_W + CONV_W), MXU_DT),
        scratch_shapes=[pltpu.VMEM((tt + hp, POOL_W), F32), pltpu.VMEM((SUBLANE, tt + hp, CONV_W), F32),
                        pltpu.VMEM((tt, CONV_W), F32)],
        name=name,
        compiler_params=_cparams("parallel"),
    )(z, z, pool_w, pool_scale.reshape(1, POOL_W), dw_w, dw_b.reshape(1, CONV_W), ln_g.reshape(1, CONV_W), ln_b.reshape(1, CONV_W))


def _mix_bwd(z, dcat, pool_w, pool_scale, dw_w, dw_b, ln_g, ln_b, name):
    t = z.shape[0]
    tt = _rows(t, TW)
    hp = MIX_HALO
    nt = t // tt
    zw = POOL_W + 2 * CONV_W
    ng = len(POOL_WINDOWS)

    def body(z_ref, zp_ref, zn_ref, dc_ref, dcn_ref, pw_ref, ps_ref, w_ref, b_ref, lg_ref, lb_ref,
             dz_ref, dpw_ref, dps_ref, dww_ref, dwb_ref, dlg_ref, dlb_ref, eu_ref, ee_ref, egl_ref, edcv_ref, cv_ref):
        i = pl.program_id(0)
        first = i == 0
        last = i == nt - 1

        @pl.when(first)
        def _():
            for r in (dpw_ref, dps_ref, dww_ref, dwb_ref, dlg_ref, dlb_ref):
                r[...] = jnp.zeros_like(r)

        eu_ref[0:hp, :] = jnp.where(first, 0.0, zp_ref[:, 0:POOL_W])
        eu_ref[hp : hp + tt, :] = z_ref[:, 0:POOL_W]
        row = i * tt + lax.broadcasted_iota(jnp.int32, (tt, 1), 0)
        row_ext = i * tt + lax.broadcasted_iota(jnp.int32, (tt + hp, 1), 0)
        for gi, w in enumerate(POOL_WINDOWS):
            cols = pl.ds(gi * POOL_GROUP, POOL_GROUP)
            u = eu_ref[hp : hp + tt, cols]
            acc = u
            for k in range(1, w):
                acc = acc + eu_ref[hp - k : hp - k + tt, cols]
            pooled = (acc / jnp.minimum(row + 1, w).astype(F32) - u).astype(MXU_DT)
            pw = pw_ref[gi].astype(MXU_DT)
            dya = dc_ref[:, cols]
            y = jnp.dot(pooled, pw, preferred_element_type=F32)
            dps_ref[:, cols] += jnp.sum(dya * y, axis=0, keepdims=True)
            scale = ps_ref[:, cols]
            dy_all = jnp.concatenate([dya, jnp.where(last, 0.0, dcn_ref[:, cols])], axis=0) * scale
            dy_all = dy_all.astype(MXU_DT)
            dpw_ref[gi] += lax.dot_general(pooled, dy_all[0:tt], TN_DIMS, preferred_element_type=F32)
            dpooled = lax.dot_general(dy_all, pw, NT, preferred_element_type=F32)
            ee_ref[:, cols] = dpooled / jnp.minimum(row_ext + 1, w).astype(F32)
            du = -dpooled[0:tt]
            for k in range(w):
                du = du + ee_ref[k : k + tt, cols]
            dz_ref[:, cols] = du.astype(dz_ref.dtype)

        ca = slice(POOL_W, POOL_W + CONV_W)
        cb = slice(POOL_W + CONV_W, zw)
        egl_ref[0, 0:hp, :] = jnp.where(first, 0.0, zp_ref[:, ca] * _sigmoid(zp_ref[:, cb]))
        egl_ref[0, hp : hp + tt, :] = z_ref[:, ca] * _sigmoid(z_ref[:, cb])
        egl_ref[0, hp + tt : hp + tt + hp, :] = zn_ref[:, ca] * _sigmoid(zn_ref[:, cb])
        _shifted_copies(egl_ref, tt + 2 * hp)
        for c0 in range(0, CONV_W, LANE):
            cs = pl.ds(c0, LANE)
            acc = jnp.broadcast_to(b_ref[:, cs], (tt + hp, LANE))
            for j in range(CONV_K):
                acc = acc + w_ref[j : j + 1, cs] * _tap(egl_ref, hp - (CONV_K - 1) + j, tt + hp, cs)
            cv_ref[:, cs] = acc
        lg = lg_ref[...]
        xh, rstd, a = _layernorm_silu(cv_ref[...], lg, lb_ref[...])
        sa = _sigmoid(a)
        dyb = jnp.concatenate([dc_ref[:, ca], jnp.where(last, 0.0, dcn_ref[:, ca])], axis=0)
        da = dyb * (sa * (1.0 + a * (1.0 - sa)))
        dlg_ref[...] += jnp.sum(da[0:tt] * xh[0:tt], axis=0, keepdims=True)
        dlb_ref[...] += jnp.sum(da[0:tt], axis=0, keepdims=True)
        dxh = da * lg
        dcv = rstd * (dxh - jnp.mean(dxh, axis=-1, keepdims=True) - xh * jnp.mean(dxh * xh, axis=-1, keepdims=True))
        edcv_ref[0] = dcv
        _shifted_copies(edcv_ref, tt + hp)
        dwb_ref[...] += jnp.sum(dcv[0:tt], axis=0, keepdims=True)
        for c0 in range(0, CONV_W, LANE):
            cs = pl.ds(c0, LANE)
            dcv_t = edcv_ref[0, 0:tt, cs]
            dgl = jnp.zeros((tt, LANE), F32)
            for j in range(CONV_K):
                tap = _tap(egl_ref, hp - (CONV_K - 1) + j, tt, cs)
                dww_ref[j : j + 1, cs] += jnp.sum(dcv_t * tap, axis=0, keepdims=True)
                dgl = dgl + w_ref[j : j + 1, cs] * _tap(edcv_ref, CONV_K - 1 - j, tt, cs)
            ga = z_ref[:, pl.ds(POOL_W + c0, LANE)]
            sgb = _sigmoid(z_ref[:, pl.ds(POOL_W + CONV_W + c0, LANE)])
            dz_ref[:, pl.ds(POOL_W + c0, LANE)] = (dgl * sgb).astype(dz_ref.dtype)
            dz_ref[:, pl.ds(POOL_W + CONV_W + c0, LANE)] = (dgl * ga * sgb * (1.0 - sgb)).astype(dz_ref.dtype)

    vec = pl.BlockSpec((1, CONV_W), lambda i: (0, 0))
    pw_spec = pl.BlockSpec((ng, POOL_GROUP, POOL_GROUP), lambda i: (0, 0, 0))
    w_spec = pl.BlockSpec((CONV_K, CONV_W), lambda i: (0, 0))
    return pl.pallas_call(
        body,
        grid=(nt,),
        in_specs=[
            pl.BlockSpec((tt, zw), lambda i: (i, 0)),
            _prev_halo(tt, hp, zw),
            _next_halo(tt, hp, zw, t),
            pl.BlockSpec((tt, POOL_W + CONV_W), lambda i: (i, 0)),
            _next_halo(tt, hp, POOL_W + CONV_W, t),
            pw_spec, vec, w_spec, vec, vec, vec,
        ],
        out_specs=[pl.BlockSpec((tt, zw), lambda i: (i, 0)), pw_spec, vec, w_spec, vec, vec, vec],
        out_shape=[
            jax.ShapeDtypeStruct((t, zw), MXU_DT),
            jax.ShapeDtypeStruct((ng, POOL_GROUP, POOL_GROUP), F32),
            jax.ShapeDtypeStruct((1, POOL_W), F32),
            jax.ShapeDtypeStruct((CONV_K, CONV_W), F32),
            jax.ShapeDtypeStruct((1, CONV_W), F32),
            jax.ShapeDtypeStruct((1, CONV_W), F32),
            jax.ShapeDtypeStruct((1, CONV_W), F32),
        ],
        scratch_shapes=[
            pltpu.VMEM((tt + hp, POOL_W), F32),
            pltpu.VMEM((tt + hp, POOL_W), F32),
            pltpu.VMEM((SUBLANE, tt + 2 * hp, CONV_W), F32),
            pltpu.VMEM((SUBLANE, tt + hp, CONV_W), F32),
            pltpu.VMEM((tt + hp, CONV_W), F32),
        ],
        name=name,
        compiler_params=_cparams("arbitrary"),
    )(z, z, z, dcat, dcat, pool_w, pool_scale.reshape(1, POOL_W), dw_w, dw_b.reshape(1, CONV_W),
      ln_g.reshape(1, CONV_W), ln_b.reshape(1, CONV_W))


def _xa_fwd(q, kvm, name):
    t = q.shape[0]
    tt = _rows(t, TT)
    scale = XA_HEAD_DIM ** -0.5

    def body(q_ref, kv_ref, o_ref):
        for h in range(XA_HEADS):
            cs = pl.ds(h * XA_HEAD_DIM, XA_HEAD_DIM)
            vs = pl.ds(D_MODEL + h * XA_HEAD_DIM, XA_HEAD_DIM)
            s = lax.dot_general(q_ref[:, cs], kv_ref[:, cs], NT, preferred_element_type=F32) * scale
            p = jnp.exp(s - jnp.max(s, axis=-1, keepdims=True))
            p = p / jnp.sum(p, axis=-1, keepdims=True)
            o_ref[:, cs] = jnp.dot(p.astype(MXU_DT), kv_ref[:, vs], preferred_element_type=F32).astype(o_ref.dtype)

    return pl.pallas_call(
        body,
        grid=(t // tt,),
        in_specs=[pl.BlockSpec((tt, D_MODEL), lambda i: (i, 0)), pl.BlockSpec((MEM_LEN, 2 * D_MODEL), lambda i: (0, 0))],
        out_specs=pl.BlockSpec((tt, D_MODEL), lambda i: (i, 0)),
        out_shape=jax.ShapeDtypeStruct((t, D_MODEL), MXU_DT),
        name=name,
        compiler_params=_cparams("parallel"),
    )(q, kvm)


def _xa_bwd(q, kvm, do, name):
    t = q.shape[0]
    tt = _rows(t, TT)
    scale = XA_HEAD_DIM ** -0.5

    def body(q_ref, kv_ref, do_ref, dq_ref, dkv_ref):
        @pl.when(pl.program_id(0) == 0)
        def _():
            dkv_ref[...] = jnp.zeros_like(dkv_ref)

        for h in range(XA_HEADS):
            cs = pl.ds(h * XA_HEAD_DIM, XA_HEAD_DIM)
            vs = pl.ds(D_MODEL + h * XA_HEAD_DIM, XA_HEAD_DIM)
            qh = q_ref[:, cs]
            kh = kv_ref[:, cs]
            doh = do_ref[:, cs]
            s = lax.dot_general(qh, kh, NT, preferred_element_type=F32) * scale
            p = jnp.exp(s - jnp.max(s, axis=-1, keepdims=True))
            p = p / jnp.sum(p, axis=-1, keepdims=True)
            dp = lax.dot_general(doh, kv_ref[:, vs], NT, preferred_element_type=F32)
            ds = (p * (dp - jnp.sum(p * dp, axis=-1, keepdims=True)) * scale).astype(MXU_DT)
            dq_ref[:, cs] = jnp.dot(ds, kh, preferred_element_type=F32).astype(dq_ref.dtype)
            dkv_ref[:, cs] += lax.dot_general(ds, qh, TN_DIMS, preferred_element_type=F32)
            dkv_ref[:, vs] += lax.dot_general(p.astype(MXU_DT), doh, TN_DIMS, preferred_element_type=F32)

    row = pl.BlockSpec((tt, D_MODEL), lambda i: (i, 0))
    kvs = pl.BlockSpec((MEM_LEN, 2 * D_MODEL), lambda i: (0, 0))
    return pl.pallas_call(
        body,
        grid=(t // tt,),
        in_specs=[row, kvs, row],
        out_specs=[row, kvs],
        out_shape=[jax.ShapeDtypeStruct((t, D_MODEL), MXU_DT), jax.ShapeDtypeStruct((MEM_LEN, 2 * D_MODEL), F32)],
        name=name,
        compiler_params=_cparams("arbitrary"),
    )(q, kvm, do)


def _rope_tables(positions, name):
    t = positions.shape[0]
    tt = _rows(t, TT)
    inv = 1.0 / (ROPE_THETA ** (np.arange(0, QK_ROPE, 2, dtype=np.float32) / QK_ROPE))
    lanes = np.zeros((1, HEAD_PAD), np.float32)
    half = QK_ROPE // 2
    lanes[0, KPE_LANE : KPE_LANE + half] = inv
    lanes[0, KPE_LANE + half : KPE_LANE + QK_ROPE] = inv

    def body(pos_ref, inv_ref, cos_ref, sa_ref, sb_ref):
        ang = pos_ref[...].astype(F32) * inv_ref[...]
        lane = lax.broadcasted_iota(jnp.int32, (tt, HEAD_PAD), 1)
        c = jnp.cos(ang)
        s = jnp.sin(ang)
        lo = (lane >= KPE_LANE) & (lane < KPE_LANE + half)
        hi = (lane >= KPE_LANE + half) & (lane < KPE_LANE + QK_ROPE)
        cos_ref[...] = jnp.where(lo | hi, c, 1.0)
        sa_ref[...] = jnp.where(hi, s, 0.0)
        sb_ref[...] = jnp.where(lo, -s, 0.0)

    tab = pl.BlockSpec((tt, HEAD_PAD), lambda i: (i, 0))
    return pl.pallas_call(
        body,
        grid=(t // tt,),
        in_specs=[pl.BlockSpec((tt, 1), lambda i: (i, 0)), pl.BlockSpec((1, HEAD_PAD), lambda i: (0, 0))],
        out_specs=[tab, tab, tab],
        out_shape=[jax.ShapeDtypeStruct((t, HEAD_PAD), F32)] * 3,
        name=name,
        compiler_params=_cparams("parallel"),
    )(positions, jnp.asarray(lanes))


def _rotate(x, cos, sa, sb, sign):
    half = QK_ROPE // 2
    return x * cos + sign * (pltpu.roll(x, half, 1) * sa + pltpu.roll(x, HEAD_PAD - half, 1) * sb)


def _rope_heads(x, tables, sign, scale, name):
    t, w = x.shape
    tt = _rows(t, TT)
    nh = w // HEAD_PAD

    def body(x_ref, c_ref, sa_ref, sb_ref, o_ref):
        cos, sa, sb = c_ref[...] * scale, sa_ref[...] * scale, sb_ref[...] * scale
        for h in range(nh):
            cs = pl.ds(h * HEAD_PAD, HEAD_PAD)
            o_ref[:, cs] = _rotate(x_ref[:, cs], cos, sa, sb, sign).astype(o_ref.dtype)

    tab = pl.BlockSpec((tt, HEAD_PAD), lambda i: (i, 0))
    row = pl.BlockSpec((tt, w), lambda i: (i, 0))
    return pl.pallas_call(
        body,
        grid=(t // tt,),
        in_specs=[row, tab, tab, tab],
        out_specs=row,
        out_shape=jax.ShapeDtypeStruct((t, w), MXU_DT),
        name=name,
        compiler_params=_cparams("parallel"),
    )(x, *tables)


def _mla_prep(cp, qg, kvg, tables, name):
    t = cp.shape[0]
    tt = _rows(t, TT)

    def body(cp_ref, qg_ref, kvg_ref, c_ref, sa_ref, sb_ref, qn_ref, kvn_ref, kpe_ref):
        cq = cp_ref[:, 0:Q_LORA]
        r = lax.rsqrt(jnp.mean(cq * cq, axis=-1, keepdims=True) + EPS)
        qn_ref[...] = ((cq * r) * qg_ref[...]).astype(qn_ref.dtype)
        ckv = cp_ref[:, Q_LORA : Q_LORA + KV_LORA]
        r = lax.rsqrt(jnp.mean(ckv * ckv, axis=-1, keepdims=True) + EPS)
        kvn_ref[...] = ((ckv * r) * kvg_ref[...]).astype(kvn_ref.dtype)
        kpe = cp_ref[:, Q_LORA + KV_LORA : C_PAD]
        kpe_ref[...] = _rotate(kpe, c_ref[...], sa_ref[...], sb_ref[...], 1.0).astype(kpe_ref.dtype)

    tab = pl.BlockSpec((tt, HEAD_PAD), lambda i: (i, 0))
    return pl.pallas_call(
        body,
        grid=(t // tt,),
        in_specs=[
            pl.BlockSpec((tt, C_PAD), lambda i: (i, 0)),
            pl.BlockSpec((1, Q_LORA), lambda i: (0, 0)),
            pl.BlockSpec((1, KV_LORA), lambda i: (0, 0)),
            tab, tab, tab,
        ],
        out_specs=[
            pl.BlockSpec((tt, Q_LORA), lambda i: (i, 0)),
            pl.BlockSpec((tt, KV_LORA), lambda i: (i, 0)),
            tab,
        ],
        out_shape=[
            jax.ShapeDtypeStruct((t, Q_LORA), MXU_DT),
            jax.ShapeDtypeStruct((t, KV_LORA), MXU_DT),
            jax.ShapeDtypeStruct((t, HEAD_PAD), MXU_DT),
        ],
        name=name,
        compiler_params=_cparams("parallel"),
    )(cp, qg.reshape(1, Q_LORA), kvg.reshape(1, KV_LORA), *tables)


def _mla_prep_bwd(cp, dqn, dkvn, dkpe_heads, qg, kvg, tables, name):
    t = cp.shape[0]
    tt = _rows(t, TT)

    def norm_bwd(x, dy, g):
        r = lax.rsqrt(jnp.mean(x * x, axis=-1, keepdims=True) + EPS)
        xh = x * r
        gy = dy * g
        return r * (gy - xh * jnp.mean(gy * xh, axis=-1, keepdims=True)), jnp.sum(dy * xh, axis=0, keepdims=True)

    def body(cp_ref, dqn_ref, dkvn_ref, dkpe_ref, qg_ref, kvg_ref, c_ref, sa_ref, sb_ref, dcp_ref, dqg_ref, dkvg_ref):
        @pl.when(pl.program_id(0) == 0)
        def _():
            dqg_ref[...] = jnp.zeros_like(dqg_ref)
            dkvg_ref[...] = jnp.zeros_like(dkvg_ref)

        dcq, dg = norm_bwd(cp_ref[:, 0:Q_LORA], dqn_ref[...], qg_ref[...])
        dcp_ref[:, 0:Q_LORA] = dcq.astype(dcp_ref.dtype)
        dqg_ref[...] += dg
        dckv, dg = norm_bwd(cp_ref[:, Q_LORA : Q_LORA + KV_LORA], dkvn_ref[...], kvg_ref[...])
        dcp_ref[:, Q_LORA : Q_LORA + KV_LORA] = dckv.astype(dcp_ref.dtype)
        dkvg_ref[...] += dg
        dk = dkpe_ref[0]
        for h in range(1, MLA_HEADS):
            dk = dk + dkpe_ref[h]
        dcp_ref[:, Q_LORA + KV_LORA : C_PAD] = _rotate(dk, c_ref[...], sa_ref[...], sb_ref[...], -1.0).astype(dcp_ref.dtype)

    tab = pl.BlockSpec((tt, HEAD_PAD), lambda i: (i, 0))
    return pl.pallas_call(
        body,
        grid=(t // tt,),
        in_specs=[
            pl.BlockSpec((tt, C_PAD), lambda i: (i, 0)),
            pl.BlockSpec((tt, Q_LORA), lambda i: (i, 0)),
            pl.BlockSpec((tt, KV_LORA), lambda i: (i, 0)),
            pl.BlockSpec((MLA_HEADS, tt, HEAD_PAD), lambda i: (0, i, 0)),
            pl.BlockSpec((1, Q_LORA), lambda i: (0, 0)),
            pl.BlockSpec((1, KV_LORA), lambda i: (0, 0)),
            tab, tab, tab,
        ],
        out_specs=[
            pl.BlockSpec((tt, C_PAD), lambda i: (i, 0)),
            pl.BlockSpec((1, Q_LORA), lambda i: (0, 0)),
            pl.BlockSpec((1, KV_LORA), lambda i: (0, 0)),
        ],
        out_shape=[
            jax.ShapeDtypeStruct((t, C_PAD), MXU_DT),
            jax.ShapeDtypeStruct((1, Q_LORA), F32),
            jax.ShapeDtypeStruct((1, KV_LORA), F32),
        ],
        name=name,
        compiler_params=_cparams("arbitrary"),
    )(cp, dqn, dkvn, dkpe_heads, qg.reshape(1, Q_LORA), kvg.reshape(1, KV_LORA), *tables)


def _flash_fwd(qs, kv, kpe, name):
    t = qs.shape[0]
    ta = _rows(t, TA)
    tq = _rows(t, TAQ)
    nq = t // tq
    per = tq // ta

    def body(q_ref, kv_ref, kpe_ref, o_ref, lse_ref):
        qi = pl.program_id(1)
        q = q_ref[...]
        lane = lax.broadcasted_iota(jnp.int32, (ta, HEAD_PAD), 1)

        def kblock(j):
            rows = pl.ds(pl.multiple_of(j * ta, ta), ta)
            kvb = kv_ref[rows, :]
            ones_v = jnp.where(lane < QK_NOPE, jnp.ones_like(kvb), kvb)
            return ones_v, jnp.where(lane < QK_NOPE, kvb, kpe_ref[rows, :])

        def update(carry, s, ones_v):
            m, acc = carry
            m_new = jnp.maximum(m, jnp.max(s, axis=-1, keepdims=True))
            p = jnp.exp2(s - m_new).astype(MXU_DT)
            acc = jnp.exp2(m - m_new) * acc + jnp.dot(p, ones_v, preferred_element_type=F32)
            return m_new, acc

        def step(j, carry):
            ones_v, k = kblock(j)
            return update(carry, lax.dot_general(q, k, NT, preferred_element_type=F32), ones_v)

        init = (jnp.full((tq, 1), -jnp.inf, F32), jnp.zeros((tq, HEAD_PAD), F32))
        carry = lax.fori_loop(0, qi * per, step, init)
        r = lax.broadcasted_iota(jnp.int32, (tq, ta), 0)
        c = lax.broadcasted_iota(jnp.int32, (tq, ta), 1)
        for d in range(per):
            ones_v, k = kblock(qi * per + d)
            s = lax.dot_general(q, k, NT, preferred_element_type=F32)
            carry = update(carry, jnp.where(c + d * ta <= r, s, NEG), ones_v)
        m, acc = carry
        l = acc[:, 0:1]
        lane_q = lax.broadcasted_iota(jnp.int32, (tq, HEAD_PAD), 1)
        o_ref[...] = jnp.where(lane_q >= QK_NOPE, acc / l, 0.0).astype(o_ref.dtype)
        lse_ref[...] = m + jnp.log2(l)

    return pl.pallas_call(
        body,
        grid=(MLA_HEADS, nq),
        in_specs=[
            pl.BlockSpec((tq, HEAD_PAD), lambda h, i: (i, h)),
            pl.BlockSpec((t, HEAD_PAD), lambda h, i: (0, h)),
            pl.BlockSpec((t, HEAD_PAD), lambda h, i: (0, 0)),
        ],
        out_specs=[
            pl.BlockSpec((tq, HEAD_PAD), lambda h, i: (i, h)),
            pl.BlockSpec((None, tq, 1), lambda h, i: (h, i, 0)),
        ],
        out_shape=[
            jax.ShapeDtypeStruct((t, MLA_HEADS * HEAD_PAD), MXU_DT),
            jax.ShapeDtypeStruct((MLA_HEADS, t, 1), F32),
        ],
        name=name,
        compiler_params=_cparams("parallel", "parallel"),
    )(qs, kv, kpe)


def _flash_bwd(qs, kv, kpe, o, do, lse, name):
    t = qs.shape[0]
    ta = _rows(t, TA)
    tq = _rows(t, TAQ)
    nq = t // ta
    per = tq // ta

    def body(q_ref, o_ref, do_ref, lse_ref, kv_ref, kpe_ref, dq_ref, dkv_ref, dkpe_ref, dk_acc, dv_acc):
        kj = pl.program_id(1)

        @pl.when(kj == 0)
        def _():
            dq_ref[...] = jnp.zeros_like(dq_ref)

        lane = lax.broadcasted_iota(jnp.int32, (ta, HEAD_PAD), 1)
        kvb = kv_ref[...]
        k = jnp.where(lane < QK_NOPE, kvb, kpe_ref[...])
        dk_acc[...] = jnp.zeros_like(dk_acc)
        dv_acc[...] = jnp.zeros_like(dv_acc)

        def tile(qq, first_key):
            rows = pl.ds(pl.multiple_of(qq * tq, tq), tq)
            q = q_ref[rows, :]
            dob = do_ref[rows, :]
            delta = jnp.sum(dob.astype(F32) * o_ref[rows, :].astype(F32), axis=-1, keepdims=True)
            s = lax.dot_general(q, k, NT, preferred_element_type=F32)
            if first_key is not None:
                r = lax.broadcasted_iota(jnp.int32, (tq, ta), 0)
                c = lax.broadcasted_iota(jnp.int32, (tq, ta), 1)
                s = jnp.where(c + first_key <= r, s, NEG)
            p = jnp.exp2(s - lse_ref[rows, :])
            dp = lax.dot_general(dob, kvb, NT, preferred_element_type=F32)
            ds = (p * (dp - delta)).astype(MXU_DT)
            dq_ref[rows, :] += jnp.dot(ds, k, preferred_element_type=F32)
            dk_acc[...] += lax.dot_general(ds, q, TN_DIMS, preferred_element_type=F32)
            dv_acc[...] += lax.dot_general(p.astype(MXU_DT), dob, TN_DIMS, preferred_element_type=F32)

        qq0 = kj // per
        tile(qq0, (kj - qq0 * per) * ta)

        def step(qq, carry):
            tile(qq, None)
            return carry

        lax.fori_loop(qq0 + 1, t // tq, step, 0)
        dk = dk_acc[...] * (1.0 / LOG2E)
        dkv_ref[...] = jnp.where(lane < QK_NOPE, dk, dv_acc[...]).astype(dkv_ref.dtype)
        dkpe_ref[...] = jnp.where((lane >= KPE_LANE) & (lane < KPE_LANE + QK_ROPE), dk, 0.0)

    head_rows = pl.BlockSpec((t, HEAD_PAD), lambda h, j: (0, h))
    return pl.pallas_call(
        body,
        grid=(MLA_HEADS, nq),
        in_specs=[
            head_rows,
            head_rows,
            head_rows,
            pl.BlockSpec((None, t, 1), lambda h, j: (h, 0, 0)),
            pl.BlockSpec((ta, HEAD_PAD), lambda h, j: (j, h)),
            pl.BlockSpec((ta, HEAD_PAD), lambda h, j: (j, 0)),
        ],
        out_specs=[
            head_rows,
            pl.BlockSpec((ta, HEAD_PAD), lambda h, j: (j, h)),
            pl.BlockSpec((None, ta, HEAD_PAD), lambda h, j: (h, j, 0)),
        ],
        out_shape=[
            jax.ShapeDtypeStruct((t, MLA_HEADS * HEAD_PAD), F32),
            jax.ShapeDtypeStruct((t, MLA_HEADS * HEAD_PAD), MXU_DT),
            jax.ShapeDtypeStruct((MLA_HEADS, t, HEAD_PAD), F32),
        ],
        scratch_shapes=[pltpu.VMEM((ta, HEAD_PAD), F32), pltpu.VMEM((ta, HEAD_PAD), F32)],
        name=name,
        compiler_params=_cparams("parallel", "arbitrary"),
    )(qs, o, do, lse, kv, kpe)


def _as2d(a):
    if a.ndim == 1:
        return a.reshape(1, a.shape[0])
    return a.reshape(-1, a.shape[-1])


def _adamw(w, g, m, v, name):
    shape = w.shape
    w2, g2, m2, v2 = (_as2d(a) for a in (w, g, m, v))
    r, c = w2.shape
    tr = _tile_rows(r, c)
    c1 = 1.0 - ADAM_B1 ** ADAM_STEP
    c2 = 1.0 - ADAM_B2 ** ADAM_STEP

    def body(w_ref, g_ref, m_ref, v_ref, d_ref, nm_ref, nv_ref):
        gv = g_ref[...]
        nm = ADAM_B1 * m_ref[...] + (1.0 - ADAM_B1) * gv
        nv = ADAM_B2 * v_ref[...] + (1.0 - ADAM_B2) * (gv * gv)
        d_ref[...] = -ADAM_LR * ((nm / c1) / (jnp.sqrt(nv / c2) + ADAM_EPS) + ADAM_WD * w_ref[...])
        nm_ref[...] = nm
        nv_ref[...] = nv

    blk = pl.BlockSpec((tr, c), lambda i: (i, 0))
    outs = pl.pallas_call(
        body,
        grid=(r // tr,),
        in_specs=[blk] * 4,
        out_specs=[blk] * 3,
        out_shape=[jax.ShapeDtypeStruct((r, c), F32)] * 3,
        name=name,
        compiler_params=_cparams("parallel"),
    )(w2, g2, m2, v2)
    return tuple(o.reshape(shape) for o in outs)


def _adamw_halves(w, mine, other, m, v, c_idx, name):
    nl, r, c = w.shape
    h = nl // 2
    tr = _tile_rows(r, 2 * c)
    c1 = 1.0 - ADAM_B1 ** ADAM_STEP
    c2 = 1.0 - ADAM_B2 ** ADAM_STEP

    def body(c_ref, w_ref, a_ref, b_ref, m_ref, v_ref, g_ref, d_ref, nm_ref, nv_ref):
        l = pl.program_id(0)
        gv = jnp.where(l // h == c_ref[0], a_ref[...], b_ref[...])
        nm = ADAM_B1 * m_ref[...] + (1.0 - ADAM_B1) * gv
        nv = ADAM_B2 * v_ref[...] + (1.0 - ADAM_B2) * (gv * gv)
        g_ref[...] = gv
        d_ref[...] = -ADAM_LR * ((nm / c1) / (jnp.sqrt(nv / c2) + ADAM_EPS) + ADAM_WD * w_ref[...])
        nm_ref[...] = nm
        nv_ref[...] = nv

    def half_map(mine_side):
        def index(l, i, cr):
            first = cr[0] * h if mine_side else (1 - cr[0]) * h
            return (jnp.clip(l - first, 0, h - 1), i, 0)
        return index

    full = pl.BlockSpec((None, tr, c), lambda l, i, cr: (l, i, 0))
    grid_spec = pltpu.PrefetchScalarGridSpec(
        num_scalar_prefetch=1,
        grid=(nl, r // tr),
        in_specs=[full, pl.BlockSpec((None, tr, c), half_map(True)), pl.BlockSpec((None, tr, c), half_map(False)), full, full],
        out_specs=[full] * 4,
    )
    return pl.pallas_call(
        body,
        grid_spec=grid_spec,
        out_shape=[jax.ShapeDtypeStruct((nl, r, c), F32)] * 4,
        name=name,
        compiler_params=_cparams("parallel", "parallel"),
    )(c_idx, w, mine, other, m, v)


def _tile_rows(r, c, mult=SUBLANE):
    limit = max(mult, (BLOCK_BYTES // 4) // (4 * c))
    if r <= limit:
        return r
    t = (limit // mult) * mult
    while t >= mult:
        if r % t == 0:
            return t
        t -= mult
    return r


def _sum_leading(a, name):
    n, r, c = a.shape
    tr = _tile_rows(r, c * n)

    def body(a_ref, o_ref):
        s = a_ref[0]
        for k in range(1, n):
            s = s + a_ref[k]
        o_ref[...] = s

    return pl.pallas_call(
        body,
        grid=(r // tr,),
        in_specs=[pl.BlockSpec((n, tr, c), lambda i: (0, i, 0))],
        out_specs=pl.BlockSpec((tr, c), lambda i: (i, 0)),
        out_shape=jax.ShapeDtypeStruct((r, c), F32),
        name=name,
        compiler_params=_cparams("parallel"),
    )(a)


def _add_half(g, s, c_idx, name):
    nl, r, c = g.shape
    h = nl // 2
    tr = _tile_rows(r, 2 * c, 2 * SUBLANE)

    def body(c_ref, g_ref, s_ref, o_ref):
        o_ref[...] = (g_ref[...] + s_ref[...]).astype(o_ref.dtype)

    grid_spec = pltpu.PrefetchScalarGridSpec(
        num_scalar_prefetch=1,
        grid=(h, r // tr),
        in_specs=[
            pl.BlockSpec((None, tr, c), lambda l, i, cr: (cr[0] * h + l, i, 0)),
            pl.BlockSpec((None, tr, c), lambda l, i, cr: (l, i, 0)),
        ],
        out_specs=pl.BlockSpec((None, tr, c), lambda l, i, cr: (l, i, 0)),
    )
    return pl.pallas_call(
        body,
        grid_spec=grid_spec,
        out_shape=jax.ShapeDtypeStruct((h, r, c), XFER_DT),
        name=name,
        compiler_params=_cparams("parallel", "parallel"),
    )(c_idx, g, s)


def _sum_chips(slots, pair, chip_idx, kind, name):
    _, h, r, c = slots.shape
    tr = _tile_rows(r, 5 * c, 2 * SUBLANE)
    nr = r // tr

    def body(chip_ref, s_ref, own_ref, o_ref):
        chip = chip_ref[0]
        own = own_ref[...].astype(F32)
        parts = [s_ref[j].astype(F32) for j in range(3)]
        total = None
        for k in range(4):
            d = jnp.bitwise_xor(chip, k)
            v = jnp.where(d == 0, own, jnp.where(d == 2, parts[0], jnp.where(d == 1, parts[1], parts[2])))
            total = v if total is None else total + v
        o_ref[...] = total

    if kind == "row":
        own_spec = pl.BlockSpec((None, tr, c), lambda l, i, cr: (l, cr[0] * nr + i, 0))
    else:
        own_spec = pl.BlockSpec((None, tr, c), lambda l, i, cr: (l, i, cr[0]))
    grid_spec = pltpu.PrefetchScalarGridSpec(
        num_scalar_prefetch=1,
        grid=(h, nr),
        in_specs=[pl.BlockSpec((3, None, tr, c), lambda l, i, cr: (0, l, i, 0)), own_spec],
        out_specs=pl.BlockSpec((None, tr, c), lambda l, i, cr: (l, i, 0)),
    )
    return pl.pallas_call(
        body,
        grid_spec=grid_spec,
        out_shape=jax.ShapeDtypeStruct((h, r, c), F32),
        name=name,
        compiler_params=_cparams("parallel", "parallel"),
    )(chip_idx, slots, pair)


def _mesh_pos():
    return lax.axis_index("x"), lax.axis_index("y"), lax.axis_index("c")


def _other_chips(x, y):
    return [(1 - x, y), (x, 1 - y), (1 - x, 1 - y)]


def _all_gather_rows(block, name):
    m_per, n = block.shape

    def body(x_ref, out_ref, send_sems, recv_sems, local_sem):
        x, y, c = _mesh_pos()
        me, sibling = (x, y, c), (x, y, 1 - c)
        chips = _other_chips(x, y)

        def rows(px, py, pc):
            return out_ref.at[pl.ds((4 * px + 2 * py + pc) * m_per, m_per), :]

        def copy(k, blk, to, src=None):
            return pltpu.make_async_remote_copy(
                src_ref=rows(*blk) if src is None else src,
                dst_ref=rows(*blk),
                send_sem=send_sems.at[k],
                recv_sem=recv_sems.at[k],
                device_id=to,
                device_id_type=MESH_ID,
            )

        mine = pltpu.make_async_copy(x_ref, rows(*me), local_sem)
        mine.start()
        first = [copy(0, me, sibling, src=x_ref)]
        first += [copy(1 + j, me, (*chip, c), src=x_ref) for j, chip in enumerate(chips)]
        for cp in first:
            cp.start()
        passed = [copy(4 + j, (*chip, c), sibling) for j, chip in enumerate(chips)]
        for j, chip in enumerate(chips):
            copy(1 + j, (*chip, c), me).wait_recv()
            passed[j].start()
        copy(0, sibling, me).wait_recv()
        for j, chip in enumerate(chips):
            copy(4 + j, (*chip, 1 - c), me).wait_recv()
        for cp in first + passed:
            cp.wait_send()
        mine.wait()

    return pl.pallas_call(
        body,
        out_shape=jax.ShapeDtypeStruct((8 * m_per, n), block.dtype),
        in_specs=[pl.BlockSpec(memory_space=pltpu.VMEM)],
        out_specs=pl.BlockSpec(memory_space=pltpu.VMEM),
        scratch_shapes=[pltpu.SemaphoreType.DMA((7,)), pltpu.SemaphoreType.DMA((7,)), pltpu.SemaphoreType.DMA],
        name=name,
        compiler_params=pltpu.CompilerParams(vmem_limit_bytes=VMEM_LIMIT),
    )(block)


def _shard_window(ref, layers, chip, rows, cols):
    if rows is not None:
        return ref.at[layers, pl.ds(pl.multiple_of(chip * rows, rows), rows), :]
    return ref.at[layers, :, pl.ds(pl.multiple_of(chip * cols, cols), cols)]


def _all_gather_weights(shards, kinds, name):
    nw = len(shards)
    out_shapes = []
    for s, kind in zip(shards, kinds):
        nl, r, c = s.shape
        full = (nl, 4 * r, c) if kind == "row" else (nl, r, 4 * c)
        out_shapes.append(jax.ShapeDtypeStruct(full, s.dtype))

    def body(*refs):
        ins, outs = refs[:nw], refs[nw : 2 * nw]
        send_sems, recv_sems, in_sems, out_sems = refs[2 * nw : 2 * nw + 4]
        bufs = refs[2 * nw + 4 :]
        x, y, c = _mesh_pos()
        sibling = (x, y, 1 - c)
        chips = _other_chips(x, y)
        my_chip = 2 * x + y

        def window(w, chip, layers):
            _, r, cc = shards[w].shape
            if kinds[w] == "row":
                return _shard_window(outs[w], layers, chip, r, None)
            return _shard_window(outs[w], layers, chip, None, cc)

        def half(w, half_idx):
            h = shards[w].shape[0] // 2
            return pl.ds(half_idx * h, h)

        def copy(w, k, src, dst, to):
            return pltpu.make_async_remote_copy(
                src_ref=src, dst_ref=dst, send_sem=send_sems.at[w, k], recv_sem=recv_sems.at[w, k],
                device_id=to, device_id_type=MESH_ID)

        sent = []
        for w in range(nw):
            mine = ins[w].at[half(w, c)]
            for j, chip in enumerate(chips):
                cp = copy(w, j, mine, window(w, my_chip, half(w, c)), (*chip, c))
                cp.start()
                sent.append(cp)
        for w in range(nw):
            nl = shards[w].shape[0]

            def load(l, w=w):
                return pltpu.make_async_copy(ins[w].at[l], bufs[w].at[l % 2], in_sems.at[w, l % 2])

            def store(l, w=w):
                return pltpu.make_async_copy(bufs[w].at[l % 2], window(w, my_chip, l), out_sems.at[w, l % 2])

            load(0).start()
            for l in range(nl):
                load(l).wait()
                store(l).start()
                if l + 1 < nl:
                    if l >= 1:
                        store(l - 1).wait()
                    load(l + 1).start()
            for l in range(max(nl - 2, 0), nl):
                store(l).wait()
        for w in range(nw):
            for j, (cx, cy) in enumerate(chips):
                got = window(w, 2 * cx + cy, half(w, c))
                copy(w, j, got, got, (cx, cy, c)).wait_recv()
                cp = copy(w, 3 + j, got, got, sibling)
                cp.start()
                sent.append(cp)
        for w in range(nw):
            for j, (cx, cy) in enumerate(chips):
                got = window(w, 2 * cx + cy, half(w, 1 - c))
                copy(w, 3 + j, got, got, sibling).wait_recv()
        for cp in sent:
            cp.wait_send()

    anyspec = pl.BlockSpec(memory_space=pl.ANY)
    return pl.pallas_call(
        body,
        out_shape=out_shapes,
        in_specs=[anyspec] * nw,
        out_specs=[anyspec] * nw,
        scratch_shapes=[pltpu.SemaphoreType.DMA((nw, 6)), pltpu.SemaphoreType.DMA((nw, 6)),
                        pltpu.SemaphoreType.DMA((nw, 2)), pltpu.SemaphoreType.DMA((nw, 2))]
        + [pltpu.VMEM((2,) + s.shape[1:], s.dtype) for s in shards],
        name=name,
        compiler_params=pltpu.CompilerParams(vmem_limit_bytes=VMEM_LIMIT),
    )(*shards)


def _exchange_halves(grads, name):
    nw = len(grads)
    out_shapes = [jax.ShapeDtypeStruct((g.shape[0] // 2,) + g.shape[1:], g.dtype) for g in grads]

    def body(*refs):
        ins, outs = refs[:nw], refs[nw : 2 * nw]
        send_sems, recv_sems = refs[2 * nw :]
        x, y, c = _mesh_pos()
        cps = []
        for w in range(nw):
            h = grads[w].shape[0] // 2
            cp = pltpu.make_async_remote_copy(
                src_ref=ins[w].at[pl.ds((1 - c) * h, h)], dst_ref=outs[w], send_sem=send_sems.at[w],
                recv_sem=recv_sems.at[w], device_id=(x, y, 1 - c), device_id_type=MESH_ID)
            cp.start()
            cps.append(cp)
        for cp in cps:
            cp.wait()

    anyspec = pl.BlockSpec(memory_space=pl.ANY)
    return pl.pallas_call(
        body,
        out_shape=out_shapes,
        in_specs=[anyspec] * nw,
        out_specs=[anyspec] * nw,
        scratch_shapes=[pltpu.SemaphoreType.DMA((nw,)), pltpu.SemaphoreType.DMA((nw,))],
        name=name,
    )(*grads)


def _scatter_to_chips(parts, kinds, name):
    nw = len(parts)
    shard_shapes = []
    for p, kind in zip(parts, kinds):
        h, r, c = p.shape
        shard_shapes.append((h, r // 4, c) if kind == "row" else (h, r, c // 4))
    out_shapes = [jax.ShapeDtypeStruct((3,) + s, p.dtype) for s, p in zip(shard_shapes, parts)]

    def body(*refs):
        ins, outs = refs[:nw], refs[nw : 2 * nw]
        send_sems, recv_sems = refs[2 * nw :]
        x, y, c = _mesh_pos()
        chips = _other_chips(x, y)

        def piece(w, chip):
            h, r, cc = shard_shapes[w]
            if kinds[w] == "row":
                return _shard_window(ins[w], pl.ds(0, h), chip, r, None)
            return _shard_window(ins[w], pl.ds(0, h), chip, None, cc)

        def copy(w, j, cx, cy):
            return pltpu.make_async_remote_copy(
                src_ref=piece(w, 2 * cx + cy), dst_ref=outs[w].at[j], send_sem=send_sems.at[w, j],
                recv_sem=recv_sems.at[w, j], device_id=(cx, cy, c), device_id_type=MESH_ID)

        cps = [copy(w, j, cx, cy) for w in range(nw) for j, (cx, cy) in enumerate(chips)]
        for cp in cps:
            cp.start()
        for cp in cps:
            cp.wait()

    anyspec = pl.BlockSpec(memory_space=pl.ANY)
    return pl.pallas_call(
        body,
        out_shape=out_shapes,
        in_specs=[anyspec] * nw,
        out_specs=[anyspec] * nw,
        scratch_shapes=[pltpu.SemaphoreType.DMA((nw, 3)), pltpu.SemaphoreType.DMA((nw, 3))],
        name=name,
    )(*parts)


def _swap_halves(halves, name):
    nw = len(halves)
    out_shapes = [jax.ShapeDtypeStruct(p.shape, p.dtype) for p in halves]

    def body(*refs):
        ins, outs = refs[:nw], refs[nw : 2 * nw]
        send_sems, recv_sems = refs[2 * nw :]
        x, y, c = _mesh_pos()
        cps = [pltpu.make_async_remote_copy(
            src_ref=ins[w], dst_ref=outs[w], send_sem=send_sems.at[w], recv_sem=recv_sems.at[w],
            device_id=(x, y, 1 - c), device_id_type=MESH_ID) for w in range(nw)]
        for cp in cps:
            cp.start()
        for cp in cps:
            cp.wait()

    anyspec = pl.BlockSpec(memory_space=pl.ANY)
    return pl.pallas_call(
        body,
        out_shape=out_shapes,
        in_specs=[anyspec] * nw,
        out_specs=[anyspec] * nw,
        scratch_shapes=[pltpu.SemaphoreType.DMA((nw,)), pltpu.SemaphoreType.DMA((nw,))],
        name=name,
    )(*halves)


def _pad_wdq(w):
    z = lambda n: jnp.zeros((w.shape[0], n), w.dtype)
    base = Q_LORA + KV_LORA
    return jnp.concatenate([w[:, :base], z(KPE_LANE), w[:, base:], z(HEAD_PAD - KPE_LANE - QK_ROPE)], axis=1)


def _unpad_wdq(g):
    base = Q_LORA + KV_LORA
    return jnp.concatenate([g[:, :base], g[:, base + KPE_LANE : base + KPE_LANE + QK_ROPE]], axis=1)


def _pad_wuq(w):
    w3 = w.reshape(Q_LORA, MLA_HEADS, QK_NOPE + QK_ROPE)
    w3 = jnp.pad(w3, ((0, 0), (0, 0), (0, HEAD_PAD - QK_NOPE - QK_ROPE)))
    return w3.reshape(Q_LORA, MLA_HEADS * HEAD_PAD)


def _unpad_wuq(g):
    g3 = g.reshape(Q_LORA, MLA_HEADS, HEAD_PAD)[:, :, : QK_NOPE + QK_ROPE]
    return g3.reshape(Q_LORA, MLA_HEADS * (QK_NOPE + QK_ROPE))


def _pad_wo(w):
    w3 = w.reshape(MLA_HEADS, V_HEAD, D_MODEL)
    w3 = jnp.pad(w3, ((0, 0), (HEAD_PAD - V_HEAD, 0), (0, 0)))
    return w3.reshape(MLA_HEADS * HEAD_PAD, D_MODEL)


def _unpad_wo(g):
    g3 = g.reshape(MLA_HEADS, HEAD_PAD, D_MODEL)[:, HEAD_PAD - V_HEAD :, :]
    return g3.reshape(MLA_HEADS * V_HEAD, D_MODEL)


def _local_step(x, mem, positions, target, wb, ws):
    t = x.shape[0]
    tables = _rope_tables(positions.reshape(t, 1), "rope_tables")
    saved = []
    for l in range(DEPTH):
        s = {"x0": x}
        h1 = _rms_fwd(x, ws["norm_mix_g"][l], f"l{l}_norm_mix")
        s["h1"] = h1
        if l % 2 == 0:
            e = l // 2
            z = _matmul(h1, wb["pc_w_in"], "nn", F32, f"l{l}_pc_in", layer=e)
            cat = _mix_fwd(z, ws["pool_w"][e], ws["pool_scale"][e], ws["conv_dw_w"][e], ws["conv_dw_b"][e],
                           ws["conv_ln_g"][e], ws["conv_ln_b"][e], f"l{l}_mix")
            x = _matmul(cat, wb["pc_w_out"], "nn", F32, f"l{l}_pc_out", layer=e, res=x)
            s.update(z=z, cat=cat)
        else:
            o = l // 2
            cp = _matmul(h1, wb["mla_wdq"], "nn", F32, f"l{l}_mla_dq", layer=o)
            qn, kvn, kpe = _mla_prep(cp, ws["mla_q_norm_g"][o], ws["mla_kv_norm_g"][o], tables, f"l{l}_mla_prep")
            q = _matmul(qn, wb["mla_wuq"], "nn", F32, f"l{l}_mla_uq", layer=o)
            qr = _rope_heads(q, tables, 1.0, MLA_SCALE * LOG2E, f"l{l}_mla_rope")
            kv = _matmul(kvn, wb["mla_w_ukv"], "nn", MXU_DT, f"l{l}_mla_ukv", layer=o)
            att, lse = _flash_fwd(qr, kv, kpe, f"l{l}_mla_attn")
            x = _matmul(att, wb["mla_wo"], "nn", F32, f"l{l}_mla_o", layer=o, res=x)
            s.update(cp=cp, qn=qn, kvn=kvn, kpe=kpe, qr=qr, kv=kv, att=att, lse=lse)
        s["x1"] = x
        h2 = _rms_fwd(x, ws["norm_xa_g"][l], f"l{l}_norm_xa")
        hm = _rms_fwd(mem, ws["norm_mem_g"][l], f"l{l}_norm_mem")
        q2 = _matmul(h2, wb["xa_wq"], "nn", MXU_DT, f"l{l}_xa_q", layer=l)
        kvm = _matmul(hm, wb["xa_wkv"], "nn", MXU_DT, f"l{l}_xa_kv", layer=l)
        o2 = _xa_fwd(q2, kvm, f"l{l}_xa_attn")
        x = _matmul(o2, wb["xa_wo"], "nn", F32, f"l{l}_xa_o", layer=l, res=x)
        s.update(h2=h2, hm=hm, q2=q2, kvm=kvm, o2=o2, x2=x)
        h3 = _rms_fwd(x, ws["norm_ffn_g"][l], f"l{l}_norm_ffn")
        up = _matmul(h3, wb["ffn_w_up"], "nn", F32, f"l{l}_ffn_up", layer=l)
        act = _ffn_fwd(up, ws["ffn_conv_w"][l], ws["ffn_conv_b"][l], f"l{l}_ffn_mid")
        x = _matmul(act, wb["ffn_w_down"], "nn", F32, f"l{l}_ffn_down", layer=l, res=x)
        s.update(h3=h3, up=up, act=act)
        saved.append(s)

    dx, dg_final, loss = _loss_head(x, target, ws["final_norm_g"], "loss_head")
    g = {k: [None] * DEPTH for k in ("norm_mix_g", "norm_xa_g", "norm_mem_g", "xa_wq", "xa_wkv", "xa_wo", "norm_ffn_g",
                                      "ffn_w_up", "ffn_conv_w", "ffn_conv_b", "ffn_w_down")}
    g.update({k: [None] * (DEPTH // 2) for k in ("pc_w_in", "pool_w", "pool_scale", "conv_dw_w", "conv_dw_b", "conv_ln_g",
                                                 "conv_ln_b", "pc_w_out", "mla_w_dq_dkv", "mla_q_norm_g", "mla_w_uq",
                                                 "mla_kv_norm_g", "mla_w_ukv", "mla_w_o")})
    for l in reversed(range(DEPTH)):
        s = saved[l]
        dact = _matmul(dx, wb["ffn_w_down"], "nt", F32, f"l{l}_b_ffn_dact", layer=l)
        g["ffn_w_down"][l] = _matmul(s["act"], dx, "tn", F32, f"l{l}_b_ffn_dwdown")
        dup, dcw, dcb = _ffn_bwd(s["up"], dact, ws["ffn_conv_w"][l], ws["ffn_conv_b"][l], f"l{l}_b_ffn_mid")
        g["ffn_conv_w"][l], g["ffn_conv_b"][l] = dcw, dcb[0]
        g["ffn_w_up"][l] = _matmul(s["h3"], dup, "tn", F32, f"l{l}_b_ffn_dwup")
        dh = _matmul(dup, wb["ffn_w_up"], "nt", F32, f"l{l}_b_ffn_dh", layer=l)
        dx, dg = _rms_bwd(dh, s["x2"], ws["norm_ffn_g"][l], dx, f"l{l}_b_norm_ffn")
        g["norm_ffn_g"][l] = dg[0]
        do2 = _matmul(dx, wb["xa_wo"], "nt", MXU_DT, f"l{l}_b_xa_do", layer=l)
        g["xa_wo"][l] = _matmul(s["o2"], dx, "tn", F32, f"l{l}_b_xa_dwo")
        dq2, dkvm = _xa_bwd(s["q2"], s["kvm"], do2, f"l{l}_b_xa_attn")
        g["xa_wq"][l] = _matmul(s["h2"], dq2, "tn", F32, f"l{l}_b_xa_dwq")
        dh = _matmul(dq2, wb["xa_wq"], "nt", F32, f"l{l}_b_xa_dh", layer=l)
        g["xa_wkv"][l] = _matmul(s["hm"], dkvm, "tn", F32, f"l{l}_b_xa_dwkv")
        dhm = _matmul(dkvm, wb["xa_wkv"], "nt", F32, f"l{l}_b_xa_dhm", layer=l)
        g["norm_mem_g"][l] = _rms_bwd_gain(dhm, mem, ws["norm_mem_g"][l], f"l{l}_b_norm_mem")[0]
        dx, dg = _rms_bwd(dh, s["x1"], ws["norm_xa_g"][l], dx, f"l{l}_b_norm_xa")
        g["norm_xa_g"][l] = dg[0]
        if l % 2 == 0:
            e = l // 2
            dcat = _matmul(dx, wb["pc_w_out"], "nt", F32, f"l{l}_b_pc_dcat", layer=e)
            g["pc_w_out"][e] = _matmul(s["cat"], dx, "tn", F32, f"l{l}_b_pc_dwout")
            dz, dpw, dps, dww, dwb, dlg, dlb = _mix_bwd(
                s["z"], dcat, ws["pool_w"][e], ws["pool_scale"][e], ws["conv_dw_w"][e], ws["conv_dw_b"][e],
                ws["conv_ln_g"][e], ws["conv_ln_b"][e], f"l{l}_b_mix")
            g["pool_w"][e], g["pool_scale"][e], g["conv_dw_w"][e] = dpw, dps[0], dww
            g["conv_dw_b"][e], g["conv_ln_g"][e], g["conv_ln_b"][e] = dwb[0], dlg[0], dlb[0]
            g["pc_w_in"][e] = _matmul(s["h1"], dz, "tn", F32, f"l{l}_b_pc_dwin")
            dh = _matmul(dz, wb["pc_w_in"], "nt", F32, f"l{l}_b_pc_dh", layer=e)
        else:
            o = l // 2
            do = _matmul(dx, wb["mla_wo"], "nt", MXU_DT, f"l{l}_b_mla_do", layer=o)
            g["mla_w_o"][o] = _unpad_wo(_matmul(s["att"], dx, "tn", F32, f"l{l}_b_mla_dwo"))
            dqr, dkv, dkpe = _flash_bwd(s["qr"], s["kv"], s["kpe"], s["att"], do, s["lse"], f"l{l}_b_mla_attn")
            dq = _rope_heads(dqr, tables, -1.0, MLA_SCALE, f"l{l}_b_mla_rope")
            g["mla_w_uq"][o] = _unpad_wuq(_matmul(s["qn"], dq, "tn", F32, f"l{l}_b_mla_dwuq"))
            dqn = _matmul(dq, wb["mla_wuq"], "nt", F32, f"l{l}_b_mla_dqn", layer=o)
            g["mla_w_ukv"][o] = _matmul(s["kvn"], dkv, "tn", F32, f"l{l}_b_mla_dwukv")
            dkvn = _matmul(dkv, wb["mla_w_ukv"], "nt", F32, f"l{l}_b_mla_dkvn", layer=o)
            dcp, dqg, dkvg = _mla_prep_bwd(s["cp"], dqn, dkvn, dkpe, ws["mla_q_norm_g"][o], ws["mla_kv_norm_g"][o],
                                           tables, f"l{l}_b_mla_prep")
            g["mla_q_norm_g"][o], g["mla_kv_norm_g"][o] = dqg[0], dkvg[0]
            g["mla_w_dq_dkv"][o] = _unpad_wdq(_matmul(s["h1"], dcp, "tn", F32, f"l{l}_b_mla_dwdq"))
            dh = _matmul(dcp, wb["mla_wdq"], "nt", F32, f"l{l}_b_mla_dh", layer=o)
        dx, dg = _rms_bwd(dh, s["x0"], ws["norm_mix_g"][l], dx, f"l{l}_b_norm_mix")
        g["norm_mix_g"][l] = dg[0]
    grads = {k: jnp.stack(v) for k, v in g.items()}
    grads["final_norm_g"] = dg_final[0]
    return loss, dx, grads


BIG = (
    ("xa_wq", "row"), ("xa_wkv", "col"), ("xa_wo", "row"), ("ffn_w_up", "col"), ("ffn_w_down", "row"),
    ("pc_w_in", "col"), ("pc_w_out", "row"), ("mla_w_dq_dkv", "row"), ("mla_w_uq", "col"), ("mla_w_ukv", "col"),
    ("mla_w_o", "row"),
)
SMALL_SHARDED = ("ffn_conv_w", "conv_dw_w", "mla_q_norm_g", "mla_kv_norm_g")
SMALL_REPLICATED = ("norm_mix_g", "norm_xa_g", "norm_mem_g", "norm_ffn_g", "ffn_conv_b", "pool_w", "pool_scale",
                    "conv_dw_b", "conv_ln_g", "conv_ln_b", "final_norm_g")
WEIGHTS = ("norm_mix_g", "norm_xa_g", "norm_mem_g", "xa_wq", "xa_wkv", "xa_wo", "norm_ffn_g", "ffn_w_up", "ffn_conv_w",
           "ffn_conv_b", "ffn_w_down", "pc_w_in", "pool_w", "pool_scale", "conv_dw_w", "conv_dw_b", "conv_ln_g",
           "conv_ln_b", "pc_w_out", "mla_w_dq_dkv", "mla_q_norm_g", "mla_w_uq", "mla_kv_norm_g", "mla_w_ukv", "mla_w_o",
           "final_norm_g")
PACK_ROW = SUBLANE * LANE


def _pack(arrays):
    flat = jnp.concatenate([a.reshape(-1) for a in arrays])
    n = flat.shape[0]
    pad = (-n) % PACK_ROW
    return jnp.pad(flat, (0, pad)).reshape(-1, LANE)


def _unpack(flat, shapes):
    out, off = [], 0
    for s in shapes:
        n = int(np.prod(s))
        out.append(flat[off : off + n].reshape(s))
        off += n
    return out


def kernel(x, mem, positions, norm_mix_g, norm_xa_g, norm_mem_g, xa_wq, xa_wkv, xa_wo, norm_ffn_g, ffn_w_up, ffn_conv_w, ffn_conv_b, ffn_w_down, pc_w_in, pool_w, pool_scale, conv_dw_w, conv_dw_b, conv_ln_g, conv_ln_b, pc_w_out, mla_w_dq_dkv, mla_q_norm_g, mla_w_uq, mla_kv_norm_g, mla_w_ukv, mla_w_o, final_norm_g, loss_target, m_norm_mix_g, m_norm_xa_g, m_norm_mem_g, m_xa_wq, m_xa_wkv, m_xa_wo, m_norm_ffn_g, m_ffn_w_up, m_ffn_conv_w, m_ffn_conv_b, m_ffn_w_down, m_pc_w_in, m_pool_w, m_pool_scale, m_conv_dw_w, m_conv_dw_b, m_conv_ln_g, m_conv_ln_b, m_pc_w_out, m_mla_w_dq_dkv, m_mla_q_norm_g, m_mla_w_uq, m_mla_kv_norm_g, m_mla_w_ukv, m_mla_w_o, m_final_norm_g, v_norm_mix_g, v_norm_xa_g, v_norm_mem_g, v_xa_wq, v_xa_wkv, v_xa_wo, v_norm_ffn_g, v_ffn_w_up, v_ffn_conv_w, v_ffn_conv_b, v_ffn_w_down, v_pc_w_in, v_pool_w, v_pool_scale, v_conv_dw_w, v_conv_dw_b, v_conv_ln_g, v_conv_ln_b, v_pc_w_out, v_mla_w_dq_dkv, v_mla_q_norm_g, v_mla_w_uq, v_mla_kv_norm_g, v_mla_w_ukv, v_mla_w_o, v_final_norm_g):
    args = dict(locals())
    w = {n: args[n] for n in WEIGHTS}
    m = {n: args["m_" + n] for n in WEIGHTS}
    v = {n: args["v_" + n] for n in WEIGHTS}
    cx, cy, cc = lax.axis_index("x"), lax.axis_index("y"), lax.axis_index("c")
    chip = 2 * cx + cy

    full = _all_gather_weights([w[n].astype(MXU_DT) for n, _ in BIG], [k for _, k in BIG], "gather_weights")
    full = dict(zip([n for n, _ in BIG], full))
    small_shapes = [w[n].shape for n in SMALL_SHARDED]
    gathered = _all_gather_rows(_pack([w[n] for n in SMALL_SHARDED]), "gather_small")
    gathered = gathered.reshape(8, -1)
    ws = {n: w[n] for n in SMALL_REPLICATED}
    pieces = [_unpack(gathered[2 * k], small_shapes) for k in range(4)]
    for i, n in enumerate(SMALL_SHARDED):
        ws[n] = jnp.concatenate([pieces[k][i] for k in range(4)], axis=-1)
    wb = {n: full[n] for n in ("xa_wq", "xa_wkv", "xa_wo", "ffn_w_up", "ffn_w_down", "pc_w_in", "pc_w_out", "mla_w_ukv")}
    wb["mla_wdq"] = jnp.stack([_pad_wdq(full["mla_w_dq_dkv"][o]) for o in range(DEPTH // 2)])
    wb["mla_wuq"] = jnp.stack([_pad_wuq(full["mla_w_uq"][o]) for o in range(DEPTH // 2)])
    wb["mla_wo"] = jnp.stack([_pad_wo(full["mla_w_o"][o]) for o in range(DEPTH // 2)])

    loss, grad_x, grads = _local_step(x[0], mem[0], positions[0], loss_target[0], wb, ws)
    loss = lax.psum(loss[0, 0], ("x", "y", "c"))

    kinds = [k for _, k in BIG]
    big = [grads[n] for n, _ in BIG]
    c_idx = cc.reshape(1).astype(jnp.int32)
    chip_idx = chip.reshape(1).astype(jnp.int32)
    theirs = _exchange_halves(big, "reduce_pair")
    pair = [_add_half(gr, th, c_idx, f"reduce_pair_add_{n}") for gr, th, (n, _) in zip(big, theirs, BIG)]
    slots = _scatter_to_chips(pair, kinds, "reduce_chips")
    halves = [_sum_chips(sl, pr, chip_idx, kind, f"reduce_chips_add_{n}")
              for sl, pr, (n, kind) in zip(slots, pair, BIG)]
    others = _swap_halves(halves, "reduce_join")
    gsum, delta, new_m, new_v = {}, {}, {}, {}
    for mine, other, (n, _) in zip(halves, others, BIG):
        gsum[n], delta[n], new_m[n], new_v[n] = _adamw_halves(w[n], mine, other, m[n], v[n], c_idx, f"adamw_{n}")

    small_names = SMALL_REPLICATED + SMALL_SHARDED
    small_grad_shapes = [grads[n].shape for n in small_names]
    packed = _pack([grads[n] for n in small_names])
    rows = packed.shape[0]
    allparts = _all_gather_rows(packed, "gather_small_grads").reshape(8, rows, LANE)
    total = _sum_leading(allparts, "sum_small_grads").reshape(-1)
    for n, gfull in zip(small_names, _unpack(total, small_grad_shapes)):
        if n in SMALL_SHARDED:
            width = w[n].shape[-1]
            gfull = lax.dynamic_slice_in_dim(gfull, chip * width, width, axis=gfull.ndim - 1)
        gsum[n] = gfull

    for n in SMALL_REPLICATED + SMALL_SHARDED:
        delta[n], new_m[n], new_v[n] = _adamw(w[n], gsum[n], m[n], v[n], f"adamw_{n}")
    return (loss, grad_x[None], *[gsum[n] for n in WEIGHTS], *[delta[n] for n in WEIGHTS],
            *[new_m[n] for n in WEIGHTS], *[new_v[n] for n in WEIGHTS])
```

```python
import functools
import math

import numpy as np
import jax
import jax.numpy as jnp
from jax import lax
from jax.experimental import pallas as pl
from jax.experimental.pallas import tpu as pltpu

F32 = jnp.float32
MXU_DT = jnp.bfloat16
XFER_DT = jnp.bfloat16

D_MODEL = 1024
DEPTH = 4
MEM_LEN = 256
XA_HEADS = 4
XA_HEAD_DIM = 256
POOL_W = 512
POOL_WINDOWS = (2, 4, 8, 16)
POOL_GROUP = 128
CONV_W = 512
CONV_K = 31
MLA_HEADS = 16
QK_NOPE = 64
QK_ROPE = 32
V_HEAD = 64
Q_LORA = 384
KV_LORA = 256
ROPE_THETA = 10000.0
MLA_SCALE = 1.0 / math.sqrt(QK_NOPE + QK_ROPE)
LOG2E = math.log2(math.e)
D_FF = 2816
FFN_CONV_K = 3
EPS = 1e-6
NEG = -1e30
ADAM_LR = 0.001
ADAM_B1 = 0.9
ADAM_B2 = 0.999
ADAM_EPS = 1e-08
ADAM_WD = 0.01
ADAM_STEP = 10

HEAD_PAD = 128
C_PAD = 768
KPE_LANE = 64

VMEM_LIMIT = 52 * 1024 * 1024
BLOCK_BYTES = 6 * 1024 * 1024
LANE = 128
SUBLANE = 8

TM = 1024
TN = 1408
TK = 2048
TT = 512
TW = 256
TWF = 128
FFN_CHUNK = 256
TA = 1024
TAQ = 1024
MIX_HALO = 32
FFN_HALO = 8

NN = (((1,), (0,)), ((), ()))
NT = (((1,), (1,)), ((), ()))
TN_DIMS = (((0,), (0,)), ((), ()))
MESH_ID = pl.DeviceIdType.MESH


def _cparams(*sem):
    return pltpu.CompilerParams(dimension_semantics=sem, vmem_limit_bytes=VMEM_LIMIT)


def _tile(n, pref, limit=None):
    cap = pref if limit is None else min(pref, limit)
    if n <= cap:
        return n
    t = (cap // LANE) * LANE
    while t >= LANE:
        if n % t == 0:
            return t
        t -= LANE
    return n


def _rows(t, pref):
    return t if t <= pref else pref


def _sigmoid(x):
    return 1.0 / (1.0 + jnp.exp(-x))


def _matmul(a, b, mode, out_dtype, name, layer=None, res=None, stack=None):
    if layer is None:
        b2 = b.shape
    else:
        b2 = b.shape[1:]
    if mode == "tn":
        k, m = a.shape
        k2, n = b2
    elif mode == "nn":
        m, k = a.shape
        k2, n = b2
    else:
        m, k = a.shape
        n, k2 = b2
    assert k == k2, (a.shape, b.shape, mode)
    isz_a = jnp.dtype(a.dtype).itemsize
    isz_b = jnp.dtype(b.dtype).itemsize
    if mode == "tn":
        tk = _tile(k, TK * 2 // max(isz_a, isz_b))
        tm = _tile(m, TN, BLOCK_BYTES // (tk * isz_a))
        tn = _tile(n, TN, BLOCK_BYTES // (tk * isz_b))
    else:
        tk = k
        tn = _tile(n, TN, BLOCK_BYTES // (tk * isz_b))
        tm = _tile(m, TM, min(BLOCK_BYTES // (tk * isz_a), BLOCK_BYTES // (tn * 4)))
    nk = k // tk
    grid = (m // tm, n // tn, nk)
    if mode == "nn":
        a_spec = pl.BlockSpec((tm, tk), lambda i, j, kk: (i, kk))
        b_blk, b_map, dn = (tk, tn), (lambda i, j, kk: (kk, j)), NN
    elif mode == "nt":
        a_spec = pl.BlockSpec((tm, tk), lambda i, j, kk: (i, kk))
        b_blk, b_map, dn = (tn, tk), (lambda i, j, kk: (j, kk)), NT
    else:
        a_spec = pl.BlockSpec((tk, tm), lambda i, j, kk: (kk, i))
        b_blk, b_map, dn = (tk, tn), (lambda i, j, kk: (kk, j)), TN_DIMS
    if layer is None:
        b_spec = pl.BlockSpec(b_blk, b_map)
    else:
        b_spec = pl.BlockSpec((None,) + b_blk, lambda i, j, kk: (layer,) + b_map(i, j, kk))
    in_specs = [a_spec, b_spec]
    args = [a, b]
    if res is not None:
        in_specs.append(pl.BlockSpec((tm, tn), lambda i, j, kk: (i, j)))
        args.append(res)
    has_res = res is not None
    aliases = {}
    if stack is None:
        o_spec = pl.BlockSpec((tm, tn), lambda i, j, kk: (i, j))
        out_shape = jax.ShapeDtypeStruct((m, n), out_dtype)
    else:
        buf, slab, nslab = stack
        o_spec = pl.BlockSpec((None, tm, tn), lambda i, j, kk: (slab, i, j))
        out_shape = jax.ShapeDtypeStruct((nslab, m, n), out_dtype)
        if buf is not None:
            in_specs.append(pl.BlockSpec(memory_space=pl.ANY))
            args.append(buf)
            aliases = {len(args) - 1: 0}
    n_in = len(args)

    def body(*refs):
        a_ref, b_ref = refs[0], refs[1]
        r_ref = refs[2] if has_res else None
        o_ref = refs[n_in]
        p = lax.dot_general(a_ref[...].astype(MXU_DT), b_ref[...].astype(MXU_DT), dn, preferred_element_type=F32)
        if nk == 1:
            if has_res:
                p = r_ref[...] + p
            o_ref[...] = p.astype(o_ref.dtype)
        else:
            acc_ref = refs[-1]
            kk = pl.program_id(2)

            @pl.when(kk == 0)
            def _():
                acc_ref[...] = jnp.zeros_like(acc_ref)

            acc_ref[...] += p

            @pl.when(kk == nk - 1)
            def _():
                r = acc_ref[...]
                if has_res:
                    r = r_ref[...] + r
                o_ref[...] = r.astype(o_ref.dtype)

    scratch = [pltpu.VMEM((tm, tn), F32)] if nk > 1 else []
    return pl.pallas_call(
        body,
        grid=grid,
        in_specs=in_specs,
        out_specs=o_spec,
        out_shape=out_shape,
        scratch_shapes=scratch,
        input_output_aliases=aliases,
        name=name,
        compiler_params=_cparams("parallel", "parallel", "arbitrary"),
    )(*args)


def _rms_fwd(x, g, name):
    t, d = x.shape
    tt = _rows(t, TT)

    def body(x_ref, g_ref, o_ref):
        xf = x_ref[...]
        r = lax.rsqrt(jnp.mean(xf * xf, axis=-1, keepdims=True) + EPS)
        o_ref[...] = ((xf * r) * g_ref[...]).astype(o_ref.dtype)

    return pl.pallas_call(
        body,
        grid=(t // tt,),
        in_specs=[pl.BlockSpec((tt, d), lambda i: (i, 0)), pl.BlockSpec((1, d), lambda i: (0, 0))],
        out_specs=pl.BlockSpec((tt, d), lambda i: (i, 0)),
        out_shape=jax.ShapeDtypeStruct((t, d), MXU_DT),
        name=name,
        compiler_params=_cparams("parallel"),
    )(x, g.reshape(1, d))


def _rms_bwd(dh, x, g, dx_in, name):
    t, d = x.shape
    tt = _rows(t, TT)

    def body(dh_ref, x_ref, g_ref, dxi_ref, dx_ref, dg_ref):
        @pl.when(pl.program_id(0) == 0)
        def _():
            dg_ref[...] = jnp.zeros_like(dg_ref)

        xf = x_ref[...]
        dh_v = dh_ref[...]
        r = lax.rsqrt(jnp.mean(xf * xf, axis=-1, keepdims=True) + EPS)
        xh = xf * r
        gy = dh_v * g_ref[...]
        dx = r * (gy - xh * jnp.mean(gy * xh, axis=-1, keepdims=True))
        dx_ref[...] = dxi_ref[...] + dx
        dg_ref[...] += jnp.sum(dh_v * xh, axis=0, keepdims=True)

    row = pl.BlockSpec((tt, d), lambda i: (i, 0))
    vec = pl.BlockSpec((1, d), lambda i: (0, 0))
    return pl.pallas_call(
        body,
        grid=(t // tt,),
        in_specs=[row, row, vec, row],
        out_specs=[row, vec],
        out_shape=[jax.ShapeDtypeStruct((t, d), F32), jax.ShapeDtypeStruct((1, d), F32)],
        name=name,
        compiler_params=_cparams("arbitrary"),
    )(dh, x, g.reshape(1, d), dx_in)


def _rms_bwd_gain(dh, x, g, name):
    t, d = x.shape
    tt = _rows(t, TT)

    def body(dh_ref, x_ref, dg_ref):
        @pl.when(pl.program_id(0) == 0)
        def _():
            dg_ref[...] = jnp.zeros_like(dg_ref)

        xf = x_ref[...]
        r = lax.rsqrt(jnp.mean(xf * xf, axis=-1, keepdims=True) + EPS)
        dg_ref[...] += jnp.sum(dh_ref[...] * (xf * r), axis=0, keepdims=True)

    row = pl.BlockSpec((tt, d), lambda i: (i, 0))
    vec = pl.BlockSpec((1, d), lambda i: (0, 0))
    return pl.pallas_call(
        body,
        grid=(t // tt,),
        in_specs=[row, row],
        out_specs=vec,
        out_shape=jax.ShapeDtypeStruct((1, d), F32),
        name=name,
        compiler_params=_cparams("arbitrary"),
    )(dh, x)


def _loss_head(x, target, g, name):
    t, d = x.shape
    tt = _rows(t, TT)

    def body(x_ref, t_ref, g_ref, dx_ref, dg_ref, loss_ref):
        @pl.when(pl.program_id(0) == 0)
        def _():
            dg_ref[...] = jnp.zeros_like(dg_ref)
            loss_ref[...] = jnp.zeros_like(loss_ref)

        xf = x_ref[...]
        gv = g_ref[...]
        r = lax.rsqrt(jnp.mean(xf * xf, axis=-1, keepdims=True) + EPS)
        xh = xf * r
        err = xh * gv - t_ref[...]
        e2 = jnp.sum(err * err, axis=-1, keepdims=True)
        loss_ref[...] += (0.5 / d) * jnp.sum(e2, axis=0, keepdims=True)
        dy = err * (1.0 / d)
        gy = dy * gv
        dx_ref[...] = r * (gy - xh * jnp.mean(gy * xh, axis=-1, keepdims=True))
        dg_ref[...] += jnp.sum(dy * xh, axis=0, keepdims=True)

    row = pl.BlockSpec((tt, d), lambda i: (i, 0))
    vec = pl.BlockSpec((1, d), lambda i: (0, 0))
    return pl.pallas_call(
        body,
        grid=(t // tt,),
        in_specs=[row, row, vec],
        out_specs=[row, vec, pl.BlockSpec((1, 1), lambda i: (0, 0))],
        out_shape=[
            jax.ShapeDtypeStruct((t, d), F32),
            jax.ShapeDtypeStruct((1, d), F32),
            jax.ShapeDtypeStruct((1, 1), F32),
        ],
        name=name,
        compiler_params=_cparams("arbitrary"),
    )(x, target, g.reshape(1, d))


def _prev_halo(tt, hp, width):
    return pl.BlockSpec((hp, width), lambda i: (jnp.maximum(i * (tt // hp) - 1, 0), 0))


def _next_halo(tt, hp, width, t):
    return pl.BlockSpec((hp, width), lambda i: (jnp.minimum((i + 1) * (tt // hp), t // hp - 1), 0))


def _ffn_chunks():
    return [(c0, FFN_CHUNK) for c0 in range(0, D_FF, FFN_CHUNK)]


def _ffn_fwd(up, conv_w, conv_b, name):
    t = up.shape[0]
    tt = _rows(t, TWF)
    hp = FFN_HALO

    def body(up_ref, gp_ref, w_ref, b_ref, act_ref, ext_ref):
        first = pl.program_id(0) == 0
        for c0, cw in _ffn_chunks():
            ga = pl.ds(D_FF + c0, cw)
            ext_ref[0:hp, :] = jnp.where(first, 0.0, gp_ref[:, ga])
            ext_ref[hp : hp + tt, :] = up_ref[:, ga]
            gc = b_ref[:, pl.ds(c0, cw)]
            for j in range(FFN_CONV_K):
                off = hp - (FFN_CONV_K - 1) + j
                gc = gc + w_ref[j : j + 1, pl.ds(c0, cw)] * ext_ref[off : off + tt, :]
            a = up_ref[:, pl.ds(c0, cw)]
            act_ref[:, pl.ds(c0, cw)] = (gc * _sigmoid(gc) * a).astype(act_ref.dtype)

    return pl.pallas_call(
        body,
        grid=(t // tt,),
        in_specs=[
            pl.BlockSpec((tt, 2 * D_FF), lambda i: (i, 0)),
            _prev_halo(tt, hp, 2 * D_FF),
            pl.BlockSpec((FFN_CONV_K, D_FF), lambda i: (0, 0)),
            pl.BlockSpec((1, D_FF), lambda i: (0, 0)),
        ],
        out_specs=pl.BlockSpec((tt, D_FF), lambda i: (i, 0)),
        out_shape=jax.ShapeDtypeStruct((t, D_FF), MXU_DT),
        scratch_shapes=[pltpu.VMEM((tt + hp, FFN_CHUNK), F32)],
        name=name,
        compiler_params=_cparams("parallel"),
    )(up, up, conv_w, conv_b.reshape(1, D_FF))


def _ffn_bwd(up, dact, conv_w, conv_b, name):
    t = up.shape[0]
    tt = _rows(t, TWF)
    hp = FFN_HALO
    nt = t // tt
    kk = FFN_CONV_K

    def body(up_ref, upp_ref, upn_ref, da_ref, dan_ref, w_ref, b_ref, dup_ref, dw_ref, db_ref, ext_ref, dgc_ref):
        i = pl.program_id(0)
        first = i == 0
        last = i == nt - 1

        @pl.when(first)
        def _():
            dw_ref[...] = jnp.zeros_like(dw_ref)
            db_ref[...] = jnp.zeros_like(db_ref)

        for c0, cw in _ffn_chunks():
            ca = pl.ds(c0, cw)
            ga = pl.ds(D_FF + c0, cw)
            ext_ref[0:hp, :] = jnp.where(first, 0.0, upp_ref[:, ga])
            ext_ref[hp : hp + tt, :] = up_ref[:, ga]
            ext_ref[hp + tt : hp + tt + hp, :] = upn_ref[:, ga]
            gc = b_ref[:, ca]
            for j in range(kk):
                off = hp - (kk - 1) + j
                gc = gc + w_ref[j : j + 1, ca] * ext_ref[off : off + tt + hp, :]
            sg = _sigmoid(gc)
            silu = gc * sg
            dsilu = sg * (1.0 + gc * (1.0 - sg))
            a_all = jnp.concatenate([up_ref[:, ca], upn_ref[:, ca]], axis=0)
            dact_all = jnp.concatenate([da_ref[:, ca], jnp.where(last, 0.0, dan_ref[:, ca])], axis=0)
            dgc = dact_all * a_all * dsilu
            dgc_ref[...] = dgc
            dup_ref[:, ca] = (dact_all[0:tt] * silu[0:tt]).astype(dup_ref.dtype)
            dg = jnp.zeros((tt, cw), F32)
            for j in range(kk):
                dg = dg + w_ref[j : j + 1, ca] * dgc_ref[kk - 1 - j : kk - 1 - j + tt, :]
            dup_ref[:, ga] = dg.astype(dup_ref.dtype)
            dgc_t = dgc[0:tt]
            db_ref[:, ca] += jnp.sum(dgc_t, axis=0, keepdims=True)
            for j in range(kk):
                off = hp - (kk - 1) + j
                dw_ref[j : j + 1, ca] += jnp.sum(dgc_t * ext_ref[off : off + tt, :], axis=0, keepdims=True)

    return pl.pallas_call(
        body,
        grid=(nt,),
        in_specs=[
            pl.BlockSpec((tt, 2 * D_FF), lambda i: (i, 0)),
            _prev_halo(tt, hp, 2 * D_FF),
            _next_halo(tt, hp, 2 * D_FF, t),
            pl.BlockSpec((tt, D_FF), lambda i: (i, 0)),
            _next_halo(tt, hp, D_FF, t),
            pl.BlockSpec((kk, D_FF), lambda i: (0, 0)),
            pl.BlockSpec((1, D_FF), lambda i: (0, 0)),
        ],
        out_specs=[
            pl.BlockSpec((tt, 2 * D_FF), lambda i: (i, 0)),
            pl.BlockSpec((kk, D_FF), lambda i: (0, 0)),
            pl.BlockSpec((1, D_FF), lambda i: (0, 0)),
        ],
        out_shape=[
            jax.ShapeDtypeStruct((t, 2 * D_FF), MXU_DT),
            jax.ShapeDtypeStruct((kk, D_FF), F32),
            jax.ShapeDtypeStruct((1, D_FF), F32),
        ],
        scratch_shapes=[pltpu.VMEM((tt + 2 * hp, FFN_CHUNK), F32), pltpu.VMEM((tt + hp, FFN_CHUNK), F32)],
        name=name,
        compiler_params=_cparams("arbitrary"),
    )(up, up, up, dact, dact, conv_w, conv_b.reshape(1, D_FF))


def _layernorm_silu(cv, ln_g, ln_b):
    mu = jnp.mean(cv, axis=-1, keepdims=True)
    xc = cv - mu
    rstd = lax.rsqrt(jnp.mean(xc * xc, axis=-1, keepdims=True) + EPS)
    xh = xc * rstd
    a = xh * ln_g + ln_b
    return xh, rstd, a


def _shifted_copies(ref, n):
    for b in range(1, SUBLANE):
        ref[b, 0 : n - SUBLANE, :] = ref[0, b : b + n - SUBLANE, :]


def _tap(ref, offset, rows, cols):
    b = offset % SUBLANE
    return ref[b, offset - b : offset - b + rows, cols]


def _mix_fwd(z, pool_w, pool_scale, dw_w, dw_b, ln_g, ln_b, name):
    t = z.shape[0]
    tt = _rows(t, TW)
    hp = MIX_HALO
    zw = POOL_W + 2 * CONV_W

    def body(z_ref, zp_ref, pw_ref, ps_ref, w_ref, b_ref, lg_ref, lb_ref, cat_ref, eu_ref, egl_ref, cv_ref):
        i = pl.program_id(0)
        first = i == 0
        eu_ref[0:hp, :] = jnp.where(first, 0.0, zp_ref[:, 0:POOL_W])
        eu_ref[hp : hp + tt, :] = z_ref[:, 0:POOL_W]
        glp = zp_ref[:, POOL_W : POOL_W + CONV_W] * _sigmoid(zp_ref[:, POOL_W + CONV_W : zw])
        egl_ref[0, 0:hp, :] = jnp.where(first, 0.0, glp)
        egl_ref[0, hp : hp + tt, :] = z_ref[:, POOL_W : POOL_W + CONV_W] * _sigmoid(z_ref[:, POOL_W + CONV_W : zw])
        _shifted_copies(egl_ref, tt + hp)
        row = i * tt + lax.broadcasted_iota(jnp.int32, (tt, 1), 0)
        for gi, w in enumerate(POOL_WINDOWS):
            cols = pl.ds(gi * POOL_GROUP, POOL_GROUP)
            u = eu_ref[hp : hp + tt, cols]
            acc = u
            for k in range(1, w):
                acc = acc + eu_ref[hp - k : hp - k + tt, cols]
            cnt = jnp.minimum(row + 1, w).astype(F32)
            pooled = acc / cnt - u
            y = jnp.dot(pooled.astype(MXU_DT), pw_ref[gi].astype(MXU_DT), preferred_element_type=F32)
            cat_ref[:, cols] = (y * ps_ref[:, cols]).astype(cat_ref.dtype)
        for c0 in range(0, CONV_W, LANE):
            cs = pl.ds(c0, LANE)
            acc = jnp.broadcast_to(b_ref[:, cs], (tt, LANE))
            for j in range(CONV_K):
                acc = acc + w_ref[j : j + 1, cs] * _tap(egl_ref, hp - (CONV_K - 1) + j, tt, cs)
            cv_ref[:, cs] = acc
        _, _, a = _layernorm_silu(cv_ref[...], lg_ref[...], lb_ref[...])
        cat_ref[:, POOL_W : POOL_W + CONV_W] = (a * _sigmoid(a)).astype(cat_ref.dtype)

    vec = pl.BlockSpec((1, CONV_W), lambda i: (0, 0))
    return pl.pallas_call(
        body,
        grid=(t // tt,),
        in_specs=[
            pl.BlockSpec((tt, zw), lambda i: (i, 0)),
            _prev_halo(tt, hp, zw),
            pl.BlockSpec((len(POOL_WINDOWS), POOL_GROUP, POOL_GROUP), lambda i: (0, 0, 0)),
            vec,
            pl.BlockSpec((CONV_K, CONV_W), lambda i: (0, 0)),
            vec,
            vec,
            vec,
        ],
        out_specs=pl.BlockSpec((tt, POOL_W + CONV_W), lambda i: (i, 0)),
        out_shape=jax.ShapeDtypeStruct((t, POOL_W + CONV_W), MXU_DT),
        scratch_shapes=[pltpu.VMEM((tt + hp, POOL_W), F32), pltpu.VMEM((SUBLANE, tt + hp, CONV_W), F32),
                        pltpu.VMEM((tt, CONV_W), F32)],
        name=name,
        compiler_params=_cparams("parallel"),
    )(z, z, pool_w, pool_scale.reshape(1, POOL_W), dw_w, dw_b.reshape(1, CONV_W), ln_g.reshape(1, CONV_W), ln_b.reshape(1, CONV_W))


def _mix_bwd(z, dcat, pool_w, pool_scale, dw_w, dw_b, ln_g, ln_b, name):
    t = z.shape[0]
    tt = _rows(t, TW)
    hp = MIX_HALO
    nt = t // tt
    zw = POOL_W + 2 * CONV_W
    ng = len(POOL_WINDOWS)

    def body(z_ref, zp_ref, zn_ref, dc_ref, dcn_ref, pw_ref, ps_ref, w_ref, b_ref, lg_ref, lb_ref,
             dz_ref, dpw_ref, dps_ref, dww_ref, dwb_ref, dlg_ref, dlb_ref, eu_ref, ee_ref, egl_ref, edcv_ref, cv_ref):
        i = pl.program_id(0)
        first = i == 0
        last = i == nt - 1

        @pl.when(first)
        def _():
            for r in (dpw_ref, dps_ref, dww_ref, dwb_ref, dlg_ref, dlb_ref):
                r[...] = jnp.zeros_like(r)

        eu_ref[0:hp, :] = jnp.where(first, 0.0, zp_ref[:, 0:POOL_W])
        eu_ref[hp : hp + tt, :] = z_ref[:, 0:POOL_W]
        row = i * tt + lax.broadcasted_iota(jnp.int32, (tt, 1), 0)
        row_ext = i * tt + lax.broadcasted_iota(jnp.int32, (tt + hp, 1), 0)
        for gi, w in enumerate(POOL_WINDOWS):
            cols = pl.ds(gi * POOL_GROUP, POOL_GROUP)
            u = eu_ref[hp : hp + tt, cols]
            acc = u
            for k in range(1, w):
                acc = acc + eu_ref[hp - k : hp - k + tt, cols]
            pooled = (acc / jnp.minimum(row + 1, w).astype(F32) - u).astype(MXU_DT)
            pw = pw_ref[gi].astype(MXU_DT)
            dya = dc_ref[:, cols]
            y = jnp.dot(pooled, pw, preferred_element_type=F32)
            dps_ref[:, cols] += jnp.sum(dya * y, axis=0, keepdims=True)
            scale = ps_ref[:, cols]
            dy_all = jnp.concatenate([dya, jnp.where(last, 0.0, dcn_ref[:, cols])], axis=0) * scale
            dy_all = dy_all.astype(MXU_DT)
            dpw_ref[gi] += lax.dot_general(pooled, dy_all[0:tt], TN_DIMS, preferred_element_type=F32)
            dpooled = lax.dot_general(dy_all, pw, NT, preferred_element_type=F32)
            ee_ref[:, cols] = dpooled / jnp.minimum(row_ext + 1, w).astype(F32)
            du = -dpooled[0:tt]
            for k in range(w):
                du = du + ee_ref[k : k + tt, cols]
            dz_ref[:, cols] = du.astype(dz_ref.dtype)

        ca = slice(POOL_W, POOL_W + CONV_W)
        cb = slice(POOL_W + CONV_W, zw)
        egl_ref[0, 0:hp, :] = jnp.where(first, 0.0, zp_ref[:, ca] * _sigmoid(zp_ref[:, cb]))
        egl_ref[0, hp : hp + tt, :] = z_ref[:, ca] * _sigmoid(z_ref[:, cb])
        egl_ref[0, hp + tt : hp + tt + hp, :] = zn_ref[:, ca] * _sigmoid(zn_ref[:, cb])
        _shifted_copies(egl_ref, tt + 2 * hp)
        for c0 in range(0, CONV_W, LANE):
            cs = pl.ds(c0, LANE)
            acc = jnp.broadcast_to(b_ref[:, cs], (tt + hp, LANE))
            for j in range(CONV_K):
                acc = acc + w_ref[j : j + 1, cs] * _tap(egl_ref, hp - (CONV_K - 1) + j, tt + hp, cs)
            cv_ref[:, cs] = acc
        lg = lg_ref[...]
        xh, rstd, a = _layernorm_silu(cv_ref[...], lg, lb_ref[...])
        sa = _sigmoid(a)
        dyb = jnp.concatenate([dc_ref[:, ca], jnp.where(last, 0.0, dcn_ref[:, ca])], axis=0)
        da = dyb * (sa * (1.0 + a * (1.0 - sa)))
        dlg_ref[...] += jnp.sum(da[0:tt] * xh[0:tt], axis=0, keepdims=True)
        dlb_ref[...] += jnp.sum(da[0:tt], axis=0, keepdims=True)
        dxh = da * lg
        dcv = rstd * (dxh - jnp.mean(dxh, axis=-1, keepdims=True) - xh * jnp.mean(dxh * xh, axis=-1, keepdims=True))
        edcv_ref[0] = dcv
        _shifted_copies(edcv_ref, tt + hp)
        dwb_ref[...] += jnp.sum(dcv[0:tt], axis=0, keepdims=True)
        for c0 in range(0, CONV_W, LANE):
            cs = pl.ds(c0, LANE)
            dcv_t = edcv_ref[0, 0:tt, cs]
            dgl = jnp.zeros((tt, LANE), F32)
            for j in range(CONV_K):
                tap = _tap(egl_ref, hp - (CONV_K - 1) + j, tt, cs)
                dww_ref[j : j + 1, cs] += jnp.sum(dcv_t * tap, axis=0, keepdims=True)
                dgl = dgl + w_ref[j : j + 1, cs] * _tap(edcv_ref, CONV_K - 1 - j, tt, cs)
            ga = z_ref[:, pl.ds(POOL_W + c0, LANE)]
            sgb = _sigmoid(z_ref[:, pl.ds(POOL_W + CONV_W + c0, LANE)])
            dz_ref[:, pl.ds(POOL_W + c0, LANE)] = (dgl * sgb).astype(dz_ref.dtype)
            dz_ref[:, pl.ds(POOL_W + CONV_W + c0, LANE)] = (dgl * ga * sgb * (1.0 - sgb)).astype(dz_ref.dtype)

    vec = pl.BlockSpec((1, CONV_W), lambda i: (0, 0))
    pw_spec = pl.BlockSpec((ng, POOL_GROUP, POOL_GROUP), lambda i: (0, 0, 0))
    w_spec = pl.BlockSpec((CONV_K, CONV_W), lambda i: (0, 0))
    return pl.pallas_call(
        body,
        grid=(nt,),
        in_specs=[
            pl.BlockSpec((tt, zw), lambda i: (i, 0)),
            _prev_halo(tt, hp, zw),
            _next_halo(tt, hp, zw, t),
            pl.BlockSpec((tt, POOL_W + CONV_W), lambda i: (i, 0)),
            _next_halo(tt, hp, POOL_W + CONV_W, t),
            pw_spec, vec, w_spec, vec, vec, vec,
        ],
        out_specs=[pl.BlockSpec((tt, zw), lambda i: (i, 0)), pw_spec, vec, w_spec, vec, vec, vec],
        out_shape=[
            jax.ShapeDtypeStruct((t, zw), MXU_DT),
            jax.ShapeDtypeStruct((ng, POOL_GROUP, POOL_GROUP), F32),
            jax.ShapeDtypeStruct((1, POOL_W), F32),
            jax.ShapeDtypeStruct((CONV_K, CONV_W), F32),
            jax.ShapeDtypeStruct((1, CONV_W), F32),
            jax.ShapeDtypeStruct((1, CONV_W), F32),
            jax.ShapeDtypeStruct((1, CONV_W), F32),
        ],
        scratch_shapes=[
            pltpu.VMEM((tt + hp, POOL_W), F32),
            pltpu.VMEM((tt + hp, POOL_W), F32),
            pltpu.VMEM((SUBLANE, tt + 2 * hp, CONV_W), F32),
            pltpu.VMEM((SUBLANE, tt + hp, CONV_W), F32),
            pltpu.VMEM((tt + hp, CONV_W), F32),
        ],
        name=name,
        compiler_params=_cparams("arbitrary"),
    )(z, z, z, dcat, dcat, pool_w, pool_scale.reshape(1, POOL_W), dw_w, dw_b.reshape(1, CONV_W),
      ln_g.reshape(1, CONV_W), ln_b.reshape(1, CONV_W))


def _xa_fwd(q, kvm, name):
    t = q.shape[0]
    tt = _rows(t, TT)
    scale = XA_HEAD_DIM ** -0.5

    def body(q_ref, kv_ref, o_ref):
        for h in range(XA_HEADS):
            cs = pl.ds(h * XA_HEAD_DIM, XA_HEAD_DIM)
            vs = pl.ds(D_MODEL + h * XA_HEAD_DIM, XA_HEAD_DIM)
            s = lax.dot_general(q_ref[:, cs], kv_ref[:, cs], NT, preferred_element_type=F32) * scale
            p = jnp.exp(s - jnp.max(s, axis=-1, keepdims=True))
            p = p / jnp.sum(p, axis=-1, keepdims=True)
            o_ref[:, cs] = jnp.dot(p.astype(MXU_DT), kv_ref[:, vs], preferred_element_type=F32).astype(o_ref.dtype)

    return pl.pallas_call(
        body,
        grid=(t // tt,),
        in_specs=[pl.BlockSpec((tt, D_MODEL), lambda i: (i, 0)), pl.BlockSpec((MEM_LEN, 2 * D_MODEL), lambda i: (0, 0))],
        out_specs=pl.BlockSpec((tt, D_MODEL), lambda i: (i, 0)),
        out_shape=jax.ShapeDtypeStruct((t, D_MODEL), MXU_DT),
        name=name,
        compiler_params=_cparams("parallel"),
    )(q, kvm)


def _xa_bwd(q, kvm, do, name):
    t = q.shape[0]
    tt = _rows(t, TT)
    scale = XA_HEAD_DIM ** -0.5

    def body(q_ref, kv_ref, do_ref, dq_ref, dkv_ref):
        @pl.when(pl.program_id(0) == 0)
        def _():
            dkv_ref[...] = jnp.zeros_like(dkv_ref)

        for h in range(XA_HEADS):
            cs = pl.ds(h * XA_HEAD_DIM, XA_HEAD_DIM)
            vs = pl.ds(D_MODEL + h * XA_HEAD_DIM, XA_HEAD_DIM)
            qh = q_ref[:, cs]
            kh = kv_ref[:, cs]
            doh = do_ref[:, cs]
            s = lax.dot_general(qh, kh, NT, preferred_element_type=F32) * scale
            p = jnp.exp(s - jnp.max(s, axis=-1, keepdims=True))
            p = p / jnp.sum(p, axis=-1, keepdims=True)
            dp = lax.dot_general(doh, kv_ref[:, vs], NT, preferred_element_type=F32)
            ds = (p * (dp - jnp.sum(p * dp, axis=-1, keepdims=True)) * scale).astype(MXU_DT)
            dq_ref[:, cs] = jnp.dot(ds, kh, preferred_element_type=F32).astype(dq_ref.dtype)
            dkv_ref[:, cs] += lax.dot_general(ds, qh, TN_DIMS, preferred_element_type=F32)
            dkv_ref[:, vs] += lax.dot_general(p.astype(MXU_DT), doh, TN_DIMS, preferred_element_type=F32)

    row = pl.BlockSpec((tt, D_MODEL), lambda i: (i, 0))
    kvs = pl.BlockSpec((MEM_LEN, 2 * D_MODEL), lambda i: (0, 0))
    return pl.pallas_call(
        body,
        grid=(t // tt,),
        in_specs=[row, kvs, row],
        out_specs=[row, kvs],
        out_shape=[jax.ShapeDtypeStruct((t, D_MODEL), MXU_DT), jax.ShapeDtypeStruct((MEM_LEN, 2 * D_MODEL), F32)],
        name=name,
        compiler_params=_cparams("arbitrary"),
    )(q, kvm, do)


def _rope_tables(positions, name):
    t = positions.shape[0]
    tt = _rows(t, TT)
    inv = 1.0 / (ROPE_THETA ** (np.arange(0, QK_ROPE, 2, dtype=np.float32) / QK_ROPE))
    lanes = np.zeros((1, HEAD_PAD), np.float32)
    half = QK_ROPE // 2
    lanes[0, KPE_LANE : KPE_LANE + half] = inv
    lanes[0, KPE_LANE + half : KPE_LANE + QK_ROPE] = inv

    def body(pos_ref, inv_ref, cos_ref, sa_ref, sb_ref):
        ang = pos_ref[...].astype(F32) * inv_ref[...]
        lane = lax.broadcasted_iota(jnp.int32, (tt, HEAD_PAD), 1)
        c = jnp.cos(ang)
        s = jnp.sin(ang)
        lo = (lane >= KPE_LANE) & (lane < KPE_LANE + half)
        hi = (lane >= KPE_LANE + half) & (lane < KPE_LANE + QK_ROPE)
        cos_ref[...] = jnp.where(lo | hi, c, 1.0)
        sa_ref[...] = jnp.where(hi, s, 0.0)
        sb_ref[...] = jnp.where(lo, -s, 0.0)

    tab = pl.BlockSpec((tt, HEAD_PAD), lambda i: (i, 0))
    return pl.pallas_call(
        body,
        grid=(t // tt,),
        in_specs=[pl.BlockSpec((tt, 1), lambda i: (i, 0)), pl.BlockSpec((1, HEAD_PAD), lambda i: (0, 0))],
        out_specs=[tab, tab, tab],
        out_shape=[jax.ShapeDtypeStruct((t, HEAD_PAD), F32)] * 3,
        name=name,
        compiler_params=_cparams("parallel"),
    )(positions, jnp.asarray(lanes))


def _rotate(x, cos, sa, sb, sign):
    half = QK_ROPE // 2
    return x * cos + sign * (pltpu.roll(x, half, 1) * sa + pltpu.roll(x, HEAD_PAD - half, 1) * sb)


def _rope_heads(x, tables, sign, scale, name):
    t, w = x.shape
    tt = _rows(t, TT)
    nh = w // HEAD_PAD

    def body(x_ref, c_ref, sa_ref, sb_ref, o_ref):
        cos, sa, sb = c_ref[...] * scale, sa_ref[...] * scale, sb_ref[...] * scale
        for h in range(nh):
            cs = pl.ds(h * HEAD_PAD, HEAD_PAD)
            o_ref[:, cs] = _rotate(x_ref[:, cs], cos, sa, sb, sign).astype(o_ref.dtype)

    tab = pl.BlockSpec((tt, HEAD_PAD), lambda i: (i, 0))
    row = pl.BlockSpec((tt, w), lambda i: (i, 0))
    return pl.pallas_call(
        body,
        grid=(t // tt,),
        in_specs=[row, tab, tab, tab],
        out_specs=row,
        out_shape=jax.ShapeDtypeStruct((t, w), MXU_DT),
        name=name,
        compiler_params=_cparams("parallel"),
    )(x, *tables)


def _mla_prep(cp, qg, kvg, tables, name):
    t = cp.shape[0]
    tt = _rows(t, TT)

    def body(cp_ref, qg_ref, kvg_ref, c_ref, sa_ref, sb_ref, qn_ref, kvn_ref, kpe_ref):
        cq = cp_ref[:, 0:Q_LORA]
        r = lax.rsqrt(jnp.mean(cq * cq, axis=-1, keepdims=True) + EPS)
        qn_ref[...] = ((cq * r) * qg_ref[...]).astype(qn_ref.dtype)
        ckv = cp_ref[:, Q_LORA : Q_LORA + KV_LORA]
        r = lax.rsqrt(jnp.mean(ckv * ckv, axis=-1, keepdims=True) + EPS)
        kvn_ref[...] = ((ckv * r) * kvg_ref[...]).astype(kvn_ref.dtype)
        kpe = cp_ref[:, Q_LORA + KV_LORA : C_PAD]
        kpe_ref[...] = _rotate(kpe, c_ref[...], sa_ref[...], sb_ref[...], 1.0).astype(kpe_ref.dtype)

    tab = pl.BlockSpec((tt, HEAD_PAD), lambda i: (i, 0))
    return pl.pallas_call(
        body,
        grid=(t // tt,),
        in_specs=[
            pl.BlockSpec((tt, C_PAD), lambda i: (i, 0)),
            pl.BlockSpec((1, Q_LORA), lambda i: (0, 0)),
            pl.BlockSpec((1, KV_LORA), lambda i: (0, 0)),
            tab, tab, tab,
        ],
        out_specs=[
            pl.BlockSpec((tt, Q_LORA), lambda i: (i, 0)),
            pl.BlockSpec((tt, KV_LORA), lambda i: (i, 0)),
            tab,
        ],
        out_shape=[
            jax.ShapeDtypeStruct((t, Q_LORA), MXU_DT),
            jax.ShapeDtypeStruct((t, KV_LORA), MXU_DT),
            jax.ShapeDtypeStruct((t, HEAD_PAD), MXU_DT),
        ],
        name=name,
        compiler_params=_cparams("parallel"),
    )(cp, qg.reshape(1, Q_LORA), kvg.reshape(1, KV_LORA), *tables)


def _mla_prep_bwd(cp, dqn, dkvn, dkpe_heads, qg, kvg, tables, name):
    t = cp.shape[0]
    tt = _rows(t, TT)

    def norm_bwd(x, dy, g):
        r = lax.rsqrt(jnp.mean(x * x, axis=-1, keepdims=True) + EPS)
        xh = x * r
        gy = dy * g
        return r * (gy - xh * jnp.mean(gy * xh, axis=-1, keepdims=True)), jnp.sum(dy * xh, axis=0, keepdims=True)

    def body(cp_ref, dqn_ref, dkvn_ref, dkpe_ref, qg_ref, kvg_ref, c_ref, sa_ref, sb_ref, dcp_ref, dqg_ref, dkvg_ref):
        @pl.when(pl.program_id(0) == 0)
        def _():
            dqg_ref[...] = jnp.zeros_like(dqg_ref)
            dkvg_ref[...] = jnp.zeros_like(dkvg_ref)

        dcq, dg = norm_bwd(cp_ref[:, 0:Q_LORA], dqn_ref[...], qg_ref[...])
        dcp_ref[:, 0:Q_LORA] = dcq.astype(dcp_ref.dtype)
        dqg_ref[...] += dg
        dckv, dg = norm_bwd(cp_ref[:, Q_LORA : Q_LORA + KV_LORA], dkvn_ref[...], kvg_ref[...])
        dcp_ref[:, Q_LORA : Q_LORA + KV_LORA] = dckv.astype(dcp_ref.dtype)
        dkvg_ref[...] += dg
        dk = dkpe_ref[0]
        for h in range(1, MLA_HEADS):
            dk = dk + dkpe_ref[h]
        dcp_ref[:, Q_LORA + KV_LORA : C_PAD] = _rotate(dk, c_ref[...], sa_ref[...], sb_ref[...], -1.0).astype(dcp_ref.dtype)

    tab = pl.BlockSpec((tt, HEAD_PAD), lambda i: (i, 0))
    return pl.pallas_call(
        body,
        grid=(t // tt,),
        in_specs=[
            pl.BlockSpec((tt, C_PAD), lambda i: (i, 0)),
            pl.BlockSpec((tt, Q_LORA), lambda i: (i, 0)),
            pl.BlockSpec((tt, KV_LORA), lambda i: (i, 0)),
            pl.BlockSpec((MLA_HEADS, tt, HEAD_PAD), lambda i: (0, i, 0)),
            pl.BlockSpec((1, Q_LORA), lambda i: (0, 0)),
            pl.BlockSpec((1, KV_LORA), lambda i: (0, 0)),
            tab, tab, tab,
        ],
        out_specs=[
            pl.BlockSpec((tt, C_PAD), lambda i: (i, 0)),
            pl.BlockSpec((1, Q_LORA), lambda i: (0, 0)),
            pl.BlockSpec((1, KV_LORA), lambda i: (0, 0)),
        ],
        out_shape=[
            jax.ShapeDtypeStruct((t, C_PAD), MXU_DT),
            jax.ShapeDtypeStruct((1, Q_LORA), F32),
            jax.ShapeDtypeStruct((1, KV_LORA), F32),
        ],
        name=name,
        compiler_params=_cparams("arbitrary"),
    )(cp, dqn, dkvn, dkpe_heads, qg.reshape(1, Q_LORA), kvg.reshape(1, KV_LORA), *tables)


def _flash_fwd(qs, kv, kpe, name):
    t = qs.shape[0]
    ta = _rows(t, TA)
    tq = _rows(t, TAQ)
    nq = t // tq
    per = tq // ta

    def body(q_ref, kv_ref, kpe_ref, o_ref, lse_ref):
        qi = pl.program_id(1)
        q = q_ref[...]
        lane = lax.broadcasted_iota(jnp.int32, (ta, HEAD_PAD), 1)

        def kblock(j):
            rows = pl.ds(pl.multiple_of(j * ta, ta), ta)
            kvb = kv_ref[rows, :]
            ones_v = jnp.where(lane < QK_NOPE, jnp.ones_like(kvb), kvb)
            return ones_v, jnp.where(lane < QK_NOPE, kvb, kpe_ref[rows, :])

        def update(carry, s, ones_v):
            m, acc = carry
            m_new = jnp.maximum(m, jnp.max(s, axis=-1, keepdims=True))
            p = jnp.exp2(s - m_new).astype(MXU_DT)
            acc = jnp.exp2(m - m_new) * acc + jnp.dot(p, ones_v, preferred_element_type=F32)
            return m_new, acc

        def step(j, carry):
            ones_v, k = kblock(j)
            return update(carry, lax.dot_general(q, k, NT, preferred_element_type=F32), ones_v)

        init = (jnp.full((tq, 1), -jnp.inf, F32), jnp.zeros((tq, HEAD_PAD), F32))
        carry = lax.fori_loop(0, qi * per, step, init)
        r = lax.broadcasted_iota(jnp.int32, (tq, ta), 0)
        c = lax.broadcasted_iota(jnp.int32, (tq, ta), 1)
        for d in range(per):
            ones_v, k = kblock(qi * per + d)
            s = lax.dot_general(q, k, NT, preferred_element_type=F32)
            carry = update(carry, jnp.where(c + d * ta <= r, s, NEG), ones_v)
        m, acc = carry
        l = acc[:, 0:1]
        lane_q = lax.broadcasted_iota(jnp.int32, (tq, HEAD_PAD), 1)
        o_ref[...] = jnp.where(lane_q >= QK_NOPE, acc / l, 0.0).astype(o_ref.dtype)
        lse_ref[...] = m + jnp.log2(l)

    return pl.pallas_call(
        body,
        grid=(MLA_HEADS, nq),
        in_specs=[
            pl.BlockSpec((tq, HEAD_PAD), lambda h, i: (i, h)),
            pl.BlockSpec((t, HEAD_PAD), lambda h, i: (0, h)),
            pl.BlockSpec((t, HEAD_PAD), lambda h, i: (0, 0)),
        ],
        out_specs=[
            pl.BlockSpec((tq, HEAD_PAD), lambda h, i: (i, h)),
            pl.BlockSpec((None, tq, 1), lambda h, i: (h, i, 0)),
        ],
        out_shape=[
            jax.ShapeDtypeStruct((t, MLA_HEADS * HEAD_PAD), MXU_DT),
            jax.ShapeDtypeStruct((MLA_HEADS, t, 1), F32),
        ],
        name=name,
        compiler_params=_cparams("parallel", "parallel"),
    )(qs, kv, kpe)


def _flash_bwd(qs, kv, kpe, o, do, lse, name):
    t = qs.shape[0]
    ta = _rows(t, TA)
    tq = _rows(t, TAQ)
    nq = t // ta
    per = tq // ta

    def body(q_ref, o_ref, do_ref, lse_ref, kv_ref, kpe_ref, dq_ref, dkv_ref, dkpe_ref, dk_acc, dv_acc):
        kj = pl.program_id(1)

        @pl.when(kj == 0)
        def _():
            dq_ref[...] = jnp.zeros_like(dq_ref)

        lane = lax.broadcasted_iota(jnp.int32, (ta, HEAD_PAD), 1)
        kvb = kv_ref[...]
        k = jnp.where(lane < QK_NOPE, kvb, kpe_ref[...])
        dk_acc[...] = jnp.zeros_like(dk_acc)
        dv_acc[...] = jnp.zeros_like(dv_acc)

        def tile(qq, first_key):
            rows = pl.ds(pl.multiple_of(qq * tq, tq), tq)
            q = q_ref[rows, :]
            dob = do_ref[rows, :]
            delta = jnp.sum(dob.astype(F32) * o_ref[rows, :].astype(F32), axis=-1, keepdims=True)
            s = lax.dot_general(q, k, NT, preferred_element_type=F32)
            if first_key is not None:
                r = lax.broadcasted_iota(jnp.int32, (tq, ta), 0)
                c = lax.broadcasted_iota(jnp.int32, (tq, ta), 1)
                s = jnp.where(c + first_key <= r, s, NEG)
            p = jnp.exp2(s - lse_ref[rows, :])
            dp = lax.dot_general(dob, kvb, NT, preferred_element_type=F32)
            ds = (p * (dp - delta)).astype(MXU_DT)
            dq_ref[rows, :] += jnp.dot(ds, k, preferred_element_type=F32)
            dk_acc[...] += lax.dot_general(ds, q, TN_DIMS, preferred_element_type=F32)
            dv_acc[...] += lax.dot_general(p.astype(MXU_DT), dob, TN_DIMS, preferred_element_type=F32)

        qq0 = kj // per
        tile(qq0, (kj - qq0 * per) * ta)

        def step(qq, carry):
            tile(qq, None)
            return carry

        lax.fori_loop(qq0 + 1, t // tq, step, 0)
        dk = dk_acc[...] * (1.0 / LOG2E)
        dkv_ref[...] = jnp.where(lane < QK_NOPE, dk, dv_acc[...]).astype(dkv_ref.dtype)
        dkpe_ref[...] = jnp.where((lane >= KPE_LANE) & (lane < KPE_LANE + QK_ROPE), dk, 0.0)

    head_rows = pl.BlockSpec((t, HEAD_PAD), lambda h, j: (0, h))
    return pl.pallas_call(
        body,
        grid=(MLA_HEADS, nq),
        in_specs=[
            head_rows,
            head_rows,
            head_rows,
            pl.BlockSpec((None, t, 1), lambda h, j: (h, 0, 0)),
            pl.BlockSpec((ta, HEAD_PAD), lambda h, j: (j, h)),
            pl.BlockSpec((ta, HEAD_PAD), lambda h, j: (j, 0)),
        ],
        out_specs=[
            head_rows,
            pl.BlockSpec((ta, HEAD_PAD), lambda h, j: (j, h)),
            pl.BlockSpec((None, ta, HEAD_PAD), lambda h, j: (h, j, 0)),
        ],
        out_shape=[
            jax.ShapeDtypeStruct((t, MLA_HEADS * HEAD_PAD), F32),
            jax.ShapeDtypeStruct((t, MLA_HEADS * HEAD_PAD), MXU_DT),
            jax.ShapeDtypeStruct((MLA_HEADS, t, HEAD_PAD), F32),
        ],
        scratch_shapes=[pltpu.VMEM((ta, HEAD_PAD), F32), pltpu.VMEM((ta, HEAD_PAD), F32)],
        name=name,
        compiler_params=_cparams("parallel", "arbitrary"),
    )(qs, o, do, lse, kv, kpe)


def _as2d(a):
    if a.ndim == 1:
        return a.reshape(1, a.shape[0])
    return a.reshape(-1, a.shape[-1])


def _adamw(w, g, m, v, name):
    shape = w.shape
    w2, g2, m2, v2 = (_as2d(a) for a in (w, g, m, v))
    r, c = w2.shape
    tr = _tile_rows(r, c)
    c1 = 1.0 - ADAM_B1 ** ADAM_STEP
    c2 = 1.0 - ADAM_B2 ** ADAM_STEP

    def body(w_ref, g_ref, m_ref, v_ref, d_ref, nm_ref, nv_ref):
        gv = g_ref[...]
        nm = ADAM_B1 * m_ref[...] + (1.0 - ADAM_B1) * gv
        nv = ADAM_B2 * v_ref[...] + (1.0 - ADAM_B2) * (gv * gv)
        d_ref[...] = -ADAM_LR * ((nm / c1) / (jnp.sqrt(nv / c2) + ADAM_EPS) + ADAM_WD * w_ref[...])
        nm_ref[...] = nm
        nv_ref[...] = nv

    blk = pl.BlockSpec((tr, c), lambda i: (i, 0))
    outs = pl.pallas_call(
        body,
        grid=(r // tr,),
        in_specs=[blk] * 4,
        out_specs=[blk] * 3,
        out_shape=[jax.ShapeDtypeStruct((r, c), F32)] * 3,
        name=name,
        compiler_params=_cparams("parallel"),
    )(w2, g2, m2, v2)
    return tuple(o.reshape(shape) for o in outs)


def _adamw_halves(w, mine, other, m, v, c_idx, name):
    nl, r, c = w.shape
    h = nl // 2
    tr = _tile_rows(r, 2 * c)
    c1 = 1.0 - ADAM_B1 ** ADAM_STEP
    c2 = 1.0 - ADAM_B2 ** ADAM_STEP

    def body(c_ref, w_ref, a_ref, b_ref, m_ref, v_ref, g_ref, d_ref, nm_ref, nv_ref):
        l = pl.program_id(0)
        gv = jnp.where(l // h == c_ref[0], a_ref[...], b_ref[...])
        nm = ADAM_B1 * m_ref[...] + (1.0 - ADAM_B1) * gv
        nv = ADAM_B2 * v_ref[...] + (1.0 - ADAM_B2) * (gv * gv)
        g_ref[...] = gv
        d_ref[...] = -ADAM_LR * ((nm / c1) / (jnp.sqrt(nv / c2) + ADAM_EPS) + ADAM_WD * w_ref[...])
        nm_ref[...] = nm
        nv_ref[...] = nv

    def half_map(mine_side):
        def index(l, i, cr):
            first = cr[0] * h if mine_side else (1 - cr[0]) * h
            return (jnp.clip(l - first, 0, h - 1), i, 0)
        return index

    full = pl.BlockSpec((None, tr, c), lambda l, i, cr: (l, i, 0))
    grid_spec = pltpu.PrefetchScalarGridSpec(
        num_scalar_prefetch=1,
        grid=(nl, r // tr),
        in_specs=[full, pl.BlockSpec((None, tr, c), half_map(True)), pl.BlockSpec((None, tr, c), half_map(False)), full, full],
        out_specs=[full] * 4,
    )
    return pl.pallas_call(
        body,
        grid_spec=grid_spec,
        out_shape=[jax.ShapeDtypeStruct((nl, r, c), F32)] * 4,
        name=name,
        compiler_params=_cparams("parallel", "parallel"),
    )(c_idx, w, mine, other, m, v)


def _tile_rows(r, c, mult=SUBLANE):
    limit = max(mult, (BLOCK_BYTES // 4) // (4 * c))
    if r <= limit:
        return r
    t = (limit // mult) * mult
    while t >= mult:
        if r % t == 0:
            return t
        t -= mult
    return r


def _sum_leading(a, name):
    n, r, c = a.shape
    tr = _tile_rows(r, c * n)

    def body(a_ref, o_ref):
        s = a_ref[0]
        for k in range(1, n):
            s = s + a_ref[k]
        o_ref[...] = s

    return pl.pallas_call(
        body,
        grid=(r // tr,),
        in_specs=[pl.BlockSpec((n, tr, c), lambda i: (0, i, 0))],
        out_specs=pl.BlockSpec((tr, c), lambda i: (i, 0)),
        out_shape=jax.ShapeDtypeStruct((r, c), F32),
        name=name,
        compiler_params=_cparams("parallel"),
    )(a)


def _add_half(g, s, c_idx, name):
    nl, r, c = g.shape
    h = nl // 2
    tr = _tile_rows(r, 2 * c, 2 * SUBLANE)

    def body(c_ref, g_ref, s_ref, o_ref):
        o_ref[...] = (g_ref[...] + s_ref[...]).astype(o_ref.dtype)

    grid_spec = pltpu.PrefetchScalarGridSpec(
        num_scalar_prefetch=1,
        grid=(h, r // tr),
        in_specs=[
            pl.BlockSpec((None, tr, c), lambda l, i, cr: (cr[0] * h + l, i, 0)),
            pl.BlockSpec((None, tr, c), lambda l, i, cr: (l, i, 0)),
        ],
        out_specs=pl.BlockSpec((None, tr, c), lambda l, i, cr: (l, i, 0)),
    )
    return pl.pallas_call(
        body,
        grid_spec=grid_spec,
        out_shape=jax.ShapeDtypeStruct((h, r, c), XFER_DT),
        name=name,
        compiler_params=_cparams("parallel", "parallel"),
    )(c_idx, g, s)


def _sum_chips(slots, pair, chip_idx, kind, name):
    _, h, r, c = slots.shape
    tr = _tile_rows(r, 5 * c, 2 * SUBLANE)
    nr = r // tr

    def body(chip_ref, s_ref, own_ref, o_ref):
        chip = chip_ref[0]
        own = own_ref[...].astype(F32)
        parts = [s_ref[j].astype(F32) for j in range(3)]
        total = None
        for k in range(4):
            d = jnp.bitwise_xor(chip, k)
            v = jnp.where(d == 0, own, jnp.where(d == 2, parts[0], jnp.where(d == 1, parts[1], parts[2])))
            total = v if total is None else total + v
        o_ref[...] = total

    if kind == "row":
        own_spec = pl.BlockSpec((None, tr, c), lambda l, i, cr: (l, cr[0] * nr + i, 0))
    else:
        own_spec = pl.BlockSpec((None, tr, c), lambda l, i, cr: (l, i, cr[0]))
    grid_spec = pltpu.PrefetchScalarGridSpec(
        num_scalar_prefetch=1,
        grid=(h, nr),
        in_specs=[pl.BlockSpec((3, None, tr, c), lambda l, i, cr: (0, l, i, 0)), own_spec],
        out_specs=pl.BlockSpec((None, tr, c), lambda l, i, cr: (l, i, 0)),
    )
    return pl.pallas_call(
        body,
        grid_spec=grid_spec,
        out_shape=jax.ShapeDtypeStruct((h, r, c), F32),
        name=name,
        compiler_params=_cparams("parallel", "parallel"),
    )(chip_idx, slots, pair)


def _mesh_pos():
    return lax.axis_index("x"), lax.axis_index("y"), lax.axis_index("c")


def _other_chips(x, y):
    return [(1 - x, y), (x, 1 - y), (1 - x, 1 - y)]


def _all_gather_rows(block, name):
    m_per, n = block.shape

    def body(x_ref, out_ref, send_sems, recv_sems, local_sem):
        x, y, c = _mesh_pos()
        me, sibling = (x, y, c), (x, y, 1 - c)
        chips = _other_chips(x, y)

        def rows(px, py, pc):
            return out_ref.at[pl.ds((4 * px + 2 * py + pc) * m_per, m_per), :]

        def copy(k, blk, to, src=None):
            return pltpu.make_async_remote_copy(
                src_ref=rows(*blk) if src is None else src,
                dst_ref=rows(*blk),
                send_sem=send_sems.at[k],
                recv_sem=recv_sems.at[k],
                device_id=to,
                device_id_type=MESH_ID,
            )

        mine = pltpu.make_async_copy(x_ref, rows(*me), local_sem)
        mine.start()
        first = [copy(0, me, sibling, src=x_ref)]
        first += [copy(1 + j, me, (*chip, c), src=x_ref) for j, chip in enumerate(chips)]
        for cp in first:
            cp.start()
        passed = [copy(4 + j, (*chip, c), sibling) for j, chip in enumerate(chips)]
        for j, chip in enumerate(chips):
            copy(1 + j, (*chip, c), me).wait_recv()
            passed[j].start()
        copy(0, sibling, me).wait_recv()
        for j, chip in enumerate(chips):
            copy(4 + j, (*chip, 1 - c), me).wait_recv()
        for cp in first + passed:
            cp.wait_send()
        mine.wait()

    return pl.pallas_call(
        body,
        out_shape=jax.ShapeDtypeStruct((8 * m_per, n), block.dtype),
        in_specs=[pl.BlockSpec(memory_space=pltpu.VMEM)],
        out_specs=pl.BlockSpec(memory_space=pltpu.VMEM),
        scratch_shapes=[pltpu.SemaphoreType.DMA((7,)), pltpu.SemaphoreType.DMA((7,)), pltpu.SemaphoreType.DMA],
        name=name,
        compiler_params=pltpu.CompilerParams(vmem_limit_bytes=VMEM_LIMIT),
    )(block)


def _shard_window(ref, layers, chip, rows, cols):
    if rows is not None:
        return ref.at[layers, pl.ds(pl.multiple_of(chip * rows, rows), rows), :]
    return ref.at[layers, :, pl.ds(pl.multiple_of(chip * cols, cols), cols)]


def _all_gather_weights(shards, kinds, name):
    nw = len(shards)
    out_shapes = []
    for s, kind in zip(shards, kinds):
        nl, r, c = s.shape
        full = (nl, 4 * r, c) if kind == "row" else (nl, r, 4 * c)
        out_shapes.append(jax.ShapeDtypeStruct(full, s.dtype))

    def body(*refs):
        ins, outs = refs[:nw], refs[nw : 2 * nw]
        send_sems, recv_sems, in_sems, out_sems = refs[2 * nw : 2 * nw + 4]
        bufs = refs[2 * nw + 4 :]
        x, y, c = _mesh_pos()
        sibling = (x, y, 1 - c)
        chips = _other_chips(x, y)
        my_chip = 2 * x + y

        def window(w, chip, layers):
            _, r, cc = shards[w].shape
            if kinds[w] == "row":
                return _shard_window(outs[w], layers, chip, r, None)
            return _shard_window(outs[w], layers, chip, None, cc)

        def half(w, half_idx):
            h = shards[w].shape[0] // 2
            return pl.ds(half_idx * h, h)

        def copy(w, k, src, dst, to):
            return pltpu.make_async_remote_copy(
                src_ref=src, dst_ref=dst, send_sem=send_sems.at[w, k], recv_sem=recv_sems.at[w, k],
                device_id=to, device_id_type=MESH_ID)

        sent = []
        for w in range(nw):
            mine = ins[w].at[half(w, c)]
            for j, chip in enumerate(chips):
                cp = copy(w, j, mine, window(w, my_chip, half(w, c)), (*chip, c))
                cp.start()
                sent.append(cp)
        for w in range(nw):
            nl = shards[w].shape[0]

            def load(l, w=w):
                return pltpu.make_async_copy(ins[w].at[l], bufs[w].at[l % 2], in_sems.at[w, l % 2])

            def store(l, w=w):
                return pltpu.make_async_copy(bufs[w].at[l % 2], window(w, my_chip, l), out_sems.at[w, l % 2])

            load(0).start()
            for l in range(nl):
                load(l).wait()
                store(l).start()
                if l + 1 < nl:
                    if l >= 1:
                        store(l - 1).wait()
                    load(l + 1).start()
            for l in range(max(nl - 2, 0), nl):
                store(l).wait()
        for w in range(nw):
            for j, (cx, cy) in enumerate(chips):
                got = window(w, 2 * cx + cy, half(w, c))
                copy(w, j, got, got, (cx, cy, c)).wait_recv()
                cp = copy(w, 3 + j, got, got, sibling)
                cp.start()
                sent.append(cp)
        for w in range(nw):
            for j, (cx, cy) in enumerate(chips):
                got = window(w, 2 * cx + cy, half(w, 1 - c))
                copy(w, 3 + j, got, got, sibling).wait_recv()
        for cp in sent:
            cp.wait_send()

    anyspec = pl.BlockSpec(memory_space=pl.ANY)
    return pl.pallas_call(
        body,
        out_shape=out_shapes,
        in_specs=[anyspec] * nw,
        out_specs=[anyspec] * nw,
        scratch_shapes=[pltpu.SemaphoreType.DMA((nw, 6)), pltpu.SemaphoreType.DMA((nw, 6)),
                        pltpu.SemaphoreType.DMA((nw, 2)), pltpu.SemaphoreType.DMA((nw, 2))]
        + [pltpu.VMEM((2,) + s.shape[1:], s.dtype) for s in shards],
        name=name,
        compiler_params=pltpu.CompilerParams(vmem_limit_bytes=VMEM_LIMIT),
    )(*shards)


def _exchange_halves(grads, name):
    nw = len(grads)
    out_shapes = [jax.ShapeDtypeStruct((g.shape[0] // 2,) + g.shape[1:], g.dtype) for g in grads]

    def body(*refs):
        ins, outs = refs[:nw], refs[nw : 2 * nw]
        send_sems, recv_sems = refs[2 * nw :]
        x, y, c = _mesh_pos()
        cps = []
        for w in range(nw):
            h = grads[w].shape[0] // 2
            cp = pltpu.make_async_remote_copy(
                src_ref=ins[w].at[pl.ds((1 - c) * h, h)], dst_ref=outs[w], send_sem=send_sems.at[w],
                recv_sem=recv_sems.at[w], device_id=(x, y, 1 - c), device_id_type=MESH_ID)
            cp.start()
            cps.append(cp)
        for cp in cps:
            cp.wait()

    anyspec = pl.BlockSpec(memory_space=pl.ANY)
    return pl.pallas_call(
        body,
        out_shape=out_shapes,
        in_specs=[anyspec] * nw,
        out_specs=[anyspec] * nw,
        scratch_shapes=[pltpu.SemaphoreType.DMA((nw,)), pltpu.SemaphoreType.DMA((nw,))],
        name=name,
    )(*grads)


def _scatter_to_chips(parts, kinds, name):
    nw = len(parts)
    shard_shapes = []
    for p, kind in zip(parts, kinds):
        h, r, c = p.shape
        shard_shapes.append((h, r // 4, c) if kind == "row" else (h, r, c // 4))
    out_shapes = [jax.ShapeDtypeStruct((3,) + s, p.dtype) for s, p in zip(shard_shapes, parts)]

    def body(*refs):
        ins, outs = refs[:nw], refs[nw : 2 * nw]
        send_sems, recv_sems = refs[2 * nw :]
        x, y, c = _mesh_pos()
        chips = _other_chips(x, y)

        def piece(w, chip):
            h, r, cc = shard_shapes[w]
            if kinds[w] == "row":
                return _shard_window(ins[w], pl.ds(0, h), chip, r, None)
            return _shard_window(ins[w], pl.ds(0, h), chip, None, cc)

        def copy(w, j, cx, cy):
            return pltpu.make_async_remote_copy(
                src_ref=piece(w, 2 * cx + cy), dst_ref=outs[w].at[j], send_sem=send_sems.at[w, j],
                recv_sem=recv_sems.at[w, j], device_id=(cx, cy, c), device_id_type=MESH_ID)

        cps = [copy(w, j, cx, cy) for w in range(nw) for j, (cx, cy) in enumerate(chips)]
        for cp in cps:
            cp.start()
        for cp in cps:
            cp.wait()

    anyspec = pl.BlockSpec(memory_space=pl.ANY)
    return pl.pallas_call(
        body,
        out_shape=out_shapes,
        in_specs=[anyspec] * nw,
        out_specs=[anyspec] * nw,
        scratch_shapes=[pltpu.SemaphoreType.DMA((nw, 3)), pltpu.SemaphoreType.DMA((nw, 3))],
        name=name,
    )(*parts)


def _swap_halves(halves, name):
    nw = len(halves)
    out_shapes = [jax.ShapeDtypeStruct(p.shape, p.dtype) for p in halves]

    def body(*refs):
        ins, outs = refs[:nw], refs[nw : 2 * nw]
        send_sems, recv_sems = refs[2 * nw :]
        x, y, c = _mesh_pos()
        cps = [pltpu.make_async_remote_copy(
            src_ref=ins[w], dst_ref=outs[w], send_sem=send_sems.at[w], recv_sem=recv_sems.at[w],
            device_id=(x, y, 1 - c), device_id_type=MESH_ID) for w in range(nw)]
        for cp in cps:
            cp.start()
        for cp in cps:
            cp.wait()

    anyspec = pl.BlockSpec(memory_space=pl.ANY)
    return pl.pallas_call(
        body,
        out_shape=out_shapes,
        in_specs=[anyspec] * nw,
        out_specs=[anyspec] * nw,
        scratch_shapes=[pltpu.SemaphoreType.DMA((nw,)), pltpu.SemaphoreType.DMA((nw,))],
        name=name,
    )(*halves)


def _pad_wdq(w):
    z = lambda n: jnp.zeros((w.shape[0], n), w.dtype)
    base = Q_LORA + KV_LORA
    return jnp.concatenate([w[:, :base], z(KPE_LANE), w[:, base:], z(HEAD_PAD - KPE_LANE - QK_ROPE)], axis=1)


def _unpad_wdq(g):
    base = Q_LORA + KV_LORA
    return jnp.concatenate([g[:, :base], g[:, base + KPE_LANE : base + KPE_LANE + QK_ROPE]], axis=1)


def _pad_wuq(w):
    w3 = w.reshape(Q_LORA, MLA_HEADS, QK_NOPE + QK_ROPE)
    w3 = jnp.pad(w3, ((0, 0), (0, 0), (0, HEAD_PAD - QK_NOPE - QK_ROPE)))
    return w3.reshape(Q_LORA, MLA_HEADS * HEAD_PAD)


def _unpad_wuq(g):
    g3 = g.reshape(Q_LORA, MLA_HEADS, HEAD_PAD)[:, :, : QK_NOPE + QK_ROPE]
    return g3.reshape(Q_LORA, MLA_HEADS * (QK_NOPE + QK_ROPE))


def _pad_wo(w):
    w3 = w.reshape(MLA_HEADS, V_HEAD, D_MODEL)
    w3 = jnp.pad(w3, ((0, 0), (HEAD_PAD - V_HEAD, 0), (0, 0)))
    return w3.reshape(MLA_HEADS * HEAD_PAD, D_MODEL)


def _unpad_wo(g):
    g3 = g.reshape(MLA_HEADS, HEAD_PAD, D_MODEL)[:, HEAD_PAD - V_HEAD :, :]
    return g3.reshape(MLA_HEADS * V_HEAD, D_MODEL)


def _local_step(x, mem, positions, target, wb, ws):
    t = x.shape[0]
    tables = _rope_tables(positions.reshape(t, 1), "rope_tables")
    saved = []
    for l in range(DEPTH):
        s = {"x0": x}
        h1 = _rms_fwd(x, ws["norm_mix_g"][l], f"l{l}_norm_mix")
        s["h1"] = h1
        if l % 2 == 0:
            e = l // 2
            z = _matmul(h1, wb["pc_w_in"], "nn", F32, f"l{l}_pc_in", layer=e)
            cat = _mix_fwd(z, ws["pool_w"][e], ws["pool_scale"][e], ws["conv_dw_w"][e], ws["conv_dw_b"][e],
                           ws["conv_ln_g"][e], ws["conv_ln_b"][e], f"l{l}_mix")
            x = _matmul(cat, wb["pc_w_out"], "nn", F32, f"l{l}_pc_out", layer=e, res=x)
            s.update(z=z, cat=cat)
        else:
            o = l // 2
            cp = _matmul(h1, wb["mla_wdq"], "nn", F32, f"l{l}_mla_dq", layer=o)
            qn, kvn, kpe = _mla_prep(cp, ws["mla_q_norm_g"][o], ws["mla_kv_norm_g"][o], tables, f"l{l}_mla_prep")
            q = _matmul(qn, wb["mla_wuq"], "nn", F32, f"l{l}_mla_uq", layer=o)
            qr = _rope_heads(q, tables, 1.0, MLA_SCALE * LOG2E, f"l{l}_mla_rope")
            kv = _matmul(kvn, wb["mla_w_ukv"], "nn", MXU_DT, f"l{l}_mla_ukv", layer=o)
            att, lse = _flash_fwd(qr, kv, kpe, f"l{l}_mla_attn")
            x = _matmul(att, wb["mla_wo"], "nn", F32, f"l{l}_mla_o", layer=o, res=x)
            s.update(cp=cp, qn=qn, kvn=kvn, kpe=kpe, qr=qr, kv=kv, att=att, lse=lse)
        s["x1"] = x
        h2 = _rms_fwd(x, ws["norm_xa_g"][l], f"l{l}_norm_xa")
        hm = _rms_fwd(mem, ws["norm_mem_g"][l], f"l{l}_norm_mem")
        q2 = _matmul(h2, wb["xa_wq"], "nn", MXU_DT, f"l{l}_xa_q", layer=l)
        kvm = _matmul(hm, wb["xa_wkv"], "nn", MXU_DT, f"l{l}_xa_kv", layer=l)
        o2 = _xa_fwd(q2, kvm, f"l{l}_xa_attn")
        x = _matmul(o2, wb["xa_wo"], "nn", F32, f"l{l}_xa_o", layer=l, res=x)
        s.update(h2=h2, hm=hm, q2=q2, kvm=kvm, o2=o2, x2=x)
        h3 = _rms_fwd(x, ws["norm_ffn_g"][l], f"l{l}_norm_ffn")
        up = _matmul(h3, wb["ffn_w_up"], "nn", F32, f"l{l}_ffn_up", layer=l)
        act = _ffn_fwd(up, ws["ffn_conv_w"][l], ws["ffn_conv_b"][l], f"l{l}_ffn_mid")
        x = _matmul(act, wb["ffn_w_down"], "nn", F32, f"l{l}_ffn_down", layer=l, res=x)
        s.update(h3=h3, up=up, act=act)
        saved.append(s)

    dx, dg_final, loss = _loss_head(x, target, ws["final_norm_g"], "loss_head")
    g = {k: [None] * DEPTH for k in ("norm_mix_g", "norm_xa_g", "norm_mem_g", "xa_wq", "xa_wkv", "xa_wo", "norm_ffn_g",
                                      "ffn_w_up", "ffn_conv_w", "ffn_conv_b", "ffn_w_down")}
    g.update({k: [None] * (DEPTH // 2) for k in ("pc_w_in", "pool_w", "pool_scale", "conv_dw_w", "conv_dw_b", "conv_ln_g",
                                                 "conv_ln_b", "pc_w_out", "mla_w_dq_dkv", "mla_q_norm_g", "mla_w_uq",
                                                 "mla_kv_norm_g", "mla_w_ukv", "mla_w_o")})
    stk = {k: None for k in ("xa_wq", "xa_wkv", "xa_wo", "ffn_w_up", "ffn_w_down", "pc_w_in", "pc_w_out")}
    for l in reversed(range(DEPTH)):
        s = saved[l]
        dact = _matmul(dx, wb["ffn_w_down"], "nt", F32, f"l{l}_b_ffn_dact", layer=l)
        stk["ffn_w_down"] = _matmul(s["act"], dx, "tn", F32, f"l{l}_b_ffn_dwdown", stack=(stk["ffn_w_down"], l, DEPTH))
        dup, dcw, dcb = _ffn_bwd(s["up"], dact, ws["ffn_conv_w"][l], ws["ffn_conv_b"][l], f"l{l}_b_ffn_mid")
        g["ffn_conv_w"][l], g["ffn_conv_b"][l] = dcw, dcb[0]
        stk["ffn_w_up"] = _matmul(s["h3"], dup, "tn", F32, f"l{l}_b_ffn_dwup", stack=(stk["ffn_w_up"], l, DEPTH))
        dh = _matmul(dup, wb["ffn_w_up"], "nt", F32, f"l{l}_b_ffn_dh", layer=l)
        dx, dg = _rms_bwd(dh, s["x2"], ws["norm_ffn_g"][l], dx, f"l{l}_b_norm_ffn")
        g["norm_ffn_g"][l] = dg[0]
        do2 = _matmul(dx, wb["xa_wo"], "nt", MXU_DT, f"l{l}_b_xa_do", layer=l)
        stk["xa_wo"] = _matmul(s["o2"], dx, "tn", F32, f"l{l}_b_xa_dwo", stack=(stk["xa_wo"], l, DEPTH))
        dq2, dkvm = _xa_bwd(s["q2"], s["kvm"], do2, f"l{l}_b_xa_attn")
        stk["xa_wq"] = _matmul(s["h2"], dq2, "tn", F32, f"l{l}_b_xa_dwq", stack=(stk["xa_wq"], l, DEPTH))
        dh = _matmul(dq2, wb["xa_wq"], "nt", F32, f"l{l}_b_xa_dh", layer=l)
        stk["xa_wkv"] = _matmul(s["hm"], dkvm, "tn", F32, f"l{l}_b_xa_dwkv", stack=(stk["xa_wkv"], l, DEPTH))
        dhm = _matmul(dkvm, wb["xa_wkv"], "nt", F32, f"l{l}_b_xa_dhm", layer=l)
        g["norm_mem_g"][l] = _rms_bwd_gain(dhm, mem, ws["norm_mem_g"][l], f"l{l}_b_norm_mem")[0]
        dx, dg = _rms_bwd(dh, s["x1"], ws["norm_xa_g"][l], dx, f"l{l}_b_norm_xa")
        g["norm_xa_g"][l] = dg[0]
        if l % 2 == 0:
            e = l // 2
            dcat = _matmul(dx, wb["pc_w_out"], "nt", F32, f"l{l}_b_pc_dcat", layer=e)
            stk["pc_w_out"] = _matmul(s["cat"], dx, "tn", F32, f"l{l}_b_pc_dwout", stack=(stk["pc_w_out"], e, DEPTH // 2))
            dz, dpw, dps, dww, dwb, dlg, dlb = _mix_bwd(
                s["z"], dcat, ws["pool_w"][e], ws["pool_scale"][e], ws["conv_dw_w"][e], ws["conv_dw_b"][e],
                ws["conv_ln_g"][e], ws["conv_ln_b"][e], f"l{l}_b_mix")
            g["pool_w"][e], g["pool_scale"][e], g["conv_dw_w"][e] = dpw, dps[0], dww
            g["conv_dw_b"][e], g["conv_ln_g"][e], g["conv_ln_b"][e] = dwb[0], dlg[0], dlb[0]
            stk["pc_w_in"] = _matmul(s["h1"], dz, "tn", F32, f"l{l}_b_pc_dwin", stack=(stk["pc_w_in"], e, DEPTH // 2))
            dh = _matmul(dz, wb["pc_w_in"], "nt", F32, f"l{l}_b_pc_dh", layer=e)
        else:
            o = l // 2
            do = _matmul(dx, wb["mla_wo"], "nt", MXU_DT, f"l{l}_b_mla_do", layer=o)
            g["mla_w_o"][o] = _unpad_wo(_matmul(s["att"], dx, "tn", F32, f"l{l}_b_mla_dwo"))
            dqr, dkv, dkpe = _flash_bwd(s["qr"], s["kv"], s["kpe"], s["att"], do, s["lse"], f"l{l}_b_mla_attn")
            dq = _rope_heads(dqr, tables, -1.0, MLA_SCALE, f"l{l}_b_mla_rope")
            g["mla_w_uq"][o] = _unpad_wuq(_matmul(s["qn"], dq, "tn", F32, f"l{l}_b_mla_dwuq"))
            dqn = _matmul(dq, wb["mla_wuq"], "nt", F32, f"l{l}_b_mla_dqn", layer=o)
            g["mla_w_ukv"][o] = _matmul(s["kvn"], dkv, "tn", F32, f"l{l}_b_mla_dwukv")
            dkvn = _matmul(dkv, wb["mla_w_ukv"], "nt", F32, f"l{l}_b_mla_dkvn", layer=o)
            dcp, dqg, dkvg = _mla_prep_bwd(s["cp"], dqn, dkvn, dkpe, ws["mla_q_norm_g"][o], ws["mla_kv_norm_g"][o],
                                           tables, f"l{l}_b_mla_prep")
            g["mla_q_norm_g"][o], g["mla_kv_norm_g"][o] = dqg[0], dkvg[0]
            g["mla_w_dq_dkv"][o] = _unpad_wdq(_matmul(s["h1"], dcp, "tn", F32, f"l{l}_b_mla_dwdq"))
            dh = _matmul(dcp, wb["mla_wdq"], "nt", F32, f"l{l}_b_mla_dh", layer=o)
        dx, dg = _rms_bwd(dh, s["x0"], ws["norm_mix_g"][l], dx, f"l{l}_b_norm_mix")
        g["norm_mix_g"][l] = dg[0]
    grads = {k: jnp.stack(v) for k, v in g.items() if k not in stk}
    grads.update(stk)
    grads["final_norm_g"] = dg_final[0]
    return loss, dx, grads


BIG = (
    ("xa_wq", "row"), ("xa_wkv", "col"), ("xa_wo", "row"), ("ffn_w_up", "col"), ("ffn_w_down", "row"),
    ("pc_w_in", "col"), ("pc_w_out", "row"), ("mla_w_dq_dkv", "row"), ("mla_w_uq", "col"), ("mla_w_ukv", "col"),
    ("mla_w_o", "row"),
)
SMALL_SHARDED = ("ffn_conv_w", "conv_dw_w", "mla_q_norm_g", "mla_kv_norm_g")
SMALL_REPLICATED = ("norm_mix_g", "norm_xa_g", "norm_mem_g", "norm_ffn_g", "ffn_conv_b", "pool_w", "pool_scale",
                    "conv_dw_b", "conv_ln_g", "conv_ln_b", "final_norm_g")
WEIGHTS = ("norm_mix_g", "norm_xa_g", "norm_mem_g", "xa_wq", "xa_wkv", "xa_wo", "norm_ffn_g", "ffn_w_up", "ffn_conv_w",
           "ffn_conv_b", "ffn_w_down", "pc_w_in", "pool_w", "pool_scale", "conv_dw_w", "conv_dw_b", "conv_ln_g",
           "conv_ln_b", "pc_w_out", "mla_w_dq_dkv", "mla_q_norm_g", "mla_w_uq", "mla_kv_norm_g", "mla_w_ukv", "mla_w_o",
           "final_norm_g")
PACK_ROW = SUBLANE * LANE


def _pack(arrays):
    flat = jnp.concatenate([a.reshape(-1) for a in arrays])
    n = flat.shape[0]
    pad = (-n) % PACK_ROW
    return jnp.pad(flat, (0, pad)).reshape(-1, LANE)


def _unpack(flat, shapes):
    out, off = [], 0
    for s in shapes:
        n = int(np.prod(s))
        out.append(flat[off : off + n].reshape(s))
        off += n
    return out


def kernel(x, mem, positions, norm_mix_g, norm_xa_g, norm_mem_g, xa_wq, xa_wkv, xa_wo, norm_ffn_g, ffn_w_up, ffn_conv_w, ffn_conv_b, ffn_w_down, pc_w_in, pool_w, pool_scale, conv_dw_w, conv_dw_b, conv_ln_g, conv_ln_b, pc_w_out, mla_w_dq_dkv, mla_q_norm_g, mla_w_uq, mla_kv_norm_g, mla_w_ukv, mla_w_o, final_norm_g, loss_target, m_norm_mix_g, m_norm_xa_g, m_norm_mem_g, m_xa_wq, m_xa_wkv, m_xa_wo, m_norm_ffn_g, m_ffn_w_up, m_ffn_conv_w, m_ffn_conv_b, m_ffn_w_down, m_pc_w_in, m_pool_w, m_pool_scale, m_conv_dw_w, m_conv_dw_b, m_conv_ln_g, m_conv_ln_b, m_pc_w_out, m_mla_w_dq_dkv, m_mla_q_norm_g, m_mla_w_uq, m_mla_kv_norm_g, m_mla_w_ukv, m_mla_w_o, m_final_norm_g, v_norm_mix_g, v_norm_xa_g, v_norm_mem_g, v_xa_wq, v_xa_wkv, v_xa_wo, v_norm_ffn_g, v_ffn_w_up, v_ffn_conv_w, v_ffn_conv_b, v_ffn_w_down, v_pc_w_in, v_pool_w, v_pool_scale, v_conv_dw_w, v_conv_dw_b, v_conv_ln_g, v_conv_ln_b, v_pc_w_out, v_mla_w_dq_dkv, v_mla_q_norm_g, v_mla_w_uq, v_mla_kv_norm_g, v_mla_w_ukv, v_mla_w_o, v_final_norm_g):
    args = dict(locals())
    w = {n: args[n] for n in WEIGHTS}
    m = {n: args["m_" + n] for n in WEIGHTS}
    v = {n: args["v_" + n] for n in WEIGHTS}
    cx, cy, cc = lax.axis_index("x"), lax.axis_index("y"), lax.axis_index("c")
    chip = 2 * cx + cy

    full = _all_gather_weights([w[n].astype(MXU_DT) for n, _ in BIG], [k for _, k in BIG], "gather_weights")
    full = dict(zip([n for n, _ in BIG], full))
    small_shapes = [w[n].shape for n in SMALL_SHARDED]
    gathered = _all_gather_rows(_pack([w[n] for n in SMALL_SHARDED]), "gather_small")
    gathered = gathered.reshape(8, -1)
    ws = {n: w[n] for n in SMALL_REPLICATED}
    pieces = [_unpack(gathered[2 * k], small_shapes) for k in range(4)]
    for i, n in enumerate(SMALL_SHARDED):
        ws[n] = jnp.concatenate([pieces[k][i] for k in range(4)], axis=-1)
    wb = {n: full[n] for n in ("xa_wq", "xa_wkv", "xa_wo", "ffn_w_up", "ffn_w_down", "pc_w_in", "pc_w_out", "mla_w_ukv")}
    wb["mla_wdq"] = jnp.stack([_pad_wdq(full["mla_w_dq_dkv"][o]) for o in range(DEPTH // 2)])
    wb["mla_wuq"] = jnp.stack([_pad_wuq(full["mla_w_uq"][o]) for o in range(DEPTH // 2)])
    wb["mla_wo"] = jnp.stack([_pad_wo(full["mla_w_o"][o]) for o in range(DEPTH // 2)])

    loss, grad_x, grads = _local_step(x[0], mem[0], positions[0], loss_target[0], wb, ws)
    loss = lax.psum(loss[0, 0], ("x", "y", "c"))

    kinds = [k for _, k in BIG]
    big = [grads[n] for n, _ in BIG]
    c_idx = cc.reshape(1).astype(jnp.int32)
    chip_idx = chip.reshape(1).astype(jnp.int32)
    theirs = _exchange_halves(big, "reduce_pair")
    pair = [_add_half(gr, th, c_idx, f"reduce_pair_add_{n}") for gr, th, (n, _) in zip(big, theirs, BIG)]
    slots = _scatter_to_chips(pair, kinds, "reduce_chips")
    halves = [_sum_chips(sl, pr, chip_idx, kind, f"reduce_chips_add_{n}")
              for sl, pr, (n, kind) in zip(slots, pair, BIG)]
    others = _swap_halves(halves, "reduce_join")
    gsum, delta, new_m, new_v = {}, {}, {}, {}
    for mine, other, (n, _) in zip(halves, others, BIG):
        gsum[n], delta[n], new_m[n], new_v[n] = _adamw_halves(w[n], mine, other, m[n], v[n], c_idx, f"adamw_{n}")

    small_names = SMALL_REPLICATED + SMALL_SHARDED
    small_grad_shapes = [grads[n].shape for n in small_names]
    packed = _pack([grads[n] for n in small_names])
    rows = packed.shape[0]
    allparts = _all_gather_rows(packed, "gather_small_grads").reshape(8, rows, LANE)
    total = _sum_leading(allparts, "sum_small_grads").reshape(-1)
    for n, gfull in zip(small_names, _unpack(total, small_grad_shapes)):
        if n in SMALL_SHARDED:
            width = w[n].shape[-1]
            gfull = lax.dynamic_slice_in_dim(gfull, chip * width, width, axis=gfull.ndim - 1)
        gsum[n] = gfull

    for n in SMALL_REPLICATED + SMALL_SHARDED:
        delta[n], new_m[n], new_v[n] = _adamw(w[n], gsum[n], m[n], v[n], f"adamw_{n}")
    return (loss, grad_x[None], *[gsum[n] for n in WEIGHTS], *[delta[n] for n in WEIGHTS],
            *[new_m[n] for n in WEIGHTS], *[new_v[n] for n in WEIGHTS])
```

```python
import functools
import math

import numpy as np
import jax
import jax.numpy as jnp
from jax import lax
from jax.experimental import pallas as pl
from jax.experimental.pallas import tpu as pltpu

F32 = jnp.float32
MXU_DT = jnp.bfloat16
XFER_DT = jnp.bfloat16

D_MODEL = 1024
DEPTH = 4
MEM_LEN = 256
XA_HEADS = 4
XA_HEAD_DIM = 256
POOL_W = 512
POOL_WINDOWS = (2, 4, 8, 16)
POOL_GROUP = 128
CONV_W = 512
CONV_K = 31
MLA_HEADS = 16
QK_NOPE = 64
QK_ROPE = 32
V_HEAD = 64
Q_LORA = 384
KV_LORA = 256
ROPE_THETA = 10000.0
MLA_SCALE = 1.0 / math.sqrt(QK_NOPE + QK_ROPE)
LOG2E = math.log2(math.e)
D_FF = 2816
FFN_CONV_K = 3
EPS = 1e-6
NEG = -1e30
ADAM_LR = 0.001
ADAM_B1 = 0.9
ADAM_B2 = 0.999
ADAM_EPS = 1e-08
ADAM_WD = 0.01
ADAM_STEP = 10

HEAD_PAD = 128
C_PAD = 768
KPE_LANE = 64

VMEM_LIMIT = 52 * 1024 * 1024
BLOCK_BYTES = 6 * 1024 * 1024
LANE = 128
SUBLANE = 8

TM = 1024
TN = 1408
TK = 2048
TT = 512
TW = 256
TWF = 128
FFN_CHUNK = 256
TA = 1024
MIX_HALO = 32
FFN_HALO = 8

NN = (((1,), (0,)), ((), ()))
NT = (((1,), (1,)), ((), ()))
TN_DIMS = (((0,), (0,)), ((), ()))
MESH_ID = pl.DeviceIdType.MESH


def _cparams(*sem):
    return pltpu.CompilerParams(dimension_semantics=sem, vmem_limit_bytes=VMEM_LIMIT)


def _tile(n, pref, limit=None):
    cap = pref if limit is None else min(pref, limit)
    if n <= cap:
        return n
    t = (cap // LANE) * LANE
    while t >= LANE:
        if n % t == 0:
            return t
        t -= LANE
    return n


def _rows(t, pref):
    return t if t <= pref else pref


def _sigmoid(x):
    return 1.0 / (1.0 + jnp.exp(-x))


def _matmul(a, b, mode, out_dtype, name, layer=None, res=None, stack=None):
    if layer is None:
        b2 = b.shape
    else:
        b2 = b.shape[1:]
    if mode == "tn":
        k, m = a.shape
        k2, n = b2
    elif mode == "nn":
        m, k = a.shape
        k2, n = b2
    else:
        m, k = a.shape
        n, k2 = b2
    assert k == k2, (a.shape, b.shape, mode)
    isz_a = jnp.dtype(a.dtype).itemsize
    isz_b = jnp.dtype(b.dtype).itemsize
    if mode == "tn":
        tk = _tile(k, TK * 2 // max(isz_a, isz_b))
        tm = _tile(m, TN, BLOCK_BYTES // (tk * isz_a))
        tn = _tile(n, TN, BLOCK_BYTES // (tk * isz_b))
    else:
        tk = k
        tn = _tile(n, TN, BLOCK_BYTES // (tk * isz_b))
        tm = _tile(m, TM, min(BLOCK_BYTES // (tk * isz_a), BLOCK_BYTES // (tn * 4)))
    nk = k // tk
    grid = (m // tm, n // tn, nk)
    if mode == "nn":
        a_spec = pl.BlockSpec((tm, tk), lambda i, j, kk: (i, kk))
        b_blk, b_map, dn = (tk, tn), (lambda i, j, kk: (kk, j)), NN
    elif mode == "nt":
        a_spec = pl.BlockSpec((tm, tk), lambda i, j, kk: (i, kk))
        b_blk, b_map, dn = (tn, tk), (lambda i, j, kk: (j, kk)), NT
    else:
        a_spec = pl.BlockSpec((tk, tm), lambda i, j, kk: (kk, i))
        b_blk, b_map, dn = (tk, tn), (lambda i, j, kk: (kk, j)), TN_DIMS
    if layer is None:
        b_spec = pl.BlockSpec(b_blk, b_map)
    else:
        b_spec = pl.BlockSpec((None,) + b_blk, lambda i, j, kk: (layer,) + b_map(i, j, kk))
    in_specs = [a_spec, b_spec]
    args = [a, b]
    if res is not None:
        in_specs.append(pl.BlockSpec((tm, tn), lambda i, j, kk: (i, j)))
        args.append(res)
    has_res = res is not None
    aliases = {}
    if stack is None:
        o_spec = pl.BlockSpec((tm, tn), lambda i, j, kk: (i, j))
        out_shape = jax.ShapeDtypeStruct((m, n), out_dtype)
    else:
        buf, slab, nslab = stack
        o_spec = pl.BlockSpec((None, tm, tn), lambda i, j, kk: (slab, i, j))
        out_shape = jax.ShapeDtypeStruct((nslab, m, n), out_dtype)
        if buf is not None:
            in_specs.append(pl.BlockSpec(memory_space=pl.ANY))
            args.append(buf)
            aliases = {len(args) - 1: 0}
    n_in = len(args)

    def body(*refs):
        a_ref, b_ref = refs[0], refs[1]
        r_ref = refs[2] if has_res else None
        o_ref = refs[n_in]
        p = lax.dot_general(a_ref[...].astype(MXU_DT), b_ref[...].astype(MXU_DT), dn, preferred_element_type=F32)
        if nk == 1:
            if has_res:
                p = r_ref[...] + p
            o_ref[...] = p.astype(o_ref.dtype)
        else:
            acc_ref = refs[-1]
            kk = pl.program_id(2)

            @pl.when(kk == 0)
            def _():
                acc_ref[...] = jnp.zeros_like(acc_ref)

            acc_ref[...] += p

            @pl.when(kk == nk - 1)
            def _():
                r = acc_ref[...]
                if has_res:
                    r = r_ref[...] + r
                o_ref[...] = r.astype(o_ref.dtype)

    scratch = [pltpu.VMEM((tm, tn), F32)] if nk > 1 else []
    return pl.pallas_call(
        body,
        grid=grid,
        in_specs=in_specs,
        out_specs=o_spec,
        out_shape=out_shape,
        scratch_shapes=scratch,
        input_output_aliases=aliases,
        name=name,
        compiler_params=_cparams("parallel", "parallel", "arbitrary"),
    )(*args)


def _rms_fwd(x, g, name):
    t, d = x.shape
    tt = _rows(t, TT)

    def body(x_ref, g_ref, o_ref):
        xf = x_ref[...]
        r = lax.rsqrt(jnp.mean(xf * xf, axis=-1, keepdims=True) + EPS)
        o_ref[...] = ((xf * r) * g_ref[...]).astype(o_ref.dtype)

    return pl.pallas_call(
        body,
        grid=(t // tt,),
        in_specs=[pl.BlockSpec((tt, d), lambda i: (i, 0)), pl.BlockSpec((1, d), lambda i: (0, 0))],
        out_specs=pl.BlockSpec((tt, d), lambda i: (i, 0)),
        out_shape=jax.ShapeDtypeStruct((t, d), MXU_DT),
        name=name,
        compiler_params=_cparams("parallel"),
    )(x, g.reshape(1, d))


def _rms_bwd(dh, x, g, dx_in, name):
    t, d = x.shape
    tt = _rows(t, TT)

    def body(dh_ref, x_ref, g_ref, dxi_ref, dx_ref, dg_ref):
        @pl.when(pl.program_id(0) == 0)
        def _():
            dg_ref[...] = jnp.zeros_like(dg_ref)

        xf = x_ref[...]
        dh_v = dh_ref[...]
        r = lax.rsqrt(jnp.mean(xf * xf, axis=-1, keepdims=True) + EPS)
        xh = xf * r
        gy = dh_v * g_ref[...]
        dx = r * (gy - xh * jnp.mean(gy * xh, axis=-1, keepdims=True))
        dx_ref[...] = dxi_ref[...] + dx
        dg_ref[...] += jnp.sum(dh_v * xh, axis=0, keepdims=True)

    row = pl.BlockSpec((tt, d), lambda i: (i, 0))
    vec = pl.BlockSpec((1, d), lambda i: (0, 0))
    return pl.pallas_call(
        body,
        grid=(t // tt,),
        in_specs=[row, row, vec, row],
        out_specs=[row, vec],
        out_shape=[jax.ShapeDtypeStruct((t, d), F32), jax.ShapeDtypeStruct((1, d), F32)],
        name=name,
        compiler_params=_cparams("arbitrary"),
    )(dh, x, g.reshape(1, d), dx_in)


def _rms_bwd_gain(dh, x, g, name):
    t, d = x.shape
    tt = _rows(t, TT)

    def body(dh_ref, x_ref, dg_ref):
        @pl.when(pl.program_id(0) == 0)
        def _():
            dg_ref[...] = jnp.zeros_like(dg_ref)

        xf = x_ref[...]
        r = lax.rsqrt(jnp.mean(xf * xf, axis=-1, keepdims=True) + EPS)
        dg_ref[...] += jnp.sum(dh_ref[...] * (xf * r), axis=0, keepdims=True)

    row = pl.BlockSpec((tt, d), lambda i: (i, 0))
    vec = pl.BlockSpec((1, d), lambda i: (0, 0))
    return pl.pallas_call(
        body,
        grid=(t // tt,),
        in_specs=[row, row],
        out_specs=vec,
        out_shape=jax.ShapeDtypeStruct((1, d), F32),
        name=name,
        compiler_params=_cparams("arbitrary"),
    )(dh, x)


def _loss_head(x, target, g, name):
    t, d = x.shape
    tt = _rows(t, TT)

    def body(x_ref, t_ref, g_ref, dx_ref, dg_ref, loss_ref):
        @pl.when(pl.program_id(0) == 0)
        def _():
            dg_ref[...] = jnp.zeros_like(dg_ref)
            loss_ref[...] = jnp.zeros_like(loss_ref)

        xf = x_ref[...]
        gv = g_ref[...]
        r = lax.rsqrt(jnp.mean(xf * xf, axis=-1, keepdims=True) + EPS)
        xh = xf * r
        err = xh * gv - t_ref[...]
        e2 = jnp.sum(err * err, axis=-1, keepdims=True)
        loss_ref[...] += (0.5 / d) * jnp.sum(e2, axis=0, keepdims=True)
        dy = err * (1.0 / d)
        gy = dy * gv
        dx_ref[...] = r * (gy - xh * jnp.mean(gy * xh, axis=-1, keepdims=True))
        dg_ref[...] += jnp.sum(dy * xh, axis=0, keepdims=True)

    row = pl.BlockSpec((tt, d), lambda i: (i, 0))
    vec = pl.BlockSpec((1, d), lambda i: (0, 0))
    return pl.pallas_call(
        body,
        grid=(t // tt,),
        in_specs=[row, row, vec],
        out_specs=[row, vec, pl.BlockSpec((1, 1), lambda i: (0, 0))],
        out_shape=[
            jax.ShapeDtypeStruct((t, d), F32),
            jax.ShapeDtypeStruct((1, d), F32),
            jax.ShapeDtypeStruct((1, 1), F32),
        ],
        name=name,
        compiler_params=_cparams("arbitrary"),
    )(x, target, g.reshape(1, d))


def _prev_halo(tt, hp, width):
    return pl.BlockSpec((hp, width), lambda i: (jnp.maximum(i * (tt // hp) - 1, 0), 0))


def _next_halo(tt, hp, width, t):
    return pl.BlockSpec((hp, width), lambda i: (jnp.minimum((i + 1) * (tt // hp), t // hp - 1), 0))


def _ffn_chunks():
    return [(c0, FFN_CHUNK) for c0 in range(0, D_FF, FFN_CHUNK)]


def _ffn_fwd(up, conv_w, conv_b, name):
    t = up.shape[0]
    tt = _rows(t, TWF)
    hp = FFN_HALO

    def body(up_ref, gp_ref, w_ref, b_ref, act_ref, ext_ref):
        first = pl.program_id(0) == 0
        for c0, cw in _ffn_chunks():
            ga = pl.ds(D_FF + c0, cw)
            ext_ref[0:hp, :] = jnp.where(first, 0.0, gp_ref[:, ga])
            ext_ref[hp : hp + tt, :] = up_ref[:, ga]
            gc = b_ref[:, pl.ds(c0, cw)]
            for j in range(FFN_CONV_K):
                off = hp - (FFN_CONV_K - 1) + j
                gc = gc + w_ref[j : j + 1, pl.ds(c0, cw)] * ext_ref[off : off + tt, :]
            a = up_ref[:, pl.ds(c0, cw)]
            act_ref[:, pl.ds(c0, cw)] = (gc * _sigmoid(gc) * a).astype(act_ref.dtype)

    return pl.pallas_call(
        body,
        grid=(t // tt,),
        in_specs=[
            pl.BlockSpec((tt, 2 * D_FF), lambda i: (i, 0)),
            _prev_halo(tt, hp, 2 * D_FF),
            pl.BlockSpec((FFN_CONV_K, D_FF), lambda i: (0, 0)),
            pl.BlockSpec((1, D_FF), lambda i: (0, 0)),
        ],
        out_specs=pl.BlockSpec((tt, D_FF), lambda i: (i, 0)),
        out_shape=jax.ShapeDtypeStruct((t, D_FF), MXU_DT),
        scratch_shapes=[pltpu.VMEM((tt + hp, FFN_CHUNK), F32)],
        name=name,
        compiler_params=_cparams("parallel"),
    )(up, up, conv_w, conv_b.reshape(1, D_FF))


def _ffn_bwd(up, dact, conv_w, conv_b, name):
    t = up.shape[0]
    tt = _rows(t, TWF)
    hp = FFN_HALO
    nt = t // tt
    kk = FFN_CONV_K

    def body(up_ref, upp_ref, upn_ref, da_ref, dan_ref, w_ref, b_ref, dup_ref, dw_ref, db_ref, ext_ref, dgc_ref):
        i = pl.program_id(0)
        first = i == 0
        last = i == nt - 1

        @pl.when(first)
        def _():
            dw_ref[...] = jnp.zeros_like(dw_ref)
            db_ref[...] = jnp.zeros_like(db_ref)

        for c0, cw in _ffn_chunks():
            ca = pl.ds(c0, cw)
            ga = pl.ds(D_FF + c0, cw)
            ext_ref[0:hp, :] = jnp.where(first, 0.0, upp_ref[:, ga])
            ext_ref[hp : hp + tt, :] = up_ref[:, ga]
            ext_ref[hp + tt : hp + tt + hp, :] = upn_ref[:, ga]
            gc = b_ref[:, ca]
            for j in range(kk):
                off = hp - (kk - 1) + j
                gc = gc + w_ref[j : j + 1, ca] * ext_ref[off : off + tt + hp, :]
            sg = _sigmoid(gc)
            silu = gc * sg
            dsilu = sg * (1.0 + gc * (1.0 - sg))
            a_all = jnp.concatenate([up_ref[:, ca], upn_ref[:, ca]], axis=0)
            dact_all = jnp.concatenate([da_ref[:, ca], jnp.where(last, 0.0, dan_ref[:, ca])], axis=0)
            dgc = dact_all * a_all * dsilu
            dgc_ref[...] = dgc
            dup_ref[:, ca] = (dact_all[0:tt] * silu[0:tt]).astype(dup_ref.dtype)
            g_t = up_ref[:, ga]
            dg = None
            for j in range(kk):
                tap = dgc_ref[kk - 1 - j : kk - 1 - j + tt, :]
                dg = w_ref[j : j + 1, ca] * tap if dg is None else dg + w_ref[j : j + 1, ca] * tap
                dw_ref[j : j + 1, ca] += jnp.sum(tap * g_t, axis=0, keepdims=True)
            dup_ref[:, ga] = dg.astype(dup_ref.dtype)
            db_ref[:, ca] += jnp.sum(dgc[0:tt], axis=0, keepdims=True)

    return pl.pallas_call(
        body,
        grid=(nt,),
        in_specs=[
            pl.BlockSpec((tt, 2 * D_FF), lambda i: (i, 0)),
            _prev_halo(tt, hp, 2 * D_FF),
            _next_halo(tt, hp, 2 * D_FF, t),
            pl.BlockSpec((tt, D_FF), lambda i: (i, 0)),
            _next_halo(tt, hp, D_FF, t),
            pl.BlockSpec((kk, D_FF), lambda i: (0, 0)),
            pl.BlockSpec((1, D_FF), lambda i: (0, 0)),
        ],
        out_specs=[
            pl.BlockSpec((tt, 2 * D_FF), lambda i: (i, 0)),
            pl.BlockSpec((kk, D_FF), lambda i: (0, 0)),
            pl.BlockSpec((1, D_FF), lambda i: (0, 0)),
        ],
        out_shape=[
            jax.ShapeDtypeStruct((t, 2 * D_FF), MXU_DT),
            jax.ShapeDtypeStruct((kk, D_FF), F32),
            jax.ShapeDtypeStruct((1, D_FF), F32),
        ],
        scratch_shapes=[pltpu.VMEM((tt + 2 * hp, FFN_CHUNK), F32), pltpu.VMEM((tt + hp, FFN_CHUNK), F32)],
        name=name,
        compiler_params=_cparams("arbitrary"),
    )(up, up, up, dact, dact, conv_w, conv_b.reshape(1, D_FF))


def _layernorm_silu(cv, ln_g, ln_b):
    mu = jnp.mean(cv, axis=-1, keepdims=True)
    xc = cv - mu
    rstd = lax.rsqrt(jnp.mean(xc * xc, axis=-1, keepdims=True) + EPS)
    xh = xc * rstd
    a = xh * ln_g + ln_b
    return xh, rstd, a


def _shifted_copies(ref, n):
    for b in range(1, SUBLANE):
        ref[b, 0 : n - SUBLANE, :] = ref[0, b : b + n - SUBLANE, :]


def _tap(ref, offset, rows, cols):
    b = offset % SUBLANE
    return ref[b, offset - b : offset - b + rows, cols]


def _mix_fwd(z, pool_w, pool_scale, dw_w, dw_b, ln_g, ln_b, name):
    t = z.shape[0]
    tt = _rows(t, TW)
    hp = MIX_HALO
    zw = POOL_W + 2 * CONV_W

    def body(z_ref, zp_ref, pw_ref, ps_ref, w_ref, b_ref, lg_ref, lb_ref, cat_ref, eu_ref, egl_ref, cv_ref):
        i = pl.program_id(0)
        first = i == 0
        eu_ref[0:hp, :] = jnp.where(first, 0.0, zp_ref[:, 0:POOL_W])
        eu_ref[hp : hp + tt, :] = z_ref[:, 0:POOL_W]
        glp = zp_ref[:, POOL_W : POOL_W + CONV_W] * _sigmoid(zp_ref[:, POOL_W + CONV_W : zw])
        egl_ref[0, 0:hp, :] = jnp.where(first, 0.0, glp)
        egl_ref[0, hp : hp + tt, :] = z_ref[:, POOL_W : POOL_W + CONV_W] * _sigmoid(z_ref[:, POOL_W + CONV_W : zw])
        _shifted_copies(egl_ref, tt + hp)
        row = i * tt + lax.broadcasted_iota(jnp.int32, (tt, 1), 0)
        for gi, w in enumerate(POOL_WINDOWS):
            cols = pl.ds(gi * POOL_GROUP, POOL_GROUP)
            u = eu_ref[hp : hp + tt, cols]
            acc = u
            for k in range(1, w):
                acc = acc + eu_ref[hp - k : hp - k + tt, cols]
            cnt = jnp.minimum(row + 1, w).astype(F32)
            pooled = acc / cnt - u
            y = jnp.dot(pooled.astype(MXU_DT), pw_ref[gi].astype(MXU_DT), preferred_element_type=F32)
            cat_ref[:, cols] = (y * ps_ref[:, cols]).astype(cat_ref.dtype)
        for c0 in range(0, CONV_W, LANE):
            cs = pl.ds(c0, LANE)
            acc = jnp.broadcast_to(b_ref[:, cs], (tt, LANE))
            for j in range(CONV_K):
                acc = acc + w_ref[j : j + 1, cs] * _tap(egl_ref, hp - (CONV_K - 1) + j, tt, cs)
            cv_ref[:, cs] = acc
        _, _, a = _layernorm_silu(cv_ref[...], lg_ref[...], lb_ref[...])
        cat_ref[:, POOL_W : POOL_W + CONV_W] = (a * _sigmoid(a)).astype(cat_ref.dtype)

    vec = pl.BlockSpec((1, CONV_W), lambda i: (0, 0))
    return pl.pallas_call(
        body,
        grid=(t // tt,),
        in_specs=[
            pl.BlockSpec((tt, zw), lambda i: (i, 0)),
            _prev_halo(tt, hp, zw),
            pl.BlockSpec((len(POOL_WINDOWS), POOL_GROUP, POOL_GROUP), lambda i: (0, 0, 0)),
            vec,
            pl.BlockSpec((CONV_K, CONV_W), lambda i: (0, 0)),
            vec,
            vec,
            vec,
        ],
        out_specs=pl.BlockSpec((tt, POOL_W + CONV_W), lambda i: (i, 0)),
        out_shape=jax.ShapeDtypeStruct((t, POOL_W + CONV_W), MXU_DT),
        scratch_shapes=[pltpu.VMEM((tt + hp, POOL_W), F32), pltpu.VMEM((SUBLANE, tt + hp, CONV_W), F32),
                        pltpu.VMEM((tt, CONV_W), F32)],
        name=name,
        compiler_params=_cparams("parallel"),
    )(z, z, pool_w, pool_scale.reshape(1, POOL_W), dw_w, dw_b.reshape(1, CONV_W), ln_g.reshape(1, CONV_W), ln_b.reshape(1, CONV_W))


def _mix_bwd(z, dcat, pool_w, pool_scale, dw_w, dw_b, ln_g, ln_b, name):
    t = z.shape[0]
    tt = _rows(t, TW)
    hp = MIX_HALO
    nt = t // tt
    zw = POOL_W + 2 * CONV_W
    ng = len(POOL_WINDOWS)

    def body(z_ref, zp_ref, zn_ref, dc_ref, dcn_ref, pw_ref, ps_ref, w_ref, b_ref, lg_ref, lb_ref,
             dz_ref, dpw_ref, dps_ref, dww_ref, dwb_ref, dlg_ref, dlb_ref, eu_ref, ee_ref, egl_ref, edcv_ref, cv_ref):
        i = pl.program_id(0)
        first = i == 0
        last = i == nt - 1

        @pl.when(first)
        def _():
            for r in (dpw_ref, dps_ref, dww_ref, dwb_ref, dlg_ref, dlb_ref):
                r[...] = jnp.zeros_like(r)

        eu_ref[0:hp, :] = jnp.where(first, 0.0, zp_ref[:, 0:POOL_W])
        eu_ref[hp : hp + tt, :] = z_ref[:, 0:POOL_W]
        row = i * tt + lax.broadcasted_iota(jnp.int32, (tt, 1), 0)
        row_ext = i * tt + lax.broadcasted_iota(jnp.int32, (tt + hp, 1), 0)
        for gi, w in enumerate(POOL_WINDOWS):
            cols = pl.ds(gi * POOL_GROUP, POOL_GROUP)
            u = eu_ref[hp : hp + tt, cols]
            acc = u
            for k in range(1, w):
                acc = acc + eu_ref[hp - k : hp - k + tt, cols]
            pooled = (acc / jnp.minimum(row + 1, w).astype(F32) - u).astype(MXU_DT)
            pw = pw_ref[gi].astype(MXU_DT)
            dya = dc_ref[:, cols]
            y = jnp.dot(pooled, pw, preferred_element_type=F32)
            dps_ref[:, cols] += jnp.sum(dya * y, axis=0, keepdims=True)
            scale = ps_ref[:, cols]
            dy_all = jnp.concatenate([dya, jnp.where(last, 0.0, dcn_ref[:, cols])], axis=0) * scale
            dy_all = dy_all.astype(MXU_DT)
            dpw_ref[gi] += lax.dot_general(pooled, dy_all[0:tt], TN_DIMS, preferred_element_type=F32)
            dpooled = lax.dot_general(dy_all, pw, NT, preferred_element_type=F32)
            ee_ref[:, cols] = dpooled / jnp.minimum(row_ext + 1, w).astype(F32)
            du = -dpooled[0:tt]
            for k in range(w):
                du = du + ee_ref[k : k + tt, cols]
            dz_ref[:, cols] = du.astype(dz_ref.dtype)

        ca = slice(POOL_W, POOL_W + CONV_W)
        cb = slice(POOL_W + CONV_W, zw)
        egl_ref[0, 0:hp, :] = jnp.where(first, 0.0, zp_ref[:, ca] * _sigmoid(zp_ref[:, cb]))
        egl_ref[0, hp : hp + tt, :] = z_ref[:, ca] * _sigmoid(z_ref[:, cb])
        egl_ref[0, hp + tt : hp + tt + hp, :] = zn_ref[:, ca] * _sigmoid(zn_ref[:, cb])
        _shifted_copies(egl_ref, tt + 2 * hp)
        for c0 in range(0, CONV_W, LANE):
            cs = pl.ds(c0, LANE)
            acc = jnp.broadcast_to(b_ref[:, cs], (tt + hp, LANE))
            for j in range(CONV_K):
                acc = acc + w_ref[j : j + 1, cs] * _tap(egl_ref, hp - (CONV_K - 1) + j, tt + hp, cs)
            cv_ref[:, cs] = acc
        lg = lg_ref[...]
        xh, rstd, a = _layernorm_silu(cv_ref[...], lg, lb_ref[...])
        sa = _sigmoid(a)
        dyb = jnp.concatenate([dc_ref[:, ca], jnp.where(last, 0.0, dcn_ref[:, ca])], axis=0)
        da = dyb * (sa * (1.0 + a * (1.0 - sa)))
        dlg_ref[...] += jnp.sum(da[0:tt] * xh[0:tt], axis=0, keepdims=True)
        dlb_ref[...] += jnp.sum(da[0:tt], axis=0, keepdims=True)
        dxh = da * lg
        dcv = rstd * (dxh - jnp.mean(dxh, axis=-1, keepdims=True) - xh * jnp.mean(dxh * xh, axis=-1, keepdims=True))
        edcv_ref[0] = dcv
        _shifted_copies(edcv_ref, tt + hp)
        dwb_ref[...] += jnp.sum(dcv[0:tt], axis=0, keepdims=True)
        for c0 in range(0, CONV_W, LANE):
            cs = pl.ds(c0, LANE)
            gl_t = egl_ref[0, hp : hp + tt, cs]
            dgl = jnp.zeros((tt, LANE), F32)
            for j in range(CONV_K):
                tap = _tap(edcv_ref, CONV_K - 1 - j, tt, cs)
                dww_ref[j : j + 1, cs] += jnp.sum(gl_t * tap, axis=0, keepdims=True)
                dgl = dgl + w_ref[j : j + 1, cs] * tap
            ga = z_ref[:, pl.ds(POOL_W + c0, LANE)]
            sgb = _sigmoid(z_ref[:, pl.ds(POOL_W + CONV_W + c0, LANE)])
            dz_ref[:, pl.ds(POOL_W + c0, LANE)] = (dgl * sgb).astype(dz_ref.dtype)
            dz_ref[:, pl.ds(POOL_W + CONV_W + c0, LANE)] = (dgl * ga * sgb * (1.0 - sgb)).astype(dz_ref.dtype)

    vec = pl.BlockSpec((1, CONV_W), lambda i: (0, 0))
    pw_spec = pl.BlockSpec((ng, POOL_GROUP, POOL_GROUP), lambda i: (0, 0, 0))
    w_spec = pl.BlockSpec((CONV_K, CONV_W), lambda i: (0, 0))
    return pl.pallas_call(
        body,
        grid=(nt,),
        in_specs=[
            pl.BlockSpec((tt, zw), lambda i: (i, 0)),
            _prev_halo(tt, hp, zw),
            _next_halo(tt, hp, zw, t),
            pl.BlockSpec((tt, POOL_W + CONV_W), lambda i: (i, 0)),
            _next_halo(tt, hp, POOL_W + CONV_W, t),
            pw_spec, vec, w_spec, vec, vec, vec,
        ],
        out_specs=[pl.BlockSpec((tt, zw), lambda i: (i, 0)), pw_spec, vec, w_spec, vec, vec, vec],
        out_shape=[
            jax.ShapeDtypeStruct((t, zw), MXU_DT),
            jax.ShapeDtypeStruct((ng, POOL_GROUP, POOL_GROUP), F32),
            jax.ShapeDtypeStruct((1, POOL_W), F32),
            jax.ShapeDtypeStruct((CONV_K, CONV_W), F32),
            jax.ShapeDtypeStruct((1, CONV_W), F32),
            jax.ShapeDtypeStruct((1, CONV_W), F32),
            jax.ShapeDtypeStruct((1, CONV_W), F32),
        ],
        scratch_shapes=[
            pltpu.VMEM((tt + hp, POOL_W), F32),
            pltpu.VMEM((tt + hp, POOL_W), F32),
            pltpu.VMEM((SUBLANE, tt + 2 * hp, CONV_W), F32),
            pltpu.VMEM((SUBLANE, tt + hp, CONV_W), F32),
            pltpu.VMEM((tt + hp, CONV_W), F32),
        ],
        name=name,
        compiler_params=_cparams("arbitrary"),
    )(z, z, z, dcat, dcat, pool_w, pool_scale.reshape(1, POOL_W), dw_w, dw_b.reshape(1, CONV_W),
      ln_g.reshape(1, CONV_W), ln_b.reshape(1, CONV_W))


def _xa_fwd(q, kvm, name):
    t = q.shape[0]
    tt = _rows(t, TT)
    scale = XA_HEAD_DIM ** -0.5

    def body(q_ref, kv_ref, o_ref):
        for h in range(XA_HEADS):
            cs = pl.ds(h * XA_HEAD_DIM, XA_HEAD_DIM)
            vs = pl.ds(D_MODEL + h * XA_HEAD_DIM, XA_HEAD_DIM)
            s = lax.dot_general(q_ref[:, cs], kv_ref[:, cs], NT, preferred_element_type=F32) * scale
            p = jnp.exp(s - jnp.max(s, axis=-1, keepdims=True))
            p = p / jnp.sum(p, axis=-1, keepdims=True)
            o_ref[:, cs] = jnp.dot(p.astype(MXU_DT), kv_ref[:, vs], preferred_element_type=F32).astype(o_ref.dtype)

    return pl.pallas_call(
        body,
        grid=(t // tt,),
        in_specs=[pl.BlockSpec((tt, D_MODEL), lambda i: (i, 0)), pl.BlockSpec((MEM_LEN, 2 * D_MODEL), lambda i: (0, 0))],
        out_specs=pl.BlockSpec((tt, D_MODEL), lambda i: (i, 0)),
        out_shape=jax.ShapeDtypeStruct((t, D_MODEL), MXU_DT),
        name=name,
        compiler_params=_cparams("parallel"),
    )(q, kvm)


def _xa_bwd(q, kvm, do, name):
    t = q.shape[0]
    tt = _rows(t, TT)
    scale = XA_HEAD_DIM ** -0.5

    def body(q_ref, kv_ref, do_ref, dq_ref, dkv_ref):
        @pl.when(pl.program_id(0) == 0)
        def _():
            dkv_ref[...] = jnp.zeros_like(dkv_ref)

        for h in range(XA_HEADS):
            cs = pl.ds(h * XA_HEAD_DIM, XA_HEAD_DIM)
            vs = pl.ds(D_MODEL + h * XA_HEAD_DIM, XA_HEAD_DIM)
            qh = q_ref[:, cs]
            kh = kv_ref[:, cs]
            doh = do_ref[:, cs]
            s = lax.dot_general(qh, kh, NT, preferred_element_type=F32) * scale
            p = jnp.exp(s - jnp.max(s, axis=-1, keepdims=True))
            p = p / jnp.sum(p, axis=-1, keepdims=True)
            dp = lax.dot_general(doh, kv_ref[:, vs], NT, preferred_element_type=F32)
            ds = (p * (dp - jnp.sum(p * dp, axis=-1, keepdims=True)) * scale).astype(MXU_DT)
            dq_ref[:, cs] = jnp.dot(ds, kh, preferred_element_type=F32).astype(dq_ref.dtype)
            dkv_ref[:, cs] += lax.dot_general(ds, qh, TN_DIMS, preferred_element_type=F32)
            dkv_ref[:, vs] += lax.dot_general(p.astype(MXU_DT), doh, TN_DIMS, preferred_element_type=F32)

    row = pl.BlockSpec((tt, D_MODEL), lambda i: (i, 0))
    kvs = pl.BlockSpec((MEM_LEN, 2 * D_MODEL), lambda i: (0, 0))
    return pl.pallas_call(
        body,
        grid=(t // tt,),
        in_specs=[row, kvs, row],
        out_specs=[row, kvs],
        out_shape=[jax.ShapeDtypeStruct((t, D_MODEL), MXU_DT), jax.ShapeDtypeStruct((MEM_LEN, 2 * D_MODEL), F32)],
        name=name,
        compiler_params=_cparams("arbitrary"),
    )(q, kvm, do)


def _rope_tables(positions, name):
    t = positions.shape[0]
    tt = _rows(t, TT)
    inv = 1.0 / (ROPE_THETA ** (np.arange(0, QK_ROPE, 2, dtype=np.float32) / QK_ROPE))
    lanes = np.zeros((1, HEAD_PAD), np.float32)
    half = QK_ROPE // 2
    lanes[0, KPE_LANE : KPE_LANE + half] = inv
    lanes[0, KPE_LANE + half : KPE_LANE + QK_ROPE] = inv

    def body(pos_ref, inv_ref, cos_ref, sa_ref, sb_ref):
        ang = pos_ref[...].astype(F32) * inv_ref[...]
        lane = lax.broadcasted_iota(jnp.int32, (tt, HEAD_PAD), 1)
        c = jnp.cos(ang)
        s = jnp.sin(ang)
        lo = (lane >= KPE_LANE) & (lane < KPE_LANE + half)
        hi = (lane >= KPE_LANE + half) & (lane < KPE_LANE + QK_ROPE)
        cos_ref[...] = jnp.where(lo | hi, c, 1.0)
        sa_ref[...] = jnp.where(hi, s, 0.0)
        sb_ref[...] = jnp.where(lo, -s, 0.0)

    tab = pl.BlockSpec((tt, HEAD_PAD), lambda i: (i, 0))
    return pl.pallas_call(
        body,
        grid=(t // tt,),
        in_specs=[pl.BlockSpec((tt, 1), lambda i: (i, 0)), pl.BlockSpec((1, HEAD_PAD), lambda i: (0, 0))],
        out_specs=[tab, tab, tab],
        out_shape=[jax.ShapeDtypeStruct((t, HEAD_PAD), F32)] * 3,
        name=name,
        compiler_params=_cparams("parallel"),
    )(positions, jnp.asarray(lanes))


def _rotate(x, cos, sa, sb, sign):
    half = QK_ROPE // 2
    return x * cos + sign * (pltpu.roll(x, half, 1) * sa + pltpu.roll(x, HEAD_PAD - half, 1) * sb)


def _rope_heads(x, tables, sign, scale, name):
    t, w = x.shape
    tt = _rows(t, TT)
    nh = w // HEAD_PAD

    def body(x_ref, c_ref, sa_ref, sb_ref, o_ref):
        cos, sa, sb = c_ref[...] * scale, sa_ref[...] * scale, sb_ref[...] * scale
        for h in range(nh):
            cs = pl.ds(h * HEAD_PAD, HEAD_PAD)
            o_ref[:, cs] = _rotate(x_ref[:, cs], cos, sa, sb, sign).astype(o_ref.dtype)

    tab = pl.BlockSpec((tt, HEAD_PAD), lambda i: (i, 0))
    row = pl.BlockSpec((tt, w), lambda i: (i, 0))
    return pl.pallas_call(
        body,
        grid=(t // tt,),
        in_specs=[row, tab, tab, tab],
        out_specs=row,
        out_shape=jax.ShapeDtypeStruct((t, w), MXU_DT),
        name=name,
        compiler_params=_cparams("parallel"),
    )(x, *tables)


def _mla_prep(cp, qg, kvg, tables, name):
    t = cp.shape[0]
    tt = _rows(t, TT)

    def body(cp_ref, qg_ref, kvg_ref, c_ref, sa_ref, sb_ref, qn_ref, kvn_ref, kpe_ref):
        cq = cp_ref[:, 0:Q_LORA]
        r = lax.rsqrt(jnp.mean(cq * cq, axis=-1, keepdims=True) + EPS)
        qn_ref[...] = ((cq * r) * qg_ref[...]).astype(qn_ref.dtype)
        ckv = cp_ref[:, Q_LORA : Q_LORA + KV_LORA]
        r = lax.rsqrt(jnp.mean(ckv * ckv, axis=-1, keepdims=True) + EPS)
        kvn_ref[...] = ((ckv * r) * kvg_ref[...]).astype(kvn_ref.dtype)
        kpe = cp_ref[:, Q_LORA + KV_LORA : C_PAD]
        kpe_ref[...] = _rotate(kpe, c_ref[...], sa_ref[...], sb_ref[...], 1.0).astype(kpe_ref.dtype)

    tab = pl.BlockSpec((tt, HEAD_PAD), lambda i: (i, 0))
    return pl.pallas_call(
        body,
        grid=(t // tt,),
        in_specs=[
            pl.BlockSpec((tt, C_PAD), lambda i: (i, 0)),
            pl.BlockSpec((1, Q_LORA), lambda i: (0, 0)),
            pl.BlockSpec((1, KV_LORA), lambda i: (0, 0)),
            tab, tab, tab,
        ],
        out_specs=[
            pl.BlockSpec((tt, Q_LORA), lambda i: (i, 0)),
            pl.BlockSpec((tt, KV_LORA), lambda i: (i, 0)),
            tab,
        ],
        out_shape=[
            jax.ShapeDtypeStruct((t, Q_LORA), MXU_DT),
            jax.ShapeDtypeStruct((t, KV_LORA), MXU_DT),
            jax.ShapeDtypeStruct((t, HEAD_PAD), MXU_DT),
        ],
        name=name,
        compiler_params=_cparams("parallel"),
    )(cp, qg.reshape(1, Q_LORA), kvg.reshape(1, KV_LORA), *tables)


def _mla_prep_bwd(cp, dqn, dkvn, dkpe_heads, qg, kvg, tables, name):
    t = cp.shape[0]
    tt = _rows(t, TT)

    def norm_bwd(x, dy, g):
        r = lax.rsqrt(jnp.mean(x * x, axis=-1, keepdims=True) + EPS)
        xh = x * r
        gy = dy * g
        return r * (gy - xh * jnp.mean(gy * xh, axis=-1, keepdims=True)), jnp.sum(dy * xh, axis=0, keepdims=True)

    def body(cp_ref, dqn_ref, dkvn_ref, dkpe_ref, qg_ref, kvg_ref, c_ref, sa_ref, sb_ref, dcp_ref, dqg_ref, dkvg_ref):
        @pl.when(pl.program_id(0) == 0)
        def _():
            dqg_ref[...] = jnp.zeros_like(dqg_ref)
            dkvg_ref[...] = jnp.zeros_like(dkvg_ref)

        dcq, dg = norm_bwd(cp_ref[:, 0:Q_LORA], dqn_ref[...], qg_ref[...])
        dcp_ref[:, 0:Q_LORA] = dcq.astype(dcp_ref.dtype)
        dqg_ref[...] += dg
        dckv, dg = norm_bwd(cp_ref[:, Q_LORA : Q_LORA + KV_LORA], dkvn_ref[...], kvg_ref[...])
        dcp_ref[:, Q_LORA : Q_LORA + KV_LORA] = dckv.astype(dcp_ref.dtype)
        dkvg_ref[...] += dg
        dk = dkpe_ref[0]
        for h in range(1, MLA_HEADS):
            dk = dk + dkpe_ref[h]
        dcp_ref[:, Q_LORA + KV_LORA : C_PAD] = _rotate(dk, c_ref[...], sa_ref[...], sb_ref[...], -1.0).astype(dcp_ref.dtype)

    tab = pl.BlockSpec((tt, HEAD_PAD), lambda i: (i, 0))
    return pl.pallas_call(
        body,
        grid=(t // tt,),
        in_specs=[
            pl.BlockSpec((tt, C_PAD), lambda i: (i, 0)),
            pl.BlockSpec((tt, Q_LORA), lambda i: (i, 0)),
            pl.BlockSpec((tt, KV_LORA), lambda i: (i, 0)),
            pl.BlockSpec((MLA_HEADS, tt, HEAD_PAD), lambda i: (0, i, 0)),
            pl.BlockSpec((1, Q_LORA), lambda i: (0, 0)),
            pl.BlockSpec((1, KV_LORA), lambda i: (0, 0)),
            tab, tab, tab,
        ],
        out_specs=[
            pl.BlockSpec((tt, C_PAD), lambda i: (i, 0)),
            pl.BlockSpec((1, Q_LORA), lambda i: (0, 0)),
            pl.BlockSpec((1, KV_LORA), lambda i: (0, 0)),
        ],
        out_shape=[
            jax.ShapeDtypeStruct((t, C_PAD), MXU_DT),
            jax.ShapeDtypeStruct((1, Q_LORA), F32),
            jax.ShapeDtypeStruct((1, KV_LORA), F32),
        ],
        name=name,
        compiler_params=_cparams("arbitrary"),
    )(cp, dqn, dkvn, dkpe_heads, qg.reshape(1, Q_LORA), kvg.reshape(1, KV_LORA), *tables)


def _flash_fwd(qs, kv, kpe, name):
    t = qs.shape[0]
    ta = _rows(t, TA)
    tq = ta
    nq = t // tq
    sub = ta // 2

    def body(q_ref, kv_ref, kpe_ref, o_ref, lse_ref):
        qi = pl.program_id(1)
        q = q_ref[...]
        lane = lax.broadcasted_iota(jnp.int32, (ta, HEAD_PAD), 1)

        def kblock(j):
            rows = pl.ds(pl.multiple_of(j * ta, ta), ta)
            kvb = kv_ref[rows, :]
            ones_v = jnp.where(lane < QK_NOPE, jnp.ones_like(kvb), kvb)
            return ones_v, jnp.where(lane < QK_NOPE, kvb, kpe_ref[rows, :])

        def update(carry, s, ones_v):
            m, acc = carry
            m_new = jnp.maximum(m, jnp.max(s, axis=-1, keepdims=True))
            p = jnp.exp2(s - m_new).astype(MXU_DT)
            acc = jnp.exp2(m - m_new) * acc + jnp.dot(p, ones_v, preferred_element_type=F32)
            return m_new, acc

        def step(j, carry):
            ones_v, k = kblock(j)
            return update(carry, lax.dot_general(q, k, NT, preferred_element_type=F32), ones_v)

        init = (jnp.full((tq, 1), -jnp.inf, F32), jnp.zeros((tq, HEAD_PAD), F32))
        m_all, acc_all = lax.fori_loop(0, qi, step, init)
        ones_v, k = kblock(qi)
        lane_h = lax.broadcasted_iota(jnp.int32, (sub, HEAD_PAD), 1)
        for b in range(2):
            rows, nk = slice(b * sub, (b + 1) * sub), (b + 1) * sub
            s = lax.dot_general(q[rows], k[0:nk], NT, preferred_element_type=F32)
            r = lax.broadcasted_iota(jnp.int32, (sub, nk), 0) + b * sub
            c = lax.broadcasted_iota(jnp.int32, (sub, nk), 1)
            m, acc = update((m_all[rows], acc_all[rows]), jnp.where(c <= r, s, NEG), ones_v[0:nk])
            l = acc[:, 0:1]
            o_ref[rows, :] = jnp.where(lane_h >= QK_NOPE, acc / l, 0.0).astype(o_ref.dtype)
            lse_ref[rows, :] = m + jnp.log2(l)

    return pl.pallas_call(
        body,
        grid=(MLA_HEADS, nq),
        in_specs=[
            pl.BlockSpec((tq, HEAD_PAD), lambda h, i: (i, h)),
            pl.BlockSpec((t, HEAD_PAD), lambda h, i: (0, h)),
            pl.BlockSpec((t, HEAD_PAD), lambda h, i: (0, 0)),
        ],
        out_specs=[
            pl.BlockSpec((tq, HEAD_PAD), lambda h, i: (i, h)),
            pl.BlockSpec((None, tq, 1), lambda h, i: (h, i, 0)),
        ],
        out_shape=[
            jax.ShapeDtypeStruct((t, MLA_HEADS * HEAD_PAD), MXU_DT),
            jax.ShapeDtypeStruct((MLA_HEADS, t, 1), F32),
        ],
        name=name,
        compiler_params=_cparams("parallel", "parallel"),
    )(qs, kv, kpe)


def _flash_bwd(qs, kv, kpe, o, do, lse, name):
    t = qs.shape[0]
    ta = _rows(t, TA)
    tq = ta
    nq = t // ta
    sub = ta // 2

    def body(q_ref, o_ref, do_ref, lse_ref, kv_ref, kpe_ref, dq_ref, dkv_ref, dkpe_ref, dk_acc, dv_acc):
        kj = pl.program_id(1)

        @pl.when(kj == 0)
        def _():
            dq_ref[...] = jnp.zeros_like(dq_ref)

        lane = lax.broadcasted_iota(jnp.int32, (ta, HEAD_PAD), 1)
        kvb = kv_ref[...]
        k = jnp.where(lane < QK_NOPE, kvb, kpe_ref[...])
        dk_acc[...] = jnp.zeros_like(dk_acc)
        dv_acc[...] = jnp.zeros_like(dv_acc)

        def tile(row0, nrows, nkeys, diagonal):
            rows = pl.ds(pl.multiple_of(row0, sub), nrows)
            keys = slice(0, nkeys)
            q = q_ref[rows, :]
            dob = do_ref[rows, :]
            delta = jnp.sum(dob.astype(F32) * o_ref[rows, :].astype(F32), axis=-1, keepdims=True)
            s = lax.dot_general(q, k[keys], NT, preferred_element_type=F32)
            if diagonal:
                r = lax.broadcasted_iota(jnp.int32, (nrows, nkeys), 0) + (nkeys - nrows)
                c = lax.broadcasted_iota(jnp.int32, (nrows, nkeys), 1)
                s = jnp.where(c <= r, s, NEG)
            p = jnp.exp2(s - lse_ref[rows, :])
            dp = lax.dot_general(dob, kvb[keys], NT, preferred_element_type=F32)
            ds = (p * (dp - delta)).astype(MXU_DT)
            dq_ref[rows, :] += jnp.dot(ds, k[keys], preferred_element_type=F32)
            dk_acc[keys, :] += lax.dot_general(ds, q, TN_DIMS, preferred_element_type=F32)
            dv_acc[keys, :] += lax.dot_general(p.astype(MXU_DT), dob, TN_DIMS, preferred_element_type=F32)

        tile(kj * ta, sub, sub, True)
        tile(kj * ta + sub, sub, ta, True)

        def step(qq, carry):
            tile(qq * tq, tq, ta, False)
            return carry

        lax.fori_loop(kj + 1, t // tq, step, 0)
        dk = dk_acc[...] * (1.0 / LOG2E)
        dkv_ref[...] = jnp.where(lane < QK_NOPE, dk, dv_acc[...]).astype(dkv_ref.dtype)
        dkpe_ref[...] = jnp.where((lane >= KPE_LANE) & (lane < KPE_LANE + QK_ROPE), dk, 0.0)

    head_rows = pl.BlockSpec((t, HEAD_PAD), lambda h, j: (0, h))
    return pl.pallas_call(
        body,
        grid=(MLA_HEADS, nq),
        in_specs=[
            head_rows,
            head_rows,
            head_rows,
            pl.BlockSpec((None, t, 1), lambda h, j: (h, 0, 0)),
            pl.BlockSpec((ta, HEAD_PAD), lambda h, j: (j, h)),
            pl.BlockSpec((ta, HEAD_PAD), lambda h, j: (j, 0)),
        ],
        out_specs=[
            head_rows,
            pl.BlockSpec((ta, HEAD_PAD), lambda h, j: (j, h)),
            pl.BlockSpec((None, ta, HEAD_PAD), lambda h, j: (h, j, 0)),
        ],
        out_shape=[
            jax.ShapeDtypeStruct((t, MLA_HEADS * HEAD_PAD), F32),
            jax.ShapeDtypeStruct((t, MLA_HEADS * HEAD_PAD), MXU_DT),
            jax.ShapeDtypeStruct((MLA_HEADS, t, HEAD_PAD), F32),
        ],
        scratch_shapes=[pltpu.VMEM((ta, HEAD_PAD), F32), pltpu.VMEM((ta, HEAD_PAD), F32)],
        name=name,
        compiler_params=_cparams("parallel", "arbitrary"),
    )(qs, o, do, lse, kv, kpe)


def _as2d(a):
    if a.ndim == 1:
        return a.reshape(1, a.shape[0])
    return a.reshape(-1, a.shape[-1])


def _adamw(w, g, m, v, name):
    shape = w.shape
    w2, g2, m2, v2 = (_as2d(a) for a in (w, g, m, v))
    r, c = w2.shape
    tr = _tile_rows(r, c)
    c1 = 1.0 - ADAM_B1 ** ADAM_STEP
    c2 = 1.0 - ADAM_B2 ** ADAM_STEP

    def body(w_ref, g_ref, m_ref, v_ref, d_ref, nm_ref, nv_ref):
        gv = g_ref[...]
        nm = ADAM_B1 * m_ref[...] + (1.0 - ADAM_B1) * gv
        nv = ADAM_B2 * v_ref[...] + (1.0 - ADAM_B2) * (gv * gv)
        d_ref[...] = -ADAM_LR * ((nm / c1) / (jnp.sqrt(nv / c2) + ADAM_EPS) + ADAM_WD * w_ref[...])
        nm_ref[...] = nm
        nv_ref[...] = nv

    blk = pl.BlockSpec((tr, c), lambda i: (i, 0))
    outs = pl.pallas_call(
        body,
        grid=(r // tr,),
        in_specs=[blk] * 4,
        out_specs=[blk] * 3,
        out_shape=[jax.ShapeDtypeStruct((r, c), F32)] * 3,
        name=name,
        compiler_params=_cparams("parallel"),
    )(w2, g2, m2, v2)
    return tuple(o.reshape(shape) for o in outs)


def _adamw_halves(w, mine, other, m, v, c_idx, name):
    nl, r, c = w.shape
    h = nl // 2
    tr = _tile_rows(r, 2 * c)
    c1 = 1.0 - ADAM_B1 ** ADAM_STEP
    c2 = 1.0 - ADAM_B2 ** ADAM_STEP

    def body(c_ref, w_ref, a_ref, b_ref, m_ref, v_ref, g_ref, d_ref, nm_ref, nv_ref):
        l = pl.program_id(0)
        gv = jnp.where(l // h == c_ref[0], a_ref[...], b_ref[...])
        nm = ADAM_B1 * m_ref[...] + (1.0 - ADAM_B1) * gv
        nv = ADAM_B2 * v_ref[...] + (1.0 - ADAM_B2) * (gv * gv)
        g_ref[...] = gv
        d_ref[...] = -ADAM_LR * ((nm / c1) / (jnp.sqrt(nv / c2) + ADAM_EPS) + ADAM_WD * w_ref[...])
        nm_ref[...] = nm
        nv_ref[...] = nv

    def half_map(mine_side):
        def index(l, i, cr):
            first = cr[0] * h if mine_side else (1 - cr[0]) * h
            return (jnp.clip(l - first, 0, h - 1), i, 0)
        return index

    full = pl.BlockSpec((None, tr, c), lambda l, i, cr: (l, i, 0))
    grid_spec = pltpu.PrefetchScalarGridSpec(
        num_scalar_prefetch=1,
        grid=(nl, r // tr),
        in_specs=[full, pl.BlockSpec((None, tr, c), half_map(True)), pl.BlockSpec((None, tr, c), half_map(False)), full, full],
        out_specs=[full] * 4,
    )
    return pl.pallas_call(
        body,
        grid_spec=grid_spec,
        out_shape=[jax.ShapeDtypeStruct((nl, r, c), F32)] * 4,
        name=name,
        compiler_params=_cparams("parallel", "parallel"),
    )(c_idx, w, mine, other, m, v)


def _tile_rows(r, c, mult=SUBLANE):
    limit = max(mult, (BLOCK_BYTES // 4) // (4 * c))
    if r <= limit:
        return r
    t = (limit // mult) * mult
    while t >= mult:
        if r % t == 0:
            return t
        t -= mult
    return r


def _sum_leading(a, name):
    n, r, c = a.shape
    tr = _tile_rows(r, c * n)

    def body(a_ref, o_ref):
        s = a_ref[0]
        for k in range(1, n):
            s = s + a_ref[k]
        o_ref[...] = s

    return pl.pallas_call(
        body,
        grid=(r // tr,),
        in_specs=[pl.BlockSpec((n, tr, c), lambda i: (0, i, 0))],
        out_specs=pl.BlockSpec((tr, c), lambda i: (i, 0)),
        out_shape=jax.ShapeDtypeStruct((r, c), F32),
        name=name,
        compiler_params=_cparams("parallel"),
    )(a)


def _add_half(g, s, c_idx, name):
    nl, r, c = g.shape
    h = nl // 2
    tr = _tile_rows(r, 2 * c, 2 * SUBLANE)

    def body(c_ref, g_ref, s_ref, o_ref):
        o_ref[...] = (g_ref[...] + s_ref[...]).astype(o_ref.dtype)

    grid_spec = pltpu.PrefetchScalarGridSpec(
        num_scalar_prefetch=1,
        grid=(h, r // tr),
        in_specs=[
            pl.BlockSpec((None, tr, c), lambda l, i, cr: (cr[0] * h + l, i, 0)),
            pl.BlockSpec((None, tr, c), lambda l, i, cr: (l, i, 0)),
        ],
        out_specs=pl.BlockSpec((None, tr, c), lambda l, i, cr: (l, i, 0)),
    )
    return pl.pallas_call(
        body,
        grid_spec=grid_spec,
        out_shape=jax.ShapeDtypeStruct((h, r, c), XFER_DT),
        name=name,
        compiler_params=_cparams("parallel", "parallel"),
    )(c_idx, g, s)


def _sum_chips(slots, pair, chip_idx, kind, name):
    _, h, r, c = slots.shape
    tr = _tile_rows(r, 5 * c, 2 * SUBLANE)
    nr = r // tr

    def body(chip_ref, s_ref, own_ref, o_ref):
        chip = chip_ref[0]
        own = own_ref[...].astype(F32)
        parts = [s_ref[j].astype(F32) for j in range(3)]
        total = None
        for k in range(4):
            d = jnp.bitwise_xor(chip, k)
            v = jnp.where(d == 0, own, jnp.where(d == 2, parts[0], jnp.where(d == 1, parts[1], parts[2])))
            total = v if total is None else total + v
        o_ref[...] = total

    if kind == "row":
        own_spec = pl.BlockSpec((None, tr, c), lambda l, i, cr: (l, cr[0] * nr + i, 0))
    else:
        own_spec = pl.BlockSpec((None, tr, c), lambda l, i, cr: (l, i, cr[0]))
    grid_spec = pltpu.PrefetchScalarGridSpec(
        num_scalar_prefetch=1,
        grid=(h, nr),
        in_specs=[pl.BlockSpec((3, None, tr, c), lambda l, i, cr: (0, l, i, 0)), own_spec],
        out_specs=pl.BlockSpec((None, tr, c), lambda l, i, cr: (l, i, 0)),
    )
    return pl.pallas_call(
        body,
        grid_spec=grid_spec,
        out_shape=jax.ShapeDtypeStruct((h, r, c), F32),
        name=name,
        compiler_params=_cparams("parallel", "parallel"),
    )(chip_idx, slots, pair)


def _mesh_pos():
    return lax.axis_index("x"), lax.axis_index("y"), lax.axis_index("c")


def _other_chips(x, y):
    return [(1 - x, y), (x, 1 - y), (1 - x, 1 - y)]


def _all_gather_rows(block, name):
    m_per, n = block.shape

    def body(x_ref, out_ref, send_sems, recv_sems, local_sem):
        x, y, c = _mesh_pos()
        me, sibling = (x, y, c), (x, y, 1 - c)
        chips = _other_chips(x, y)

        def rows(px, py, pc):
            return out_ref.at[pl.ds((4 * px + 2 * py + pc) * m_per, m_per), :]

        def copy(k, blk, to, src=None):
            return pltpu.make_async_remote_copy(
                src_ref=rows(*blk) if src is None else src,
                dst_ref=rows(*blk),
                send_sem=send_sems.at[k],
                recv_sem=recv_sems.at[k],
                device_id=to,
                device_id_type=MESH_ID,
            )

        mine = pltpu.make_async_copy(x_ref, rows(*me), local_sem)
        mine.start()
        first = [copy(0, me, sibling, src=x_ref)]
        first += [copy(1 + j, me, (*chip, c), src=x_ref) for j, chip in enumerate(chips)]
        for cp in first:
            cp.start()
        passed = [copy(4 + j, (*chip, c), sibling) for j, chip in enumerate(chips)]
        for j, chip in enumerate(chips):
            copy(1 + j, (*chip, c), me).wait_recv()
            passed[j].start()
        copy(0, sibling, me).wait_recv()
        for j, chip in enumerate(chips):
            copy(4 + j, (*chip, 1 - c), me).wait_recv()
        for cp in first + passed:
            cp.wait_send()
        mine.wait()

    return pl.pallas_call(
        body,
        out_shape=jax.ShapeDtypeStruct((8 * m_per, n), block.dtype),
        in_specs=[pl.BlockSpec(memory_space=pltpu.VMEM)],
        out_specs=pl.BlockSpec(memory_space=pltpu.VMEM),
        scratch_shapes=[pltpu.SemaphoreType.DMA((7,)), pltpu.SemaphoreType.DMA((7,)), pltpu.SemaphoreType.DMA],
        name=name,
        compiler_params=pltpu.CompilerParams(vmem_limit_bytes=VMEM_LIMIT),
    )(block)


def _shard_window(ref, layers, chip, rows, cols):
    if rows is not None:
        return ref.at[layers, pl.ds(pl.multiple_of(chip * rows, rows), rows), :]
    return ref.at[layers, :, pl.ds(pl.multiple_of(chip * cols, cols), cols)]


def _all_gather_weights(shards, kinds, name):
    nw = len(shards)
    out_shapes = []
    for s, kind in zip(shards, kinds):
        nl, r, c = s.shape
        full = (nl, 4 * r, c) if kind == "row" else (nl, r, 4 * c)
        out_shapes.append(jax.ShapeDtypeStruct(full, s.dtype))

    def body(*refs):
        ins, outs = refs[:nw], refs[nw : 2 * nw]
        send_sems, recv_sems, in_sems, out_sems = refs[2 * nw : 2 * nw + 4]
        bufs = refs[2 * nw + 4 :]
        x, y, c = _mesh_pos()
        sibling = (x, y, 1 - c)
        chips = _other_chips(x, y)
        my_chip = 2 * x + y

        def window(w, chip, layers):
            _, r, cc = shards[w].shape
            if kinds[w] == "row":
                return _shard_window(outs[w], layers, chip, r, None)
            return _shard_window(outs[w], layers, chip, None, cc)

        def half(w, half_idx):
            h = shards[w].shape[0] // 2
            return pl.ds(half_idx * h, h)

        def copy(w, k, src, dst, to):
            return pltpu.make_async_remote_copy(
                src_ref=src, dst_ref=dst, send_sem=send_sems.at[w, k], recv_sem=recv_sems.at[w, k],
                device_id=to, device_id_type=MESH_ID)

        sent = []
        for w in range(nw):
            mine = ins[w].at[half(w, c)]
            for j, chip in enumerate(chips):
                cp = copy(w, j, mine, window(w, my_chip, half(w, c)), (*chip, c))
                cp.start()
                sent.append(cp)
        for w in range(nw):
            nl = shards[w].shape[0]

            def load(l, w=w):
                return pltpu.make_async_copy(ins[w].at[l], bufs[w].at[l % 2], in_sems.at[w, l % 2])

            def store(l, w=w):
                return pltpu.make_async_copy(bufs[w].at[l % 2], window(w, my_chip, l), out_sems.at[w, l % 2])

            load(0).start()
            for l in range(nl):
                load(l).wait()
                store(l).start()
                if l + 1 < nl:
                    if l >= 1:
                        store(l - 1).wait()
                    load(l + 1).start()
            for l in range(max(nl - 2, 0), nl):
                store(l).wait()
        for w in range(nw):
            for j, (cx, cy) in enumerate(chips):
                got = window(w, 2 * cx + cy, half(w, c))
                copy(w, j, got, got, (cx, cy, c)).wait_recv()
                cp = copy(w, 3 + j, got, got, sibling)
                cp.start()
                sent.append(cp)
        for w in range(nw):
            for j, (cx, cy) in enumerate(chips):
                got = window(w, 2 * cx + cy, half(w, 1 - c))
                copy(w, 3 + j, got, got, sibling).wait_recv()
        for cp in sent:
            cp.wait_send()

    anyspec = pl.BlockSpec(memory_space=pl.ANY)
    return pl.pallas_call(
        body,
        out_shape=out_shapes,
        in_specs=[anyspec] * nw,
        out_specs=[anyspec] * nw,
        scratch_shapes=[pltpu.SemaphoreType.DMA((nw, 6)), pltpu.SemaphoreType.DMA((nw, 6)),
                        pltpu.SemaphoreType.DMA((nw, 2)), pltpu.SemaphoreType.DMA((nw, 2))]
        + [pltpu.VMEM((2,) + s.shape[1:], s.dtype) for s in shards],
        name=name,
        compiler_params=pltpu.CompilerParams(vmem_limit_bytes=VMEM_LIMIT),
    )(*shards)


def _exchange_halves(grads, name):
    nw = len(grads)
    out_shapes = [jax.ShapeDtypeStruct((g.shape[0] // 2,) + g.shape[1:], g.dtype) for g in grads]

    def body(*refs):
        ins, outs = refs[:nw], refs[nw : 2 * nw]
        send_sems, recv_sems = refs[2 * nw :]
        x, y, c = _mesh_pos()
        cps = []
        for w in range(nw):
            h = grads[w].shape[0] // 2
            cp = pltpu.make_async_remote_copy(
                src_ref=ins[w].at[pl.ds((1 - c) * h, h)], dst_ref=outs[w], send_sem=send_sems.at[w],
                recv_sem=recv_sems.at[w], device_id=(x, y, 1 - c), device_id_type=MESH_ID)
            cp.start()
            cps.append(cp)
        for cp in cps:
            cp.wait()

    anyspec = pl.BlockSpec(memory_space=pl.ANY)
    return pl.pallas_call(
        body,
        out_shape=out_shapes,
        in_specs=[anyspec] * nw,
        out_specs=[anyspec] * nw,
        scratch_shapes=[pltpu.SemaphoreType.DMA((nw,)), pltpu.SemaphoreType.DMA((nw,))],
        name=name,
    )(*grads)


def _scatter_to_chips(parts, kinds, name):
    nw = len(parts)
    shard_shapes = []
    for p, kind in zip(parts, kinds):
        h, r, c = p.shape
        shard_shapes.append((h, r // 4, c) if kind == "row" else (h, r, c // 4))
    out_shapes = [jax.ShapeDtypeStruct((3,) + s, p.dtype) for s, p in zip(shard_shapes, parts)]

    def body(*refs):
        ins, outs = refs[:nw], refs[nw : 2 * nw]
        send_sems, recv_sems = refs[2 * nw :]
        x, y, c = _mesh_pos()
        chips = _other_chips(x, y)

        def piece(w, chip):
            h, r, cc = shard_shapes[w]
            if kinds[w] == "row":
                return _shard_window(ins[w], pl.ds(0, h), chip, r, None)
            return _shard_window(ins[w], pl.ds(0, h), chip, None, cc)

        def copy(w, j, cx, cy):
            return pltpu.make_async_remote_copy(
                src_ref=piece(w, 2 * cx + cy), dst_ref=outs[w].at[j], send_sem=send_sems.at[w, j],
                recv_sem=recv_sems.at[w, j], device_id=(cx, cy, c), device_id_type=MESH_ID)

        cps = [copy(w, j, cx, cy) for w in range(nw) for j, (cx, cy) in enumerate(chips)]
        for cp in cps:
            cp.start()
        for cp in cps:
            cp.wait()

    anyspec = pl.BlockSpec(memory_space=pl.ANY)
    return pl.pallas_call(
        body,
        out_shape=out_shapes,
        in_specs=[anyspec] * nw,
        out_specs=[anyspec] * nw,
        scratch_shapes=[pltpu.SemaphoreType.DMA((nw, 3)), pltpu.SemaphoreType.DMA((nw, 3))],
        name=name,
    )(*parts)


def _swap_halves(halves, name):
    nw = len(halves)
    out_shapes = [jax.ShapeDtypeStruct(p.shape, p.dtype) for p in halves]

    def body(*refs):
        ins, outs = refs[:nw], refs[nw : 2 * nw]
        send_sems, recv_sems = refs[2 * nw :]
        x, y, c = _mesh_pos()
        cps = [pltpu.make_async_remote_copy(
            src_ref=ins[w], dst_ref=outs[w], send_sem=send_sems.at[w], recv_sem=recv_sems.at[w],
            device_id=(x, y, 1 - c), device_id_type=MESH_ID) for w in range(nw)]
        for cp in cps:
            cp.start()
        for cp in cps:
            cp.wait()

    anyspec = pl.BlockSpec(memory_space=pl.ANY)
    return pl.pallas_call(
        body,
        out_shape=out_shapes,
        in_specs=[anyspec] * nw,
        out_specs=[anyspec] * nw,
        scratch_shapes=[pltpu.SemaphoreType.DMA((nw,)), pltpu.SemaphoreType.DMA((nw,))],
        name=name,
    )(*halves)


def _pad_wdq(w):
    z = lambda n: jnp.zeros((w.shape[0], n), w.dtype)
    base = Q_LORA + KV_LORA
    return jnp.concatenate([w[:, :base], z(KPE_LANE), w[:, base:], z(HEAD_PAD - KPE_LANE - QK_ROPE)], axis=1)


def _unpad_wdq(g):
    base = Q_LORA + KV_LORA
    return jnp.concatenate([g[:, :base], g[:, base + KPE_LANE : base + KPE_LANE + QK_ROPE]], axis=1)


def _pad_wuq(w):
    w3 = w.reshape(Q_LORA, MLA_HEADS, QK_NOPE + QK_ROPE)
    w3 = jnp.pad(w3, ((0, 0), (0, 0), (0, HEAD_PAD - QK_NOPE - QK_ROPE)))
    return w3.reshape(Q_LORA, MLA_HEADS * HEAD_PAD)


def _unpad_wuq(g):
    g3 = g.reshape(Q_LORA, MLA_HEADS, HEAD_PAD)[:, :, : QK_NOPE + QK_ROPE]
    return g3.reshape(Q_LORA, MLA_HEADS * (QK_NOPE + QK_ROPE))


def _pad_wo(w):
    w3 = w.reshape(MLA_HEADS, V_HEAD, D_MODEL)
    w3 = jnp.pad(w3, ((0, 0), (HEAD_PAD - V_HEAD, 0), (0, 0)))
    return w3.reshape(MLA_HEADS * HEAD_PAD, D_MODEL)


def _unpad_wo(g):
    g3 = g.reshape(MLA_HEADS, HEAD_PAD, D_MODEL)[:, HEAD_PAD - V_HEAD :, :]
    return g3.reshape(MLA_HEADS * V_HEAD, D_MODEL)


def _local_step(x, mem, positions, target, wb, ws):
    t = x.shape[0]
    tables = _rope_tables(positions.reshape(t, 1), "rope_tables")
    saved = []
    for l in range(DEPTH):
        s = {"x0": x}
        h1 = _rms_fwd(x, ws["norm_mix_g"][l], f"l{l}_norm_mix")
        s["h1"] = h1
        if l % 2 == 0:
            e = l // 2
            z = _matmul(h1, wb["pc_w_in"], "nn", F32, f"l{l}_pc_in", layer=e)
            cat = _mix_fwd(z, ws["pool_w"][e], ws["pool_scale"][e], ws["conv_dw_w"][e], ws["conv_dw_b"][e],
                           ws["conv_ln_g"][e], ws["conv_ln_b"][e], f"l{l}_mix")
            x = _matmul(cat, wb["pc_w_out"], "nn", F32, f"l{l}_pc_out", layer=e, res=x)
            s.update(z=z, cat=cat)
        else:
            o = l // 2
            cp = _matmul(h1, wb["mla_wdq"], "nn", F32, f"l{l}_mla_dq", layer=o)
            qn, kvn, kpe = _mla_prep(cp, ws["mla_q_norm_g"][o], ws["mla_kv_norm_g"][o], tables, f"l{l}_mla_prep")
            q = _matmul(qn, wb["mla_wuq"], "nn", F32, f"l{l}_mla_uq", layer=o)
            qr = _rope_heads(q, tables, 1.0, MLA_SCALE * LOG2E, f"l{l}_mla_rope")
            kv = _matmul(kvn, wb["mla_w_ukv"], "nn", MXU_DT, f"l{l}_mla_ukv", layer=o)
            att, lse = _flash_fwd(qr, kv, kpe, f"l{l}_mla_attn")
            x = _matmul(att, wb["mla_wo"], "nn", F32, f"l{l}_mla_o", layer=o, res=x)
            s.update(cp=cp, qn=qn, kvn=kvn, kpe=kpe, qr=qr, kv=kv, att=att, lse=lse)
        s["x1"] = x
        h2 = _rms_fwd(x, ws["norm_xa_g"][l], f"l{l}_norm_xa")
        hm = _rms_fwd(mem, ws["norm_mem_g"][l], f"l{l}_norm_mem")
        q2 = _matmul(h2, wb["xa_wq"], "nn", MXU_DT, f"l{l}_xa_q", layer=l)
        kvm = _matmul(hm, wb["xa_wkv"], "nn", MXU_DT, f"l{l}_xa_kv", layer=l)
        o2 = _xa_fwd(q2, kvm, f"l{l}_xa_attn")
        x = _matmul(o2, wb["xa_wo"], "nn", F32, f"l{l}_xa_o", layer=l, res=x)
        s.update(h2=h2, hm=hm, q2=q2, kvm=kvm, o2=o2, x2=x)
        h3 = _rms_fwd(x, ws["norm_ffn_g"][l], f"l{l}_norm_ffn")
        up = _matmul(h3, wb["ffn_w_up"], "nn", F32, f"l{l}_ffn_up", layer=l)
        act = _ffn_fwd(up, ws["ffn_conv_w"][l], ws["ffn_conv_b"][l], f"l{l}_ffn_mid")
        x = _matmul(act, wb["ffn_w_down"], "nn", F32, f"l{l}_ffn_down", layer=l, res=x)
        s.update(h3=h3, up=up, act=act)
        saved.append(s)

    dx, dg_final, loss = _loss_head(x, target, ws["final_norm_g"], "loss_head")
    g = {k: [None] * DEPTH for k in ("norm_mix_g", "norm_xa_g", "norm_mem_g", "xa_wq", "xa_wkv", "xa_wo", "norm_ffn_g",
                                      "ffn_w_up", "ffn_conv_w", "ffn_conv_b", "ffn_w_down")}
    g.update({k: [None] * (DEPTH // 2) for k in ("pc_w_in", "pool_w", "pool_scale", "conv_dw_w", "conv_dw_b", "conv_ln_g",
                                                 "conv_ln_b", "pc_w_out", "mla_w_dq_dkv", "mla_q_norm_g", "mla_w_uq",
                                                 "mla_kv_norm_g", "mla_w_ukv", "mla_w_o")})
    stk = {k: None for k in ("xa_wq", "xa_wkv", "xa_wo", "ffn_w_up", "ffn_w_down", "pc_w_in", "pc_w_out")}
    for l in reversed(range(DEPTH)):
        s = saved[l]
        dact = _matmul(dx, wb["ffn_w_down"], "nt", F32, f"l{l}_b_ffn_dact", layer=l)
        stk["ffn_w_down"] = _matmul(s["act"], dx, "tn", F32, f"l{l}_b_ffn_dwdown", stack=(stk["ffn_w_down"], l, DEPTH))
        dup, dcw, dcb = _ffn_bwd(s["up"], dact, ws["ffn_conv_w"][l], ws["ffn_conv_b"][l], f"l{l}_b_ffn_mid")
        g["ffn_conv_w"][l], g["ffn_conv_b"][l] = dcw, dcb[0]
        stk["ffn_w_up"] = _matmul(s["h3"], dup, "tn", F32, f"l{l}_b_ffn_dwup", stack=(stk["ffn_w_up"], l, DEPTH))
        dh = _matmul(dup, wb["ffn_w_up"], "nt", F32, f"l{l}_b_ffn_dh", layer=l)
        dx, dg = _rms_bwd(dh, s["x2"], ws["norm_ffn_g"][l], dx, f"l{l}_b_norm_ffn")
        g["norm_ffn_g"][l] = dg[0]
        do2 = _matmul(dx, wb["xa_wo"], "nt", MXU_DT, f"l{l}_b_xa_do", layer=l)
        stk["xa_wo"] = _matmul(s["o2"], dx, "tn", F32, f"l{l}_b_xa_dwo", stack=(stk["xa_wo"], l, DEPTH))
        dq2, dkvm = _xa_bwd(s["q2"], s["kvm"], do2, f"l{l}_b_xa_attn")
        stk["xa_wq"] = _matmul(s["h2"], dq2, "tn", F32, f"l{l}_b_xa_dwq", stack=(stk["xa_wq"], l, DEPTH))
        dh = _matmul(dq2, wb["xa_wq"], "nt", F32, f"l{l}_b_xa_dh", layer=l)
        stk["xa_wkv"] = _matmul(s["hm"], dkvm, "tn", F32, f"l{l}_b_xa_dwkv", stack=(stk["xa_wkv"], l, DEPTH))
        dhm = _matmul(dkvm, wb["xa_wkv"], "nt", F32, f"l{l}_b_xa_dhm", layer=l)
        g["norm_mem_g"][l] = _rms_bwd_gain(dhm, mem, ws["norm_mem_g"][l], f"l{l}_b_norm_mem")[0]
        dx, dg = _rms_bwd(dh, s["x1"], ws["norm_xa_g"][l], dx, f"l{l}_b_norm_xa")
        g["norm_xa_g"][l] = dg[0]
        if l % 2 == 0:
            e = l // 2
            dcat = _matmul(dx, wb["pc_w_out"], "nt", F32, f"l{l}_b_pc_dcat", layer=e)
            stk["pc_w_out"] = _matmul(s["cat"], dx, "tn", F32, f"l{l}_b_pc_dwout", stack=(stk["pc_w_out"], e, DEPTH // 2))
            dz, dpw, dps, dww, dwb, dlg, dlb = _mix_bwd(
                s["z"], dcat, ws["pool_w"][e], ws["pool_scale"][e], ws["conv_dw_w"][e], ws["conv_dw_b"][e],
                ws["conv_ln_g"][e], ws["conv_ln_b"][e], f"l{l}_b_mix")
            g["pool_w"][e], g["pool_scale"][e], g["conv_dw_w"][e] = dpw, dps[0], dww
            g["conv_dw_b"][e], g["conv_ln_g"][e], g["conv_ln_b"][e] = dwb[0], dlg[0], dlb[0]
            stk["pc_w_in"] = _matmul(s["h1"], dz, "tn", F32, f"l{l}_b_pc_dwin", stack=(stk["pc_w_in"], e, DEPTH // 2))
            dh = _matmul(dz, wb["pc_w_in"], "nt", F32, f"l{l}_b_pc_dh", layer=e)
        else:
            o = l // 2
            do = _matmul(dx, wb["mla_wo"], "nt", MXU_DT, f"l{l}_b_mla_do", layer=o)
            g["mla_w_o"][o] = _unpad_wo(_matmul(s["att"], dx, "tn", F32, f"l{l}_b_mla_dwo"))
            dqr, dkv, dkpe = _flash_bwd(s["qr"], s["kv"], s["kpe"], s["att"], do, s["lse"], f"l{l}_b_mla_attn")
            dq = _rope_heads(dqr, tables, -1.0, MLA_SCALE, f"l{l}_b_mla_rope")
            g["mla_w_uq"][o] = _unpad_wuq(_matmul(s["qn"], dq, "tn", F32, f"l{l}_b_mla_dwuq"))
            dqn = _matmul(dq, wb["mla_wuq"], "nt", F32, f"l{l}_b_mla_dqn", layer=o)
            g["mla_w_ukv"][o] = _matmul(s["kvn"], dkv, "tn", F32, f"l{l}_b_mla_dwukv")
            dkvn = _matmul(dkv, wb["mla_w_ukv"], "nt", F32, f"l{l}_b_mla_dkvn", layer=o)
            dcp, dqg, dkvg = _mla_prep_bwd(s["cp"], dqn, dkvn, dkpe, ws["mla_q_norm_g"][o], ws["mla_kv_norm_g"][o],
                                           tables, f"l{l}_b_mla_prep")
            g["mla_q_norm_g"][o], g["mla_kv_norm_g"][o] = dqg[0], dkvg[0]
            g["mla_w_dq_dkv"][o] = _unpad_wdq(_matmul(s["h1"], dcp, "tn", F32, f"l{l}_b_mla_dwdq"))
            dh = _matmul(dcp, wb["mla_wdq"], "nt", F32, f"l{l}_b_mla_dh", layer=o)
        dx, dg = _rms_bwd(dh, s["x0"], ws["norm_mix_g"][l], dx, f"l{l}_b_norm_mix")
        g["norm_mix_g"][l] = dg[0]
    grads = {k: jnp.stack(v) for k, v in g.items() if k not in stk}
    grads.update(stk)
    grads["final_norm_g"] = dg_final[0]
    return loss, dx, grads


BIG = (
    ("xa_wq", "row"), ("xa_wkv", "col"), ("xa_wo", "row"), ("ffn_w_up", "col"), ("ffn_w_down", "row"),
    ("pc_w_in", "col"), ("pc_w_out", "row"), ("mla_w_dq_dkv", "row"), ("mla_w_uq", "col"), ("mla_w_ukv", "col"),
    ("mla_w_o", "row"),
)
SMALL_SHARDED = ("ffn_conv_w", "conv_dw_w", "mla_q_norm_g", "mla_kv_norm_g")
SMALL_REPLICATED = ("norm_mix_g", "norm_xa_g", "norm_mem_g", "norm_ffn_g", "ffn_conv_b", "pool_w", "pool_scale",
                    "conv_dw_b", "conv_ln_g", "conv_ln_b", "final_norm_g")
WEIGHTS = ("norm_mix_g", "norm_xa_g", "norm_mem_g", "xa_wq", "xa_wkv", "xa_wo", "norm_ffn_g", "ffn_w_up", "ffn_conv_w",
           "ffn_conv_b", "ffn_w_down", "pc_w_in", "pool_w", "pool_scale", "conv_dw_w", "conv_dw_b", "conv_ln_g",
           "conv_ln_b", "pc_w_out", "mla_w_dq_dkv", "mla_q_norm_g", "mla_w_uq", "mla_kv_norm_g", "mla_w_ukv", "mla_w_o",
           "final_norm_g")
PACK_ROW = SUBLANE * LANE


def _pack(arrays):
    flat = jnp.concatenate([a.reshape(-1) for a in arrays])
    n = flat.shape[0]
    pad = (-n) % PACK_ROW
    return jnp.pad(flat, (0, pad)).reshape(-1, LANE)


def _unpack(flat, shapes):
    out, off = [], 0
    for s in shapes:
        n = int(np.prod(s))
        out.append(flat[off : off + n].reshape(s))
        off += n
    return out


def kernel(x, mem, positions, norm_mix_g, norm_xa_g, norm_mem_g, xa_wq, xa_wkv, xa_wo, norm_ffn_g, ffn_w_up, ffn_conv_w, ffn_conv_b, ffn_w_down, pc_w_in, pool_w, pool_scale, conv_dw_w, conv_dw_b, conv_ln_g, conv_ln_b, pc_w_out, mla_w_dq_dkv, mla_q_norm_g, mla_w_uq, mla_kv_norm_g, mla_w_ukv, mla_w_o, final_norm_g, loss_target, m_norm_mix_g, m_norm_xa_g, m_norm_mem_g, m_xa_wq, m_xa_wkv, m_xa_wo, m_norm_ffn_g, m_ffn_w_up, m_ffn_conv_w, m_ffn_conv_b, m_ffn_w_down, m_pc_w_in, m_pool_w, m_pool_scale, m_conv_dw_w, m_conv_dw_b, m_conv_ln_g, m_conv_ln_b, m_pc_w_out, m_mla_w_dq_dkv, m_mla_q_norm_g, m_mla_w_uq, m_mla_kv_norm_g, m_mla_w_ukv, m_mla_w_o, m_final_norm_g, v_norm_mix_g, v_norm_xa_g, v_norm_mem_g, v_xa_wq, v_xa_wkv, v_xa_wo, v_norm_ffn_g, v_ffn_w_up, v_ffn_conv_w, v_ffn_conv_b, v_ffn_w_down, v_pc_w_in, v_pool_w, v_pool_scale, v_conv_dw_w, v_conv_dw_b, v_conv_ln_g, v_conv_ln_b, v_pc_w_out, v_mla_w_dq_dkv, v_mla_q_norm_g, v_mla_w_uq, v_mla_kv_norm_g, v_mla_w_ukv, v_mla_w_o, v_final_norm_g):
    args = dict(locals())
    w = {n: args[n] for n in WEIGHTS}
    m = {n: args["m_" + n] for n in WEIGHTS}
    v = {n: args["v_" + n] for n in WEIGHTS}
    cx, cy, cc = lax.axis_index("x"), lax.axis_index("y"), lax.axis_index("c")
    chip = 2 * cx + cy

    full = _all_gather_weights([w[n].astype(MXU_DT) for n, _ in BIG], [k for _, k in BIG], "gather_weights")
    full = dict(zip([n for n, _ in BIG], full))
    small_shapes = [w[n].shape for n in SMALL_SHARDED]
    gathered = _all_gather_rows(_pack([w[n] for n in SMALL_SHARDED]), "gather_small")
    gathered = gathered.reshape(8, -1)
    ws = {n: w[n] for n in SMALL_REPLICATED}
    pieces = [_unpack(gathered[2 * k], small_shapes) for k in range(4)]
    for i, n in enumerate(SMALL_SHARDED):
        ws[n] = jnp.concatenate([pieces[k][i] for k in range(4)], axis=-1)
    wb = {n: full[n] for n in ("xa_wq", "xa_wkv", "xa_wo", "ffn_w_up", "ffn_w_down", "pc_w_in", "pc_w_out", "mla_w_ukv")}
    wb["mla_wdq"] = jnp.stack([_pad_wdq(full["mla_w_dq_dkv"][o]) for o in range(DEPTH // 2)])
    wb["mla_wuq"] = jnp.stack([_pad_wuq(full["mla_w_uq"][o]) for o in range(DEPTH // 2)])
    wb["mla_wo"] = jnp.stack([_pad_wo(full["mla_w_o"][o]) for o in range(DEPTH // 2)])

    loss, grad_x, grads = _local_step(x[0], mem[0], positions[0], loss_target[0], wb, ws)
    loss = lax.psum(loss[0, 0], ("x", "y", "c"))

    kinds = [k for _, k in BIG]
    big = [grads[n] for n, _ in BIG]
    c_idx = cc.reshape(1).astype(jnp.int32)
    chip_idx = chip.reshape(1).astype(jnp.int32)
    theirs = _exchange_halves(big, "reduce_pair")
    pair = [_add_half(gr, th, c_idx, f"reduce_pair_add_{n}") for gr, th, (n, _) in zip(big, theirs, BIG)]
    slots = _scatter_to_chips(pair, kinds, "reduce_chips")
    halves = [_sum_chips(sl, pr, chip_idx, kind, f"reduce_chips_add_{n}")
              for sl, pr, (n, kind) in zip(slots, pair, BIG)]
    others = _swap_halves(halves, "reduce_join")
    gsum, delta, new_m, new_v = {}, {}, {}, {}
    for mine, other, (n, _) in zip(halves, others, BIG):
        gsum[n], delta[n], new_m[n], new_v[n] = _adamw_halves(w[n], mine, other, m[n], v[n], c_idx, f"adamw_{n}")

    small_names = SMALL_REPLICATED + SMALL_SHARDED
    small_grad_shapes = [grads[n].shape for n in small_names]
    packed = _pack([grads[n] for n in small_names])
    rows = packed.shape[0]
    allparts = _all_gather_rows(packed, "gather_small_grads").reshape(8, rows, LANE)
    total = _sum_leading(allparts, "sum_small_grads").reshape(-1)
    for n, gfull in zip(small_names, _unpack(total, small_grad_shapes)):
        if n in SMALL_SHARDED:
            width = w[n].shape[-1]
            gfull = lax.dynamic_slice_in_dim(gfull, chip * width, width, axis=gfull.ndim - 1)
        gsum[n] = gfull

    for n in SMALL_REPLICATED + SMALL_SHARDED:
        delta[n], new_m[n], new_v[n] = _adamw(w[n], gsum[n], m[n], v[n], f"adamw_{n}")
    return (loss, grad_x[None], *[gsum[n] for n in WEIGHTS], *[delta[n] for n in WEIGHTS],
            *[new_m[n] for n in WEIGHTS], *[new_v[n] for n in WEIGHTS])
```

```python
import functools
import math

import numpy as np
import jax
import jax.numpy as jnp
from jax import lax
from jax.experimental import pallas as pl
from jax.experimental.pallas import tpu as pltpu

F32 = jnp.float32
MXU_DT = jnp.bfloat16
XFER_DT = jnp.bfloat16

D_MODEL = 1024
DEPTH = 4
MEM_LEN = 256
XA_HEADS = 4
XA_HEAD_DIM = 256
POOL_W = 512
POOL_WINDOWS = (2, 4, 8, 16)
POOL_GROUP = 128
CONV_W = 512
CONV_K = 31
MLA_HEADS = 16
QK_NOPE = 64
QK_ROPE = 32
V_HEAD = 64
Q_LORA = 384
KV_LORA = 256
ROPE_THETA = 10000.0
MLA_SCALE = 1.0 / math.sqrt(QK_NOPE + QK_ROPE)
LOG2E = math.log2(math.e)
D_FF = 2816
FFN_CONV_K = 3
EPS = 1e-6
NEG = -1e30
ADAM_LR = 0.001
ADAM_B1 = 0.9
ADAM_B2 = 0.999
ADAM_EPS = 1e-08
ADAM_WD = 0.01
ADAM_STEP = 10

HEAD_PAD = 128
C_PAD = 768
KPE_LANE = 64

VMEM_LIMIT = 52 * 1024 * 1024
BLOCK_BYTES = 6 * 1024 * 1024
LANE = 128
SUBLANE = 8

TM = 1024
TN = 1408
TK = 2048
TT = 512
TW = 256
TWF = 128
FFN_CHUNK = 256
TA = 1024
MIX_HALO = 32
FFN_HALO = 8

NN = (((1,), (0,)), ((), ()))
NT = (((1,), (1,)), ((), ()))
TN_DIMS = (((0,), (0,)), ((), ()))
MESH_ID = pl.DeviceIdType.MESH


def _cparams(*sem):
    return pltpu.CompilerParams(dimension_semantics=sem, vmem_limit_bytes=VMEM_LIMIT)


def _tile(n, pref, limit=None):
    cap = pref if limit is None else min(pref, limit)
    if n <= cap:
        return n
    t = (cap // LANE) * LANE
    while t >= LANE:
        if n % t == 0:
            return t
        t -= LANE
    return n


def _rows(t, pref):
    return t if t <= pref else pref


def _sigmoid(x):
    return 1.0 / (1.0 + jnp.exp(-x))


def _matmul(a, b, mode, out_dtype, name, layer=None, res=None, stack=None):
    if layer is None:
        b2 = b.shape
    else:
        b2 = b.shape[1:]
    if mode == "tn":
        k, m = a.shape
        k2, n = b2
    elif mode == "nn":
        m, k = a.shape
        k2, n = b2
    else:
        m, k = a.shape
        n, k2 = b2
    assert k == k2, (a.shape, b.shape, mode)
    isz_a = jnp.dtype(a.dtype).itemsize
    isz_b = jnp.dtype(b.dtype).itemsize
    if mode == "tn":
        tk = _tile(k, TK * 2 // max(isz_a, isz_b))
        tm = _tile(m, TN, BLOCK_BYTES // (tk * isz_a))
        tn = _tile(n, TN, BLOCK_BYTES // (tk * isz_b))
    else:
        tk = k
        tn = _tile(n, TN, BLOCK_BYTES // (tk * isz_b))
        tm = _tile(m, TM, min(BLOCK_BYTES // (tk * isz_a), BLOCK_BYTES // (tn * 4)))
    nk = k // tk
    grid = (m // tm, n // tn, nk)
    if mode == "nn":
        a_spec = pl.BlockSpec((tm, tk), lambda i, j, kk: (i, kk))
        b_blk, b_map, dn = (tk, tn), (lambda i, j, kk: (kk, j)), NN
    elif mode == "nt":
        a_spec = pl.BlockSpec((tm, tk), lambda i, j, kk: (i, kk))
        b_blk, b_map, dn = (tn, tk), (lambda i, j, kk: (j, kk)), NT
    else:
        a_spec = pl.BlockSpec((tk, tm), lambda i, j, kk: (kk, i))
        b_blk, b_map, dn = (tk, tn), (lambda i, j, kk: (kk, j)), TN_DIMS
    if layer is None:
        b_spec = pl.BlockSpec(b_blk, b_map)
    else:
        b_spec = pl.BlockSpec((None,) + b_blk, lambda i, j, kk: (layer,) + b_map(i, j, kk))
    in_specs = [a_spec, b_spec]
    args = [a, b]
    if res is not None:
        in_specs.append(pl.BlockSpec((tm, tn), lambda i, j, kk: (i, j)))
        args.append(res)
    has_res = res is not None
    aliases = {}
    if stack is None:
        o_spec = pl.BlockSpec((tm, tn), lambda i, j, kk: (i, j))
        out_shape = jax.ShapeDtypeStruct((m, n), out_dtype)
    else:
        buf, slab, nslab = stack
        o_spec = pl.BlockSpec((None, tm, tn), lambda i, j, kk: (slab, i, j))
        out_shape = jax.ShapeDtypeStruct((nslab, m, n), out_dtype)
        if buf is not None:
            in_specs.append(pl.BlockSpec(memory_space=pl.ANY))
            args.append(buf)
            aliases = {len(args) - 1: 0}
    n_in = len(args)

    def body(*refs):
        a_ref, b_ref = refs[0], refs[1]
        r_ref = refs[2] if has_res else None
        o_ref = refs[n_in]
        p = lax.dot_general(a_ref[...].astype(MXU_DT), b_ref[...].astype(MXU_DT), dn, preferred_element_type=F32)
        if nk == 1:
            if has_res:
                p = r_ref[...] + p
            o_ref[...] = p.astype(o_ref.dtype)
        else:
            acc_ref = refs[-1]
            kk = pl.program_id(2)

            @pl.when(kk == 0)
            def _():
                acc_ref[...] = jnp.zeros_like(acc_ref)

            acc_ref[...] += p

            @pl.when(kk == nk - 1)
            def _():
                r = acc_ref[...]
                if has_res:
                    r = r_ref[...] + r
                o_ref[...] = r.astype(o_ref.dtype)

    scratch = [pltpu.VMEM((tm, tn), F32)] if nk > 1 else []
    return pl.pallas_call(
        body,
        grid=grid,
        in_specs=in_specs,
        out_specs=o_spec,
        out_shape=out_shape,
        scratch_shapes=scratch,
        input_output_aliases=aliases,
        name=name,
        compiler_params=_cparams("parallel", "parallel", "arbitrary"),
    )(*args)


def _row_tile(m, k, isz):
    return _tile(m, TM, min(BLOCK_BYTES // (k * isz), TM if k <= 2 * TK else TM // 4))


def _matmul_res_norm(a, b, layer, res, gain, name):
    m, k = a.shape
    n = b.shape[-1]
    tm = _row_tile(m, k, jnp.dtype(a.dtype).itemsize)

    def body(a_ref, b_ref, r_ref, g_ref, x_ref, h_ref):
        x = r_ref[...] + jnp.dot(a_ref[...].astype(MXU_DT), b_ref[...].astype(MXU_DT), preferred_element_type=F32)
        x_ref[...] = x
        r = lax.rsqrt(jnp.mean(x * x, axis=-1, keepdims=True) + EPS)
        h_ref[...] = ((x * r) * g_ref[...]).astype(h_ref.dtype)

    row = pl.BlockSpec((tm, n), lambda i: (i, 0))
    return pl.pallas_call(
        body,
        grid=(m // tm,),
        in_specs=[pl.BlockSpec((tm, k), lambda i: (i, 0)), pl.BlockSpec((None, k, n), lambda i: (layer, 0, 0)), row,
                  pl.BlockSpec((1, n), lambda i: (0, 0))],
        out_specs=[row, row],
        out_shape=[jax.ShapeDtypeStruct((m, n), F32), jax.ShapeDtypeStruct((m, n), MXU_DT)],
        name=name,
        compiler_params=_cparams("parallel"),
    )(a, b, res, gain.reshape(1, n))


def _matmul_rms_bwd(a, b, layer, x, gain, dx_in, name):
    m, k = a.shape
    n = b.shape[-2]
    tm = _row_tile(m, k, jnp.dtype(a.dtype).itemsize)

    def body(a_ref, b_ref, x_ref, g_ref, dxi_ref, dx_ref, dg_ref):
        @pl.when(pl.program_id(0) == 0)
        def _():
            dg_ref[...] = jnp.zeros_like(dg_ref)

        dh = lax.dot_general(a_ref[...].astype(MXU_DT), b_ref[...].astype(MXU_DT), NT, preferred_element_type=F32)
        xf = x_ref[...]
        r = lax.rsqrt(jnp.mean(xf * xf, axis=-1, keepdims=True) + EPS)
        xh = xf * r
        gy = dh * g_ref[...]
        dx_ref[...] = dxi_ref[...] + r * (gy - xh * jnp.mean(gy * xh, axis=-1, keepdims=True))
        dg_ref[...] += jnp.sum(dh * xh, axis=0, keepdims=True)

    row = pl.BlockSpec((tm, n), lambda i: (i, 0))
    vec = pl.BlockSpec((1, n), lambda i: (0, 0))
    return pl.pallas_call(
        body,
        grid=(m // tm,),
        in_specs=[pl.BlockSpec((tm, k), lambda i: (i, 0)), pl.BlockSpec((None, n, k), lambda i: (layer, 0, 0)), row, vec, row],
        out_specs=[row, vec],
        out_shape=[jax.ShapeDtypeStruct((m, n), F32), jax.ShapeDtypeStruct((1, n), F32)],
        name=name,
        compiler_params=_cparams("arbitrary"),
    )(a, b, x, gain.reshape(1, n), dx_in)


def _rms_fwd(x, g, name):
    t, d = x.shape
    tt = _rows(t, TT)

    def body(x_ref, g_ref, o_ref):
        xf = x_ref[...]
        r = lax.rsqrt(jnp.mean(xf * xf, axis=-1, keepdims=True) + EPS)
        o_ref[...] = ((xf * r) * g_ref[...]).astype(o_ref.dtype)

    return pl.pallas_call(
        body,
        grid=(t // tt,),
        in_specs=[pl.BlockSpec((tt, d), lambda i: (i, 0)), pl.BlockSpec((1, d), lambda i: (0, 0))],
        out_specs=pl.BlockSpec((tt, d), lambda i: (i, 0)),
        out_shape=jax.ShapeDtypeStruct((t, d), MXU_DT),
        name=name,
        compiler_params=_cparams("parallel"),
    )(x, g.reshape(1, d))


def _rms_bwd(dh, x, g, dx_in, name):
    t, d = x.shape
    tt = _rows(t, TT)

    def body(dh_ref, x_ref, g_ref, dxi_ref, dx_ref, dg_ref):
        @pl.when(pl.program_id(0) == 0)
        def _():
            dg_ref[...] = jnp.zeros_like(dg_ref)

        xf = x_ref[...]
        dh_v = dh_ref[...]
        r = lax.rsqrt(jnp.mean(xf * xf, axis=-1, keepdims=True) + EPS)
        xh = xf * r
        gy = dh_v * g_ref[...]
        dx = r * (gy - xh * jnp.mean(gy * xh, axis=-1, keepdims=True))
        dx_ref[...] = dxi_ref[...] + dx
        dg_ref[...] += jnp.sum(dh_v * xh, axis=0, keepdims=True)

    row = pl.BlockSpec((tt, d), lambda i: (i, 0))
    vec = pl.BlockSpec((1, d), lambda i: (0, 0))
    return pl.pallas_call(
        body,
        grid=(t // tt,),
        in_specs=[row, row, vec, row],
        out_specs=[row, vec],
        out_shape=[jax.ShapeDtypeStruct((t, d), F32), jax.ShapeDtypeStruct((1, d), F32)],
        name=name,
        compiler_params=_cparams("arbitrary"),
    )(dh, x, g.reshape(1, d), dx_in)


def _rms_bwd_gain(dh, x, g, name):
    t, d = x.shape
    tt = _rows(t, TT)

    def body(dh_ref, x_ref, dg_ref):
        @pl.when(pl.program_id(0) == 0)
        def _():
            dg_ref[...] = jnp.zeros_like(dg_ref)

        xf = x_ref[...]
        r = lax.rsqrt(jnp.mean(xf * xf, axis=-1, keepdims=True) + EPS)
        dg_ref[...] += jnp.sum(dh_ref[...] * (xf * r), axis=0, keepdims=True)

    row = pl.BlockSpec((tt, d), lambda i: (i, 0))
    vec = pl.BlockSpec((1, d), lambda i: (0, 0))
    return pl.pallas_call(
        body,
        grid=(t // tt,),
        in_specs=[row, row],
        out_specs=vec,
        out_shape=jax.ShapeDtypeStruct((1, d), F32),
        name=name,
        compiler_params=_cparams("arbitrary"),
    )(dh, x)


def _loss_head(x, target, g, name):
    t, d = x.shape
    tt = _rows(t, TT)

    def body(x_ref, t_ref, g_ref, dx_ref, dg_ref, loss_ref):
        @pl.when(pl.program_id(0) == 0)
        def _():
            dg_ref[...] = jnp.zeros_like(dg_ref)
            loss_ref[...] = jnp.zeros_like(loss_ref)

        xf = x_ref[...]
        gv = g_ref[...]
        r = lax.rsqrt(jnp.mean(xf * xf, axis=-1, keepdims=True) + EPS)
        xh = xf * r
        err = xh * gv - t_ref[...]
        e2 = jnp.sum(err * err, axis=-1, keepdims=True)
        loss_ref[...] += (0.5 / d) * jnp.sum(e2, axis=0, keepdims=True)
        dy = err * (1.0 / d)
        gy = dy * gv
        dx_ref[...] = r * (gy - xh * jnp.mean(gy * xh, axis=-1, keepdims=True))
        dg_ref[...] += jnp.sum(dy * xh, axis=0, keepdims=True)

    row = pl.BlockSpec((tt, d), lambda i: (i, 0))
    vec = pl.BlockSpec((1, d), lambda i: (0, 0))
    return pl.pallas_call(
        body,
        grid=(t // tt,),
        in_specs=[row, row, vec],
        out_specs=[row, vec, pl.BlockSpec((1, 1), lambda i: (0, 0))],
        out_shape=[
            jax.ShapeDtypeStruct((t, d), F32),
            jax.ShapeDtypeStruct((1, d), F32),
            jax.ShapeDtypeStruct((1, 1), F32),
        ],
        name=name,
        compiler_params=_cparams("arbitrary"),
    )(x, target, g.reshape(1, d))


def _prev_halo(tt, hp, width):
    return pl.BlockSpec((hp, width), lambda i: (jnp.maximum(i * (tt // hp) - 1, 0), 0))


def _next_halo(tt, hp, width, t):
    return pl.BlockSpec((hp, width), lambda i: (jnp.minimum((i + 1) * (tt // hp), t // hp - 1), 0))


def _ffn_chunks():
    return [(c0, FFN_CHUNK) for c0 in range(0, D_FF, FFN_CHUNK)]


def _ffn_fwd(up, conv_w, conv_b, name):
    t = up.shape[0]
    tt = _rows(t, TWF)
    hp = FFN_HALO

    def body(up_ref, gp_ref, w_ref, b_ref, act_ref, ext_ref):
        first = pl.program_id(0) == 0
        for c0, cw in _ffn_chunks():
            ga = pl.ds(D_FF + c0, cw)
            ext_ref[0:hp, :] = jnp.where(first, 0.0, gp_ref[:, ga])
            ext_ref[hp : hp + tt, :] = up_ref[:, ga]
            gc = b_ref[:, pl.ds(c0, cw)]
            for j in range(FFN_CONV_K):
                off = hp - (FFN_CONV_K - 1) + j
                gc = gc + w_ref[j : j + 1, pl.ds(c0, cw)] * ext_ref[off : off + tt, :]
            a = up_ref[:, pl.ds(c0, cw)]
            act_ref[:, pl.ds(c0, cw)] = (gc * _sigmoid(gc) * a).astype(act_ref.dtype)

    return pl.pallas_call(
        body,
        grid=(t // tt,),
        in_specs=[
            pl.BlockSpec((tt, 2 * D_FF), lambda i: (i, 0)),
            _prev_halo(tt, hp, 2 * D_FF),
            pl.BlockSpec((FFN_CONV_K, D_FF), lambda i: (0, 0)),
            pl.BlockSpec((1, D_FF), lambda i: (0, 0)),
        ],
        out_specs=pl.BlockSpec((tt, D_FF), lambda i: (i, 0)),
        out_shape=jax.ShapeDtypeStruct((t, D_FF), MXU_DT),
        scratch_shapes=[pltpu.VMEM((tt + hp, FFN_CHUNK), F32)],
        name=name,
        compiler_params=_cparams("parallel"),
    )(up, up, conv_w, conv_b.reshape(1, D_FF))


def _ffn_bwd(up, dact, conv_w, conv_b, name):
    t = up.shape[0]
    tt = _rows(t, TWF)
    hp = FFN_HALO
    nt = t // tt
    kk = FFN_CONV_K

    def body(up_ref, upp_ref, upn_ref, da_ref, dan_ref, w_ref, b_ref, dup_ref, dw_ref, db_ref, ext_ref, dgc_ref):
        i = pl.program_id(0)
        first = i == 0
        last = i == nt - 1

        @pl.when(first)
        def _():
            dw_ref[...] = jnp.zeros_like(dw_ref)
            db_ref[...] = jnp.zeros_like(db_ref)

        for c0, cw in _ffn_chunks():
            ca = pl.ds(c0, cw)
            ga = pl.ds(D_FF + c0, cw)
            ext_ref[0:hp, :] = jnp.where(first, 0.0, upp_ref[:, ga])
            ext_ref[hp : hp + tt, :] = up_ref[:, ga]
            ext_ref[hp + tt : hp + tt + hp, :] = upn_ref[:, ga]
            gc = b_ref[:, ca]
            for j in range(kk):
                off = hp - (kk - 1) + j
                gc = gc + w_ref[j : j + 1, ca] * ext_ref[off : off + tt + hp, :]
            sg = _sigmoid(gc)
            silu = gc * sg
            dsilu = sg * (1.0 + gc * (1.0 - sg))
            a_all = jnp.concatenate([up_ref[:, ca], upn_ref[:, ca]], axis=0)
            dact_all = jnp.concatenate([da_ref[:, ca], jnp.where(last, 0.0, dan_ref[:, ca])], axis=0)
            dgc = dact_all * a_all * dsilu
            dgc_ref[...] = dgc
            dup_ref[:, ca] = (dact_all[0:tt] * silu[0:tt]).astype(dup_ref.dtype)
            dg = jnp.zeros((tt, cw), F32)
            for j in range(kk):
                dg = dg + w_ref[j : j + 1, ca] * dgc_ref[kk - 1 - j : kk - 1 - j + tt, :]
            dup_ref[:, ga] = dg.astype(dup_ref.dtype)
            dgc_t = dgc[0:tt]
            db_ref[:, ca] += jnp.sum(dgc_t, axis=0, keepdims=True)
            for j in range(kk):
                off = hp - (kk - 1) + j
                dw_ref[j : j + 1, ca] += jnp.sum(dgc_t * ext_ref[off : off + tt, :], axis=0, keepdims=True)

    return pl.pallas_call(
        body,
        grid=(nt,),
        in_specs=[
            pl.BlockSpec((tt, 2 * D_FF), lambda i: (i, 0)),
            _prev_halo(tt, hp, 2 * D_FF),
            _next_halo(tt, hp, 2 * D_FF, t),
            pl.BlockSpec((tt, D_FF), lambda i: (i, 0)),
            _next_halo(tt, hp, D_FF, t),
            pl.BlockSpec((kk, D_FF), lambda i: (0, 0)),
            pl.BlockSpec((1, D_FF), lambda i: (0, 0)),
        ],
        out_specs=[
            pl.BlockSpec((tt, 2 * D_FF), lambda i: (i, 0)),
            pl.BlockSpec((kk, D_FF), lambda i: (0, 0)),
            pl.BlockSpec((1, D_FF), lambda i: (0, 0)),
        ],
        out_shape=[
            jax.ShapeDtypeStruct((t, 2 * D_FF), MXU_DT),
            jax.ShapeDtypeStruct((kk, D_FF), F32),
            jax.ShapeDtypeStruct((1, D_FF), F32),
        ],
        scratch_shapes=[pltpu.VMEM((tt + 2 * hp, FFN_CHUNK), F32), pltpu.VMEM((tt + hp, FFN_CHUNK), F32)],
        name=name,
        compiler_params=_cparams("arbitrary"),
    )(up, up, up, dact, dact, conv_w, conv_b.reshape(1, D_FF))


def _layernorm_silu(cv, ln_g, ln_b):
    mu = jnp.mean(cv, axis=-1, keepdims=True)
    xc = cv - mu
    rstd = lax.rsqrt(jnp.mean(xc * xc, axis=-1, keepdims=True) + EPS)
    xh = xc * rstd
    a = xh * ln_g + ln_b
    return xh, rstd, a


def _shifted_copies(ref, n):
    for b in range(1, SUBLANE):
        ref[b, 0 : n - SUBLANE, :] = ref[0, b : b + n - SUBLANE, :]


def _tap(ref, offset, rows, cols):
    b = offset % SUBLANE
    return ref[b, offset - b : offset - b + rows, cols]


def _mix_fwd(z, pool_w, pool_scale, dw_w, dw_b, ln_g, ln_b, name):
    t = z.shape[0]
    tt = _rows(t, TW)
    hp = MIX_HALO
    zw = POOL_W + 2 * CONV_W

    def body(z_ref, zp_ref, pw_ref, ps_ref, w_ref, b_ref, lg_ref, lb_ref, cat_ref, eu_ref, egl_ref, cv_ref):
        i = pl.program_id(0)
        first = i == 0
        eu_ref[0:hp, :] = jnp.where(first, 0.0, zp_ref[:, 0:POOL_W])
        eu_ref[hp : hp + tt, :] = z_ref[:, 0:POOL_W]
        glp = zp_ref[:, POOL_W : POOL_W + CONV_W] * _sigmoid(zp_ref[:, POOL_W + CONV_W : zw])
        egl_ref[0, 0:hp, :] = jnp.where(first, 0.0, glp)
        egl_ref[0, hp : hp + tt, :] = z_ref[:, POOL_W : POOL_W + CONV_W] * _sigmoid(z_ref[:, POOL_W + CONV_W : zw])
        _shifted_copies(egl_ref, tt + hp)
        row = i * tt + lax.broadcasted_iota(jnp.int32, (tt, 1), 0)
        for gi, w in enumerate(POOL_WINDOWS):
            cols = pl.ds(gi * POOL_GROUP, POOL_GROUP)
            u = eu_ref[hp : hp + tt, cols]
            acc = u
            for k in range(1, w):
                acc = acc + eu_ref[hp - k : hp - k + tt, cols]
            cnt = jnp.minimum(row + 1, w).astype(F32)
            pooled = acc / cnt - u
            y = jnp.dot(pooled.astype(MXU_DT), pw_ref[gi].astype(MXU_DT), preferred_element_type=F32)
            cat_ref[:, cols] = (y * ps_ref[:, cols]).astype(cat_ref.dtype)
        for c0 in range(0, CONV_W, LANE):
            cs = pl.ds(c0, LANE)
            acc = jnp.broadcast_to(b_ref[:, cs], (tt, LANE))
            for j in range(CONV_K):
                acc = acc + w_ref[j : j + 1, cs] * _tap(egl_ref, hp - (CONV_K - 1) + j, tt, cs)
            cv_ref[:, cs] = acc
        _, _, a = _layernorm_silu(cv_ref[...], lg_ref[...], lb_ref[...])
        cat_ref[:, POOL_W : POOL_W + CONV_W] = (a * _sigmoid(a)).astype(cat_ref.dtype)

    vec = pl.BlockSpec((1, CONV_W), lambda i: (0, 0))
    return pl.pallas_call(
        body,
        grid=(t // tt,),
        in_specs=[
            pl.BlockSpec((tt, zw), lambda i: (i, 0)),
            _prev_halo(tt, hp, zw),
            pl.BlockSpec((len(POOL_WINDOWS), POOL_GROUP, POOL_GROUP), lambda i: (0, 0, 0)),
            vec,
            pl.BlockSpec((CONV_K, CONV_W), lambda i: (0, 0)),
            vec,
            vec,
            vec,
        ],
        out_specs=pl.BlockSpec((tt, POOL_W + CONV_W), lambda i: (i, 0)),
        out_shape=jax.ShapeDtypeStruct((t, POOL_W + CONV_W), MXU_DT),
        scratch_shapes=[pltpu.VMEM((tt + hp, POOL_W), F32), pltpu.VMEM((SUBLANE, tt + hp, CONV_W), F32),
                        pltpu.VMEM((tt, CONV_W), F32)],
        name=name,
        compiler_params=_cparams("parallel"),
    )(z, z, pool_w, pool_scale.reshape(1, POOL_W), dw_w, dw_b.reshape(1, CONV_W), ln_g.reshape(1, CONV_W), ln_b.reshape(1, CONV_W))


def _mix_bwd(z, dcat, pool_w, pool_scale, dw_w, dw_b, ln_g, ln_b, name):
    t = z.shape[0]
    tt = _rows(t, TW)
    hp = MIX_HALO
    nt = t // tt
    zw = POOL_W + 2 * CONV_W
    ng = len(POOL_WINDOWS)

    def body(z_ref, zp_ref, zn_ref, dc_ref, dcn_ref, pw_ref, ps_ref, w_ref, b_ref, lg_ref, lb_ref,
             dz_ref, dpw_ref, dps_ref, dww_ref, dwb_ref, dlg_ref, dlb_ref, eu_ref, ee_ref, egl_ref, edcv_ref, cv_ref):
        i = pl.program_id(0)
        first = i == 0
        last = i == nt - 1

        @pl.when(first)
        def _():
            for r in (dpw_ref, dps_ref, dww_ref, dwb_ref, dlg_ref, dlb_ref):
                r[...] = jnp.zeros_like(r)

        eu_ref[0:hp, :] = jnp.where(first, 0.0, zp_ref[:, 0:POOL_W])
        eu_ref[hp : hp + tt, :] = z_ref[:, 0:POOL_W]
        row = i * tt + lax.broadcasted_iota(jnp.int32, (tt, 1), 0)
        row_ext = i * tt + lax.broadcasted_iota(jnp.int32, (tt + hp, 1), 0)
        for gi, w in enumerate(POOL_WINDOWS):
            cols = pl.ds(gi * POOL_GROUP, POOL_GROUP)
            u = eu_ref[hp : hp + tt, cols]
            acc = u
            for k in range(1, w):
                acc = acc + eu_ref[hp - k : hp - k + tt, cols]
            pooled = (acc / jnp.minimum(row + 1, w).astype(F32) - u).astype(MXU_DT)
            pw = pw_ref[gi].astype(MXU_DT)
            dya = dc_ref[:, cols]
            y = jnp.dot(pooled, pw, preferred_element_type=F32)
            dps_ref[:, cols] += jnp.sum(dya * y, axis=0, keepdims=True)
            scale = ps_ref[:, cols]
            dy_all = jnp.concatenate([dya, jnp.where(last, 0.0, dcn_ref[:, cols])], axis=0) * scale
            dy_all = dy_all.astype(MXU_DT)
            dpw_ref[gi] += lax.dot_general(pooled, dy_all[0:tt], TN_DIMS, preferred_element_type=F32)
            dpooled = lax.dot_general(dy_all, pw, NT, preferred_element_type=F32)
            ee_ref[:, cols] = dpooled / jnp.minimum(row_ext + 1, w).astype(F32)
            du = -dpooled[0:tt]
            for k in range(w):
                du = du + ee_ref[k : k + tt, cols]
            dz_ref[:, cols] = du.astype(dz_ref.dtype)

        ca = slice(POOL_W, POOL_W + CONV_W)
        cb = slice(POOL_W + CONV_W, zw)
        egl_ref[0, 0:hp, :] = jnp.where(first, 0.0, zp_ref[:, ca] * _sigmoid(zp_ref[:, cb]))
        egl_ref[0, hp : hp + tt, :] = z_ref[:, ca] * _sigmoid(z_ref[:, cb])
        egl_ref[0, hp + tt : hp + tt + hp, :] = zn_ref[:, ca] * _sigmoid(zn_ref[:, cb])
        _shifted_copies(egl_ref, tt + 2 * hp)
        for c0 in range(0, CONV_W, LANE):
            cs = pl.ds(c0, LANE)
            acc = jnp.broadcast_to(b_ref[:, cs], (tt + hp, LANE))
            for j in range(CONV_K):
                acc = acc + w_ref[j : j + 1, cs] * _tap(egl_ref, hp - (CONV_K - 1) + j, tt + hp, cs)
            cv_ref[:, cs] = acc
        lg = lg_ref[...]
        xh, rstd, a = _layernorm_silu(cv_ref[...], lg, lb_ref[...])
        sa = _sigmoid(a)
        dyb = jnp.concatenate([dc_ref[:, ca], jnp.where(last, 0.0, dcn_ref[:, ca])], axis=0)
        da = dyb * (sa * (1.0 + a * (1.0 - sa)))
        dlg_ref[...] += jnp.sum(da[0:tt] * xh[0:tt], axis=0, keepdims=True)
        dlb_ref[...] += jnp.sum(da[0:tt], axis=0, keepdims=True)
        dxh = da * lg
        dcv = rstd * (dxh - jnp.mean(dxh, axis=-1, keepdims=True) - xh * jnp.mean(dxh * xh, axis=-1, keepdims=True))
        edcv_ref[0] = dcv
        _shifted_copies(edcv_ref, tt + hp)
        dwb_ref[...] += jnp.sum(dcv[0:tt], axis=0, keepdims=True)
        for c0 in range(0, CONV_W, LANE):
            cs = pl.ds(c0, LANE)
            gl_t = egl_ref[0, hp : hp + tt, cs]
            dgl = jnp.zeros((tt, LANE), F32)
            for j in range(CONV_K):
                tap = _tap(edcv_ref, CONV_K - 1 - j, tt, cs)
                dww_ref[j : j + 1, cs] += jnp.sum(gl_t * tap, axis=0, keepdims=True)
                dgl = dgl + w_ref[j : j + 1, cs] * tap
            ga = z_ref[:, pl.ds(POOL_W + c0, LANE)]
            sgb = _sigmoid(z_ref[:, pl.ds(POOL_W + CONV_W + c0, LANE)])
            dz_ref[:, pl.ds(POOL_W + c0, LANE)] = (dgl * sgb).astype(dz_ref.dtype)
            dz_ref[:, pl.ds(POOL_W + CONV_W + c0, LANE)] = (dgl * ga * sgb * (1.0 - sgb)).astype(dz_ref.dtype)

    vec = pl.BlockSpec((1, CONV_W), lambda i: (0, 0))
    pw_spec = pl.BlockSpec((ng, POOL_GROUP, POOL_GROUP), lambda i: (0, 0, 0))
    w_spec = pl.BlockSpec((CONV_K, CONV_W), lambda i: (0, 0))
    return pl.pallas_call(
        body,
        grid=(nt,),
        in_specs=[
            pl.BlockSpec((tt, zw), lambda i: (i, 0)),
            _prev_halo(tt, hp, zw),
            _next_halo(tt, hp, zw, t),
            pl.BlockSpec((tt, POOL_W + CONV_W), lambda i: (i, 0)),
            _next_halo(tt, hp, POOL_W + CONV_W, t),
            pw_spec, vec, w_spec, vec, vec, vec,
        ],
        out_specs=[pl.BlockSpec((tt, zw), lambda i: (i, 0)), pw_spec, vec, w_spec, vec, vec, vec],
        out_shape=[
            jax.ShapeDtypeStruct((t, zw), MXU_DT),
            jax.ShapeDtypeStruct((ng, POOL_GROUP, POOL_GROUP), F32),
            jax.ShapeDtypeStruct((1, POOL_W), F32),
            jax.ShapeDtypeStruct((CONV_K, CONV_W), F32),
            jax.ShapeDtypeStruct((1, CONV_W), F32),
            jax.ShapeDtypeStruct((1, CONV_W), F32),
            jax.ShapeDtypeStruct((1, CONV_W), F32),
        ],
        scratch_shapes=[
            pltpu.VMEM((tt + hp, POOL_W), F32),
            pltpu.VMEM((tt + hp, POOL_W), F32),
            pltpu.VMEM((SUBLANE, tt + 2 * hp, CONV_W), F32),
            pltpu.VMEM((SUBLANE, tt + hp, CONV_W), F32),
            pltpu.VMEM((tt + hp, CONV_W), F32),
        ],
        name=name,
        compiler_params=_cparams("arbitrary"),
    )(z, z, z, dcat, dcat, pool_w, pool_scale.reshape(1, POOL_W), dw_w, dw_b.reshape(1, CONV_W),
      ln_g.reshape(1, CONV_W), ln_b.reshape(1, CONV_W))


def _xa_fwd(q, kvm, name):
    t = q.shape[0]
    tt = _rows(t, TT)
    scale = XA_HEAD_DIM ** -0.5

    def body(q_ref, kv_ref, o_ref):
        for h in range(XA_HEADS):
            cs = pl.ds(h * XA_HEAD_DIM, XA_HEAD_DIM)
            vs = pl.ds(D_MODEL + h * XA_HEAD_DIM, XA_HEAD_DIM)
            s = lax.dot_general(q_ref[:, cs], kv_ref[:, cs], NT, preferred_element_type=F32) * scale
            p = jnp.exp(s - jnp.max(s, axis=-1, keepdims=True))
            p = p / jnp.sum(p, axis=-1, keepdims=True)
            o_ref[:, cs] = jnp.dot(p.astype(MXU_DT), kv_ref[:, vs], preferred_element_type=F32).astype(o_ref.dtype)

    return pl.pallas_call(
        body,
        grid=(t // tt,),
        in_specs=[pl.BlockSpec((tt, D_MODEL), lambda i: (i, 0)), pl.BlockSpec((MEM_LEN, 2 * D_MODEL), lambda i: (0, 0))],
        out_specs=pl.BlockSpec((tt, D_MODEL), lambda i: (i, 0)),
        out_shape=jax.ShapeDtypeStruct((t, D_MODEL), MXU_DT),
        name=name,
        compiler_params=_cparams("parallel"),
    )(q, kvm)


def _xa_bwd(q, kvm, do, name):
    t = q.shape[0]
    tt = _rows(t, TT)
    scale = XA_HEAD_DIM ** -0.5

    def body(q_ref, kv_ref, do_ref, dq_ref, dkv_ref):
        @pl.when(pl.program_id(0) == 0)
        def _():
            dkv_ref[...] = jnp.zeros_like(dkv_ref)

        for h in range(XA_HEADS):
            cs = pl.ds(h * XA_HEAD_DIM, XA_HEAD_DIM)
            vs = pl.ds(D_MODEL + h * XA_HEAD_DIM, XA_HEAD_DIM)
            qh = q_ref[:, cs]
            kh = kv_ref[:, cs]
            doh = do_ref[:, cs]
            s = lax.dot_general(qh, kh, NT, preferred_element_type=F32) * scale
            p = jnp.exp(s - jnp.max(s, axis=-1, keepdims=True))
            p = p / jnp.sum(p, axis=-1, keepdims=True)
            dp = lax.dot_general(doh, kv_ref[:, vs], NT, preferred_element_type=F32)
            ds = (p * (dp - jnp.sum(p * dp, axis=-1, keepdims=True)) * scale).astype(MXU_DT)
            dq_ref[:, cs] = jnp.dot(ds, kh, preferred_element_type=F32).astype(dq_ref.dtype)
            dkv_ref[:, cs] += lax.dot_general(ds, qh, TN_DIMS, preferred_element_type=F32)
            dkv_ref[:, vs] += lax.dot_general(p.astype(MXU_DT), doh, TN_DIMS, preferred_element_type=F32)

    row = pl.BlockSpec((tt, D_MODEL), lambda i: (i, 0))
    kvs = pl.BlockSpec((MEM_LEN, 2 * D_MODEL), lambda i: (0, 0))
    return pl.pallas_call(
        body,
        grid=(t // tt,),
        in_specs=[row, kvs, row],
        out_specs=[row, kvs],
        out_shape=[jax.ShapeDtypeStruct((t, D_MODEL), MXU_DT), jax.ShapeDtypeStruct((MEM_LEN, 2 * D_MODEL), F32)],
        name=name,
        compiler_params=_cparams("arbitrary"),
    )(q, kvm, do)


def _rope_tables(positions, name):
    t = positions.shape[0]
    tt = _rows(t, TT)
    inv = 1.0 / (ROPE_THETA ** (np.arange(0, QK_ROPE, 2, dtype=np.float32) / QK_ROPE))
    lanes = np.zeros((1, HEAD_PAD), np.float32)
    half = QK_ROPE // 2
    lanes[0, KPE_LANE : KPE_LANE + half] = inv
    lanes[0, KPE_LANE + half : KPE_LANE + QK_ROPE] = inv

    def body(pos_ref, inv_ref, cos_ref, sa_ref, sb_ref):
        ang = pos_ref[...].astype(F32) * inv_ref[...]
        lane = lax.broadcasted_iota(jnp.int32, (tt, HEAD_PAD), 1)
        c = jnp.cos(ang)
        s = jnp.sin(ang)
        lo = (lane >= KPE_LANE) & (lane < KPE_LANE + half)
        hi = (lane >= KPE_LANE + half) & (lane < KPE_LANE + QK_ROPE)
        cos_ref[...] = jnp.where(lo | hi, c, 1.0)
        sa_ref[...] = jnp.where(hi, s, 0.0)
        sb_ref[...] = jnp.where(lo, -s, 0.0)

    tab = pl.BlockSpec((tt, HEAD_PAD), lambda i: (i, 0))
    return pl.pallas_call(
        body,
        grid=(t // tt,),
        in_specs=[pl.BlockSpec((tt, 1), lambda i: (i, 0)), pl.BlockSpec((1, HEAD_PAD), lambda i: (0, 0))],
        out_specs=[tab, tab, tab],
        out_shape=[jax.ShapeDtypeStruct((t, HEAD_PAD), F32)] * 3,
        name=name,
        compiler_params=_cparams("parallel"),
    )(positions, jnp.asarray(lanes))


def _rotate(x, cos, sa, sb, sign):
    half = QK_ROPE // 2
    return x * cos + sign * (pltpu.roll(x, half, 1) * sa + pltpu.roll(x, HEAD_PAD - half, 1) * sb)


def _rope_heads(x, tables, sign, scale, name):
    t, w = x.shape
    tt = _rows(t, TT)
    nh = w // HEAD_PAD

    def body(x_ref, c_ref, sa_ref, sb_ref, o_ref):
        cos, sa, sb = c_ref[...] * scale, sa_ref[...] * scale, sb_ref[...] * scale
        for h in range(nh):
            cs = pl.ds(h * HEAD_PAD, HEAD_PAD)
            o_ref[:, cs] = _rotate(x_ref[:, cs], cos, sa, sb, sign).astype(o_ref.dtype)

    tab = pl.BlockSpec((tt, HEAD_PAD), lambda i: (i, 0))
    row = pl.BlockSpec((tt, w), lambda i: (i, 0))
    return pl.pallas_call(
        body,
        grid=(t // tt,),
        in_specs=[row, tab, tab, tab],
        out_specs=row,
        out_shape=jax.ShapeDtypeStruct((t, w), MXU_DT),
        name=name,
        compiler_params=_cparams("parallel"),
    )(x, *tables)


def _mla_prep(cp, qg, kvg, tables, name):
    t = cp.shape[0]
    tt = _rows(t, TT)

    def body(cp_ref, qg_ref, kvg_ref, c_ref, sa_ref, sb_ref, qn_ref, kvn_ref, kpe_ref):
        cq = cp_ref[:, 0:Q_LORA]
        r = lax.rsqrt(jnp.mean(cq * cq, axis=-1, keepdims=True) + EPS)
        qn_ref[...] = ((cq * r) * qg_ref[...]).astype(qn_ref.dtype)
        ckv = cp_ref[:, Q_LORA : Q_LORA + KV_LORA]
        r = lax.rsqrt(jnp.mean(ckv * ckv, axis=-1, keepdims=True) + EPS)
        kvn_ref[...] = ((ckv * r) * kvg_ref[...]).astype(kvn_ref.dtype)
        kpe = cp_ref[:, Q_LORA + KV_LORA : C_PAD]
        kpe_ref[...] = _rotate(kpe, c_ref[...], sa_ref[...], sb_ref[...], 1.0).astype(kpe_ref.dtype)

    tab = pl.BlockSpec((tt, HEAD_PAD), lambda i: (i, 0))
    return pl.pallas_call(
        body,
        grid=(t // tt,),
        in_specs=[
            pl.BlockSpec((tt, C_PAD), lambda i: (i, 0)),
            pl.BlockSpec((1, Q_LORA), lambda i: (0, 0)),
            pl.BlockSpec((1, KV_LORA), lambda i: (0, 0)),
            tab, tab, tab,
        ],
        out_specs=[
            pl.BlockSpec((tt, Q_LORA), lambda i: (i, 0)),
            pl.BlockSpec((tt, KV_LORA), lambda i: (i, 0)),
            tab,
        ],
        out_shape=[
            jax.ShapeDtypeStruct((t, Q_LORA), MXU_DT),
            jax.ShapeDtypeStruct((t, KV_LORA), MXU_DT),
            jax.ShapeDtypeStruct((t, HEAD_PAD), MXU_DT),
        ],
        name=name,
        compiler_params=_cparams("parallel"),
    )(cp, qg.reshape(1, Q_LORA), kvg.reshape(1, KV_LORA), *tables)


def _mla_prep_bwd(cp, dqn, dkvn, dkpe_heads, qg, kvg, tables, name):
    t = cp.shape[0]
    tt = _rows(t, TT)

    def norm_bwd(x, dy, g):
        r = lax.rsqrt(jnp.mean(x * x, axis=-1, keepdims=True) + EPS)
        xh = x * r
        gy = dy * g
        return r * (gy - xh * jnp.mean(gy * xh, axis=-1, keepdims=True)), jnp.sum(dy * xh, axis=0, keepdims=True)

    def body(cp_ref, dqn_ref, dkvn_ref, dkpe_ref, qg_ref, kvg_ref, c_ref, sa_ref, sb_ref, dcp_ref, dqg_ref, dkvg_ref):
        @pl.when(pl.program_id(0) == 0)
        def _():
            dqg_ref[...] = jnp.zeros_like(dqg_ref)
            dkvg_ref[...] = jnp.zeros_like(dkvg_ref)

        dcq, dg = norm_bwd(cp_ref[:, 0:Q_LORA], dqn_ref[...], qg_ref[...])
        dcp_ref[:, 0:Q_LORA] = dcq.astype(dcp_ref.dtype)
        dqg_ref[...] += dg
        dckv, dg = norm_bwd(cp_ref[:, Q_LORA : Q_LORA + KV_LORA], dkvn_ref[...], kvg_ref[...])
        dcp_ref[:, Q_LORA : Q_LORA + KV_LORA] = dckv.astype(dcp_ref.dtype)
        dkvg_ref[...] += dg
        dk = dkpe_ref[0]
        for h in range(1, MLA_HEADS):
            dk = dk + dkpe_ref[h]
        dcp_ref[:, Q_LORA + KV_LORA : C_PAD] = _rotate(dk, c_ref[...], sa_ref[...], sb_ref[...], -1.0).astype(dcp_ref.dtype)

    tab = pl.BlockSpec((tt, HEAD_PAD), lambda i: (i, 0))
    return pl.pallas_call(
        body,
        grid=(t // tt,),
        in_specs=[
            pl.BlockSpec((tt, C_PAD), lambda i: (i, 0)),
            pl.BlockSpec((tt, Q_LORA), lambda i: (i, 0)),
            pl.BlockSpec((tt, KV_LORA), lambda i: (i, 0)),
            pl.BlockSpec((MLA_HEADS, tt, HEAD_PAD), lambda i: (0, i, 0)),
            pl.BlockSpec((1, Q_LORA), lambda i: (0, 0)),
            pl.BlockSpec((1, KV_LORA), lambda i: (0, 0)),
            tab, tab, tab,
        ],
        out_specs=[
            pl.BlockSpec((tt, C_PAD), lambda i: (i, 0)),
            pl.BlockSpec((1, Q_LORA), lambda i: (0, 0)),
            pl.BlockSpec((1, KV_LORA), lambda i: (0, 0)),
        ],
        out_shape=[
            jax.ShapeDtypeStruct((t, C_PAD), MXU_DT),
            jax.ShapeDtypeStruct((1, Q_LORA), F32),
            jax.ShapeDtypeStruct((1, KV_LORA), F32),
        ],
        name=name,
        compiler_params=_cparams("arbitrary"),
    )(cp, dqn, dkvn, dkpe_heads, qg.reshape(1, Q_LORA), kvg.reshape(1, KV_LORA), *tables)


def _flash_fwd(qs, kv, kpe, name):
    t = qs.shape[0]
    ta = _rows(t, TA)
    tq = ta
    nq = t // tq
    sub = ta // 2

    def body(q_ref, kv_ref, kpe_ref, o_ref, lse_ref):
        qi = pl.program_id(1)
        q = q_ref[...]
        lane = lax.broadcasted_iota(jnp.int32, (ta, HEAD_PAD), 1)

        def kblock(j):
            rows = pl.ds(pl.multiple_of(j * ta, ta), ta)
            kvb = kv_ref[rows, :]
            ones_v = jnp.where(lane < QK_NOPE, jnp.ones_like(kvb), kvb)
            return ones_v, jnp.where(lane < QK_NOPE, kvb, kpe_ref[rows, :])

        def update(carry, s, ones_v):
            m, acc = carry
            m_new = jnp.maximum(m, jnp.max(s, axis=-1, keepdims=True))
            p = jnp.exp2(s - m_new).astype(MXU_DT)
            acc = jnp.exp2(m - m_new) * acc + jnp.dot(p, ones_v, preferred_element_type=F32)
            return m_new, acc

        def step(j, carry):
            ones_v, k = kblock(j)
            return update(carry, lax.dot_general(q, k, NT, preferred_element_type=F32), ones_v)

        init = (jnp.full((tq, 1), -jnp.inf, F32), jnp.zeros((tq, HEAD_PAD), F32))
        m_all, acc_all = lax.fori_loop(0, qi, step, init)
        ones_v, k = kblock(qi)
        lane_h = lax.broadcasted_iota(jnp.int32, (sub, HEAD_PAD), 1)
        for b in range(2):
            rows, nk = slice(b * sub, (b + 1) * sub), (b + 1) * sub
            s = lax.dot_general(q[rows], k[0:nk], NT, preferred_element_type=F32)
            r = lax.broadcasted_iota(jnp.int32, (sub, nk), 0) + b * sub
            c = lax.broadcasted_iota(jnp.int32, (sub, nk), 1)
            m, acc = update((m_all[rows], acc_all[rows]), jnp.where(c <= r, s, NEG), ones_v[0:nk])
            l = acc[:, 0:1]
            o_ref[rows, :] = jnp.where(lane_h >= QK_NOPE, acc / l, 0.0).astype(o_ref.dtype)
            lse_ref[rows, :] = m + jnp.log2(l)

    return pl.pallas_call(
        body,
        grid=(MLA_HEADS, nq),
        in_specs=[
            pl.BlockSpec((tq, HEAD_PAD), lambda h, i: (i, h)),
            pl.BlockSpec((t, HEAD_PAD), lambda h, i: (0, h)),
            pl.BlockSpec((t, HEAD_PAD), lambda h, i: (0, 0)),
        ],
        out_specs=[
            pl.BlockSpec((tq, HEAD_PAD), lambda h, i: (i, h)),
            pl.BlockSpec((None, tq, 1), lambda h, i: (h, i, 0)),
        ],
        out_shape=[
            jax.ShapeDtypeStruct((t, MLA_HEADS * HEAD_PAD), MXU_DT),
            jax.ShapeDtypeStruct((MLA_HEADS, t, 1), F32),
        ],
        name=name,
        compiler_params=_cparams("parallel", "parallel"),
    )(qs, kv, kpe)


def _flash_bwd(qs, kv, kpe, o, do, lse, name):
    t = qs.shape[0]
    ta = _rows(t, TA)
    tq = ta
    nq = t // ta
    sub = ta // 2

    def body(q_ref, o_ref, do_ref, lse_ref, kv_ref, kpe_ref, dq_ref, dkv_ref, dkpe_ref, dk_acc, dv_acc):
        kj = pl.program_id(1)

        @pl.when(kj == 0)
        def _():
            dq_ref[...] = jnp.zeros_like(dq_ref)

        lane = lax.broadcasted_iota(jnp.int32, (ta, HEAD_PAD), 1)
        kvb = kv_ref[...]
        k = jnp.where(lane < QK_NOPE, kvb, kpe_ref[...])
        dk_acc[...] = jnp.zeros_like(dk_acc)
        dv_acc[...] = jnp.zeros_like(dv_acc)

        def tile(row0, nrows, nkeys, diagonal):
            rows = pl.ds(pl.multiple_of(row0, sub), nrows)
            keys = slice(0, nkeys)
            q = q_ref[rows, :]
            dob = do_ref[rows, :]
            delta = jnp.sum(dob.astype(F32) * o_ref[rows, :].astype(F32), axis=-1, keepdims=True)
            s = lax.dot_general(q, k[keys], NT, preferred_element_type=F32)
            if diagonal:
                r = lax.broadcasted_iota(jnp.int32, (nrows, nkeys), 0) + (nkeys - nrows)
                c = lax.broadcasted_iota(jnp.int32, (nrows, nkeys), 1)
                s = jnp.where(c <= r, s, NEG)
            p = jnp.exp2(s - lse_ref[rows, :])
            dp = lax.dot_general(dob, kvb[keys], NT, preferred_element_type=F32)
            ds = (p * (dp - delta)).astype(MXU_DT)
            dq_ref[rows, :] += jnp.dot(ds, k[keys], preferred_element_type=F32)
            dk_acc[keys, :] += lax.dot_general(ds, q, TN_DIMS, preferred_element_type=F32)
            dv_acc[keys, :] += lax.dot_general(p.astype(MXU_DT), dob, TN_DIMS, preferred_element_type=F32)

        tile(kj * ta, sub, sub, True)
        tile(kj * ta + sub, sub, ta, True)

        def step(qq, carry):
            tile(qq * tq, tq, ta, False)
            return carry

        lax.fori_loop(kj + 1, t // tq, step, 0)
        dk = dk_acc[...] * (1.0 / LOG2E)
        dkv_ref[...] = jnp.where(lane < QK_NOPE, dk, dv_acc[...]).astype(dkv_ref.dtype)
        dkpe_ref[...] = jnp.where((lane >= KPE_LANE) & (lane < KPE_LANE + QK_ROPE), dk, 0.0)

    head_rows = pl.BlockSpec((t, HEAD_PAD), lambda h, j: (0, h))
    return pl.pallas_call(
        body,
        grid=(MLA_HEADS, nq),
        in_specs=[
            head_rows,
            head_rows,
            head_rows,
            pl.BlockSpec((None, t, 1), lambda h, j: (h, 0, 0)),
            pl.BlockSpec((ta, HEAD_PAD), lambda h, j: (j, h)),
            pl.BlockSpec((ta, HEAD_PAD), lambda h, j: (j, 0)),
        ],
        out_specs=[
            head_rows,
            pl.BlockSpec((ta, HEAD_PAD), lambda h, j: (j, h)),
            pl.BlockSpec((None, ta, HEAD_PAD), lambda h, j: (h, j, 0)),
        ],
        out_shape=[
            jax.ShapeDtypeStruct((t, MLA_HEADS * HEAD_PAD), F32),
            jax.ShapeDtypeStruct((t, MLA_HEADS * HEAD_PAD), MXU_DT),
            jax.ShapeDtypeStruct((MLA_HEADS, t, HEAD_PAD), F32),
        ],
        scratch_shapes=[pltpu.VMEM((ta, HEAD_PAD), F32), pltpu.VMEM((ta, HEAD_PAD), F32)],
        name=name,
        compiler_params=_cparams("parallel", "arbitrary"),
    )(qs, o, do, lse, kv, kpe)


def _as2d(a):
    if a.ndim == 1:
        return a.reshape(1, a.shape[0])
    return a.reshape(-1, a.shape[-1])


def _adamw(w, g, m, v, name):
    shape = w.shape
    w2, g2, m2, v2 = (_as2d(a) for a in (w, g, m, v))
    r, c = w2.shape
    tr = _tile_rows(r, c)
    c1 = 1.0 - ADAM_B1 ** ADAM_STEP
    c2 = 1.0 - ADAM_B2 ** ADAM_STEP

    def body(w_ref, g_ref, m_ref, v_ref, d_ref, nm_ref, nv_ref):
        gv = g_ref[...]
        nm = ADAM_B1 * m_ref[...] + (1.0 - ADAM_B1) * gv
        nv = ADAM_B2 * v_ref[...] + (1.0 - ADAM_B2) * (gv * gv)
        d_ref[...] = -ADAM_LR * ((nm / c1) / (jnp.sqrt(nv / c2) + ADAM_EPS) + ADAM_WD * w_ref[...])
        nm_ref[...] = nm
        nv_ref[...] = nv

    blk = pl.BlockSpec((tr, c), lambda i: (i, 0))
    outs = pl.pallas_call(
        body,
        grid=(r // tr,),
        in_specs=[blk] * 4,
        out_specs=[blk] * 3,
        out_shape=[jax.ShapeDtypeStruct((r, c), F32)] * 3,
        name=name,
        compiler_params=_cparams("parallel"),
    )(w2, g2, m2, v2)
    return tuple(o.reshape(shape) for o in outs)


def _adamw_halves(w, mine, other, m, v, c_idx, name):
    nl, r, c = w.shape
    h = nl // 2
    tr = _tile_rows(r, 2 * c)
    c1 = 1.0 - ADAM_B1 ** ADAM_STEP
    c2 = 1.0 - ADAM_B2 ** ADAM_STEP

    def body(c_ref, w_ref, a_ref, b_ref, m_ref, v_ref, g_ref, d_ref, nm_ref, nv_ref):
        l = pl.program_id(0)
        gv = jnp.where(l // h == c_ref[0], a_ref[...], b_ref[...])
        nm = ADAM_B1 * m_ref[...] + (1.0 - ADAM_B1) * gv
        nv = ADAM_B2 * v_ref[...] + (1.0 - ADAM_B2) * (gv * gv)
        g_ref[...] = gv
        d_ref[...] = -ADAM_LR * ((nm / c1) / (jnp.sqrt(nv / c2) + ADAM_EPS) + ADAM_WD * w_ref[...])
        nm_ref[...] = nm
        nv_ref[...] = nv

    def half_map(mine_side):
        def index(l, i, cr):
            first = cr[0] * h if mine_side else (1 - cr[0]) * h
            return (jnp.clip(l - first, 0, h - 1), i, 0)
        return index

    full = pl.BlockSpec((None, tr, c), lambda l, i, cr: (l, i, 0))
    grid_spec = pltpu.PrefetchScalarGridSpec(
        num_scalar_prefetch=1,
        grid=(nl, r // tr),
        in_specs=[full, pl.BlockSpec((None, tr, c), half_map(True)), pl.BlockSpec((None, tr, c), half_map(False)), full, full],
        out_specs=[full] * 4,
    )
    return pl.pallas_call(
        body,
        grid_spec=grid_spec,
        out_shape=[jax.ShapeDtypeStruct((nl, r, c), F32)] * 4,
        name=name,
        compiler_params=_cparams("parallel", "parallel"),
    )(c_idx, w, mine, other, m, v)


def _tile_rows(r, c, mult=SUBLANE):
    limit = max(mult, (BLOCK_BYTES // 4) // (4 * c))
    if r <= limit:
        return r
    t = (limit // mult) * mult
    while t >= mult:
        if r % t == 0:
            return t
        t -= mult
    return r


def _sum_leading(a, name):
    n, r, c = a.shape
    tr = _tile_rows(r, c * n)

    def body(a_ref, o_ref):
        s = a_ref[0]
        for k in range(1, n):
            s = s + a_ref[k]
        o_ref[...] = s

    return pl.pallas_call(
        body,
        grid=(r // tr,),
        in_specs=[pl.BlockSpec((n, tr, c), lambda i: (0, i, 0))],
        out_specs=pl.BlockSpec((tr, c), lambda i: (i, 0)),
        out_shape=jax.ShapeDtypeStruct((r, c), F32),
        name=name,
        compiler_params=_cparams("parallel"),
    )(a)


def _add_half(g, s, c_idx, name):
    nl, r, c = g.shape
    h = nl // 2
    tr = _tile_rows(r, 2 * c, 2 * SUBLANE)

    def body(c_ref, g_ref, s_ref, o_ref):
        o_ref[...] = (g_ref[...] + s_ref[...]).astype(o_ref.dtype)

    grid_spec = pltpu.PrefetchScalarGridSpec(
        num_scalar_prefetch=1,
        grid=(h, r // tr),
        in_specs=[
            pl.BlockSpec((None, tr, c), lambda l, i, cr: (cr[0] * h + l, i, 0)),
            pl.BlockSpec((None, tr, c), lambda l, i, cr: (l, i, 0)),
        ],
        out_specs=pl.BlockSpec((None, tr, c), lambda l, i, cr: (l, i, 0)),
    )
    return pl.pallas_call(
        body,
        grid_spec=grid_spec,
        out_shape=jax.ShapeDtypeStruct((h, r, c), XFER_DT),
        name=name,
        compiler_params=_cparams("parallel", "parallel"),
    )(c_idx, g, s)


def _sum_chips(slots, pair, chip_idx, kind, name):
    _, h, r, c = slots.shape
    tr = _tile_rows(r, 5 * c, 2 * SUBLANE)
    nr = r // tr

    def body(chip_ref, s_ref, own_ref, o_ref):
        chip = chip_ref[0]
        own = own_ref[...].astype(F32)
        parts = [s_ref[j].astype(F32) for j in range(3)]
        total = None
        for k in range(4):
            d = jnp.bitwise_xor(chip, k)
            v = jnp.where(d == 0, own, jnp.where(d == 2, parts[0], jnp.where(d == 1, parts[1], parts[2])))
            total = v if total is None else total + v
        o_ref[...] = total

    if kind == "row":
        own_spec = pl.BlockSpec((None, tr, c), lambda l, i, cr: (l, cr[0] * nr + i, 0))
    else:
        own_spec = pl.BlockSpec((None, tr, c), lambda l, i, cr: (l, i, cr[0]))
    grid_spec = pltpu.PrefetchScalarGridSpec(
        num_scalar_prefetch=1,
        grid=(h, nr),
        in_specs=[pl.BlockSpec((3, None, tr, c), lambda l, i, cr: (0, l, i, 0)), own_spec],
        out_specs=pl.BlockSpec((None, tr, c), lambda l, i, cr: (l, i, 0)),
    )
    return pl.pallas_call(
        body,
        grid_spec=grid_spec,
        out_shape=jax.ShapeDtypeStruct((h, r, c), F32),
        name=name,
        compiler_params=_cparams("parallel", "parallel"),
    )(chip_idx, slots, pair)


def _mesh_pos():
    return lax.axis_index("x"), lax.axis_index("y"), lax.axis_index("c")


def _other_chips(x, y):
    return [(1 - x, y), (x, 1 - y), (1 - x, 1 - y)]


def _all_gather_rows(block, name):
    m_per, n = block.shape

    def body(x_ref, out_ref, send_sems, recv_sems, local_sem):
        x, y, c = _mesh_pos()
        me, sibling = (x, y, c), (x, y, 1 - c)
        chips = _other_chips(x, y)

        def rows(px, py, pc):
            return out_ref.at[pl.ds((4 * px + 2 * py + pc) * m_per, m_per), :]

        def copy(k, blk, to, src=None):
            return pltpu.make_async_remote_copy(
                src_ref=rows(*blk) if src is None else src,
                dst_ref=rows(*blk),
                send_sem=send_sems.at[k],
                recv_sem=recv_sems.at[k],
                device_id=to,
                device_id_type=MESH_ID,
            )

        mine = pltpu.make_async_copy(x_ref, rows(*me), local_sem)
        mine.start()
        first = [copy(0, me, sibling, src=x_ref)]
        first += [copy(1 + j, me, (*chip, c), src=x_ref) for j, chip in enumerate(chips)]
        for cp in first:
            cp.start()
        passed = [copy(4 + j, (*chip, c), sibling) for j, chip in enumerate(chips)]
        for j, chip in enumerate(chips):
            copy(1 + j, (*chip, c), me).wait_recv()
            passed[j].start()
        copy(0, sibling, me).wait_recv()
        for j, chip in enumerate(chips):
            copy(4 + j, (*chip, 1 - c), me).wait_recv()
        for cp in first + passed:
            cp.wait_send()
        mine.wait()

    return pl.pallas_call(
        body,
        out_shape=jax.ShapeDtypeStruct((8 * m_per, n), block.dtype),
        in_specs=[pl.BlockSpec(memory_space=pltpu.VMEM)],
        out_specs=pl.BlockSpec(memory_space=pltpu.VMEM),
        scratch_shapes=[pltpu.SemaphoreType.DMA((7,)), pltpu.SemaphoreType.DMA((7,)), pltpu.SemaphoreType.DMA],
        name=name,
        compiler_params=pltpu.CompilerParams(vmem_limit_bytes=VMEM_LIMIT),
    )(block)


def _shard_window(ref, layers, chip, rows, cols):
    if rows is not None:
        return ref.at[layers, pl.ds(pl.multiple_of(chip * rows, rows), rows), :]
    return ref.at[layers, :, pl.ds(pl.multiple_of(chip * cols, cols), cols)]


def _all_gather_weights(shards, kinds, name):
    nw = len(shards)
    out_shapes = []
    for s, kind in zip(shards, kinds):
        nl, r, c = s.shape
        full = (nl, 4 * r, c) if kind == "row" else (nl, r, 4 * c)
        out_shapes.append(jax.ShapeDtypeStruct(full, s.dtype))

    def body(*refs):
        ins, outs = refs[:nw], refs[nw : 2 * nw]
        send_sems, recv_sems, in_sems, out_sems = refs[2 * nw : 2 * nw + 4]
        bufs = refs[2 * nw + 4 :]
        x, y, c = _mesh_pos()
        sibling = (x, y, 1 - c)
        chips = _other_chips(x, y)
        my_chip = 2 * x + y

        def window(w, chip, layers):
            _, r, cc = shards[w].shape
            if kinds[w] == "row":
                return _shard_window(outs[w], layers, chip, r, None)
            return _shard_window(outs[w], layers, chip, None, cc)

        def half(w, half_idx):
            h = shards[w].shape[0] // 2
            return pl.ds(half_idx * h, h)

        def copy(w, k, src, dst, to):
            return pltpu.make_async_remote_copy(
                src_ref=src, dst_ref=dst, send_sem=send_sems.at[w, k], recv_sem=recv_sems.at[w, k],
                device_id=to, device_id_type=MESH_ID)

        sent = []
        for w in range(nw):
            mine = ins[w].at[half(w, c)]
            for j, chip in enumerate(chips):
                cp = copy(w, j, mine, window(w, my_chip, half(w, c)), (*chip, c))
                cp.start()
                sent.append(cp)
        for w in range(nw):
            nl = shards[w].shape[0]

            def load(l, w=w):
                return pltpu.make_async_copy(ins[w].at[l], bufs[w].at[l % 2], in_sems.at[w, l % 2])

            def store(l, w=w):
                return pltpu.make_async_copy(bufs[w].at[l % 2], window(w, my_chip, l), out_sems.at[w, l % 2])

            load(0).start()
            for l in range(nl):
                load(l).wait()
                store(l).start()
                if l + 1 < nl:
                    if l >= 1:
                        store(l - 1).wait()
                    load(l + 1).start()
            for l in range(max(nl - 2, 0), nl):
                store(l).wait()
        for w in range(nw):
            for j, (cx, cy) in enumerate(chips):
                got = window(w, 2 * cx + cy, half(w, c))
                copy(w, j, got, got, (cx, cy, c)).wait_recv()
                cp = copy(w, 3 + j, got, got, sibling)
                cp.start()
                sent.append(cp)
        for w in range(nw):
            for j, (cx, cy) in enumerate(chips):
                got = window(w, 2 * cx + cy, half(w, 1 - c))
                copy(w, 3 + j, got, got, sibling).wait_recv()
        for cp in sent:
            cp.wait_send()

    anyspec = pl.BlockSpec(memory_space=pl.ANY)
    return pl.pallas_call(
        body,
        out_shape=out_shapes,
        in_specs=[anyspec] * nw,
        out_specs=[anyspec] * nw,
        scratch_shapes=[pltpu.SemaphoreType.DMA((nw, 6)), pltpu.SemaphoreType.DMA((nw, 6)),
                        pltpu.SemaphoreType.DMA((nw, 2)), pltpu.SemaphoreType.DMA((nw, 2))]
        + [pltpu.VMEM((2,) + s.shape[1:], s.dtype) for s in shards],
        name=name,
        compiler_params=pltpu.CompilerParams(vmem_limit_bytes=VMEM_LIMIT),
    )(*shards)


def _exchange_halves(grads, name):
    nw = len(grads)
    out_shapes = [jax.ShapeDtypeStruct((g.shape[0] // 2,) + g.shape[1:], g.dtype) for g in grads]

    def body(*refs):
        ins, outs = refs[:nw], refs[nw : 2 * nw]
        send_sems, recv_sems = refs[2 * nw :]
        x, y, c = _mesh_pos()
        cps = []
        for w in range(nw):
            h = grads[w].shape[0] // 2
            cp = pltpu.make_async_remote_copy(
                src_ref=ins[w].at[pl.ds((1 - c) * h, h)], dst_ref=outs[w], send_sem=send_sems.at[w],
                recv_sem=recv_sems.at[w], device_id=(x, y, 1 - c), device_id_type=MESH_ID)
            cp.start()
            cps.append(cp)
        for cp in cps:
            cp.wait()

    anyspec = pl.BlockSpec(memory_space=pl.ANY)
    return pl.pallas_call(
        body,
        out_shape=out_shapes,
        in_specs=[anyspec] * nw,
        out_specs=[anyspec] * nw,
        scratch_shapes=[pltpu.SemaphoreType.DMA((nw,)), pltpu.SemaphoreType.DMA((nw,))],
        name=name,
    )(*grads)


def _scatter_to_chips(parts, kinds, name):
    nw = len(parts)
    shard_shapes = []
    for p, kind in zip(parts, kinds):
        h, r, c = p.shape
        shard_shapes.append((h, r // 4, c) if kind == "row" else (h, r, c // 4))
    out_shapes = [jax.ShapeDtypeStruct((3,) + s, p.dtype) for s, p in zip(shard_shapes, parts)]

    def body(*refs):
        ins, outs = refs[:nw], refs[nw : 2 * nw]
        send_sems, recv_sems = refs[2 * nw :]
        x, y, c = _mesh_pos()
        chips = _other_chips(x, y)

        def piece(w, chip):
            h, r, cc = shard_shapes[w]
            if kinds[w] == "row":
                return _shard_window(ins[w], pl.ds(0, h), chip, r, None)
            return _shard_window(ins[w], pl.ds(0, h), chip, None, cc)

        def copy(w, j, cx, cy):
            return pltpu.make_async_remote_copy(
                src_ref=piece(w, 2 * cx + cy), dst_ref=outs[w].at[j], send_sem=send_sems.at[w, j],
                recv_sem=recv_sems.at[w, j], device_id=(cx, cy, c), device_id_type=MESH_ID)

        cps = [copy(w, j, cx, cy) for w in range(nw) for j, (cx, cy) in enumerate(chips)]
        for cp in cps:
            cp.start()
        for cp in cps:
            cp.wait()

    anyspec = pl.BlockSpec(memory_space=pl.ANY)
    return pl.pallas_call(
        body,
        out_shape=out_shapes,
        in_specs=[anyspec] * nw,
        out_specs=[anyspec] * nw,
        scratch_shapes=[pltpu.SemaphoreType.DMA((nw, 3)), pltpu.SemaphoreType.DMA((nw, 3))],
        name=name,
    )(*parts)


def _swap_halves(halves, name):
    nw = len(halves)
    out_shapes = [jax.ShapeDtypeStruct(p.shape, p.dtype) for p in halves]

    def body(*refs):
        ins, outs = refs[:nw], refs[nw : 2 * nw]
        send_sems, recv_sems = refs[2 * nw :]
        x, y, c = _mesh_pos()
        cps = [pltpu.make_async_remote_copy(
            src_ref=ins[w], dst_ref=outs[w], send_sem=send_sems.at[w], recv_sem=recv_sems.at[w],
            device_id=(x, y, 1 - c), device_id_type=MESH_ID) for w in range(nw)]
        for cp in cps:
            cp.start()
        for cp in cps:
            cp.wait()

    anyspec = pl.BlockSpec(memory_space=pl.ANY)
    return pl.pallas_call(
        body,
        out_shape=out_shapes,
        in_specs=[anyspec] * nw,
        out_specs=[anyspec] * nw,
        scratch_shapes=[pltpu.SemaphoreType.DMA((nw,)), pltpu.SemaphoreType.DMA((nw,))],
        name=name,
    )(*halves)


def _pad_wdq(w):
    z = lambda n: jnp.zeros((w.shape[0], n), w.dtype)
    base = Q_LORA + KV_LORA
    return jnp.concatenate([w[:, :base], z(KPE_LANE), w[:, base:], z(HEAD_PAD - KPE_LANE - QK_ROPE)], axis=1)


def _unpad_wdq(g):
    base = Q_LORA + KV_LORA
    return jnp.concatenate([g[:, :base], g[:, base + KPE_LANE : base + KPE_LANE + QK_ROPE]], axis=1)


def _pad_wuq(w):
    w3 = w.reshape(Q_LORA, MLA_HEADS, QK_NOPE + QK_ROPE)
    w3 = jnp.pad(w3, ((0, 0), (0, 0), (0, HEAD_PAD - QK_NOPE - QK_ROPE)))
    return w3.reshape(Q_LORA, MLA_HEADS * HEAD_PAD)


def _unpad_wuq(g):
    g3 = g.reshape(Q_LORA, MLA_HEADS, HEAD_PAD)[:, :, : QK_NOPE + QK_ROPE]
    return g3.reshape(Q_LORA, MLA_HEADS * (QK_NOPE + QK_ROPE))


def _pad_wo(w):
    w3 = w.reshape(MLA_HEADS, V_HEAD, D_MODEL)
    w3 = jnp.pad(w3, ((0, 0), (HEAD_PAD - V_HEAD, 0), (0, 0)))
    return w3.reshape(MLA_HEADS * HEAD_PAD, D_MODEL)


def _unpad_wo(g):
    g3 = g.reshape(MLA_HEADS, HEAD_PAD, D_MODEL)[:, HEAD_PAD - V_HEAD :, :]
    return g3.reshape(MLA_HEADS * V_HEAD, D_MODEL)


def _local_step(x, mem, positions, target, wb, ws):
    t = x.shape[0]
    tables = _rope_tables(positions.reshape(t, 1), "rope_tables")
    saved = []
    h1 = _rms_fwd(x, ws["norm_mix_g"][0], "l0_norm_mix")
    for l in range(DEPTH):
        s = {"x0": x}
        s["h1"] = h1
        if l % 2 == 0:
            e = l // 2
            z = _matmul(h1, wb["pc_w_in"], "nn", F32, f"l{l}_pc_in", layer=e)
            cat = _mix_fwd(z, ws["pool_w"][e], ws["pool_scale"][e], ws["conv_dw_w"][e], ws["conv_dw_b"][e],
                           ws["conv_ln_g"][e], ws["conv_ln_b"][e], f"l{l}_mix")
            x, h2 = _matmul_res_norm(cat, wb["pc_w_out"], e, x, ws["norm_xa_g"][l], f"l{l}_pc_out")
            s.update(z=z, cat=cat)
        else:
            o = l // 2
            cp = _matmul(h1, wb["mla_wdq"], "nn", F32, f"l{l}_mla_dq", layer=o)
            qn, kvn, kpe = _mla_prep(cp, ws["mla_q_norm_g"][o], ws["mla_kv_norm_g"][o], tables, f"l{l}_mla_prep")
            q = _matmul(qn, wb["mla_wuq"], "nn", F32, f"l{l}_mla_uq", layer=o)
            qr = _rope_heads(q, tables, 1.0, MLA_SCALE * LOG2E, f"l{l}_mla_rope")
            kv = _matmul(kvn, wb["mla_w_ukv"], "nn", MXU_DT, f"l{l}_mla_ukv", layer=o)
            att, lse = _flash_fwd(qr, kv, kpe, f"l{l}_mla_attn")
            x, h2 = _matmul_res_norm(att, wb["mla_wo"], o, x, ws["norm_xa_g"][l], f"l{l}_mla_o")
            s.update(cp=cp, qn=qn, kvn=kvn, kpe=kpe, qr=qr, kv=kv, att=att, lse=lse)
        s["x1"] = x
        hm = _rms_fwd(mem, ws["norm_mem_g"][l], f"l{l}_norm_mem")
        q2 = _matmul(h2, wb["xa_wq"], "nn", MXU_DT, f"l{l}_xa_q", layer=l)
        kvm = _matmul(hm, wb["xa_wkv"], "nn", MXU_DT, f"l{l}_xa_kv", layer=l)
        o2 = _xa_fwd(q2, kvm, f"l{l}_xa_attn")
        x, h3 = _matmul_res_norm(o2, wb["xa_wo"], l, x, ws["norm_ffn_g"][l], f"l{l}_xa_o")
        s.update(h2=h2, hm=hm, q2=q2, kvm=kvm, o2=o2, x2=x)
        up = _matmul(h3, wb["ffn_w_up"], "nn", F32, f"l{l}_ffn_up", layer=l)
        act = _ffn_fwd(up, ws["ffn_conv_w"][l], ws["ffn_conv_b"][l], f"l{l}_ffn_mid")
        if l + 1 < DEPTH:
            x, h1 = _matmul_res_norm(act, wb["ffn_w_down"], l, x, ws["norm_mix_g"][l + 1], f"l{l}_ffn_down")
        else:
            x = _matmul(act, wb["ffn_w_down"], "nn", F32, f"l{l}_ffn_down", layer=l, res=x)
        s.update(h3=h3, up=up, act=act)
        saved.append(s)

    dx, dg_final, loss = _loss_head(x, target, ws["final_norm_g"], "loss_head")
    g = {k: [None] * DEPTH for k in ("norm_mix_g", "norm_xa_g", "norm_mem_g", "xa_wq", "xa_wkv", "xa_wo", "norm_ffn_g",
                                      "ffn_w_up", "ffn_conv_w", "ffn_conv_b", "ffn_w_down")}
    g.update({k: [None] * (DEPTH // 2) for k in ("pc_w_in", "pool_w", "pool_scale", "conv_dw_w", "conv_dw_b", "conv_ln_g",
                                                 "conv_ln_b", "pc_w_out", "mla_w_dq_dkv", "mla_q_norm_g", "mla_w_uq",
                                                 "mla_kv_norm_g", "mla_w_ukv", "mla_w_o")})
    stk = {k: None for k in ("xa_wq", "xa_wkv", "xa_wo", "ffn_w_up", "ffn_w_down", "pc_w_in", "pc_w_out")}
    for l in reversed(range(DEPTH)):
        s = saved[l]
        dact = _matmul(dx, wb["ffn_w_down"], "nt", F32, f"l{l}_b_ffn_dact", layer=l)
        stk["ffn_w_down"] = _matmul(s["act"], dx, "tn", F32, f"l{l}_b_ffn_dwdown", stack=(stk["ffn_w_down"], l, DEPTH))
        dup, dcw, dcb = _ffn_bwd(s["up"], dact, ws["ffn_conv_w"][l], ws["ffn_conv_b"][l], f"l{l}_b_ffn_mid")
        g["ffn_conv_w"][l], g["ffn_conv_b"][l] = dcw, dcb[0]
        stk["ffn_w_up"] = _matmul(s["h3"], dup, "tn", F32, f"l{l}_b_ffn_dwup", stack=(stk["ffn_w_up"], l, DEPTH))
        dx, dg = _matmul_rms_bwd(dup, wb["ffn_w_up"], l, s["x2"], ws["norm_ffn_g"][l], dx, f"l{l}_b_ffn_dh")
        g["norm_ffn_g"][l] = dg[0]
        do2 = _matmul(dx, wb["xa_wo"], "nt", MXU_DT, f"l{l}_b_xa_do", layer=l)
        stk["xa_wo"] = _matmul(s["o2"], dx, "tn", F32, f"l{l}_b_xa_dwo", stack=(stk["xa_wo"], l, DEPTH))
        dq2, dkvm = _xa_bwd(s["q2"], s["kvm"], do2, f"l{l}_b_xa_attn")
        stk["xa_wq"] = _matmul(s["h2"], dq2, "tn", F32, f"l{l}_b_xa_dwq", stack=(stk["xa_wq"], l, DEPTH))
        stk["xa_wkv"] = _matmul(s["hm"], dkvm, "tn", F32, f"l{l}_b_xa_dwkv", stack=(stk["xa_wkv"], l, DEPTH))
        dhm = _matmul(dkvm, wb["xa_wkv"], "nt", F32, f"l{l}_b_xa_dhm", layer=l)
        g["norm_mem_g"][l] = _rms_bwd_gain(dhm, mem, ws["norm_mem_g"][l], f"l{l}_b_norm_mem")[0]
        dx, dg = _matmul_rms_bwd(dq2, wb["xa_wq"], l, s["x1"], ws["norm_xa_g"][l], dx, f"l{l}_b_xa_dh")
        g["norm_xa_g"][l] = dg[0]
        if l % 2 == 0:
            e = l // 2
            dcat = _matmul(dx, wb["pc_w_out"], "nt", F32, f"l{l}_b_pc_dcat", layer=e)
            stk["pc_w_out"] = _matmul(s["cat"], dx, "tn", F32, f"l{l}_b_pc_dwout", stack=(stk["pc_w_out"], e, DEPTH // 2))
            dz, dpw, dps, dww, dwb, dlg, dlb = _mix_bwd(
                s["z"], dcat, ws["pool_w"][e], ws["pool_scale"][e], ws["conv_dw_w"][e], ws["conv_dw_b"][e],
                ws["conv_ln_g"][e], ws["conv_ln_b"][e], f"l{l}_b_mix")
            g["pool_w"][e], g["pool_scale"][e], g["conv_dw_w"][e] = dpw, dps[0], dww
            g["conv_dw_b"][e], g["conv_ln_g"][e], g["conv_ln_b"][e] = dwb[0], dlg[0], dlb[0]
            stk["pc_w_in"] = _matmul(s["h1"], dz, "tn", F32, f"l{l}_b_pc_dwin", stack=(stk["pc_w_in"], e, DEPTH // 2))
            dx, dg = _matmul_rms_bwd(dz, wb["pc_w_in"], e, s["x0"], ws["norm_mix_g"][l], dx, f"l{l}_b_pc_dh")
        else:
            o = l // 2
            do = _matmul(dx, wb["mla_wo"], "nt", MXU_DT, f"l{l}_b_mla_do", layer=o)
            g["mla_w_o"][o] = _unpad_wo(_matmul(s["att"], dx, "tn", F32, f"l{l}_b_mla_dwo"))
            dqr, dkv, dkpe = _flash_bwd(s["qr"], s["kv"], s["kpe"], s["att"], do, s["lse"], f"l{l}_b_mla_attn")
            dq = _rope_heads(dqr, tables, -1.0, MLA_SCALE, f"l{l}_b_mla_rope")
            g["mla_w_uq"][o] = _unpad_wuq(_matmul(s["qn"], dq, "tn", F32, f"l{l}_b_mla_dwuq"))
            dqn = _matmul(dq, wb["mla_wuq"], "nt", F32, f"l{l}_b_mla_dqn", layer=o)
            g["mla_w_ukv"][o] = _matmul(s["kvn"], dkv, "tn", F32, f"l{l}_b_mla_dwukv")
            dkvn = _matmul(dkv, wb["mla_w_ukv"], "nt", F32, f"l{l}_b_mla_dkvn", layer=o)
            dcp, dqg, dkvg = _mla_prep_bwd(s["cp"], dqn, dkvn, dkpe, ws["mla_q_norm_g"][o], ws["mla_kv_norm_g"][o],
                                           tables, f"l{l}_b_mla_prep")
            g["mla_q_norm_g"][o], g["mla_kv_norm_g"][o] = dqg[0], dkvg[0]
            g["mla_w_dq_dkv"][o] = _unpad_wdq(_matmul(s["h1"], dcp, "tn", F32, f"l{l}_b_mla_dwdq"))
            dx, dg = _matmul_rms_bwd(dcp, wb["mla_wdq"], o, s["x0"], ws["norm_mix_g"][l], dx, f"l{l}_b_mla_dh")
        g["norm_mix_g"][l] = dg[0]
    grads = {k: jnp.stack(v) for k, v in g.items() if k not in stk}
    grads.update(stk)
    grads["final_norm_g"] = dg_final[0]
    return loss, dx, grads


BIG = (
    ("xa_wq", "row"), ("xa_wkv", "col"), ("xa_wo", "row"), ("ffn_w_up", "col"), ("ffn_w_down", "row"),
    ("pc_w_in", "col"), ("pc_w_out", "row"), ("mla_w_dq_dkv", "row"), ("mla_w_uq", "col"), ("mla_w_ukv", "col"),
    ("mla_w_o", "row"),
)
SMALL_SHARDED = ("ffn_conv_w", "conv_dw_w", "mla_q_norm_g", "mla_kv_norm_g")
SMALL_REPLICATED = ("norm_mix_g", "norm_xa_g", "norm_mem_g", "norm_ffn_g", "ffn_conv_b", "pool_w", "pool_scale",
                    "conv_dw_b", "conv_ln_g", "conv_ln_b", "final_norm_g")
WEIGHTS = ("norm_mix_g", "norm_xa_g", "norm_mem_g", "xa_wq", "xa_wkv", "xa_wo", "norm_ffn_g", "ffn_w_up", "ffn_conv_w",
           "ffn_conv_b", "ffn_w_down", "pc_w_in", "pool_w", "pool_scale", "conv_dw_w", "conv_dw_b", "conv_ln_g",
           "conv_ln_b", "pc_w_out", "mla_w_dq_dkv", "mla_q_norm_g", "mla_w_uq", "mla_kv_norm_g", "mla_w_ukv", "mla_w_o",
           "final_norm_g")
PACK_ROW = SUBLANE * LANE


def _pack(arrays):
    flat = jnp.concatenate([a.reshape(-1) for a in arrays])
    n = flat.shape[0]
    pad = (-n) % PACK_ROW
    return jnp.pad(flat, (0, pad)).reshape(-1, LANE)


def _unpack(flat, shapes):
    out, off = [], 0
    for s in shapes:
        n = int(np.prod(s))
        out.append(flat[off : off + n].reshape(s))
        off += n
    return out


def kernel(x, mem, positions, norm_mix_g, norm_xa_g, norm_mem_g, xa_wq, xa_wkv, xa_wo, norm_ffn_g, ffn_w_up, ffn_conv_w, ffn_conv_b, ffn_w_down, pc_w_in, pool_w, pool_scale, conv_dw_w, conv_dw_b, conv_ln_g, conv_ln_b, pc_w_out, mla_w_dq_dkv, mla_q_norm_g, mla_w_uq, mla_kv_norm_g, mla_w_ukv, mla_w_o, final_norm_g, loss_target, m_norm_mix_g, m_norm_xa_g, m_norm_mem_g, m_xa_wq, m_xa_wkv, m_xa_wo, m_norm_ffn_g, m_ffn_w_up, m_ffn_conv_w, m_ffn_conv_b, m_ffn_w_down, m_pc_w_in, m_pool_w, m_pool_scale, m_conv_dw_w, m_conv_dw_b, m_conv_ln_g, m_conv_ln_b, m_pc_w_out, m_mla_w_dq_dkv, m_mla_q_norm_g, m_mla_w_uq, m_mla_kv_norm_g, m_mla_w_ukv, m_mla_w_o, m_final_norm_g, v_norm_mix_g, v_norm_xa_g, v_norm_mem_g, v_xa_wq, v_xa_wkv, v_xa_wo, v_norm_ffn_g, v_ffn_w_up, v_ffn_conv_w, v_ffn_conv_b, v_ffn_w_down, v_pc_w_in, v_pool_w, v_pool_scale, v_conv_dw_w, v_conv_dw_b, v_conv_ln_g, v_conv_ln_b, v_pc_w_out, v_mla_w_dq_dkv, v_mla_q_norm_g, v_mla_w_uq, v_mla_kv_norm_g, v_mla_w_ukv, v_mla_w_o, v_final_norm_g):
    args = dict(locals())
    w = {n: args[n] for n in WEIGHTS}
    m = {n: args["m_" + n] for n in WEIGHTS}
    v = {n: args["v_" + n] for n in WEIGHTS}
    cx, cy, cc = lax.axis_index("x"), lax.axis_index("y"), lax.axis_index("c")
    chip = 2 * cx + cy

    full = _all_gather_weights([w[n].astype(MXU_DT) for n, _ in BIG], [k for _, k in BIG], "gather_weights")
    full = dict(zip([n for n, _ in BIG], full))
    small_shapes = [w[n].shape for n in SMALL_SHARDED]
    gathered = _all_gather_rows(_pack([w[n] for n in SMALL_SHARDED]), "gather_small")
    gathered = gathered.reshape(8, -1)
    ws = {n: w[n] for n in SMALL_REPLICATED}
    pieces = [_unpack(gathered[2 * k], small_shapes) for k in range(4)]
    for i, n in enumerate(SMALL_SHARDED):
        ws[n] = jnp.concatenate([pieces[k][i] for k in range(4)], axis=-1)
    wb = {n: full[n] for n in ("xa_wq", "xa_wkv", "xa_wo", "ffn_w_up", "ffn_w_down", "pc_w_in", "pc_w_out", "mla_w_ukv")}
    wb["mla_wdq"] = jnp.stack([_pad_wdq(full["mla_w_dq_dkv"][o]) for o in range(DEPTH // 2)])
    wb["mla_wuq"] = jnp.stack([_pad_wuq(full["mla_w_uq"][o]) for o in range(DEPTH // 2)])
    wb["mla_wo"] = jnp.stack([_pad_wo(full["mla_w_o"][o]) for o in range(DEPTH // 2)])

    loss, grad_x, grads = _local_step(x[0], mem[0], positions[0], loss_target[0], wb, ws)
    loss = lax.psum(loss[0, 0], ("x", "y", "c"))

    kinds = [k for _, k in BIG]
    big = [grads[n] for n, _ in BIG]
    c_idx = cc.reshape(1).astype(jnp.int32)
    chip_idx = chip.reshape(1).astype(jnp.int32)
    theirs = _exchange_halves(big, "reduce_pair")
    pair = [_add_half(gr, th, c_idx, f"reduce_pair_add_{n}") for gr, th, (n, _) in zip(big, theirs, BIG)]
    slots = _scatter_to_chips(pair, kinds, "reduce_chips")
    halves = [_sum_chips(sl, pr, chip_idx, kind, f"reduce_chips_add_{n}")
              for sl, pr, (n, kind) in zip(slots, pair, BIG)]
    others = _swap_halves(halves, "reduce_join")
    gsum, delta, new_m, new_v = {}, {}, {}, {}
    for mine, other, (n, _) in zip(halves, others, BIG):
        gsum[n], delta[n], new_m[n], new_v[n] = _adamw_halves(w[n], mine, other, m[n], v[n], c_idx, f"adamw_{n}")

    small_names = SMALL_REPLICATED + SMALL_SHARDED
    small_grad_shapes = [grads[n].shape for n in small_names]
    packed = _pack([grads[n] for n in small_names])
    rows = packed.shape[0]
    allparts = _all_gather_rows(packed, "gather_small_grads").reshape(8, rows, LANE)
    total = _sum_leading(allparts, "sum_small_grads").reshape(-1)
    for n, gfull in zip(small_names, _unpack(total, small_grad_shapes)):
        if n in SMALL_SHARDED:
            width = w[n].shape[-1]
            gfull = lax.dynamic_slice_in_dim(gfull, chip * width, width, axis=gfull.ndim - 1)
        gsum[n] = gfull

    for n in SMALL_REPLICATED + SMALL_SHARDED:
        delta[n], new_m[n], new_v[n] = _adamw(w[n], gsum[n], m[n], v[n], f"adamw_{n}")
    return (loss, grad_x[None], *[gsum[n] for n in WEIGHTS], *[delta[n] for n in WEIGHTS],
            *[new_m[n] for n in WEIGHTS], *[new_v[n] for n in WEIGHTS])
```

```python
import functools
import math

import numpy as np
import jax
import jax.numpy as jnp
from jax import lax
from jax.experimental import pallas as pl
from jax.experimental.pallas import tpu as pltpu

F32 = jnp.float32
MXU_DT = jnp.bfloat16
XFER_DT = jnp.bfloat16

D_MODEL = 1024
DEPTH = 4
MEM_LEN = 256
XA_HEADS = 4
XA_HEAD_DIM = 256
POOL_W = 512
POOL_WINDOWS = (2, 4, 8, 16)
POOL_GROUP = 128
CONV_W = 512
CONV_K = 31
MLA_HEADS = 16
QK_NOPE = 64
QK_ROPE = 32
V_HEAD = 64
Q_LORA = 384
KV_LORA = 256
ROPE_THETA = 10000.0
MLA_SCALE = 1.0 / math.sqrt(QK_NOPE + QK_ROPE)
LOG2E = math.log2(math.e)
D_FF = 2816
FFN_CONV_K = 3
EPS = 1e-6
NEG = -1e30
ADAM_LR = 0.001
ADAM_B1 = 0.9
ADAM_B2 = 0.999
ADAM_EPS = 1e-08
ADAM_WD = 0.01
ADAM_STEP = 10

HEAD_PAD = 128
C_PAD = 768
KPE_LANE = 64

VMEM_LIMIT = 52 * 1024 * 1024
BLOCK_BYTES = 6 * 1024 * 1024
LANE = 128
SUBLANE = 8

TM = 1024
TN = 1408
TK = 2048
TT = 512
TW = 256
TWF = 128
FFN_CHUNK = 256
TA = 1024
MIX_HALO = 32
FFN_HALO = 8

NN = (((1,), (0,)), ((), ()))
NT = (((1,), (1,)), ((), ()))
TN_DIMS = (((0,), (0,)), ((), ()))
MESH_ID = pl.DeviceIdType.MESH


def _cparams(*sem):
    return pltpu.CompilerParams(dimension_semantics=sem, vmem_limit_bytes=VMEM_LIMIT)


def _tile(n, pref, limit=None):
    cap = pref if limit is None else min(pref, limit)
    if n <= cap:
        return n
    t = (cap // LANE) * LANE
    while t >= LANE:
        if n % t == 0:
            return t
        t -= LANE
    return n


def _rows(t, pref):
    return t if t <= pref else pref


def _sigmoid(x):
    return 1.0 / (1.0 + jnp.exp(-x))


def _matmul(a, b, mode, out_dtype, name, layer=None, res=None, stack=None):
    if layer is None:
        b2 = b.shape
    else:
        b2 = b.shape[1:]
    if mode == "tn":
        k, m = a.shape
        k2, n = b2
    elif mode == "nn":
        m, k = a.shape
        k2, n = b2
    else:
        m, k = a.shape
        n, k2 = b2
    assert k == k2, (a.shape, b.shape, mode)
    isz_a = jnp.dtype(a.dtype).itemsize
    isz_b = jnp.dtype(b.dtype).itemsize
    if mode == "tn":
        tk = _tile(k, TK * 2 // max(isz_a, isz_b))
        tm = _tile(m, TN, BLOCK_BYTES // (tk * isz_a))
        tn = _tile(n, TN, BLOCK_BYTES // (tk * isz_b))
    else:
        tk = k
        tn = _tile(n, TN, BLOCK_BYTES // (tk * isz_b))
        tm = _tile(m, TM, min(BLOCK_BYTES // (tk * isz_a), BLOCK_BYTES // (tn * 4)))
    nk = k // tk
    grid = (m // tm, n // tn, nk)
    if mode == "nn":
        a_spec = pl.BlockSpec((tm, tk), lambda i, j, kk: (i, kk))
        b_blk, b_map, dn = (tk, tn), (lambda i, j, kk: (kk, j)), NN
    elif mode == "nt":
        a_spec = pl.BlockSpec((tm, tk), lambda i, j, kk: (i, kk))
        b_blk, b_map, dn = (tn, tk), (lambda i, j, kk: (j, kk)), NT
    else:
        a_spec = pl.BlockSpec((tk, tm), lambda i, j, kk: (kk, i))
        b_blk, b_map, dn = (tk, tn), (lambda i, j, kk: (kk, j)), TN_DIMS
    if layer is None:
        b_spec = pl.BlockSpec(b_blk, b_map)
    else:
        b_spec = pl.BlockSpec((None,) + b_blk, lambda i, j, kk: (layer,) + b_map(i, j, kk))
    in_specs = [a_spec, b_spec]
    args = [a, b]
    if res is not None:
        in_specs.append(pl.BlockSpec((tm, tn), lambda i, j, kk: (i, j)))
        args.append(res)
    has_res = res is not None
    aliases = {}
    if stack is None:
        o_spec = pl.BlockSpec((tm, tn), lambda i, j, kk: (i, j))
        out_shape = jax.ShapeDtypeStruct((m, n), out_dtype)
    else:
        buf, slab, nslab = stack
        o_spec = pl.BlockSpec((None, tm, tn), lambda i, j, kk: (slab, i, j))
        out_shape = jax.ShapeDtypeStruct((nslab, m, n), out_dtype)
        if buf is not None:
            in_specs.append(pl.BlockSpec(memory_space=pl.ANY))
            args.append(buf)
            aliases = {len(args) - 1: 0}
    n_in = len(args)

    def body(*refs):
        a_ref, b_ref = refs[0], refs[1]
        r_ref = refs[2] if has_res else None
        o_ref = refs[n_in]
        p = lax.dot_general(a_ref[...].astype(MXU_DT), b_ref[...].astype(MXU_DT), dn, preferred_element_type=F32)
        if nk == 1:
            if has_res:
                p = r_ref[...] + p
            o_ref[...] = p.astype(o_ref.dtype)
        else:
            acc_ref = refs[-1]
            kk = pl.program_id(2)

            @pl.when(kk == 0)
            def _():
                acc_ref[...] = jnp.zeros_like(acc_ref)

            acc_ref[...] += p

            @pl.when(kk == nk - 1)
            def _():
                r = acc_ref[...]
                if has_res:
                    r = r_ref[...] + r
                o_ref[...] = r.astype(o_ref.dtype)

    scratch = [pltpu.VMEM((tm, tn), F32)] if nk > 1 else []
    return pl.pallas_call(
        body,
        grid=grid,
        in_specs=in_specs,
        out_specs=o_spec,
        out_shape=out_shape,
        scratch_shapes=scratch,
        input_output_aliases=aliases,
        name=name,
        compiler_params=_cparams("parallel", "parallel", "arbitrary"),
    )(*args)


def _row_tile(m, k, isz):
    return _tile(m, TM, min(BLOCK_BYTES // (k * isz), TM if k <= 2 * TK else TM // 4))


def _matmul_res_norm(a, b, layer, res, gain, name):
    m, k = a.shape
    n = b.shape[-1]
    tm = _row_tile(m, k, jnp.dtype(a.dtype).itemsize)

    def body(a_ref, b_ref, r_ref, g_ref, x_ref, h_ref):
        x = r_ref[...] + jnp.dot(a_ref[...].astype(MXU_DT), b_ref[...].astype(MXU_DT), preferred_element_type=F32)
        x_ref[...] = x
        r = lax.rsqrt(jnp.mean(x * x, axis=-1, keepdims=True) + EPS)
        h_ref[...] = ((x * r) * g_ref[...]).astype(h_ref.dtype)

    row = pl.BlockSpec((tm, n), lambda i: (i, 0))
    return pl.pallas_call(
        body,
        grid=(m // tm,),
        in_specs=[pl.BlockSpec((tm, k), lambda i: (i, 0)), pl.BlockSpec((None, k, n), lambda i: (layer, 0, 0)), row,
                  pl.BlockSpec((1, n), lambda i: (0, 0))],
        out_specs=[row, row],
        out_shape=[jax.ShapeDtypeStruct((m, n), F32), jax.ShapeDtypeStruct((m, n), MXU_DT)],
        name=name,
        compiler_params=_cparams("parallel"),
    )(a, b, res, gain.reshape(1, n))


def _matmul_rms_bwd(a, b, layer, x, gain, dx_in, name):
    m, k = a.shape
    n = b.shape[-2]
    tm = _row_tile(m, k, jnp.dtype(a.dtype).itemsize)

    def body(a_ref, b_ref, x_ref, g_ref, dxi_ref, dx_ref, dx16_ref, dg_ref):
        @pl.when(pl.program_id(0) == 0)
        def _():
            dg_ref[...] = jnp.zeros_like(dg_ref)

        dh = lax.dot_general(a_ref[...].astype(MXU_DT), b_ref[...].astype(MXU_DT), NT, preferred_element_type=F32)
        xf = x_ref[...]
        r = lax.rsqrt(jnp.mean(xf * xf, axis=-1, keepdims=True) + EPS)
        xh = xf * r
        gy = dh * g_ref[...]
        dx = dxi_ref[...] + r * (gy - xh * jnp.mean(gy * xh, axis=-1, keepdims=True))
        dx_ref[...] = dx
        dx16_ref[...] = dx.astype(dx16_ref.dtype)
        dg_ref[...] += jnp.sum(dh * xh, axis=0, keepdims=True)

    row = pl.BlockSpec((tm, n), lambda i: (i, 0))
    vec = pl.BlockSpec((1, n), lambda i: (0, 0))
    return pl.pallas_call(
        body,
        grid=(m // tm,),
        in_specs=[pl.BlockSpec((tm, k), lambda i: (i, 0)), pl.BlockSpec((None, n, k), lambda i: (layer, 0, 0)), row, vec, row],
        out_specs=[row, row, vec],
        out_shape=[jax.ShapeDtypeStruct((m, n), F32), jax.ShapeDtypeStruct((m, n), MXU_DT), jax.ShapeDtypeStruct((1, n), F32)],
        name=name,
        compiler_params=_cparams("arbitrary"),
    )(a, b, x, gain.reshape(1, n), dx_in)


def _rms_fwd(x, g, name):
    t, d = x.shape
    tt = _rows(t, TT)

    def body(x_ref, g_ref, o_ref):
        xf = x_ref[...]
        r = lax.rsqrt(jnp.mean(xf * xf, axis=-1, keepdims=True) + EPS)
        o_ref[...] = ((xf * r) * g_ref[...]).astype(o_ref.dtype)

    return pl.pallas_call(
        body,
        grid=(t // tt,),
        in_specs=[pl.BlockSpec((tt, d), lambda i: (i, 0)), pl.BlockSpec((1, d), lambda i: (0, 0))],
        out_specs=pl.BlockSpec((tt, d), lambda i: (i, 0)),
        out_shape=jax.ShapeDtypeStruct((t, d), MXU_DT),
        name=name,
        compiler_params=_cparams("parallel"),
    )(x, g.reshape(1, d))


def _rms_bwd(dh, x, g, dx_in, name):
    t, d = x.shape
    tt = _rows(t, TT)

    def body(dh_ref, x_ref, g_ref, dxi_ref, dx_ref, dg_ref):
        @pl.when(pl.program_id(0) == 0)
        def _():
            dg_ref[...] = jnp.zeros_like(dg_ref)

        xf = x_ref[...]
        dh_v = dh_ref[...]
        r = lax.rsqrt(jnp.mean(xf * xf, axis=-1, keepdims=True) + EPS)
        xh = xf * r
        gy = dh_v * g_ref[...]
        dx = r * (gy - xh * jnp.mean(gy * xh, axis=-1, keepdims=True))
        dx_ref[...] = dxi_ref[...] + dx
        dg_ref[...] += jnp.sum(dh_v * xh, axis=0, keepdims=True)

    row = pl.BlockSpec((tt, d), lambda i: (i, 0))
    vec = pl.BlockSpec((1, d), lambda i: (0, 0))
    return pl.pallas_call(
        body,
        grid=(t // tt,),
        in_specs=[row, row, vec, row],
        out_specs=[row, vec],
        out_shape=[jax.ShapeDtypeStruct((t, d), F32), jax.ShapeDtypeStruct((1, d), F32)],
        name=name,
        compiler_params=_cparams("arbitrary"),
    )(dh, x, g.reshape(1, d), dx_in)


def _rms_bwd_gain(dh, x, g, name):
    t, d = x.shape
    tt = _rows(t, TT)

    def body(dh_ref, x_ref, dg_ref):
        @pl.when(pl.program_id(0) == 0)
        def _():
            dg_ref[...] = jnp.zeros_like(dg_ref)

        xf = x_ref[...]
        r = lax.rsqrt(jnp.mean(xf * xf, axis=-1, keepdims=True) + EPS)
        dg_ref[...] += jnp.sum(dh_ref[...] * (xf * r), axis=0, keepdims=True)

    row = pl.BlockSpec((tt, d), lambda i: (i, 0))
    vec = pl.BlockSpec((1, d), lambda i: (0, 0))
    return pl.pallas_call(
        body,
        grid=(t // tt,),
        in_specs=[row, row],
        out_specs=vec,
        out_shape=jax.ShapeDtypeStruct((1, d), F32),
        name=name,
        compiler_params=_cparams("arbitrary"),
    )(dh, x)


def _loss_head(x, target, g, name):
    t, d = x.shape
    tt = _rows(t, TT)

    def body(x_ref, t_ref, g_ref, dx_ref, dx16_ref, dg_ref, loss_ref):
        @pl.when(pl.program_id(0) == 0)
        def _():
            dg_ref[...] = jnp.zeros_like(dg_ref)
            loss_ref[...] = jnp.zeros_like(loss_ref)

        xf = x_ref[...]
        gv = g_ref[...]
        r = lax.rsqrt(jnp.mean(xf * xf, axis=-1, keepdims=True) + EPS)
        xh = xf * r
        err = xh * gv - t_ref[...]
        e2 = jnp.sum(err * err, axis=-1, keepdims=True)
        loss_ref[...] += (0.5 / d) * jnp.sum(e2, axis=0, keepdims=True)
        dy = err * (1.0 / d)
        gy = dy * gv
        dx = r * (gy - xh * jnp.mean(gy * xh, axis=-1, keepdims=True))
        dx_ref[...] = dx
        dx16_ref[...] = dx.astype(dx16_ref.dtype)
        dg_ref[...] += jnp.sum(dy * xh, axis=0, keepdims=True)

    row = pl.BlockSpec((tt, d), lambda i: (i, 0))
    vec = pl.BlockSpec((1, d), lambda i: (0, 0))
    return pl.pallas_call(
        body,
        grid=(t // tt,),
        in_specs=[row, row, vec],
        out_specs=[row, row, vec, pl.BlockSpec((1, 1), lambda i: (0, 0))],
        out_shape=[
            jax.ShapeDtypeStruct((t, d), F32),
            jax.ShapeDtypeStruct((t, d), MXU_DT),
            jax.ShapeDtypeStruct((1, d), F32),
            jax.ShapeDtypeStruct((1, 1), F32),
        ],
        name=name,
        compiler_params=_cparams("arbitrary"),
    )(x, target, g.reshape(1, d))


def _prev_halo(tt, hp, width):
    return pl.BlockSpec((hp, width), lambda i: (jnp.maximum(i * (tt // hp) - 1, 0), 0))


def _next_halo(tt, hp, width, t):
    return pl.BlockSpec((hp, width), lambda i: (jnp.minimum((i + 1) * (tt // hp), t // hp - 1), 0))


def _ffn_chunks():
    return [(c0, FFN_CHUNK) for c0 in range(0, D_FF, FFN_CHUNK)]


def _ffn_fwd(up, conv_w, conv_b, name):
    t = up.shape[0]
    tt = _rows(t, TWF)
    hp = FFN_HALO

    def body(up_ref, gp_ref, w_ref, b_ref, act_ref, ext_ref):
        first = pl.program_id(0) == 0
        for c0, cw in _ffn_chunks():
            ga = pl.ds(D_FF + c0, cw)
            ext_ref[0:hp, :] = jnp.where(first, 0.0, gp_ref[:, ga])
            ext_ref[hp : hp + tt, :] = up_ref[:, ga]
            gc = b_ref[:, pl.ds(c0, cw)]
            for j in range(FFN_CONV_K):
                off = hp - (FFN_CONV_K - 1) + j
                gc = gc + w_ref[j : j + 1, pl.ds(c0, cw)] * ext_ref[off : off + tt, :]
            a = up_ref[:, pl.ds(c0, cw)]
            act_ref[:, pl.ds(c0, cw)] = (gc * _sigmoid(gc) * a).astype(act_ref.dtype)

    return pl.pallas_call(
        body,
        grid=(t // tt,),
        in_specs=[
            pl.BlockSpec((tt, 2 * D_FF), lambda i: (i, 0)),
            _prev_halo(tt, hp, 2 * D_FF),
            pl.BlockSpec((FFN_CONV_K, D_FF), lambda i: (0, 0)),
            pl.BlockSpec((1, D_FF), lambda i: (0, 0)),
        ],
        out_specs=pl.BlockSpec((tt, D_FF), lambda i: (i, 0)),
        out_shape=jax.ShapeDtypeStruct((t, D_FF), MXU_DT),
        scratch_shapes=[pltpu.VMEM((tt + hp, FFN_CHUNK), F32)],
        name=name,
        compiler_params=_cparams("parallel"),
    )(up, up, conv_w, conv_b.reshape(1, D_FF))


def _ffn_bwd(up, dact, conv_w, conv_b, name):
    t = up.shape[0]
    tt = _rows(t, TWF)
    hp = FFN_HALO
    nt = t // tt
    kk = FFN_CONV_K

    def body(up_ref, upp_ref, upn_ref, da_ref, dan_ref, w_ref, b_ref, dup_ref, dw_ref, db_ref, ext_ref, dgc_ref):
        i = pl.program_id(0)
        first = i == 0
        last = i == nt - 1

        @pl.when(first)
        def _():
            dw_ref[...] = jnp.zeros_like(dw_ref)
            db_ref[...] = jnp.zeros_like(db_ref)

        for c0, cw in _ffn_chunks():
            ca = pl.ds(c0, cw)
            ga = pl.ds(D_FF + c0, cw)
            ext_ref[0:hp, :] = jnp.where(first, 0.0, upp_ref[:, ga])
            ext_ref[hp : hp + tt, :] = up_ref[:, ga]
            ext_ref[hp + tt : hp + tt + hp, :] = upn_ref[:, ga]
            gc = b_ref[:, ca]
            for j in range(kk):
                off = hp - (kk - 1) + j
                gc = gc + w_ref[j : j + 1, ca] * ext_ref[off : off + tt + hp, :]
            sg = _sigmoid(gc)
            silu = gc * sg
            dsilu = sg * (1.0 + gc * (1.0 - sg))
            a_all = jnp.concatenate([up_ref[:, ca], upn_ref[:, ca]], axis=0)
            dact_all = jnp.concatenate([da_ref[:, ca], jnp.where(last, 0.0, dan_ref[:, ca])], axis=0)
            dgc = dact_all * a_all * dsilu
            dgc_ref[...] = dgc
            dup_ref[:, ca] = (dact_all[0:tt] * silu[0:tt]).astype(dup_ref.dtype)
            dg = jnp.zeros((tt, cw), F32)
            for j in range(kk):
                dg = dg + w_ref[j : j + 1, ca] * dgc_ref[kk - 1 - j : kk - 1 - j + tt, :]
            dup_ref[:, ga] = dg.astype(dup_ref.dtype)
            dgc_t = dgc[0:tt]
            db_ref[:, ca] += jnp.sum(dgc_t, axis=0, keepdims=True)
            for j in range(kk):
                off = hp - (kk - 1) + j
                dw_ref[j : j + 1, ca] += jnp.sum(dgc_t * ext_ref[off : off + tt, :], axis=0, keepdims=True)

    return pl.pallas_call(
        body,
        grid=(nt,),
        in_specs=[
            pl.BlockSpec((tt, 2 * D_FF), lambda i: (i, 0)),
            _prev_halo(tt, hp, 2 * D_FF),
            _next_halo(tt, hp, 2 * D_FF, t),
            pl.BlockSpec((tt, D_FF), lambda i: (i, 0)),
            _next_halo(tt, hp, D_FF, t),
            pl.BlockSpec((kk, D_FF), lambda i: (0, 0)),
            pl.BlockSpec((1, D_FF), lambda i: (0, 0)),
        ],
        out_specs=[
            pl.BlockSpec((tt, 2 * D_FF), lambda i: (i, 0)),
            pl.BlockSpec((kk, D_FF), lambda i: (0, 0)),
            pl.BlockSpec((1, D_FF), lambda i: (0, 0)),
        ],
        out_shape=[
            jax.ShapeDtypeStruct((t, 2 * D_FF), MXU_DT),
            jax.ShapeDtypeStruct((kk, D_FF), F32),
            jax.ShapeDtypeStruct((1, D_FF), F32),
        ],
        scratch_shapes=[pltpu.VMEM((tt + 2 * hp, FFN_CHUNK), F32), pltpu.VMEM((tt + hp, FFN_CHUNK), F32)],
        name=name,
        compiler_params=_cparams("arbitrary"),
    )(up, up, up, dact, dact, conv_w, conv_b.reshape(1, D_FF))


def _layernorm_silu(cv, ln_g, ln_b):
    mu = jnp.mean(cv, axis=-1, keepdims=True)
    xc = cv - mu
    rstd = lax.rsqrt(jnp.mean(xc * xc, axis=-1, keepdims=True) + EPS)
    xh = xc * rstd
    a = xh * ln_g + ln_b
    return xh, rstd, a


def _shifted_copies(ref, n):
    for b in range(1, SUBLANE):
        ref[b, 0 : n - SUBLANE, :] = ref[0, b : b + n - SUBLANE, :]


def _tap(ref, offset, rows, cols):
    b = offset % SUBLANE
    return ref[b, offset - b : offset - b + rows, cols]


def _mix_fwd(z, pool_w, pool_scale, dw_w, dw_b, ln_g, ln_b, name):
    t = z.shape[0]
    tt = _rows(t, TW)
    hp = MIX_HALO
    zw = POOL_W + 2 * CONV_W

    def body(z_ref, zp_ref, pw_ref, ps_ref, w_ref, b_ref, lg_ref, lb_ref, cat_ref, eu_ref, egl_ref, cv_ref):
        i = pl.program_id(0)
        first = i == 0
        eu_ref[0:hp, :] = jnp.where(first, 0.0, zp_ref[:, 0:POOL_W])
        eu_ref[hp : hp + tt, :] = z_ref[:, 0:POOL_W]
        glp = zp_ref[:, POOL_W : POOL_W + CONV_W] * _sigmoid(zp_ref[:, POOL_W + CONV_W : zw])
        egl_ref[0, 0:hp, :] = jnp.where(first, 0.0, glp)
        egl_ref[0, hp : hp + tt, :] = z_ref[:, POOL_W : POOL_W + CONV_W] * _sigmoid(z_ref[:, POOL_W + CONV_W : zw])
        _shifted_copies(egl_ref, tt + hp)
        row = i * tt + lax.broadcasted_iota(jnp.int32, (tt, 1), 0)
        for gi, w in enumerate(POOL_WINDOWS):
            cols = pl.ds(gi * POOL_GROUP, POOL_GROUP)
            u = eu_ref[hp : hp + tt, cols]
            acc = u
            for k in range(1, w):
                acc = acc + eu_ref[hp - k : hp - k + tt, cols]
            cnt = jnp.minimum(row + 1, w).astype(F32)
            pooled = acc / cnt - u
            y = jnp.dot(pooled.astype(MXU_DT), pw_ref[gi].astype(MXU_DT), preferred_element_type=F32)
            cat_ref[:, cols] = (y * ps_ref[:, cols]).astype(cat_ref.dtype)
        for c0 in range(0, CONV_W, LANE):
            cs = pl.ds(c0, LANE)
            acc = jnp.broadcast_to(b_ref[:, cs], (tt, LANE))
            for j in range(CONV_K):
                acc = acc + w_ref[j : j + 1, cs] * _tap(egl_ref, hp - (CONV_K - 1) + j, tt, cs)
            cv_ref[:, cs] = acc
        _, _, a = _layernorm_silu(cv_ref[...], lg_ref[...], lb_ref[...])
        cat_ref[:, POOL_W : POOL_W + CONV_W] = (a * _sigmoid(a)).astype(cat_ref.dtype)

    vec = pl.BlockSpec((1, CONV_W), lambda i: (0, 0))
    return pl.pallas_call(
        body,
        grid=(t // tt,),
        in_specs=[
            pl.BlockSpec((tt, zw), lambda i: (i, 0)),
            _prev_halo(tt, hp, zw),
            pl.BlockSpec((len(POOL_WINDOWS), POOL_GROUP, POOL_GROUP), lambda i: (0, 0, 0)),
            vec,
            pl.BlockSpec((CONV_K, CONV_W), lambda i: (0, 0)),
            vec,
            vec,
            vec,
        ],
        out_specs=pl.BlockSpec((tt, POOL_W + CONV_W), lambda i: (i, 0)),
        out_shape=jax.ShapeDtypeStruct((t, POOL_W + CONV_W), MXU_DT),
        scratch_shapes=[pltpu.VMEM((tt + hp, POOL_W), F32), pltpu.VMEM((SUBLANE, tt + hp, CONV_W), F32),
                        pltpu.VMEM((tt, CONV_W), F32)],
        name=name,
        compiler_params=_cparams("parallel"),
    )(z, z, pool_w, pool_scale.reshape(1, POOL_W), dw_w, dw_b.reshape(1, CONV_W), ln_g.reshape(1, CONV_W), ln_b.reshape(1, CONV_W))


def _mix_bwd(z, dcat, pool_w, pool_scale, dw_w, dw_b, ln_g, ln_b, name):
    t = z.shape[0]
    tt = _rows(t, TW)
    hp = MIX_HALO
    nt = t // tt
    zw = POOL_W + 2 * CONV_W
    ng = len(POOL_WINDOWS)

    def body(z_ref, zp_ref, zn_ref, dc_ref, dcn_ref, pw_ref, ps_ref, w_ref, b_ref, lg_ref, lb_ref,
             dz_ref, dpw_ref, dps_ref, dww_ref, dwb_ref, dlg_ref, dlb_ref, eu_ref, ee_ref, egl_ref, edcv_ref, cv_ref):
        i = pl.program_id(0)
        first = i == 0
        last = i == nt - 1

        @pl.when(first)
        def _():
            for r in (dpw_ref, dps_ref, dww_ref, dwb_ref, dlg_ref, dlb_ref):
                r[...] = jnp.zeros_like(r)

        eu_ref[0:hp, :] = jnp.where(first, 0.0, zp_ref[:, 0:POOL_W])
        eu_ref[hp : hp + tt, :] = z_ref[:, 0:POOL_W]
        row = i * tt + lax.broadcasted_iota(jnp.int32, (tt, 1), 0)
        row_ext = i * tt + lax.broadcasted_iota(jnp.int32, (tt + hp, 1), 0)
        for gi, w in enumerate(POOL_WINDOWS):
            cols = pl.ds(gi * POOL_GROUP, POOL_GROUP)
            u = eu_ref[hp : hp + tt, cols]
            acc = u
            for k in range(1, w):
                acc = acc + eu_ref[hp - k : hp - k + tt, cols]
            pooled = (acc / jnp.minimum(row + 1, w).astype(F32) - u).astype(MXU_DT)
            pw = pw_ref[gi].astype(MXU_DT)
            dya = dc_ref[:, cols]
            y = jnp.dot(pooled, pw, preferred_element_type=F32)
            dps_ref[:, cols] += jnp.sum(dya * y, axis=0, keepdims=True)
            scale = ps_ref[:, cols]
            dy_all = jnp.concatenate([dya, jnp.where(last, 0.0, dcn_ref[:, cols])], axis=0) * scale
            dy_all = dy_all.astype(MXU_DT)
            dpw_ref[gi] += lax.dot_general(pooled, dy_all[0:tt], TN_DIMS, preferred_element_type=F32)
            dpooled = lax.dot_general(dy_all, pw, NT, preferred_element_type=F32)
            ee_ref[:, cols] = dpooled / jnp.minimum(row_ext + 1, w).astype(F32)
            du = -dpooled[0:tt]
            for k in range(w):
                du = du + ee_ref[k : k + tt, cols]
            dz_ref[:, cols] = du.astype(dz_ref.dtype)

        ca = slice(POOL_W, POOL_W + CONV_W)
        cb = slice(POOL_W + CONV_W, zw)
        egl_ref[0, 0:hp, :] = jnp.where(first, 0.0, zp_ref[:, ca] * _sigmoid(zp_ref[:, cb]))
        egl_ref[0, hp : hp + tt, :] = z_ref[:, ca] * _sigmoid(z_ref[:, cb])
        egl_ref[0, hp + tt : hp + tt + hp, :] = zn_ref[:, ca] * _sigmoid(zn_ref[:, cb])
        _shifted_copies(egl_ref, tt + 2 * hp)
        for c0 in range(0, CONV_W, LANE):
            cs = pl.ds(c0, LANE)
            acc = jnp.broadcast_to(b_ref[:, cs], (tt + hp, LANE))
            for j in range(CONV_K):
                acc = acc + w_ref[j : j + 1, cs] * _tap(egl_ref, hp - (CONV_K - 1) + j, tt + hp, cs)
            cv_ref[:, cs] = acc
        lg = lg_ref[...]
        xh, rstd, a = _layernorm_silu(cv_ref[...], lg, lb_ref[...])
        sa = _sigmoid(a)
        dyb = jnp.concatenate([dc_ref[:, ca], jnp.where(last, 0.0, dcn_ref[:, ca])], axis=0)
        da = dyb * (sa * (1.0 + a * (1.0 - sa)))
        dlg_ref[...] += jnp.sum(da[0:tt] * xh[0:tt], axis=0, keepdims=True)
        dlb_ref[...] += jnp.sum(da[0:tt], axis=0, keepdims=True)
        dxh = da * lg
        dcv = rstd * (dxh - jnp.mean(dxh, axis=-1, keepdims=True) - xh * jnp.mean(dxh * xh, axis=-1, keepdims=True))
        edcv_ref[0] = dcv
        _shifted_copies(edcv_ref, tt + hp)
        dwb_ref[...] += jnp.sum(dcv[0:tt], axis=0, keepdims=True)
        for c0 in range(0, CONV_W, LANE):
            cs = pl.ds(c0, LANE)
            gl_t = egl_ref[0, hp : hp + tt, cs]
            dgl = jnp.zeros((tt, LANE), F32)
            for j in range(CONV_K):
                tap = _tap(edcv_ref, CONV_K - 1 - j, tt, cs)
                dww_ref[j : j + 1, cs] += jnp.sum(gl_t * tap, axis=0, keepdims=True)
                dgl = dgl + w_ref[j : j + 1, cs] * tap
            ga = z_ref[:, pl.ds(POOL_W + c0, LANE)]
            sgb = _sigmoid(z_ref[:, pl.ds(POOL_W + CONV_W + c0, LANE)])
            dz_ref[:, pl.ds(POOL_W + c0, LANE)] = (dgl * sgb).astype(dz_ref.dtype)
            dz_ref[:, pl.ds(POOL_W + CONV_W + c0, LANE)] = (dgl * ga * sgb * (1.0 - sgb)).astype(dz_ref.dtype)

    vec = pl.BlockSpec((1, CONV_W), lambda i: (0, 0))
    pw_spec = pl.BlockSpec((ng, POOL_GROUP, POOL_GROUP), lambda i: (0, 0, 0))
    w_spec = pl.BlockSpec((CONV_K, CONV_W), lambda i: (0, 0))
    return pl.pallas_call(
        body,
        grid=(nt,),
        in_specs=[
            pl.BlockSpec((tt, zw), lambda i: (i, 0)),
            _prev_halo(tt, hp, zw),
            _next_halo(tt, hp, zw, t),
            pl.BlockSpec((tt, POOL_W + CONV_W), lambda i: (i, 0)),
            _next_halo(tt, hp, POOL_W + CONV_W, t),
            pw_spec, vec, w_spec, vec, vec, vec,
        ],
        out_specs=[pl.BlockSpec((tt, zw), lambda i: (i, 0)), pw_spec, vec, w_spec, vec, vec, vec],
        out_shape=[
            jax.ShapeDtypeStruct((t, zw), MXU_DT),
            jax.ShapeDtypeStruct((ng, POOL_GROUP, POOL_GROUP), F32),
            jax.ShapeDtypeStruct((1, POOL_W), F32),
            jax.ShapeDtypeStruct((CONV_K, CONV_W), F32),
            jax.ShapeDtypeStruct((1, CONV_W), F32),
            jax.ShapeDtypeStruct((1, CONV_W), F32),
            jax.ShapeDtypeStruct((1, CONV_W), F32),
        ],
        scratch_shapes=[
            pltpu.VMEM((tt + hp, POOL_W), F32),
            pltpu.VMEM((tt + hp, POOL_W), F32),
            pltpu.VMEM((SUBLANE, tt + 2 * hp, CONV_W), F32),
            pltpu.VMEM((SUBLANE, tt + hp, CONV_W), F32),
            pltpu.VMEM((tt + hp, CONV_W), F32),
        ],
        name=name,
        compiler_params=_cparams("arbitrary"),
    )(z, z, z, dcat, dcat, pool_w, pool_scale.reshape(1, POOL_W), dw_w, dw_b.reshape(1, CONV_W),
      ln_g.reshape(1, CONV_W), ln_b.reshape(1, CONV_W))


def _xa_fwd(q, kvm, name):
    t = q.shape[0]
    tt = _rows(t, TT)
    scale = XA_HEAD_DIM ** -0.5

    def body(q_ref, kv_ref, o_ref):
        for h in range(XA_HEADS):
            cs = pl.ds(h * XA_HEAD_DIM, XA_HEAD_DIM)
            vs = pl.ds(D_MODEL + h * XA_HEAD_DIM, XA_HEAD_DIM)
            s = lax.dot_general(q_ref[:, cs], kv_ref[:, cs], NT, preferred_element_type=F32) * scale
            p = jnp.exp(s - jnp.max(s, axis=-1, keepdims=True))
            p = p / jnp.sum(p, axis=-1, keepdims=True)
            o_ref[:, cs] = jnp.dot(p.astype(MXU_DT), kv_ref[:, vs], preferred_element_type=F32).astype(o_ref.dtype)

    return pl.pallas_call(
        body,
        grid=(t // tt,),
        in_specs=[pl.BlockSpec((tt, D_MODEL), lambda i: (i, 0)), pl.BlockSpec((MEM_LEN, 2 * D_MODEL), lambda i: (0, 0))],
        out_specs=pl.BlockSpec((tt, D_MODEL), lambda i: (i, 0)),
        out_shape=jax.ShapeDtypeStruct((t, D_MODEL), MXU_DT),
        name=name,
        compiler_params=_cparams("parallel"),
    )(q, kvm)


def _xa_bwd(q, kvm, do, name):
    t = q.shape[0]
    tt = _rows(t, TT)
    scale = XA_HEAD_DIM ** -0.5

    def body(q_ref, kv_ref, do_ref, dq_ref, dkv_ref):
        @pl.when(pl.program_id(0) == 0)
        def _():
            dkv_ref[...] = jnp.zeros_like(dkv_ref)

        for h in range(XA_HEADS):
            cs = pl.ds(h * XA_HEAD_DIM, XA_HEAD_DIM)
            vs = pl.ds(D_MODEL + h * XA_HEAD_DIM, XA_HEAD_DIM)
            qh = q_ref[:, cs]
            kh = kv_ref[:, cs]
            doh = do_ref[:, cs]
            s = lax.dot_general(qh, kh, NT, preferred_element_type=F32) * scale
            p = jnp.exp(s - jnp.max(s, axis=-1, keepdims=True))
            p = p / jnp.sum(p, axis=-1, keepdims=True)
            dp = lax.dot_general(doh, kv_ref[:, vs], NT, preferred_element_type=F32)
            ds = (p * (dp - jnp.sum(p * dp, axis=-1, keepdims=True)) * scale).astype(MXU_DT)
            dq_ref[:, cs] = jnp.dot(ds, kh, preferred_element_type=F32).astype(dq_ref.dtype)
            dkv_ref[:, cs] += lax.dot_general(ds, qh, TN_DIMS, preferred_element_type=F32)
            dkv_ref[:, vs] += lax.dot_general(p.astype(MXU_DT), doh, TN_DIMS, preferred_element_type=F32)

    row = pl.BlockSpec((tt, D_MODEL), lambda i: (i, 0))
    kvs = pl.BlockSpec((MEM_LEN, 2 * D_MODEL), lambda i: (0, 0))
    return pl.pallas_call(
        body,
        grid=(t // tt,),
        in_specs=[row, kvs, row],
        out_specs=[row, kvs],
        out_shape=[jax.ShapeDtypeStruct((t, D_MODEL), MXU_DT), jax.ShapeDtypeStruct((MEM_LEN, 2 * D_MODEL), F32)],
        name=name,
        compiler_params=_cparams("arbitrary"),
    )(q, kvm, do)


def _rope_tables(positions, name):
    t = positions.shape[0]
    tt = _rows(t, TT)
    inv = 1.0 / (ROPE_THETA ** (np.arange(0, QK_ROPE, 2, dtype=np.float32) / QK_ROPE))
    lanes = np.zeros((1, HEAD_PAD), np.float32)
    half = QK_ROPE // 2
    lanes[0, KPE_LANE : KPE_LANE + half] = inv
    lanes[0, KPE_LANE + half : KPE_LANE + QK_ROPE] = inv

    def body(pos_ref, inv_ref, cos_ref, sa_ref, sb_ref):
        ang = pos_ref[...].astype(F32) * inv_ref[...]
        lane = lax.broadcasted_iota(jnp.int32, (tt, HEAD_PAD), 1)
        c = jnp.cos(ang)
        s = jnp.sin(ang)
        lo = (lane >= KPE_LANE) & (lane < KPE_LANE + half)
        hi = (lane >= KPE_LANE + half) & (lane < KPE_LANE + QK_ROPE)
        cos_ref[...] = jnp.where(lo | hi, c, 1.0)
        sa_ref[...] = jnp.where(hi, s, 0.0)
        sb_ref[...] = jnp.where(lo, -s, 0.0)

    tab = pl.BlockSpec((tt, HEAD_PAD), lambda i: (i, 0))
    return pl.pallas_call(
        body,
        grid=(t // tt,),
        in_specs=[pl.BlockSpec((tt, 1), lambda i: (i, 0)), pl.BlockSpec((1, HEAD_PAD), lambda i: (0, 0))],
        out_specs=[tab, tab, tab],
        out_shape=[jax.ShapeDtypeStruct((t, HEAD_PAD), F32)] * 3,
        name=name,
        compiler_params=_cparams("parallel"),
    )(positions, jnp.asarray(lanes))


def _rotate(x, cos, sa, sb, sign):
    half = QK_ROPE // 2
    return x * cos + sign * (pltpu.roll(x, half, 1) * sa + pltpu.roll(x, HEAD_PAD - half, 1) * sb)


def _rope_heads(x, tables, sign, scale, name):
    t, w = x.shape
    tt = _rows(t, TT)
    nh = w // HEAD_PAD

    def body(x_ref, c_ref, sa_ref, sb_ref, o_ref):
        cos, sa, sb = c_ref[...] * scale, sa_ref[...] * scale, sb_ref[...] * scale
        for h in range(nh):
            cs = pl.ds(h * HEAD_PAD, HEAD_PAD)
            o_ref[:, cs] = _rotate(x_ref[:, cs], cos, sa, sb, sign).astype(o_ref.dtype)

    tab = pl.BlockSpec((tt, HEAD_PAD), lambda i: (i, 0))
    row = pl.BlockSpec((tt, w), lambda i: (i, 0))
    return pl.pallas_call(
        body,
        grid=(t // tt,),
        in_specs=[row, tab, tab, tab],
        out_specs=row,
        out_shape=jax.ShapeDtypeStruct((t, w), MXU_DT),
        name=name,
        compiler_params=_cparams("parallel"),
    )(x, *tables)


def _matmul_rope(a, b, layer, tables, scale, name):
    m, k = a.shape
    n = b.shape[-1]
    tm = _rows(m, TT)
    nh = n // HEAD_PAD

    def body(a_ref, b_ref, c_ref, sa_ref, sb_ref, o_ref):
        q = jnp.dot(a_ref[...].astype(MXU_DT), b_ref[...].astype(MXU_DT), preferred_element_type=F32)
        cos, sa, sb = c_ref[...] * scale, sa_ref[...] * scale, sb_ref[...] * scale
        for h in range(nh):
            cs = slice(h * HEAD_PAD, (h + 1) * HEAD_PAD)
            o_ref[:, cs] = _rotate(q[:, cs], cos, sa, sb, 1.0).astype(o_ref.dtype)

    tab = pl.BlockSpec((tm, HEAD_PAD), lambda i: (i, 0))
    return pl.pallas_call(
        body,
        grid=(m // tm,),
        in_specs=[pl.BlockSpec((tm, k), lambda i: (i, 0)), pl.BlockSpec((None, k, n), lambda i: (layer, 0, 0)), tab, tab, tab],
        out_specs=pl.BlockSpec((tm, n), lambda i: (i, 0)),
        out_shape=jax.ShapeDtypeStruct((m, n), MXU_DT),
        name=name,
        compiler_params=_cparams("parallel"),
    )(a, b, *tables)


def _mla_prep(cp, qg, kvg, tables, name):
    t = cp.shape[0]
    tt = _rows(t, TT)

    def body(cp_ref, qg_ref, kvg_ref, c_ref, sa_ref, sb_ref, qn_ref, kvn_ref, kpe_ref):
        cq = cp_ref[:, 0:Q_LORA]
        r = lax.rsqrt(jnp.mean(cq * cq, axis=-1, keepdims=True) + EPS)
        qn_ref[...] = ((cq * r) * qg_ref[...]).astype(qn_ref.dtype)
        ckv = cp_ref[:, Q_LORA : Q_LORA + KV_LORA]
        r = lax.rsqrt(jnp.mean(ckv * ckv, axis=-1, keepdims=True) + EPS)
        kvn_ref[...] = ((ckv * r) * kvg_ref[...]).astype(kvn_ref.dtype)
        kpe = cp_ref[:, Q_LORA + KV_LORA : C_PAD]
        kpe_ref[...] = _rotate(kpe, c_ref[...], sa_ref[...], sb_ref[...], 1.0).astype(kpe_ref.dtype)

    tab = pl.BlockSpec((tt, HEAD_PAD), lambda i: (i, 0))
    return pl.pallas_call(
        body,
        grid=(t // tt,),
        in_specs=[
            pl.BlockSpec((tt, C_PAD), lambda i: (i, 0)),
            pl.BlockSpec((1, Q_LORA), lambda i: (0, 0)),
            pl.BlockSpec((1, KV_LORA), lambda i: (0, 0)),
            tab, tab, tab,
        ],
        out_specs=[
            pl.BlockSpec((tt, Q_LORA), lambda i: (i, 0)),
            pl.BlockSpec((tt, KV_LORA), lambda i: (i, 0)),
            tab,
        ],
        out_shape=[
            jax.ShapeDtypeStruct((t, Q_LORA), MXU_DT),
            jax.ShapeDtypeStruct((t, KV_LORA), MXU_DT),
            jax.ShapeDtypeStruct((t, HEAD_PAD), MXU_DT),
        ],
        name=name,
        compiler_params=_cparams("parallel"),
    )(cp, qg.reshape(1, Q_LORA), kvg.reshape(1, KV_LORA), *tables)


def _mla_prep_bwd(cp, dqn, dkvn, dkpe_heads, qg, kvg, tables, name):
    t = cp.shape[0]
    tt = _rows(t, TT)

    def norm_bwd(x, dy, g):
        r = lax.rsqrt(jnp.mean(x * x, axis=-1, keepdims=True) + EPS)
        xh = x * r
        gy = dy * g
        return r * (gy - xh * jnp.mean(gy * xh, axis=-1, keepdims=True)), jnp.sum(dy * xh, axis=0, keepdims=True)

    def body(cp_ref, dqn_ref, dkvn_ref, dkpe_ref, qg_ref, kvg_ref, c_ref, sa_ref, sb_ref, dcp_ref, dqg_ref, dkvg_ref):
        @pl.when(pl.program_id(0) == 0)
        def _():
            dqg_ref[...] = jnp.zeros_like(dqg_ref)
            dkvg_ref[...] = jnp.zeros_like(dkvg_ref)

        dcq, dg = norm_bwd(cp_ref[:, 0:Q_LORA], dqn_ref[...], qg_ref[...])
        dcp_ref[:, 0:Q_LORA] = dcq.astype(dcp_ref.dtype)
        dqg_ref[...] += dg
        dckv, dg = norm_bwd(cp_ref[:, Q_LORA : Q_LORA + KV_LORA], dkvn_ref[...], kvg_ref[...])
        dcp_ref[:, Q_LORA : Q_LORA + KV_LORA] = dckv.astype(dcp_ref.dtype)
        dkvg_ref[...] += dg
        dk = dkpe_ref[0]
        for h in range(1, MLA_HEADS):
            dk = dk + dkpe_ref[h]
        dcp_ref[:, Q_LORA + KV_LORA : C_PAD] = _rotate(dk, c_ref[...], sa_ref[...], sb_ref[...], -1.0).astype(dcp_ref.dtype)

    tab = pl.BlockSpec((tt, HEAD_PAD), lambda i: (i, 0))
    return pl.pallas_call(
        body,
        grid=(t // tt,),
        in_specs=[
            pl.BlockSpec((tt, C_PAD), lambda i: (i, 0)),
            pl.BlockSpec((tt, Q_LORA), lambda i: (i, 0)),
            pl.BlockSpec((tt, KV_LORA), lambda i: (i, 0)),
            pl.BlockSpec((MLA_HEADS, tt, HEAD_PAD), lambda i: (0, i, 0)),
            pl.BlockSpec((1, Q_LORA), lambda i: (0, 0)),
            pl.BlockSpec((1, KV_LORA), lambda i: (0, 0)),
            tab, tab, tab,
        ],
        out_specs=[
            pl.BlockSpec((tt, C_PAD), lambda i: (i, 0)),
            pl.BlockSpec((1, Q_LORA), lambda i: (0, 0)),
            pl.BlockSpec((1, KV_LORA), lambda i: (0, 0)),
        ],
        out_shape=[
            jax.ShapeDtypeStruct((t, C_PAD), MXU_DT),
            jax.ShapeDtypeStruct((1, Q_LORA), F32),
            jax.ShapeDtypeStruct((1, KV_LORA), F32),
        ],
        name=name,
        compiler_params=_cparams("arbitrary"),
    )(cp, dqn, dkvn, dkpe_heads, qg.reshape(1, Q_LORA), kvg.reshape(1, KV_LORA), *tables)


def _flash_fwd(qs, kv, kpe, name):
    t = qs.shape[0]
    ta = _rows(t, TA)
    tq = ta
    nq = t // tq
    sub = ta // 2

    def body(q_ref, kv_ref, kpe_ref, o_ref, lse_ref):
        qi = pl.program_id(1)
        q = q_ref[...]
        lane = lax.broadcasted_iota(jnp.int32, (ta, HEAD_PAD), 1)

        def kblock(j):
            rows = pl.ds(pl.multiple_of(j * ta, ta), ta)
            kvb = kv_ref[rows, :]
            ones_v = jnp.where(lane < QK_NOPE, jnp.ones_like(kvb), kvb)
            return ones_v, jnp.where(lane < QK_NOPE, kvb, kpe_ref[rows, :])

        def update(carry, s, ones_v):
            m, acc = carry
            m_new = jnp.maximum(m, jnp.max(s, axis=-1, keepdims=True))
            p = jnp.exp2(s - m_new).astype(MXU_DT)
            acc = jnp.exp2(m - m_new) * acc + jnp.dot(p, ones_v, preferred_element_type=F32)
            return m_new, acc

        def step(j, carry):
            ones_v, k = kblock(j)
            return update(carry, lax.dot_general(q, k, NT, preferred_element_type=F32), ones_v)

        init = (jnp.full((tq, 1), -jnp.inf, F32), jnp.zeros((tq, HEAD_PAD), F32))
        m_all, acc_all = lax.fori_loop(0, qi, step, init)
        ones_v, k = kblock(qi)
        lane_h = lax.broadcasted_iota(jnp.int32, (sub, HEAD_PAD), 1)
        for b in range(2):
            rows, nk = slice(b * sub, (b + 1) * sub), (b + 1) * sub
            s = lax.dot_general(q[rows], k[0:nk], NT, preferred_element_type=F32)
            r = lax.broadcasted_iota(jnp.int32, (sub, nk), 0) + b * sub
            c = lax.broadcasted_iota(jnp.int32, (sub, nk), 1)
            m, acc = update((m_all[rows], acc_all[rows]), jnp.where(c <= r, s, NEG), ones_v[0:nk])
            l = acc[:, 0:1]
            o_ref[rows, :] = jnp.where(lane_h >= QK_NOPE, acc / l, 0.0).astype(o_ref.dtype)
            lse_ref[rows, :] = m + jnp.log2(l)

    return pl.pallas_call(
        body,
        grid=(MLA_HEADS, nq),
        in_specs=[
            pl.BlockSpec((tq, HEAD_PAD), lambda h, i: (i, h)),
            pl.BlockSpec((t, HEAD_PAD), lambda h, i: (0, h)),
            pl.BlockSpec((t, HEAD_PAD), lambda h, i: (0, 0)),
        ],
        out_specs=[
            pl.BlockSpec((tq, HEAD_PAD), lambda h, i: (i, h)),
            pl.BlockSpec((None, tq, 1), lambda h, i: (h, i, 0)),
        ],
        out_shape=[
            jax.ShapeDtypeStruct((t, MLA_HEADS * HEAD_PAD), MXU_DT),
            jax.ShapeDtypeStruct((MLA_HEADS, t, 1), F32),
        ],
        name=name,
        compiler_params=_cparams("parallel", "parallel"),
    )(qs, kv, kpe)


def _flash_bwd(qs, kv, kpe, o, do, lse, name):
    t = qs.shape[0]
    ta = _rows(t, TA)
    tq = ta
    nq = t // ta
    sub = ta // 2

    def body(q_ref, o_ref, do_ref, lse_ref, kv_ref, kpe_ref, dq_ref, dkv_ref, dkpe_ref, dk_acc, dv_acc):
        kj = pl.program_id(1)

        @pl.when(kj == 0)
        def _():
            dq_ref[...] = jnp.zeros_like(dq_ref)

        lane = lax.broadcasted_iota(jnp.int32, (ta, HEAD_PAD), 1)
        kvb = kv_ref[...]
        k = jnp.where(lane < QK_NOPE, kvb, kpe_ref[...])
        dk_acc[...] = jnp.zeros_like(dk_acc)
        dv_acc[...] = jnp.zeros_like(dv_acc)

        def tile(row0, nrows, nkeys, diagonal):
            rows = pl.ds(pl.multiple_of(row0, sub), nrows)
            keys = slice(0, nkeys)
            q = q_ref[rows, :]
            dob = do_ref[rows, :]
            delta = jnp.sum(dob.astype(F32) * o_ref[rows, :].astype(F32), axis=-1, keepdims=True)
            s = lax.dot_general(q, k[keys], NT, preferred_element_type=F32)
            if diagonal:
                r = lax.broadcasted_iota(jnp.int32, (nrows, nkeys), 0) + (nkeys - nrows)
                c = lax.broadcasted_iota(jnp.int32, (nrows, nkeys), 1)
                s = jnp.where(c <= r, s, NEG)
            p = jnp.exp2(s - lse_ref[rows, :])
            dp = lax.dot_general(dob, kvb[keys], NT, preferred_element_type=F32)
            ds = (p * (dp - delta)).astype(MXU_DT)
            dq_ref[rows, :] += jnp.dot(ds, k[keys], preferred_element_type=F32)
            dk_acc[keys, :] += lax.dot_general(ds, q, TN_DIMS, preferred_element_type=F32)
            dv_acc[keys, :] += lax.dot_general(p.astype(MXU_DT), dob, TN_DIMS, preferred_element_type=F32)

        tile(kj * ta, sub, sub, True)
        tile(kj * ta + sub, sub, ta, True)

        def step(qq, carry):
            tile(qq * tq, tq, ta, False)
            return carry

        lax.fori_loop(kj + 1, t // tq, step, 0)
        dk = dk_acc[...] * (1.0 / LOG2E)
        dkv_ref[...] = jnp.where(lane < QK_NOPE, dk, dv_acc[...]).astype(dkv_ref.dtype)
        dkpe_ref[...] = jnp.where((lane >= KPE_LANE) & (lane < KPE_LANE + QK_ROPE), dk, 0.0)

    head_rows = pl.BlockSpec((t, HEAD_PAD), lambda h, j: (0, h))
    return pl.pallas_call(
        body,
        grid=(MLA_HEADS, nq),
        in_specs=[
            head_rows,
            head_rows,
            head_rows,
            pl.BlockSpec((None, t, 1), lambda h, j: (h, 0, 0)),
            pl.BlockSpec((ta, HEAD_PAD), lambda h, j: (j, h)),
            pl.BlockSpec((ta, HEAD_PAD), lambda h, j: (j, 0)),
        ],
        out_specs=[
            head_rows,
            pl.BlockSpec((ta, HEAD_PAD), lambda h, j: (j, h)),
            pl.BlockSpec((None, ta, HEAD_PAD), lambda h, j: (h, j, 0)),
        ],
        out_shape=[
            jax.ShapeDtypeStruct((t, MLA_HEADS * HEAD_PAD), F32),
            jax.ShapeDtypeStruct((t, MLA_HEADS * HEAD_PAD), MXU_DT),
            jax.ShapeDtypeStruct((MLA_HEADS, t, HEAD_PAD), F32),
        ],
        scratch_shapes=[pltpu.VMEM((ta, HEAD_PAD), F32), pltpu.VMEM((ta, HEAD_PAD), F32)],
        name=name,
        compiler_params=_cparams("parallel", "arbitrary"),
    )(qs, o, do, lse, kv, kpe)


def _as2d(a):
    if a.ndim == 1:
        return a.reshape(1, a.shape[0])
    return a.reshape(-1, a.shape[-1])


def _adamw(w, g, m, v, name):
    shape = w.shape
    w2, g2, m2, v2 = (_as2d(a) for a in (w, g, m, v))
    r, c = w2.shape
    tr = _tile_rows(r, c)
    c1 = 1.0 - ADAM_B1 ** ADAM_STEP
    c2 = 1.0 - ADAM_B2 ** ADAM_STEP

    def body(w_ref, g_ref, m_ref, v_ref, d_ref, nm_ref, nv_ref):
        gv = g_ref[...]
        nm = ADAM_B1 * m_ref[...] + (1.0 - ADAM_B1) * gv
        nv = ADAM_B2 * v_ref[...] + (1.0 - ADAM_B2) * (gv * gv)
        d_ref[...] = -ADAM_LR * ((nm / c1) / (jnp.sqrt(nv / c2) + ADAM_EPS) + ADAM_WD * w_ref[...])
        nm_ref[...] = nm
        nv_ref[...] = nv

    blk = pl.BlockSpec((tr, c), lambda i: (i, 0))
    outs = pl.pallas_call(
        body,
        grid=(r // tr,),
        in_specs=[blk] * 4,
        out_specs=[blk] * 3,
        out_shape=[jax.ShapeDtypeStruct((r, c), F32)] * 3,
        name=name,
        compiler_params=_cparams("parallel"),
    )(w2, g2, m2, v2)
    return tuple(o.reshape(shape) for o in outs)


def _adamw_halves(w, mine, other, m, v, c_idx, name):
    nl, r, c = w.shape
    h = nl // 2
    tr = _tile_rows(r, 2 * c)
    c1 = 1.0 - ADAM_B1 ** ADAM_STEP
    c2 = 1.0 - ADAM_B2 ** ADAM_STEP

    def body(c_ref, w_ref, a_ref, b_ref, m_ref, v_ref, g_ref, d_ref, nm_ref, nv_ref):
        l = pl.program_id(0)
        gv = jnp.where(l // h == c_ref[0], a_ref[...], b_ref[...])
        nm = ADAM_B1 * m_ref[...] + (1.0 - ADAM_B1) * gv
        nv = ADAM_B2 * v_ref[...] + (1.0 - ADAM_B2) * (gv * gv)
        g_ref[...] = gv
        d_ref[...] = -ADAM_LR * ((nm / c1) / (jnp.sqrt(nv / c2) + ADAM_EPS) + ADAM_WD * w_ref[...])
        nm_ref[...] = nm
        nv_ref[...] = nv

    def half_map(mine_side):
        def index(l, i, cr):
            first = cr[0] * h if mine_side else (1 - cr[0]) * h
            return (jnp.clip(l - first, 0, h - 1), i, 0)
        return index

    full = pl.BlockSpec((None, tr, c), lambda l, i, cr: (l, i, 0))
    grid_spec = pltpu.PrefetchScalarGridSpec(
        num_scalar_prefetch=1,
        grid=(nl, r // tr),
        in_specs=[full, pl.BlockSpec((None, tr, c), half_map(True)), pl.BlockSpec((None, tr, c), half_map(False)), full, full],
        out_specs=[full] * 4,
    )
    return pl.pallas_call(
        body,
        grid_spec=grid_spec,
        out_shape=[jax.ShapeDtypeStruct((nl, r, c), F32)] * 4,
        name=name,
        compiler_params=_cparams("parallel", "parallel"),
    )(c_idx, w, mine, other, m, v)


def _tile_rows(r, c, mult=SUBLANE):
    limit = max(mult, (BLOCK_BYTES // 4) // (4 * c))
    if r <= limit:
        return r
    t = (limit // mult) * mult
    while t >= mult:
        if r % t == 0:
            return t
        t -= mult
    return r


def _sum_leading(a, name):
    n, r, c = a.shape
    tr = _tile_rows(r, c * n)

    def body(a_ref, o_ref):
        s = a_ref[0]
        for k in range(1, n):
            s = s + a_ref[k]
        o_ref[...] = s

    return pl.pallas_call(
        body,
        grid=(r // tr,),
        in_specs=[pl.BlockSpec((n, tr, c), lambda i: (0, i, 0))],
        out_specs=pl.BlockSpec((tr, c), lambda i: (i, 0)),
        out_shape=jax.ShapeDtypeStruct((r, c), F32),
        name=name,
        compiler_params=_cparams("parallel"),
    )(a)


def _add_half(g, s, c_idx, name):
    nl, r, c = g.shape
    h = nl // 2
    tr = _tile_rows(r, 2 * c, 2 * SUBLANE)

    def body(c_ref, g_ref, s_ref, o_ref):
        o_ref[...] = (g_ref[...] + s_ref[...]).astype(o_ref.dtype)

    grid_spec = pltpu.PrefetchScalarGridSpec(
        num_scalar_prefetch=1,
        grid=(h, r // tr),
        in_specs=[
            pl.BlockSpec((None, tr, c), lambda l, i, cr: (cr[0] * h + l, i, 0)),
            pl.BlockSpec((None, tr, c), lambda l, i, cr: (l, i, 0)),
        ],
        out_specs=pl.BlockSpec((None, tr, c), lambda l, i, cr: (l, i, 0)),
    )
    return pl.pallas_call(
        body,
        grid_spec=grid_spec,
        out_shape=jax.ShapeDtypeStruct((h, r, c), XFER_DT),
        name=name,
        compiler_params=_cparams("parallel", "parallel"),
    )(c_idx, g, s)


def _sum_chips(slots, pair, chip_idx, kind, name):
    _, h, r, c = slots.shape
    tr = _tile_rows(r, 5 * c, 2 * SUBLANE)
    nr = r // tr

    def body(chip_ref, s_ref, own_ref, o_ref):
        chip = chip_ref[0]
        own = own_ref[...].astype(F32)
        parts = [s_ref[j].astype(F32) for j in range(3)]
        total = None
        for k in range(4):
            d = jnp.bitwise_xor(chip, k)
            v = jnp.where(d == 0, own, jnp.where(d == 2, parts[0], jnp.where(d == 1, parts[1], parts[2])))
            total = v if total is None else total + v
        o_ref[...] = total

    if kind == "row":
        own_spec = pl.BlockSpec((None, tr, c), lambda l, i, cr: (l, cr[0] * nr + i, 0))
    else:
        own_spec = pl.BlockSpec((None, tr, c), lambda l, i, cr: (l, i, cr[0]))
    grid_spec = pltpu.PrefetchScalarGridSpec(
        num_scalar_prefetch=1,
        grid=(h, nr),
        in_specs=[pl.BlockSpec((3, None, tr, c), lambda l, i, cr: (0, l, i, 0)), own_spec],
        out_specs=pl.BlockSpec((None, tr, c), lambda l, i, cr: (l, i, 0)),
    )
    return pl.pallas_call(
        body,
        grid_spec=grid_spec,
        out_shape=jax.ShapeDtypeStruct((h, r, c), F32),
        name=name,
        compiler_params=_cparams("parallel", "parallel"),
    )(chip_idx, slots, pair)


def _mesh_pos():
    return lax.axis_index("x"), lax.axis_index("y"), lax.axis_index("c")


def _other_chips(x, y):
    return [(1 - x, y), (x, 1 - y), (1 - x, 1 - y)]


def _all_gather_rows(block, name):
    m_per, n = block.shape

    def body(x_ref, out_ref, send_sems, recv_sems, local_sem):
        x, y, c = _mesh_pos()
        me, sibling = (x, y, c), (x, y, 1 - c)
        chips = _other_chips(x, y)

        def rows(px, py, pc):
            return out_ref.at[pl.ds((4 * px + 2 * py + pc) * m_per, m_per), :]

        def copy(k, blk, to, src=None):
            return pltpu.make_async_remote_copy(
                src_ref=rows(*blk) if src is None else src,
                dst_ref=rows(*blk),
                send_sem=send_sems.at[k],
                recv_sem=recv_sems.at[k],
                device_id=to,
                device_id_type=MESH_ID,
            )

        mine = pltpu.make_async_copy(x_ref, rows(*me), local_sem)
        mine.start()
        first = [copy(0, me, sibling, src=x_ref)]
        first += [copy(1 + j, me, (*chip, c), src=x_ref) for j, chip in enumerate(chips)]
        for cp in first:
            cp.start()
        passed = [copy(4 + j, (*chip, c), sibling) for j, chip in enumerate(chips)]
        for j, chip in enumerate(chips):
            copy(1 + j, (*chip, c), me).wait_recv()
            passed[j].start()
        copy(0, sibling, me).wait_recv()
        for j, chip in enumerate(chips):
            copy(4 + j, (*chip, 1 - c), me).wait_recv()
        for cp in first + passed:
            cp.wait_send()
        mine.wait()

    return pl.pallas_call(
        body,
        out_shape=jax.ShapeDtypeStruct((8 * m_per, n), block.dtype),
        in_specs=[pl.BlockSpec(memory_space=pltpu.VMEM)],
        out_specs=pl.BlockSpec(memory_space=pltpu.VMEM),
        scratch_shapes=[pltpu.SemaphoreType.DMA((7,)), pltpu.SemaphoreType.DMA((7,)), pltpu.SemaphoreType.DMA],
        name=name,
        compiler_params=pltpu.CompilerParams(vmem_limit_bytes=VMEM_LIMIT),
    )(block)


def _shard_window(ref, layers, chip, rows, cols):
    if rows is not None:
        return ref.at[layers, pl.ds(pl.multiple_of(chip * rows, rows), rows), :]
    return ref.at[layers, :, pl.ds(pl.multiple_of(chip * cols, cols), cols)]


def _all_gather_weights(shards, kinds, name):
    nw = len(shards)
    out_shapes = []
    for s, kind in zip(shards, kinds):
        nl, r, c = s.shape
        full = (nl, 4 * r, c) if kind == "row" else (nl, r, 4 * c)
        out_shapes.append(jax.ShapeDtypeStruct(full, s.dtype))

    def body(*refs):
        ins, outs = refs[:nw], refs[nw : 2 * nw]
        send_sems, recv_sems, in_sems, out_sems = refs[2 * nw : 2 * nw + 4]
        bufs = refs[2 * nw + 4 :]
        x, y, c = _mesh_pos()
        sibling = (x, y, 1 - c)
        chips = _other_chips(x, y)
        my_chip = 2 * x + y

        def window(w, chip, layers):
            _, r, cc = shards[w].shape
            if kinds[w] == "row":
                return _shard_window(outs[w], layers, chip, r, None)
            return _shard_window(outs[w], layers, chip, None, cc)

        def half(w, half_idx):
            h = shards[w].shape[0] // 2
            return pl.ds(half_idx * h, h)

        def copy(w, k, src, dst, to):
            return pltpu.make_async_remote_copy(
                src_ref=src, dst_ref=dst, send_sem=send_sems.at[w, k], recv_sem=recv_sems.at[w, k],
                device_id=to, device_id_type=MESH_ID)

        sent = []
        for w in range(nw):
            mine = ins[w].at[half(w, c)]
            for j, chip in enumerate(chips):
                cp = copy(w, j, mine, window(w, my_chip, half(w, c)), (*chip, c))
                cp.start()
                sent.append(cp)
        for w in range(nw):
            nl = shards[w].shape[0]

            def load(l, w=w):
                return pltpu.make_async_copy(ins[w].at[l], bufs[w].at[l % 2], in_sems.at[w, l % 2])

            def store(l, w=w):
                return pltpu.make_async_copy(bufs[w].at[l % 2], window(w, my_chip, l), out_sems.at[w, l % 2])

            load(0).start()
            for l in range(nl):
                load(l).wait()
                store(l).start()
                if l + 1 < nl:
                    if l >= 1:
                        store(l - 1).wait()
                    load(l + 1).start()
            for l in range(max(nl - 2, 0), nl):
                store(l).wait()
        for w in range(nw):
            for j, (cx, cy) in enumerate(chips):
                got = window(w, 2 * cx + cy, half(w, c))
                copy(w, j, got, got, (cx, cy, c)).wait_recv()
                cp = copy(w, 3 + j, got, got, sibling)
                cp.start()
                sent.append(cp)
        for w in range(nw):
            for j, (cx, cy) in enumerate(chips):
                got = window(w, 2 * cx + cy, half(w, 1 - c))
                copy(w, 3 + j, got, got, sibling).wait_recv()
        for cp in sent:
            cp.wait_send()

    anyspec = pl.BlockSpec(memory_space=pl.ANY)
    return pl.pallas_call(
        body,
        out_shape=out_shapes,
        in_specs=[anyspec] * nw,
        out_specs=[anyspec] * nw,
        scratch_shapes=[pltpu.SemaphoreType.DMA((nw, 6)), pltpu.SemaphoreType.DMA((nw, 6)),
                        pltpu.SemaphoreType.DMA((nw, 2)), pltpu.SemaphoreType.DMA((nw, 2))]
        + [pltpu.VMEM((2,) + s.shape[1:], s.dtype) for s in shards],
        name=name,
        compiler_params=pltpu.CompilerParams(vmem_limit_bytes=VMEM_LIMIT),
    )(*shards)


def _exchange_halves(grads, name):
    nw = len(grads)
    out_shapes = [jax.ShapeDtypeStruct((g.shape[0] // 2,) + g.shape[1:], g.dtype) for g in grads]

    def body(*refs):
        ins, outs = refs[:nw], refs[nw : 2 * nw]
        send_sems, recv_sems = refs[2 * nw :]
        x, y, c = _mesh_pos()
        cps = []
        for w in range(nw):
            h = grads[w].shape[0] // 2
            cp = pltpu.make_async_remote_copy(
                src_ref=ins[w].at[pl.ds((1 - c) * h, h)], dst_ref=outs[w], send_sem=send_sems.at[w],
                recv_sem=recv_sems.at[w], device_id=(x, y, 1 - c), device_id_type=MESH_ID)
            cp.start()
            cps.append(cp)
        for cp in cps:
            cp.wait()

    anyspec = pl.BlockSpec(memory_space=pl.ANY)
    return pl.pallas_call(
        body,
        out_shape=out_shapes,
        in_specs=[anyspec] * nw,
        out_specs=[anyspec] * nw,
        scratch_shapes=[pltpu.SemaphoreType.DMA((nw,)), pltpu.SemaphoreType.DMA((nw,))],
        name=name,
    )(*grads)


def _scatter_to_chips(parts, kinds, name):
    nw = len(parts)
    shard_shapes = []
    for p, kind in zip(parts, kinds):
        h, r, c = p.shape
        shard_shapes.append((h, r // 4, c) if kind == "row" else (h, r, c // 4))
    out_shapes = [jax.ShapeDtypeStruct((3,) + s, p.dtype) for s, p in zip(shard_shapes, parts)]

    def body(*refs):
        ins, outs = refs[:nw], refs[nw : 2 * nw]
        send_sems, recv_sems = refs[2 * nw :]
        x, y, c = _mesh_pos()
        chips = _other_chips(x, y)

        def piece(w, chip):
            h, r, cc = shard_shapes[w]
            if kinds[w] == "row":
                return _shard_window(ins[w], pl.ds(0, h), chip, r, None)
            return _shard_window(ins[w], pl.ds(0, h), chip, None, cc)

        def copy(w, j, cx, cy):
            return pltpu.make_async_remote_copy(
                src_ref=piece(w, 2 * cx + cy), dst_ref=outs[w].at[j], send_sem=send_sems.at[w, j],
                recv_sem=recv_sems.at[w, j], device_id=(cx, cy, c), device_id_type=MESH_ID)

        cps = [copy(w, j, cx, cy) for w in range(nw) for j, (cx, cy) in enumerate(chips)]
        for cp in cps:
            cp.start()
        for cp in cps:
            cp.wait()

    anyspec = pl.BlockSpec(memory_space=pl.ANY)
    return pl.pallas_call(
        body,
        out_shape=out_shapes,
        in_specs=[anyspec] * nw,
        out_specs=[anyspec] * nw,
        scratch_shapes=[pltpu.SemaphoreType.DMA((nw, 3)), pltpu.SemaphoreType.DMA((nw, 3))],
        name=name,
    )(*parts)


def _swap_halves(halves, name):
    nw = len(halves)
    out_shapes = [jax.ShapeDtypeStruct(p.shape, p.dtype) for p in halves]

    def body(*refs):
        ins, outs = refs[:nw], refs[nw : 2 * nw]
        send_sems, recv_sems = refs[2 * nw :]
        x, y, c = _mesh_pos()
        cps = [pltpu.make_async_remote_copy(
            src_ref=ins[w], dst_ref=outs[w], send_sem=send_sems.at[w], recv_sem=recv_sems.at[w],
            device_id=(x, y, 1 - c), device_id_type=MESH_ID) for w in range(nw)]
        for cp in cps:
            cp.start()
        for cp in cps:
            cp.wait()

    anyspec = pl.BlockSpec(memory_space=pl.ANY)
    return pl.pallas_call(
        body,
        out_shape=out_shapes,
        in_specs=[anyspec] * nw,
        out_specs=[anyspec] * nw,
        scratch_shapes=[pltpu.SemaphoreType.DMA((nw,)), pltpu.SemaphoreType.DMA((nw,))],
        name=name,
    )(*halves)


def _pad_wdq(w):
    z = lambda n: jnp.zeros((w.shape[0], n), w.dtype)
    base = Q_LORA + KV_LORA
    return jnp.concatenate([w[:, :base], z(KPE_LANE), w[:, base:], z(HEAD_PAD - KPE_LANE - QK_ROPE)], axis=1)


def _unpad_wdq(g):
    base = Q_LORA + KV_LORA
    return jnp.concatenate([g[:, :base], g[:, base + KPE_LANE : base + KPE_LANE + QK_ROPE]], axis=1)


def _pad_wuq(w):
    w3 = w.reshape(Q_LORA, MLA_HEADS, QK_NOPE + QK_ROPE)
    w3 = jnp.pad(w3, ((0, 0), (0, 0), (0, HEAD_PAD - QK_NOPE - QK_ROPE)))
    return w3.reshape(Q_LORA, MLA_HEADS * HEAD_PAD)


def _unpad_wuq(g):
    g3 = g.reshape(Q_LORA, MLA_HEADS, HEAD_PAD)[:, :, : QK_NOPE + QK_ROPE]
    return g3.reshape(Q_LORA, MLA_HEADS * (QK_NOPE + QK_ROPE))


def _pad_wo(w):
    w3 = w.reshape(MLA_HEADS, V_HEAD, D_MODEL)
    w3 = jnp.pad(w3, ((0, 0), (HEAD_PAD - V_HEAD, 0), (0, 0)))
    return w3.reshape(MLA_HEADS * HEAD_PAD, D_MODEL)


def _unpad_wo(g):
    g3 = g.reshape(MLA_HEADS, HEAD_PAD, D_MODEL)[:, HEAD_PAD - V_HEAD :, :]
    return g3.reshape(MLA_HEADS * V_HEAD, D_MODEL)


def _local_step(x, mem, positions, target, wb, ws):
    t = x.shape[0]
    tables = _rope_tables(positions.reshape(t, 1), "rope_tables")
    saved = []
    h1 = _rms_fwd(x, ws["norm_mix_g"][0], "l0_norm_mix")
    for l in range(DEPTH):
        s = {"x0": x}
        s["h1"] = h1
        if l % 2 == 0:
            e = l // 2
            z = _matmul(h1, wb["pc_w_in"], "nn", F32, f"l{l}_pc_in", layer=e)
            cat = _mix_fwd(z, ws["pool_w"][e], ws["pool_scale"][e], ws["conv_dw_w"][e], ws["conv_dw_b"][e],
                           ws["conv_ln_g"][e], ws["conv_ln_b"][e], f"l{l}_mix")
            x, h2 = _matmul_res_norm(cat, wb["pc_w_out"], e, x, ws["norm_xa_g"][l], f"l{l}_pc_out")
            s.update(z=z, cat=cat)
        else:
            o = l // 2
            cp = _matmul(h1, wb["mla_wdq"], "nn", F32, f"l{l}_mla_dq", layer=o)
            qn, kvn, kpe = _mla_prep(cp, ws["mla_q_norm_g"][o], ws["mla_kv_norm_g"][o], tables, f"l{l}_mla_prep")
            qr = _matmul_rope(qn, wb["mla_wuq"], o, tables, MLA_SCALE * LOG2E, f"l{l}_mla_uq")
            kv = _matmul(kvn, wb["mla_w_ukv"], "nn", MXU_DT, f"l{l}_mla_ukv", layer=o)
            att, lse = _flash_fwd(qr, kv, kpe, f"l{l}_mla_attn")
            x, h2 = _matmul_res_norm(att, wb["mla_wo"], o, x, ws["norm_xa_g"][l], f"l{l}_mla_o")
            s.update(cp=cp, qn=qn, kvn=kvn, kpe=kpe, qr=qr, kv=kv, att=att, lse=lse)
        s["x1"] = x
        hm = _rms_fwd(mem, ws["norm_mem_g"][l], f"l{l}_norm_mem")
        q2 = _matmul(h2, wb["xa_wq"], "nn", MXU_DT, f"l{l}_xa_q", layer=l)
        kvm = _matmul(hm, wb["xa_wkv"], "nn", MXU_DT, f"l{l}_xa_kv", layer=l)
        o2 = _xa_fwd(q2, kvm, f"l{l}_xa_attn")
        x, h3 = _matmul_res_norm(o2, wb["xa_wo"], l, x, ws["norm_ffn_g"][l], f"l{l}_xa_o")
        s.update(h2=h2, hm=hm, q2=q2, kvm=kvm, o2=o2, x2=x)
        up = _matmul(h3, wb["ffn_w_up"], "nn", F32, f"l{l}_ffn_up", layer=l)
        act = _ffn_fwd(up, ws["ffn_conv_w"][l], ws["ffn_conv_b"][l], f"l{l}_ffn_mid")
        if l + 1 < DEPTH:
            x, h1 = _matmul_res_norm(act, wb["ffn_w_down"], l, x, ws["norm_mix_g"][l + 1], f"l{l}_ffn_down")
        else:
            x = _matmul(act, wb["ffn_w_down"], "nn", F32, f"l{l}_ffn_down", layer=l, res=x)
        s.update(h3=h3, up=up, act=act)
        saved.append(s)

    dx, dx16, dg_final, loss = _loss_head(x, target, ws["final_norm_g"], "loss_head")
    g = {k: [None] * DEPTH for k in ("norm_mix_g", "norm_xa_g", "norm_mem_g", "xa_wq", "xa_wkv", "xa_wo", "norm_ffn_g",
                                      "ffn_w_up", "ffn_conv_w", "ffn_conv_b", "ffn_w_down")}
    g.update({k: [None] * (DEPTH // 2) for k in ("pc_w_in", "pool_w", "pool_scale", "conv_dw_w", "conv_dw_b", "conv_ln_g",
                                                 "conv_ln_b", "pc_w_out", "mla_w_dq_dkv", "mla_q_norm_g", "mla_w_uq",
                                                 "mla_kv_norm_g", "mla_w_ukv", "mla_w_o")})
    stk = {k: None for k in ("xa_wq", "xa_wkv", "xa_wo", "ffn_w_up", "ffn_w_down", "pc_w_in", "pc_w_out")}
    for l in reversed(range(DEPTH)):
        s = saved[l]
        dact = _matmul(dx16, wb["ffn_w_down"], "nt", F32, f"l{l}_b_ffn_dact", layer=l)
        stk["ffn_w_down"] = _matmul(s["act"], dx16, "tn", F32, f"l{l}_b_ffn_dwdown", stack=(stk["ffn_w_down"], l, DEPTH))
        dup, dcw, dcb = _ffn_bwd(s["up"], dact, ws["ffn_conv_w"][l], ws["ffn_conv_b"][l], f"l{l}_b_ffn_mid")
        g["ffn_conv_w"][l], g["ffn_conv_b"][l] = dcw, dcb[0]
        stk["ffn_w_up"] = _matmul(s["h3"], dup, "tn", F32, f"l{l}_b_ffn_dwup", stack=(stk["ffn_w_up"], l, DEPTH))
        dx, dx16, dg = _matmul_rms_bwd(dup, wb["ffn_w_up"], l, s["x2"], ws["norm_ffn_g"][l], dx, f"l{l}_b_ffn_dh")
        g["norm_ffn_g"][l] = dg[0]
        do2 = _matmul(dx16, wb["xa_wo"], "nt", MXU_DT, f"l{l}_b_xa_do", layer=l)
        stk["xa_wo"] = _matmul(s["o2"], dx16, "tn", F32, f"l{l}_b_xa_dwo", stack=(stk["xa_wo"], l, DEPTH))
        dq2, dkvm = _xa_bwd(s["q2"], s["kvm"], do2, f"l{l}_b_xa_attn")
        stk["xa_wq"] = _matmul(s["h2"], dq2, "tn", F32, f"l{l}_b_xa_dwq", stack=(stk["xa_wq"], l, DEPTH))
        stk["xa_wkv"] = _matmul(s["hm"], dkvm, "tn", F32, f"l{l}_b_xa_dwkv", stack=(stk["xa_wkv"], l, DEPTH))
        dhm = _matmul(dkvm, wb["xa_wkv"], "nt", F32, f"l{l}_b_xa_dhm", layer=l)
        g["norm_mem_g"][l] = _rms_bwd_gain(dhm, mem, ws["norm_mem_g"][l], f"l{l}_b_norm_mem")[0]
        dx, dx16, dg = _matmul_rms_bwd(dq2, wb["xa_wq"], l, s["x1"], ws["norm_xa_g"][l], dx, f"l{l}_b_xa_dh")
        g["norm_xa_g"][l] = dg[0]
        if l % 2 == 0:
            e = l // 2
            dcat = _matmul(dx16, wb["pc_w_out"], "nt", F32, f"l{l}_b_pc_dcat", layer=e)
            stk["pc_w_out"] = _matmul(s["cat"], dx16, "tn", F32, f"l{l}_b_pc_dwout", stack=(stk["pc_w_out"], e, DEPTH // 2))
            dz, dpw, dps, dww, dwb, dlg, dlb = _mix_bwd(
                s["z"], dcat, ws["pool_w"][e], ws["pool_scale"][e], ws["conv_dw_w"][e], ws["conv_dw_b"][e],
                ws["conv_ln_g"][e], ws["conv_ln_b"][e], f"l{l}_b_mix")
            g["pool_w"][e], g["pool_scale"][e], g["conv_dw_w"][e] = dpw, dps[0], dww
            g["conv_dw_b"][e], g["conv_ln_g"][e], g["conv_ln_b"][e] = dwb[0], dlg[0], dlb[0]
            stk["pc_w_in"] = _matmul(s["h1"], dz, "tn", F32, f"l{l}_b_pc_dwin", stack=(stk["pc_w_in"], e, DEPTH // 2))
            dx, dx16, dg = _matmul_rms_bwd(dz, wb["pc_w_in"], e, s["x0"], ws["norm_mix_g"][l], dx, f"l{l}_b_pc_dh")
        else:
            o = l // 2
            do = _matmul(dx16, wb["mla_wo"], "nt", MXU_DT, f"l{l}_b_mla_do", layer=o)
            g["mla_w_o"][o] = _unpad_wo(_matmul(s["att"], dx16, "tn", F32, f"l{l}_b_mla_dwo"))
            dqr, dkv, dkpe = _flash_bwd(s["qr"], s["kv"], s["kpe"], s["att"], do, s["lse"], f"l{l}_b_mla_attn")
            dq = _rope_heads(dqr, tables, -1.0, MLA_SCALE, f"l{l}_b_mla_rope")
            g["mla_w_uq"][o] = _unpad_wuq(_matmul(s["qn"], dq, "tn", F32, f"l{l}_b_mla_dwuq"))
            dqn = _matmul(dq, wb["mla_wuq"], "nt", F32, f"l{l}_b_mla_dqn", layer=o)
            g["mla_w_ukv"][o] = _matmul(s["kvn"], dkv, "tn", F32, f"l{l}_b_mla_dwukv")
            dkvn = _matmul(dkv, wb["mla_w_ukv"], "nt", F32, f"l{l}_b_mla_dkvn", layer=o)
            dcp, dqg, dkvg = _mla_prep_bwd(s["cp"], dqn, dkvn, dkpe, ws["mla_q_norm_g"][o], ws["mla_kv_norm_g"][o],
                                           tables, f"l{l}_b_mla_prep")
            g["mla_q_norm_g"][o], g["mla_kv_norm_g"][o] = dqg[0], dkvg[0]
            g["mla_w_dq_dkv"][o] = _unpad_wdq(_matmul(s["h1"], dcp, "tn", F32, f"l{l}_b_mla_dwdq"))
            dx, dx16, dg = _matmul_rms_bwd(dcp, wb["mla_wdq"], o, s["x0"], ws["norm_mix_g"][l], dx, f"l{l}_b_mla_dh")
        g["norm_mix_g"][l] = dg[0]
    grads = {k: jnp.stack(v) for k, v in g.items() if k not in stk}
    grads.update(stk)
    grads["final_norm_g"] = dg_final[0]
    return loss, dx, grads


BIG = (
    ("xa_wq", "row"), ("xa_wkv", "col"), ("xa_wo", "row"), ("ffn_w_up", "col"), ("ffn_w_down", "row"),
    ("pc_w_in", "col"), ("pc_w_out", "row"), ("mla_w_dq_dkv", "row"), ("mla_w_uq", "col"), ("mla_w_ukv", "col"),
    ("mla_w_o", "row"),
)
SMALL_SHARDED = ("ffn_conv_w", "conv_dw_w", "mla_q_norm_g", "mla_kv_norm_g")
SMALL_REPLICATED = ("norm_mix_g", "norm_xa_g", "norm_mem_g", "norm_ffn_g", "ffn_conv_b", "pool_w", "pool_scale",
                    "conv_dw_b", "conv_ln_g", "conv_ln_b", "final_norm_g")
WEIGHTS = ("norm_mix_g", "norm_xa_g", "norm_mem_g", "xa_wq", "xa_wkv", "xa_wo", "norm_ffn_g", "ffn_w_up", "ffn_conv_w",
           "ffn_conv_b", "ffn_w_down", "pc_w_in", "pool_w", "pool_scale", "conv_dw_w", "conv_dw_b", "conv_ln_g",
           "conv_ln_b", "pc_w_out", "mla_w_dq_dkv", "mla_q_norm_g", "mla_w_uq", "mla_kv_norm_g", "mla_w_ukv", "mla_w_o",
           "final_norm_g")
PACK_ROW = SUBLANE * LANE


def _pack(arrays):
    flat = jnp.concatenate([a.reshape(-1) for a in arrays])
    n = flat.shape[0]
    pad = (-n) % PACK_ROW
    return jnp.pad(flat, (0, pad)).reshape(-1, LANE)


def _unpack(flat, shapes):
    out, off = [], 0
    for s in shapes:
        n = int(np.prod(s))
        out.append(flat[off : off + n].reshape(s))
        off += n
    return out


def kernel(x, mem, positions, norm_mix_g, norm_xa_g, norm_mem_g, xa_wq, xa_wkv, xa_wo, norm_ffn_g, ffn_w_up, ffn_conv_w, ffn_conv_b, ffn_w_down, pc_w_in, pool_w, pool_scale, conv_dw_w, conv_dw_b, conv_ln_g, conv_ln_b, pc_w_out, mla_w_dq_dkv, mla_q_norm_g, mla_w_uq, mla_kv_norm_g, mla_w_ukv, mla_w_o, final_norm_g, loss_target, m_norm_mix_g, m_norm_xa_g, m_norm_mem_g, m_xa_wq, m_xa_wkv, m_xa_wo, m_norm_ffn_g, m_ffn_w_up, m_ffn_conv_w, m_ffn_conv_b, m_ffn_w_down, m_pc_w_in, m_pool_w, m_pool_scale, m_conv_dw_w, m_conv_dw_b, m_conv_ln_g, m_conv_ln_b, m_pc_w_out, m_mla_w_dq_dkv, m_mla_q_norm_g, m_mla_w_uq, m_mla_kv_norm_g, m_mla_w_ukv, m_mla_w_o, m_final_norm_g, v_norm_mix_g, v_norm_xa_g, v_norm_mem_g, v_xa_wq, v_xa_wkv, v_xa_wo, v_norm_ffn_g, v_ffn_w_up, v_ffn_conv_w, v_ffn_conv_b, v_ffn_w_down, v_pc_w_in, v_pool_w, v_pool_scale, v_conv_dw_w, v_conv_dw_b, v_conv_ln_g, v_conv_ln_b, v_pc_w_out, v_mla_w_dq_dkv, v_mla_q_norm_g, v_mla_w_uq, v_mla_kv_norm_g, v_mla_w_ukv, v_mla_w_o, v_final_norm_g):
    args = dict(locals())
    w = {n: args[n] for n in WEIGHTS}
    m = {n: args["m_" + n] for n in WEIGHTS}
    v = {n: args["v_" + n] for n in WEIGHTS}
    cx, cy, cc = lax.axis_index("x"), lax.axis_index("y"), lax.axis_index("c")
    chip = 2 * cx + cy

    full = _all_gather_weights([w[n].astype(MXU_DT) for n, _ in BIG], [k for _, k in BIG], "gather_weights")
    full = dict(zip([n for n, _ in BIG], full))
    small_shapes = [w[n].shape for n in SMALL_SHARDED]
    gathered = _all_gather_rows(_pack([w[n] for n in SMALL_SHARDED]), "gather_small")
    gathered = gathered.reshape(8, -1)
    ws = {n: w[n] for n in SMALL_REPLICATED}
    pieces = [_unpack(gathered[2 * k], small_shapes) for k in range(4)]
    for i, n in enumerate(SMALL_SHARDED):
        ws[n] = jnp.concatenate([pieces[k][i] for k in range(4)], axis=-1)
    wb = {n: full[n] for n in ("xa_wq", "xa_wkv", "xa_wo", "ffn_w_up", "ffn_w_down", "pc_w_in", "pc_w_out", "mla_w_ukv")}
    wb["mla_wdq"] = jnp.stack([_pad_wdq(full["mla_w_dq_dkv"][o]) for o in range(DEPTH // 2)])
    wb["mla_wuq"] = jnp.stack([_pad_wuq(full["mla_w_uq"][o]) for o in range(DEPTH // 2)])
    wb["mla_wo"] = jnp.stack([_pad_wo(full["mla_w_o"][o]) for o in range(DEPTH // 2)])

    loss, grad_x, grads = _local_step(x[0], mem[0], positions[0], loss_target[0], wb, ws)
    loss = lax.psum(loss[0, 0], ("x", "y", "c"))

    kinds = [k for _, k in BIG]
    big = [grads[n] for n, _ in BIG]
    c_idx = cc.reshape(1).astype(jnp.int32)
    chip_idx = chip.reshape(1).astype(jnp.int32)
    theirs = _exchange_halves(big, "reduce_pair")
    pair = [_add_half(gr, th, c_idx, f"reduce_pair_add_{n}") for gr, th, (n, _) in zip(big, theirs, BIG)]
    slots = _scatter_to_chips(pair, kinds, "reduce_chips")
    halves = [_sum_chips(sl, pr, chip_idx, kind, f"reduce_chips_add_{n}")
              for sl, pr, (n, kind) in zip(slots, pair, BIG)]
    others = _swap_halves(halves, "reduce_join")
    gsum, delta, new_m, new_v = {}, {}, {}, {}
    for mine, other, (n, _) in zip(halves, others, BIG):
        gsum[n], delta[n], new_m[n], new_v[n] = _adamw_halves(w[n], mine, other, m[n], v[n], c_idx, f"adamw_{n}")

    small_names = SMALL_REPLICATED + SMALL_SHARDED
    small_grad_shapes = [grads[n].shape for n in small_names]
    packed = _pack([grads[n] for n in small_names])
    rows = packed.shape[0]
    allparts = _all_gather_rows(packed, "gather_small_grads").reshape(8, rows, LANE)
    total = _sum_leading(allparts, "sum_small_grads").reshape(-1)
    for n, gfull in zip(small_names, _unpack(total, small_grad_shapes)):
        if n in SMALL_SHARDED:
            width = w[n].shape[-1]
            gfull = lax.dynamic_slice_in_dim(gfull, chip * width, width, axis=gfull.ndim - 1)
        gsum[n] = gfull

    for n in SMALL_REPLICATED + SMALL_SHARDED:
        delta[n], new_m[n], new_v[n] = _adamw(w[n], gsum[n], m[n], v[n], f"adamw_{n}")
    return (loss, grad_x[None], *[gsum[n] for n in WEIGHTS], *[delta[n] for n in WEIGHTS],
            *[new_m[n] for n in WEIGHTS], *[new_v[n] for n in WEIGHTS])
```

```python
import functools
import math

import numpy as np
import jax
import jax.numpy as jnp
from jax import lax
from jax.experimental import pallas as pl
from jax.experimental.pallas import tpu as pltpu

F32 = jnp.float32
MXU_DT = jnp.bfloat16
XFER_DT = jnp.bfloat16

D_MODEL = 1024
DEPTH = 4
MEM_LEN = 256
XA_HEADS = 4
XA_HEAD_DIM = 256
POOL_W = 512
POOL_WINDOWS = (2, 4, 8, 16)
POOL_GROUP = 128
CONV_W = 512
CONV_K = 31
MLA_HEADS = 16
QK_NOPE = 64
QK_ROPE = 32
V_HEAD = 64
Q_LORA = 384
KV_LORA = 256
ROPE_THETA = 10000.0
MLA_SCALE = 1.0 / math.sqrt(QK_NOPE + QK_ROPE)
LOG2E = math.log2(math.e)
D_FF = 2816
FFN_CONV_K = 3
EPS = 1e-6
NEG = -1e30
ADAM_LR = 0.001
ADAM_B1 = 0.9
ADAM_B2 = 0.999
ADAM_EPS = 1e-08
ADAM_WD = 0.01
ADAM_STEP = 10

HEAD_PAD = 128
C_PAD = 768
KPE_LANE = 64

VMEM_LIMIT = 52 * 1024 * 1024
BLOCK_BYTES = 6 * 1024 * 1024
LANE = 128
SUBLANE = 8

TM = 1024
TN = 1408
TK = 2048
TT = 512
TW = 256
TWF = 128
FFN_CHUNK = 256
TA = 1024
MIX_HALO = 32
FFN_HALO = 8

NN = (((1,), (0,)), ((), ()))
NT = (((1,), (1,)), ((), ()))
TN_DIMS = (((0,), (0,)), ((), ()))
MESH_ID = pl.DeviceIdType.MESH


def _cparams(*sem):
    return pltpu.CompilerParams(dimension_semantics=sem, vmem_limit_bytes=VMEM_LIMIT)


def _tile(n, pref, limit=None):
    cap = pref if limit is None else min(pref, limit)
    if n <= cap:
        return n
    t = (cap // LANE) * LANE
    while t >= LANE:
        if n % t == 0:
            return t
        t -= LANE
    return n


def _rows(t, pref):
    return t if t <= pref else pref


def _sigmoid(x):
    return 1.0 / (1.0 + jnp.exp(-x))


def _matmul(a, b, mode, out_dtype, name, layer=None, res=None, stack=None):
    if layer is None:
        b2 = b.shape
    else:
        b2 = b.shape[1:]
    if mode == "tn":
        k, m = a.shape
        k2, n = b2
    elif mode == "nn":
        m, k = a.shape
        k2, n = b2
    else:
        m, k = a.shape
        n, k2 = b2
    assert k == k2, (a.shape, b.shape, mode)
    isz_a = jnp.dtype(a.dtype).itemsize
    isz_b = jnp.dtype(b.dtype).itemsize
    if mode == "tn":
        tk = _tile(k, TK * 2 // max(isz_a, isz_b))
        tm = _tile(m, TN, BLOCK_BYTES // (tk * isz_a))
        tn = _tile(n, TN, BLOCK_BYTES // (tk * isz_b))
    else:
        tk = k
        tn = _tile(n, TN, BLOCK_BYTES // (tk * isz_b))
        tm = _tile(m, TM, min(BLOCK_BYTES // (tk * isz_a), BLOCK_BYTES // (tn * 4)))
    nk = k // tk
    grid = (m // tm, n // tn, nk)
    if mode == "nn":
        a_spec = pl.BlockSpec((tm, tk), lambda i, j, kk: (i, kk))
        b_blk, b_map, dn = (tk, tn), (lambda i, j, kk: (kk, j)), NN
    elif mode == "nt":
        a_spec = pl.BlockSpec((tm, tk), lambda i, j, kk: (i, kk))
        b_blk, b_map, dn = (tn, tk), (lambda i, j, kk: (j, kk)), NT
    else:
        a_spec = pl.BlockSpec((tk, tm), lambda i, j, kk: (kk, i))
        b_blk, b_map, dn = (tk, tn), (lambda i, j, kk: (kk, j)), TN_DIMS
    if layer is None:
        b_spec = pl.BlockSpec(b_blk, b_map)
    else:
        b_spec = pl.BlockSpec((None,) + b_blk, lambda i, j, kk: (layer,) + b_map(i, j, kk))
    in_specs = [a_spec, b_spec]
    args = [a, b]
    if res is not None:
        in_specs.append(pl.BlockSpec((tm, tn), lambda i, j, kk: (i, j)))
        args.append(res)
    has_res = res is not None
    aliases = {}
    if stack is None:
        o_spec = pl.BlockSpec((tm, tn), lambda i, j, kk: (i, j))
        out_shape = jax.ShapeDtypeStruct((m, n), out_dtype)
    else:
        buf, slab, nslab = stack
        o_spec = pl.BlockSpec((None, tm, tn), lambda i, j, kk: (slab, i, j))
        out_shape = jax.ShapeDtypeStruct((nslab, m, n), out_dtype)
        if buf is not None:
            in_specs.append(pl.BlockSpec(memory_space=pl.ANY))
            args.append(buf)
            aliases = {len(args) - 1: 0}
    n_in = len(args)

    def body(*refs):
        a_ref, b_ref = refs[0], refs[1]
        r_ref = refs[2] if has_res else None
        o_ref = refs[n_in]
        p = lax.dot_general(a_ref[...].astype(MXU_DT), b_ref[...].astype(MXU_DT), dn, preferred_element_type=F32)
        if nk == 1:
            if has_res:
                p = r_ref[...] + p
            o_ref[...] = p.astype(o_ref.dtype)
        else:
            acc_ref = refs[-1]
            kk = pl.program_id(2)

            @pl.when(kk == 0)
            def _():
                acc_ref[...] = jnp.zeros_like(acc_ref)

            acc_ref[...] += p

            @pl.when(kk == nk - 1)
            def _():
                r = acc_ref[...]
                if has_res:
                    r = r_ref[...] + r
                o_ref[...] = r.astype(o_ref.dtype)

    scratch = [pltpu.VMEM((tm, tn), F32)] if nk > 1 else []
    return pl.pallas_call(
        body,
        grid=grid,
        in_specs=in_specs,
        out_specs=o_spec,
        out_shape=out_shape,
        scratch_shapes=scratch,
        input_output_aliases=aliases,
        name=name,
        compiler_params=_cparams("parallel", "parallel", "arbitrary"),
    )(*args)


def _row_tile(m, k, isz):
    return _tile(m, TM, min(BLOCK_BYTES // (k * isz), TM if k <= 2 * TK else TM // 4))


def _matmul_res_norm(a, b, layer, res, gain, name):
    m, k = a.shape
    n = b.shape[-1]
    tm = _row_tile(m, k, jnp.dtype(a.dtype).itemsize)

    def body(a_ref, b_ref, r_ref, g_ref, x_ref, h_ref):
        x = r_ref[...] + jnp.dot(a_ref[...].astype(MXU_DT), b_ref[...].astype(MXU_DT), preferred_element_type=F32)
        x_ref[...] = x
        r = lax.rsqrt(jnp.mean(x * x, axis=-1, keepdims=True) + EPS)
        h_ref[...] = ((x * r) * g_ref[...]).astype(h_ref.dtype)

    row = pl.BlockSpec((tm, n), lambda i: (i, 0))
    return pl.pallas_call(
        body,
        grid=(m // tm,),
        in_specs=[pl.BlockSpec((tm, k), lambda i: (i, 0)), pl.BlockSpec((None, k, n), lambda i: (layer, 0, 0)), row,
                  pl.BlockSpec((1, n), lambda i: (0, 0))],
        out_specs=[row, row],
        out_shape=[jax.ShapeDtypeStruct((m, n), F32), jax.ShapeDtypeStruct((m, n), MXU_DT)],
        name=name,
        compiler_params=_cparams("parallel"),
    )(a, b, res, gain.reshape(1, n))


def _matmul_rms_bwd(a, b, layer, x, gain, dx_in, name):
    m, k = a.shape
    n = b.shape[-2]
    tm = _row_tile(m, k, jnp.dtype(a.dtype).itemsize)

    def body(a_ref, b_ref, x_ref, g_ref, dxi_ref, dx_ref, dx16_ref, dg_ref):
        @pl.when(pl.program_id(0) == 0)
        def _():
            dg_ref[...] = jnp.zeros_like(dg_ref)

        dh = lax.dot_general(a_ref[...].astype(MXU_DT), b_ref[...].astype(MXU_DT), NT, preferred_element_type=F32)
        xf = x_ref[...]
        r = lax.rsqrt(jnp.mean(xf * xf, axis=-1, keepdims=True) + EPS)
        xh = xf * r
        gy = dh * g_ref[...]
        dx = dxi_ref[...] + r * (gy - xh * jnp.mean(gy * xh, axis=-1, keepdims=True))
        dx_ref[...] = dx
        dx16_ref[...] = dx.astype(dx16_ref.dtype)
        dg_ref[...] += jnp.sum(dh * xh, axis=0, keepdims=True)

    row = pl.BlockSpec((tm, n), lambda i: (i, 0))
    vec = pl.BlockSpec((1, n), lambda i: (0, 0))
    return pl.pallas_call(
        body,
        grid=(m // tm,),
        in_specs=[pl.BlockSpec((tm, k), lambda i: (i, 0)), pl.BlockSpec((None, n, k), lambda i: (layer, 0, 0)), row, vec, row],
        out_specs=[row, row, vec],
        out_shape=[jax.ShapeDtypeStruct((m, n), F32), jax.ShapeDtypeStruct((m, n), MXU_DT), jax.ShapeDtypeStruct((1, n), F32)],
        name=name,
        compiler_params=_cparams("arbitrary"),
    )(a, b, x, gain.reshape(1, n), dx_in)


def _rms_fwd(x, g, name):
    t, d = x.shape
    tt = _rows(t, TT)

    def body(x_ref, g_ref, o_ref):
        xf = x_ref[...]
        r = lax.rsqrt(jnp.mean(xf * xf, axis=-1, keepdims=True) + EPS)
        o_ref[...] = ((xf * r) * g_ref[...]).astype(o_ref.dtype)

    return pl.pallas_call(
        body,
        grid=(t // tt,),
        in_specs=[pl.BlockSpec((tt, d), lambda i: (i, 0)), pl.BlockSpec((1, d), lambda i: (0, 0))],
        out_specs=pl.BlockSpec((tt, d), lambda i: (i, 0)),
        out_shape=jax.ShapeDtypeStruct((t, d), MXU_DT),
        name=name,
        compiler_params=_cparams("parallel"),
    )(x, g.reshape(1, d))


def _rms_bwd(dh, x, g, dx_in, name):
    t, d = x.shape
    tt = _rows(t, TT)

    def body(dh_ref, x_ref, g_ref, dxi_ref, dx_ref, dg_ref):
        @pl.when(pl.program_id(0) == 0)
        def _():
            dg_ref[...] = jnp.zeros_like(dg_ref)

        xf = x_ref[...]
        dh_v = dh_ref[...]
        r = lax.rsqrt(jnp.mean(xf * xf, axis=-1, keepdims=True) + EPS)
        xh = xf * r
        gy = dh_v * g_ref[...]
        dx = r * (gy - xh * jnp.mean(gy * xh, axis=-1, keepdims=True))
        dx_ref[...] = dxi_ref[...] + dx
        dg_ref[...] += jnp.sum(dh_v * xh, axis=0, keepdims=True)

    row = pl.BlockSpec((tt, d), lambda i: (i, 0))
    vec = pl.BlockSpec((1, d), lambda i: (0, 0))
    return pl.pallas_call(
        body,
        grid=(t // tt,),
        in_specs=[row, row, vec, row],
        out_specs=[row, vec],
        out_shape=[jax.ShapeDtypeStruct((t, d), F32), jax.ShapeDtypeStruct((1, d), F32)],
        name=name,
        compiler_params=_cparams("arbitrary"),
    )(dh, x, g.reshape(1, d), dx_in)


def _rms_bwd_gain(dh, x, g, name):
    t, d = x.shape
    tt = _rows(t, TT)

    def body(dh_ref, x_ref, dg_ref):
        @pl.when(pl.program_id(0) == 0)
        def _():
            dg_ref[...] = jnp.zeros_like(dg_ref)

        xf = x_ref[...]
        r = lax.rsqrt(jnp.mean(xf * xf, axis=-1, keepdims=True) + EPS)
        dg_ref[...] += jnp.sum(dh_ref[...] * (xf * r), axis=0, keepdims=True)

    row = pl.BlockSpec((tt, d), lambda i: (i, 0))
    vec = pl.BlockSpec((1, d), lambda i: (0, 0))
    return pl.pallas_call(
        body,
        grid=(t // tt,),
        in_specs=[row, row],
        out_specs=vec,
        out_shape=jax.ShapeDtypeStruct((1, d), F32),
        name=name,
        compiler_params=_cparams("arbitrary"),
    )(dh, x)


def _loss_head(x, target, g, name):
    t, d = x.shape
    tt = _rows(t, TT)

    def body(x_ref, t_ref, g_ref, dx_ref, dx16_ref, dg_ref, loss_ref):
        @pl.when(pl.program_id(0) == 0)
        def _():
            dg_ref[...] = jnp.zeros_like(dg_ref)
            loss_ref[...] = jnp.zeros_like(loss_ref)

        xf = x_ref[...]
        gv = g_ref[...]
        r = lax.rsqrt(jnp.mean(xf * xf, axis=-1, keepdims=True) + EPS)
        xh = xf * r
        err = xh * gv - t_ref[...]
        e2 = jnp.sum(err * err, axis=-1, keepdims=True)
        loss_ref[...] += (0.5 / d) * jnp.sum(e2, axis=0, keepdims=True)
        dy = err * (1.0 / d)
        gy = dy * gv
        dx = r * (gy - xh * jnp.mean(gy * xh, axis=-1, keepdims=True))
        dx_ref[...] = dx
        dx16_ref[...] = dx.astype(dx16_ref.dtype)
        dg_ref[...] += jnp.sum(dy * xh, axis=0, keepdims=True)

    row = pl.BlockSpec((tt, d), lambda i: (i, 0))
    vec = pl.BlockSpec((1, d), lambda i: (0, 0))
    return pl.pallas_call(
        body,
        grid=(t // tt,),
        in_specs=[row, row, vec],
        out_specs=[row, row, vec, pl.BlockSpec((1, 1), lambda i: (0, 0))],
        out_shape=[
            jax.ShapeDtypeStruct((t, d), F32),
            jax.ShapeDtypeStruct((t, d), MXU_DT),
            jax.ShapeDtypeStruct((1, d), F32),
            jax.ShapeDtypeStruct((1, 1), F32),
        ],
        name=name,
        compiler_params=_cparams("arbitrary"),
    )(x, target, g.reshape(1, d))


def _prev_halo(tt, hp, width):
    return pl.BlockSpec((hp, width), lambda i: (jnp.maximum(i * (tt // hp) - 1, 0), 0))


def _next_halo(tt, hp, width, t):
    return pl.BlockSpec((hp, width), lambda i: (jnp.minimum((i + 1) * (tt // hp), t // hp - 1), 0))


def _ffn_chunks():
    return [(c0, FFN_CHUNK) for c0 in range(0, D_FF, FFN_CHUNK)]


def _ffn_fwd(up, conv_w, conv_b, name):
    t = up.shape[0]
    tt = _rows(t, 2 * TWF)
    hp = FFN_HALO

    def body(up_ref, gp_ref, w_ref, b_ref, act_ref, ext_ref):
        first = pl.program_id(0) == 0
        for c0, cw in _ffn_chunks():
            ga = pl.ds(D_FF + c0, cw)
            ext_ref[0:hp, :] = jnp.where(first, 0.0, gp_ref[:, ga])
            ext_ref[hp : hp + tt, :] = up_ref[:, ga]
            gc = b_ref[:, pl.ds(c0, cw)]
            for j in range(FFN_CONV_K):
                off = hp - (FFN_CONV_K - 1) + j
                gc = gc + w_ref[j : j + 1, pl.ds(c0, cw)] * ext_ref[off : off + tt, :]
            a = up_ref[:, pl.ds(c0, cw)]
            act_ref[:, pl.ds(c0, cw)] = (gc * _sigmoid(gc) * a).astype(act_ref.dtype)

    return pl.pallas_call(
        body,
        grid=(t // tt,),
        in_specs=[
            pl.BlockSpec((tt, 2 * D_FF), lambda i: (i, 0)),
            _prev_halo(tt, hp, 2 * D_FF),
            pl.BlockSpec((FFN_CONV_K, D_FF), lambda i: (0, 0)),
            pl.BlockSpec((1, D_FF), lambda i: (0, 0)),
        ],
        out_specs=pl.BlockSpec((tt, D_FF), lambda i: (i, 0)),
        out_shape=jax.ShapeDtypeStruct((t, D_FF), MXU_DT),
        scratch_shapes=[pltpu.VMEM((tt + hp, FFN_CHUNK), F32)],
        name=name,
        compiler_params=_cparams("parallel"),
    )(up, up, conv_w, conv_b.reshape(1, D_FF))


def _ffn_bwd(up, dact, conv_w, conv_b, name):
    t = up.shape[0]
    tt = _rows(t, TWF)
    hp = FFN_HALO
    nt = t // tt
    kk = FFN_CONV_K

    def body(up_ref, upp_ref, upn_ref, da_ref, dan_ref, w_ref, b_ref, dup_ref, dw_ref, db_ref, ext_ref, dgc_ref):
        i = pl.program_id(0)
        first = i == 0
        last = i == nt - 1

        @pl.when(first)
        def _():
            dw_ref[...] = jnp.zeros_like(dw_ref)
            db_ref[...] = jnp.zeros_like(db_ref)

        for c0, cw in _ffn_chunks():
            ca = pl.ds(c0, cw)
            ga = pl.ds(D_FF + c0, cw)
            ext_ref[0:hp, :] = jnp.where(first, 0.0, upp_ref[:, ga])
            ext_ref[hp : hp + tt, :] = up_ref[:, ga]
            ext_ref[hp + tt : hp + tt + hp, :] = upn_ref[:, ga]
            gc = b_ref[:, ca]
            for j in range(kk):
                off = hp - (kk - 1) + j
                gc = gc + w_ref[j : j + 1, ca] * ext_ref[off : off + tt + hp, :]
            sg = _sigmoid(gc)
            silu = gc * sg
            dsilu = sg * (1.0 + gc * (1.0 - sg))
            a_all = jnp.concatenate([up_ref[:, ca], upn_ref[:, ca]], axis=0)
            dact_all = jnp.concatenate([da_ref[:, ca], jnp.where(last, 0.0, dan_ref[:, ca])], axis=0)
            dgc = dact_all * a_all * dsilu
            dgc_ref[...] = dgc
            dup_ref[:, ca] = (dact_all[0:tt] * silu[0:tt]).astype(dup_ref.dtype)
            dg = jnp.zeros((tt, cw), F32)
            for j in range(kk):
                dg = dg + w_ref[j : j + 1, ca] * dgc_ref[kk - 1 - j : kk - 1 - j + tt, :]
            dup_ref[:, ga] = dg.astype(dup_ref.dtype)
            dgc_t = dgc[0:tt]
            db_ref[:, ca] += jnp.sum(dgc_t, axis=0, keepdims=True)
            for j in range(kk):
                off = hp - (kk - 1) + j
                dw_ref[j : j + 1, ca] += jnp.sum(dgc_t * ext_ref[off : off + tt, :], axis=0, keepdims=True)

    return pl.pallas_call(
        body,
        grid=(nt,),
        in_specs=[
            pl.BlockSpec((tt, 2 * D_FF), lambda i: (i, 0)),
            _prev_halo(tt, hp, 2 * D_FF),
            _next_halo(tt, hp, 2 * D_FF, t),
            pl.BlockSpec((tt, D_FF), lambda i: (i, 0)),
            _next_halo(tt, hp, D_FF, t),
            pl.BlockSpec((kk, D_FF), lambda i: (0, 0)),
            pl.BlockSpec((1, D_FF), lambda i: (0, 0)),
        ],
        out_specs=[
            pl.BlockSpec((tt, 2 * D_FF), lambda i: (i, 0)),
            pl.BlockSpec((kk, D_FF), lambda i: (0, 0)),
            pl.BlockSpec((1, D_FF), lambda i: (0, 0)),
        ],
        out_shape=[
            jax.ShapeDtypeStruct((t, 2 * D_FF), MXU_DT),
            jax.ShapeDtypeStruct((kk, D_FF), F32),
            jax.ShapeDtypeStruct((1, D_FF), F32),
        ],
        scratch_shapes=[pltpu.VMEM((tt + 2 * hp, FFN_CHUNK), F32), pltpu.VMEM((tt + hp, FFN_CHUNK), F32)],
        name=name,
        compiler_params=_cparams("arbitrary"),
    )(up, up, up, dact, dact, conv_w, conv_b.reshape(1, D_FF))


def _layernorm_silu(cv, ln_g, ln_b):
    mu = jnp.mean(cv, axis=-1, keepdims=True)
    xc = cv - mu
    rstd = lax.rsqrt(jnp.mean(xc * xc, axis=-1, keepdims=True) + EPS)
    xh = xc * rstd
    a = xh * ln_g + ln_b
    return xh, rstd, a


def _shifted_copies(ref, n):
    for b in range(1, SUBLANE):
        ref[b, 0 : n - SUBLANE, :] = ref[0, b : b + n - SUBLANE, :]


def _tap(ref, offset, rows, cols):
    b = offset % SUBLANE
    return ref[b, offset - b : offset - b + rows, cols]


def _mix_fwd(z, pool_w, pool_scale, dw_w, dw_b, ln_g, ln_b, name):
    t = z.shape[0]
    tt = _rows(t, TW)
    hp = MIX_HALO
    zw = POOL_W + 2 * CONV_W

    def body(z_ref, zp_ref, pw_ref, ps_ref, w_ref, b_ref, lg_ref, lb_ref, cat_ref, eu_ref, egl_ref, cv_ref):
        i = pl.program_id(0)
        first = i == 0
        eu_ref[0:hp, :] = jnp.where(first, 0.0, zp_ref[:, 0:POOL_W])
        eu_ref[hp : hp + tt, :] = z_ref[:, 0:POOL_W]
        glp = zp_ref[:, POOL_W : POOL_W + CONV_W] * _sigmoid(zp_ref[:, POOL_W + CONV_W : zw])
        egl_ref[0, 0:hp, :] = jnp.where(first, 0.0, glp)
        egl_ref[0, hp : hp + tt, :] = z_ref[:, POOL_W : POOL_W + CONV_W] * _sigmoid(z_ref[:, POOL_W + CONV_W : zw])
        _shifted_copies(egl_ref, tt + hp)
        row = i * tt + lax.broadcasted_iota(jnp.int32, (tt, 1), 0)
        for gi, w in enumerate(POOL_WINDOWS):
            cols = pl.ds(gi * POOL_GROUP, POOL_GROUP)
            u = eu_ref[hp : hp + tt, cols]
            acc = u
            for k in range(1, w):
                acc = acc + eu_ref[hp - k : hp - k + tt, cols]
            cnt = jnp.minimum(row + 1, w).astype(F32)
            pooled = acc / cnt - u
            y = jnp.dot(pooled.astype(MXU_DT), pw_ref[gi].astype(MXU_DT), preferred_element_type=F32)
            cat_ref[:, cols] = (y * ps_ref[:, cols]).astype(cat_ref.dtype)
        for c0 in range(0, CONV_W, LANE):
            cs = pl.ds(c0, LANE)
            acc = jnp.broadcast_to(b_ref[:, cs], (tt, LANE))
            for j in range(CONV_K):
                acc = acc + w_ref[j : j + 1, cs] * _tap(egl_ref, hp - (CONV_K - 1) + j, tt, cs)
            cv_ref[:, cs] = acc
        _, _, a = _layernorm_silu(cv_ref[...], lg_ref[...], lb_ref[...])
        cat_ref[:, POOL_W : POOL_W + CONV_W] = (a * _sigmoid(a)).astype(cat_ref.dtype)

    vec = pl.BlockSpec((1, CONV_W), lambda i: (0, 0))
    return pl.pallas_call(
        body,
        grid=(t // tt,),
        in_specs=[
            pl.BlockSpec((tt, zw), lambda i: (i, 0)),
            _prev_halo(tt, hp, zw),
            pl.BlockSpec((len(POOL_WINDOWS), POOL_GROUP, POOL_GROUP), lambda i: (0, 0, 0)),
            vec,
            pl.BlockSpec((CONV_K, CONV_W), lambda i: (0, 0)),
            vec,
            vec,
            vec,
        ],
        out_specs=pl.BlockSpec((tt, POOL_W + CONV_W), lambda i: (i, 0)),
        out_shape=jax.ShapeDtypeStruct((t, POOL_W + CONV_W), MXU_DT),
        scratch_shapes=[pltpu.VMEM((tt + hp, POOL_W), F32), pltpu.VMEM((SUBLANE, tt + hp, CONV_W), F32),
                        pltpu.VMEM((tt, CONV_W), F32)],
        name=name,
        compiler_params=_cparams("parallel"),
    )(z, z, pool_w, pool_scale.reshape(1, POOL_W), dw_w, dw_b.reshape(1, CONV_W), ln_g.reshape(1, CONV_W), ln_b.reshape(1, CONV_W))


def _mix_bwd(z, dcat, pool_w, pool_scale, dw_w, dw_b, ln_g, ln_b, name):
    t = z.shape[0]
    tt = _rows(t, TW)
    hp = MIX_HALO
    nt = t // tt
    zw = POOL_W + 2 * CONV_W
    ng = len(POOL_WINDOWS)

    def body(z_ref, zp_ref, zn_ref, dc_ref, dcn_ref, pw_ref, ps_ref, w_ref, b_ref, lg_ref, lb_ref,
             dz_ref, dpw_ref, dps_ref, dww_ref, dwb_ref, dlg_ref, dlb_ref, eu_ref, ee_ref, egl_ref, edcv_ref, cv_ref):
        i = pl.program_id(0)
        first = i == 0
        last = i == nt - 1

        @pl.when(first)
        def _():
            for r in (dpw_ref, dps_ref, dww_ref, dwb_ref, dlg_ref, dlb_ref):
                r[...] = jnp.zeros_like(r)

        eu_ref[0:hp, :] = jnp.where(first, 0.0, zp_ref[:, 0:POOL_W])
        eu_ref[hp : hp + tt, :] = z_ref[:, 0:POOL_W]
        row = i * tt + lax.broadcasted_iota(jnp.int32, (tt, 1), 0)
        row_ext = i * tt + lax.broadcasted_iota(jnp.int32, (tt + hp, 1), 0)
        for gi, w in enumerate(POOL_WINDOWS):
            cols = pl.ds(gi * POOL_GROUP, POOL_GROUP)
            u = eu_ref[hp : hp + tt, cols]
            acc = u
            for k in range(1, w):
                acc = acc + eu_ref[hp - k : hp - k + tt, cols]
            pooled = (acc / jnp.minimum(row + 1, w).astype(F32) - u).astype(MXU_DT)
            pw = pw_ref[gi].astype(MXU_DT)
            dya = dc_ref[:, cols]
            y = jnp.dot(pooled, pw, preferred_element_type=F32)
            dps_ref[:, cols] += jnp.sum(dya * y, axis=0, keepdims=True)
            scale = ps_ref[:, cols]
            dy_all = jnp.concatenate([dya, jnp.where(last, 0.0, dcn_ref[:, cols])], axis=0) * scale
            dy_all = dy_all.astype(MXU_DT)
            dpw_ref[gi] += lax.dot_general(pooled, dy_all[0:tt], TN_DIMS, preferred_element_type=F32)
            dpooled = lax.dot_general(dy_all, pw, NT, preferred_element_type=F32)
            ee_ref[:, cols] = dpooled / jnp.minimum(row_ext + 1, w).astype(F32)
            du = -dpooled[0:tt]
            for k in range(w):
                du = du + ee_ref[k : k + tt, cols]
            dz_ref[:, cols] = du.astype(dz_ref.dtype)

        ca = slice(POOL_W, POOL_W + CONV_W)
        cb = slice(POOL_W + CONV_W, zw)
        egl_ref[0, 0:hp, :] = jnp.where(first, 0.0, zp_ref[:, ca] * _sigmoid(zp_ref[:, cb]))
        egl_ref[0, hp : hp + tt, :] = z_ref[:, ca] * _sigmoid(z_ref[:, cb])
        egl_ref[0, hp + tt : hp + tt + hp, :] = zn_ref[:, ca] * _sigmoid(zn_ref[:, cb])
        _shifted_copies(egl_ref, tt + 2 * hp)
        for c0 in range(0, CONV_W, LANE):
            cs = pl.ds(c0, LANE)
            acc = jnp.broadcast_to(b_ref[:, cs], (tt + hp, LANE))
            for j in range(CONV_K):
                acc = acc + w_ref[j : j + 1, cs] * _tap(egl_ref, hp - (CONV_K - 1) + j, tt + hp, cs)
            cv_ref[:, cs] = acc
        lg = lg_ref[...]
        xh, rstd, a = _layernorm_silu(cv_ref[...], lg, lb_ref[...])
        sa = _sigmoid(a)
        dyb = jnp.concatenate([dc_ref[:, ca], jnp.where(last, 0.0, dcn_ref[:, ca])], axis=0)
        da = dyb * (sa * (1.0 + a * (1.0 - sa)))
        dlg_ref[...] += jnp.sum(da[0:tt] * xh[0:tt], axis=0, keepdims=True)
        dlb_ref[...] += jnp.sum(da[0:tt], axis=0, keepdims=True)
        dxh = da * lg
        dcv = rstd * (dxh - jnp.mean(dxh, axis=-1, keepdims=True) - xh * jnp.mean(dxh * xh, axis=-1, keepdims=True))
        edcv_ref[0] = dcv
        _shifted_copies(edcv_ref, tt + hp)
        dwb_ref[...] += jnp.sum(dcv[0:tt], axis=0, keepdims=True)
        for c0 in range(0, CONV_W, LANE):
            cs = pl.ds(c0, LANE)
            gl_t = egl_ref[0, hp : hp + tt, cs]
            dgl = jnp.zeros((tt, LANE), F32)
            for j in range(CONV_K):
                tap = _tap(edcv_ref, CONV_K - 1 - j, tt, cs)
                dww_ref[j : j + 1, cs] += jnp.sum(gl_t * tap, axis=0, keepdims=True)
                dgl = dgl + w_ref[j : j + 1, cs] * tap
            ga = z_ref[:, pl.ds(POOL_W + c0, LANE)]
            sgb = _sigmoid(z_ref[:, pl.ds(POOL_W + CONV_W + c0, LANE)])
            dz_ref[:, pl.ds(POOL_W + c0, LANE)] = (dgl * sgb).astype(dz_ref.dtype)
            dz_ref[:, pl.ds(POOL_W + CONV_W + c0, LANE)] = (dgl * ga * sgb * (1.0 - sgb)).astype(dz_ref.dtype)

    vec = pl.BlockSpec((1, CONV_W), lambda i: (0, 0))
    pw_spec = pl.BlockSpec((ng, POOL_GROUP, POOL_GROUP), lambda i: (0, 0, 0))
    w_spec = pl.BlockSpec((CONV_K, CONV_W), lambda i: (0, 0))
    return pl.pallas_call(
        body,
        grid=(nt,),
        in_specs=[
            pl.BlockSpec((tt, zw), lambda i: (i, 0)),
            _prev_halo(tt, hp, zw),
            _next_halo(tt, hp, zw, t),
            pl.BlockSpec((tt, POOL_W + CONV_W), lambda i: (i, 0)),
            _next_halo(tt, hp, POOL_W + CONV_W, t),
            pw_spec, vec, w_spec, vec, vec, vec,
        ],
        out_specs=[pl.BlockSpec((tt, zw), lambda i: (i, 0)), pw_spec, vec, w_spec, vec, vec, vec],
        out_shape=[
            jax.ShapeDtypeStruct((t, zw), MXU_DT),
            jax.ShapeDtypeStruct((ng, POOL_GROUP, POOL_GROUP), F32),
            jax.ShapeDtypeStruct((1, POOL_W), F32),
            jax.ShapeDtypeStruct((CONV_K, CONV_W), F32),
            jax.ShapeDtypeStruct((1, CONV_W), F32),
            jax.ShapeDtypeStruct((1, CONV_W), F32),
            jax.ShapeDtypeStruct((1, CONV_W), F32),
        ],
        scratch_shapes=[
            pltpu.VMEM((tt + hp, POOL_W), F32),
            pltpu.VMEM((tt + hp, POOL_W), F32),
            pltpu.VMEM((SUBLANE, tt + 2 * hp, CONV_W), F32),
            pltpu.VMEM((SUBLANE, tt + hp, CONV_W), F32),
            pltpu.VMEM((tt + hp, CONV_W), F32),
        ],
        name=name,
        compiler_params=_cparams("arbitrary"),
    )(z, z, z, dcat, dcat, pool_w, pool_scale.reshape(1, POOL_W), dw_w, dw_b.reshape(1, CONV_W),
      ln_g.reshape(1, CONV_W), ln_b.reshape(1, CONV_W))


def _xa_fwd(q, kvm, name):
    t = q.shape[0]
    tt = _rows(t, TT)
    scale = XA_HEAD_DIM ** -0.5

    def body(q_ref, kv_ref, o_ref):
        for h in range(XA_HEADS):
            cs = pl.ds(h * XA_HEAD_DIM, XA_HEAD_DIM)
            vs = pl.ds(D_MODEL + h * XA_HEAD_DIM, XA_HEAD_DIM)
            s = lax.dot_general(q_ref[:, cs], kv_ref[:, cs], NT, preferred_element_type=F32) * scale
            p = jnp.exp(s - jnp.max(s, axis=-1, keepdims=True))
            p = p / jnp.sum(p, axis=-1, keepdims=True)
            o_ref[:, cs] = jnp.dot(p.astype(MXU_DT), kv_ref[:, vs], preferred_element_type=F32).astype(o_ref.dtype)

    return pl.pallas_call(
        body,
        grid=(t // tt,),
        in_specs=[pl.BlockSpec((tt, D_MODEL), lambda i: (i, 0)), pl.BlockSpec((MEM_LEN, 2 * D_MODEL), lambda i: (0, 0))],
        out_specs=pl.BlockSpec((tt, D_MODEL), lambda i: (i, 0)),
        out_shape=jax.ShapeDtypeStruct((t, D_MODEL), MXU_DT),
        name=name,
        compiler_params=_cparams("parallel"),
    )(q, kvm)


def _xa_bwd(q, kvm, do, name):
    t = q.shape[0]
    tt = _rows(t, TT)
    scale = XA_HEAD_DIM ** -0.5

    def body(q_ref, kv_ref, do_ref, dq_ref, dkv_ref):
        @pl.when(pl.program_id(0) == 0)
        def _():
            dkv_ref[...] = jnp.zeros_like(dkv_ref)

        for h in range(XA_HEADS):
            cs = pl.ds(h * XA_HEAD_DIM, XA_HEAD_DIM)
            vs = pl.ds(D_MODEL + h * XA_HEAD_DIM, XA_HEAD_DIM)
            qh = q_ref[:, cs]
            kh = kv_ref[:, cs]
            doh = do_ref[:, cs]
            s = lax.dot_general(qh, kh, NT, preferred_element_type=F32) * scale
            p = jnp.exp(s - jnp.max(s, axis=-1, keepdims=True))
            p = p / jnp.sum(p, axis=-1, keepdims=True)
            dp = lax.dot_general(doh, kv_ref[:, vs], NT, preferred_element_type=F32)
            ds = (p * (dp - jnp.sum(p * dp, axis=-1, keepdims=True)) * scale).astype(MXU_DT)
            dq_ref[:, cs] = jnp.dot(ds, kh, preferred_element_type=F32).astype(dq_ref.dtype)
            dkv_ref[:, cs] += lax.dot_general(ds, qh, TN_DIMS, preferred_element_type=F32)
            dkv_ref[:, vs] += lax.dot_general(p.astype(MXU_DT), doh, TN_DIMS, preferred_element_type=F32)

    row = pl.BlockSpec((tt, D_MODEL), lambda i: (i, 0))
    kvs = pl.BlockSpec((MEM_LEN, 2 * D_MODEL), lambda i: (0, 0))
    return pl.pallas_call(
        body,
        grid=(t // tt,),
        in_specs=[row, kvs, row],
        out_specs=[row, kvs],
        out_shape=[jax.ShapeDtypeStruct((t, D_MODEL), MXU_DT), jax.ShapeDtypeStruct((MEM_LEN, 2 * D_MODEL), F32)],
        name=name,
        compiler_params=_cparams("arbitrary"),
    )(q, kvm, do)


def _rope_tables(positions, name):
    t = positions.shape[0]
    tt = _rows(t, TT)
    inv = 1.0 / (ROPE_THETA ** (np.arange(0, QK_ROPE, 2, dtype=np.float32) / QK_ROPE))
    lanes = np.zeros((1, HEAD_PAD), np.float32)
    half = QK_ROPE // 2
    lanes[0, KPE_LANE : KPE_LANE + half] = inv
    lanes[0, KPE_LANE + half : KPE_LANE + QK_ROPE] = inv

    def body(pos_ref, inv_ref, cos_ref, sa_ref, sb_ref):
        ang = pos_ref[...].astype(F32) * inv_ref[...]
        lane = lax.broadcasted_iota(jnp.int32, (tt, HEAD_PAD), 1)
        c = jnp.cos(ang)
        s = jnp.sin(ang)
        lo = (lane >= KPE_LANE) & (lane < KPE_LANE + half)
        hi = (lane >= KPE_LANE + half) & (lane < KPE_LANE + QK_ROPE)
        cos_ref[...] = jnp.where(lo | hi, c, 1.0)
        sa_ref[...] = jnp.where(hi, s, 0.0)
        sb_ref[...] = jnp.where(lo, -s, 0.0)

    tab = pl.BlockSpec((tt, HEAD_PAD), lambda i: (i, 0))
    return pl.pallas_call(
        body,
        grid=(t // tt,),
        in_specs=[pl.BlockSpec((tt, 1), lambda i: (i, 0)), pl.BlockSpec((1, HEAD_PAD), lambda i: (0, 0))],
        out_specs=[tab, tab, tab],
        out_shape=[jax.ShapeDtypeStruct((t, HEAD_PAD), F32)] * 3,
        name=name,
        compiler_params=_cparams("parallel"),
    )(positions, jnp.asarray(lanes))


def _rotate(x, cos, sa, sb, sign):
    half = QK_ROPE // 2
    return x * cos + sign * (pltpu.roll(x, half, 1) * sa + pltpu.roll(x, HEAD_PAD - half, 1) * sb)


def _rope_heads(x, tables, sign, scale, name):
    t, w = x.shape
    tt = _rows(t, TT)
    nh = w // HEAD_PAD

    def body(x_ref, c_ref, sa_ref, sb_ref, o_ref):
        cos, sa, sb = c_ref[...] * scale, sa_ref[...] * scale, sb_ref[...] * scale
        for h in range(nh):
            cs = pl.ds(h * HEAD_PAD, HEAD_PAD)
            o_ref[:, cs] = _rotate(x_ref[:, cs], cos, sa, sb, sign).astype(o_ref.dtype)

    tab = pl.BlockSpec((tt, HEAD_PAD), lambda i: (i, 0))
    row = pl.BlockSpec((tt, w), lambda i: (i, 0))
    return pl.pallas_call(
        body,
        grid=(t // tt,),
        in_specs=[row, tab, tab, tab],
        out_specs=row,
        out_shape=jax.ShapeDtypeStruct((t, w), MXU_DT),
        name=name,
        compiler_params=_cparams("parallel"),
    )(x, *tables)


def _matmul_rope(a, b, layer, tables, scale, name):
    m, k = a.shape
    n = b.shape[-1]
    tm = _rows(m, TT)
    nh = n // HEAD_PAD

    def body(a_ref, b_ref, c_ref, sa_ref, sb_ref, o_ref):
        q = jnp.dot(a_ref[...].astype(MXU_DT), b_ref[...].astype(MXU_DT), preferred_element_type=F32)
        cos, sa, sb = c_ref[...] * scale, sa_ref[...] * scale, sb_ref[...] * scale
        for h in range(nh):
            cs = slice(h * HEAD_PAD, (h + 1) * HEAD_PAD)
            o_ref[:, cs] = _rotate(q[:, cs], cos, sa, sb, 1.0).astype(o_ref.dtype)

    tab = pl.BlockSpec((tm, HEAD_PAD), lambda i: (i, 0))
    return pl.pallas_call(
        body,
        grid=(m // tm,),
        in_specs=[pl.BlockSpec((tm, k), lambda i: (i, 0)), pl.BlockSpec((None, k, n), lambda i: (layer, 0, 0)), tab, tab, tab],
        out_specs=pl.BlockSpec((tm, n), lambda i: (i, 0)),
        out_shape=jax.ShapeDtypeStruct((m, n), MXU_DT),
        name=name,
        compiler_params=_cparams("parallel"),
    )(a, b, *tables)


def _mla_prep(cp, qg, kvg, tables, name):
    t = cp.shape[0]
    tt = _rows(t, TT)

    def body(cp_ref, qg_ref, kvg_ref, c_ref, sa_ref, sb_ref, qn_ref, kvn_ref, kpe_ref):
        cq = cp_ref[:, 0:Q_LORA]
        r = lax.rsqrt(jnp.mean(cq * cq, axis=-1, keepdims=True) + EPS)
        qn_ref[...] = ((cq * r) * qg_ref[...]).astype(qn_ref.dtype)
        ckv = cp_ref[:, Q_LORA : Q_LORA + KV_LORA]
        r = lax.rsqrt(jnp.mean(ckv * ckv, axis=-1, keepdims=True) + EPS)
        kvn_ref[...] = ((ckv * r) * kvg_ref[...]).astype(kvn_ref.dtype)
        kpe = cp_ref[:, Q_LORA + KV_LORA : C_PAD]
        kpe_ref[...] = _rotate(kpe, c_ref[...], sa_ref[...], sb_ref[...], 1.0).astype(kpe_ref.dtype)

    tab = pl.BlockSpec((tt, HEAD_PAD), lambda i: (i, 0))
    return pl.pallas_call(
        body,
        grid=(t // tt,),
        in_specs=[
            pl.BlockSpec((tt, C_PAD), lambda i: (i, 0)),
            pl.BlockSpec((1, Q_LORA), lambda i: (0, 0)),
            pl.BlockSpec((1, KV_LORA), lambda i: (0, 0)),
            tab, tab, tab,
        ],
        out_specs=[
            pl.BlockSpec((tt, Q_LORA), lambda i: (i, 0)),
            pl.BlockSpec((tt, KV_LORA), lambda i: (i, 0)),
            tab,
        ],
        out_shape=[
            jax.ShapeDtypeStruct((t, Q_LORA), MXU_DT),
            jax.ShapeDtypeStruct((t, KV_LORA), MXU_DT),
            jax.ShapeDtypeStruct((t, HEAD_PAD), MXU_DT),
        ],
        name=name,
        compiler_params=_cparams("parallel"),
    )(cp, qg.reshape(1, Q_LORA), kvg.reshape(1, KV_LORA), *tables)


def _mla_prep_bwd(cp, dqn, dkvn, dkpe_heads, qg, kvg, tables, name):
    t = cp.shape[0]
    tt = _rows(t, TT)

    def norm_bwd(x, dy, g):
        r = lax.rsqrt(jnp.mean(x * x, axis=-1, keepdims=True) + EPS)
        xh = x * r
        gy = dy * g
        return r * (gy - xh * jnp.mean(gy * xh, axis=-1, keepdims=True)), jnp.sum(dy * xh, axis=0, keepdims=True)

    def body(cp_ref, dqn_ref, dkvn_ref, dkpe_ref, qg_ref, kvg_ref, c_ref, sa_ref, sb_ref, dcp_ref, dqg_ref, dkvg_ref):
        @pl.when(pl.program_id(0) == 0)
        def _():
            dqg_ref[...] = jnp.zeros_like(dqg_ref)
            dkvg_ref[...] = jnp.zeros_like(dkvg_ref)

        dcq, dg = norm_bwd(cp_ref[:, 0:Q_LORA], dqn_ref[...], qg_ref[...])
        dcp_ref[:, 0:Q_LORA] = dcq.astype(dcp_ref.dtype)
        dqg_ref[...] += dg
        dckv, dg = norm_bwd(cp_ref[:, Q_LORA : Q_LORA + KV_LORA], dkvn_ref[...], kvg_ref[...])
        dcp_ref[:, Q_LORA : Q_LORA + KV_LORA] = dckv.astype(dcp_ref.dtype)
        dkvg_ref[...] += dg
        dk = dkpe_ref[0]
        for h in range(1, MLA_HEADS):
            dk = dk + dkpe_ref[h]
        dcp_ref[:, Q_LORA + KV_LORA : C_PAD] = _rotate(dk, c_ref[...], sa_ref[...], sb_ref[...], -1.0).astype(dcp_ref.dtype)

    tab = pl.BlockSpec((tt, HEAD_PAD), lambda i: (i, 0))
    return pl.pallas_call(
        body,
        grid=(t // tt,),
        in_specs=[
            pl.BlockSpec((tt, C_PAD), lambda i: (i, 0)),
            pl.BlockSpec((tt, Q_LORA), lambda i: (i, 0)),
            pl.BlockSpec((tt, KV_LORA), lambda i: (i, 0)),
            pl.BlockSpec((MLA_HEADS, tt, HEAD_PAD), lambda i: (0, i, 0)),
            pl.BlockSpec((1, Q_LORA), lambda i: (0, 0)),
            pl.BlockSpec((1, KV_LORA), lambda i: (0, 0)),
            tab, tab, tab,
        ],
        out_specs=[
            pl.BlockSpec((tt, C_PAD), lambda i: (i, 0)),
            pl.BlockSpec((1, Q_LORA), lambda i: (0, 0)),
            pl.BlockSpec((1, KV_LORA), lambda i: (0, 0)),
        ],
        out_shape=[
            jax.ShapeDtypeStruct((t, C_PAD), MXU_DT),
            jax.ShapeDtypeStruct((1, Q_LORA), F32),
            jax.ShapeDtypeStruct((1, KV_LORA), F32),
        ],
        name=name,
        compiler_params=_cparams("arbitrary"),
    )(cp, dqn, dkvn, dkpe_heads, qg.reshape(1, Q_LORA), kvg.reshape(1, KV_LORA), *tables)


def _flash_fwd(qs, kv, kpe, name):
    t = qs.shape[0]
    ta = _rows(t, TA)
    tq = ta
    nq = t // tq
    sub = ta // 2

    def body(q_ref, kv_ref, kpe_ref, o_ref, lse_ref):
        qi = pl.program_id(1)
        q = q_ref[...]
        lane = lax.broadcasted_iota(jnp.int32, (ta, HEAD_PAD), 1)

        def kblock(j):
            rows = pl.ds(pl.multiple_of(j * ta, ta), ta)
            kvb = kv_ref[rows, :]
            ones_v = jnp.where(lane < QK_NOPE, jnp.ones_like(kvb), kvb)
            return ones_v, jnp.where(lane < QK_NOPE, kvb, kpe_ref[rows, :])

        def update(carry, s, ones_v):
            m, acc = carry
            m_new = jnp.maximum(m, jnp.max(s, axis=-1, keepdims=True))
            p = jnp.exp2(s - m_new).astype(MXU_DT)
            acc = jnp.exp2(m - m_new) * acc + jnp.dot(p, ones_v, preferred_element_type=F32)
            return m_new, acc

        def step(j, carry):
            ones_v, k = kblock(j)
            return update(carry, lax.dot_general(q, k, NT, preferred_element_type=F32), ones_v)

        init = (jnp.full((tq, 1), -jnp.inf, F32), jnp.zeros((tq, HEAD_PAD), F32))
        m_all, acc_all = lax.fori_loop(0, qi, step, init)
        ones_v, k = kblock(qi)
        lane_h = lax.broadcasted_iota(jnp.int32, (sub, HEAD_PAD), 1)
        for b in range(2):
            rows, nk = slice(b * sub, (b + 1) * sub), (b + 1) * sub
            s = lax.dot_general(q[rows], k[0:nk], NT, preferred_element_type=F32)
            r = lax.broadcasted_iota(jnp.int32, (sub, nk), 0) + b * sub
            c = lax.broadcasted_iota(jnp.int32, (sub, nk), 1)
            m, acc = update((m_all[rows], acc_all[rows]), jnp.where(c <= r, s, NEG), ones_v[0:nk])
            l = acc[:, 0:1]
            o_ref[rows, :] = jnp.where(lane_h >= QK_NOPE, acc / l, 0.0).astype(o_ref.dtype)
            lse_ref[rows, :] = m + jnp.log2(l)

    return pl.pallas_call(
        body,
        grid=(MLA_HEADS, nq),
        in_specs=[
            pl.BlockSpec((tq, HEAD_PAD), lambda h, i: (i, h)),
            pl.BlockSpec((t, HEAD_PAD), lambda h, i: (0, h)),
            pl.BlockSpec((t, HEAD_PAD), lambda h, i: (0, 0)),
        ],
        out_specs=[
            pl.BlockSpec((tq, HEAD_PAD), lambda h, i: (i, h)),
            pl.BlockSpec((None, tq, 1), lambda h, i: (h, i, 0)),
        ],
        out_shape=[
            jax.ShapeDtypeStruct((t, MLA_HEADS * HEAD_PAD), MXU_DT),
            jax.ShapeDtypeStruct((MLA_HEADS, t, 1), F32),
        ],
        name=name,
        compiler_params=_cparams("parallel", "parallel"),
    )(qs, kv, kpe)


def _flash_bwd(qs, kv, kpe, o, do, lse, name):
    t = qs.shape[0]
    ta = _rows(t, TA)
    tq = ta
    nq = t // ta
    sub = ta // 2

    def body(q_ref, o_ref, do_ref, lse_ref, kv_ref, kpe_ref, dq_ref, dkv_ref, dkpe_ref, dk_acc, dv_acc):
        kj = pl.program_id(1)

        @pl.when(kj == 0)
        def _():
            dq_ref[...] = jnp.zeros_like(dq_ref)

        lane = lax.broadcasted_iota(jnp.int32, (ta, HEAD_PAD), 1)
        kvb = kv_ref[...]
        k = jnp.where(lane < QK_NOPE, kvb, kpe_ref[...])
        dk_acc[...] = jnp.zeros_like(dk_acc)
        dv_acc[...] = jnp.zeros_like(dv_acc)

        def tile(row0, nrows, nkeys, diagonal):
            rows = pl.ds(pl.multiple_of(row0, sub), nrows)
            keys = slice(0, nkeys)
            q = q_ref[rows, :]
            dob = do_ref[rows, :]
            delta = jnp.sum(dob.astype(F32) * o_ref[rows, :].astype(F32), axis=-1, keepdims=True)
            s = lax.dot_general(q, k[keys], NT, preferred_element_type=F32)
            if diagonal:
                r = lax.broadcasted_iota(jnp.int32, (nrows, nkeys), 0) + (nkeys - nrows)
                c = lax.broadcasted_iota(jnp.int32, (nrows, nkeys), 1)
                s = jnp.where(c <= r, s, NEG)
            p = jnp.exp2(s - lse_ref[rows, :])
            dp = lax.dot_general(dob, kvb[keys], NT, preferred_element_type=F32)
            ds = (p * (dp - delta)).astype(MXU_DT)
            dq_ref[rows, :] += jnp.dot(ds, k[keys], preferred_element_type=F32)
            dk_acc[keys, :] += lax.dot_general(ds, q, TN_DIMS, preferred_element_type=F32)
            dv_acc[keys, :] += lax.dot_general(p.astype(MXU_DT), dob, TN_DIMS, preferred_element_type=F32)

        tile(kj * ta, sub, sub, True)
        tile(kj * ta + sub, sub, ta, True)

        def step(qq, carry):
            tile(qq * tq, tq, ta, False)
            return carry

        lax.fori_loop(kj + 1, t // tq, step, 0)
        dk = dk_acc[...] * (1.0 / LOG2E)
        dkv_ref[...] = jnp.where(lane < QK_NOPE, dk, dv_acc[...]).astype(dkv_ref.dtype)
        dkpe_ref[...] = jnp.where((lane >= KPE_LANE) & (lane < KPE_LANE + QK_ROPE), dk, 0.0)

    head_rows = pl.BlockSpec((t, HEAD_PAD), lambda h, j: (0, h))
    return pl.pallas_call(
        body,
        grid=(MLA_HEADS, nq),
        in_specs=[
            head_rows,
            head_rows,
            head_rows,
            pl.BlockSpec((None, t, 1), lambda h, j: (h, 0, 0)),
            pl.BlockSpec((ta, HEAD_PAD), lambda h, j: (j, h)),
            pl.BlockSpec((ta, HEAD_PAD), lambda h, j: (j, 0)),
        ],
        out_specs=[
            head_rows,
            pl.BlockSpec((ta, HEAD_PAD), lambda h, j: (j, h)),
            pl.BlockSpec((None, ta, HEAD_PAD), lambda h, j: (h, j, 0)),
        ],
        out_shape=[
            jax.ShapeDtypeStruct((t, MLA_HEADS * HEAD_PAD), F32),
            jax.ShapeDtypeStruct((t, MLA_HEADS * HEAD_PAD), MXU_DT),
            jax.ShapeDtypeStruct((MLA_HEADS, t, HEAD_PAD), F32),
        ],
        scratch_shapes=[pltpu.VMEM((ta, HEAD_PAD), F32), pltpu.VMEM((ta, HEAD_PAD), F32)],
        name=name,
        compiler_params=_cparams("parallel", "arbitrary"),
    )(qs, o, do, lse, kv, kpe)


def _as2d(a):
    if a.ndim == 1:
        return a.reshape(1, a.shape[0])
    return a.reshape(-1, a.shape[-1])


def _adamw(w, g, m, v, name):
    shape = w.shape
    w2, g2, m2, v2 = (_as2d(a) for a in (w, g, m, v))
    r, c = w2.shape
    tr = _tile_rows(r, c)
    c1 = 1.0 - ADAM_B1 ** ADAM_STEP
    c2 = 1.0 - ADAM_B2 ** ADAM_STEP

    def body(w_ref, g_ref, m_ref, v_ref, d_ref, nm_ref, nv_ref):
        gv = g_ref[...]
        nm = ADAM_B1 * m_ref[...] + (1.0 - ADAM_B1) * gv
        nv = ADAM_B2 * v_ref[...] + (1.0 - ADAM_B2) * (gv * gv)
        d_ref[...] = -ADAM_LR * ((nm / c1) / (jnp.sqrt(nv / c2) + ADAM_EPS) + ADAM_WD * w_ref[...])
        nm_ref[...] = nm
        nv_ref[...] = nv

    blk = pl.BlockSpec((tr, c), lambda i: (i, 0))
    outs = pl.pallas_call(
        body,
        grid=(r // tr,),
        in_specs=[blk] * 4,
        out_specs=[blk] * 3,
        out_shape=[jax.ShapeDtypeStruct((r, c), F32)] * 3,
        name=name,
        compiler_params=_cparams("parallel"),
    )(w2, g2, m2, v2)
    return tuple(o.reshape(shape) for o in outs)


def _adamw_halves(w, mine, other, m, v, c_idx, name):
    nl, r, c = w.shape
    h = nl // 2
    tr = _tile_rows(r, 2 * c)
    c1 = 1.0 - ADAM_B1 ** ADAM_STEP
    c2 = 1.0 - ADAM_B2 ** ADAM_STEP

    def body(c_ref, w_ref, a_ref, b_ref, m_ref, v_ref, g_ref, d_ref, nm_ref, nv_ref):
        l = pl.program_id(0)
        gv = jnp.where(l // h == c_ref[0], a_ref[...], b_ref[...])
        nm = ADAM_B1 * m_ref[...] + (1.0 - ADAM_B1) * gv
        nv = ADAM_B2 * v_ref[...] + (1.0 - ADAM_B2) * (gv * gv)
        g_ref[...] = gv
        d_ref[...] = -ADAM_LR * ((nm / c1) / (jnp.sqrt(nv / c2) + ADAM_EPS) + ADAM_WD * w_ref[...])
        nm_ref[...] = nm
        nv_ref[...] = nv

    def half_map(mine_side):
        def index(l, i, cr):
            first = cr[0] * h if mine_side else (1 - cr[0]) * h
            return (jnp.clip(l - first, 0, h - 1), i, 0)
        return index

    full = pl.BlockSpec((None, tr, c), lambda l, i, cr: (l, i, 0))
    grid_spec = pltpu.PrefetchScalarGridSpec(
        num_scalar_prefetch=1,
        grid=(nl, r // tr),
        in_specs=[full, pl.BlockSpec((None, tr, c), half_map(True)), pl.BlockSpec((None, tr, c), half_map(False)), full, full],
        out_specs=[full] * 4,
    )
    return pl.pallas_call(
        body,
        grid_spec=grid_spec,
        out_shape=[jax.ShapeDtypeStruct((nl, r, c), F32)] * 4,
        name=name,
        compiler_params=_cparams("parallel", "parallel"),
    )(c_idx, w, mine, other, m, v)


def _tile_rows(r, c, mult=SUBLANE):
    limit = max(mult, (BLOCK_BYTES // 4) // (4 * c))
    if r <= limit:
        return r
    t = (limit // mult) * mult
    while t >= mult:
        if r % t == 0:
            return t
        t -= mult
    return r


def _sum_leading(a, name):
    n, r, c = a.shape
    tr = _tile_rows(r, c * n)

    def body(a_ref, o_ref):
        s = a_ref[0]
        for k in range(1, n):
            s = s + a_ref[k]
        o_ref[...] = s

    return pl.pallas_call(
        body,
        grid=(r // tr,),
        in_specs=[pl.BlockSpec((n, tr, c), lambda i: (0, i, 0))],
        out_specs=pl.BlockSpec((tr, c), lambda i: (i, 0)),
        out_shape=jax.ShapeDtypeStruct((r, c), F32),
        name=name,
        compiler_params=_cparams("parallel"),
    )(a)


def _add_half(g, s, c_idx, name):
    nl, r, c = g.shape
    h = nl // 2
    tr = _tile_rows(r, 2 * c, 2 * SUBLANE)

    def body(c_ref, g_ref, s_ref, o_ref):
        o_ref[...] = (g_ref[...] + s_ref[...]).astype(o_ref.dtype)

    grid_spec = pltpu.PrefetchScalarGridSpec(
        num_scalar_prefetch=1,
        grid=(h, r // tr),
        in_specs=[
            pl.BlockSpec((None, tr, c), lambda l, i, cr: (cr[0] * h + l, i, 0)),
            pl.BlockSpec((None, tr, c), lambda l, i, cr: (l, i, 0)),
        ],
        out_specs=pl.BlockSpec((None, tr, c), lambda l, i, cr: (l, i, 0)),
    )
    return pl.pallas_call(
        body,
        grid_spec=grid_spec,
        out_shape=jax.ShapeDtypeStruct((h, r, c), XFER_DT),
        name=name,
        compiler_params=_cparams("parallel", "parallel"),
    )(c_idx, g, s)


def _sum_chips(slots, pair, chip_idx, kind, name):
    _, h, r, c = slots.shape
    tr = _tile_rows(r, 5 * c, 2 * SUBLANE)
    nr = r // tr

    def body(chip_ref, s_ref, own_ref, o_ref):
        chip = chip_ref[0]
        own = own_ref[...].astype(F32)
        parts = [s_ref[j].astype(F32) for j in range(3)]
        total = None
        for k in range(4):
            d = jnp.bitwise_xor(chip, k)
            v = jnp.where(d == 0, own, jnp.where(d == 2, parts[0], jnp.where(d == 1, parts[1], parts[2])))
            total = v if total is None else total + v
        o_ref[...] = total

    if kind == "row":
        own_spec = pl.BlockSpec((None, tr, c), lambda l, i, cr: (l, cr[0] * nr + i, 0))
    else:
        own_spec = pl.BlockSpec((None, tr, c), lambda l, i, cr: (l, i, cr[0]))
    grid_spec = pltpu.PrefetchScalarGridSpec(
        num_scalar_prefetch=1,
        grid=(h, nr),
        in_specs=[pl.BlockSpec((3, None, tr, c), lambda l, i, cr: (0, l, i, 0)), own_spec],
        out_specs=pl.BlockSpec((None, tr, c), lambda l, i, cr: (l, i, 0)),
    )
    return pl.pallas_call(
        body,
        grid_spec=grid_spec,
        out_shape=jax.ShapeDtypeStruct((h, r, c), F32),
        name=name,
        compiler_params=_cparams("parallel", "parallel"),
    )(chip_idx, slots, pair)


def _mesh_pos():
    return lax.axis_index("x"), lax.axis_index("y"), lax.axis_index("c")


def _other_chips(x, y):
    return [(1 - x, y), (x, 1 - y), (1 - x, 1 - y)]


def _all_gather_rows(block, name):
    m_per, n = block.shape

    def body(x_ref, out_ref, send_sems, recv_sems, local_sem):
        x, y, c = _mesh_pos()
        me, sibling = (x, y, c), (x, y, 1 - c)
        chips = _other_chips(x, y)

        def rows(px, py, pc):
            return out_ref.at[pl.ds((4 * px + 2 * py + pc) * m_per, m_per), :]

        def copy(k, blk, to, src=None):
            return pltpu.make_async_remote_copy(
                src_ref=rows(*blk) if src is None else src,
                dst_ref=rows(*blk),
                send_sem=send_sems.at[k],
                recv_sem=recv_sems.at[k],
                device_id=to,
                device_id_type=MESH_ID,
            )

        mine = pltpu.make_async_copy(x_ref, rows(*me), local_sem)
        mine.start()
        first = [copy(0, me, sibling, src=x_ref)]
        first += [copy(1 + j, me, (*chip, c), src=x_ref) for j, chip in enumerate(chips)]
        for cp in first:
            cp.start()
        passed = [copy(4 + j, (*chip, c), sibling) for j, chip in enumerate(chips)]
        for j, chip in enumerate(chips):
            copy(1 + j, (*chip, c), me).wait_recv()
            passed[j].start()
        copy(0, sibling, me).wait_recv()
        for j, chip in enumerate(chips):
            copy(4 + j, (*chip, 1 - c), me).wait_recv()
        for cp in first + passed:
            cp.wait_send()
        mine.wait()

    return pl.pallas_call(
        body,
        out_shape=jax.ShapeDtypeStruct((8 * m_per, n), block.dtype),
        in_specs=[pl.BlockSpec(memory_space=pltpu.VMEM)],
        out_specs=pl.BlockSpec(memory_space=pltpu.VMEM),
        scratch_shapes=[pltpu.SemaphoreType.DMA((7,)), pltpu.SemaphoreType.DMA((7,)), pltpu.SemaphoreType.DMA],
        name=name,
        compiler_params=pltpu.CompilerParams(vmem_limit_bytes=VMEM_LIMIT),
    )(block)


def _shard_window(ref, layers, chip, rows, cols):
    if rows is not None:
        return ref.at[layers, pl.ds(pl.multiple_of(chip * rows, rows), rows), :]
    return ref.at[layers, :, pl.ds(pl.multiple_of(chip * cols, cols), cols)]


def _all_gather_weights(shards, kinds, name):
    nw = len(shards)
    out_shapes = []
    for s, kind in zip(shards, kinds):
        nl, r, c = s.shape
        full = (nl, 4 * r, c) if kind == "row" else (nl, r, 4 * c)
        out_shapes.append(jax.ShapeDtypeStruct(full, s.dtype))

    def body(*refs):
        ins, outs = refs[:nw], refs[nw : 2 * nw]
        send_sems, recv_sems, in_sems, out_sems = refs[2 * nw : 2 * nw + 4]
        bufs = refs[2 * nw + 4 :]
        x, y, c = _mesh_pos()
        sibling = (x, y, 1 - c)
        chips = _other_chips(x, y)
        my_chip = 2 * x + y

        def window(w, chip, layers):
            _, r, cc = shards[w].shape
            if kinds[w] == "row":
                return _shard_window(outs[w], layers, chip, r, None)
            return _shard_window(outs[w], layers, chip, None, cc)

        def half(w, half_idx):
            h = shards[w].shape[0] // 2
            return pl.ds(half_idx * h, h)

        def copy(w, k, src, dst, to):
            return pltpu.make_async_remote_copy(
                src_ref=src, dst_ref=dst, send_sem=send_sems.at[w, k], recv_sem=recv_sems.at[w, k],
                device_id=to, device_id_type=MESH_ID)

        sent = []
        for w in range(nw):
            mine = ins[w].at[half(w, c)]
            for j, chip in enumerate(chips):
                cp = copy(w, j, mine, window(w, my_chip, half(w, c)), (*chip, c))
                cp.start()
                sent.append(cp)
        for w in range(nw):
            nl = shards[w].shape[0]

            def load(l, w=w):
                return pltpu.make_async_copy(ins[w].at[l], bufs[w].at[l % 2], in_sems.at[w, l % 2])

            def store(l, w=w):
                return pltpu.make_async_copy(bufs[w].at[l % 2], window(w, my_chip, l), out_sems.at[w, l % 2])

            load(0).start()
            for l in range(nl):
                load(l).wait()
                store(l).start()
                if l + 1 < nl:
                    if l >= 1:
                        store(l - 1).wait()
                    load(l + 1).start()
            for l in range(max(nl - 2, 0), nl):
                store(l).wait()
        for w in range(nw):
            for j, (cx, cy) in enumerate(chips):
                got = window(w, 2 * cx + cy, half(w, c))
                copy(w, j, got, got, (cx, cy, c)).wait_recv()
                cp = copy(w, 3 + j, got, got, sibling)
                cp.start()
                sent.append(cp)
        for w in range(nw):
            for j, (cx, cy) in enumerate(chips):
                got = window(w, 2 * cx + cy, half(w, 1 - c))
                copy(w, 3 + j, got, got, sibling).wait_recv()
        for cp in sent:
            cp.wait_send()

    anyspec = pl.BlockSpec(memory_space=pl.ANY)
    return pl.pallas_call(
        body,
        out_shape=out_shapes,
        in_specs=[anyspec] * nw,
        out_specs=[anyspec] * nw,
        scratch_shapes=[pltpu.SemaphoreType.DMA((nw, 6)), pltpu.SemaphoreType.DMA((nw, 6)),
                        pltpu.SemaphoreType.DMA((nw, 2)), pltpu.SemaphoreType.DMA((nw, 2))]
        + [pltpu.VMEM((2,) + s.shape[1:], s.dtype) for s in shards],
        name=name,
        compiler_params=pltpu.CompilerParams(vmem_limit_bytes=VMEM_LIMIT),
    )(*shards)


def _exchange_halves(grads, name):
    nw = len(grads)
    out_shapes = [jax.ShapeDtypeStruct((g.shape[0] // 2,) + g.shape[1:], g.dtype) for g in grads]

    def body(*refs):
        ins, outs = refs[:nw], refs[nw : 2 * nw]
        send_sems, recv_sems = refs[2 * nw :]
        x, y, c = _mesh_pos()
        cps = []
        for w in range(nw):
            h = grads[w].shape[0] // 2
            cp = pltpu.make_async_remote_copy(
                src_ref=ins[w].at[pl.ds((1 - c) * h, h)], dst_ref=outs[w], send_sem=send_sems.at[w],
                recv_sem=recv_sems.at[w], device_id=(x, y, 1 - c), device_id_type=MESH_ID)
            cp.start()
            cps.append(cp)
        for cp in cps:
            cp.wait()

    anyspec = pl.BlockSpec(memory_space=pl.ANY)
    return pl.pallas_call(
        body,
        out_shape=out_shapes,
        in_specs=[anyspec] * nw,
        out_specs=[anyspec] * nw,
        scratch_shapes=[pltpu.SemaphoreType.DMA((nw,)), pltpu.SemaphoreType.DMA((nw,))],
        name=name,
    )(*grads)


def _scatter_to_chips(parts, kinds, name):
    nw = len(parts)
    shard_shapes = []
    for p, kind in zip(parts, kinds):
        h, r, c = p.shape
        shard_shapes.append((h, r // 4, c) if kind == "row" else (h, r, c // 4))
    out_shapes = [jax.ShapeDtypeStruct((3,) + s, p.dtype) for s, p in zip(shard_shapes, parts)]

    def body(*refs):
        ins, outs = refs[:nw], refs[nw : 2 * nw]
        send_sems, recv_sems = refs[2 * nw :]
        x, y, c = _mesh_pos()
        chips = _other_chips(x, y)

        def piece(w, chip):
            h, r, cc = shard_shapes[w]
            if kinds[w] == "row":
                return _shard_window(ins[w], pl.ds(0, h), chip, r, None)
            return _shard_window(ins[w], pl.ds(0, h), chip, None, cc)

        def copy(w, j, cx, cy):
            return pltpu.make_async_remote_copy(
                src_ref=piece(w, 2 * cx + cy), dst_ref=outs[w].at[j], send_sem=send_sems.at[w, j],
                recv_sem=recv_sems.at[w, j], device_id=(cx, cy, c), device_id_type=MESH_ID)

        cps = [copy(w, j, cx, cy) for w in range(nw) for j, (cx, cy) in enumerate(chips)]
        for cp in cps:
            cp.start()
        for cp in cps:
            cp.wait()

    anyspec = pl.BlockSpec(memory_space=pl.ANY)
    return pl.pallas_call(
        body,
        out_shape=out_shapes,
        in_specs=[anyspec] * nw,
        out_specs=[anyspec] * nw,
        scratch_shapes=[pltpu.SemaphoreType.DMA((nw, 3)), pltpu.SemaphoreType.DMA((nw, 3))],
        name=name,
    )(*parts)


def _swap_halves(halves, name):
    nw = len(halves)
    out_shapes = [jax.ShapeDtypeStruct(p.shape, p.dtype) for p in halves]

    def body(*refs):
        ins, outs = refs[:nw], refs[nw : 2 * nw]
        send_sems, recv_sems = refs[2 * nw :]
        x, y, c = _mesh_pos()
        cps = [pltpu.make_async_remote_copy(
            src_ref=ins[w], dst_ref=outs[w], send_sem=send_sems.at[w], recv_sem=recv_sems.at[w],
            device_id=(x, y, 1 - c), device_id_type=MESH_ID) for w in range(nw)]
        for cp in cps:
            cp.start()
        for cp in cps:
            cp.wait()

    anyspec = pl.BlockSpec(memory_space=pl.ANY)
    return pl.pallas_call(
        body,
        out_shape=out_shapes,
        in_specs=[anyspec] * nw,
        out_specs=[anyspec] * nw,
        scratch_shapes=[pltpu.SemaphoreType.DMA((nw,)), pltpu.SemaphoreType.DMA((nw,))],
        name=name,
    )(*halves)


def _pad_wdq(w):
    z = lambda n: jnp.zeros((w.shape[0], n), w.dtype)
    base = Q_LORA + KV_LORA
    return jnp.concatenate([w[:, :base], z(KPE_LANE), w[:, base:], z(HEAD_PAD - KPE_LANE - QK_ROPE)], axis=1)


def _unpad_wdq(g):
    base = Q_LORA + KV_LORA
    return jnp.concatenate([g[:, :base], g[:, base + KPE_LANE : base + KPE_LANE + QK_ROPE]], axis=1)


def _pad_wuq(w):
    w3 = w.reshape(Q_LORA, MLA_HEADS, QK_NOPE + QK_ROPE)
    w3 = jnp.pad(w3, ((0, 0), (0, 0), (0, HEAD_PAD - QK_NOPE - QK_ROPE)))
    return w3.reshape(Q_LORA, MLA_HEADS * HEAD_PAD)


def _unpad_wuq(g):
    g3 = g.reshape(Q_LORA, MLA_HEADS, HEAD_PAD)[:, :, : QK_NOPE + QK_ROPE]
    return g3.reshape(Q_LORA, MLA_HEADS * (QK_NOPE + QK_ROPE))


def _pad_wo(w):
    w3 = w.reshape(MLA_HEADS, V_HEAD, D_MODEL)
    w3 = jnp.pad(w3, ((0, 0), (HEAD_PAD - V_HEAD, 0), (0, 0)))
    return w3.reshape(MLA_HEADS * HEAD_PAD, D_MODEL)


def _unpad_wo(g):
    g3 = g.reshape(MLA_HEADS, HEAD_PAD, D_MODEL)[:, HEAD_PAD - V_HEAD :, :]
    return g3.reshape(MLA_HEADS * V_HEAD, D_MODEL)


def _local_step(x, mem, positions, target, wb, ws):
    t = x.shape[0]
    tables = _rope_tables(positions.reshape(t, 1), "rope_tables")
    saved = []
    h1 = _rms_fwd(x, ws["norm_mix_g"][0], "l0_norm_mix")
    for l in range(DEPTH):
        s = {"x0": x}
        s["h1"] = h1
        if l % 2 == 0:
            e = l // 2
            z = _matmul(h1, wb["pc_w_in"], "nn", F32, f"l{l}_pc_in", layer=e)
            cat = _mix_fwd(z, ws["pool_w"][e], ws["pool_scale"][e], ws["conv_dw_w"][e], ws["conv_dw_b"][e],
                           ws["conv_ln_g"][e], ws["conv_ln_b"][e], f"l{l}_mix")
            x, h2 = _matmul_res_norm(cat, wb["pc_w_out"], e, x, ws["norm_xa_g"][l], f"l{l}_pc_out")
            s.update(z=z, cat=cat)
        else:
            o = l // 2
            cp = _matmul(h1, wb["mla_wdq"], "nn", F32, f"l{l}_mla_dq", layer=o)
            qn, kvn, kpe = _mla_prep(cp, ws["mla_q_norm_g"][o], ws["mla_kv_norm_g"][o], tables, f"l{l}_mla_prep")
            qr = _matmul_rope(qn, wb["mla_wuq"], o, tables, MLA_SCALE * LOG2E, f"l{l}_mla_uq")
            kv = _matmul(kvn, wb["mla_w_ukv"], "nn", MXU_DT, f"l{l}_mla_ukv", layer=o)
            att, lse = _flash_fwd(qr, kv, kpe, f"l{l}_mla_attn")
            x, h2 = _matmul_res_norm(att, wb["mla_wo"], o, x, ws["norm_xa_g"][l], f"l{l}_mla_o")
            s.update(cp=cp, qn=qn, kvn=kvn, kpe=kpe, qr=qr, kv=kv, att=att, lse=lse)
        s["x1"] = x
        hm = _rms_fwd(mem, ws["norm_mem_g"][l], f"l{l}_norm_mem")
        q2 = _matmul(h2, wb["xa_wq"], "nn", MXU_DT, f"l{l}_xa_q", layer=l)
        kvm = _matmul(hm, wb["xa_wkv"], "nn", MXU_DT, f"l{l}_xa_kv", layer=l)
        o2 = _xa_fwd(q2, kvm, f"l{l}_xa_attn")
        x, h3 = _matmul_res_norm(o2, wb["xa_wo"], l, x, ws["norm_ffn_g"][l], f"l{l}_xa_o")
        s.update(h2=h2, hm=hm, q2=q2, kvm=kvm, o2=o2, x2=x)
        up = _matmul(h3, wb["ffn_w_up"], "nn", F32, f"l{l}_ffn_up", layer=l)
        act = _ffn_fwd(up, ws["ffn_conv_w"][l], ws["ffn_conv_b"][l], f"l{l}_ffn_mid")
        if l + 1 < DEPTH:
            x, h1 = _matmul_res_norm(act, wb["ffn_w_down"], l, x, ws["norm_mix_g"][l + 1], f"l{l}_ffn_down")
        else:
            x = _matmul(act, wb["ffn_w_down"], "nn", F32, f"l{l}_ffn_down", layer=l, res=x)
        s.update(h3=h3, up=up, act=act)
        saved.append(s)

    dx, dx16, dg_final, loss = _loss_head(x, target, ws["final_norm_g"], "loss_head")
    g = {k: [None] * DEPTH for k in ("norm_mix_g", "norm_xa_g", "norm_mem_g", "xa_wq", "xa_wkv", "xa_wo", "norm_ffn_g",
                                      "ffn_w_up", "ffn_conv_w", "ffn_conv_b", "ffn_w_down")}
    g.update({k: [None] * (DEPTH // 2) for k in ("pc_w_in", "pool_w", "pool_scale", "conv_dw_w", "conv_dw_b", "conv_ln_g",
                                                 "conv_ln_b", "pc_w_out", "mla_w_dq_dkv", "mla_q_norm_g", "mla_w_uq",
                                                 "mla_kv_norm_g", "mla_w_ukv", "mla_w_o")})
    stk = {k: None for k in ("xa_wq", "xa_wkv", "xa_wo", "ffn_w_up", "ffn_w_down", "pc_w_in", "pc_w_out")}
    for l in reversed(range(DEPTH)):
        s = saved[l]
        dact = _matmul(dx16, wb["ffn_w_down"], "nt", F32, f"l{l}_b_ffn_dact", layer=l)
        stk["ffn_w_down"] = _matmul(s["act"], dx16, "tn", F32, f"l{l}_b_ffn_dwdown", stack=(stk["ffn_w_down"], l, DEPTH))
        dup, dcw, dcb = _ffn_bwd(s["up"], dact, ws["ffn_conv_w"][l], ws["ffn_conv_b"][l], f"l{l}_b_ffn_mid")
        g["ffn_conv_w"][l], g["ffn_conv_b"][l] = dcw, dcb[0]
        stk["ffn_w_up"] = _matmul(s["h3"], dup, "tn", F32, f"l{l}_b_ffn_dwup", stack=(stk["ffn_w_up"], l, DEPTH))
        dx, dx16, dg = _matmul_rms_bwd(dup, wb["ffn_w_up"], l, s["x2"], ws["norm_ffn_g"][l], dx, f"l{l}_b_ffn_dh")
        g["norm_ffn_g"][l] = dg[0]
        do2 = _matmul(dx16, wb["xa_wo"], "nt", MXU_DT, f"l{l}_b_xa_do", layer=l)
        stk["xa_wo"] = _matmul(s["o2"], dx16, "tn", F32, f"l{l}_b_xa_dwo", stack=(stk["xa_wo"], l, DEPTH))
        dq2, dkvm = _xa_bwd(s["q2"], s["kvm"], do2, f"l{l}_b_xa_attn")
        stk["xa_wq"] = _matmul(s["h2"], dq2, "tn", F32, f"l{l}_b_xa_dwq", stack=(stk["xa_wq"], l, DEPTH))
        stk["xa_wkv"] = _matmul(s["hm"], dkvm, "tn", F32, f"l{l}_b_xa_dwkv", stack=(stk["xa_wkv"], l, DEPTH))
        dhm = _matmul(dkvm, wb["xa_wkv"], "nt", F32, f"l{l}_b_xa_dhm", layer=l)
        g["norm_mem_g"][l] = _rms_bwd_gain(dhm, mem, ws["norm_mem_g"][l], f"l{l}_b_norm_mem")[0]
        dx, dx16, dg = _matmul_rms_bwd(dq2, wb["xa_wq"], l, s["x1"], ws["norm_xa_g"][l], dx, f"l{l}_b_xa_dh")
        g["norm_xa_g"][l] = dg[0]
        if l % 2 == 0:
            e = l // 2
            dcat = _matmul(dx16, wb["pc_w_out"], "nt", F32, f"l{l}_b_pc_dcat", layer=e)
            stk["pc_w_out"] = _matmul(s["cat"], dx16, "tn", F32, f"l{l}_b_pc_dwout", stack=(stk["pc_w_out"], e, DEPTH // 2))
            dz, dpw, dps, dww, dwb, dlg, dlb = _mix_bwd(
                s["z"], dcat, ws["pool_w"][e], ws["pool_scale"][e], ws["conv_dw_w"][e], ws["conv_dw_b"][e],
                ws["conv_ln_g"][e], ws["conv_ln_b"][e], f"l{l}_b_mix")
            g["pool_w"][e], g["pool_scale"][e], g["conv_dw_w"][e] = dpw, dps[0], dww
            g["conv_dw_b"][e], g["conv_ln_g"][e], g["conv_ln_b"][e] = dwb[0], dlg[0], dlb[0]
            stk["pc_w_in"] = _matmul(s["h1"], dz, "tn", F32, f"l{l}_b_pc_dwin", stack=(stk["pc_w_in"], e, DEPTH // 2))
            dx, dx16, dg = _matmul_rms_bwd(dz, wb["pc_w_in"], e, s["x0"], ws["norm_mix_g"][l], dx, f"l{l}_b_pc_dh")
        else:
            o = l // 2
            do = _matmul(dx16, wb["mla_wo"], "nt", MXU_DT, f"l{l}_b_mla_do", layer=o)
            g["mla_w_o"][o] = _unpad_wo(_matmul(s["att"], dx16, "tn", F32, f"l{l}_b_mla_dwo"))
            dqr, dkv, dkpe = _flash_bwd(s["qr"], s["kv"], s["kpe"], s["att"], do, s["lse"], f"l{l}_b_mla_attn")
            dq = _rope_heads(dqr, tables, -1.0, MLA_SCALE, f"l{l}_b_mla_rope")
            g["mla_w_uq"][o] = _unpad_wuq(_matmul(s["qn"], dq, "tn", F32, f"l{l}_b_mla_dwuq"))
            dqn = _matmul(dq, wb["mla_wuq"], "nt", F32, f"l{l}_b_mla_dqn", layer=o)
            g["mla_w_ukv"][o] = _matmul(s["kvn"], dkv, "tn", F32, f"l{l}_b_mla_dwukv")
            dkvn = _matmul(dkv, wb["mla_w_ukv"], "nt", F32, f"l{l}_b_mla_dkvn", layer=o)
            dcp, dqg, dkvg = _mla_prep_bwd(s["cp"], dqn, dkvn, dkpe, ws["mla_q_norm_g"][o], ws["mla_kv_norm_g"][o],
                                           tables, f"l{l}_b_mla_prep")
            g["mla_q_norm_g"][o], g["mla_kv_norm_g"][o] = dqg[0], dkvg[0]
            g["mla_w_dq_dkv"][o] = _unpad_wdq(_matmul(s["h1"], dcp, "tn", F32, f"l{l}_b_mla_dwdq"))
            dx, dx16, dg = _matmul_rms_bwd(dcp, wb["mla_wdq"], o, s["x0"], ws["norm_mix_g"][l], dx, f"l{l}_b_mla_dh")
        g["norm_mix_g"][l] = dg[0]
    grads = {k: jnp.stack(v) for k, v in g.items() if k not in stk}
    grads.update(stk)
    grads["final_norm_g"] = dg_final[0]
    return loss, dx, grads


BIG = (
    ("xa_wq", "row"), ("xa_wkv", "col"), ("xa_wo", "row"), ("ffn_w_up", "col"), ("ffn_w_down", "row"),
    ("pc_w_in", "col"), ("pc_w_out", "row"), ("mla_w_dq_dkv", "row"), ("mla_w_uq", "col"), ("mla_w_ukv", "col"),
    ("mla_w_o", "row"),
)
SMALL_SHARDED = ("ffn_conv_w", "conv_dw_w", "mla_q_norm_g", "mla_kv_norm_g")
SMALL_REPLICATED = ("norm_mix_g", "norm_xa_g", "norm_mem_g", "norm_ffn_g", "ffn_conv_b", "pool_w", "pool_scale",
                    "conv_dw_b", "conv_ln_g", "conv_ln_b", "final_norm_g")
WEIGHTS = ("norm_mix_g", "norm_xa_g", "norm_mem_g", "xa_wq", "xa_wkv", "xa_wo", "norm_ffn_g", "ffn_w_up", "ffn_conv_w",
           "ffn_conv_b", "ffn_w_down", "pc_w_in", "pool_w", "pool_scale", "conv_dw_w", "conv_dw_b", "conv_ln_g",
           "conv_ln_b", "pc_w_out", "mla_w_dq_dkv", "mla_q_norm_g", "mla_w_uq", "mla_kv_norm_g", "mla_w_ukv", "mla_w_o",
           "final_norm_g")
PACK_ROW = SUBLANE * LANE


def _pack(arrays):
    flat = jnp.concatenate([a.reshape(-1) for a in arrays])
    n = flat.shape[0]
    pad = (-n) % PACK_ROW
    return jnp.pad(flat, (0, pad)).reshape(-1, LANE)


def _unpack(flat, shapes):
    out, off = [], 0
    for s in shapes:
        n = int(np.prod(s))
        out.append(flat[off : off + n].reshape(s))
        off += n
    return out


def kernel(x, mem, positions, norm_mix_g, norm_xa_g, norm_mem_g, xa_wq, xa_wkv, xa_wo, norm_ffn_g, ffn_w_up, ffn_conv_w, ffn_conv_b, ffn_w_down, pc_w_in, pool_w, pool_scale, conv_dw_w, conv_dw_b, conv_ln_g, conv_ln_b, pc_w_out, mla_w_dq_dkv, mla_q_norm_g, mla_w_uq, mla_kv_norm_g, mla_w_ukv, mla_w_o, final_norm_g, loss_target, m_norm_mix_g, m_norm_xa_g, m_norm_mem_g, m_xa_wq, m_xa_wkv, m_xa_wo, m_norm_ffn_g, m_ffn_w_up, m_ffn_conv_w, m_ffn_conv_b, m_ffn_w_down, m_pc_w_in, m_pool_w, m_pool_scale, m_conv_dw_w, m_conv_dw_b, m_conv_ln_g, m_conv_ln_b, m_pc_w_out, m_mla_w_dq_dkv, m_mla_q_norm_g, m_mla_w_uq, m_mla_kv_norm_g, m_mla_w_ukv, m_mla_w_o, m_final_norm_g, v_norm_mix_g, v_norm_xa_g, v_norm_mem_g, v_xa_wq, v_xa_wkv, v_xa_wo, v_norm_ffn_g, v_ffn_w_up, v_ffn_conv_w, v_ffn_conv_b, v_ffn_w_down, v_pc_w_in, v_pool_w, v_pool_scale, v_conv_dw_w, v_conv_dw_b, v_conv_ln_g, v_conv_ln_b, v_pc_w_out, v_mla_w_dq_dkv, v_mla_q_norm_g, v_mla_w_uq, v_mla_kv_norm_g, v_mla_w_ukv, v_mla_w_o, v_final_norm_g):
    args = dict(locals())
    w = {n: args[n] for n in WEIGHTS}
    m = {n: args["m_" + n] for n in WEIGHTS}
    v = {n: args["v_" + n] for n in WEIGHTS}
    cx, cy, cc = lax.axis_index("x"), lax.axis_index("y"), lax.axis_index("c")
    chip = 2 * cx + cy

    full = _all_gather_weights([w[n].astype(MXU_DT) for n, _ in BIG], [k for _, k in BIG], "gather_weights")
    full = dict(zip([n for n, _ in BIG], full))
    small_shapes = [w[n].shape for n in SMALL_SHARDED]
    gathered = _all_gather_rows(_pack([w[n] for n in SMALL_SHARDED]), "gather_small")
    gathered = gathered.reshape(8, -1)
    ws = {n: w[n] for n in SMALL_REPLICATED}
    pieces = [_unpack(gathered[2 * k], small_shapes) for k in range(4)]
    for i, n in enumerate(SMALL_SHARDED):
        ws[n] = jnp.concatenate([pieces[k][i] for k in range(4)], axis=-1)
    wb = {n: full[n] for n in ("xa_wq", "xa_wkv", "xa_wo", "ffn_w_up", "ffn_w_down", "pc_w_in", "pc_w_out", "mla_w_ukv")}
    wb["mla_wdq"] = jnp.stack([_pad_wdq(full["mla_w_dq_dkv"][o]) for o in range(DEPTH // 2)])
    wb["mla_wuq"] = jnp.stack([_pad_wuq(full["mla_w_uq"][o]) for o in range(DEPTH // 2)])
    wb["mla_wo"] = jnp.stack([_pad_wo(full["mla_w_o"][o]) for o in range(DEPTH // 2)])

    loss, grad_x, grads = _local_step(x[0], mem[0], positions[0], loss_target[0], wb, ws)
    loss = lax.psum(loss[0, 0], ("x", "y", "c"))

    kinds = [k for _, k in BIG]
    big = [grads[n] for n, _ in BIG]
    c_idx = cc.reshape(1).astype(jnp.int32)
    chip_idx = chip.reshape(1).astype(jnp.int32)
    theirs = _exchange_halves(big, "reduce_pair")
    pair = [_add_half(gr, th, c_idx, f"reduce_pair_add_{n}") for gr, th, (n, _) in zip(big, theirs, BIG)]
    slots = _scatter_to_chips(pair, kinds, "reduce_chips")
    halves = [_sum_chips(sl, pr, chip_idx, kind, f"reduce_chips_add_{n}")
              for sl, pr, (n, kind) in zip(slots, pair, BIG)]
    others = _swap_halves(halves, "reduce_join")
    gsum, delta, new_m, new_v = {}, {}, {}, {}
    for mine, other, (n, _) in zip(halves, others, BIG):
        gsum[n], delta[n], new_m[n], new_v[n] = _adamw_halves(w[n], mine, other, m[n], v[n], c_idx, f"adamw_{n}")

    small_names = SMALL_REPLICATED + SMALL_SHARDED
    small_grad_shapes = [grads[n].shape for n in small_names]
    packed = _pack([grads[n] for n in small_names])
    rows = packed.shape[0]
    allparts = _all_gather_rows(packed, "gather_small_grads").reshape(8, rows, LANE)
    total = _sum_leading(allparts, "sum_small_grads").reshape(-1)
    for n, gfull in zip(small_names, _unpack(total, small_grad_shapes)):
        if n in SMALL_SHARDED:
            width = w[n].shape[-1]
            gfull = lax.dynamic_slice_in_dim(gfull, chip * width, width, axis=gfull.ndim - 1)
        gsum[n] = gfull

    for n in SMALL_REPLICATED + SMALL_SHARDED:
        delta[n], new_m[n], new_v[n] = _adamw(w[n], gsum[n], m[n], v[n], f"adamw_{n}")
    return (loss, grad_x[None], *[gsum[n] for n in WEIGHTS], *[delta[n] for n in WEIGHTS],
            *[new_m[n] for n in WEIGHTS], *[new_v[n] for n in WEIGHTS])
```

```python
import functools
import math

import numpy as np
import jax
import jax.numpy as jnp
from jax import lax
from jax.experimental import pallas as pl
from jax.experimental.pallas import tpu as pltpu

F32 = jnp.float32
MXU_DT = jnp.bfloat16
XFER_DT = jnp.bfloat16

D_MODEL = 1024
DEPTH = 4
MEM_LEN = 256
XA_HEADS = 4
XA_HEAD_DIM = 256
POOL_W = 512
POOL_WINDOWS = (2, 4, 8, 16)
POOL_GROUP = 128
CONV_W = 512
CONV_K = 31
MLA_HEADS = 16
QK_NOPE = 64
QK_ROPE = 32
V_HEAD = 64
Q_LORA = 384
KV_LORA = 256
ROPE_THETA = 10000.0
MLA_SCALE = 1.0 / math.sqrt(QK_NOPE + QK_ROPE)
LOG2E = math.log2(math.e)
D_FF = 2816
FFN_CONV_K = 3
EPS = 1e-6
NEG = -1e30
ADAM_LR = 0.001
ADAM_B1 = 0.9
ADAM_B2 = 0.999
ADAM_EPS = 1e-08
ADAM_WD = 0.01
ADAM_STEP = 10

HEAD_PAD = 128
C_PAD = 768
KPE_LANE = 64

VMEM_LIMIT = 52 * 1024 * 1024
BLOCK_BYTES = 6 * 1024 * 1024
LANE = 128
SUBLANE = 8

TM = 1024
TN = 1408
TK = 2048
TT = 512
TW = 256
TWF = 128
FFN_CHUNK = 256
TA = 1024
MIX_HALO = 32
FFN_HALO = 8

NN = (((1,), (0,)), ((), ()))
NT = (((1,), (1,)), ((), ()))
TN_DIMS = (((0,), (0,)), ((), ()))
MESH_ID = pl.DeviceIdType.MESH


def _cparams(*sem):
    return pltpu.CompilerParams(dimension_semantics=sem, vmem_limit_bytes=VMEM_LIMIT)


def _tile(n, pref, limit=None):
    cap = pref if limit is None else min(pref, limit)
    if n <= cap:
        return n
    t = (cap // LANE) * LANE
    while t >= LANE:
        if n % t == 0:
            return t
        t -= LANE
    return n


def _rows(t, pref):
    return t if t <= pref else pref


def _sigmoid(x):
    return 1.0 / (1.0 + jnp.exp(-x))


def _matmul(a, b, mode, out_dtype, name, layer=None, res=None, stack=None):
    if layer is None:
        b2 = b.shape
    else:
        b2 = b.shape[1:]
    if mode == "tn":
        k, m = a.shape
        k2, n = b2
    elif mode == "nn":
        m, k = a.shape
        k2, n = b2
    else:
        m, k = a.shape
        n, k2 = b2
    assert k == k2, (a.shape, b.shape, mode)
    isz_a = jnp.dtype(a.dtype).itemsize
    isz_b = jnp.dtype(b.dtype).itemsize
    if mode == "tn":
        tk = _tile(k, TK * 2 // max(isz_a, isz_b))
        tm = _tile(m, TN, BLOCK_BYTES // (tk * isz_a))
        tn = _tile(n, TN, BLOCK_BYTES // (tk * isz_b))
    else:
        tk = k
        tn = _tile(n, TN, BLOCK_BYTES // (tk * isz_b))
        tm = _tile(m, TM, min(BLOCK_BYTES // (tk * isz_a), BLOCK_BYTES // (tn * 4)))
    nk = k // tk
    grid = (m // tm, n // tn, nk)
    if mode == "nn":
        a_spec = pl.BlockSpec((tm, tk), lambda i, j, kk: (i, kk))
        b_blk, b_map, dn = (tk, tn), (lambda i, j, kk: (kk, j)), NN
    elif mode == "nt":
        a_spec = pl.BlockSpec((tm, tk), lambda i, j, kk: (i, kk))
        b_blk, b_map, dn = (tn, tk), (lambda i, j, kk: (j, kk)), NT
    else:
        a_spec = pl.BlockSpec((tk, tm), lambda i, j, kk: (kk, i))
        b_blk, b_map, dn = (tk, tn), (lambda i, j, kk: (kk, j)), TN_DIMS
    if layer is None:
        b_spec = pl.BlockSpec(b_blk, b_map)
    else:
        b_spec = pl.BlockSpec((None,) + b_blk, lambda i, j, kk: (layer,) + b_map(i, j, kk))
    in_specs = [a_spec, b_spec]
    args = [a, b]
    if res is not None:
        in_specs.append(pl.BlockSpec((tm, tn), lambda i, j, kk: (i, j)))
        args.append(res)
    has_res = res is not None
    aliases = {}
    if stack is None:
        o_spec = pl.BlockSpec((tm, tn), lambda i, j, kk: (i, j))
        out_shape = jax.ShapeDtypeStruct((m, n), out_dtype)
    else:
        buf, slab, nslab = stack
        o_spec = pl.BlockSpec((None, tm, tn), lambda i, j, kk: (slab, i, j))
        out_shape = jax.ShapeDtypeStruct((nslab, m, n), out_dtype)
        if buf is not None:
            in_specs.append(pl.BlockSpec(memory_space=pl.ANY))
            args.append(buf)
            aliases = {len(args) - 1: 0}
    n_in = len(args)

    def body(*refs):
        a_ref, b_ref = refs[0], refs[1]
        r_ref = refs[2] if has_res else None
        o_ref = refs[n_in]
        p = lax.dot_general(a_ref[...].astype(MXU_DT), b_ref[...].astype(MXU_DT), dn, preferred_element_type=F32)
        if nk == 1:
            if has_res:
                p = r_ref[...] + p
            o_ref[...] = p.astype(o_ref.dtype)
        else:
            acc_ref = refs[-1]
            kk = pl.program_id(2)

            @pl.when(kk == 0)
            def _():
                acc_ref[...] = jnp.zeros_like(acc_ref)

            acc_ref[...] += p

            @pl.when(kk == nk - 1)
            def _():
                r = acc_ref[...]
                if has_res:
                    r = r_ref[...] + r
                o_ref[...] = r.astype(o_ref.dtype)

    scratch = [pltpu.VMEM((tm, tn), F32)] if nk > 1 else []
    return pl.pallas_call(
        body,
        grid=grid,
        in_specs=in_specs,
        out_specs=o_spec,
        out_shape=out_shape,
        scratch_shapes=scratch,
        input_output_aliases=aliases,
        name=name,
        compiler_params=_cparams("parallel", "parallel", "arbitrary"),
    )(*args)


def _row_tile(m, k, isz):
    return _tile(m, TM, min(BLOCK_BYTES // (k * isz), TM if k <= 2 * TK else TM // 4))


def _matmul_res_norm(a, b, layer, res, gain, name):
    m, k = a.shape
    n = b.shape[-1]
    tm = _row_tile(m, k, jnp.dtype(a.dtype).itemsize)

    def body(a_ref, b_ref, r_ref, g_ref, x_ref, h_ref):
        x = r_ref[...] + jnp.dot(a_ref[...].astype(MXU_DT), b_ref[...].astype(MXU_DT), preferred_element_type=F32)
        x_ref[...] = x
        r = lax.rsqrt(jnp.mean(x * x, axis=-1, keepdims=True) + EPS)
        h_ref[...] = ((x * r) * g_ref[...]).astype(h_ref.dtype)

    row = pl.BlockSpec((tm, n), lambda i: (i, 0))
    return pl.pallas_call(
        body,
        grid=(m // tm,),
        in_specs=[pl.BlockSpec((tm, k), lambda i: (i, 0)), pl.BlockSpec((None, k, n), lambda i: (layer, 0, 0)), row,
                  pl.BlockSpec((1, n), lambda i: (0, 0))],
        out_specs=[row, row],
        out_shape=[jax.ShapeDtypeStruct((m, n), F32), jax.ShapeDtypeStruct((m, n), MXU_DT)],
        name=name,
        compiler_params=_cparams("parallel"),
    )(a, b, res, gain.reshape(1, n))


def _matmul_rms_bwd(a, b, layer, x, gain, dx_in, name):
    m, k = a.shape
    n = b.shape[-2]
    tm = _row_tile(m, k, jnp.dtype(a.dtype).itemsize)

    def body(a_ref, b_ref, x_ref, g_ref, dxi_ref, dx_ref, dx16_ref, dg_ref):
        @pl.when(pl.program_id(0) == 0)
        def _():
            dg_ref[...] = jnp.zeros_like(dg_ref)

        dh = lax.dot_general(a_ref[...].astype(MXU_DT), b_ref[...].astype(MXU_DT), NT, preferred_element_type=F32)
        xf = x_ref[...]
        r = lax.rsqrt(jnp.mean(xf * xf, axis=-1, keepdims=True) + EPS)
        xh = xf * r
        gy = dh * g_ref[...]
        dx = dxi_ref[...] + r * (gy - xh * jnp.mean(gy * xh, axis=-1, keepdims=True))
        dx_ref[...] = dx
        dx16_ref[...] = dx.astype(dx16_ref.dtype)
        dg_ref[...] += jnp.sum(dh * xh, axis=0, keepdims=True)

    row = pl.BlockSpec((tm, n), lambda i: (i, 0))
    vec = pl.BlockSpec((1, n), lambda i: (0, 0))
    return pl.pallas_call(
        body,
        grid=(m // tm,),
        in_specs=[pl.BlockSpec((tm, k), lambda i: (i, 0)), pl.BlockSpec((None, n, k), lambda i: (layer, 0, 0)), row, vec, row],
        out_specs=[row, row, vec],
        out_shape=[jax.ShapeDtypeStruct((m, n), F32), jax.ShapeDtypeStruct((m, n), MXU_DT), jax.ShapeDtypeStruct((1, n), F32)],
        name=name,
        compiler_params=_cparams("arbitrary"),
    )(a, b, x, gain.reshape(1, n), dx_in)


def _rms_fwd(x, g, name):
    t, d = x.shape
    tt = _rows(t, TT)

    def body(x_ref, g_ref, o_ref):
        xf = x_ref[...]
        r = lax.rsqrt(jnp.mean(xf * xf, axis=-1, keepdims=True) + EPS)
        o_ref[...] = ((xf * r) * g_ref[...]).astype(o_ref.dtype)

    return pl.pallas_call(
        body,
        grid=(t // tt,),
        in_specs=[pl.BlockSpec((tt, d), lambda i: (i, 0)), pl.BlockSpec((1, d), lambda i: (0, 0))],
        out_specs=pl.BlockSpec((tt, d), lambda i: (i, 0)),
        out_shape=jax.ShapeDtypeStruct((t, d), MXU_DT),
        name=name,
        compiler_params=_cparams("parallel"),
    )(x, g.reshape(1, d))


def _rms_bwd(dh, x, g, dx_in, name):
    t, d = x.shape
    tt = _rows(t, TT)

    def body(dh_ref, x_ref, g_ref, dxi_ref, dx_ref, dg_ref):
        @pl.when(pl.program_id(0) == 0)
        def _():
            dg_ref[...] = jnp.zeros_like(dg_ref)

        xf = x_ref[...]
        dh_v = dh_ref[...]
        r = lax.rsqrt(jnp.mean(xf * xf, axis=-1, keepdims=True) + EPS)
        xh = xf * r
        gy = dh_v * g_ref[...]
        dx = r * (gy - xh * jnp.mean(gy * xh, axis=-1, keepdims=True))
        dx_ref[...] = dxi_ref[...] + dx
        dg_ref[...] += jnp.sum(dh_v * xh, axis=0, keepdims=True)

    row = pl.BlockSpec((tt, d), lambda i: (i, 0))
    vec = pl.BlockSpec((1, d), lambda i: (0, 0))
    return pl.pallas_call(
        body,
        grid=(t // tt,),
        in_specs=[row, row, vec, row],
        out_specs=[row, vec],
        out_shape=[jax.ShapeDtypeStruct((t, d), F32), jax.ShapeDtypeStruct((1, d), F32)],
        name=name,
        compiler_params=_cparams("arbitrary"),
    )(dh, x, g.reshape(1, d), dx_in)


def _rms_bwd_gain(dh, x, g, name):
    t, d = x.shape
    tt = _rows(t, TT)

    def body(dh_ref, x_ref, dg_ref):
        @pl.when(pl.program_id(0) == 0)
        def _():
            dg_ref[...] = jnp.zeros_like(dg_ref)

        xf = x_ref[...]
        r = lax.rsqrt(jnp.mean(xf * xf, axis=-1, keepdims=True) + EPS)
        dg_ref[...] += jnp.sum(dh_ref[...] * (xf * r), axis=0, keepdims=True)

    row = pl.BlockSpec((tt, d), lambda i: (i, 0))
    vec = pl.BlockSpec((1, d), lambda i: (0, 0))
    return pl.pallas_call(
        body,
        grid=(t // tt,),
        in_specs=[row, row],
        out_specs=vec,
        out_shape=jax.ShapeDtypeStruct((1, d), F32),
        name=name,
        compiler_params=_cparams("arbitrary"),
    )(dh, x)


def _loss_head(x, target, g, name):
    t, d = x.shape
    tt = _rows(t, TT)

    def body(x_ref, t_ref, g_ref, dx_ref, dx16_ref, dg_ref, loss_ref):
        @pl.when(pl.program_id(0) == 0)
        def _():
            dg_ref[...] = jnp.zeros_like(dg_ref)
            loss_ref[...] = jnp.zeros_like(loss_ref)

        xf = x_ref[...]
        gv = g_ref[...]
        r = lax.rsqrt(jnp.mean(xf * xf, axis=-1, keepdims=True) + EPS)
        xh = xf * r
        err = xh * gv - t_ref[...]
        e2 = jnp.sum(err * err, axis=-1, keepdims=True)
        loss_ref[...] += (0.5 / d) * jnp.sum(e2, axis=0, keepdims=True)
        dy = err * (1.0 / d)
        gy = dy * gv
        dx = r * (gy - xh * jnp.mean(gy * xh, axis=-1, keepdims=True))
        dx_ref[...] = dx
        dx16_ref[...] = dx.astype(dx16_ref.dtype)
        dg_ref[...] += jnp.sum(dy * xh, axis=0, keepdims=True)

    row = pl.BlockSpec((tt, d), lambda i: (i, 0))
    vec = pl.BlockSpec((1, d), lambda i: (0, 0))
    return pl.pallas_call(
        body,
        grid=(t // tt,),
        in_specs=[row, row, vec],
        out_specs=[row, row, vec, pl.BlockSpec((1, 1), lambda i: (0, 0))],
        out_shape=[
            jax.ShapeDtypeStruct((t, d), F32),
            jax.ShapeDtypeStruct((t, d), MXU_DT),
            jax.ShapeDtypeStruct((1, d), F32),
            jax.ShapeDtypeStruct((1, 1), F32),
        ],
        name=name,
        compiler_params=_cparams("arbitrary"),
    )(x, target, g.reshape(1, d))


def _prev_halo(tt, hp, width):
    return pl.BlockSpec((hp, width), lambda i: (jnp.maximum(i * (tt // hp) - 1, 0), 0))


def _next_halo(tt, hp, width, t):
    return pl.BlockSpec((hp, width), lambda i: (jnp.minimum((i + 1) * (tt // hp), t // hp - 1), 0))


def _ffn_chunks():
    return [(c0, FFN_CHUNK) for c0 in range(0, D_FF, FFN_CHUNK)]


def _ffn_fwd(up, conv_w, conv_b, name):
    t = up.shape[0]
    tt = _rows(t, 2 * TWF)
    hp = FFN_HALO

    def body(up_ref, gp_ref, w_ref, b_ref, act_ref, ext_ref):
        first = pl.program_id(0) == 0
        for c0, cw in _ffn_chunks():
            ga = pl.ds(D_FF + c0, cw)
            ext_ref[0:hp, :] = jnp.where(first, 0.0, gp_ref[:, ga])
            ext_ref[hp : hp + tt, :] = up_ref[:, ga]
            gc = b_ref[:, pl.ds(c0, cw)]
            for j in range(FFN_CONV_K):
                off = hp - (FFN_CONV_K - 1) + j
                gc = gc + w_ref[j : j + 1, pl.ds(c0, cw)] * ext_ref[off : off + tt, :]
            a = up_ref[:, pl.ds(c0, cw)]
            act_ref[:, pl.ds(c0, cw)] = (gc * _sigmoid(gc) * a).astype(act_ref.dtype)

    return pl.pallas_call(
        body,
        grid=(t // tt,),
        in_specs=[
            pl.BlockSpec((tt, 2 * D_FF), lambda i: (i, 0)),
            _prev_halo(tt, hp, 2 * D_FF),
            pl.BlockSpec((FFN_CONV_K, D_FF), lambda i: (0, 0)),
            pl.BlockSpec((1, D_FF), lambda i: (0, 0)),
        ],
        out_specs=pl.BlockSpec((tt, D_FF), lambda i: (i, 0)),
        out_shape=jax.ShapeDtypeStruct((t, D_FF), MXU_DT),
        scratch_shapes=[pltpu.VMEM((tt + hp, FFN_CHUNK), F32)],
        name=name,
        compiler_params=_cparams("parallel"),
    )(up, up, conv_w, conv_b.reshape(1, D_FF))


def _ffn_bwd(up, dact, conv_w, conv_b, name):
    t = up.shape[0]
    tt = _rows(t, TWF)
    hp = FFN_HALO
    nt = t // tt
    kk = FFN_CONV_K

    def body(up_ref, upp_ref, upn_ref, da_ref, dan_ref, w_ref, b_ref, dup_ref, dw_ref, db_ref, ext_ref, dgc_ref):
        i = pl.program_id(0)
        first = i == 0
        last = i == nt - 1

        @pl.when(first)
        def _():
            dw_ref[...] = jnp.zeros_like(dw_ref)
            db_ref[...] = jnp.zeros_like(db_ref)

        for c0, cw in _ffn_chunks():
            ca = pl.ds(c0, cw)
            ga = pl.ds(D_FF + c0, cw)
            ext_ref[0:hp, :] = jnp.where(first, 0.0, upp_ref[:, ga])
            ext_ref[hp : hp + tt, :] = up_ref[:, ga]
            ext_ref[hp + tt : hp + tt + hp, :] = upn_ref[:, ga]
            gc = b_ref[:, ca]
            for j in range(kk):
                off = hp - (kk - 1) + j
                gc = gc + w_ref[j : j + 1, ca] * ext_ref[off : off + tt + hp, :]
            sg = _sigmoid(gc)
            silu = gc * sg
            dsilu = sg * (1.0 + gc * (1.0 - sg))
            a_all = jnp.concatenate([up_ref[:, ca], upn_ref[:, ca]], axis=0)
            dact_all = jnp.concatenate([da_ref[:, ca], jnp.where(last, 0.0, dan_ref[:, ca])], axis=0)
            dgc = dact_all * a_all * dsilu
            dgc_ref[...] = dgc
            dup_ref[:, ca] = (dact_all[0:tt] * silu[0:tt]).astype(dup_ref.dtype)
            dg = jnp.zeros((tt, cw), F32)
            for j in range(kk):
                dg = dg + w_ref[j : j + 1, ca] * dgc_ref[kk - 1 - j : kk - 1 - j + tt, :]
            dup_ref[:, ga] = dg.astype(dup_ref.dtype)
            dgc_t = dgc[0:tt]
            db_ref[:, ca] += jnp.sum(dgc_t, axis=0, keepdims=True)
            for j in range(kk):
                off = hp - (kk - 1) + j
                dw_ref[j : j + 1, ca] += jnp.sum(dgc_t * ext_ref[off : off + tt, :], axis=0, keepdims=True)

    return pl.pallas_call(
        body,
        grid=(nt,),
        in_specs=[
            pl.BlockSpec((tt, 2 * D_FF), lambda i: (i, 0)),
            _prev_halo(tt, hp, 2 * D_FF),
            _next_halo(tt, hp, 2 * D_FF, t),
            pl.BlockSpec((tt, D_FF), lambda i: (i, 0)),
            _next_halo(tt, hp, D_FF, t),
            pl.BlockSpec((kk, D_FF), lambda i: (0, 0)),
            pl.BlockSpec((1, D_FF), lambda i: (0, 0)),
        ],
        out_specs=[
            pl.BlockSpec((tt, 2 * D_FF), lambda i: (i, 0)),
            pl.BlockSpec((kk, D_FF), lambda i: (0, 0)),
            pl.BlockSpec((1, D_FF), lambda i: (0, 0)),
        ],
        out_shape=[
            jax.ShapeDtypeStruct((t, 2 * D_FF), MXU_DT),
            jax.ShapeDtypeStruct((kk, D_FF), F32),
            jax.ShapeDtypeStruct((1, D_FF), F32),
        ],
        scratch_shapes=[pltpu.VMEM((tt + 2 * hp, FFN_CHUNK), F32), pltpu.VMEM((tt + hp, FFN_CHUNK), F32)],
        name=name,
        compiler_params=_cparams("arbitrary"),
    )(up, up, up, dact, dact, conv_w, conv_b.reshape(1, D_FF))


def _layernorm_silu(cv, ln_g, ln_b):
    mu = jnp.mean(cv, axis=-1, keepdims=True)
    xc = cv - mu
    rstd = lax.rsqrt(jnp.mean(xc * xc, axis=-1, keepdims=True) + EPS)
    xh = xc * rstd
    a = xh * ln_g + ln_b
    return xh, rstd, a


def _shifted_copies(ref, n):
    for b in range(1, SUBLANE):
        ref[b, 0 : n - SUBLANE, :] = ref[0, b : b + n - SUBLANE, :]


def _tap(ref, offset, rows, cols):
    b = offset % SUBLANE
    return ref[b, offset - b : offset - b + rows, cols]


def _mix_fwd(z, pool_w, pool_scale, dw_w, dw_b, ln_g, ln_b, name):
    t = z.shape[0]
    tt = _rows(t, TW)
    hp = MIX_HALO
    zw = POOL_W + 2 * CONV_W

    def body(z_ref, zp_ref, pw_ref, ps_ref, w_ref, b_ref, lg_ref, lb_ref, cat_ref, eu_ref, egl_ref, cv_ref):
        i = pl.program_id(0)
        first = i == 0
        eu_ref[0:hp, :] = jnp.where(first, 0.0, zp_ref[:, 0:POOL_W])
        eu_ref[hp : hp + tt, :] = z_ref[:, 0:POOL_W]
        glp = zp_ref[:, POOL_W : POOL_W + CONV_W] * _sigmoid(zp_ref[:, POOL_W + CONV_W : zw])
        egl_ref[0, 0:hp, :] = jnp.where(first, 0.0, glp)
        egl_ref[0, hp : hp + tt, :] = z_ref[:, POOL_W : POOL_W + CONV_W] * _sigmoid(z_ref[:, POOL_W + CONV_W : zw])
        _shifted_copies(egl_ref, tt + hp)
        row = i * tt + lax.broadcasted_iota(jnp.int32, (tt, 1), 0)
        for gi, w in enumerate(POOL_WINDOWS):
            cols = pl.ds(gi * POOL_GROUP, POOL_GROUP)
            u = eu_ref[hp : hp + tt, cols]
            acc = u
            for k in range(1, w):
                acc = acc + eu_ref[hp - k : hp - k + tt, cols]
            cnt = jnp.minimum(row + 1, w).astype(F32)
            pooled = acc / cnt - u
            y = jnp.dot(pooled.astype(MXU_DT), pw_ref[gi].astype(MXU_DT), preferred_element_type=F32)
            cat_ref[:, cols] = (y * ps_ref[:, cols]).astype(cat_ref.dtype)
        for c0 in range(0, CONV_W, LANE):
            cs = pl.ds(c0, LANE)
            acc = jnp.broadcast_to(b_ref[:, cs], (tt, LANE))
            for j in range(CONV_K):
                acc = acc + w_ref[j : j + 1, cs] * _tap(egl_ref, hp - (CONV_K - 1) + j, tt, cs)
            cv_ref[:, cs] = acc
        _, _, a = _layernorm_silu(cv_ref[...], lg_ref[...], lb_ref[...])
        cat_ref[:, POOL_W : POOL_W + CONV_W] = (a * _sigmoid(a)).astype(cat_ref.dtype)

    vec = pl.BlockSpec((1, CONV_W), lambda i: (0, 0))
    return pl.pallas_call(
        body,
        grid=(t // tt,),
        in_specs=[
            pl.BlockSpec((tt, zw), lambda i: (i, 0)),
            _prev_halo(tt, hp, zw),
            pl.BlockSpec((len(POOL_WINDOWS), POOL_GROUP, POOL_GROUP), lambda i: (0, 0, 0)),
            vec,
            pl.BlockSpec((CONV_K, CONV_W), lambda i: (0, 0)),
            vec,
            vec,
            vec,
        ],
        out_specs=pl.BlockSpec((tt, POOL_W + CONV_W), lambda i: (i, 0)),
        out_shape=jax.ShapeDtypeStruct((t, POOL_W + CONV_W), MXU_DT),
        scratch_shapes=[pltpu.VMEM((tt + hp, POOL_W), F32), pltpu.VMEM((SUBLANE, tt + hp, CONV_W), F32),
                        pltpu.VMEM((tt, CONV_W), F32)],
        name=name,
        compiler_params=_cparams("parallel"),
    )(z, z, pool_w, pool_scale.reshape(1, POOL_W), dw_w, dw_b.reshape(1, CONV_W), ln_g.reshape(1, CONV_W), ln_b.reshape(1, CONV_W))


def _mix_bwd(z, dcat, pool_w, pool_scale, dw_w, dw_b, ln_g, ln_b, name):
    t = z.shape[0]
    tt = _rows(t, TW)
    hp = MIX_HALO
    nt = t // tt
    zw = POOL_W + 2 * CONV_W
    ng = len(POOL_WINDOWS)

    def body(z_ref, zp_ref, zn_ref, dc_ref, dcn_ref, pw_ref, ps_ref, w_ref, b_ref, lg_ref, lb_ref,
             dz_ref, dpw_ref, dps_ref, dww_ref, dwb_ref, dlg_ref, dlb_ref, eu_ref, ee_ref, egl_ref, edcv_ref, cv_ref):
        i = pl.program_id(0)
        first = i == 0
        last = i == nt - 1

        @pl.when(first)
        def _():
            for r in (dpw_ref, dps_ref, dww_ref, dwb_ref, dlg_ref, dlb_ref):
                r[...] = jnp.zeros_like(r)

        eu_ref[0:hp, :] = jnp.where(first, 0.0, zp_ref[:, 0:POOL_W])
        eu_ref[hp : hp + tt, :] = z_ref[:, 0:POOL_W]
        row = i * tt + lax.broadcasted_iota(jnp.int32, (tt, 1), 0)
        row_ext = i * tt + lax.broadcasted_iota(jnp.int32, (tt + hp, 1), 0)
        for gi, w in enumerate(POOL_WINDOWS):
            cols = pl.ds(gi * POOL_GROUP, POOL_GROUP)
            u = eu_ref[hp : hp + tt, cols]
            acc = u
            for k in range(1, w):
                acc = acc + eu_ref[hp - k : hp - k + tt, cols]
            pooled = (acc / jnp.minimum(row + 1, w).astype(F32) - u).astype(MXU_DT)
            pw = pw_ref[gi].astype(MXU_DT)
            dya = dc_ref[:, cols]
            y = jnp.dot(pooled, pw, preferred_element_type=F32)
            dps_ref[:, cols] += jnp.sum(dya * y, axis=0, keepdims=True)
            scale = ps_ref[:, cols]
            dy_all = jnp.concatenate([dya, jnp.where(last, 0.0, dcn_ref[:, cols])], axis=0) * scale
            dy_all = dy_all.astype(MXU_DT)
            dpw_ref[gi] += lax.dot_general(pooled, dy_all[0:tt], TN_DIMS, preferred_element_type=F32)
            dpooled = lax.dot_general(dy_all, pw, NT, preferred_element_type=F32)
            ee_ref[:, cols] = dpooled / jnp.minimum(row_ext + 1, w).astype(F32)
            du = -dpooled[0:tt]
            for k in range(w):
                du = du + ee_ref[k : k + tt, cols]
            dz_ref[:, cols] = du.astype(dz_ref.dtype)

        ca = slice(POOL_W, POOL_W + CONV_W)
        cb = slice(POOL_W + CONV_W, zw)
        egl_ref[0, 0:hp, :] = jnp.where(first, 0.0, zp_ref[:, ca] * _sigmoid(zp_ref[:, cb]))
        egl_ref[0, hp : hp + tt, :] = z_ref[:, ca] * _sigmoid(z_ref[:, cb])
        egl_ref[0, hp + tt : hp + tt + hp, :] = zn_ref[:, ca] * _sigmoid(zn_ref[:, cb])
        _shifted_copies(egl_ref, tt + 2 * hp)
        for c0 in range(0, CONV_W, LANE):
            cs = pl.ds(c0, LANE)
            acc = jnp.broadcast_to(b_ref[:, cs], (tt + hp, LANE))
            for j in range(CONV_K):
                acc = acc + w_ref[j : j + 1, cs] * _tap(egl_ref, hp - (CONV_K - 1) + j, tt + hp, cs)
            cv_ref[:, cs] = acc
        lg = lg_ref[...]
        xh, rstd, a = _layernorm_silu(cv_ref[...], lg, lb_ref[...])
        sa = _sigmoid(a)
        dyb = jnp.concatenate([dc_ref[:, ca], jnp.where(last, 0.0, dcn_ref[:, ca])], axis=0)
        da = dyb * (sa * (1.0 + a * (1.0 - sa)))
        dlg_ref[...] += jnp.sum(da[0:tt] * xh[0:tt], axis=0, keepdims=True)
        dlb_ref[...] += jnp.sum(da[0:tt], axis=0, keepdims=True)
        dxh = da * lg
        dcv = rstd * (dxh - jnp.mean(dxh, axis=-1, keepdims=True) - xh * jnp.mean(dxh * xh, axis=-1, keepdims=True))
        edcv_ref[0] = dcv
        _shifted_copies(edcv_ref, tt + hp)
        dwb_ref[...] += jnp.sum(dcv[0:tt], axis=0, keepdims=True)
        for c0 in range(0, CONV_W, LANE):
            cs = pl.ds(c0, LANE)
            gl_t = egl_ref[0, hp : hp + tt, cs]
            dgl = jnp.zeros((tt, LANE), F32)
            for j in range(CONV_K):
                tap = _tap(edcv_ref, CONV_K - 1 - j, tt, cs)
                dww_ref[j : j + 1, cs] += jnp.sum(gl_t * tap, axis=0, keepdims=True)
                dgl = dgl + w_ref[j : j + 1, cs] * tap
            ga = z_ref[:, pl.ds(POOL_W + c0, LANE)]
            sgb = _sigmoid(z_ref[:, pl.ds(POOL_W + CONV_W + c0, LANE)])
            dz_ref[:, pl.ds(POOL_W + c0, LANE)] = (dgl * sgb).astype(dz_ref.dtype)
            dz_ref[:, pl.ds(POOL_W + CONV_W + c0, LANE)] = (dgl * ga * sgb * (1.0 - sgb)).astype(dz_ref.dtype)

    vec = pl.BlockSpec((1, CONV_W), lambda i: (0, 0))
    pw_spec = pl.BlockSpec((ng, POOL_GROUP, POOL_GROUP), lambda i: (0, 0, 0))
    w_spec = pl.BlockSpec((CONV_K, CONV_W), lambda i: (0, 0))
    return pl.pallas_call(
        body,
        grid=(nt,),
        in_specs=[
            pl.BlockSpec((tt, zw), lambda i: (i, 0)),
            _prev_halo(tt, hp, zw),
            _next_halo(tt, hp, zw, t),
            pl.BlockSpec((tt, POOL_W + CONV_W), lambda i: (i, 0)),
            _next_halo(tt, hp, POOL_W + CONV_W, t),
            pw_spec, vec, w_spec, vec, vec, vec,
        ],
        out_specs=[pl.BlockSpec((tt, zw), lambda i: (i, 0)), pw_spec, vec, w_spec, vec, vec, vec],
        out_shape=[
            jax.ShapeDtypeStruct((t, zw), MXU_DT),
            jax.ShapeDtypeStruct((ng, POOL_GROUP, POOL_GROUP), F32),
            jax.ShapeDtypeStruct((1, POOL_W), F32),
            jax.ShapeDtypeStruct((CONV_K, CONV_W), F32),
            jax.ShapeDtypeStruct((1, CONV_W), F32),
            jax.ShapeDtypeStruct((1, CONV_W), F32),
            jax.ShapeDtypeStruct((1, CONV_W), F32),
        ],
        scratch_shapes=[
            pltpu.VMEM((tt + hp, POOL_W), F32),
            pltpu.VMEM((tt + hp, POOL_W), F32),
            pltpu.VMEM((SUBLANE, tt + 2 * hp, CONV_W), F32),
            pltpu.VMEM((SUBLANE, tt + hp, CONV_W), F32),
            pltpu.VMEM((tt + hp, CONV_W), F32),
        ],
        name=name,
        compiler_params=_cparams("arbitrary"),
    )(z, z, z, dcat, dcat, pool_w, pool_scale.reshape(1, POOL_W), dw_w, dw_b.reshape(1, CONV_W),
      ln_g.reshape(1, CONV_W), ln_b.reshape(1, CONV_W))


def _xa_fwd(q, kvm, name):
    t = q.shape[0]
    tt = _rows(t, TT)
    scale = XA_HEAD_DIM ** -0.5

    def body(q_ref, kv_ref, o_ref):
        for h in range(XA_HEADS):
            cs = pl.ds(h * XA_HEAD_DIM, XA_HEAD_DIM)
            vs = pl.ds(D_MODEL + h * XA_HEAD_DIM, XA_HEAD_DIM)
            s = lax.dot_general(q_ref[:, cs], kv_ref[:, cs], NT, preferred_element_type=F32) * scale
            p = jnp.exp(s - jnp.max(s, axis=-1, keepdims=True))
            p = p / jnp.sum(p, axis=-1, keepdims=True)
            o_ref[:, cs] = jnp.dot(p.astype(MXU_DT), kv_ref[:, vs], preferred_element_type=F32).astype(o_ref.dtype)

    return pl.pallas_call(
        body,
        grid=(t // tt,),
        in_specs=[pl.BlockSpec((tt, D_MODEL), lambda i: (i, 0)), pl.BlockSpec((MEM_LEN, 2 * D_MODEL), lambda i: (0, 0))],
        out_specs=pl.BlockSpec((tt, D_MODEL), lambda i: (i, 0)),
        out_shape=jax.ShapeDtypeStruct((t, D_MODEL), MXU_DT),
        name=name,
        compiler_params=_cparams("parallel"),
    )(q, kvm)


def _xa_bwd(q, kvm, do, name):
    t = q.shape[0]
    tt = _rows(t, TT)
    scale = XA_HEAD_DIM ** -0.5

    def body(q_ref, kv_ref, do_ref, dq_ref, dkv_ref):
        @pl.when(pl.program_id(0) == 0)
        def _():
            dkv_ref[...] = jnp.zeros_like(dkv_ref)

        for h in range(XA_HEADS):
            cs = pl.ds(h * XA_HEAD_DIM, XA_HEAD_DIM)
            vs = pl.ds(D_MODEL + h * XA_HEAD_DIM, XA_HEAD_DIM)
            qh = q_ref[:, cs]
            kh = kv_ref[:, cs]
            doh = do_ref[:, cs]
            s = lax.dot_general(qh, kh, NT, preferred_element_type=F32) * scale
            p = jnp.exp(s - jnp.max(s, axis=-1, keepdims=True))
            p = p / jnp.sum(p, axis=-1, keepdims=True)
            dp = lax.dot_general(doh, kv_ref[:, vs], NT, preferred_element_type=F32)
            ds = (p * (dp - jnp.sum(p * dp, axis=-1, keepdims=True)) * scale).astype(MXU_DT)
            dq_ref[:, cs] = jnp.dot(ds, kh, preferred_element_type=F32).astype(dq_ref.dtype)
            dkv_ref[:, cs] += lax.dot_general(ds, qh, TN_DIMS, preferred_element_type=F32)
            dkv_ref[:, vs] += lax.dot_general(p.astype(MXU_DT), doh, TN_DIMS, preferred_element_type=F32)

    row = pl.BlockSpec((tt, D_MODEL), lambda i: (i, 0))
    kvs = pl.BlockSpec((MEM_LEN, 2 * D_MODEL), lambda i: (0, 0))
    return pl.pallas_call(
        body,
        grid=(t // tt,),
        in_specs=[row, kvs, row],
        out_specs=[row, kvs],
        out_shape=[jax.ShapeDtypeStruct((t, D_MODEL), MXU_DT), jax.ShapeDtypeStruct((MEM_LEN, 2 * D_MODEL), F32)],
        name=name,
        compiler_params=_cparams("arbitrary"),
    )(q, kvm, do)


def _rope_tables(positions, name):
    t = positions.shape[0]
    tt = _rows(t, TT)
    inv = 1.0 / (ROPE_THETA ** (np.arange(0, QK_ROPE, 2, dtype=np.float32) / QK_ROPE))
    lanes = np.zeros((1, HEAD_PAD), np.float32)
    half = QK_ROPE // 2
    lanes[0, KPE_LANE : KPE_LANE + half] = inv
    lanes[0, KPE_LANE + half : KPE_LANE + QK_ROPE] = inv

    def body(pos_ref, inv_ref, cos_ref, sa_ref, sb_ref):
        ang = pos_ref[...].astype(F32) * inv_ref[...]
        lane = lax.broadcasted_iota(jnp.int32, (tt, HEAD_PAD), 1)
        c = jnp.cos(ang)
        s = jnp.sin(ang)
        lo = (lane >= KPE_LANE) & (lane < KPE_LANE + half)
        hi = (lane >= KPE_LANE + half) & (lane < KPE_LANE + QK_ROPE)
        cos_ref[...] = jnp.where(lo | hi, c, 1.0)
        sa_ref[...] = jnp.where(hi, s, 0.0)
        sb_ref[...] = jnp.where(lo, -s, 0.0)

    tab = pl.BlockSpec((tt, HEAD_PAD), lambda i: (i, 0))
    return pl.pallas_call(
        body,
        grid=(t // tt,),
        in_specs=[pl.BlockSpec((tt, 1), lambda i: (i, 0)), pl.BlockSpec((1, HEAD_PAD), lambda i: (0, 0))],
        out_specs=[tab, tab, tab],
        out_shape=[jax.ShapeDtypeStruct((t, HEAD_PAD), F32)] * 3,
        name=name,
        compiler_params=_cparams("parallel"),
    )(positions, jnp.asarray(lanes))


def _rotate(x, cos, sa, sb, sign):
    half = QK_ROPE // 2
    return x * cos + sign * (pltpu.roll(x, half, 1) * sa + pltpu.roll(x, HEAD_PAD - half, 1) * sb)


def _rope_heads(x, tables, sign, scale, name):
    t, w = x.shape
    tt = _rows(t, TT)
    nh = w // HEAD_PAD

    def body(x_ref, c_ref, sa_ref, sb_ref, o_ref):
        cos, sa, sb = c_ref[...] * scale, sa_ref[...] * scale, sb_ref[...] * scale
        for h in range(nh):
            cs = pl.ds(h * HEAD_PAD, HEAD_PAD)
            o_ref[:, cs] = _rotate(x_ref[:, cs], cos, sa, sb, sign).astype(o_ref.dtype)

    tab = pl.BlockSpec((tt, HEAD_PAD), lambda i: (i, 0))
    row = pl.BlockSpec((tt, w), lambda i: (i, 0))
    return pl.pallas_call(
        body,
        grid=(t // tt,),
        in_specs=[row, tab, tab, tab],
        out_specs=row,
        out_shape=jax.ShapeDtypeStruct((t, w), MXU_DT),
        name=name,
        compiler_params=_cparams("parallel"),
    )(x, *tables)


def _matmul_rope(a, b, layer, tables, scale, name):
    m, k = a.shape
    n = b.shape[-1]
    tm = _rows(m, TT)
    nh = n // HEAD_PAD

    def body(a_ref, b_ref, c_ref, sa_ref, sb_ref, o_ref):
        q = jnp.dot(a_ref[...].astype(MXU_DT), b_ref[...].astype(MXU_DT), preferred_element_type=F32)
        cos, sa, sb = c_ref[...] * scale, sa_ref[...] * scale, sb_ref[...] * scale
        for h in range(nh):
            cs = slice(h * HEAD_PAD, (h + 1) * HEAD_PAD)
            o_ref[:, cs] = _rotate(q[:, cs], cos, sa, sb, 1.0).astype(o_ref.dtype)

    tab = pl.BlockSpec((tm, HEAD_PAD), lambda i: (i, 0))
    return pl.pallas_call(
        body,
        grid=(m // tm,),
        in_specs=[pl.BlockSpec((tm, k), lambda i: (i, 0)), pl.BlockSpec((None, k, n), lambda i: (layer, 0, 0)), tab, tab, tab],
        out_specs=pl.BlockSpec((tm, n), lambda i: (i, 0)),
        out_shape=jax.ShapeDtypeStruct((m, n), MXU_DT),
        name=name,
        compiler_params=_cparams("parallel"),
    )(a, b, *tables)


def _mla_prep(cp, qg, kvg, tables, name):
    t = cp.shape[0]
    tt = _rows(t, TT)

    def body(cp_ref, qg_ref, kvg_ref, c_ref, sa_ref, sb_ref, qn_ref, kvn_ref, kpe_ref):
        cq = cp_ref[:, 0:Q_LORA]
        r = lax.rsqrt(jnp.mean(cq * cq, axis=-1, keepdims=True) + EPS)
        qn_ref[...] = ((cq * r) * qg_ref[...]).astype(qn_ref.dtype)
        ckv = cp_ref[:, Q_LORA : Q_LORA + KV_LORA]
        r = lax.rsqrt(jnp.mean(ckv * ckv, axis=-1, keepdims=True) + EPS)
        kvn_ref[...] = ((ckv * r) * kvg_ref[...]).astype(kvn_ref.dtype)
        kpe = cp_ref[:, Q_LORA + KV_LORA : C_PAD]
        kpe_ref[...] = _rotate(kpe, c_ref[...], sa_ref[...], sb_ref[...], 1.0).astype(kpe_ref.dtype)

    tab = pl.BlockSpec((tt, HEAD_PAD), lambda i: (i, 0))
    return pl.pallas_call(
        body,
        grid=(t // tt,),
        in_specs=[
            pl.BlockSpec((tt, C_PAD), lambda i: (i, 0)),
            pl.BlockSpec((1, Q_LORA), lambda i: (0, 0)),
            pl.BlockSpec((1, KV_LORA), lambda i: (0, 0)),
            tab, tab, tab,
        ],
        out_specs=[
            pl.BlockSpec((tt, Q_LORA), lambda i: (i, 0)),
            pl.BlockSpec((tt, KV_LORA), lambda i: (i, 0)),
            tab,
        ],
        out_shape=[
            jax.ShapeDtypeStruct((t, Q_LORA), MXU_DT),
            jax.ShapeDtypeStruct((t, KV_LORA), MXU_DT),
            jax.ShapeDtypeStruct((t, HEAD_PAD), MXU_DT),
        ],
        name=name,
        compiler_params=_cparams("parallel"),
    )(cp, qg.reshape(1, Q_LORA), kvg.reshape(1, KV_LORA), *tables)


def _mla_prep_bwd(cp, dqn, dkvn, dkpe_heads, qg, kvg, tables, name):
    t = cp.shape[0]
    tt = _rows(t, TT)

    def norm_bwd(x, dy, g):
        r = lax.rsqrt(jnp.mean(x * x, axis=-1, keepdims=True) + EPS)
        xh = x * r
        gy = dy * g
        return r * (gy - xh * jnp.mean(gy * xh, axis=-1, keepdims=True)), jnp.sum(dy * xh, axis=0, keepdims=True)

    def body(cp_ref, dqn_ref, dkvn_ref, dkpe_ref, qg_ref, kvg_ref, c_ref, sa_ref, sb_ref, dcp_ref, dqg_ref, dkvg_ref):
        @pl.when(pl.program_id(0) == 0)
        def _():
            dqg_ref[...] = jnp.zeros_like(dqg_ref)
            dkvg_ref[...] = jnp.zeros_like(dkvg_ref)

        dcq, dg = norm_bwd(cp_ref[:, 0:Q_LORA], dqn_ref[...], qg_ref[...])
        dcp_ref[:, 0:Q_LORA] = dcq.astype(dcp_ref.dtype)
        dqg_ref[...] += dg
        dckv, dg = norm_bwd(cp_ref[:, Q_LORA : Q_LORA + KV_LORA], dkvn_ref[...], kvg_ref[...])
        dcp_ref[:, Q_LORA : Q_LORA + KV_LORA] = dckv.astype(dcp_ref.dtype)
        dkvg_ref[...] += dg
        dk = dkpe_ref[0]
        for h in range(1, MLA_HEADS):
            dk = dk + dkpe_ref[h]
        dcp_ref[:, Q_LORA + KV_LORA : C_PAD] = _rotate(dk, c_ref[...], sa_ref[...], sb_ref[...], -1.0).astype(dcp_ref.dtype)

    tab = pl.BlockSpec((tt, HEAD_PAD), lambda i: (i, 0))
    return pl.pallas_call(
        body,
        grid=(t // tt,),
        in_specs=[
            pl.BlockSpec((tt, C_PAD), lambda i: (i, 0)),
            pl.BlockSpec((tt, Q_LORA), lambda i: (i, 0)),
            pl.BlockSpec((tt, KV_LORA), lambda i: (i, 0)),
            pl.BlockSpec((MLA_HEADS, tt, HEAD_PAD), lambda i: (0, i, 0)),
            pl.BlockSpec((1, Q_LORA), lambda i: (0, 0)),
            pl.BlockSpec((1, KV_LORA), lambda i: (0, 0)),
            tab, tab, tab,
        ],
        out_specs=[
            pl.BlockSpec((tt, C_PAD), lambda i: (i, 0)),
            pl.BlockSpec((1, Q_LORA), lambda i: (0, 0)),
            pl.BlockSpec((1, KV_LORA), lambda i: (0, 0)),
        ],
        out_shape=[
            jax.ShapeDtypeStruct((t, C_PAD), MXU_DT),
            jax.ShapeDtypeStruct((1, Q_LORA), F32),
            jax.ShapeDtypeStruct((1, KV_LORA), F32),
        ],
        name=name,
        compiler_params=_cparams("arbitrary"),
    )(cp, dqn, dkvn, dkpe_heads, qg.reshape(1, Q_LORA), kvg.reshape(1, KV_LORA), *tables)


def _flash_fwd(qs, kv, kpe, name):
    t = qs.shape[0]
    ta = _rows(t, TA)
    tq = ta
    nq = t // tq
    sub = ta // 2

    def body(q_ref, kv_ref, kpe_ref, o_ref, lse_ref):
        qi = pl.program_id(1)
        q = q_ref[...]
        lane = lax.broadcasted_iota(jnp.int32, (ta, HEAD_PAD), 1)

        def kblock(j):
            rows = pl.ds(pl.multiple_of(j * ta, ta), ta)
            kvb = kv_ref[rows, :]
            ones_v = jnp.where(lane < QK_NOPE, jnp.ones_like(kvb), kvb)
            return ones_v, jnp.where(lane < QK_NOPE, kvb, kpe_ref[rows, :])

        def update(carry, s, ones_v):
            m, acc = carry
            m_new = jnp.maximum(m, jnp.max(s, axis=-1, keepdims=True))
            p = jnp.exp2(s - m_new).astype(MXU_DT)
            acc = jnp.exp2(m - m_new) * acc + jnp.dot(p, ones_v, preferred_element_type=F32)
            return m_new, acc

        def step(j, carry):
            ones_v, k = kblock(j)
            return update(carry, lax.dot_general(q, k, NT, preferred_element_type=F32), ones_v)

        init = (jnp.full((tq, 1), -jnp.inf, F32), jnp.zeros((tq, HEAD_PAD), F32))
        m_all, acc_all = lax.fori_loop(0, qi, step, init)
        ones_v, k = kblock(qi)
        lane_h = lax.broadcasted_iota(jnp.int32, (sub, HEAD_PAD), 1)
        for b in range(2):
            rows, nk = slice(b * sub, (b + 1) * sub), (b + 1) * sub
            s = lax.dot_general(q[rows], k[0:nk], NT, preferred_element_type=F32)
            r = lax.broadcasted_iota(jnp.int32, (sub, nk), 0) + b * sub
            c = lax.broadcasted_iota(jnp.int32, (sub, nk), 1)
            m, acc = update((m_all[rows], acc_all[rows]), jnp.where(c <= r, s, NEG), ones_v[0:nk])
            l = acc[:, 0:1]
            o_ref[rows, :] = jnp.where(lane_h >= QK_NOPE, acc / l, 0.0).astype(o_ref.dtype)
            lse_ref[rows, :] = m + jnp.log2(l)

    return pl.pallas_call(
        body,
        grid=(MLA_HEADS, nq),
        in_specs=[
            pl.BlockSpec((tq, HEAD_PAD), lambda h, i: (i, h)),
            pl.BlockSpec((t, HEAD_PAD), lambda h, i: (0, h)),
            pl.BlockSpec((t, HEAD_PAD), lambda h, i: (0, 0)),
        ],
        out_specs=[
            pl.BlockSpec((tq, HEAD_PAD), lambda h, i: (i, h)),
            pl.BlockSpec((None, tq, 1), lambda h, i: (h, i, 0)),
        ],
        out_shape=[
            jax.ShapeDtypeStruct((t, MLA_HEADS * HEAD_PAD), MXU_DT),
            jax.ShapeDtypeStruct((MLA_HEADS, t, 1), F32),
        ],
        name=name,
        compiler_params=_cparams("parallel", "parallel"),
    )(qs, kv, kpe)


def _flash_bwd(qs, kv, kpe, o, do, lse, name):
    t = qs.shape[0]
    ta = _rows(t, TA)
    tq = ta
    nq = t // ta
    sub = ta // 2

    def body(q_ref, o_ref, do_ref, lse_ref, kv_ref, kpe_ref, dq_ref, dkv_ref, dkpe_ref, dk_acc, dv_acc):
        kj = pl.program_id(1)

        @pl.when(kj == 0)
        def _():
            dq_ref[...] = jnp.zeros_like(dq_ref)

        lane = lax.broadcasted_iota(jnp.int32, (ta, HEAD_PAD), 1)
        kvb = kv_ref[...]
        k = jnp.where(lane < QK_NOPE, kvb, kpe_ref[...])
        dk_acc[...] = jnp.zeros_like(dk_acc)
        dv_acc[...] = jnp.zeros_like(dv_acc)

        def tile(row0, nrows, nkeys, diagonal):
            rows = pl.ds(pl.multiple_of(row0, sub), nrows)
            keys = slice(0, nkeys)
            q = q_ref[rows, :]
            dob = do_ref[rows, :]
            delta = jnp.sum(dob.astype(F32) * o_ref[rows, :].astype(F32), axis=-1, keepdims=True)
            s = lax.dot_general(q, k[keys], NT, preferred_element_type=F32)
            if diagonal:
                r = lax.broadcasted_iota(jnp.int32, (nrows, nkeys), 0) + (nkeys - nrows)
                c = lax.broadcasted_iota(jnp.int32, (nrows, nkeys), 1)
                s = jnp.where(c <= r, s, NEG)
            p = jnp.exp2(s - lse_ref[rows, :])
            dp = lax.dot_general(dob, kvb[keys], NT, preferred_element_type=F32)
            ds = (p * (dp - delta)).astype(MXU_DT)
            dq_ref[rows, :] += jnp.dot(ds, k[keys], preferred_element_type=F32)
            dk_acc[keys, :] += lax.dot_general(ds, q, TN_DIMS, preferred_element_type=F32)
            dv_acc[keys, :] += lax.dot_general(p.astype(MXU_DT), dob, TN_DIMS, preferred_element_type=F32)

        tile(kj * ta, sub, sub, True)
        tile(kj * ta + sub, sub, ta, True)

        def step(qq, carry):
            tile(qq * tq, tq, ta, False)
            return carry

        lax.fori_loop(kj + 1, t // tq, step, 0)
        dk = dk_acc[...] * (1.0 / LOG2E)
        dkv_ref[...] = jnp.where(lane < QK_NOPE, dk, dv_acc[...]).astype(dkv_ref.dtype)
        dkpe_ref[...] = jnp.where((lane >= KPE_LANE) & (lane < KPE_LANE + QK_ROPE), dk, 0.0)

    head_rows = pl.BlockSpec((t, HEAD_PAD), lambda h, j: (0, h))
    return pl.pallas_call(
        body,
        grid=(MLA_HEADS, nq),
        in_specs=[
            head_rows,
            head_rows,
            head_rows,
            pl.BlockSpec((None, t, 1), lambda h, j: (h, 0, 0)),
            pl.BlockSpec((ta, HEAD_PAD), lambda h, j: (j, h)),
            pl.BlockSpec((ta, HEAD_PAD), lambda h, j: (j, 0)),
        ],
        out_specs=[
            head_rows,
            pl.BlockSpec((ta, HEAD_PAD), lambda h, j: (j, h)),
            pl.BlockSpec((None, ta, HEAD_PAD), lambda h, j: (h, j, 0)),
        ],
        out_shape=[
            jax.ShapeDtypeStruct((t, MLA_HEADS * HEAD_PAD), F32),
            jax.ShapeDtypeStruct((t, MLA_HEADS * HEAD_PAD), MXU_DT),
            jax.ShapeDtypeStruct((MLA_HEADS, t, HEAD_PAD), F32),
        ],
        scratch_shapes=[pltpu.VMEM((ta, HEAD_PAD), F32), pltpu.VMEM((ta, HEAD_PAD), F32)],
        name=name,
        compiler_params=_cparams("parallel", "arbitrary"),
    )(qs, o, do, lse, kv, kpe)


def _as2d(a):
    if a.ndim == 1:
        return a.reshape(1, a.shape[0])
    return a.reshape(-1, a.shape[-1])


def _adamw(w, g, m, v, name):
    shape = w.shape
    w2, g2, m2, v2 = (_as2d(a) for a in (w, g, m, v))
    r, c = w2.shape
    tr = _tile_rows(r, c)
    c1 = 1.0 - ADAM_B1 ** ADAM_STEP
    c2 = 1.0 - ADAM_B2 ** ADAM_STEP

    def body(w_ref, g_ref, m_ref, v_ref, d_ref, nm_ref, nv_ref):
        gv = g_ref[...]
        nm = ADAM_B1 * m_ref[...] + (1.0 - ADAM_B1) * gv
        nv = ADAM_B2 * v_ref[...] + (1.0 - ADAM_B2) * (gv * gv)
        d_ref[...] = -ADAM_LR * ((nm / c1) / (jnp.sqrt(nv / c2) + ADAM_EPS) + ADAM_WD * w_ref[...])
        nm_ref[...] = nm
        nv_ref[...] = nv

    blk = pl.BlockSpec((tr, c), lambda i: (i, 0))
    outs = pl.pallas_call(
        body,
        grid=(r // tr,),
        in_specs=[blk] * 4,
        out_specs=[blk] * 3,
        out_shape=[jax.ShapeDtypeStruct((r, c), F32)] * 3,
        name=name,
        compiler_params=_cparams("parallel"),
    )(w2, g2, m2, v2)
    return tuple(o.reshape(shape) for o in outs)


def _adamw_halves(w, mine, other, m, v, c_idx, name):
    nl, r, c = w.shape
    h = nl // 2
    tr = _tile_rows(r, 2 * c)
    c1 = 1.0 - ADAM_B1 ** ADAM_STEP
    c2 = 1.0 - ADAM_B2 ** ADAM_STEP

    def body(c_ref, w_ref, a_ref, b_ref, m_ref, v_ref, g_ref, d_ref, nm_ref, nv_ref):
        l = pl.program_id(0)
        gv = jnp.where(l // h == c_ref[0], a_ref[...], b_ref[...])
        nm = ADAM_B1 * m_ref[...] + (1.0 - ADAM_B1) * gv
        nv = ADAM_B2 * v_ref[...] + (1.0 - ADAM_B2) * (gv * gv)
        g_ref[...] = gv
        d_ref[...] = -ADAM_LR * ((nm / c1) / (jnp.sqrt(nv / c2) + ADAM_EPS) + ADAM_WD * w_ref[...])
        nm_ref[...] = nm
        nv_ref[...] = nv

    def half_map(mine_side):
        def index(l, i, cr):
            first = cr[0] * h if mine_side else (1 - cr[0]) * h
            return (jnp.clip(l - first, 0, h - 1), i, 0)
        return index

    full = pl.BlockSpec((None, tr, c), lambda l, i, cr: (l, i, 0))
    grid_spec = pltpu.PrefetchScalarGridSpec(
        num_scalar_prefetch=1,
        grid=(nl, r // tr),
        in_specs=[full, pl.BlockSpec((None, tr, c), half_map(True)), pl.BlockSpec((None, tr, c), half_map(False)), full, full],
        out_specs=[full] * 4,
    )
    return pl.pallas_call(
        body,
        grid_spec=grid_spec,
        out_shape=[jax.ShapeDtypeStruct((nl, r, c), F32)] * 4,
        name=name,
        compiler_params=_cparams("parallel", "parallel"),
    )(c_idx, w, mine, other, m, v)


def _tile_rows(r, c, mult=SUBLANE):
    limit = max(mult, (BLOCK_BYTES // 4) // (4 * c))
    if r <= limit:
        return r
    t = (limit // mult) * mult
    while t >= mult:
        if r % t == 0:
            return t
        t -= mult
    return r


def _sum_leading(a, name):
    n, r, c = a.shape
    tr = _tile_rows(r, c * n)

    def body(a_ref, o_ref):
        s = a_ref[0]
        for k in range(1, n):
            s = s + a_ref[k]
        o_ref[...] = s

    return pl.pallas_call(
        body,
        grid=(r // tr,),
        in_specs=[pl.BlockSpec((n, tr, c), lambda i: (0, i, 0))],
        out_specs=pl.BlockSpec((tr, c), lambda i: (i, 0)),
        out_shape=jax.ShapeDtypeStruct((r, c), F32),
        name=name,
        compiler_params=_cparams("parallel"),
    )(a)


def _add_half(g, s, c_idx, name):
    nl, r, c = g.shape
    h = nl // 2
    tr = _tile_rows(r, c, 2 * SUBLANE)

    def body(c_ref, g_ref, s_ref, o_ref):
        o_ref[...] = (g_ref[...] + s_ref[...]).astype(o_ref.dtype)

    grid_spec = pltpu.PrefetchScalarGridSpec(
        num_scalar_prefetch=1,
        grid=(h, r // tr),
        in_specs=[
            pl.BlockSpec((None, tr, c), lambda l, i, cr: (cr[0] * h + l, i, 0)),
            pl.BlockSpec((None, tr, c), lambda l, i, cr: (l, i, 0)),
        ],
        out_specs=pl.BlockSpec((None, tr, c), lambda l, i, cr: (l, i, 0)),
    )
    return pl.pallas_call(
        body,
        grid_spec=grid_spec,
        out_shape=jax.ShapeDtypeStruct((h, r, c), XFER_DT),
        name=name,
        compiler_params=_cparams("parallel", "parallel"),
    )(c_idx, g, s)


def _sum_chips(slots, pair, chip_idx, kind, name):
    _, h, r, c = slots.shape
    tr = _tile_rows(r, c, 2 * SUBLANE)
    nr = r // tr

    def body(chip_ref, s_ref, own_ref, o_ref):
        chip = chip_ref[0]
        own = own_ref[...].astype(F32)
        parts = [s_ref[j].astype(F32) for j in range(3)]
        total = None
        for k in range(4):
            d = jnp.bitwise_xor(chip, k)
            v = jnp.where(d == 0, own, jnp.where(d == 2, parts[0], jnp.where(d == 1, parts[1], parts[2])))
            total = v if total is None else total + v
        o_ref[...] = total

    if kind == "row":
        own_spec = pl.BlockSpec((None, tr, c), lambda l, i, cr: (l, cr[0] * nr + i, 0))
    else:
        own_spec = pl.BlockSpec((None, tr, c), lambda l, i, cr: (l, i, cr[0]))
    grid_spec = pltpu.PrefetchScalarGridSpec(
        num_scalar_prefetch=1,
        grid=(h, nr),
        in_specs=[pl.BlockSpec((3, None, tr, c), lambda l, i, cr: (0, l, i, 0)), own_spec],
        out_specs=pl.BlockSpec((None, tr, c), lambda l, i, cr: (l, i, 0)),
    )
    return pl.pallas_call(
        body,
        grid_spec=grid_spec,
        out_shape=jax.ShapeDtypeStruct((h, r, c), F32),
        name=name,
        compiler_params=_cparams("parallel", "parallel"),
    )(chip_idx, slots, pair)


def _mesh_pos():
    return lax.axis_index("x"), lax.axis_index("y"), lax.axis_index("c")


def _other_chips(x, y):
    return [(1 - x, y), (x, 1 - y), (1 - x, 1 - y)]


def _all_gather_rows(block, name):
    m_per, n = block.shape

    def body(x_ref, out_ref, send_sems, recv_sems, local_sem):
        x, y, c = _mesh_pos()
        me, sibling = (x, y, c), (x, y, 1 - c)
        chips = _other_chips(x, y)

        def rows(px, py, pc):
            return out_ref.at[pl.ds((4 * px + 2 * py + pc) * m_per, m_per), :]

        def copy(k, blk, to, src=None):
            return pltpu.make_async_remote_copy(
                src_ref=rows(*blk) if src is None else src,
                dst_ref=rows(*blk),
                send_sem=send_sems.at[k],
                recv_sem=recv_sems.at[k],
                device_id=to,
                device_id_type=MESH_ID,
            )

        mine = pltpu.make_async_copy(x_ref, rows(*me), local_sem)
        mine.start()
        first = [copy(0, me, sibling, src=x_ref)]
        first += [copy(1 + j, me, (*chip, c), src=x_ref) for j, chip in enumerate(chips)]
        for cp in first:
            cp.start()
        passed = [copy(4 + j, (*chip, c), sibling) for j, chip in enumerate(chips)]
        for j, chip in enumerate(chips):
            copy(1 + j, (*chip, c), me).wait_recv()
            passed[j].start()
        copy(0, sibling, me).wait_recv()
        for j, chip in enumerate(chips):
            copy(4 + j, (*chip, 1 - c), me).wait_recv()
        for cp in first + passed:
            cp.wait_send()
        mine.wait()

    return pl.pallas_call(
        body,
        out_shape=jax.ShapeDtypeStruct((8 * m_per, n), block.dtype),
        in_specs=[pl.BlockSpec(memory_space=pltpu.VMEM)],
        out_specs=pl.BlockSpec(memory_space=pltpu.VMEM),
        scratch_shapes=[pltpu.SemaphoreType.DMA((7,)), pltpu.SemaphoreType.DMA((7,)), pltpu.SemaphoreType.DMA],
        name=name,
        compiler_params=pltpu.CompilerParams(vmem_limit_bytes=VMEM_LIMIT),
    )(block)


def _shard_window(ref, layers, chip, rows, cols):
    if rows is not None:
        return ref.at[layers, pl.ds(pl.multiple_of(chip * rows, rows), rows), :]
    return ref.at[layers, :, pl.ds(pl.multiple_of(chip * cols, cols), cols)]


def _all_gather_weights(shards, kinds, name):
    nw = len(shards)
    out_shapes = []
    for s, kind in zip(shards, kinds):
        nl, r, c = s.shape
        full = (nl, 4 * r, c) if kind == "row" else (nl, r, 4 * c)
        out_shapes.append(jax.ShapeDtypeStruct(full, s.dtype))

    def body(*refs):
        ins, outs = refs[:nw], refs[nw : 2 * nw]
        send_sems, recv_sems, in_sems, out_sems = refs[2 * nw : 2 * nw + 4]
        bufs = refs[2 * nw + 4 :]
        x, y, c = _mesh_pos()
        sibling = (x, y, 1 - c)
        chips = _other_chips(x, y)
        my_chip = 2 * x + y

        def window(w, chip, layers):
            _, r, cc = shards[w].shape
            if kinds[w] == "row":
                return _shard_window(outs[w], layers, chip, r, None)
            return _shard_window(outs[w], layers, chip, None, cc)

        def half(w, half_idx):
            h = shards[w].shape[0] // 2
            return pl.ds(half_idx * h, h)

        def copy(w, k, src, dst, to):
            return pltpu.make_async_remote_copy(
                src_ref=src, dst_ref=dst, send_sem=send_sems.at[w, k], recv_sem=recv_sems.at[w, k],
                device_id=to, device_id_type=MESH_ID)

        sent = []
        for w in range(nw):
            mine = ins[w].at[half(w, c)]
            for j, chip in enumerate(chips):
                cp = copy(w, j, mine, window(w, my_chip, half(w, c)), (*chip, c))
                cp.start()
                sent.append(cp)
        for w in range(nw):
            nl = shards[w].shape[0]

            def load(l, w=w):
                return pltpu.make_async_copy(ins[w].at[l], bufs[w].at[l % 2], in_sems.at[w, l % 2])

            def store(l, w=w):
                return pltpu.make_async_copy(bufs[w].at[l % 2], window(w, my_chip, l), out_sems.at[w, l % 2])

            load(0).start()
            for l in range(nl):
                load(l).wait()
                store(l).start()
                if l + 1 < nl:
                    if l >= 1:
                        store(l - 1).wait()
                    load(l + 1).start()
            for l in range(max(nl - 2, 0), nl):
                store(l).wait()
        for w in range(nw):
            for j, (cx, cy) in enumerate(chips):
                got = window(w, 2 * cx + cy, half(w, c))
                copy(w, j, got, got, (cx, cy, c)).wait_recv()
                cp = copy(w, 3 + j, got, got, sibling)
                cp.start()
                sent.append(cp)
        for w in range(nw):
            for j, (cx, cy) in enumerate(chips):
                got = window(w, 2 * cx + cy, half(w, 1 - c))
                copy(w, 3 + j, got, got, sibling).wait_recv()
        for cp in sent:
            cp.wait_send()

    anyspec = pl.BlockSpec(memory_space=pl.ANY)
    return pl.pallas_call(
        body,
        out_shape=out_shapes,
        in_specs=[anyspec] * nw,
        out_specs=[anyspec] * nw,
        scratch_shapes=[pltpu.SemaphoreType.DMA((nw, 6)), pltpu.SemaphoreType.DMA((nw, 6)),
                        pltpu.SemaphoreType.DMA((nw, 2)), pltpu.SemaphoreType.DMA((nw, 2))]
        + [pltpu.VMEM((2,) + s.shape[1:], s.dtype) for s in shards],
        name=name,
        compiler_params=pltpu.CompilerParams(vmem_limit_bytes=VMEM_LIMIT),
    )(*shards)


def _exchange_halves(grads, name):
    nw = len(grads)
    out_shapes = [jax.ShapeDtypeStruct((g.shape[0] // 2,) + g.shape[1:], g.dtype) for g in grads]

    def body(*refs):
        ins, outs = refs[:nw], refs[nw : 2 * nw]
        send_sems, recv_sems = refs[2 * nw :]
        x, y, c = _mesh_pos()
        cps = []
        for w in range(nw):
            h = grads[w].shape[0] // 2
            cp = pltpu.make_async_remote_copy(
                src_ref=ins[w].at[pl.ds((1 - c) * h, h)], dst_ref=outs[w], send_sem=send_sems.at[w],
                recv_sem=recv_sems.at[w], device_id=(x, y, 1 - c), device_id_type=MESH_ID)
            cp.start()
            cps.append(cp)
        for cp in cps:
            cp.wait()

    anyspec = pl.BlockSpec(memory_space=pl.ANY)
    return pl.pallas_call(
        body,
        out_shape=out_shapes,
        in_specs=[anyspec] * nw,
        out_specs=[anyspec] * nw,
        scratch_shapes=[pltpu.SemaphoreType.DMA((nw,)), pltpu.SemaphoreType.DMA((nw,))],
        name=name,
    )(*grads)


def _scatter_to_chips(parts, kinds, name):
    nw = len(parts)
    shard_shapes = []
    for p, kind in zip(parts, kinds):
        h, r, c = p.shape
        shard_shapes.append((h, r // 4, c) if kind == "row" else (h, r, c // 4))
    out_shapes = [jax.ShapeDtypeStruct((3,) + s, p.dtype) for s, p in zip(shard_shapes, parts)]

    def body(*refs):
        ins, outs = refs[:nw], refs[nw : 2 * nw]
        send_sems, recv_sems = refs[2 * nw :]
        x, y, c = _mesh_pos()
        chips = _other_chips(x, y)

        def piece(w, chip):
            h, r, cc = shard_shapes[w]
            if kinds[w] == "row":
                return _shard_window(ins[w], pl.ds(0, h), chip, r, None)
            return _shard_window(ins[w], pl.ds(0, h), chip, None, cc)

        def copy(w, j, cx, cy):
            return pltpu.make_async_remote_copy(
                src_ref=piece(w, 2 * cx + cy), dst_ref=outs[w].at[j], send_sem=send_sems.at[w, j],
                recv_sem=recv_sems.at[w, j], device_id=(cx, cy, c), device_id_type=MESH_ID)

        cps = [copy(w, j, cx, cy) for w in range(nw) for j, (cx, cy) in enumerate(chips)]
        for cp in cps:
            cp.start()
        for cp in cps:
            cp.wait()

    anyspec = pl.BlockSpec(memory_space=pl.ANY)
    return pl.pallas_call(
        body,
        out_shape=out_shapes,
        in_specs=[anyspec] * nw,
        out_specs=[anyspec] * nw,
        scratch_shapes=[pltpu.SemaphoreType.DMA((nw, 3)), pltpu.SemaphoreType.DMA((nw, 3))],
        name=name,
    )(*parts)


def _swap_halves(halves, name):
    nw = len(halves)
    out_shapes = [jax.ShapeDtypeStruct(p.shape, p.dtype) for p in halves]

    def body(*refs):
        ins, outs = refs[:nw], refs[nw : 2 * nw]
        send_sems, recv_sems = refs[2 * nw :]
        x, y, c = _mesh_pos()
        cps = [pltpu.make_async_remote_copy(
            src_ref=ins[w], dst_ref=outs[w], send_sem=send_sems.at[w], recv_sem=recv_sems.at[w],
            device_id=(x, y, 1 - c), device_id_type=MESH_ID) for w in range(nw)]
        for cp in cps:
            cp.start()
        for cp in cps:
            cp.wait()

    anyspec = pl.BlockSpec(memory_space=pl.ANY)
    return pl.pallas_call(
        body,
        out_shape=out_shapes,
        in_specs=[anyspec] * nw,
        out_specs=[anyspec] * nw,
        scratch_shapes=[pltpu.SemaphoreType.DMA((nw,)), pltpu.SemaphoreType.DMA((nw,))],
        name=name,
    )(*halves)


def _pad_wdq(w):
    z = lambda n: jnp.zeros((w.shape[0], n), w.dtype)
    base = Q_LORA + KV_LORA
    return jnp.concatenate([w[:, :base], z(KPE_LANE), w[:, base:], z(HEAD_PAD - KPE_LANE - QK_ROPE)], axis=1)


def _unpad_wdq(g):
    base = Q_LORA + KV_LORA
    return jnp.concatenate([g[:, :base], g[:, base + KPE_LANE : base + KPE_LANE + QK_ROPE]], axis=1)


def _pad_wuq(w):
    w3 = w.reshape(Q_LORA, MLA_HEADS, QK_NOPE + QK_ROPE)
    w3 = jnp.pad(w3, ((0, 0), (0, 0), (0, HEAD_PAD - QK_NOPE - QK_ROPE)))
    return w3.reshape(Q_LORA, MLA_HEADS * HEAD_PAD)


def _unpad_wuq(g):
    g3 = g.reshape(Q_LORA, MLA_HEADS, HEAD_PAD)[:, :, : QK_NOPE + QK_ROPE]
    return g3.reshape(Q_LORA, MLA_HEADS * (QK_NOPE + QK_ROPE))


def _pad_wo(w):
    w3 = w.reshape(MLA_HEADS, V_HEAD, D_MODEL)
    w3 = jnp.pad(w3, ((0, 0), (HEAD_PAD - V_HEAD, 0), (0, 0)))
    return w3.reshape(MLA_HEADS * HEAD_PAD, D_MODEL)


def _unpad_wo(g):
    g3 = g.reshape(MLA_HEADS, HEAD_PAD, D_MODEL)[:, HEAD_PAD - V_HEAD :, :]
    return g3.reshape(MLA_HEADS * V_HEAD, D_MODEL)


def _local_step(x, mem, positions, target, wb, ws):
    t = x.shape[0]
    tables = _rope_tables(positions.reshape(t, 1), "rope_tables")
    saved = []
    h1 = _rms_fwd(x, ws["norm_mix_g"][0], "l0_norm_mix")
    for l in range(DEPTH):
        s = {"x0": x}
        s["h1"] = h1
        if l % 2 == 0:
            e = l // 2
            z = _matmul(h1, wb["pc_w_in"], "nn", F32, f"l{l}_pc_in", layer=e)
            cat = _mix_fwd(z, ws["pool_w"][e], ws["pool_scale"][e], ws["conv_dw_w"][e], ws["conv_dw_b"][e],
                           ws["conv_ln_g"][e], ws["conv_ln_b"][e], f"l{l}_mix")
            x, h2 = _matmul_res_norm(cat, wb["pc_w_out"], e, x, ws["norm_xa_g"][l], f"l{l}_pc_out")
            s.update(z=z, cat=cat)
        else:
            o = l // 2
            cp = _matmul(h1, wb["mla_wdq"], "nn", F32, f"l{l}_mla_dq", layer=o)
            qn, kvn, kpe = _mla_prep(cp, ws["mla_q_norm_g"][o], ws["mla_kv_norm_g"][o], tables, f"l{l}_mla_prep")
            qr = _matmul_rope(qn, wb["mla_wuq"], o, tables, MLA_SCALE * LOG2E, f"l{l}_mla_uq")
            kv = _matmul(kvn, wb["mla_w_ukv"], "nn", MXU_DT, f"l{l}_mla_ukv", layer=o)
            att, lse = _flash_fwd(qr, kv, kpe, f"l{l}_mla_attn")
            x, h2 = _matmul_res_norm(att, wb["mla_wo"], o, x, ws["norm_xa_g"][l], f"l{l}_mla_o")
            s.update(cp=cp, qn=qn, kvn=kvn, kpe=kpe, qr=qr, kv=kv, att=att, lse=lse)
        s["x1"] = x
        hm = _rms_fwd(mem, ws["norm_mem_g"][l], f"l{l}_norm_mem")
        q2 = _matmul(h2, wb["xa_wq"], "nn", MXU_DT, f"l{l}_xa_q", layer=l)
        kvm = _matmul(hm, wb["xa_wkv"], "nn", MXU_DT, f"l{l}_xa_kv", layer=l)
        o2 = _xa_fwd(q2, kvm, f"l{l}_xa_attn")
        x, h3 = _matmul_res_norm(o2, wb["xa_wo"], l, x, ws["norm_ffn_g"][l], f"l{l}_xa_o")
        s.update(h2=h2, hm=hm, q2=q2, kvm=kvm, o2=o2, x2=x)
        up = _matmul(h3, wb["ffn_w_up"], "nn", F32, f"l{l}_ffn_up", layer=l)
        act = _ffn_fwd(up, ws["ffn_conv_w"][l], ws["ffn_conv_b"][l], f"l{l}_ffn_mid")
        if l + 1 < DEPTH:
            x, h1 = _matmul_res_norm(act, wb["ffn_w_down"], l, x, ws["norm_mix_g"][l + 1], f"l{l}_ffn_down")
        else:
            x = _matmul(act, wb["ffn_w_down"], "nn", F32, f"l{l}_ffn_down", layer=l, res=x)
        s.update(h3=h3, up=up, act=act)
        saved.append(s)

    dx, dx16, dg_final, loss = _loss_head(x, target, ws["final_norm_g"], "loss_head")
    g = {k: [None] * DEPTH for k in ("norm_mix_g", "norm_xa_g", "norm_mem_g", "xa_wq", "xa_wkv", "xa_wo", "norm_ffn_g",
                                      "ffn_w_up", "ffn_conv_w", "ffn_conv_b", "ffn_w_down")}
    g.update({k: [None] * (DEPTH // 2) for k in ("pc_w_in", "pool_w", "pool_scale", "conv_dw_w", "conv_dw_b", "conv_ln_g",
                                                 "conv_ln_b", "pc_w_out", "mla_w_dq_dkv", "mla_q_norm_g", "mla_w_uq",
                                                 "mla_kv_norm_g", "mla_w_ukv", "mla_w_o")})
    stk = {k: None for k in ("xa_wq", "xa_wkv", "xa_wo", "ffn_w_up", "ffn_w_down", "pc_w_in", "pc_w_out")}
    for l in reversed(range(DEPTH)):
        s = saved[l]
        dact = _matmul(dx16, wb["ffn_w_down"], "nt", F32, f"l{l}_b_ffn_dact", layer=l)
        stk["ffn_w_down"] = _matmul(s["act"], dx16, "tn", F32, f"l{l}_b_ffn_dwdown", stack=(stk["ffn_w_down"], l, DEPTH))
        dup, dcw, dcb = _ffn_bwd(s["up"], dact, ws["ffn_conv_w"][l], ws["ffn_conv_b"][l], f"l{l}_b_ffn_mid")
        g["ffn_conv_w"][l], g["ffn_conv_b"][l] = dcw, dcb[0]
        stk["ffn_w_up"] = _matmul(s["h3"], dup, "tn", F32, f"l{l}_b_ffn_dwup", stack=(stk["ffn_w_up"], l, DEPTH))
        dx, dx16, dg = _matmul_rms_bwd(dup, wb["ffn_w_up"], l, s["x2"], ws["norm_ffn_g"][l], dx, f"l{l}_b_ffn_dh")
        g["norm_ffn_g"][l] = dg[0]
        do2 = _matmul(dx16, wb["xa_wo"], "nt", MXU_DT, f"l{l}_b_xa_do", layer=l)
        stk["xa_wo"] = _matmul(s["o2"], dx16, "tn", F32, f"l{l}_b_xa_dwo", stack=(stk["xa_wo"], l, DEPTH))
        dq2, dkvm = _xa_bwd(s["q2"], s["kvm"], do2, f"l{l}_b_xa_attn")
        stk["xa_wq"] = _matmul(s["h2"], dq2, "tn", F32, f"l{l}_b_xa_dwq", stack=(stk["xa_wq"], l, DEPTH))
        stk["xa_wkv"] = _matmul(s["hm"], dkvm, "tn", F32, f"l{l}_b_xa_dwkv", stack=(stk["xa_wkv"], l, DEPTH))
        dhm = _matmul(dkvm, wb["xa_wkv"], "nt", F32, f"l{l}_b_xa_dhm", layer=l)
        g["norm_mem_g"][l] = _rms_bwd_gain(dhm, mem, ws["norm_mem_g"][l], f"l{l}_b_norm_mem")[0]
        dx, dx16, dg = _matmul_rms_bwd(dq2, wb["xa_wq"], l, s["x1"], ws["norm_xa_g"][l], dx, f"l{l}_b_xa_dh")
        g["norm_xa_g"][l] = dg[0]
        if l % 2 == 0:
            e = l // 2
            dcat = _matmul(dx16, wb["pc_w_out"], "nt", F32, f"l{l}_b_pc_dcat", layer=e)
            stk["pc_w_out"] = _matmul(s["cat"], dx16, "tn", F32, f"l{l}_b_pc_dwout", stack=(stk["pc_w_out"], e, DEPTH // 2))
            dz, dpw, dps, dww, dwb, dlg, dlb = _mix_bwd(
                s["z"], dcat, ws["pool_w"][e], ws["pool_scale"][e], ws["conv_dw_w"][e], ws["conv_dw_b"][e],
                ws["conv_ln_g"][e], ws["conv_ln_b"][e], f"l{l}_b_mix")
            g["pool_w"][e], g["pool_scale"][e], g["conv_dw_w"][e] = dpw, dps[0], dww
            g["conv_dw_b"][e], g["conv_ln_g"][e], g["conv_ln_b"][e] = dwb[0], dlg[0], dlb[0]
            stk["pc_w_in"] = _matmul(s["h1"], dz, "tn", F32, f"l{l}_b_pc_dwin", stack=(stk["pc_w_in"], e, DEPTH // 2))
            dx, dx16, dg = _matmul_rms_bwd(dz, wb["pc_w_in"], e, s["x0"], ws["norm_mix_g"][l], dx, f"l{l}_b_pc_dh")
        else:
            o = l // 2
            do = _matmul(dx16, wb["mla_wo"], "nt", MXU_DT, f"l{l}_b_mla_do", layer=o)
            g["mla_w_o"][o] = _unpad_wo(_matmul(s["att"], dx16, "tn", F32, f"l{l}_b_mla_dwo"))
            dqr, dkv, dkpe = _flash_bwd(s["qr"], s["kv"], s["kpe"], s["att"], do, s["lse"], f"l{l}_b_mla_attn")
            dq = _rope_heads(dqr, tables, -1.0, MLA_SCALE, f"l{l}_b_mla_rope")
            g["mla_w_uq"][o] = _unpad_wuq(_matmul(s["qn"], dq, "tn", F32, f"l{l}_b_mla_dwuq"))
            dqn = _matmul(dq, wb["mla_wuq"], "nt", F32, f"l{l}_b_mla_dqn", layer=o)
            g["mla_w_ukv"][o] = _matmul(s["kvn"], dkv, "tn", F32, f"l{l}_b_mla_dwukv")
            dkvn = _matmul(dkv, wb["mla_w_ukv"], "nt", F32, f"l{l}_b_mla_dkvn", layer=o)
            dcp, dqg, dkvg = _mla_prep_bwd(s["cp"], dqn, dkvn, dkpe, ws["mla_q_norm_g"][o], ws["mla_kv_norm_g"][o],
                                           tables, f"l{l}_b_mla_prep")
            g["mla_q_norm_g"][o], g["mla_kv_norm_g"][o] = dqg[0], dkvg[0]
            g["mla_w_dq_dkv"][o] = _unpad_wdq(_matmul(s["h1"], dcp, "tn", F32, f"l{l}_b_mla_dwdq"))
            dx, dx16, dg = _matmul_rms_bwd(dcp, wb["mla_wdq"], o, s["x0"], ws["norm_mix_g"][l], dx, f"l{l}_b_mla_dh")
        g["norm_mix_g"][l] = dg[0]
    grads = {k: jnp.stack(v) for k, v in g.items() if k not in stk}
    grads.update(stk)
    grads["final_norm_g"] = dg_final[0]
    return loss, dx, grads


BIG = (
    ("xa_wq", "row"), ("xa_wkv", "col"), ("xa_wo", "row"), ("ffn_w_up", "col"), ("ffn_w_down", "row"),
    ("pc_w_in", "col"), ("pc_w_out", "row"), ("mla_w_dq_dkv", "row"), ("mla_w_uq", "col"), ("mla_w_ukv", "col"),
    ("mla_w_o", "row"),
)
SMALL_SHARDED = ("ffn_conv_w", "conv_dw_w", "mla_q_norm_g", "mla_kv_norm_g")
SMALL_REPLICATED = ("norm_mix_g", "norm_xa_g", "norm_mem_g", "norm_ffn_g", "ffn_conv_b", "pool_w", "pool_scale",
                    "conv_dw_b", "conv_ln_g", "conv_ln_b", "final_norm_g")
WEIGHTS = ("norm_mix_g", "norm_xa_g", "norm_mem_g", "xa_wq", "xa_wkv", "xa_wo", "norm_ffn_g", "ffn_w_up", "ffn_conv_w",
           "ffn_conv_b", "ffn_w_down", "pc_w_in", "pool_w", "pool_scale", "conv_dw_w", "conv_dw_b", "conv_ln_g",
           "conv_ln_b", "pc_w_out", "mla_w_dq_dkv", "mla_q_norm_g", "mla_w_uq", "mla_kv_norm_g", "mla_w_ukv", "mla_w_o",
           "final_norm_g")
PACK_ROW = SUBLANE * LANE


def _pack(arrays):
    flat = jnp.concatenate([a.reshape(-1) for a in arrays])
    n = flat.shape[0]
    pad = (-n) % PACK_ROW
    return jnp.pad(flat, (0, pad)).reshape(-1, LANE)


def _unpack(flat, shapes):
    out, off = [], 0
    for s in shapes:
        n = int(np.prod(s))
        out.append(flat[off : off + n].reshape(s))
        off += n
    return out


def kernel(x, mem, positions, norm_mix_g, norm_xa_g, norm_mem_g, xa_wq, xa_wkv, xa_wo, norm_ffn_g, ffn_w_up, ffn_conv_w, ffn_conv_b, ffn_w_down, pc_w_in, pool_w, pool_scale, conv_dw_w, conv_dw_b, conv_ln_g, conv_ln_b, pc_w_out, mla_w_dq_dkv, mla_q_norm_g, mla_w_uq, mla_kv_norm_g, mla_w_ukv, mla_w_o, final_norm_g, loss_target, m_norm_mix_g, m_norm_xa_g, m_norm_mem_g, m_xa_wq, m_xa_wkv, m_xa_wo, m_norm_ffn_g, m_ffn_w_up, m_ffn_conv_w, m_ffn_conv_b, m_ffn_w_down, m_pc_w_in, m_pool_w, m_pool_scale, m_conv_dw_w, m_conv_dw_b, m_conv_ln_g, m_conv_ln_b, m_pc_w_out, m_mla_w_dq_dkv, m_mla_q_norm_g, m_mla_w_uq, m_mla_kv_norm_g, m_mla_w_ukv, m_mla_w_o, m_final_norm_g, v_norm_mix_g, v_norm_xa_g, v_norm_mem_g, v_xa_wq, v_xa_wkv, v_xa_wo, v_norm_ffn_g, v_ffn_w_up, v_ffn_conv_w, v_ffn_conv_b, v_ffn_w_down, v_pc_w_in, v_pool_w, v_pool_scale, v_conv_dw_w, v_conv_dw_b, v_conv_ln_g, v_conv_ln_b, v_pc_w_out, v_mla_w_dq_dkv, v_mla_q_norm_g, v_mla_w_uq, v_mla_kv_norm_g, v_mla_w_ukv, v_mla_w_o, v_final_norm_g):
    args = dict(locals())
    w = {n: args[n] for n in WEIGHTS}
    m = {n: args["m_" + n] for n in WEIGHTS}
    v = {n: args["v_" + n] for n in WEIGHTS}
    cx, cy, cc = lax.axis_index("x"), lax.axis_index("y"), lax.axis_index("c")
    chip = 2 * cx + cy

    full = _all_gather_weights([w[n].astype(MXU_DT) for n, _ in BIG], [k for _, k in BIG], "gather_weights")
    full = dict(zip([n for n, _ in BIG], full))
    small_shapes = [w[n].shape for n in SMALL_SHARDED]
    gathered = _all_gather_rows(_pack([w[n] for n in SMALL_SHARDED]), "gather_small")
    gathered = gathered.reshape(8, -1)
    ws = {n: w[n] for n in SMALL_REPLICATED}
    pieces = [_unpack(gathered[2 * k], small_shapes) for k in range(4)]
    for i, n in enumerate(SMALL_SHARDED):
        ws[n] = jnp.concatenate([pieces[k][i] for k in range(4)], axis=-1)
    wb = {n: full[n] for n in ("xa_wq", "xa_wkv", "xa_wo", "ffn_w_up", "ffn_w_down", "pc_w_in", "pc_w_out", "mla_w_ukv")}
    wb["mla_wdq"] = jnp.stack([_pad_wdq(full["mla_w_dq_dkv"][o]) for o in range(DEPTH // 2)])
    wb["mla_wuq"] = jnp.stack([_pad_wuq(full["mla_w_uq"][o]) for o in range(DEPTH // 2)])
    wb["mla_wo"] = jnp.stack([_pad_wo(full["mla_w_o"][o]) for o in range(DEPTH // 2)])

    loss, grad_x, grads = _local_step(x[0], mem[0], positions[0], loss_target[0], wb, ws)
    loss = lax.psum(loss[0, 0], ("x", "y", "c"))

    kinds = [k for _, k in BIG]
    big = [grads[n] for n, _ in BIG]
    c_idx = cc.reshape(1).astype(jnp.int32)
    chip_idx = chip.reshape(1).astype(jnp.int32)
    theirs = _exchange_halves(big, "reduce_pair")
    pair = [_add_half(gr, th, c_idx, f"reduce_pair_add_{n}") for gr, th, (n, _) in zip(big, theirs, BIG)]
    slots = _scatter_to_chips(pair, kinds, "reduce_chips")
    halves = [_sum_chips(sl, pr, chip_idx, kind, f"reduce_chips_add_{n}")
              for sl, pr, (n, kind) in zip(slots, pair, BIG)]
    others = _swap_halves(halves, "reduce_join")
    gsum, delta, new_m, new_v = {}, {}, {}, {}
    for mine, other, (n, _) in zip(halves, others, BIG):
        gsum[n], delta[n], new_m[n], new_v[n] = _adamw_halves(w[n], mine, other, m[n], v[n], c_idx, f"adamw_{n}")

    small_names = SMALL_REPLICATED + SMALL_SHARDED
    small_grad_shapes = [grads[n].shape for n in small_names]
    packed = _pack([grads[n] for n in small_names])
    rows = packed.shape[0]
    allparts = _all_gather_rows(packed, "gather_small_grads").reshape(8, rows, LANE)
    total = _sum_leading(allparts, "sum_small_grads").reshape(-1)
    for n, gfull in zip(small_names, _unpack(total, small_grad_shapes)):
        if n in SMALL_SHARDED:
            width = w[n].shape[-1]
            gfull = lax.dynamic_slice_in_dim(gfull, chip * width, width, axis=gfull.ndim - 1)
        gsum[n] = gfull

    for n in SMALL_REPLICATED + SMALL_SHARDED:
        delta[n], new_m[n], new_v[n] = _adamw(w[n], gsum[n], m[n], v[n], f"adamw_{n}")
    return (loss, grad_x[None], *[gsum[n] for n in WEIGHTS], *[delta[n] for n in WEIGHTS],
            *[new_m[n] for n in WEIGHTS], *[new_v[n] for n in WEIGHTS])
```

```python
import functools
import math

import numpy as np
import jax
import jax.numpy as jnp
from jax import lax
from jax.experimental import pallas as pl
from jax.experimental.pallas import tpu as pltpu

F32 = jnp.float32
MXU_DT = jnp.bfloat16
XFER_DT = jnp.bfloat16

D_MODEL = 1024
DEPTH = 4
MEM_LEN = 256
XA_HEADS = 4
XA_HEAD_DIM = 256
POOL_W = 512
POOL_WINDOWS = (2, 4, 8, 16)
POOL_GROUP = 128
CONV_W = 512
CONV_K = 31
MLA_HEADS = 16
QK_NOPE = 64
QK_ROPE = 32
V_HEAD = 64
Q_LORA = 384
KV_LORA = 256
ROPE_THETA = 10000.0
MLA_SCALE = 1.0 / math.sqrt(QK_NOPE + QK_ROPE)
LOG2E = math.log2(math.e)
D_FF = 2816
FFN_CONV_K = 3
EPS = 1e-6
NEG = -1e30
ADAM_LR = 0.001
ADAM_B1 = 0.9
ADAM_B2 = 0.999
ADAM_EPS = 1e-08
ADAM_WD = 0.01
ADAM_STEP = 10

HEAD_PAD = 128
C_PAD = 768
KPE_LANE = 64

VMEM_LIMIT = 52 * 1024 * 1024
BLOCK_BYTES = 6 * 1024 * 1024
LANE = 128
SUBLANE = 8

TM = 1024
TN = 1408
TK = 2048
TT = 512
TW = 256
TWF = 128
FFN_CHUNK = 256
TA = 1024
MIX_HALO = 32
FFN_HALO = 8

NN = (((1,), (0,)), ((), ()))
NT = (((1,), (1,)), ((), ()))
TN_DIMS = (((0,), (0,)), ((), ()))
MESH_ID = pl.DeviceIdType.MESH


def _cparams(*sem):
    return pltpu.CompilerParams(dimension_semantics=sem, vmem_limit_bytes=VMEM_LIMIT)


def _tile(n, pref, limit=None):
    cap = pref if limit is None else min(pref, limit)
    if n <= cap:
        return n
    t = (cap // LANE) * LANE
    while t >= LANE:
        if n % t == 0:
            return t
        t -= LANE
    return n


def _rows(t, pref):
    return t if t <= pref else pref


def _sigmoid(x):
    return 1.0 / (1.0 + jnp.exp(-x))


def _matmul(a, b, mode, out_dtype, name, layer=None, res=None, stack=None):
    if layer is None:
        b2 = b.shape
    else:
        b2 = b.shape[1:]
    if mode == "tn":
        k, m = a.shape
        k2, n = b2
    elif mode == "nn":
        m, k = a.shape
        k2, n = b2
    else:
        m, k = a.shape
        n, k2 = b2
    assert k == k2, (a.shape, b.shape, mode)
    isz_a = jnp.dtype(a.dtype).itemsize
    isz_b = jnp.dtype(b.dtype).itemsize
    if mode == "tn":
        tk = _tile(k, TK * 2 // max(isz_a, isz_b))
        tm = _tile(m, TN, BLOCK_BYTES // (tk * isz_a))
        tn = _tile(n, TN, BLOCK_BYTES // (tk * isz_b))
    else:
        tk = k
        tn = _tile(n, TN, BLOCK_BYTES // (tk * isz_b))
        tm = _tile(m, TM, min(BLOCK_BYTES // (tk * isz_a), BLOCK_BYTES // (tn * 4)))
    nk = k // tk
    grid = (m // tm, n // tn, nk)
    if mode == "nn":
        a_spec = pl.BlockSpec((tm, tk), lambda i, j, kk: (i, kk))
        b_blk, b_map, dn = (tk, tn), (lambda i, j, kk: (kk, j)), NN
    elif mode == "nt":
        a_spec = pl.BlockSpec((tm, tk), lambda i, j, kk: (i, kk))
        b_blk, b_map, dn = (tn, tk), (lambda i, j, kk: (j, kk)), NT
    else:
        a_spec = pl.BlockSpec((tk, tm), lambda i, j, kk: (kk, i))
        b_blk, b_map, dn = (tk, tn), (lambda i, j, kk: (kk, j)), TN_DIMS
    if layer is None:
        b_spec = pl.BlockSpec(b_blk, b_map)
    else:
        b_spec = pl.BlockSpec((None,) + b_blk, lambda i, j, kk: (layer,) + b_map(i, j, kk))
    in_specs = [a_spec, b_spec]
    args = [a, b]
    if res is not None:
        in_specs.append(pl.BlockSpec((tm, tn), lambda i, j, kk: (i, j)))
        args.append(res)
    has_res = res is not None
    aliases = {}
    if stack is None:
        o_spec = pl.BlockSpec((tm, tn), lambda i, j, kk: (i, j))
        out_shape = jax.ShapeDtypeStruct((m, n), out_dtype)
    else:
        buf, slab, nslab = stack
        o_spec = pl.BlockSpec((None, tm, tn), lambda i, j, kk: (slab, i, j))
        out_shape = jax.ShapeDtypeStruct((nslab, m, n), out_dtype)
        if buf is not None:
            in_specs.append(pl.BlockSpec(memory_space=pl.ANY))
            args.append(buf)
            aliases = {len(args) - 1: 0}
    n_in = len(args)

    def body(*refs):
        a_ref, b_ref = refs[0], refs[1]
        r_ref = refs[2] if has_res else None
        o_ref = refs[n_in]
        p = lax.dot_general(a_ref[...].astype(MXU_DT), b_ref[...].astype(MXU_DT), dn, preferred_element_type=F32)
        if nk == 1:
            if has_res:
                p = r_ref[...] + p
            o_ref[...] = p.astype(o_ref.dtype)
        else:
            acc_ref = refs[-1]
            kk = pl.program_id(2)

            @pl.when(kk == 0)
            def _():
                acc_ref[...] = jnp.zeros_like(acc_ref)

            acc_ref[...] += p

            @pl.when(kk == nk - 1)
            def _():
                r = acc_ref[...]
                if has_res:
                    r = r_ref[...] + r
                o_ref[...] = r.astype(o_ref.dtype)

    scratch = [pltpu.VMEM((tm, tn), F32)] if nk > 1 else []
    return pl.pallas_call(
        body,
        grid=grid,
        in_specs=in_specs,
        out_specs=o_spec,
        out_shape=out_shape,
        scratch_shapes=scratch,
        input_output_aliases=aliases,
        name=name,
        compiler_params=_cparams("parallel", "parallel", "arbitrary"),
    )(*args)


def _row_tile(m, k, isz):
    return _tile(m, TM, min(BLOCK_BYTES // (k * isz), TM if k <= 2 * TK else TM // 4))


def _matmul_res_norm(a, b, layer, res, gain, name):
    m, k = a.shape
    n = b.shape[-1]
    tm = _row_tile(m, k, jnp.dtype(a.dtype).itemsize)

    def body(a_ref, b_ref, r_ref, g_ref, x_ref, h_ref):
        x = r_ref[...] + jnp.dot(a_ref[...].astype(MXU_DT), b_ref[...].astype(MXU_DT), preferred_element_type=F32)
        x_ref[...] = x
        r = lax.rsqrt(jnp.mean(x * x, axis=-1, keepdims=True) + EPS)
        h_ref[...] = ((x * r) * g_ref[...]).astype(h_ref.dtype)

    row = pl.BlockSpec((tm, n), lambda i: (i, 0))
    return pl.pallas_call(
        body,
        grid=(m // tm,),
        in_specs=[pl.BlockSpec((tm, k), lambda i: (i, 0)), pl.BlockSpec((None, k, n), lambda i: (layer, 0, 0)), row,
                  pl.BlockSpec((1, n), lambda i: (0, 0))],
        out_specs=[row, row],
        out_shape=[jax.ShapeDtypeStruct((m, n), F32), jax.ShapeDtypeStruct((m, n), MXU_DT)],
        name=name,
        compiler_params=_cparams("parallel"),
    )(a, b, res, gain.reshape(1, n))


def _matmul_rms_bwd(a, b, layer, x, gain, dx_in, name):
    m, k = a.shape
    n = b.shape[-2]
    tm = _row_tile(m, k, jnp.dtype(a.dtype).itemsize)

    def body(a_ref, b_ref, x_ref, g_ref, dxi_ref, dx_ref, dx16_ref, dg_ref):
        @pl.when(pl.program_id(0) == 0)
        def _():
            dg_ref[...] = jnp.zeros_like(dg_ref)

        dh = lax.dot_general(a_ref[...].astype(MXU_DT), b_ref[...].astype(MXU_DT), NT, preferred_element_type=F32)
        xf = x_ref[...]
        r = lax.rsqrt(jnp.mean(xf * xf, axis=-1, keepdims=True) + EPS)
        xh = xf * r
        gy = dh * g_ref[...]
        dx = dxi_ref[...] + r * (gy - xh * jnp.mean(gy * xh, axis=-1, keepdims=True))
        dx_ref[...] = dx
        dx16_ref[...] = dx.astype(dx16_ref.dtype)
        dg_ref[...] += jnp.sum(dh * xh, axis=0, keepdims=True)

    row = pl.BlockSpec((tm, n), lambda i: (i, 0))
    vec = pl.BlockSpec((1, n), lambda i: (0, 0))
    return pl.pallas_call(
        body,
        grid=(m // tm,),
        in_specs=[pl.BlockSpec((tm, k), lambda i: (i, 0)), pl.BlockSpec((None, n, k), lambda i: (layer, 0, 0)), row, vec, row],
        out_specs=[row, row, vec],
        out_shape=[jax.ShapeDtypeStruct((m, n), F32), jax.ShapeDtypeStruct((m, n), MXU_DT), jax.ShapeDtypeStruct((1, n), F32)],
        name=name,
        compiler_params=_cparams("arbitrary"),
    )(a, b, x, gain.reshape(1, n), dx_in)


def _rms_fwd(x, g, name):
    t, d = x.shape
    tt = _rows(t, TT)

    def body(x_ref, g_ref, o_ref):
        xf = x_ref[...]
        r = lax.rsqrt(jnp.mean(xf * xf, axis=-1, keepdims=True) + EPS)
        o_ref[...] = ((xf * r) * g_ref[...]).astype(o_ref.dtype)

    return pl.pallas_call(
        body,
        grid=(t // tt,),
        in_specs=[pl.BlockSpec((tt, d), lambda i: (i, 0)), pl.BlockSpec((1, d), lambda i: (0, 0))],
        out_specs=pl.BlockSpec((tt, d), lambda i: (i, 0)),
        out_shape=jax.ShapeDtypeStruct((t, d), MXU_DT),
        name=name,
        compiler_params=_cparams("parallel"),
    )(x, g.reshape(1, d))


def _rms_bwd(dh, x, g, dx_in, name):
    t, d = x.shape
    tt = _rows(t, TT)

    def body(dh_ref, x_ref, g_ref, dxi_ref, dx_ref, dg_ref):
        @pl.when(pl.program_id(0) == 0)
        def _():
            dg_ref[...] = jnp.zeros_like(dg_ref)

        xf = x_ref[...]
        dh_v = dh_ref[...]
        r = lax.rsqrt(jnp.mean(xf * xf, axis=-1, keepdims=True) + EPS)
        xh = xf * r
        gy = dh_v * g_ref[...]
        dx = r * (gy - xh * jnp.mean(gy * xh, axis=-1, keepdims=True))
        dx_ref[...] = dxi_ref[...] + dx
        dg_ref[...] += jnp.sum(dh_v * xh, axis=0, keepdims=True)

    row = pl.BlockSpec((tt, d), lambda i: (i, 0))
    vec = pl.BlockSpec((1, d), lambda i: (0, 0))
    return pl.pallas_call(
        body,
        grid=(t // tt,),
        in_specs=[row, row, vec, row],
        out_specs=[row, vec],
        out_shape=[jax.ShapeDtypeStruct((t, d), F32), jax.ShapeDtypeStruct((1, d), F32)],
        name=name,
        compiler_params=_cparams("arbitrary"),
    )(dh, x, g.reshape(1, d), dx_in)


def _rms_bwd_gain(dh, x, g, name):
    t, d = x.shape
    tt = _rows(t, TT)

    def body(dh_ref, x_ref, dg_ref):
        @pl.when(pl.program_id(0) == 0)
        def _():
            dg_ref[...] = jnp.zeros_like(dg_ref)

        xf = x_ref[...]
        r = lax.rsqrt(jnp.mean(xf * xf, axis=-1, keepdims=True) + EPS)
        dg_ref[...] += jnp.sum(dh_ref[...] * (xf * r), axis=0, keepdims=True)

    row = pl.BlockSpec((tt, d), lambda i: (i, 0))
    vec = pl.BlockSpec((1, d), lambda i: (0, 0))
    return pl.pallas_call(
        body,
        grid=(t // tt,),
        in_specs=[row, row],
        out_specs=vec,
        out_shape=jax.ShapeDtypeStruct((1, d), F32),
        name=name,
        compiler_params=_cparams("arbitrary"),
    )(dh, x)


def _loss_head(x, target, g, name):
    t, d = x.shape
    tt = _rows(t, TT)

    def body(x_ref, t_ref, g_ref, dx_ref, dx16_ref, dg_ref, loss_ref):
        @pl.when(pl.program_id(0) == 0)
        def _():
            dg_ref[...] = jnp.zeros_like(dg_ref)
            loss_ref[...] = jnp.zeros_like(loss_ref)

        xf = x_ref[...]
        gv = g_ref[...]
        r = lax.rsqrt(jnp.mean(xf * xf, axis=-1, keepdims=True) + EPS)
        xh = xf * r
        err = xh * gv - t_ref[...]
        e2 = jnp.sum(err * err, axis=-1, keepdims=True)
        loss_ref[...] += (0.5 / d) * jnp.sum(e2, axis=0, keepdims=True)
        dy = err * (1.0 / d)
        gy = dy * gv
        dx = r * (gy - xh * jnp.mean(gy * xh, axis=-1, keepdims=True))
        dx_ref[...] = dx
        dx16_ref[...] = dx.astype(dx16_ref.dtype)
        dg_ref[...] += jnp.sum(dy * xh, axis=0, keepdims=True)

    row = pl.BlockSpec((tt, d), lambda i: (i, 0))
    vec = pl.BlockSpec((1, d), lambda i: (0, 0))
    return pl.pallas_call(
        body,
        grid=(t // tt,),
        in_specs=[row, row, vec],
        out_specs=[row, row, vec, pl.BlockSpec((1, 1), lambda i: (0, 0))],
        out_shape=[
            jax.ShapeDtypeStruct((t, d), F32),
            jax.ShapeDtypeStruct((t, d), MXU_DT),
            jax.ShapeDtypeStruct((1, d), F32),
            jax.ShapeDtypeStruct((1, 1), F32),
        ],
        name=name,
        compiler_params=_cparams("arbitrary"),
    )(x, target, g.reshape(1, d))


def _prev_halo(tt, hp, width):
    return pl.BlockSpec((hp, width), lambda i: (jnp.maximum(i * (tt // hp) - 1, 0), 0))


def _next_halo(tt, hp, width, t):
    return pl.BlockSpec((hp, width), lambda i: (jnp.minimum((i + 1) * (tt // hp), t // hp - 1), 0))


def _ffn_chunks():
    return [(c0, FFN_CHUNK) for c0 in range(0, D_FF, FFN_CHUNK)]


def _ffn_fwd(up, conv_w, conv_b, name):
    t = up.shape[0]
    tt = _rows(t, 2 * TWF)
    hp = FFN_HALO

    def body(up_ref, gp_ref, w_ref, b_ref, act_ref, ext_ref):
        first = pl.program_id(0) == 0
        for c0, cw in _ffn_chunks():
            ga = pl.ds(D_FF + c0, cw)
            ext_ref[0:hp, :] = jnp.where(first, 0.0, gp_ref[:, ga])
            ext_ref[hp : hp + tt, :] = up_ref[:, ga]
            gc = b_ref[:, pl.ds(c0, cw)]
            for j in range(FFN_CONV_K):
                off = hp - (FFN_CONV_K - 1) + j
                gc = gc + w_ref[j : j + 1, pl.ds(c0, cw)] * ext_ref[off : off + tt, :]
            a = up_ref[:, pl.ds(c0, cw)]
            act_ref[:, pl.ds(c0, cw)] = (gc * _sigmoid(gc) * a).astype(act_ref.dtype)

    return pl.pallas_call(
        body,
        grid=(t // tt,),
        in_specs=[
            pl.BlockSpec((tt, 2 * D_FF), lambda i: (i, 0)),
            _prev_halo(tt, hp, 2 * D_FF),
            pl.BlockSpec((FFN_CONV_K, D_FF), lambda i: (0, 0)),
            pl.BlockSpec((1, D_FF), lambda i: (0, 0)),
        ],
        out_specs=pl.BlockSpec((tt, D_FF), lambda i: (i, 0)),
        out_shape=jax.ShapeDtypeStruct((t, D_FF), MXU_DT),
        scratch_shapes=[pltpu.VMEM((tt + hp, FFN_CHUNK), F32)],
        name=name,
        compiler_params=_cparams("parallel"),
    )(up, up, conv_w, conv_b.reshape(1, D_FF))


def _ffn_bwd(up, dact, conv_w, conv_b, name):
    t = up.shape[0]
    tt = _rows(t, TWF)
    hp = FFN_HALO
    nt = t // tt
    kk = FFN_CONV_K

    def body(up_ref, upp_ref, upn_ref, da_ref, dan_ref, w_ref, b_ref, dup_ref, dw_ref, db_ref, ext_ref, dgc_ref):
        i = pl.program_id(0)
        first = i == 0
        last = i == nt - 1

        @pl.when(first)
        def _():
            dw_ref[...] = jnp.zeros_like(dw_ref)
            db_ref[...] = jnp.zeros_like(db_ref)

        for c0, cw in _ffn_chunks():
            ca = pl.ds(c0, cw)
            ga = pl.ds(D_FF + c0, cw)
            ext_ref[0:hp, :] = jnp.where(first, 0.0, upp_ref[:, ga])
            ext_ref[hp : hp + tt, :] = up_ref[:, ga]
            ext_ref[hp + tt : hp + tt + hp, :] = upn_ref[:, ga]
            gc = b_ref[:, ca]
            for j in range(kk):
                off = hp - (kk - 1) + j
                gc = gc + w_ref[j : j + 1, ca] * ext_ref[off : off + tt + hp, :]
            sg = _sigmoid(gc)
            silu = gc * sg
            dsilu = sg * (1.0 + gc * (1.0 - sg))
            a_all = jnp.concatenate([up_ref[:, ca], upn_ref[:, ca]], axis=0)
            dact_all = jnp.concatenate([da_ref[:, ca], jnp.where(last, 0.0, dan_ref[:, ca])], axis=0)
            dgc = dact_all * a_all * dsilu
            dgc_ref[...] = dgc
            dup_ref[:, ca] = (dact_all[0:tt] * silu[0:tt]).astype(dup_ref.dtype)
            dg = jnp.zeros((tt, cw), F32)
            for j in range(kk):
                dg = dg + w_ref[j : j + 1, ca] * dgc_ref[kk - 1 - j : kk - 1 - j + tt, :]
            dup_ref[:, ga] = dg.astype(dup_ref.dtype)
            dgc_t = dgc[0:tt]
            db_ref[:, ca] += jnp.sum(dgc_t, axis=0, keepdims=True)
            for j in range(kk):
                off = hp - (kk - 1) + j
                dw_ref[j : j + 1, ca] += jnp.sum(dgc_t * ext_ref[off : off + tt, :], axis=0, keepdims=True)

    return pl.pallas_call(
        body,
        grid=(nt,),
        in_specs=[
            pl.BlockSpec((tt, 2 * D_FF), lambda i: (i, 0)),
            _prev_halo(tt, hp, 2 * D_FF),
            _next_halo(tt, hp, 2 * D_FF, t),
            pl.BlockSpec((tt, D_FF), lambda i: (i, 0)),
            _next_halo(tt, hp, D_FF, t),
            pl.BlockSpec((kk, D_FF), lambda i: (0, 0)),
            pl.BlockSpec((1, D_FF), lambda i: (0, 0)),
        ],
        out_specs=[
            pl.BlockSpec((tt, 2 * D_FF), lambda i: (i, 0)),
            pl.BlockSpec((kk, D_FF), lambda i: (0, 0)),
            pl.BlockSpec((1, D_FF), lambda i: (0, 0)),
        ],
        out_shape=[
            jax.ShapeDtypeStruct((t, 2 * D_FF), MXU_DT),
            jax.ShapeDtypeStruct((kk, D_FF), F32),
            jax.ShapeDtypeStruct((1, D_FF), F32),
        ],
        scratch_shapes=[pltpu.VMEM((tt + 2 * hp, FFN_CHUNK), F32), pltpu.VMEM((tt + hp, FFN_CHUNK), F32)],
        name=name,
        compiler_params=_cparams("arbitrary"),
    )(up, up, up, dact, dact, conv_w, conv_b.reshape(1, D_FF))


def _layernorm_silu(cv, ln_g, ln_b):
    mu = jnp.mean(cv, axis=-1, keepdims=True)
    xc = cv - mu
    rstd = lax.rsqrt(jnp.mean(xc * xc, axis=-1, keepdims=True) + EPS)
    xh = xc * rstd
    a = xh * ln_g + ln_b
    return xh, rstd, a


def _shifted_copies(ref, n):
    for b in range(1, SUBLANE):
        ref[b, 0 : n - SUBLANE, :] = ref[0, b : b + n - SUBLANE, :]


def _tap(ref, offset, rows, cols):
    b = offset % SUBLANE
    return ref[b, offset - b : offset - b + rows, cols]


def _mix_fwd(z, pool_w, pool_scale, dw_w, dw_b, ln_g, ln_b, name):
    t = z.shape[0]
    tt = _rows(t, TW)
    hp = MIX_HALO
    zw = POOL_W + 2 * CONV_W

    def body(z_ref, zp_ref, pw_ref, ps_ref, w_ref, b_ref, lg_ref, lb_ref, cat_ref, eu_ref, egl_ref, cv_ref):
        i = pl.program_id(0)
        first = i == 0
        eu_ref[0:hp, :] = jnp.where(first, 0.0, zp_ref[:, 0:POOL_W])
        eu_ref[hp : hp + tt, :] = z_ref[:, 0:POOL_W]
        glp = zp_ref[:, POOL_W : POOL_W + CONV_W] * _sigmoid(zp_ref[:, POOL_W + CONV_W : zw])
        egl_ref[0, 0:hp, :] = jnp.where(first, 0.0, glp)
        egl_ref[0, hp : hp + tt, :] = z_ref[:, POOL_W : POOL_W + CONV_W] * _sigmoid(z_ref[:, POOL_W + CONV_W : zw])
        _shifted_copies(egl_ref, tt + hp)
        row = i * tt + lax.broadcasted_iota(jnp.int32, (tt, 1), 0)
        for gi, w in enumerate(POOL_WINDOWS):
            cols = pl.ds(gi * POOL_GROUP, POOL_GROUP)
            u = eu_ref[hp : hp + tt, cols]
            acc = u
            for k in range(1, w):
                acc = acc + eu_ref[hp - k : hp - k + tt, cols]
            cnt = jnp.minimum(row + 1, w).astype(F32)
            pooled = acc / cnt - u
            y = jnp.dot(pooled.astype(MXU_DT), pw_ref[gi].astype(MXU_DT), preferred_element_type=F32)
            cat_ref[:, cols] = (y * ps_ref[:, cols]).astype(cat_ref.dtype)
        for c0 in range(0, CONV_W, LANE):
            cs = pl.ds(c0, LANE)
            acc = jnp.broadcast_to(b_ref[:, cs], (tt, LANE))
            for j in range(CONV_K):
                acc = acc + w_ref[j : j + 1, cs] * _tap(egl_ref, hp - (CONV_K - 1) + j, tt, cs)
            cv_ref[:, cs] = acc
        _, _, a = _layernorm_silu(cv_ref[...], lg_ref[...], lb_ref[...])
        cat_ref[:, POOL_W : POOL_W + CONV_W] = (a * _sigmoid(a)).astype(cat_ref.dtype)

    vec = pl.BlockSpec((1, CONV_W), lambda i: (0, 0))
    return pl.pallas_call(
        body,
        grid=(t // tt,),
        in_specs=[
            pl.BlockSpec((tt, zw), lambda i: (i, 0)),
            _prev_halo(tt, hp, zw),
            pl.BlockSpec((len(POOL_WINDOWS), POOL_GROUP, POOL_GROUP), lambda i: (0, 0, 0)),
            vec,
            pl.BlockSpec((CONV_K, CONV_W), lambda i: (0, 0)),
            vec,
            vec,
            vec,
        ],
        out_specs=pl.BlockSpec((tt, POOL_W + CONV_W), lambda i: (i, 0)),
        out_shape=jax.ShapeDtypeStruct((t, POOL_W + CONV_W), MXU_DT),
        scratch_shapes=[pltpu.VMEM((tt + hp, POOL_W), F32), pltpu.VMEM((SUBLANE, tt + hp, CONV_W), F32),
                        pltpu.VMEM((tt, CONV_W), F32)],
        name=name,
        compiler_params=_cparams("parallel"),
    )(z, z, pool_w, pool_scale.reshape(1, POOL_W), dw_w, dw_b.reshape(1, CONV_W), ln_g.reshape(1, CONV_W), ln_b.reshape(1, CONV_W))


def _mix_bwd(z, dcat, pool_w, pool_scale, dw_w, dw_b, ln_g, ln_b, name):
    t = z.shape[0]
    tt = _rows(t, TW)
    hp = MIX_HALO
    nt = t // tt
    zw = POOL_W + 2 * CONV_W
    ng = len(POOL_WINDOWS)

    def body(z_ref, zp_ref, zn_ref, dc_ref, dcn_ref, pw_ref, ps_ref, w_ref, b_ref, lg_ref, lb_ref,
             dz_ref, dpw_ref, dps_ref, dww_ref, dwb_ref, dlg_ref, dlb_ref, eu_ref, ee_ref, egl_ref, edcv_ref, cv_ref):
        i = pl.program_id(0)
        first = i == 0
        last = i == nt - 1

        @pl.when(first)
        def _():
            for r in (dpw_ref, dps_ref, dww_ref, dwb_ref, dlg_ref, dlb_ref):
                r[...] = jnp.zeros_like(r)

        eu_ref[0:hp, :] = jnp.where(first, 0.0, zp_ref[:, 0:POOL_W])
        eu_ref[hp : hp + tt, :] = z_ref[:, 0:POOL_W]
        row = i * tt + lax.broadcasted_iota(jnp.int32, (tt, 1), 0)
        row_ext = i * tt + lax.broadcasted_iota(jnp.int32, (tt + hp, 1), 0)
        for gi, w in enumerate(POOL_WINDOWS):
            cols = pl.ds(gi * POOL_GROUP, POOL_GROUP)
            u = eu_ref[hp : hp + tt, cols]
            acc = u
            for k in range(1, w):
                acc = acc + eu_ref[hp - k : hp - k + tt, cols]
            pooled = (acc / jnp.minimum(row + 1, w).astype(F32) - u).astype(MXU_DT)
            pw = pw_ref[gi].astype(MXU_DT)
            dya = dc_ref[:, cols]
            y = jnp.dot(pooled, pw, preferred_element_type=F32)
            dps_ref[:, cols] += jnp.sum(dya * y, axis=0, keepdims=True)
            scale = ps_ref[:, cols]
            dy_all = jnp.concatenate([dya, jnp.where(last, 0.0, dcn_ref[:, cols])], axis=0) * scale
            dy_all = dy_all.astype(MXU_DT)
            dpw_ref[gi] += lax.dot_general(pooled, dy_all[0:tt], TN_DIMS, preferred_element_type=F32)
            dpooled = lax.dot_general(dy_all, pw, NT, preferred_element_type=F32)
            ee_ref[:, cols] = dpooled / jnp.minimum(row_ext + 1, w).astype(F32)
            du = -dpooled[0:tt]
            for k in range(w):
                du = du + ee_ref[k : k + tt, cols]
            dz_ref[:, cols] = du.astype(dz_ref.dtype)

        ca = slice(POOL_W, POOL_W + CONV_W)
        cb = slice(POOL_W + CONV_W, zw)
        egl_ref[0, 0:hp, :] = jnp.where(first, 0.0, zp_ref[:, ca] * _sigmoid(zp_ref[:, cb]))
        egl_ref[0, hp : hp + tt, :] = z_ref[:, ca] * _sigmoid(z_ref[:, cb])
        egl_ref[0, hp + tt : hp + tt + hp, :] = zn_ref[:, ca] * _sigmoid(zn_ref[:, cb])
        _shifted_copies(egl_ref, tt + 2 * hp)
        for c0 in range(0, CONV_W, LANE):
            cs = pl.ds(c0, LANE)
            acc = jnp.broadcast_to(b_ref[:, cs], (tt + hp, LANE))
            for j in range(CONV_K):
                acc = acc + w_ref[j : j + 1, cs] * _tap(egl_ref, hp - (CONV_K - 1) + j, tt + hp, cs)
            cv_ref[:, cs] = acc
        lg = lg_ref[...]
        xh, rstd, a = _layernorm_silu(cv_ref[...], lg, lb_ref[...])
        sa = _sigmoid(a)
        dyb = jnp.concatenate([dc_ref[:, ca], jnp.where(last, 0.0, dcn_ref[:, ca])], axis=0)
        da = dyb * (sa * (1.0 + a * (1.0 - sa)))
        dlg_ref[...] += jnp.sum(da[0:tt] * xh[0:tt], axis=0, keepdims=True)
        dlb_ref[...] += jnp.sum(da[0:tt], axis=0, keepdims=True)
        dxh = da * lg
        dcv = rstd * (dxh - jnp.mean(dxh, axis=-1, keepdims=True) - xh * jnp.mean(dxh * xh, axis=-1, keepdims=True))
        edcv_ref[0] = dcv
        _shifted_copies(edcv_ref, tt + hp)
        dwb_ref[...] += jnp.sum(dcv[0:tt], axis=0, keepdims=True)
        for c0 in range(0, CONV_W, LANE):
            cs = pl.ds(c0, LANE)
            gl_t = egl_ref[0, hp : hp + tt, cs]
            dgl = jnp.zeros((tt, LANE), F32)
            for j in range(CONV_K):
                tap = _tap(edcv_ref, CONV_K - 1 - j, tt, cs)
                dww_ref[j : j + 1, cs] += jnp.sum(gl_t * tap, axis=0, keepdims=True)
                dgl = dgl + w_ref[j : j + 1, cs] * tap
            ga = z_ref[:, pl.ds(POOL_W + c0, LANE)]
            sgb = _sigmoid(z_ref[:, pl.ds(POOL_W + CONV_W + c0, LANE)])
            dz_ref[:, pl.ds(POOL_W + c0, LANE)] = (dgl * sgb).astype(dz_ref.dtype)
            dz_ref[:, pl.ds(POOL_W + CONV_W + c0, LANE)] = (dgl * ga * sgb * (1.0 - sgb)).astype(dz_ref.dtype)

    vec = pl.BlockSpec((1, CONV_W), lambda i: (0, 0))
    pw_spec = pl.BlockSpec((ng, POOL_GROUP, POOL_GROUP), lambda i: (0, 0, 0))
    w_spec = pl.BlockSpec((CONV_K, CONV_W), lambda i: (0, 0))
    return pl.pallas_call(
        body,
        grid=(nt,),
        in_specs=[
            pl.BlockSpec((tt, zw), lambda i: (i, 0)),
            _prev_halo(tt, hp, zw),
            _next_halo(tt, hp, zw, t),
            pl.BlockSpec((tt, POOL_W + CONV_W), lambda i: (i, 0)),
            _next_halo(tt, hp, POOL_W + CONV_W, t),
            pw_spec, vec, w_spec, vec, vec, vec,
        ],
        out_specs=[pl.BlockSpec((tt, zw), lambda i: (i, 0)), pw_spec, vec, w_spec, vec, vec, vec],
        out_shape=[
            jax.ShapeDtypeStruct((t, zw), MXU_DT),
            jax.ShapeDtypeStruct((ng, POOL_GROUP, POOL_GROUP), F32),
            jax.ShapeDtypeStruct((1, POOL_W), F32),
            jax.ShapeDtypeStruct((CONV_K, CONV_W), F32),
            jax.ShapeDtypeStruct((1, CONV_W), F32),
            jax.ShapeDtypeStruct((1, CONV_W), F32),
            jax.ShapeDtypeStruct((1, CONV_W), F32),
        ],
        scratch_shapes=[
            pltpu.VMEM((tt + hp, POOL_W), F32),
            pltpu.VMEM((tt + hp, POOL_W), F32),
            pltpu.VMEM((SUBLANE, tt + 2 * hp, CONV_W), F32),
            pltpu.VMEM((SUBLANE, tt + hp, CONV_W), F32),
            pltpu.VMEM((tt + hp, CONV_W), F32),
        ],
        name=name,
        compiler_params=_cparams("arbitrary"),
    )(z, z, z, dcat, dcat, pool_w, pool_scale.reshape(1, POOL_W), dw_w, dw_b.reshape(1, CONV_W),
      ln_g.reshape(1, CONV_W), ln_b.reshape(1, CONV_W))


def _xa_fwd(q, kvm, name):
    t = q.shape[0]
    tt = _rows(t, TT)
    scale = XA_HEAD_DIM ** -0.5

    def body(q_ref, kv_ref, o_ref):
        for h in range(XA_HEADS):
            cs = pl.ds(h * XA_HEAD_DIM, XA_HEAD_DIM)
            vs = pl.ds(D_MODEL + h * XA_HEAD_DIM, XA_HEAD_DIM)
            s = lax.dot_general(q_ref[:, cs], kv_ref[:, cs], NT, preferred_element_type=F32) * scale
            p = jnp.exp(s - jnp.max(s, axis=-1, keepdims=True))
            p = p / jnp.sum(p, axis=-1, keepdims=True)
            o_ref[:, cs] = jnp.dot(p.astype(MXU_DT), kv_ref[:, vs], preferred_element_type=F32).astype(o_ref.dtype)

    return pl.pallas_call(
        body,
        grid=(t // tt,),
        in_specs=[pl.BlockSpec((tt, D_MODEL), lambda i: (i, 0)), pl.BlockSpec((MEM_LEN, 2 * D_MODEL), lambda i: (0, 0))],
        out_specs=pl.BlockSpec((tt, D_MODEL), lambda i: (i, 0)),
        out_shape=jax.ShapeDtypeStruct((t, D_MODEL), MXU_DT),
        name=name,
        compiler_params=_cparams("parallel"),
    )(q, kvm)


def _xa_bwd(q, kvm, do, name):
    t = q.shape[0]
    tt = _rows(t, TT)
    scale = XA_HEAD_DIM ** -0.5

    def body(q_ref, kv_ref, do_ref, dq_ref, dkv_ref):
        @pl.when(pl.program_id(0) == 0)
        def _():
            dkv_ref[...] = jnp.zeros_like(dkv_ref)

        for h in range(XA_HEADS):
            cs = pl.ds(h * XA_HEAD_DIM, XA_HEAD_DIM)
            vs = pl.ds(D_MODEL + h * XA_HEAD_DIM, XA_HEAD_DIM)
            qh = q_ref[:, cs]
            kh = kv_ref[:, cs]
            doh = do_ref[:, cs]
            s = lax.dot_general(qh, kh, NT, preferred_element_type=F32) * scale
            p = jnp.exp(s - jnp.max(s, axis=-1, keepdims=True))
            p = p / jnp.sum(p, axis=-1, keepdims=True)
            dp = lax.dot_general(doh, kv_ref[:, vs], NT, preferred_element_type=F32)
            ds = (p * (dp - jnp.sum(p * dp, axis=-1, keepdims=True)) * scale).astype(MXU_DT)
            dq_ref[:, cs] = jnp.dot(ds, kh, preferred_element_type=F32).astype(dq_ref.dtype)
            dkv_ref[:, cs] += lax.dot_general(ds, qh, TN_DIMS, preferred_element_type=F32)
            dkv_ref[:, vs] += lax.dot_general(p.astype(MXU_DT), doh, TN_DIMS, preferred_element_type=F32)

    row = pl.BlockSpec((tt, D_MODEL), lambda i: (i, 0))
    kvs = pl.BlockSpec((MEM_LEN, 2 * D_MODEL), lambda i: (0, 0))
    return pl.pallas_call(
        body,
        grid=(t // tt,),
        in_specs=[row, kvs, row],
        out_specs=[row, kvs],
        out_shape=[jax.ShapeDtypeStruct((t, D_MODEL), MXU_DT), jax.ShapeDtypeStruct((MEM_LEN, 2 * D_MODEL), F32)],
        name=name,
        compiler_params=_cparams("arbitrary"),
    )(q, kvm, do)


def _rope_tables(positions, name):
    t = positions.shape[0]
    tt = _rows(t, TT)
    inv = 1.0 / (ROPE_THETA ** (np.arange(0, QK_ROPE, 2, dtype=np.float32) / QK_ROPE))
    lanes = np.zeros((1, HEAD_PAD), np.float32)
    half = QK_ROPE // 2
    lanes[0, KPE_LANE : KPE_LANE + half] = inv
    lanes[0, KPE_LANE + half : KPE_LANE + QK_ROPE] = inv

    def body(pos_ref, inv_ref, cos_ref, sa_ref, sb_ref):
        ang = pos_ref[...].astype(F32) * inv_ref[...]
        lane = lax.broadcasted_iota(jnp.int32, (tt, HEAD_PAD), 1)
        c = jnp.cos(ang)
        s = jnp.sin(ang)
        lo = (lane >= KPE_LANE) & (lane < KPE_LANE + half)
        hi = (lane >= KPE_LANE + half) & (lane < KPE_LANE + QK_ROPE)
        cos_ref[...] = jnp.where(lo | hi, c, 1.0)
        sa_ref[...] = jnp.where(hi, s, 0.0)
        sb_ref[...] = jnp.where(lo, -s, 0.0)

    tab = pl.BlockSpec((tt, HEAD_PAD), lambda i: (i, 0))
    return pl.pallas_call(
        body,
        grid=(t // tt,),
        in_specs=[pl.BlockSpec((tt, 1), lambda i: (i, 0)), pl.BlockSpec((1, HEAD_PAD), lambda i: (0, 0))],
        out_specs=[tab, tab, tab],
        out_shape=[jax.ShapeDtypeStruct((t, HEAD_PAD), F32)] * 3,
        name=name,
        compiler_params=_cparams("parallel"),
    )(positions, jnp.asarray(lanes))


def _rotate(x, cos, sa, sb, sign):
    half = QK_ROPE // 2
    return x * cos + sign * (pltpu.roll(x, half, 1) * sa + pltpu.roll(x, HEAD_PAD - half, 1) * sb)


def _rope_heads(x, tables, sign, scale, name):
    t, w = x.shape
    tt = _rows(t, TT)
    nh = w // HEAD_PAD

    def body(x_ref, c_ref, sa_ref, sb_ref, o_ref):
        cos, sa, sb = c_ref[...] * scale, sa_ref[...] * scale, sb_ref[...] * scale
        for h in range(nh):
            cs = pl.ds(h * HEAD_PAD, HEAD_PAD)
            o_ref[:, cs] = _rotate(x_ref[:, cs], cos, sa, sb, sign).astype(o_ref.dtype)

    tab = pl.BlockSpec((tt, HEAD_PAD), lambda i: (i, 0))
    row = pl.BlockSpec((tt, w), lambda i: (i, 0))
    return pl.pallas_call(
        body,
        grid=(t // tt,),
        in_specs=[row, tab, tab, tab],
        out_specs=row,
        out_shape=jax.ShapeDtypeStruct((t, w), MXU_DT),
        name=name,
        compiler_params=_cparams("parallel"),
    )(x, *tables)


def _matmul_rope(a, b, layer, tables, scale, name):
    m, k = a.shape
    n = b.shape[-1]
    tm = _rows(m, TT)
    nh = n // HEAD_PAD

    def body(a_ref, b_ref, c_ref, sa_ref, sb_ref, o_ref):
        q = jnp.dot(a_ref[...].astype(MXU_DT), b_ref[...].astype(MXU_DT), preferred_element_type=F32)
        cos, sa, sb = c_ref[...] * scale, sa_ref[...] * scale, sb_ref[...] * scale
        for h in range(nh):
            cs = slice(h * HEAD_PAD, (h + 1) * HEAD_PAD)
            o_ref[:, cs] = _rotate(q[:, cs], cos, sa, sb, 1.0).astype(o_ref.dtype)

    tab = pl.BlockSpec((tm, HEAD_PAD), lambda i: (i, 0))
    return pl.pallas_call(
        body,
        grid=(m // tm,),
        in_specs=[pl.BlockSpec((tm, k), lambda i: (i, 0)), pl.BlockSpec((None, k, n), lambda i: (layer, 0, 0)), tab, tab, tab],
        out_specs=pl.BlockSpec((tm, n), lambda i: (i, 0)),
        out_shape=jax.ShapeDtypeStruct((m, n), MXU_DT),
        name=name,
        compiler_params=_cparams("parallel"),
    )(a, b, *tables)


def _mla_prep(cp, qg, kvg, tables, name):
    t = cp.shape[0]
    tt = _rows(t, TT)

    def body(cp_ref, qg_ref, kvg_ref, c_ref, sa_ref, sb_ref, qn_ref, kvn_ref, kpe_ref):
        cq = cp_ref[:, 0:Q_LORA]
        r = lax.rsqrt(jnp.mean(cq * cq, axis=-1, keepdims=True) + EPS)
        qn_ref[...] = ((cq * r) * qg_ref[...]).astype(qn_ref.dtype)
        ckv = cp_ref[:, Q_LORA : Q_LORA + KV_LORA]
        r = lax.rsqrt(jnp.mean(ckv * ckv, axis=-1, keepdims=True) + EPS)
        kvn_ref[...] = ((ckv * r) * kvg_ref[...]).astype(kvn_ref.dtype)
        kpe = cp_ref[:, Q_LORA + KV_LORA : C_PAD]
        kpe_ref[...] = _rotate(kpe, c_ref[...], sa_ref[...], sb_ref[...], 1.0).astype(kpe_ref.dtype)

    tab = pl.BlockSpec((tt, HEAD_PAD), lambda i: (i, 0))
    return pl.pallas_call(
        body,
        grid=(t // tt,),
        in_specs=[
            pl.BlockSpec((tt, C_PAD), lambda i: (i, 0)),
            pl.BlockSpec((1, Q_LORA), lambda i: (0, 0)),
            pl.BlockSpec((1, KV_LORA), lambda i: (0, 0)),
            tab, tab, tab,
        ],
        out_specs=[
            pl.BlockSpec((tt, Q_LORA), lambda i: (i, 0)),
            pl.BlockSpec((tt, KV_LORA), lambda i: (i, 0)),
            tab,
        ],
        out_shape=[
            jax.ShapeDtypeStruct((t, Q_LORA), MXU_DT),
            jax.ShapeDtypeStruct((t, KV_LORA), MXU_DT),
            jax.ShapeDtypeStruct((t, HEAD_PAD), MXU_DT),
        ],
        name=name,
        compiler_params=_cparams("parallel"),
    )(cp, qg.reshape(1, Q_LORA), kvg.reshape(1, KV_LORA), *tables)


def _mla_prep_bwd(cp, dqn, dkvn, dkpe_heads, qg, kvg, tables, name):
    t = cp.shape[0]
    tt = _rows(t, TT)

    def norm_bwd(x, dy, g):
        r = lax.rsqrt(jnp.mean(x * x, axis=-1, keepdims=True) + EPS)
        xh = x * r
        gy = dy * g
        return r * (gy - xh * jnp.mean(gy * xh, axis=-1, keepdims=True)), jnp.sum(dy * xh, axis=0, keepdims=True)

    def body(cp_ref, dqn_ref, dkvn_ref, dkpe_ref, qg_ref, kvg_ref, c_ref, sa_ref, sb_ref, dcp_ref, dqg_ref, dkvg_ref):
        @pl.when(pl.program_id(0) == 0)
        def _():
            dqg_ref[...] = jnp.zeros_like(dqg_ref)
            dkvg_ref[...] = jnp.zeros_like(dkvg_ref)

        dcq, dg = norm_bwd(cp_ref[:, 0:Q_LORA], dqn_ref[...], qg_ref[...])
        dcp_ref[:, 0:Q_LORA] = dcq.astype(dcp_ref.dtype)
        dqg_ref[...] += dg
        dckv, dg = norm_bwd(cp_ref[:, Q_LORA : Q_LORA + KV_LORA], dkvn_ref[...], kvg_ref[...])
        dcp_ref[:, Q_LORA : Q_LORA + KV_LORA] = dckv.astype(dcp_ref.dtype)
        dkvg_ref[...] += dg
        dk = dkpe_ref[0]
        for h in range(1, MLA_HEADS):
            dk = dk + dkpe_ref[h]
        dcp_ref[:, Q_LORA + KV_LORA : C_PAD] = _rotate(dk, c_ref[...], sa_ref[...], sb_ref[...], -1.0).astype(dcp_ref.dtype)

    tab = pl.BlockSpec((tt, HEAD_PAD), lambda i: (i, 0))
    return pl.pallas_call(
        body,
        grid=(t // tt,),
        in_specs=[
            pl.BlockSpec((tt, C_PAD), lambda i: (i, 0)),
            pl.BlockSpec((tt, Q_LORA), lambda i: (i, 0)),
            pl.BlockSpec((tt, KV_LORA), lambda i: (i, 0)),
            pl.BlockSpec((MLA_HEADS, tt, HEAD_PAD), lambda i: (0, i, 0)),
            pl.BlockSpec((1, Q_LORA), lambda i: (0, 0)),
            pl.BlockSpec((1, KV_LORA), lambda i: (0, 0)),
            tab, tab, tab,
        ],
        out_specs=[
            pl.BlockSpec((tt, C_PAD), lambda i: (i, 0)),
            pl.BlockSpec((1, Q_LORA), lambda i: (0, 0)),
            pl.BlockSpec((1, KV_LORA), lambda i: (0, 0)),
        ],
        out_shape=[
            jax.ShapeDtypeStruct((t, C_PAD), MXU_DT),
            jax.ShapeDtypeStruct((1, Q_LORA), F32),
            jax.ShapeDtypeStruct((1, KV_LORA), F32),
        ],
        name=name,
        compiler_params=_cparams("arbitrary"),
    )(cp, dqn, dkvn, dkpe_heads, qg.reshape(1, Q_LORA), kvg.reshape(1, KV_LORA), *tables)


def _flash_fwd(qs, kv, kpe, name):
    t = qs.shape[0]
    ta = _rows(t, TA)
    tq = ta
    nq = t // tq
    sub = ta // 2

    def body(q_ref, kv_ref, kpe_ref, o_ref, lse_ref):
        qi = pl.program_id(1)
        q = q_ref[...]
        lane = lax.broadcasted_iota(jnp.int32, (ta, HEAD_PAD), 1)

        def kblock(j):
            rows = pl.ds(pl.multiple_of(j * ta, ta), ta)
            kvb = kv_ref[rows, :]
            ones_v = jnp.where(lane < QK_NOPE, jnp.ones_like(kvb), kvb)
            return ones_v, jnp.where(lane < QK_NOPE, kvb, kpe_ref[rows, :])

        def update(carry, s, ones_v):
            m, acc = carry
            m_new = jnp.maximum(m, jnp.max(s, axis=-1, keepdims=True))
            p = jnp.exp2(s - m_new).astype(MXU_DT)
            acc = jnp.exp2(m - m_new) * acc + jnp.dot(p, ones_v, preferred_element_type=F32)
            return m_new, acc

        def step(j, carry):
            ones_v, k = kblock(j)
            return update(carry, lax.dot_general(q, k, NT, preferred_element_type=F32), ones_v)

        init = (jnp.full((tq, 1), -jnp.inf, F32), jnp.zeros((tq, HEAD_PAD), F32))
        m_all, acc_all = lax.fori_loop(0, qi, step, init)
        ones_v, k = kblock(qi)
        lane_h = lax.broadcasted_iota(jnp.int32, (sub, HEAD_PAD), 1)
        for b in range(2):
            rows, nk = slice(b * sub, (b + 1) * sub), (b + 1) * sub
            s = lax.dot_general(q[rows], k[0:nk], NT, preferred_element_type=F32)
            r = lax.broadcasted_iota(jnp.int32, (sub, nk), 0) + b * sub
            c = lax.broadcasted_iota(jnp.int32, (sub, nk), 1)
            m, acc = update((m_all[rows], acc_all[rows]), jnp.where(c <= r, s, NEG), ones_v[0:nk])
            l = acc[:, 0:1]
            o_ref[rows, :] = jnp.where(lane_h >= QK_NOPE, acc / l, 0.0).astype(o_ref.dtype)
            lse_ref[rows, :] = m + jnp.log2(l)

    return pl.pallas_call(
        body,
        grid=(MLA_HEADS, nq),
        in_specs=[
            pl.BlockSpec((tq, HEAD_PAD), lambda h, i: (i, h)),
            pl.BlockSpec((t, HEAD_PAD), lambda h, i: (0, h)),
            pl.BlockSpec((t, HEAD_PAD), lambda h, i: (0, 0)),
        ],
        out_specs=[
            pl.BlockSpec((tq, HEAD_PAD), lambda h, i: (i, h)),
            pl.BlockSpec((None, tq, 1), lambda h, i: (h, i, 0)),
        ],
        out_shape=[
            jax.ShapeDtypeStruct((t, MLA_HEADS * HEAD_PAD), MXU_DT),
            jax.ShapeDtypeStruct((MLA_HEADS, t, 1), F32),
        ],
        name=name,
        compiler_params=_cparams("parallel", "parallel"),
    )(qs, kv, kpe)


def _flash_bwd(qs, kv, kpe, o, do, lse, name):
    t = qs.shape[0]
    ta = _rows(t, TA)
    tq = ta
    nq = t // ta
    sub = ta // 2

    def body(q_ref, o_ref, do_ref, lse_ref, kv_ref, kpe_ref, dq_ref, dkv_ref, dkpe_ref, dk_acc, dv_acc):
        kj = pl.program_id(1)

        @pl.when(kj == 0)
        def _():
            dq_ref[...] = jnp.zeros_like(dq_ref)

        lane = lax.broadcasted_iota(jnp.int32, (ta, HEAD_PAD), 1)
        kvb = kv_ref[...]
        k = jnp.where(lane < QK_NOPE, kvb, kpe_ref[...])
        dk_acc[...] = jnp.zeros_like(dk_acc)
        dv_acc[...] = jnp.zeros_like(dv_acc)

        def tile(row0, nrows, nkeys, diagonal):
            rows = pl.ds(pl.multiple_of(row0, sub), nrows)
            keys = slice(0, nkeys)
            q = q_ref[rows, :]
            dob = do_ref[rows, :]
            delta = jnp.sum(dob.astype(F32) * o_ref[rows, :].astype(F32), axis=-1, keepdims=True)
            s = lax.dot_general(q, k[keys], NT, preferred_element_type=F32)
            if diagonal:
                r = lax.broadcasted_iota(jnp.int32, (nrows, nkeys), 0) + (nkeys - nrows)
                c = lax.broadcasted_iota(jnp.int32, (nrows, nkeys), 1)
                s = jnp.where(c <= r, s, NEG)
            p = jnp.exp2(s - lse_ref[rows, :])
            dp = lax.dot_general(dob, kvb[keys], NT, preferred_element_type=F32)
            ds = (p * (dp - delta)).astype(MXU_DT)
            dq_ref[rows, :] += jnp.dot(ds, k[keys], preferred_element_type=F32)
            dk_acc[keys, :] += lax.dot_general(ds, q, TN_DIMS, preferred_element_type=F32)
            dv_acc[keys, :] += lax.dot_general(p.astype(MXU_DT), dob, TN_DIMS, preferred_element_type=F32)

        tile(kj * ta, sub, sub, True)
        tile(kj * ta + sub, sub, ta, True)

        def step(qq, carry):
            tile(qq * tq, tq, ta, False)
            return carry

        lax.fori_loop(kj + 1, t // tq, step, 0)
        dk = dk_acc[...] * (1.0 / LOG2E)
        dkv_ref[...] = jnp.where(lane < QK_NOPE, dk, dv_acc[...]).astype(dkv_ref.dtype)
        dkpe_ref[...] = jnp.where((lane >= KPE_LANE) & (lane < KPE_LANE + QK_ROPE), dk, 0.0)

    head_rows = pl.BlockSpec((t, HEAD_PAD), lambda h, j: (0, h))
    return pl.pallas_call(
        body,
        grid=(MLA_HEADS, nq),
        in_specs=[
            head_rows,
            head_rows,
            head_rows,
            pl.BlockSpec((None, t, 1), lambda h, j: (h, 0, 0)),
            pl.BlockSpec((ta, HEAD_PAD), lambda h, j: (j, h)),
            pl.BlockSpec((ta, HEAD_PAD), lambda h, j: (j, 0)),
        ],
        out_specs=[
            head_rows,
            pl.BlockSpec((ta, HEAD_PAD), lambda h, j: (j, h)),
            pl.BlockSpec((None, ta, HEAD_PAD), lambda h, j: (h, j, 0)),
        ],
        out_shape=[
            jax.ShapeDtypeStruct((t, MLA_HEADS * HEAD_PAD), F32),
            jax.ShapeDtypeStruct((t, MLA_HEADS * HEAD_PAD), MXU_DT),
            jax.ShapeDtypeStruct((MLA_HEADS, t, HEAD_PAD), F32),
        ],
        scratch_shapes=[pltpu.VMEM((ta, HEAD_PAD), F32), pltpu.VMEM((ta, HEAD_PAD), F32)],
        name=name,
        compiler_params=_cparams("parallel", "arbitrary"),
    )(qs, o, do, lse, kv, kpe)


def _as2d(a):
    if a.ndim == 1:
        return a.reshape(1, a.shape[0])
    return a.reshape(-1, a.shape[-1])


def _adamw(w, g, m, v, name):
    shape = w.shape
    w2, g2, m2, v2 = (_as2d(a) for a in (w, g, m, v))
    r, c = w2.shape
    tr = _tile_rows(r, c)
    c1 = 1.0 - ADAM_B1 ** ADAM_STEP
    c2 = 1.0 - ADAM_B2 ** ADAM_STEP

    def body(w_ref, g_ref, m_ref, v_ref, d_ref, nm_ref, nv_ref):
        gv = g_ref[...]
        nm = ADAM_B1 * m_ref[...] + (1.0 - ADAM_B1) * gv
        nv = ADAM_B2 * v_ref[...] + (1.0 - ADAM_B2) * (gv * gv)
        d_ref[...] = -ADAM_LR * ((nm / c1) / (jnp.sqrt(nv / c2) + ADAM_EPS) + ADAM_WD * w_ref[...])
        nm_ref[...] = nm
        nv_ref[...] = nv

    blk = pl.BlockSpec((tr, c), lambda i: (i, 0))
    outs = pl.pallas_call(
        body,
        grid=(r // tr,),
        in_specs=[blk] * 4,
        out_specs=[blk] * 3,
        out_shape=[jax.ShapeDtypeStruct((r, c), F32)] * 3,
        name=name,
        compiler_params=_cparams("parallel"),
    )(w2, g2, m2, v2)
    return tuple(o.reshape(shape) for o in outs)


def _adamw_halves(w, mine, other, m, v, c_idx, name):
    nl, r, c = w.shape
    h = nl // 2
    tr = _tile_rows(r, c)
    c1 = 1.0 - ADAM_B1 ** ADAM_STEP
    c2 = 1.0 - ADAM_B2 ** ADAM_STEP

    def body(c_ref, w_ref, a_ref, b_ref, m_ref, v_ref, g_ref, d_ref, nm_ref, nv_ref):
        l = pl.program_id(0)
        gv = jnp.where(l // h == c_ref[0], a_ref[...], b_ref[...])
        nm = ADAM_B1 * m_ref[...] + (1.0 - ADAM_B1) * gv
        nv = ADAM_B2 * v_ref[...] + (1.0 - ADAM_B2) * (gv * gv)
        g_ref[...] = gv
        d_ref[...] = -ADAM_LR * ((nm / c1) / (jnp.sqrt(nv / c2) + ADAM_EPS) + ADAM_WD * w_ref[...])
        nm_ref[...] = nm
        nv_ref[...] = nv

    def half_map(mine_side):
        def index(l, i, cr):
            first = cr[0] * h if mine_side else (1 - cr[0]) * h
            return (jnp.clip(l - first, 0, h - 1), i, 0)
        return index

    full = pl.BlockSpec((None, tr, c), lambda l, i, cr: (l, i, 0))
    grid_spec = pltpu.PrefetchScalarGridSpec(
        num_scalar_prefetch=1,
        grid=(nl, r // tr),
        in_specs=[full, pl.BlockSpec((None, tr, c), half_map(True)), pl.BlockSpec((None, tr, c), half_map(False)), full, full],
        out_specs=[full] * 4,
    )
    return pl.pallas_call(
        body,
        grid_spec=grid_spec,
        out_shape=[jax.ShapeDtypeStruct((nl, r, c), F32)] * 4,
        name=name,
        compiler_params=_cparams("parallel", "parallel"),
    )(c_idx, w, mine, other, m, v)


def _tile_rows(r, c, mult=SUBLANE):
    limit = max(mult, (BLOCK_BYTES // 4) // (4 * c))
    if r <= limit:
        return r
    t = (limit // mult) * mult
    while t >= mult:
        if r % t == 0:
            return t
        t -= mult
    return r


def _sum_leading(a, name):
    n, r, c = a.shape
    tr = _tile_rows(r, c * n)

    def body(a_ref, o_ref):
        s = a_ref[0]
        for k in range(1, n):
            s = s + a_ref[k]
        o_ref[...] = s

    return pl.pallas_call(
        body,
        grid=(r // tr,),
        in_specs=[pl.BlockSpec((n, tr, c), lambda i: (0, i, 0))],
        out_specs=pl.BlockSpec((tr, c), lambda i: (i, 0)),
        out_shape=jax.ShapeDtypeStruct((r, c), F32),
        name=name,
        compiler_params=_cparams("parallel"),
    )(a)


def _add_half(g, s, c_idx, name):
    nl, r, c = g.shape
    h = nl // 2
    tr = _tile_rows(r, c, 2 * SUBLANE)

    def body(c_ref, g_ref, s_ref, o_ref):
        o_ref[...] = (g_ref[...] + s_ref[...]).astype(o_ref.dtype)

    grid_spec = pltpu.PrefetchScalarGridSpec(
        num_scalar_prefetch=1,
        grid=(h, r // tr),
        in_specs=[
            pl.BlockSpec((None, tr, c), lambda l, i, cr: (cr[0] * h + l, i, 0)),
            pl.BlockSpec((None, tr, c), lambda l, i, cr: (l, i, 0)),
        ],
        out_specs=pl.BlockSpec((None, tr, c), lambda l, i, cr: (l, i, 0)),
    )
    return pl.pallas_call(
        body,
        grid_spec=grid_spec,
        out_shape=jax.ShapeDtypeStruct((h, r, c), XFER_DT),
        name=name,
        compiler_params=_cparams("parallel", "parallel"),
    )(c_idx, g, s)


def _sum_chips(slots, pair, chip_idx, kind, name):
    _, h, r, c = slots.shape
    tr = _tile_rows(r, c, 2 * SUBLANE)
    nr = r // tr

    def body(chip_ref, s_ref, own_ref, o_ref):
        chip = chip_ref[0]
        own = own_ref[...].astype(F32)
        parts = [s_ref[j].astype(F32) for j in range(3)]
        total = None
        for k in range(4):
            d = jnp.bitwise_xor(chip, k)
            v = jnp.where(d == 0, own, jnp.where(d == 2, parts[0], jnp.where(d == 1, parts[1], parts[2])))
            total = v if total is None else total + v
        o_ref[...] = total

    if kind == "row":
        own_spec = pl.BlockSpec((None, tr, c), lambda l, i, cr: (l, cr[0] * nr + i, 0))
    else:
        own_spec = pl.BlockSpec((None, tr, c), lambda l, i, cr: (l, i, cr[0]))
    grid_spec = pltpu.PrefetchScalarGridSpec(
        num_scalar_prefetch=1,
        grid=(h, nr),
        in_specs=[pl.BlockSpec((3, None, tr, c), lambda l, i, cr: (0, l, i, 0)), own_spec],
        out_specs=pl.BlockSpec((None, tr, c), lambda l, i, cr: (l, i, 0)),
    )
    return pl.pallas_call(
        body,
        grid_spec=grid_spec,
        out_shape=jax.ShapeDtypeStruct((h, r, c), F32),
        name=name,
        compiler_params=_cparams("parallel", "parallel"),
    )(chip_idx, slots, pair)


def _mesh_pos():
    return lax.axis_index("x"), lax.axis_index("y"), lax.axis_index("c")


def _other_chips(x, y):
    return [(1 - x, y), (x, 1 - y), (1 - x, 1 - y)]


def _all_gather_rows(block, name):
    m_per, n = block.shape

    def body(x_ref, out_ref, send_sems, recv_sems, local_sem):
        x, y, c = _mesh_pos()
        me, sibling = (x, y, c), (x, y, 1 - c)
        chips = _other_chips(x, y)

        def rows(px, py, pc):
            return out_ref.at[pl.ds((4 * px + 2 * py + pc) * m_per, m_per), :]

        def copy(k, blk, to, src=None):
            return pltpu.make_async_remote_copy(
                src_ref=rows(*blk) if src is None else src,
                dst_ref=rows(*blk),
                send_sem=send_sems.at[k],
                recv_sem=recv_sems.at[k],
                device_id=to,
                device_id_type=MESH_ID,
            )

        mine = pltpu.make_async_copy(x_ref, rows(*me), local_sem)
        mine.start()
        first = [copy(0, me, sibling, src=x_ref)]
        first += [copy(1 + j, me, (*chip, c), src=x_ref) for j, chip in enumerate(chips)]
        for cp in first:
            cp.start()
        passed = [copy(4 + j, (*chip, c), sibling) for j, chip in enumerate(chips)]
        for j, chip in enumerate(chips):
            copy(1 + j, (*chip, c), me).wait_recv()
            passed[j].start()
        copy(0, sibling, me).wait_recv()
        for j, chip in enumerate(chips):
            copy(4 + j, (*chip, 1 - c), me).wait_recv()
        for cp in first + passed:
            cp.wait_send()
        mine.wait()

    return pl.pallas_call(
        body,
        out_shape=jax.ShapeDtypeStruct((8 * m_per, n), block.dtype),
        in_specs=[pl.BlockSpec(memory_space=pltpu.VMEM)],
        out_specs=pl.BlockSpec(memory_space=pltpu.VMEM),
        scratch_shapes=[pltpu.SemaphoreType.DMA((7,)), pltpu.SemaphoreType.DMA((7,)), pltpu.SemaphoreType.DMA],
        name=name,
        compiler_params=pltpu.CompilerParams(vmem_limit_bytes=VMEM_LIMIT),
    )(block)


def _shard_window(ref, layers, chip, rows, cols):
    if rows is not None:
        return ref.at[layers, pl.ds(pl.multiple_of(chip * rows, rows), rows), :]
    return ref.at[layers, :, pl.ds(pl.multiple_of(chip * cols, cols), cols)]


def _all_gather_weights(shards, kinds, name):
    nw = len(shards)
    out_shapes = []
    for s, kind in zip(shards, kinds):
        nl, r, c = s.shape
        full = (nl, 4 * r, c) if kind == "row" else (nl, r, 4 * c)
        out_shapes.append(jax.ShapeDtypeStruct(full, s.dtype))

    def body(*refs):
        ins, outs = refs[:nw], refs[nw : 2 * nw]
        send_sems, recv_sems, in_sems, out_sems = refs[2 * nw : 2 * nw + 4]
        bufs = refs[2 * nw + 4 :]
        x, y, c = _mesh_pos()
        sibling = (x, y, 1 - c)
        chips = _other_chips(x, y)
        my_chip = 2 * x + y

        def window(w, chip, layers):
            _, r, cc = shards[w].shape
            if kinds[w] == "row":
                return _shard_window(outs[w], layers, chip, r, None)
            return _shard_window(outs[w], layers, chip, None, cc)

        def half(w, half_idx):
            h = shards[w].shape[0] // 2
            return pl.ds(half_idx * h, h)

        def copy(w, k, src, dst, to):
            return pltpu.make_async_remote_copy(
                src_ref=src, dst_ref=dst, send_sem=send_sems.at[w, k], recv_sem=recv_sems.at[w, k],
                device_id=to, device_id_type=MESH_ID)

        sent = []
        for w in range(nw):
            mine = ins[w].at[half(w, c)]
            for j, chip in enumerate(chips):
                cp = copy(w, j, mine, window(w, my_chip, half(w, c)), (*chip, c))
                cp.start()
                sent.append(cp)
        for w in range(nw):
            nl = shards[w].shape[0]

            def load(l, w=w):
                return pltpu.make_async_copy(ins[w].at[l], bufs[w].at[l % 2], in_sems.at[w, l % 2])

            def store(l, w=w):
                return pltpu.make_async_copy(bufs[w].at[l % 2], window(w, my_chip, l), out_sems.at[w, l % 2])

            load(0).start()
            for l in range(nl):
                load(l).wait()
                store(l).start()
                if l + 1 < nl:
                    if l >= 1:
                        store(l - 1).wait()
                    load(l + 1).start()
            for l in range(max(nl - 2, 0), nl):
                store(l).wait()
        for w in range(nw):
            for j, (cx, cy) in enumerate(chips):
                got = window(w, 2 * cx + cy, half(w, c))
                copy(w, j, got, got, (cx, cy, c)).wait_recv()
                cp = copy(w, 3 + j, got, got, sibling)
                cp.start()
                sent.append(cp)
        for w in range(nw):
            for j, (cx, cy) in enumerate(chips):
                got = window(w, 2 * cx + cy, half(w, 1 - c))
                copy(w, 3 + j, got, got, sibling).wait_recv()
        for cp in sent:
            cp.wait_send()

    anyspec = pl.BlockSpec(memory_space=pl.ANY)
    return pl.pallas_call(
        body,
        out_shape=out_shapes,
        in_specs=[anyspec] * nw,
        out_specs=[anyspec] * nw,
        scratch_shapes=[pltpu.SemaphoreType.DMA((nw, 6)), pltpu.SemaphoreType.DMA((nw, 6)),
                        pltpu.SemaphoreType.DMA((nw, 2)), pltpu.SemaphoreType.DMA((nw, 2))]
        + [pltpu.VMEM((2,) + s.shape[1:], s.dtype) for s in shards],
        name=name,
        compiler_params=pltpu.CompilerParams(vmem_limit_bytes=VMEM_LIMIT),
    )(*shards)


def _exchange_halves(grads, name):
    nw = len(grads)
    out_shapes = [jax.ShapeDtypeStruct((g.shape[0] // 2,) + g.shape[1:], g.dtype) for g in grads]

    def body(*refs):
        ins, outs = refs[:nw], refs[nw : 2 * nw]
        send_sems, recv_sems = refs[2 * nw :]
        x, y, c = _mesh_pos()
        cps = []
        for w in range(nw):
            h = grads[w].shape[0] // 2
            cp = pltpu.make_async_remote_copy(
                src_ref=ins[w].at[pl.ds((1 - c) * h, h)], dst_ref=outs[w], send_sem=send_sems.at[w],
                recv_sem=recv_sems.at[w], device_id=(x, y, 1 - c), device_id_type=MESH_ID)
            cp.start()
            cps.append(cp)
        for cp in cps:
            cp.wait()

    anyspec = pl.BlockSpec(memory_space=pl.ANY)
    return pl.pallas_call(
        body,
        out_shape=out_shapes,
        in_specs=[anyspec] * nw,
        out_specs=[anyspec] * nw,
        scratch_shapes=[pltpu.SemaphoreType.DMA((nw,)), pltpu.SemaphoreType.DMA((nw,))],
        name=name,
    )(*grads)


def _scatter_to_chips(parts, kinds, name):
    nw = len(parts)
    shard_shapes = []
    for p, kind in zip(parts, kinds):
        h, r, c = p.shape
        shard_shapes.append((h, r // 4, c) if kind == "row" else (h, r, c // 4))
    out_shapes = [jax.ShapeDtypeStruct((3,) + s, p.dtype) for s, p in zip(shard_shapes, parts)]

    def body(*refs):
        ins, outs = refs[:nw], refs[nw : 2 * nw]
        send_sems, recv_sems = refs[2 * nw :]
        x, y, c = _mesh_pos()
        chips = _other_chips(x, y)

        def piece(w, chip):
            h, r, cc = shard_shapes[w]
            if kinds[w] == "row":
                return _shard_window(ins[w], pl.ds(0, h), chip, r, None)
            return _shard_window(ins[w], pl.ds(0, h), chip, None, cc)

        def copy(w, j, cx, cy):
            return pltpu.make_async_remote_copy(
                src_ref=piece(w, 2 * cx + cy), dst_ref=outs[w].at[j], send_sem=send_sems.at[w, j],
                recv_sem=recv_sems.at[w, j], device_id=(cx, cy, c), device_id_type=MESH_ID)

        cps = [copy(w, j, cx, cy) for w in range(nw) for j, (cx, cy) in enumerate(chips)]
        for cp in cps:
            cp.start()
        for cp in cps:
            cp.wait()

    anyspec = pl.BlockSpec(memory_space=pl.ANY)
    return pl.pallas_call(
        body,
        out_shape=out_shapes,
        in_specs=[anyspec] * nw,
        out_specs=[anyspec] * nw,
        scratch_shapes=[pltpu.SemaphoreType.DMA((nw, 3)), pltpu.SemaphoreType.DMA((nw, 3))],
        name=name,
    )(*parts)


def _swap_halves(halves, name):
    nw = len(halves)
    out_shapes = [jax.ShapeDtypeStruct(p.shape, p.dtype) for p in halves]

    def body(*refs):
        ins, outs = refs[:nw], refs[nw : 2 * nw]
        send_sems, recv_sems = refs[2 * nw :]
        x, y, c = _mesh_pos()
        cps = [pltpu.make_async_remote_copy(
            src_ref=ins[w], dst_ref=outs[w], send_sem=send_sems.at[w], recv_sem=recv_sems.at[w],
            device_id=(x, y, 1 - c), device_id_type=MESH_ID) for w in range(nw)]
        for cp in cps:
            cp.start()
        for cp in cps:
            cp.wait()

    anyspec = pl.BlockSpec(memory_space=pl.ANY)
    return pl.pallas_call(
        body,
        out_shape=out_shapes,
        in_specs=[anyspec] * nw,
        out_specs=[anyspec] * nw,
        scratch_shapes=[pltpu.SemaphoreType.DMA((nw,)), pltpu.SemaphoreType.DMA((nw,))],
        name=name,
    )(*halves)


def _pad_wdq(w):
    z = lambda n: jnp.zeros((w.shape[0], n), w.dtype)
    base = Q_LORA + KV_LORA
    return jnp.concatenate([w[:, :base], z(KPE_LANE), w[:, base:], z(HEAD_PAD - KPE_LANE - QK_ROPE)], axis=1)


def _unpad_wdq(g):
    base = Q_LORA + KV_LORA
    return jnp.concatenate([g[:, :base], g[:, base + KPE_LANE : base + KPE_LANE + QK_ROPE]], axis=1)


def _pad_wuq(w):
    w3 = w.reshape(Q_LORA, MLA_HEADS, QK_NOPE + QK_ROPE)
    w3 = jnp.pad(w3, ((0, 0), (0, 0), (0, HEAD_PAD - QK_NOPE - QK_ROPE)))
    return w3.reshape(Q_LORA, MLA_HEADS * HEAD_PAD)


def _unpad_wuq(g):
    g3 = g.reshape(Q_LORA, MLA_HEADS, HEAD_PAD)[:, :, : QK_NOPE + QK_ROPE]
    return g3.reshape(Q_LORA, MLA_HEADS * (QK_NOPE + QK_ROPE))


def _pad_wo(w):
    w3 = w.reshape(MLA_HEADS, V_HEAD, D_MODEL)
    w3 = jnp.pad(w3, ((0, 0), (HEAD_PAD - V_HEAD, 0), (0, 0)))
    return w3.reshape(MLA_HEADS * HEAD_PAD, D_MODEL)


def _unpad_wo(g):
    g3 = g.reshape(MLA_HEADS, HEAD_PAD, D_MODEL)[:, HEAD_PAD - V_HEAD :, :]
    return g3.reshape(MLA_HEADS * V_HEAD, D_MODEL)


def _local_step(x, mem, positions, target, wb, ws):
    t = x.shape[0]
    tables = _rope_tables(positions.reshape(t, 1), "rope_tables")
    saved = []
    h1 = _rms_fwd(x, ws["norm_mix_g"][0], "l0_norm_mix")
    for l in range(DEPTH):
        s = {"x0": x}
        s["h1"] = h1
        if l % 2 == 0:
            e = l // 2
            z = _matmul(h1, wb["pc_w_in"], "nn", F32, f"l{l}_pc_in", layer=e)
            cat = _mix_fwd(z, ws["pool_w"][e], ws["pool_scale"][e], ws["conv_dw_w"][e], ws["conv_dw_b"][e],
                           ws["conv_ln_g"][e], ws["conv_ln_b"][e], f"l{l}_mix")
            x, h2 = _matmul_res_norm(cat, wb["pc_w_out"], e, x, ws["norm_xa_g"][l], f"l{l}_pc_out")
            s.update(z=z, cat=cat)
        else:
            o = l // 2
            cp = _matmul(h1, wb["mla_wdq"], "nn", F32, f"l{l}_mla_dq", layer=o)
            qn, kvn, kpe = _mla_prep(cp, ws["mla_q_norm_g"][o], ws["mla_kv_norm_g"][o], tables, f"l{l}_mla_prep")
            qr = _matmul_rope(qn, wb["mla_wuq"], o, tables, MLA_SCALE * LOG2E, f"l{l}_mla_uq")
            kv = _matmul(kvn, wb["mla_w_ukv"], "nn", MXU_DT, f"l{l}_mla_ukv", layer=o)
            att, lse = _flash_fwd(qr, kv, kpe, f"l{l}_mla_attn")
            x, h2 = _matmul_res_norm(att, wb["mla_wo"], o, x, ws["norm_xa_g"][l], f"l{l}_mla_o")
            s.update(cp=cp, qn=qn, kvn=kvn, kpe=kpe, qr=qr, kv=kv, att=att, lse=lse)
        s["x1"] = x
        hm = _rms_fwd(mem, ws["norm_mem_g"][l], f"l{l}_norm_mem")
        q2 = _matmul(h2, wb["xa_wq"], "nn", MXU_DT, f"l{l}_xa_q", layer=l)
        kvm = _matmul(hm, wb["xa_wkv"], "nn", MXU_DT, f"l{l}_xa_kv", layer=l)
        o2 = _xa_fwd(q2, kvm, f"l{l}_xa_attn")
        x, h3 = _matmul_res_norm(o2, wb["xa_wo"], l, x, ws["norm_ffn_g"][l], f"l{l}_xa_o")
        s.update(h2=h2, hm=hm, q2=q2, kvm=kvm, o2=o2, x2=x)
        up = _matmul(h3, wb["ffn_w_up"], "nn", F32, f"l{l}_ffn_up", layer=l)
        act = _ffn_fwd(up, ws["ffn_conv_w"][l], ws["ffn_conv_b"][l], f"l{l}_ffn_mid")
        if l + 1 < DEPTH:
            x, h1 = _matmul_res_norm(act, wb["ffn_w_down"], l, x, ws["norm_mix_g"][l + 1], f"l{l}_ffn_down")
        else:
            x = _matmul(act, wb["ffn_w_down"], "nn", F32, f"l{l}_ffn_down", layer=l, res=x)
        s.update(h3=h3, up=up, act=act)
        saved.append(s)

    dx, dx16, dg_final, loss = _loss_head(x, target, ws["final_norm_g"], "loss_head")
    g = {k: [None] * DEPTH for k in ("norm_mix_g", "norm_xa_g", "norm_mem_g", "xa_wq", "xa_wkv", "xa_wo", "norm_ffn_g",
                                      "ffn_w_up", "ffn_conv_w", "ffn_conv_b", "ffn_w_down")}
    g.update({k: [None] * (DEPTH // 2) for k in ("pc_w_in", "pool_w", "pool_scale", "conv_dw_w", "conv_dw_b", "conv_ln_g",
                                                 "conv_ln_b", "pc_w_out", "mla_w_dq_dkv", "mla_q_norm_g", "mla_w_uq",
                                                 "mla_kv_norm_g", "mla_w_ukv", "mla_w_o")})
    stk = {k: None for k in ("xa_wq", "xa_wkv", "xa_wo", "ffn_w_up", "ffn_w_down", "pc_w_in", "pc_w_out")}
    for l in reversed(range(DEPTH)):
        s = saved[l]
        dact = _matmul(dx16, wb["ffn_w_down"], "nt", F32, f"l{l}_b_ffn_dact", layer=l)
        stk["ffn_w_down"] = _matmul(s["act"], dx16, "tn", F32, f"l{l}_b_ffn_dwdown", stack=(stk["ffn_w_down"], l, DEPTH))
        dup, dcw, dcb = _ffn_bwd(s["up"], dact, ws["ffn_conv_w"][l], ws["ffn_conv_b"][l], f"l{l}_b_ffn_mid")
        g["ffn_conv_w"][l], g["ffn_conv_b"][l] = dcw, dcb[0]
        stk["ffn_w_up"] = _matmul(s["h3"], dup, "tn", F32, f"l{l}_b_ffn_dwup", stack=(stk["ffn_w_up"], l, DEPTH))
        dx, dx16, dg = _matmul_rms_bwd(dup, wb["ffn_w_up"], l, s["x2"], ws["norm_ffn_g"][l], dx, f"l{l}_b_ffn_dh")
        g["norm_ffn_g"][l] = dg[0]
        do2 = _matmul(dx16, wb["xa_wo"], "nt", MXU_DT, f"l{l}_b_xa_do", layer=l)
        stk["xa_wo"] = _matmul(s["o2"], dx16, "tn", F32, f"l{l}_b_xa_dwo", stack=(stk["xa_wo"], l, DEPTH))
        dq2, dkvm = _xa_bwd(s["q2"], s["kvm"], do2, f"l{l}_b_xa_attn")
        stk["xa_wq"] = _matmul(s["h2"], dq2, "tn", F32, f"l{l}_b_xa_dwq", stack=(stk["xa_wq"], l, DEPTH))
        stk["xa_wkv"] = _matmul(s["hm"], dkvm, "tn", F32, f"l{l}_b_xa_dwkv", stack=(stk["xa_wkv"], l, DEPTH))
        dhm = _matmul(dkvm, wb["xa_wkv"], "nt", F32, f"l{l}_b_xa_dhm", layer=l)
        g["norm_mem_g"][l] = _rms_bwd_gain(dhm, mem, ws["norm_mem_g"][l], f"l{l}_b_norm_mem")[0]
        dx, dx16, dg = _matmul_rms_bwd(dq2, wb["xa_wq"], l, s["x1"], ws["norm_xa_g"][l], dx, f"l{l}_b_xa_dh")
        g["norm_xa_g"][l] = dg[0]
        if l % 2 == 0:
            e = l // 2
            dcat = _matmul(dx16, wb["pc_w_out"], "nt", F32, f"l{l}_b_pc_dcat", layer=e)
            stk["pc_w_out"] = _matmul(s["cat"], dx16, "tn", F32, f"l{l}_b_pc_dwout", stack=(stk["pc_w_out"], e, DEPTH // 2))
            dz, dpw, dps, dww, dwb, dlg, dlb = _mix_bwd(
                s["z"], dcat, ws["pool_w"][e], ws["pool_scale"][e], ws["conv_dw_w"][e], ws["conv_dw_b"][e],
                ws["conv_ln_g"][e], ws["conv_ln_b"][e], f"l{l}_b_mix")
            g["pool_w"][e], g["pool_scale"][e], g["conv_dw_w"][e] = dpw, dps[0], dww
            g["conv_dw_b"][e], g["conv_ln_g"][e], g["conv_ln_b"][e] = dwb[0], dlg[0], dlb[0]
            stk["pc_w_in"] = _matmul(s["h1"], dz, "tn", F32, f"l{l}_b_pc_dwin", stack=(stk["pc_w_in"], e, DEPTH // 2))
            dx, dx16, dg = _matmul_rms_bwd(dz, wb["pc_w_in"], e, s["x0"], ws["norm_mix_g"][l], dx, f"l{l}_b_pc_dh")
        else:
            o = l // 2
            do = _matmul(dx16, wb["mla_wo"], "nt", MXU_DT, f"l{l}_b_mla_do", layer=o)
            g["mla_w_o"][o] = _unpad_wo(_matmul(s["att"], dx16, "tn", F32, f"l{l}_b_mla_dwo"))
            dqr, dkv, dkpe = _flash_bwd(s["qr"], s["kv"], s["kpe"], s["att"], do, s["lse"], f"l{l}_b_mla_attn")
            dq = _rope_heads(dqr, tables, -1.0, MLA_SCALE, f"l{l}_b_mla_rope")
            g["mla_w_uq"][o] = _unpad_wuq(_matmul(s["qn"], dq, "tn", F32, f"l{l}_b_mla_dwuq"))
            dqn = _matmul(dq, wb["mla_wuq"], "nt", F32, f"l{l}_b_mla_dqn", layer=o)
            g["mla_w_ukv"][o] = _matmul(s["kvn"], dkv, "tn", F32, f"l{l}_b_mla_dwukv")
            dkvn = _matmul(dkv, wb["mla_w_ukv"], "nt", F32, f"l{l}_b_mla_dkvn", layer=o)
            dcp, dqg, dkvg = _mla_prep_bwd(s["cp"], dqn, dkvn, dkpe, ws["mla_q_norm_g"][o], ws["mla_kv_norm_g"][o],
                                           tables, f"l{l}_b_mla_prep")
            g["mla_q_norm_g"][o], g["mla_kv_norm_g"][o] = dqg[0], dkvg[0]
            g["mla_w_dq_dkv"][o] = _unpad_wdq(_matmul(s["h1"], dcp, "tn", F32, f"l{l}_b_mla_dwdq"))
            dx, dx16, dg = _matmul_rms_bwd(dcp, wb["mla_wdq"], o, s["x0"], ws["norm_mix_g"][l], dx, f"l{l}_b_mla_dh")
        g["norm_mix_g"][l] = dg[0]
    grads = {k: jnp.stack(v) for k, v in g.items() if k not in stk}
    grads.update(stk)
    grads["final_norm_g"] = dg_final[0]
    return loss, dx, grads


BIG = (
    ("xa_wq", "row"), ("xa_wkv", "col"), ("xa_wo", "row"), ("ffn_w_up", "col"), ("ffn_w_down", "row"),
    ("pc_w_in", "col"), ("pc_w_out", "row"), ("mla_w_dq_dkv", "row"), ("mla_w_uq", "col"), ("mla_w_ukv", "col"),
    ("mla_w_o", "row"),
)
SMALL_SHARDED = ("ffn_conv_w", "conv_dw_w", "mla_q_norm_g", "mla_kv_norm_g")
SMALL_REPLICATED = ("norm_mix_g", "norm_xa_g", "norm_mem_g", "norm_ffn_g", "ffn_conv_b", "pool_w", "pool_scale",
                    "conv_dw_b", "conv_ln_g", "conv_ln_b", "final_norm_g")
WEIGHTS = ("norm_mix_g", "norm_xa_g", "norm_mem_g", "xa_wq", "xa_wkv", "xa_wo", "norm_ffn_g", "ffn_w_up", "ffn_conv_w",
           "ffn_conv_b", "ffn_w_down", "pc_w_in", "pool_w", "pool_scale", "conv_dw_w", "conv_dw_b", "conv_ln_g",
           "conv_ln_b", "pc_w_out", "mla_w_dq_dkv", "mla_q_norm_g", "mla_w_uq", "mla_kv_norm_g", "mla_w_ukv", "mla_w_o",
           "final_norm_g")
PACK_ROW = SUBLANE * LANE


def _pack(arrays):
    flat = jnp.concatenate([a.reshape(-1) for a in arrays])
    n = flat.shape[0]
    pad = (-n) % PACK_ROW
    return jnp.pad(flat, (0, pad)).reshape(-1, LANE)


def _unpack(flat, shapes):
    out, off = [], 0
    for s in shapes:
        n = int(np.prod(s))
        out.append(flat[off : off + n].reshape(s))
        off += n
    return out


def kernel(x, mem, positions, norm_mix_g, norm_xa_g, norm_mem_g, xa_wq, xa_wkv, xa_wo, norm_ffn_g, ffn_w_up, ffn_conv_w, ffn_conv_b, ffn_w_down, pc_w_in, pool_w, pool_scale, conv_dw_w, conv_dw_b, conv_ln_g, conv_ln_b, pc_w_out, mla_w_dq_dkv, mla_q_norm_g, mla_w_uq, mla_kv_norm_g, mla_w_ukv, mla_w_o, final_norm_g, loss_target, m_norm_mix_g, m_norm_xa_g, m_norm_mem_g, m_xa_wq, m_xa_wkv, m_xa_wo, m_norm_ffn_g, m_ffn_w_up, m_ffn_conv_w, m_ffn_conv_b, m_ffn_w_down, m_pc_w_in, m_pool_w, m_pool_scale, m_conv_dw_w, m_conv_dw_b, m_conv_ln_g, m_conv_ln_b, m_pc_w_out, m_mla_w_dq_dkv, m_mla_q_norm_g, m_mla_w_uq, m_mla_kv_norm_g, m_mla_w_ukv, m_mla_w_o, m_final_norm_g, v_norm_mix_g, v_norm_xa_g, v_norm_mem_g, v_xa_wq, v_xa_wkv, v_xa_wo, v_norm_ffn_g, v_ffn_w_up, v_ffn_conv_w, v_ffn_conv_b, v_ffn_w_down, v_pc_w_in, v_pool_w, v_pool_scale, v_conv_dw_w, v_conv_dw_b, v_conv_ln_g, v_conv_ln_b, v_pc_w_out, v_mla_w_dq_dkv, v_mla_q_norm_g, v_mla_w_uq, v_mla_kv_norm_g, v_mla_w_ukv, v_mla_w_o, v_final_norm_g):
    args = dict(locals())
    w = {n: args[n] for n in WEIGHTS}
    m = {n: args["m_" + n] for n in WEIGHTS}
    v = {n: args["v_" + n] for n in WEIGHTS}
    cx, cy, cc = lax.axis_index("x"), lax.axis_index("y"), lax.axis_index("c")
    chip = 2 * cx + cy

    full = _all_gather_weights([w[n].astype(MXU_DT) for n, _ in BIG], [k for _, k in BIG], "gather_weights")
    full = dict(zip([n for n, _ in BIG], full))
    small_shapes = [w[n].shape for n in SMALL_SHARDED]
    gathered = _all_gather_rows(_pack([w[n] for n in SMALL_SHARDED]), "gather_small")
    gathered = gathered.reshape(8, -1)
    ws = {n: w[n] for n in SMALL_REPLICATED}
    pieces = [_unpack(gathered[2 * k], small_shapes) for k in range(4)]
    for i, n in enumerate(SMALL_SHARDED):
        ws[n] = jnp.concatenate([pieces[k][i] for k in range(4)], axis=-1)
    wb = {n: full[n] for n in ("xa_wq", "xa_wkv", "xa_wo", "ffn_w_up", "ffn_w_down", "pc_w_in", "pc_w_out", "mla_w_ukv")}
    wb["mla_wdq"] = jnp.stack([_pad_wdq(full["mla_w_dq_dkv"][o]) for o in range(DEPTH // 2)])
    wb["mla_wuq"] = jnp.stack([_pad_wuq(full["mla_w_uq"][o]) for o in range(DEPTH // 2)])
    wb["mla_wo"] = jnp.stack([_pad_wo(full["mla_w_o"][o]) for o in range(DEPTH // 2)])

    loss, grad_x, grads = _local_step(x[0], mem[0], positions[0], loss_target[0], wb, ws)
    loss = lax.psum(loss[0, 0], ("x", "y", "c"))

    kinds = [k for _, k in BIG]
    big = [grads[n] for n, _ in BIG]
    c_idx = cc.reshape(1).astype(jnp.int32)
    chip_idx = chip.reshape(1).astype(jnp.int32)
    theirs = _exchange_halves(big, "reduce_pair")
    pair = [_add_half(gr, th, c_idx, f"reduce_pair_add_{n}") for gr, th, (n, _) in zip(big, theirs, BIG)]
    slots = _scatter_to_chips(pair, kinds, "reduce_chips")
    halves = [_sum_chips(sl, pr, chip_idx, kind, f"reduce_chips_add_{n}")
              for sl, pr, (n, kind) in zip(slots, pair, BIG)]
    others = _swap_halves(halves, "reduce_join")
    gsum, delta, new_m, new_v = {}, {}, {}, {}
    for mine, other, (n, _) in zip(halves, others, BIG):
        gsum[n], delta[n], new_m[n], new_v[n] = _adamw_halves(w[n], mine, other, m[n], v[n], c_idx, f"adamw_{n}")

    small_names = SMALL_REPLICATED + SMALL_SHARDED
    small_grad_shapes = [grads[n].shape for n in small_names]
    packed = _pack([grads[n] for n in small_names])
    rows = packed.shape[0]
    allparts = _all_gather_rows(packed, "gather_small_grads").reshape(8, rows, LANE)
    total = _sum_leading(allparts, "sum_small_grads").reshape(-1)
    for n, gfull in zip(small_names, _unpack(total, small_grad_shapes)):
        if n in SMALL_SHARDED:
            width = w[n].shape[-1]
            gfull = lax.dynamic_slice_in_dim(gfull, chip * width, width, axis=gfull.ndim - 1)
        gsum[n] = gfull

    for n in SMALL_REPLICATED + SMALL_SHARDED:
        delta[n], new_m[n], new_v[n] = _adamw(w[n], gsum[n], m[n], v[n], f"adamw_{n}")
    return (loss, grad_x[None], *[gsum[n] for n in WEIGHTS], *[delta[n] for n in WEIGHTS],
            *[new_m[n] for n in WEIGHTS], *[new_v[n] for n in WEIGHTS])
```
